```python
import math
import jax, jax.numpy as jnp
from jax import lax
import numpy as np

D_MODEL = 1024
BATCH = 4
SEQ = 4096
DEPTH = 2

ATT_HEADS = 8
ATT_KV_HEADS = 2
ATT_HEAD_DIM = 64
ATT_WIDTH = ATT_HEADS * ATT_HEAD_DIM
WINDOW = 128
ATT_BLOCK = 128
N_BUCKETS = 32
MAX_DISTANCE = 128
ML_HEADS = 4
ML_DQK = 64
ML_DV = 128
ML_WIDTH = ML_HEADS * ML_DV
ML_CHUNK = 64
CONV_K = 4
D_MIX = ATT_WIDTH + ML_WIDTH
IN_SPLITS = [ATT_WIDTH, ATT_KV_HEADS * ATT_HEAD_DIM, ATT_KV_HEADS * ATT_HEAD_DIM,
             ML_HEADS * ML_DQK, ML_HEADS * ML_DQK, ML_WIDTH, ML_WIDTH, ML_HEADS, ML_HEADS]
N_IN = sum(IN_SPLITS)
SPLIT_POINTS = [int(v) for v in np.cumsum(IN_SPLITS)[:-1]]
N_GROUPS = 4
EXPERTS_PER_GROUP = 4
N_EXPERTS = N_GROUPS * EXPERTS_PER_GROUP
TOP_K = 2
D_FF_EXPERT = 512
EPS = 1e-6

kernel_name = "hymba_swa_mlstm_hier_moe"


def rms_norm(x, g):
    xf = x.astype(jnp.float32)
    y = xf * lax.rsqrt(jnp.mean(xf * xf, axis=-1, keepdims=True) + EPS)
    return (y * g.astype(jnp.float32)).astype(x.dtype)


def t5_bucket(dist):
    max_exact = N_BUCKETS // 2
    d = jnp.maximum(dist, 1).astype(jnp.float32)
    large = max_exact + (jnp.log(d / max_exact) / math.log(MAX_DISTANCE / max_exact)
                         * (N_BUCKETS - max_exact)).astype(jnp.int32)
    large = jnp.minimum(large, N_BUCKETS - 1)
    return jnp.where(dist < max_exact, dist, large)


def swa_sink_attention(q, k, v, q_g, k_g, sink, rel_bias):
    B, S, _ = q.shape
    nb = S // ATT_BLOCK
    G = ATT_HEADS // ATT_KV_HEADS
    q = rms_norm(q.reshape(B, S, ATT_HEADS, ATT_HEAD_DIM), q_g)
    k = rms_norm(k.reshape(B, S, ATT_KV_HEADS, ATT_HEAD_DIM), k_g)
    v = v.reshape(B, S, ATT_KV_HEADS, ATT_HEAD_DIM)
    qb = q.reshape(B, nb, ATT_BLOCK, ATT_KV_HEADS, G, ATT_HEAD_DIM)
    pad = ((0, 0), (ATT_BLOCK, 0), (0, 0), (0, 0))
    kp = jnp.pad(k, pad).reshape(B, nb + 1, ATT_BLOCK, ATT_KV_HEADS, ATT_HEAD_DIM)
    vp = jnp.pad(v, pad).reshape(B, nb + 1, ATT_BLOCK, ATT_KV_HEADS, ATT_HEAD_DIM)
    kb = jnp.concatenate([kp[:, :-1], kp[:, 1:]], axis=2)
    vb = jnp.concatenate([vp[:, :-1], vp[:, 1:]], axis=2)
    scores = jnp.einsum('bnqkgd,bnskd->bnkgqs', qb, kb).astype(jnp.float32) * (ATT_HEAD_DIM ** -0.5)
    qi = jnp.arange(ATT_BLOCK)[:, None]
    kj = jnp.arange(2 * ATT_BLOCK)[None, :]
    dist = qi + ATT_BLOCK - kj
    in_window = (dist >= 0) & (dist < WINDOW)
    key_pos = jnp.arange(nb)[:, None, None] * ATT_BLOCK + kj[None] - ATT_BLOCK
    valid = in_window[None] & (key_pos >= 0)
    bucket = t5_bucket(jnp.clip(dist, 0, WINDOW - 1))
    bias = rel_bias.astype(jnp.float32)[bucket]
    bias = bias.transpose(2, 0, 1).reshape(ATT_KV_HEADS, G, ATT_BLOCK, 2 * ATT_BLOCK)
    logits = jnp.where(valid[None, :, None, None], scores + bias, -jnp.inf)
    sink_l = sink.astype(jnp.float32).reshape(ATT_KV_HEADS, G)[:, :, None, None]
    m = jnp.maximum(jnp.max(logits, axis=-1, keepdims=True), sink_l)
    p = jnp.exp(logits - m)
    probs = p / (jnp.sum(p, axis=-1, keepdims=True) + jnp.exp(sink_l - m))
    out = jnp.einsum('bnkgqs,bnskd->bnqkgd', probs.astype(vb.dtype), vb)
    return out.reshape(B, S, ATT_WIDTH)


def causal_conv_silu(x, w, b):
    K = w.shape[0]
    S = x.shape[1]
    xp = jnp.pad(x, ((0, 0), (K - 1, 0), (0, 0)))
    y = xp[:, 0:S] * w[0]
    for j in range(1, K):
        y = y + xp[:, j:j + S] * w[j]
    return jax.nn.silu(y + b)


def mlstm_chunkwise(q, k, v, i_pre, f_pre):
    B, S = q.shape[0], q.shape[1]
    nc = S // ML_CHUNK

    def to_chunks(a):
        a = a.reshape(B, nc, ML_CHUNK, ML_HEADS, *a.shape[3:])
        return jnp.moveaxis(a, (1, 3), (0, 2))

    qc = to_chunks(q.astype(jnp.float32))
    kc = to_chunks(k.astype(jnp.float32) * (ML_DQK ** -0.5))
    vc = to_chunks(v.astype(jnp.float32))
    lic = to_chunks(i_pre.astype(jnp.float32))
    lfc = to_chunks(jax.nn.log_sigmoid(f_pre.astype(jnp.float32)))
    causal = jnp.tril(jnp.ones((ML_CHUNK, ML_CHUNK), dtype=bool))

    def step(carry, inp):
        C, n, m = carry
        qt, kt, vt, li, lf = inp
        b = jnp.cumsum(lf, axis=-1)
        log_d = jnp.where(causal, b[..., :, None] - b[..., None, :] + li[..., None, :], -jnp.inf)
        m_inter = b + m[..., None]
        m_row = jnp.maximum(m_inter, jnp.max(log_d, axis=-1))
        d = jnp.exp(log_d - m_row[..., None])
        inter_scale = jnp.exp(m_inter - m_row)
        s = jnp.einsum('bhtd,bhsd->bhts', qt, kt) * d
        num = inter_scale[..., None] * jnp.einsum('bhvd,bhtd->bhtv', C, qt) + jnp.einsum('bhts,bhsv->bhtv', s, vt)
        den = inter_scale * jnp.einsum('bhd,bhtd->bht', n, qt) + jnp.sum(s, axis=-1)
        h = num / jnp.maximum(jnp.abs(den), jnp.exp(-m_row))[..., None]
        b_last = b[..., -1]
        log_w = b_last[..., None] - b + li
        m_next = jnp.maximum(b_last + m, jnp.max(log_w, axis=-1))
        w = jnp.exp(log_w - m_next[..., None])
        decay = jnp.exp(b_last + m - m_next)
        C_next = decay[..., None, None] * C + jnp.einsum('bhs,bhsv,bhsd->bhvd', w, vt, kt)
        n_next = decay[..., None] * n + jnp.einsum('bhs,bhsd->bhd', w, kt)
        return (C_next, n_next, m_next), h

    init = (jnp.zeros((B, ML_HEADS, ML_DV, ML_DQK), jnp.float32),
            jnp.zeros((B, ML_HEADS, ML_DQK), jnp.float32),
            jnp.zeros((B, ML_HEADS), jnp.float32))
    _, h = lax.scan(step, init, (qc, kc, vc, lic, lfc))
    h = jnp.moveaxis(h, (0, 2), (1, 3))
    return h.reshape(B, S, ML_HEADS, ML_DV)


def hier_moe(x, w_rg, b_rg, w_re, b_re, w_gate, w_up, w_down):
    B, S, D = x.shape
    t = x.reshape(-1, D)
    g_logits = (t @ w_rg).astype(jnp.float32) + b_rg.astype(jnp.float32)
    g_prob = jax.nn.softmax(g_logits, axis=-1)
    grp = jnp.argmax(g_logits, axis=-1)
    p_grp = jnp.take_along_axis(g_prob, grp[:, None], axis=-1)
    e_logits = ((t @ w_re).astype(jnp.float32) + b_re.astype(jnp.float32)).reshape(-1, N_GROUPS, EXPERTS_PER_GROUP)
    e_in = jnp.take_along_axis(e_logits, grp[:, None, None], axis=1)[:, 0]
    e_prob = jax.nn.softmax(e_in, axis=-1)
    top_p, top_i = lax.top_k(e_prob, TOP_K)
    top_p = top_p / jnp.sum(top_p, axis=-1, keepdims=True)
    weights = p_grp * top_p
    expert_ids = grp[:, None] * EXPERTS_PER_GROUP + top_i
    combine = jnp.sum(jax.nn.one_hot(expert_ids, N_EXPERTS, dtype=jnp.float32) * weights[..., None], axis=1)
    y = jnp.zeros(t.shape, jnp.float32)
    for e in range(N_EXPERTS):
        h = jax.nn.silu(t @ w_gate[e]) * (t @ w_up[e])
        y = y + combine[:, e:e + 1] * (h @ w_down[e]).astype(jnp.float32)
    return y.astype(x.dtype).reshape(B, S, D)


def setup_inputs(seed: int = 0) -> dict:
    key = jax.random.key(seed)
    ks = jax.random.split(key, 20)
    f32 = jnp.float32
    nrm = lambda k, shape, s: jax.random.normal(k, shape, f32) * s
    gate_b = jnp.concatenate([
        nrm(ks[9], (DEPTH, ML_HEADS), 0.1),
        jnp.linspace(3.0, 6.0, ML_HEADS, dtype=f32)[None, :] + nrm(ks[10], (DEPTH, ML_HEADS), 0.1)], axis=-1)
    return {
        "x": nrm(ks[0], (BATCH, SEQ, D_MODEL), 1.0),
        "rel_bias": nrm(ks[1], (N_BUCKETS, ATT_HEADS), 0.5),
        "norm_mix_g": 1.0 + nrm(ks[2], (DEPTH, D_MODEL), 0.02),
        "w_in": nrm(ks[3], (DEPTH, D_MODEL, N_IN), D_MODEL ** -0.5),
        "q_norm_g": 1.0 + nrm(ks[4], (DEPTH, ATT_HEAD_DIM), 0.02),
        "k_norm_g": 1.0 + nrm(ks[5], (DEPTH, ATT_HEAD_DIM), 0.02),
        "attn_sink": nrm(ks[6], (DEPTH, ATT_HEADS), 0.5),
        "conv_w": nrm(ks[7], (DEPTH, CONV_K, 2 * ML_HEADS * ML_DQK), CONV_K ** -0.5),
        "conv_b": nrm(ks[8], (DEPTH, 2 * ML_HEADS * ML_DQK), 0.02),
        "gate_b": gate_b,
        "mlstm_norm_g": 1.0 + nrm(ks[11], (DEPTH, ML_WIDTH), 0.02),
        "w_out": nrm(ks[12], (DEPTH, D_MIX, D_MODEL), D_MIX ** -0.5),
        "norm_ffn_g": 1.0 + nrm(ks[13], (DEPTH, D_MODEL), 0.02),
        "w_router_group": nrm(ks[14], (DEPTH, D_MODEL, N_GROUPS), D_MODEL ** -0.5),
        "b_router_group": nrm(ks[15], (DEPTH, N_GROUPS), 0.01),
        "w_router_expert": nrm(ks[16], (DEPTH, D_MODEL, N_EXPERTS), D_MODEL ** -0.5),
        "b_router_expert": nrm(ks[17], (DEPTH, N_EXPERTS), 0.01),
        "w_gate": nrm(ks[18], (DEPTH, N_EXPERTS, D_MODEL, D_FF_EXPERT), D_MODEL ** -0.5),
        "w_up": nrm(jax.random.fold_in(ks[18], 1), (DEPTH, N_EXPERTS, D_MODEL, D_FF_EXPERT), D_MODEL ** -0.5),
        "w_down": nrm(ks[19], (DEPTH, N_EXPERTS, D_FF_EXPERT, D_MODEL), D_FF_EXPERT ** -0.5),
    }


def reference(x, rel_bias, norm_mix_g, w_in, q_norm_g, k_norm_g, attn_sink, conv_w, conv_b, gate_b,
              mlstm_norm_g, w_out, norm_ffn_g, w_router_group, b_router_group, w_router_expert,
              b_router_expert, w_gate, w_up, w_down):
    B, S, _ = x.shape
    for l in range(DEPTH):
        hn = rms_norm(x, norm_mix_g[l])
        proj = hn @ w_in[l]
        q_a, k_a, v_a, q_m, k_m, v_m, o_m, i_m, f_m = jnp.split(proj, SPLIT_POINTS, axis=-1)
        att = swa_sink_attention(q_a, k_a, v_a, q_norm_g[l], k_norm_g[l], attn_sink[l], rel_bias)
        qk_m = causal_conv_silu(jnp.concatenate([q_m, k_m], axis=-1), conv_w[l], conv_b[l])
        q_m, k_m = jnp.split(qk_m, 2, axis=-1)
        i_pre = i_m + gate_b[l, :ML_HEADS]
        f_pre = f_m + gate_b[l, ML_HEADS:]
        h_m = mlstm_chunkwise(q_m.reshape(B, S, ML_HEADS, ML_DQK), k_m.reshape(B, S, ML_HEADS, ML_DQK),
                              v_m.reshape(B, S, ML_HEADS, ML_DV), i_pre, f_pre).astype(x.dtype)
        h_m = rms_norm(h_m, mlstm_norm_g[l].reshape(ML_HEADS, ML_DV)).reshape(B, S, ML_WIDTH) * jax.nn.sigmoid(o_m)
        x = x + jnp.concatenate([att, h_m], axis=-1) @ w_out[l]
        x = x + hier_moe(rms_norm(x, norm_ffn_g[l]), w_router_group[l], b_router_group[l],
                         w_router_expert[l], b_router_expert[l], w_gate[l], w_up[l], w_down[l])
    return x
```

```python
import functools
import math

import jax
import jax.numpy as jnp
import numpy as np
from jax import lax
from jax.experimental import pallas as pl
from jax.experimental.pallas import tpu as pltpu

D_MODEL = 1024
DEPTH = 2
ATT_HEADS = 8
ATT_KV_HEADS = 2
ATT_HEAD_DIM = 64
ATT_WIDTH = ATT_HEADS * ATT_HEAD_DIM
WINDOW = 128
ATT_BLOCK = 128
N_BUCKETS = 32
MAX_DISTANCE = 128
ML_HEADS = 4
ML_DQK = 64
ML_DV = 128
ML_WIDTH = ML_HEADS * ML_DV
ML_CHUNK = 64
CONV_K = 4
N_GROUPS = 4
EXPERTS_PER_GROUP = 4
N_EXPERTS = N_GROUPS * EXPERTS_PER_GROUP
D_FF_EXPERT = 512
EPS = 1e-6

LANES = 128
SUBLANES = 8
NEG_BIG = -1e30
VMEM_LIMIT = 48 * 1024 * 1024

C_Q = 0
C_K = C_Q + ATT_WIDTH
C_V = C_K + 2 * LANES
C_QKM = C_V + 2 * LANES
C_VM = C_QKM + 2 * ML_HEADS * ML_DQK
C_OM = C_VM + ML_WIDTH
C_G = C_OM + ML_WIDTH
N_PACK = C_G + LANES

TM_PROJ = 512
TM_MOE = 1024
ML_STEP = 2 * ML_CHUNK


def _bf16(a):
    return a.astype(jnp.bfloat16)


def _log_sigmoid(z):
    return jnp.minimum(z, 0.0) - jnp.log(1.0 + jnp.exp(-jnp.abs(z)))


def _in_proj_kernel(tiles_per_seq, x_ref, g_ref, w_ref, wgt_ref, qg_ref, kg_ref, bdq_ref, bdk_ref,
                    cw_ref, cb_ref, gb_ref, gbt_ref,
                    qn_ref, kn_ref, vd_ref, qkm_ref, vm_ref, om_ref, gate_ref, gatet_ref,
                    conv_scr):
    i = pl.program_id(0)
    tm = x_ref.shape[0]
    x = x_ref[...]
    hn = x * lax.rsqrt(jnp.mean(x * x, axis=-1, keepdims=True) + EPS) * g_ref[...]
    hb = _bf16(hn)

    def proj(c0, width):
        return jnp.dot(hb, w_ref[:, c0:c0 + width], preferred_element_type=jnp.float32)

    q = proj(C_Q, ATT_WIDTH)
    q_ms = jnp.dot(_bf16(q * q), bdq_ref[...], preferred_element_type=jnp.float32)
    qn_ref[...] = _bf16(q * lax.rsqrt(q_ms + EPS) * qg_ref[...])
    k = proj(C_K, 2 * LANES)
    k_ms = jnp.dot(_bf16(k * k), bdk_ref[...], preferred_element_type=jnp.float32)
    kn_ref[...] = _bf16(k * lax.rsqrt(k_ms + EPS) * kg_ref[...])
    vd_ref[...] = _bf16(proj(C_V, 2 * LANES))

    qk = proj(C_QKM, 2 * ML_HEADS * ML_DQK)

    @pl.when(i % tiles_per_seq == 0)
    def _():
        conv_scr[0:SUBLANES, :] = jnp.zeros((SUBLANES, qk.shape[1]), jnp.float32)

    @pl.when(i % tiles_per_seq != 0)
    def _():
        conv_scr[0:SUBLANES, :] = conv_scr[tm:tm + SUBLANES, :]

    conv_scr[SUBLANES:SUBLANES + tm, :] = qk
    y = qk * cw_ref[CONV_K - 1:CONV_K, :] + cb_ref[...]
    for j in range(CONV_K - 1):
        off = SUBLANES - (CONV_K - 1) + j
        y = y + conv_scr[off:off + tm, :] * cw_ref[j:j + 1, :]
    y = y * jax.nn.sigmoid(y)
    lane = lax.broadcasted_iota(jnp.int32, y.shape, 1)
    y = jnp.where(lane >= ML_HEADS * ML_DQK, y * (ML_DQK ** -0.5), y)
    qkm_ref[...] = _bf16(y)

    vm_ref[...] = _bf16(proj(C_VM, ML_WIDTH))
    om_ref[...] = _bf16(jax.nn.sigmoid(proj(C_OM, ML_WIDTH)))

    gp = proj(C_G, LANES) + gb_ref[...]
    glane = lax.broadcasted_iota(jnp.int32, gp.shape, 1)
    gate_ref[...] = jnp.where(glane >= ML_HEADS, _log_sigmoid(gp), gp)
    gt = lax.dot_general(wgt_ref[...], hb, (((1,), (1,)), ((), ())),
                         preferred_element_type=jnp.float32) + gbt_ref[...]
    grow = lax.broadcasted_iota(jnp.int32, gt.shape, 0)
    gatet_ref[...] = jnp.where(grow >= ML_HEADS, _log_sigmoid(gt), gt)


def _in_proj(x2, g, w_pack, wgt, qg, kg, bdq, bdk, cw, cb, gb, gbt, batch, seq_len):
    t = x2.shape[0]
    tm = TM_PROJ
    n = t // tm
    tps = seq_len // tm
    row = lambda i: (i, 0)
    fix = lambda i: (0, 0)
    out_shapes = (
        jax.ShapeDtypeStruct((t, ATT_WIDTH), jnp.bfloat16),
        jax.ShapeDtypeStruct((t, 2 * LANES), jnp.bfloat16),
        jax.ShapeDtypeStruct((t, 2 * LANES), jnp.bfloat16),
        jax.ShapeDtypeStruct((t, 2 * ML_HEADS * ML_DQK), jnp.bfloat16),
        jax.ShapeDtypeStruct((t, ML_WIDTH), jnp.bfloat16),
        jax.ShapeDtypeStruct((t, ML_WIDTH), jnp.bfloat16),
        jax.ShapeDtypeStruct((t, LANES), jnp.float32),
        jax.ShapeDtypeStruct((batch, SUBLANES, seq_len), jnp.float32),
    )
    in_specs = [
        pl.BlockSpec((tm, D_MODEL), row),
        pl.BlockSpec((1, D_MODEL), fix),
        pl.BlockSpec((D_MODEL, N_PACK), fix),
        pl.BlockSpec((SUBLANES, D_MODEL), fix),
        pl.BlockSpec((1, ATT_WIDTH), fix),
        pl.BlockSpec((1, 2 * LANES), fix),
        pl.BlockSpec((ATT_WIDTH, ATT_WIDTH), fix),
        pl.BlockSpec((2 * LANES, 2 * LANES), fix),
        pl.BlockSpec((CONV_K, 2 * ML_HEADS * ML_DQK), fix),
        pl.BlockSpec((1, 2 * ML_HEADS * ML_DQK), fix),
        pl.BlockSpec((1, LANES), fix),
        pl.BlockSpec((SUBLANES, 1), fix),
    ]
    out_specs = [
        pl.BlockSpec((tm, ATT_WIDTH), row),
        pl.BlockSpec((tm, 2 * LANES), row),
        pl.BlockSpec((tm, 2 * LANES), row),
        pl.BlockSpec((tm, 2 * ML_HEADS * ML_DQK), row),
        pl.BlockSpec((tm, ML_WIDTH), row),
        pl.BlockSpec((tm, ML_WIDTH), row),
        pl.BlockSpec((tm, LANES), row),
        pl.BlockSpec((None, SUBLANES, tm), lambda i: (i // tps, 0, i % tps)),
    ]
    return pl.pallas_call(
        functools.partial(_in_proj_kernel, seq_len // tm),
        grid=(n,),
        in_specs=in_specs,
        out_specs=out_specs,
        out_shape=out_shapes,
        scratch_shapes=[pltpu.VMEM((tm + 2 * SUBLANES, 2 * ML_HEADS * ML_DQK), jnp.float32)],
        compiler_params=pltpu.CompilerParams(dimension_semantics=("arbitrary",),
                                             vmem_limit_bytes=VMEM_LIMIT),
        name="in_proj",
    )(x2, g, w_pack, wgt, qg, kg, bdq, bdk, cw, cb, gb, gbt)


def _attn_kernel(sink_ref, q_ref, kp_ref, kc_ref, vp_ref, vc_ref, bias_ref, o_ref):
    i = pl.program_id(1)
    blk = q_ref.shape[0]
    lane = lax.broadcasted_iota(jnp.int32, (blk, LANES), 1)
    low = lane < ATT_HEAD_DIM
    col = lax.broadcasted_iota(jnp.int32, (blk, 2 * blk), 1)
    no_prev = jnp.logical_and(col < blk, i == 0)
    group = ATT_HEADS // ATT_KV_HEADS
    for pair in range(ATT_HEADS // 2):
        kv = (2 * pair) // group
        qp = q_ref[:, pair * LANES:(pair + 1) * LANES]
        kcat = jnp.concatenate([kp_ref[:, kv * LANES:(kv + 1) * LANES],
                                kc_ref[:, kv * LANES:(kv + 1) * LANES]], axis=0)
        vcat = jnp.concatenate([vp_ref[:, kv * LANES:(kv + 1) * LANES],
                                vc_ref[:, kv * LANES:(kv + 1) * LANES]], axis=0)
        halves = []
        for sub in range(2):
            h = 2 * pair + sub
            qm = jnp.where(low if sub == 0 else jnp.logical_not(low), qp, jnp.zeros_like(qp))
            s = lax.dot_general(qm, kcat, (((1,), (1,)), ((), ())),
                                preferred_element_type=jnp.float32)
            logits = jnp.where(no_prev, NEG_BIG, s + bias_ref[h])
            sink = sink_ref[h]
            m = jnp.maximum(jnp.max(logits, axis=-1, keepdims=True), sink)
            p = jnp.exp(logits - m)
            den = jnp.sum(p, axis=-1, keepdims=True) + jnp.exp(sink - m)
            o = jnp.dot(_bf16(p), vcat, preferred_element_type=jnp.float32)
            halves.append(o / den)
        o_ref[:, pair * LANES:(pair + 1) * LANES] = _bf16(jnp.where(low, halves[0], halves[1]))


def _attention(qn, kn, vd, bias, sink, batch, seq_len):
    nb = seq_len // ATT_BLOCK
    cur = lambda b, i, s: (b * nb + i, 0)
    prev = lambda b, i, s: (b * nb + jnp.maximum(i - 1, 0), 0)
    grid_spec = pltpu.PrefetchScalarGridSpec(
        num_scalar_prefetch=1,
        grid=(batch, nb),
        in_specs=[
            pl.BlockSpec((ATT_BLOCK, ATT_WIDTH), cur),
            pl.BlockSpec((ATT_BLOCK, 2 * LANES), prev),
            pl.BlockSpec((ATT_BLOCK, 2 * LANES), cur),
            pl.BlockSpec((ATT_BLOCK, 2 * LANES), prev),
            pl.BlockSpec((ATT_BLOCK, 2 * LANES), cur),
            pl.BlockSpec((ATT_HEADS, ATT_BLOCK, 2 * ATT_BLOCK), lambda b, i, s: (0, 0, 0)),
        ],
        out_specs=pl.BlockSpec((ATT_BLOCK, ATT_WIDTH), cur),
    )
    return pl.pallas_call(
        _attn_kernel,
        grid_spec=grid_spec,
        out_shape=jax.ShapeDtypeStruct((batch * seq_len, ATT_WIDTH), jnp.bfloat16),
        compiler_params=pltpu.CompilerParams(dimension_semantics=("arbitrary", "arbitrary"),
                                             vmem_limit_bytes=VMEM_LIMIT),
        name="swa_attention",
    )(sink, qn, kn, kn, vd, vd, bias)


def _mlstm_kernel(qk_ref, v_ref, o_ref, g_ref, gt_ref, ng_ref, out_ref, c_scr, m_scr):
    step = pl.program_id(0)
    batch = qk_ref.shape[0]
    L = ML_CHUNK
    pairs = ML_HEADS // 2
    hi = lax.Precision.HIGHEST

    @pl.when(step == 0)
    def _():
        c_scr[...] = jnp.zeros(c_scr.shape, jnp.float32)
        m_scr[...] = jnp.zeros(m_scr.shape, jnp.float32)

    r_i = lax.broadcasted_iota(jnp.int32, (L, L), 0)
    c_i = lax.broadcasted_iota(jnp.int32, (L, L), 1)
    causal = c_i <= r_i
    tril = causal.astype(jnp.float32)
    triu = (r_i <= c_i).astype(jnp.float32)
    lane = lax.broadcasted_iota(jnp.int32, (L, LANES), 1)
    low = lane < ML_DQK
    one_col = (lane == 0).astype(jnp.bfloat16)
    row2 = lax.broadcasted_iota(jnp.int32, (2 * ML_DQK, 1), 0)

    for ch in range(ML_STEP // L):
        r0 = ch * L
        for b in range(batch):
            g = g_ref[b, r0:r0 + L, :]
            gt = gt_ref[b, :, r0:r0 + L]
            bcols = jnp.dot(tril, g, precision=hi, preferred_element_type=jnp.float32)
            brows = jnp.dot(gt, triu, precision=hi, preferred_element_type=jnp.float32)
            for pair in range(pairs):
                sidx = b * pairs + pair
                qp = qk_ref[b, r0:r0 + L, pair * LANES:(pair + 1) * LANES]
                kp = qk_ref[b, r0:r0 + L, (pairs + pair) * LANES:(pairs + pair + 1) * LANES]
                c_pair = c_scr[sidx]
                c_bf = _bf16(c_pair)
                new_c = None
                decays = []
                for sub in range(2):
                    h = 2 * pair + sub
                    sel = low if sub == 0 else jnp.logical_not(low)
                    m_prev = m_scr[b * ML_HEADS + h]
                    bc = bcols[:, ML_HEADS + h:ML_HEADS + h + 1]
                    br = brows[ML_HEADS + h:ML_HEADS + h + 1, :]
                    lic = g[:, h:h + 1]
                    lir = gt[h:h + 1, :]
                    log_d = jnp.where(causal, bc - br + lir, NEG_BIG)
                    m_inter = bc + m_prev
                    m_row = jnp.maximum(m_inter, jnp.max(log_d, axis=-1, keepdims=True))
                    d = jnp.exp(log_d - m_row)
                    inter = jnp.exp(m_inter - m_row)
                    qm = jnp.where(sel, qp, jnp.zeros_like(qp))
                    s = lax.dot_general(qm, kp, (((1,), (1,)), ((), ())),
                                        preferred_element_type=jnp.float32) * d
                    v_ext = jnp.concatenate([v_ref[b, r0:r0 + L, h * ML_DV:(h + 1) * ML_DV], one_col],
                                            axis=-1)
                    num = inter * jnp.dot(qm, c_bf, preferred_element_type=jnp.float32) \
                        + jnp.dot(_bf16(s), v_ext, preferred_element_type=jnp.float32)
                    den = num[:, ML_DV:ML_DV + 1]
                    hval = num[:, 0:ML_DV] / jnp.maximum(jnp.abs(den), jnp.exp(-m_row))
                    hn = hval * lax.rsqrt(jnp.mean(hval * hval, axis=-1, keepdims=True) + EPS)
                    hn = hn * ng_ref[:, h * ML_DV:(h + 1) * ML_DV]
                    out_ref[b, r0:r0 + L, h * ML_DV:(h + 1) * ML_DV] = _bf16(
                        hn * o_ref[b, r0:r0 + L, h * ML_DV:(h + 1) * ML_DV].astype(jnp.float32))
                    b_last = bc[L - 1:L, :]
                    log_w = b_last - bc + lic
                    m_next = jnp.maximum(b_last + m_prev, jnp.max(log_w, axis=0, keepdims=True))
                    w = jnp.exp(log_w - m_next)
                    decays.append(jnp.exp(b_last + m_prev - m_next))
                    m_scr[b * ML_HEADS + h] = m_next
                    kw = _bf16(jnp.where(sel, kp.astype(jnp.float32) * w, 0.0))
                    upd = lax.dot_general(kw, v_ext, (((0,), (0,)), ((), ())),
                                          preferred_element_type=jnp.float32)
                    new_c = upd if new_c is None else new_c + upd
                decay_rows = jnp.where(row2 < ML_DQK, decays[0], decays[1])
                c_scr[sidx] = decay_rows * c_pair + new_c


def _mlstm(qkm, vm, om, gates, gates_t, ng, batch, seq_len):
    n_steps = seq_len // ML_STEP
    blk = lambda c: (0, c, 0)
    return pl.pallas_call(
        _mlstm_kernel,
        grid=(n_steps,),
        in_specs=[
            pl.BlockSpec((batch, ML_STEP, 2 * ML_HEADS * ML_DQK), blk),
            pl.BlockSpec((batch, ML_STEP, ML_WIDTH), blk),
            pl.BlockSpec((batch, ML_STEP, ML_WIDTH), blk),
            pl.BlockSpec((batch, ML_STEP, LANES), blk),
            pl.BlockSpec((batch, SUBLANES, ML_STEP), lambda c: (0, 0, c)),
            pl.BlockSpec((1, ML_WIDTH), lambda c: (0, 0)),
        ],
        out_specs=pl.BlockSpec((batch, ML_STEP, ML_WIDTH), blk),
        out_shape=jax.ShapeDtypeStruct((batch, seq_len, ML_WIDTH), jnp.bfloat16),
        scratch_shapes=[
            pltpu.VMEM((batch * ML_HEADS // 2, 2 * ML_DQK, 2 * ML_DV), jnp.float32),
            pltpu.VMEM((batch * ML_HEADS, 1, 1), jnp.float32),
        ],
        compiler_params=pltpu.CompilerParams(dimension_semantics=("arbitrary",),
                                             vmem_limit_bytes=VMEM_LIMIT),
        name="mlstm_scan",
    )(qkm, vm, om, gates, gates_t, ng)


def _out_proj_router_kernel(x_ref, att_ref, hm_ref, wo_ref, g_ref, wr_ref, br_ref,
                            x1_ref, t_ref, comb_ref):
    x1 = x_ref[...] \
        + jnp.dot(att_ref[...], wo_ref[0:ATT_WIDTH, :], preferred_element_type=jnp.float32) \
        + jnp.dot(hm_ref[...], wo_ref[ATT_WIDTH:, :], preferred_element_type=jnp.float32)
    x1_ref[...] = x1
    tn = x1 * lax.rsqrt(jnp.mean(x1 * x1, axis=-1, keepdims=True) + EPS) * g_ref[...]
    t_ref[...] = _bf16(tn)
    logits = jnp.dot(tn, wr_ref[...], precision=lax.Precision.HIGHEST,
                     preferred_element_type=jnp.float32) + br_ref[...]
    lane = lax.broadcasted_iota(jnp.int32, logits.shape, 1)
    is_grp = jnp.logical_and(lane >= N_EXPERTS, lane < N_EXPERTS + N_GROUPS)
    gl = jnp.where(is_grp, logits, NEG_BIG)
    gmax = jnp.max(gl, axis=-1, keepdims=True)
    grp = jnp.min(jnp.where(gl == gmax, lane - N_EXPERTS, N_GROUPS), axis=-1, keepdims=True)
    p_grp = 1.0 / jnp.sum(jnp.exp(gl - gmax), axis=-1, keepdims=True)
    in_grp = jnp.logical_and(lane < N_EXPERTS, (lane // EXPERTS_PER_GROUP) == grp)
    el = jnp.where(in_grp, logits, NEG_BIG)
    e1 = jnp.max(el, axis=-1, keepdims=True)
    i1 = jnp.min(jnp.where(el == e1, lane, LANES), axis=-1, keepdims=True)
    el2 = jnp.where(lane == i1, NEG_BIG, el)
    e2 = jnp.max(el2, axis=-1, keepdims=True)
    i2 = jnp.min(jnp.where(el2 == e2, lane, LANES), axis=-1, keepdims=True)
    z2 = jnp.exp(e2 - e1)
    w1 = p_grp / (1.0 + z2)
    w2 = p_grp * z2 / (1.0 + z2)
    comb_ref[...] = jnp.where(lane == i1, w1, jnp.where(lane == i2, w2, 0.0))


def _out_proj_router(x2, att, hm, wo, g, wr, br):
    t = x2.shape[0]
    tm = TM_PROJ
    row = lambda i: (i, 0)
    fix = lambda i: (0, 0)
    return pl.pallas_call(
        _out_proj_router_kernel,
        grid=(t // tm,),
        in_specs=[
            pl.BlockSpec((tm, D_MODEL), row),
            pl.BlockSpec((tm, ATT_WIDTH), row),
            pl.BlockSpec((tm, ML_WIDTH), row),
            pl.BlockSpec((D_MODEL, D_MODEL), fix),
            pl.BlockSpec((1, D_MODEL), fix),
            pl.BlockSpec((D_MODEL, LANES), fix),
            pl.BlockSpec((1, LANES), fix),
        ],
        out_specs=[
            pl.BlockSpec((tm, D_MODEL), row),
            pl.BlockSpec((tm, D_MODEL), row),
            pl.BlockSpec((tm, LANES), row),
        ],
        out_shape=(
            jax.ShapeDtypeStruct((t, D_MODEL), jnp.float32),
            jax.ShapeDtypeStruct((t, D_MODEL), jnp.bfloat16),
            jax.ShapeDtypeStruct((t, LANES), jnp.float32),
        ),
        compiler_params=pltpu.CompilerParams(dimension_semantics=("arbitrary",),
                                             vmem_limit_bytes=VMEM_LIMIT),
        name="out_proj_router",
    )(x2, att, hm, wo, g, wr, br)


def _moe_kernel(x1_ref, t_ref, comb_ref, wg_ref, wu_ref, wd_ref, out_ref, acc_ref):
    e = pl.program_id(1)

    @pl.when(e == 0)
    def _():
        acc_ref[...] = x1_ref[...]

    t = t_ref[...]
    lane = lax.broadcasted_iota(jnp.int32, comb_ref.shape, 1)
    c = jnp.sum(jnp.where(lane == e, comb_ref[...], 0.0), axis=-1, keepdims=True)
    a = jnp.dot(t, wg_ref[0], preferred_element_type=jnp.float32)
    u = jnp.dot(t, wu_ref[0], preferred_element_type=jnp.float32)
    h = a * jax.nn.sigmoid(a) * u
    y = jnp.dot(_bf16(h), wd_ref[0], preferred_element_type=jnp.float32)
    acc_ref[...] += c * y

    @pl.when(e == pl.num_programs(1) - 1)
    def _():
        out_ref[...] = acc_ref[...]


def _moe(x1, t, comb, wg, wu, wd):
    n_tok = x1.shape[0]
    tm = TM_MOE
    row = lambda i, e: (i, 0)
    return pl.pallas_call(
        _moe_kernel,
        grid=(n_tok // tm, N_EXPERTS),
        in_specs=[
            pl.BlockSpec((tm, D_MODEL), row),
            pl.BlockSpec((tm, D_MODEL), row),
            pl.BlockSpec((tm, LANES), row),
            pl.BlockSpec((1, D_MODEL, D_FF_EXPERT), lambda i, e: (e, 0, 0)),
            pl.BlockSpec((1, D_MODEL, D_FF_EXPERT), lambda i, e: (e, 0, 0)),
            pl.BlockSpec((1, D_FF_EXPERT, D_MODEL), lambda i, e: (e, 0, 0)),
        ],
        out_specs=pl.BlockSpec((tm, D_MODEL), row),
        out_shape=jax.ShapeDtypeStruct((n_tok, D_MODEL), jnp.float32),
        scratch_shapes=[pltpu.VMEM((tm, D_MODEL), jnp.float32)],
        compiler_params=pltpu.CompilerParams(dimension_semantics=("arbitrary", "arbitrary"),
                                             vmem_limit_bytes=VMEM_LIMIT),
        name="moe_experts",
    )(x1, t, comb, wg, wu, wd)


def _t5_bucket_np(dist):
    max_exact = N_BUCKETS // 2
    d = np.maximum(dist, 1).astype(np.float32)
    large = max_exact + (np.log(d / max_exact) / math.log(MAX_DISTANCE / max_exact)
                         * (N_BUCKETS - max_exact)).astype(np.int32)
    large = np.minimum(large, N_BUCKETS - 1)
    return np.where(dist < max_exact, dist, large)


def _attention_bias(rel_bias):
    qi = np.arange(ATT_BLOCK)[:, None]
    kj = np.arange(2 * ATT_BLOCK)[None, :]
    dist = qi + ATT_BLOCK - kj
    in_window = (dist >= 0) & (dist < WINDOW)
    bucket = _t5_bucket_np(np.clip(dist, 0, WINDOW - 1))
    bias = rel_bias.astype(jnp.float32)[bucket]
    bias = jnp.transpose(bias, (2, 0, 1))
    return jnp.where(jnp.asarray(in_window)[None], bias, NEG_BIG)


def _block_diag_mean(width, block):
    idx = np.arange(width) // block
    return jnp.asarray((idx[:, None] == idx[None, :]).astype(np.float32) / block, dtype=jnp.bfloat16)


def _pack_w_in(w):
    hd = ATT_HEAD_DIM
    o = 0
    q = w[:, o:o + ATT_WIDTH]; o += ATT_WIDTH
    k = w[:, o:o + ATT_KV_HEADS * hd]; o += ATT_KV_HEADS * hd
    v = w[:, o:o + ATT_KV_HEADS * hd]; o += ATT_KV_HEADS * hd
    qm = w[:, o:o + ML_HEADS * ML_DQK]; o += ML_HEADS * ML_DQK
    km = w[:, o:o + ML_HEADS * ML_DQK]; o += ML_HEADS * ML_DQK
    vm = w[:, o:o + ML_WIDTH]; o += ML_WIDTH
    om = w[:, o:o + ML_WIDTH]; o += ML_WIDTH
    gates = w[:, o:o + 2 * ML_HEADS]
    dup = lambda a: jnp.concatenate([a[:, 0:hd], a[:, 0:hd], a[:, hd:2 * hd], a[:, hd:2 * hd]], axis=1)
    gpad = jnp.pad(gates, ((0, 0), (0, LANES - 2 * ML_HEADS)))
    packed = jnp.concatenate([q, dup(k), dup(v), qm, km, vm, om, gpad], axis=1)
    return _bf16(packed), _bf16(gates.T)


def kernel(x, rel_bias, norm_mix_g, w_in, q_norm_g, k_norm_g, attn_sink, conv_w, conv_b, gate_b,
           mlstm_norm_g, w_out, norm_ffn_g, w_router_group, b_router_group, w_router_expert,
           b_router_expert, w_gate, w_up, w_down):
    batch, seq_len, _ = x.shape
    n_tok = batch * seq_len
    assert seq_len % TM_PROJ == 0 and seq_len % ML_STEP == 0 and n_tok % TM_MOE == 0
    f32 = jnp.float32
    bias = _attention_bias(rel_bias)
    bdq = _block_diag_mean(ATT_WIDTH, ATT_HEAD_DIM)
    bdk = _block_diag_mean(2 * LANES, ATT_HEAD_DIM)
    x2 = x.reshape(n_tok, D_MODEL)
    for l in range(DEPTH):
        w_pack, wgt = _pack_w_in(w_in[l])
        qg = (jnp.tile(q_norm_g[l].astype(f32), ATT_HEADS) * (ATT_HEAD_DIM ** -0.5))[None, :]
        kg = jnp.tile(k_norm_g[l].astype(f32), 2 * ATT_KV_HEADS)[None, :]
        gb = jnp.pad(gate_b[l].astype(f32), (0, LANES - 2 * ML_HEADS))[None, :]
        gbt = gate_b[l].astype(f32)[:, None]
        qn, kn, vd, qkm, vm, om, gates, gates_t = _in_proj(
            x2, norm_mix_g[l][None, :], w_pack, wgt, qg, kg, bdq, bdk,
            conv_w[l], conv_b[l][None, :], gb, gbt, batch, seq_len)
        att = _attention(qn, kn, vd, bias, attn_sink[l].astype(f32), batch, seq_len)
        r3 = lambda a: a.reshape(batch, seq_len, a.shape[-1])
        hm = _mlstm(r3(qkm), r3(vm), r3(om), r3(gates), gates_t,
                    mlstm_norm_g[l][None, :], batch, seq_len)
        wr = jnp.pad(jnp.concatenate([w_router_expert[l], w_router_group[l]], axis=1).astype(f32),
                     ((0, 0), (0, LANES - N_EXPERTS - N_GROUPS)))
        br = jnp.pad(jnp.concatenate([b_router_expert[l], b_router_group[l]]).astype(f32),
                     (0, LANES - N_EXPERTS - N_GROUPS))[None, :]
        x1, t, comb = _out_proj_router(x2, att, hm.reshape(n_tok, ML_WIDTH), _bf16(w_out[l]),
                                       norm_ffn_g[l][None, :], wr, br)
        x2 = _moe(x1, t, comb, _bf16(w_gate[l]), _bf16(w_up[l]), _bf16(w_down[l]))
    return x2.reshape(batch, seq_len, D_MODEL)
```

```python
import functools
import math

import jax
import jax.numpy as jnp
import numpy as np
from jax import lax
from jax.experimental import pallas as pl
from jax.experimental.pallas import tpu as pltpu

D_MODEL = 1024
DEPTH = 2
ATT_HEADS = 8
ATT_KV_HEADS = 2
ATT_HEAD_DIM = 64
ATT_WIDTH = ATT_HEADS * ATT_HEAD_DIM
WINDOW = 128
ATT_BLOCK = 128
N_BUCKETS = 32
MAX_DISTANCE = 128
ML_HEADS = 4
ML_DQK = 64
ML_DV = 128
ML_WIDTH = ML_HEADS * ML_DV
ML_CHUNK = 64
CONV_K = 4
N_GROUPS = 4
EXPERTS_PER_GROUP = 4
N_EXPERTS = N_GROUPS * EXPERTS_PER_GROUP
D_FF_EXPERT = 512
EPS = 1e-6

LANES = 128
SUBLANES = 8
NEG_BIG = -1e30
VMEM_LIMIT = 48 * 1024 * 1024

C_Q = 0
C_K = C_Q + ATT_WIDTH
C_V = C_K + 2 * LANES
C_QKM = C_V + 2 * LANES
C_VM = C_QKM + 2 * ML_HEADS * ML_DQK
C_OM = C_VM + ML_WIDTH
C_G = C_OM + ML_WIDTH
N_PACK = C_G + LANES

TM_PROJ = 512
ML_STEP = 2 * ML_CHUNK

SEG_ALIGN = 16
MOE_BM = 256
L_CAP = 2 * TM_PROJ + N_EXPERTS * SEG_ALIGN
N_CHUNK = L_CAP // SEG_ALIGN
DISPATCH_ROWS = 256
COMBINE_ROWS = 256
ROW_POS0, ROW_POS1, ROW_W0, ROW_W1, ROW_E0, ROW_E1 = 0, 1, 2, 3, 4, 5
COL_W0H, COL_W1H, COL_E0, COL_E1 = 8, 11, 14, 15


def _moe_capacity(n_tok):
    n_tiles = n_tok // TM_PROJ
    rows = 2 * n_tok + n_tiles * N_EXPERTS * (SEG_ALIGN - 1) + N_EXPERTS * (MOE_BM - SEG_ALIGN)
    n_blk = -(-rows // MOE_BM)
    return n_blk * MOE_BM, n_blk


def _bf16(a):
    return a.astype(jnp.bfloat16)


def _log_sigmoid(z):
    return jnp.minimum(z, 0.0) - jnp.log(1.0 + jnp.exp(-jnp.abs(z)))


def _in_proj_kernel(tiles_per_seq, x_ref, g_ref, w_ref, wgt_ref, qg_ref, kg_ref, bdq_ref, bdk_ref,
                    cw_ref, cb_ref, gb_ref, gbt_ref,
                    qn_ref, kn_ref, vd_ref, qkm_ref, vm_ref, om_ref, gate_ref, gatet_ref,
                    conv_scr):
    i = pl.program_id(0)
    tm = x_ref.shape[0]
    x = x_ref[...]
    hn = x * lax.rsqrt(jnp.mean(x * x, axis=-1, keepdims=True) + EPS) * g_ref[...]
    hb = _bf16(hn)

    def proj(c0, width):
        return jnp.dot(hb, w_ref[:, c0:c0 + width], preferred_element_type=jnp.float32)

    q = proj(C_Q, ATT_WIDTH)
    q_ms = jnp.dot(_bf16(q * q), bdq_ref[...], preferred_element_type=jnp.float32)
    qn_ref[...] = _bf16(q * lax.rsqrt(q_ms + EPS) * qg_ref[...])
    k = proj(C_K, 2 * LANES)
    k_ms = jnp.dot(_bf16(k * k), bdk_ref[...], preferred_element_type=jnp.float32)
    kn_ref[...] = _bf16(k * lax.rsqrt(k_ms + EPS) * kg_ref[...])
    vd_ref[...] = _bf16(proj(C_V, 2 * LANES))

    qk = proj(C_QKM, 2 * ML_HEADS * ML_DQK)

    @pl.when(i % tiles_per_seq == 0)
    def _():
        conv_scr[0:SUBLANES, :] = jnp.zeros((SUBLANES, qk.shape[1]), jnp.float32)

    @pl.when(i % tiles_per_seq != 0)
    def _():
        conv_scr[0:SUBLANES, :] = conv_scr[tm:tm + SUBLANES, :]

    conv_scr[SUBLANES:SUBLANES + tm, :] = qk
    y = qk * cw_ref[CONV_K - 1:CONV_K, :] + cb_ref[...]
    for j in range(CONV_K - 1):
        off = SUBLANES - (CONV_K - 1) + j
        y = y + conv_scr[off:off + tm, :] * cw_ref[j:j + 1, :]
    y = y * jax.nn.sigmoid(y)
    lane = lax.broadcasted_iota(jnp.int32, y.shape, 1)
    y = jnp.where(lane >= ML_HEADS * ML_DQK, y * (ML_DQK ** -0.5), y)
    qkm_ref[...] = _bf16(y)

    vm_ref[...] = _bf16(proj(C_VM, ML_WIDTH))
    om_ref[...] = _bf16(jax.nn.sigmoid(proj(C_OM, ML_WIDTH)))

    gp = proj(C_G, LANES) + gb_ref[...]
    glane = lax.broadcasted_iota(jnp.int32, gp.shape, 1)
    gate_ref[...] = jnp.where(glane >= ML_HEADS, _log_sigmoid(gp), gp)
    gt = lax.dot_general(wgt_ref[...], hb, (((1,), (1,)), ((), ())),
                         preferred_element_type=jnp.float32) + gbt_ref[...]
    grow = lax.broadcasted_iota(jnp.int32, gt.shape, 0)
    gatet_ref[...] = jnp.where(grow >= ML_HEADS, _log_sigmoid(gt), gt)


def _in_proj(x2, g, w_pack, wgt, qg, kg, bdq, bdk, cw, cb, gb, gbt, batch, seq_len):
    t = x2.shape[0]
    tm = TM_PROJ
    n = t // tm
    tps = seq_len // tm
    row = lambda i: (i, 0)
    fix = lambda i: (0, 0)
    out_shapes = (
        jax.ShapeDtypeStruct((t, ATT_WIDTH), jnp.bfloat16),
        jax.ShapeDtypeStruct((t, 2 * LANES), jnp.bfloat16),
        jax.ShapeDtypeStruct((t, 2 * LANES), jnp.bfloat16),
        jax.ShapeDtypeStruct((t, 2 * ML_HEADS * ML_DQK), jnp.bfloat16),
        jax.ShapeDtypeStruct((t, ML_WIDTH), jnp.bfloat16),
        jax.ShapeDtypeStruct((t, ML_WIDTH), jnp.bfloat16),
        jax.ShapeDtypeStruct((t, LANES), jnp.float32),
        jax.ShapeDtypeStruct((batch, SUBLANES, seq_len), jnp.float32),
    )
    in_specs = [
        pl.BlockSpec((tm, D_MODEL), row),
        pl.BlockSpec((1, D_MODEL), fix),
        pl.BlockSpec((D_MODEL, N_PACK), fix),
        pl.BlockSpec((SUBLANES, D_MODEL), fix),
        pl.BlockSpec((1, ATT_WIDTH), fix),
        pl.BlockSpec((1, 2 * LANES), fix),
        pl.BlockSpec((ATT_WIDTH, ATT_WIDTH), fix),
        pl.BlockSpec((2 * LANES, 2 * LANES), fix),
        pl.BlockSpec((CONV_K, 2 * ML_HEADS * ML_DQK), fix),
        pl.BlockSpec((1, 2 * ML_HEADS * ML_DQK), fix),
        pl.BlockSpec((1, LANES), fix),
        pl.BlockSpec((SUBLANES, 1), fix),
    ]
    out_specs = [
        pl.BlockSpec((tm, ATT_WIDTH), row),
        pl.BlockSpec((tm, 2 * LANES), row),
        pl.BlockSpec((tm, 2 * LANES), row),
        pl.BlockSpec((tm, 2 * ML_HEADS * ML_DQK), row),
        pl.BlockSpec((tm, ML_WIDTH), row),
        pl.BlockSpec((tm, ML_WIDTH), row),
        pl.BlockSpec((tm, LANES), row),
        pl.BlockSpec((None, SUBLANES, tm), lambda i: (i // tps, 0, i % tps)),
    ]
    return pl.pallas_call(
        functools.partial(_in_proj_kernel, seq_len // tm),
        grid=(n,),
        in_specs=in_specs,
        out_specs=out_specs,
        out_shape=out_shapes,
        scratch_shapes=[pltpu.VMEM((tm + 2 * SUBLANES, 2 * ML_HEADS * ML_DQK), jnp.float32)],
        compiler_params=pltpu.CompilerParams(dimension_semantics=("arbitrary",),
                                             vmem_limit_bytes=VMEM_LIMIT),
        name="in_proj",
    )(x2, g, w_pack, wgt, qg, kg, bdq, bdk, cw, cb, gb, gbt)


def _attn_kernel(sink_ref, q_ref, kp_ref, kc_ref, vp_ref, vc_ref, bias_ref, o_ref):
    i = pl.program_id(1)
    blk = q_ref.shape[0]
    lane = lax.broadcasted_iota(jnp.int32, (blk, LANES), 1)
    low = lane < ATT_HEAD_DIM
    col = lax.broadcasted_iota(jnp.int32, (blk, 2 * blk), 1)
    no_prev = jnp.logical_and(col < blk, i == 0)
    group = ATT_HEADS // ATT_KV_HEADS
    for pair in range(ATT_HEADS // 2):
        kv = (2 * pair) // group
        qp = q_ref[:, pair * LANES:(pair + 1) * LANES]
        kcat = jnp.concatenate([kp_ref[:, kv * LANES:(kv + 1) * LANES],
                                kc_ref[:, kv * LANES:(kv + 1) * LANES]], axis=0)
        vcat = jnp.concatenate([vp_ref[:, kv * LANES:(kv + 1) * LANES],
                                vc_ref[:, kv * LANES:(kv + 1) * LANES]], axis=0)
        halves = []
        for sub in range(2):
            h = 2 * pair + sub
            qm = jnp.where(low if sub == 0 else jnp.logical_not(low), qp, jnp.zeros_like(qp))
            s = lax.dot_general(qm, kcat, (((1,), (1,)), ((), ())),
                                preferred_element_type=jnp.float32)
            logits = jnp.where(no_prev, NEG_BIG, s + bias_ref[h])
            sink = sink_ref[h]
            m = jnp.maximum(jnp.max(logits, axis=-1, keepdims=True), sink)
            p = jnp.exp(logits - m)
            den = jnp.sum(p, axis=-1, keepdims=True) + jnp.exp(sink - m)
            o = jnp.dot(_bf16(p), vcat, preferred_element_type=jnp.float32)
            halves.append(o / den)
        o_ref[:, pair * LANES:(pair + 1) * LANES] = _bf16(jnp.where(low, halves[0], halves[1]))


def _attention(qn, kn, vd, bias, sink, batch, seq_len):
    nb = seq_len // ATT_BLOCK
    cur = lambda b, i, s: (b * nb + i, 0)
    prev = lambda b, i, s: (b * nb + jnp.maximum(i - 1, 0), 0)
    grid_spec = pltpu.PrefetchScalarGridSpec(
        num_scalar_prefetch=1,
        grid=(batch, nb),
        in_specs=[
            pl.BlockSpec((ATT_BLOCK, ATT_WIDTH), cur),
            pl.BlockSpec((ATT_BLOCK, 2 * LANES), prev),
            pl.BlockSpec((ATT_BLOCK, 2 * LANES), cur),
            pl.BlockSpec((ATT_BLOCK, 2 * LANES), prev),
            pl.BlockSpec((ATT_BLOCK, 2 * LANES), cur),
            pl.BlockSpec((ATT_HEADS, ATT_BLOCK, 2 * ATT_BLOCK), lambda b, i, s: (0, 0, 0)),
        ],
        out_specs=pl.BlockSpec((ATT_BLOCK, ATT_WIDTH), cur),
    )
    return pl.pallas_call(
        _attn_kernel,
        grid_spec=grid_spec,
        out_shape=jax.ShapeDtypeStruct((batch * seq_len, ATT_WIDTH), jnp.bfloat16),
        compiler_params=pltpu.CompilerParams(dimension_semantics=("arbitrary", "arbitrary"),
                                             vmem_limit_bytes=VMEM_LIMIT),
        name="swa_attention",
    )(sink, qn, kn, kn, vd, vd, bias)


def _mlstm_kernel(qk_ref, v_ref, o_ref, g_ref, gt_ref, ng_ref, out_ref, c_scr, m_scr):
    step = pl.program_id(0)
    batch = qk_ref.shape[0]
    L = ML_CHUNK
    pairs = ML_HEADS // 2
    hi = lax.Precision.HIGHEST

    @pl.when(step == 0)
    def _():
        c_scr[...] = jnp.zeros(c_scr.shape, jnp.float32)
        m_scr[...] = jnp.zeros(m_scr.shape, jnp.float32)

    r_i = lax.broadcasted_iota(jnp.int32, (L, L), 0)
    c_i = lax.broadcasted_iota(jnp.int32, (L, L), 1)
    causal = c_i <= r_i
    tril = causal.astype(jnp.float32)
    triu = (r_i <= c_i).astype(jnp.float32)
    lane = lax.broadcasted_iota(jnp.int32, (L, LANES), 1)
    low = lane < ML_DQK
    one_col = (lane == 0).astype(jnp.bfloat16)
    row2 = lax.broadcasted_iota(jnp.int32, (2 * ML_DQK, 1), 0)

    for ch in range(ML_STEP // L):
        r0 = ch * L
        for b in range(batch):
            g = g_ref[b, r0:r0 + L, :]
            gt = gt_ref[b, :, r0:r0 + L]
            bcols = jnp.dot(tril, g, precision=hi, preferred_element_type=jnp.float32)
            brows = jnp.dot(gt, triu, precision=hi, preferred_element_type=jnp.float32)
            for pair in range(pairs):
                sidx = b * pairs + pair
                qp = qk_ref[b, r0:r0 + L, pair * LANES:(pair + 1) * LANES]
                kp = qk_ref[b, r0:r0 + L, (pairs + pair) * LANES:(pairs + pair + 1) * LANES]
                c_pair = c_scr[sidx]
                c_bf = _bf16(c_pair)
                new_c = None
                decays = []
                for sub in range(2):
                    h = 2 * pair + sub
                    sel = low if sub == 0 else jnp.logical_not(low)
                    m_prev = m_scr[b * ML_HEADS + h]
                    bc = bcols[:, ML_HEADS + h:ML_HEADS + h + 1]
                    br = brows[ML_HEADS + h:ML_HEADS + h + 1, :]
                    lic = g[:, h:h + 1]
                    lir = gt[h:h + 1, :]
                    log_d = jnp.where(causal, bc - br + lir, NEG_BIG)
                    m_inter = bc + m_prev
                    m_row = jnp.maximum(m_inter, jnp.max(log_d, axis=-1, keepdims=True))
                    d = jnp.exp(log_d - m_row)
                    inter = jnp.exp(m_inter - m_row)
                    qm = jnp.where(sel, qp, jnp.zeros_like(qp))
                    s = lax.dot_general(qm, kp, (((1,), (1,)), ((), ())),
                                        preferred_element_type=jnp.float32) * d
                    v_ext = jnp.concatenate([v_ref[b, r0:r0 + L, h * ML_DV:(h + 1) * ML_DV], one_col],
                                            axis=-1)
                    num = inter * jnp.dot(qm, c_bf, preferred_element_type=jnp.float32) \
                        + jnp.dot(_bf16(s), v_ext, preferred_element_type=jnp.float32)
                    den = num[:, ML_DV:ML_DV + 1]
                    hval = num[:, 0:ML_DV] / jnp.maximum(jnp.abs(den), jnp.exp(-m_row))
                    hn = hval * lax.rsqrt(jnp.mean(hval * hval, axis=-1, keepdims=True) + EPS)
                    hn = hn * ng_ref[:, h * ML_DV:(h + 1) * ML_DV]
                    out_ref[b, r0:r0 + L, h * ML_DV:(h + 1) * ML_DV] = _bf16(
                        hn * o_ref[b, r0:r0 + L, h * ML_DV:(h + 1) * ML_DV].astype(jnp.float32))
                    b_last = bc[L - 1:L, :]
                    log_w = b_last - bc + lic
                    m_next = jnp.maximum(b_last + m_prev, jnp.max(log_w, axis=0, keepdims=True))
                    w = jnp.exp(log_w - m_next)
                    decays.append(jnp.exp(b_last + m_prev - m_next))
                    m_scr[b * ML_HEADS + h] = m_next
                    kw = _bf16(jnp.where(sel, kp.astype(jnp.float32) * w, 0.0))
                    upd = lax.dot_general(kw, v_ext, (((0,), (0,)), ((), ())),
                                          preferred_element_type=jnp.float32)
                    new_c = upd if new_c is None else new_c + upd
                decay_rows = jnp.where(row2 < ML_DQK, decays[0], decays[1])
                c_scr[sidx] = decay_rows * c_pair + new_c


def _mlstm(qkm, vm, om, gates, gates_t, ng, batch, seq_len):
    n_steps = seq_len // ML_STEP
    blk = lambda c: (0, c, 0)
    return pl.pallas_call(
        _mlstm_kernel,
        grid=(n_steps,),
        in_specs=[
            pl.BlockSpec((batch, ML_STEP, 2 * ML_HEADS * ML_DQK), blk),
            pl.BlockSpec((batch, ML_STEP, ML_WIDTH), blk),
            pl.BlockSpec((batch, ML_STEP, ML_WIDTH), blk),
            pl.BlockSpec((batch, ML_STEP, LANES), blk),
            pl.BlockSpec((batch, SUBLANES, ML_STEP), lambda c: (0, 0, c)),
            pl.BlockSpec((1, ML_WIDTH), lambda c: (0, 0)),
        ],
        out_specs=pl.BlockSpec((batch, ML_STEP, ML_WIDTH), blk),
        out_shape=jax.ShapeDtypeStruct((batch, seq_len, ML_WIDTH), jnp.bfloat16),
        scratch_shapes=[
            pltpu.VMEM((batch * ML_HEADS // 2, 2 * ML_DQK, 2 * ML_DV), jnp.float32),
            pltpu.VMEM((batch * ML_HEADS, 1, 1), jnp.float32),
        ],
        compiler_params=pltpu.CompilerParams(dimension_semantics=("arbitrary",),
                                             vmem_limit_bytes=VMEM_LIMIT),
        name="mlstm_scan",
    )(qkm, vm, om, gates, gates_t, ng)


def _out_proj_router_kernel(x_ref, att_ref, hm_ref, wo_ref, g_ref, wrt_ref, brt_ref,
                            x1_ref, t_ref, row_ref, col_ref, cnt_ref):
    tm = x_ref.shape[0]
    hi = lax.Precision.HIGHEST
    x1 = x_ref[...] \
        + jnp.dot(att_ref[...], wo_ref[0:ATT_WIDTH, :], preferred_element_type=jnp.float32) \
        + jnp.dot(hm_ref[...], wo_ref[ATT_WIDTH:, :], preferred_element_type=jnp.float32)
    x1_ref[...] = x1
    tn = x1 * lax.rsqrt(jnp.mean(x1 * x1, axis=-1, keepdims=True) + EPS) * g_ref[...]
    t_ref[...] = _bf16(tn)
    logits = lax.dot_general(wrt_ref[...], tn, (((1,), (1,)), ((), ())), precision=hi,
                             preferred_element_type=jnp.float32) + brt_ref[...]
    el_all = logits[0:N_EXPERTS, :]
    gl = logits[N_EXPERTS:N_EXPERTS + SUBLANES, :]
    grow = lax.broadcasted_iota(jnp.int32, gl.shape, 0).astype(jnp.float32)
    gl = jnp.where(grow < N_GROUPS, gl, NEG_BIG)
    gmax = jnp.max(gl, axis=0, keepdims=True)
    grp = jnp.min(jnp.where(gl == gmax, grow, float(N_GROUPS)), axis=0, keepdims=True)
    p_grp = 1.0 / jnp.sum(jnp.exp(gl - gmax), axis=0, keepdims=True)
    erow = lax.broadcasted_iota(jnp.int32, el_all.shape, 0).astype(jnp.float32)
    egrp = jnp.floor(erow * (1.0 / EXPERTS_PER_GROUP))
    el = jnp.where(egrp == grp, el_all, NEG_BIG)
    e1 = jnp.max(el, axis=0, keepdims=True)
    i1 = jnp.min(jnp.where(el == e1, erow, float(N_EXPERTS)), axis=0, keepdims=True)
    el2 = jnp.where(erow == i1, NEG_BIG, el)
    e2 = jnp.max(el2, axis=0, keepdims=True)
    i2 = jnp.min(jnp.where(el2 == e2, erow, float(N_EXPERTS)), axis=0, keepdims=True)
    z2 = jnp.exp(e2 - e1)
    w1 = p_grp / (1.0 + z2)
    w2 = p_grp * z2 / (1.0 + z2)
    sel1 = erow == i1
    sel2 = erow == i2
    onehot = jnp.logical_or(sel1, sel2)
    t_r = lax.broadcasted_iota(jnp.int32, (tm, tm), 0)
    t_c = lax.broadcasted_iota(jnp.int32, (tm, tm), 1)
    before = (t_r < t_c).astype(jnp.bfloat16)
    rank = jnp.dot(onehot.astype(jnp.bfloat16), before, preferred_element_type=jnp.float32)
    cnt = jnp.sum(onehot.astype(jnp.float32), axis=1, keepdims=True)
    cnt_al = jnp.floor((cnt + (SEG_ALIGN - 1)) * (1.0 / SEG_ALIGN)) * SEG_ALIGN
    e_r = lax.broadcasted_iota(jnp.int32, (N_EXPERTS, N_EXPERTS), 0)
    e_c = lax.broadcasted_iota(jnp.int32, (N_EXPERTS, N_EXPERTS), 1)
    lstart = jnp.dot((e_c < e_r).astype(jnp.float32), jnp.broadcast_to(cnt_al, (N_EXPERTS, LANES)),
                     precision=hi, preferred_element_type=jnp.float32)[:, 0:1]
    slot = lstart + rank
    pos1 = jnp.sum(jnp.where(sel1, slot, 0.0), axis=0, keepdims=True)
    pos2 = jnp.sum(jnp.where(sel2, slot, 0.0), axis=0, keepdims=True)
    r8 = lax.broadcasted_iota(jnp.int32, (SUBLANES, tm), 0)
    info = jnp.where(r8 == ROW_POS0, pos1, jnp.where(r8 == ROW_POS1, pos2, jnp.where(
        r8 == ROW_W0, w1, jnp.where(r8 == ROW_W1, w2, jnp.where(
            r8 == ROW_E0, i1, jnp.where(r8 == ROW_E1, i2, 0.0))))))
    row_ref[...] = info

    def split3(w):
        h = _bf16(w).astype(jnp.float32)
        m = _bf16(w - h).astype(jnp.float32)
        return h, m, _bf16(w - h - m).astype(jnp.float32)

    w1h, w1m, w1l = split3(w1)
    w2h, w2m, w2l = split3(w2)
    parts = jnp.where(r8 == 0, w1h, jnp.where(r8 == 1, w1m, jnp.where(r8 == 2, w1l, jnp.where(
        r8 == 3, w2h, jnp.where(r8 == 4, w2m, jnp.where(r8 == 5, w2l, jnp.where(r8 == 6, i1, i2)))))))
    col_ref[...] = jnp.concatenate(
        [info, parts, jnp.zeros((LANES - 2 * SUBLANES, tm), jnp.float32)], axis=0).T
    cnt_ref[...] = jnp.broadcast_to(cnt, (N_EXPERTS, LANES)).astype(jnp.int32)


def _out_proj_router(x2, att, hm, wo, g, wrt, brt):
    t = x2.shape[0]
    tm = TM_PROJ
    row = lambda i: (i, 0)
    fix = lambda i: (0, 0)
    return pl.pallas_call(
        _out_proj_router_kernel,
        grid=(t // tm,),
        in_specs=[
            pl.BlockSpec((tm, D_MODEL), row),
            pl.BlockSpec((tm, ATT_WIDTH), row),
            pl.BlockSpec((tm, ML_WIDTH), row),
            pl.BlockSpec((D_MODEL, D_MODEL), fix),
            pl.BlockSpec((1, D_MODEL), fix),
            pl.BlockSpec((4 * SUBLANES, D_MODEL), fix),
            pl.BlockSpec((4 * SUBLANES, 1), fix),
        ],
        out_specs=[
            pl.BlockSpec((tm, D_MODEL), row),
            pl.BlockSpec((tm, D_MODEL), row),
            pl.BlockSpec((SUBLANES, tm), lambda i: (0, i)),
            pl.BlockSpec((tm, LANES), row),
            pl.BlockSpec((None, N_EXPERTS, LANES), lambda i: (i, 0, 0)),
        ],
        out_shape=(
            jax.ShapeDtypeStruct((t, D_MODEL), jnp.float32),
            jax.ShapeDtypeStruct((t, D_MODEL), jnp.bfloat16),
            jax.ShapeDtypeStruct((SUBLANES, t), jnp.float32),
            jax.ShapeDtypeStruct((t, LANES), jnp.float32),
            jax.ShapeDtypeStruct((t // tm, N_EXPERTS, LANES), jnp.int32),
        ),
        compiler_params=pltpu.CompilerParams(dimension_semantics=("arbitrary",),
                                             vmem_limit_bytes=VMEM_LIMIT),
        name="out_proj_router",
    )(x2, att, hm, wo, g, wrt, brt)


def _chunk_rows(j):
    return pl.ds(pl.multiple_of(j * SEG_ALIGN, SEG_ALIGN), SEG_ALIGN)


def _dispatch_kernel(nch_ref, dst_ref, nz_ref, zdst_ref, t_ref, col_ref, row_ref, xs_hbm, sd_hbm,
                     xbuf, sbuf, zx, zs, sem_x, sem_s, sem_z):
    i = pl.program_id(0)
    n = pl.num_programs(0)
    p = i % 2
    tm = t_ref.shape[0]

    def copies(tile, par, j):
        d = pl.ds(pl.multiple_of(dst_ref[tile, j], SEG_ALIGN), SEG_ALIGN)
        return (pltpu.make_async_copy(xbuf.at[par, _chunk_rows(j)], xs_hbm.at[d], sem_x.at[par]),
                pltpu.make_async_copy(sbuf.at[par, _chunk_rows(j)], sd_hbm.at[d], sem_s.at[par]))

    def zero_copies(j):
        d = pl.ds(pl.multiple_of(zdst_ref[j], SEG_ALIGN), SEG_ALIGN)
        return (pltpu.make_async_copy(zx, xs_hbm.at[d], sem_z.at[0]),
                pltpu.make_async_copy(zs, sd_hbm.at[d], sem_z.at[1]))

    def wait_tile(tile, par):
        def body(j, c):
            cx, cs = copies(tile, par, j)
            cx.wait()
            cs.wait()
            return c
        lax.fori_loop(0, nch_ref[tile], body, 0)

    @pl.when(i == 0)
    def _():
        zx[...] = jnp.zeros(zx.shape, zx.dtype)
        zs[...] = jnp.zeros(zs.shape, zs.dtype)

        def body(j, c):
            cx, cs = zero_copies(j)
            cx.start()
            cs.start()
            return c
        lax.fori_loop(0, nz_ref[0], body, 0)

    @pl.when(i >= 2)
    def _():
        wait_tile(i - 2, p)

    pos0 = row_ref[ROW_POS0:ROW_POS0 + 1, :].astype(jnp.int32)
    pos1 = row_ref[ROW_POS1:ROW_POS1 + 1, :].astype(jnp.int32)
    t = t_ref[...]
    side = _bf16(col_ref[...])
    rows = DISPATCH_ROWS
    for c in range(L_CAP // rows):
        r = lax.broadcasted_iota(jnp.int32, (rows, tm), 0) + c * rows
        perm = jnp.logical_or(r == pos0, r == pos1).astype(jnp.bfloat16)
        xbuf[p, c * rows:(c + 1) * rows, :] = _bf16(
            jnp.dot(perm, t, preferred_element_type=jnp.float32))
        sbuf[p, c * rows:(c + 1) * rows, :] = jnp.dot(perm, side, preferred_element_type=jnp.float32)

    def start_body(j, c):
        cx, cs = copies(i, p, j)
        cx.start()
        cs.start()
        return c
    lax.fori_loop(0, nch_ref[i], start_body, 0)

    @pl.when(i == n - 1)
    def _():
        @pl.when(n >= 2)
        def _():
            wait_tile(i - 1, 1 - p)
        wait_tile(i, p)

        def body(j, c):
            cx, cs = zero_copies(j)
            cx.wait()
            cs.wait()
            return c
        lax.fori_loop(0, nz_ref[0], body, 0)


def _dispatch(t, col, row, nch, dst, nz, zdst):
    n_tok = t.shape[0]
    r_cap, _ = _moe_capacity(n_tok)
    tm = TM_PROJ
    grid_spec = pltpu.PrefetchScalarGridSpec(
        num_scalar_prefetch=4,
        grid=(n_tok // tm,),
        in_specs=[
            pl.BlockSpec((tm, D_MODEL), lambda i, *_: (i, 0)),
            pl.BlockSpec((tm, LANES), lambda i, *_: (i, 0)),
            pl.BlockSpec((SUBLANES, tm), lambda i, *_: (0, i)),
        ],
        out_specs=[pl.BlockSpec(memory_space=pl.ANY), pl.BlockSpec(memory_space=pl.ANY)],
        scratch_shapes=[
            pltpu.VMEM((2, L_CAP, D_MODEL), jnp.bfloat16),
            pltpu.VMEM((2, L_CAP, LANES), jnp.float32),
            pltpu.VMEM((SEG_ALIGN, D_MODEL), jnp.bfloat16),
            pltpu.VMEM((SEG_ALIGN, LANES), jnp.float32),
            pltpu.SemaphoreType.DMA((2,)),
            pltpu.SemaphoreType.DMA((2,)),
            pltpu.SemaphoreType.DMA((2,)),
        ],
    )
    return pl.pallas_call(
        _dispatch_kernel,
        grid_spec=grid_spec,
        out_shape=(jax.ShapeDtypeStruct((r_cap, D_MODEL), jnp.bfloat16),
                   jax.ShapeDtypeStruct((r_cap, LANES), jnp.float32)),
        compiler_params=pltpu.CompilerParams(dimension_semantics=("arbitrary",),
                                             vmem_limit_bytes=VMEM_LIMIT),
        name="moe_dispatch",
    )(nch, dst, nz, zdst, t, col, row)


def _moe_kernel(blk_e_ref, nused_ref, xs_ref, sd_ref, wg_ref, wu_ref, wd_ref, ys_ref):
    b = pl.program_id(0)

    @pl.when(b < nused_ref[0])
    def _():
        x = xs_ref[...]
        a = jnp.dot(x, wg_ref[0], preferred_element_type=jnp.float32)
        u = jnp.dot(x, wu_ref[0], preferred_element_type=jnp.float32)
        h = a * jax.nn.sigmoid(a) * u
        y = jnp.dot(_bf16(h), wd_ref[0], preferred_element_type=jnp.float32)
        sd = sd_ref[...]
        e_blk = blk_e_ref[b].astype(jnp.float32)
        w0 = sd[:, COL_W0H:COL_W0H + 1] + sd[:, COL_W0H + 1:COL_W0H + 2] + sd[:, COL_W0H + 2:COL_W0H + 3]
        w1 = sd[:, COL_W1H:COL_W1H + 1] + sd[:, COL_W1H + 1:COL_W1H + 2] + sd[:, COL_W1H + 2:COL_W1H + 3]
        w = jnp.where(sd[:, COL_E0:COL_E0 + 1] == e_blk, w0, w1)
        ys_ref[...] = _bf16(y * w)


def _moe(xs, sd, blk_e, nused, wg, wu, wd):
    blk = lambda b, be, nu: (jnp.minimum(b, nu[0] - 1), 0)
    wsel = lambda b, be, nu: (be[b], 0, 0)
    r_cap = xs.shape[0]
    grid_spec = pltpu.PrefetchScalarGridSpec(
        num_scalar_prefetch=2,
        grid=(r_cap // MOE_BM,),
        in_specs=[
            pl.BlockSpec((MOE_BM, D_MODEL), blk),
            pl.BlockSpec((MOE_BM, LANES), blk),
            pl.BlockSpec((1, D_MODEL, D_FF_EXPERT), wsel),
            pl.BlockSpec((1, D_MODEL, D_FF_EXPERT), wsel),
            pl.BlockSpec((1, D_FF_EXPERT, D_MODEL), wsel),
        ],
        out_specs=pl.BlockSpec((MOE_BM, D_MODEL), blk),
    )
    return pl.pallas_call(
        _moe_kernel,
        grid_spec=grid_spec,
        out_shape=jax.ShapeDtypeStruct((r_cap, D_MODEL), jnp.bfloat16),
        compiler_params=pltpu.CompilerParams(dimension_semantics=("arbitrary",),
                                             vmem_limit_bytes=VMEM_LIMIT),
        name="moe_experts",
    )(blk_e, nused, xs, sd, wg, wu, wd)


def _combine_kernel(nch_ref, dst_ref, x1_ref, col_ref, ys_hbm, out_ref, ybuf, sem):
    i = pl.program_id(0)
    n = pl.num_programs(0)
    p = i % 2
    tm = x1_ref.shape[0]

    def copy(tile, par, j):
        s = pl.ds(pl.multiple_of(dst_ref[tile, j], SEG_ALIGN), SEG_ALIGN)
        return pltpu.make_async_copy(ys_hbm.at[s], ybuf.at[par, _chunk_rows(j)], sem.at[par])

    def start_tile(tile, par):
        def body(j, c):
            copy(tile, par, j).start()
            return c
        lax.fori_loop(0, nch_ref[tile], body, 0)

    @pl.when(i == 0)
    def _():
        ybuf[...] = jnp.zeros(ybuf.shape, ybuf.dtype)
        start_tile(0, 0)

    @pl.when(i + 1 < n)
    def _():
        start_tile(i + 1, 1 - p)

    def wait_body(j, c):
        copy(i, p, j).wait()
        return c
    lax.fori_loop(0, nch_ref[i], wait_body, 0)

    rows = COMBINE_ROWS
    for c in range(tm // rows):
        col = col_ref[c * rows:(c + 1) * rows, :]
        pos0 = col[:, ROW_POS0:ROW_POS0 + 1].astype(jnp.int32)
        pos1 = col[:, ROW_POS1:ROW_POS1 + 1].astype(jnp.int32)
        l = lax.broadcasted_iota(jnp.int32, (rows, L_CAP), 1)
        perm = jnp.logical_or(l == pos0, l == pos1).astype(jnp.bfloat16)
        y = jnp.dot(perm, ybuf[p], preferred_element_type=jnp.float32)
        out_ref[c * rows:(c + 1) * rows, :] = x1_ref[c * rows:(c + 1) * rows, :] + y


def _combine(x1, col, ys, nch, dst):
    n_tok = x1.shape[0]
    tm = TM_PROJ
    grid_spec = pltpu.PrefetchScalarGridSpec(
        num_scalar_prefetch=2,
        grid=(n_tok // tm,),
        in_specs=[
            pl.BlockSpec((tm, D_MODEL), lambda i, *_: (i, 0)),
            pl.BlockSpec((tm, LANES), lambda i, *_: (i, 0)),
            pl.BlockSpec(memory_space=pl.ANY),
        ],
        out_specs=pl.BlockSpec((tm, D_MODEL), lambda i, *_: (i, 0)),
        scratch_shapes=[
            pltpu.VMEM((2, L_CAP, D_MODEL), jnp.bfloat16),
            pltpu.SemaphoreType.DMA((2,)),
        ],
    )
    return pl.pallas_call(
        _combine_kernel,
        grid_spec=grid_spec,
        out_shape=jax.ShapeDtypeStruct((n_tok, D_MODEL), jnp.float32),
        compiler_params=pltpu.CompilerParams(dimension_semantics=("arbitrary",),
                                             vmem_limit_bytes=VMEM_LIMIT),
        name="moe_combine",
    )(nch, dst, x1, col, ys)


def _routing_tables(counts, n_blk_cap):
    i32 = jnp.int32
    ca = ((counts + (SEG_ALIGN - 1)) // SEG_ALIGN) * SEG_ALIGN
    lend = jnp.cumsum(ca, axis=1)
    lstart = lend - ca
    nch = (lend[:, -1] // SEG_ALIGN).astype(i32)
    tot = jnp.sum(ca, axis=0)
    region = ((tot + (MOE_BM - 1)) // MOE_BM) * MOE_BM
    rend = jnp.cumsum(region)
    base = rend - region
    gstart = base[None, :] + jnp.cumsum(ca, axis=0) - ca
    j16 = jnp.arange(N_CHUNK, dtype=i32) * SEG_ALIGN
    e_of_j = jnp.minimum(jnp.sum(lend[:, None, :] <= j16[None, :, None], axis=2), N_EXPERTS - 1)
    dst = (jnp.take_along_axis(gstart - lstart, e_of_j, axis=1) + j16[None, :]).astype(i32)
    dst = jnp.where(j16[None, :] < lend[:, -1:], dst, 0)
    nused = (rend[-1] // MOE_BM).astype(i32)
    brow = jnp.arange(n_blk_cap, dtype=i32) * MOE_BM
    blk_e = jnp.minimum(jnp.sum(rend[None, :] <= brow[:, None], axis=1), N_EXPERTS - 1)
    blk_e = jnp.where(jnp.arange(n_blk_cap) < nused, blk_e, blk_e[nused - 1]).astype(i32)
    k16 = jnp.arange(MOE_BM // SEG_ALIGN - 1, dtype=i32) * SEG_ALIGN
    zrow = (base + tot)[:, None] + k16[None, :]
    zvalid = zrow < rend[:, None]
    order = jnp.argsort(jnp.logical_not(zvalid).reshape(-1), stable=True)
    zdst = jnp.where(zvalid, zrow, 0).reshape(-1)[order].astype(i32)
    nz = jnp.sum(zvalid).astype(i32)
    return nch, dst, nz[None], zdst, blk_e, nused[None]


def _t5_bucket_np(dist):
    max_exact = N_BUCKETS // 2
    d = np.maximum(dist, 1).astype(np.float32)
    large = max_exact + (np.log(d / max_exact) / math.log(MAX_DISTANCE / max_exact)
                         * (N_BUCKETS - max_exact)).astype(np.int32)
    large = np.minimum(large, N_BUCKETS - 1)
    return np.where(dist < max_exact, dist, large)


def _attention_bias(rel_bias):
    qi = np.arange(ATT_BLOCK)[:, None]
    kj = np.arange(2 * ATT_BLOCK)[None, :]
    dist = qi + ATT_BLOCK - kj
    in_window = (dist >= 0) & (dist < WINDOW)
    bucket = _t5_bucket_np(np.clip(dist, 0, WINDOW - 1))
    bias = rel_bias.astype(jnp.float32)[bucket]
    bias = jnp.transpose(bias, (2, 0, 1))
    return jnp.where(jnp.asarray(in_window)[None], bias, NEG_BIG)


def _block_diag_mean(width, block):
    idx = np.arange(width) // block
    return jnp.asarray((idx[:, None] == idx[None, :]).astype(np.float32) / block, dtype=jnp.bfloat16)


def _pack_w_in(w):
    hd = ATT_HEAD_DIM
    o = 0
    q = w[:, o:o + ATT_WIDTH]; o += ATT_WIDTH
    k = w[:, o:o + ATT_KV_HEADS * hd]; o += ATT_KV_HEADS * hd
    v = w[:, o:o + ATT_KV_HEADS * hd]; o += ATT_KV_HEADS * hd
    qm = w[:, o:o + ML_HEADS * ML_DQK]; o += ML_HEADS * ML_DQK
    km = w[:, o:o + ML_HEADS * ML_DQK]; o += ML_HEADS * ML_DQK
    vm = w[:, o:o + ML_WIDTH]; o += ML_WIDTH
    om = w[:, o:o + ML_WIDTH]; o += ML_WIDTH
    gates = w[:, o:o + 2 * ML_HEADS]
    dup = lambda a: jnp.concatenate([a[:, 0:hd], a[:, 0:hd], a[:, hd:2 * hd], a[:, hd:2 * hd]], axis=1)
    gpad = jnp.pad(gates, ((0, 0), (0, LANES - 2 * ML_HEADS)))
    packed = jnp.concatenate([q, dup(k), dup(v), qm, km, vm, om, gpad], axis=1)
    return _bf16(packed), _bf16(gates.T)


def kernel(x, rel_bias, norm_mix_g, w_in, q_norm_g, k_norm_g, attn_sink, conv_w, conv_b, gate_b,
           mlstm_norm_g, w_out, norm_ffn_g, w_router_group, b_router_group, w_router_expert,
           b_router_expert, w_gate, w_up, w_down):
    batch, seq_len, _ = x.shape
    n_tok = batch * seq_len
    assert seq_len % TM_PROJ == 0 and seq_len % ML_STEP == 0
    f32 = jnp.float32
    bias = _attention_bias(rel_bias)
    bdq = _block_diag_mean(ATT_WIDTH, ATT_HEAD_DIM)
    bdk = _block_diag_mean(2 * LANES, ATT_HEAD_DIM)
    x2 = x.reshape(n_tok, D_MODEL)
    for l in range(DEPTH):
        w_pack, wgt = _pack_w_in(w_in[l])
        qg = (jnp.tile(q_norm_g[l].astype(f32), ATT_HEADS) * (ATT_HEAD_DIM ** -0.5))[None, :]
        kg = jnp.tile(k_norm_g[l].astype(f32), 2 * ATT_KV_HEADS)[None, :]
        gb = jnp.pad(gate_b[l].astype(f32), (0, LANES - 2 * ML_HEADS))[None, :]
        gbt = gate_b[l].astype(f32)[:, None]
        qn, kn, vd, qkm, vm, om, gates, gates_t = _in_proj(
            x2, norm_mix_g[l][None, :], w_pack, wgt, qg, kg, bdq, bdk,
            conv_w[l], conv_b[l][None, :], gb, gbt, batch, seq_len)
        att = _attention(qn, kn, vd, bias, attn_sink[l].astype(f32), batch, seq_len)
        r3 = lambda a: a.reshape(batch, seq_len, a.shape[-1])
        hm = _mlstm(r3(qkm), r3(vm), r3(om), r3(gates), gates_t,
                    mlstm_norm_g[l][None, :], batch, seq_len)
        n_rt = 4 * SUBLANES
        wrt = jnp.pad(jnp.concatenate([w_router_expert[l], w_router_group[l]], axis=1).astype(f32).T,
                      ((0, n_rt - N_EXPERTS - N_GROUPS), (0, 0)))
        brt = jnp.pad(jnp.concatenate([b_router_expert[l], b_router_group[l]]).astype(f32),
                      (0, n_rt - N_EXPERTS - N_GROUPS))[:, None]
        x1, t, row, col, cnt = _out_proj_router(x2, att, hm.reshape(n_tok, ML_WIDTH), _bf16(w_out[l]),
                                                norm_ffn_g[l][None, :], wrt, brt)
        _, n_blk_cap = _moe_capacity(n_tok)
        nch, dst, nz, zdst, blk_e, nused = _routing_tables(cnt[:, :, 0], n_blk_cap)
        xs, sd = _dispatch(t, col, row, nch, dst, nz, zdst)
        ys = _moe(xs, sd, blk_e, nused, _bf16(w_gate[l]), _bf16(w_up[l]), _bf16(w_down[l]))
        x2 = _combine(x1, col, ys, nch, dst)
    return x2.reshape(batch, seq_len, D_MODEL)
```

```python
import functools
import math

import jax
import jax.numpy as jnp
import numpy as np
from jax import lax
from jax.experimental import pallas as pl
from jax.experimental.pallas import tpu as pltpu

D_MODEL = 1024
DEPTH = 2
ATT_HEADS = 8
ATT_KV_HEADS = 2
ATT_HEAD_DIM = 64
ATT_WIDTH = ATT_HEADS * ATT_HEAD_DIM
WINDOW = 128
ATT_BLOCK = 128
N_BUCKETS = 32
MAX_DISTANCE = 128
ML_HEADS = 4
ML_DQK = 64
ML_DV = 128
ML_WIDTH = ML_HEADS * ML_DV
ML_CHUNK = 64
CONV_K = 4
N_GROUPS = 4
EXPERTS_PER_GROUP = 4
N_EXPERTS = N_GROUPS * EXPERTS_PER_GROUP
D_FF_EXPERT = 512
EPS = 1e-6

LANES = 128
SUBLANES = 8
NEG_BIG = -1e30
VMEM_LIMIT = 48 * 1024 * 1024

C_Q = 0
C_K = C_Q + ATT_WIDTH
C_V = C_K + 2 * LANES
C_QKM = C_V + 2 * LANES
C_VM = C_QKM + 2 * ML_HEADS * ML_DQK
C_OM = C_VM + ML_WIDTH
C_G = C_OM + ML_WIDTH
N_PACK = C_G + LANES

TM_PROJ = 512
ML_KCHUNK = 256
ML_STEP = ML_KCHUNK

SEG_ALIGN = 16
MOE_BM = 256
L_CAP = 2 * TM_PROJ + N_EXPERTS * SEG_ALIGN
N_CHUNK = L_CAP // SEG_ALIGN
DISPATCH_ROWS = 256
COMBINE_ROWS = 256
ROW_POS0, ROW_POS1, ROW_W0, ROW_W1, ROW_E0, ROW_E1 = 0, 1, 2, 3, 4, 5
COL_W0H, COL_W1H, COL_E0, COL_E1 = 8, 11, 14, 15


def _moe_capacity(n_tok):
    n_tiles = n_tok // TM_PROJ
    rows = 2 * n_tok + n_tiles * N_EXPERTS * (SEG_ALIGN - 1) + N_EXPERTS * (MOE_BM - SEG_ALIGN)
    n_blk = -(-rows // MOE_BM)
    return n_blk * MOE_BM, n_blk


def _bf16(a):
    return a.astype(jnp.bfloat16)


def _log_sigmoid(z):
    return jnp.minimum(z, 0.0) - jnp.log(1.0 + jnp.exp(-jnp.abs(z)))


def _in_proj_kernel(tiles_per_seq, x_ref, g_ref, w_ref, wgt_ref, qg_ref, kg_ref, bdq_ref, bdk_ref,
                    cw_ref, cb_ref, gb_ref, gbt_ref,
                    qn_ref, kn_ref, vd_ref, qkm_ref, vm_ref, om_ref, gate_ref, gatet_ref,
                    conv_scr):
    i = pl.program_id(0)
    tm = x_ref.shape[0]
    x = x_ref[...]
    hn = x * lax.rsqrt(jnp.mean(x * x, axis=-1, keepdims=True) + EPS) * g_ref[...]
    hb = _bf16(hn)

    def proj(c0, width):
        return jnp.dot(hb, w_ref[:, c0:c0 + width], preferred_element_type=jnp.float32)

    q = proj(C_Q, ATT_WIDTH)
    q_ms = jnp.dot(_bf16(q * q), bdq_ref[...], preferred_element_type=jnp.float32)
    qn_ref[...] = _bf16(q * lax.rsqrt(q_ms + EPS) * qg_ref[...])
    k = proj(C_K, 2 * LANES)
    k_ms = jnp.dot(_bf16(k * k), bdk_ref[...], preferred_element_type=jnp.float32)
    kn_ref[...] = _bf16(k * lax.rsqrt(k_ms + EPS) * kg_ref[...])
    vd_ref[...] = _bf16(proj(C_V, 2 * LANES))

    qk = proj(C_QKM, 2 * ML_HEADS * ML_DQK)

    @pl.when(i % tiles_per_seq == 0)
    def _():
        conv_scr[0:SUBLANES, :] = jnp.zeros((SUBLANES, qk.shape[1]), jnp.float32)

    @pl.when(i % tiles_per_seq != 0)
    def _():
        conv_scr[0:SUBLANES, :] = conv_scr[tm:tm + SUBLANES, :]

    conv_scr[SUBLANES:SUBLANES + tm, :] = qk
    y = qk * cw_ref[CONV_K - 1:CONV_K, :] + cb_ref[...]
    for j in range(CONV_K - 1):
        off = SUBLANES - (CONV_K - 1) + j
        y = y + conv_scr[off:off + tm, :] * cw_ref[j:j + 1, :]
    y = y * jax.nn.sigmoid(y)
    lane = lax.broadcasted_iota(jnp.int32, y.shape, 1)
    y = jnp.where(lane >= ML_HEADS * ML_DQK, y * (ML_DQK ** -0.5), y)
    qkm_ref[...] = _bf16(y)

    vm_ref[...] = _bf16(proj(C_VM, ML_WIDTH))
    om_ref[...] = _bf16(jax.nn.sigmoid(proj(C_OM, ML_WIDTH)))

    gp = proj(C_G, LANES) + gb_ref[...]
    glane = lax.broadcasted_iota(jnp.int32, gp.shape, 1)
    gate_ref[...] = jnp.where(glane >= ML_HEADS, _log_sigmoid(gp), gp)
    gt = lax.dot_general(wgt_ref[...], hb, (((1,), (1,)), ((), ())),
                         preferred_element_type=jnp.float32) + gbt_ref[...]
    grow = lax.broadcasted_iota(jnp.int32, gt.shape, 0)
    gatet_ref[...] = jnp.where(grow >= ML_HEADS, _log_sigmoid(gt), gt)


def _in_proj(x2, g, w_pack, wgt, qg, kg, bdq, bdk, cw, cb, gb, gbt, batch, seq_len):
    t = x2.shape[0]
    tm = TM_PROJ
    n = t // tm
    tps = seq_len // tm
    row = lambda i: (i, 0)
    fix = lambda i: (0, 0)
    out_shapes = (
        jax.ShapeDtypeStruct((t, ATT_WIDTH), jnp.bfloat16),
        jax.ShapeDtypeStruct((t, 2 * LANES), jnp.bfloat16),
        jax.ShapeDtypeStruct((t, 2 * LANES), jnp.bfloat16),
        jax.ShapeDtypeStruct((t, 2 * ML_HEADS * ML_DQK), jnp.bfloat16),
        jax.ShapeDtypeStruct((t, ML_WIDTH), jnp.bfloat16),
        jax.ShapeDtypeStruct((t, ML_WIDTH), jnp.bfloat16),
        jax.ShapeDtypeStruct((t, LANES), jnp.float32),
        jax.ShapeDtypeStruct((batch, SUBLANES, seq_len), jnp.float32),
    )
    in_specs = [
        pl.BlockSpec((tm, D_MODEL), row),
        pl.BlockSpec((1, D_MODEL), fix),
        pl.BlockSpec((D_MODEL, N_PACK), fix),
        pl.BlockSpec((SUBLANES, D_MODEL), fix),
        pl.BlockSpec((1, ATT_WIDTH), fix),
        pl.BlockSpec((1, 2 * LANES), fix),
        pl.BlockSpec((ATT_WIDTH, ATT_WIDTH), fix),
        pl.BlockSpec((2 * LANES, 2 * LANES), fix),
        pl.BlockSpec((CONV_K, 2 * ML_HEADS * ML_DQK), fix),
        pl.BlockSpec((1, 2 * ML_HEADS * ML_DQK), fix),
        pl.BlockSpec((1, LANES), fix),
        pl.BlockSpec((SUBLANES, 1), fix),
    ]
    out_specs = [
        pl.BlockSpec((tm, ATT_WIDTH), row),
        pl.BlockSpec((tm, 2 * LANES), row),
        pl.BlockSpec((tm, 2 * LANES), row),
        pl.BlockSpec((tm, 2 * ML_HEADS * ML_DQK), row),
        pl.BlockSpec((tm, ML_WIDTH), row),
        pl.BlockSpec((tm, ML_WIDTH), row),
        pl.BlockSpec((tm, LANES), row),
        pl.BlockSpec((None, SUBLANES, tm), lambda i: (i // tps, 0, i % tps)),
    ]
    return pl.pallas_call(
        functools.partial(_in_proj_kernel, seq_len // tm),
        grid=(n,),
        in_specs=in_specs,
        out_specs=out_specs,
        out_shape=out_shapes,
        scratch_shapes=[pltpu.VMEM((tm + 2 * SUBLANES, 2 * ML_HEADS * ML_DQK), jnp.float32)],
        compiler_params=pltpu.CompilerParams(dimension_semantics=("arbitrary",),
                                             vmem_limit_bytes=VMEM_LIMIT),
        name="in_proj",
    )(x2, g, w_pack, wgt, qg, kg, bdq, bdk, cw, cb, gb, gbt)


def _attn_kernel(sink_ref, q_ref, kp_ref, kc_ref, vp_ref, vc_ref, bias_ref, o_ref):
    i = pl.program_id(1)
    blk = q_ref.shape[0]
    lane = lax.broadcasted_iota(jnp.int32, (blk, LANES), 1)
    low = lane < ATT_HEAD_DIM
    col = lax.broadcasted_iota(jnp.int32, (blk, 2 * blk), 1)
    no_prev = jnp.logical_and(col < blk, i == 0)
    group = ATT_HEADS // ATT_KV_HEADS
    for pair in range(ATT_HEADS // 2):
        kv = (2 * pair) // group
        qp = q_ref[:, pair * LANES:(pair + 1) * LANES]
        kcat = jnp.concatenate([kp_ref[:, kv * LANES:(kv + 1) * LANES],
                                kc_ref[:, kv * LANES:(kv + 1) * LANES]], axis=0)
        vcat = jnp.concatenate([vp_ref[:, kv * LANES:(kv + 1) * LANES],
                                vc_ref[:, kv * LANES:(kv + 1) * LANES]], axis=0)
        halves = []
        for sub in range(2):
            h = 2 * pair + sub
            qm = jnp.where(low if sub == 0 else jnp.logical_not(low), qp, jnp.zeros_like(qp))
            s = lax.dot_general(qm, kcat, (((1,), (1,)), ((), ())),
                                preferred_element_type=jnp.float32)
            logits = jnp.where(no_prev, NEG_BIG, s + bias_ref[h])
            sink = sink_ref[h]
            m = jnp.maximum(jnp.max(logits, axis=-1, keepdims=True), sink)
            p = jnp.exp(logits - m)
            den = jnp.sum(p, axis=-1, keepdims=True) + jnp.exp(sink - m)
            o = jnp.dot(_bf16(p), vcat, preferred_element_type=jnp.float32)
            halves.append(o / den)
        o_ref[:, pair * LANES:(pair + 1) * LANES] = _bf16(jnp.where(low, halves[0], halves[1]))


def _attention(qn, kn, vd, bias, sink, batch, seq_len):
    nb = seq_len // ATT_BLOCK
    cur = lambda b, i, s: (b * nb + i, 0)
    prev = lambda b, i, s: (b * nb + jnp.maximum(i - 1, 0), 0)
    grid_spec = pltpu.PrefetchScalarGridSpec(
        num_scalar_prefetch=1,
        grid=(batch, nb),
        in_specs=[
            pl.BlockSpec((ATT_BLOCK, ATT_WIDTH), cur),
            pl.BlockSpec((ATT_BLOCK, 2 * LANES), prev),
            pl.BlockSpec((ATT_BLOCK, 2 * LANES), cur),
            pl.BlockSpec((ATT_BLOCK, 2 * LANES), prev),
            pl.BlockSpec((ATT_BLOCK, 2 * LANES), cur),
            pl.BlockSpec((ATT_HEADS, ATT_BLOCK, 2 * ATT_BLOCK), lambda b, i, s: (0, 0, 0)),
        ],
        out_specs=pl.BlockSpec((ATT_BLOCK, ATT_WIDTH), cur),
    )
    return pl.pallas_call(
        _attn_kernel,
        grid_spec=grid_spec,
        out_shape=jax.ShapeDtypeStruct((batch * seq_len, ATT_WIDTH), jnp.bfloat16),
        compiler_params=pltpu.CompilerParams(dimension_semantics=("arbitrary", "arbitrary"),
                                             vmem_limit_bytes=VMEM_LIMIT),
        name="swa_attention",
    )(sink, qn, kn, kn, vd, vd, bias)


def _mlstm_kernel(qk_ref, v_ref, o_ref, g_ref, gt_ref, ng_ref, out_ref, c_scr, m_scr):
    step = pl.program_id(0)
    batch = qk_ref.shape[0]
    L = ML_KCHUNK
    pairs = ML_HEADS // 2
    hi = lax.Precision.HIGHEST

    @pl.when(step == 0)
    def _():
        c_scr[...] = jnp.zeros(c_scr.shape, jnp.float32)
        m_scr[...] = jnp.zeros(m_scr.shape, jnp.float32)

    r_i = lax.broadcasted_iota(jnp.int32, (L, L), 0)
    c_i = lax.broadcasted_iota(jnp.int32, (L, L), 1)
    causal = c_i <= r_i
    tril = causal.astype(jnp.float32)
    triu = (r_i <= c_i).astype(jnp.float32)
    lane = lax.broadcasted_iota(jnp.int32, (L, LANES), 1)
    low = lane < ML_DQK
    one_col = (lane == 0).astype(jnp.bfloat16)
    row2 = lax.broadcasted_iota(jnp.int32, (2 * ML_DQK, 1), 0)

    for ch in range(ML_STEP // L):
        r0 = ch * L
        for b in range(batch):
            g = g_ref[b, r0:r0 + L, :]
            gt = gt_ref[b, :, r0:r0 + L]
            bcols = jnp.dot(tril, g, precision=hi, preferred_element_type=jnp.float32)
            brows = jnp.dot(gt, triu, precision=hi, preferred_element_type=jnp.float32)
            for pair in range(pairs):
                sidx = b * pairs + pair
                qp = qk_ref[b, r0:r0 + L, pair * LANES:(pair + 1) * LANES]
                kp = qk_ref[b, r0:r0 + L, (pairs + pair) * LANES:(pairs + pair + 1) * LANES]
                c_pair = c_scr[sidx]
                c_bf = _bf16(c_pair)
                new_c = None
                decays = []
                for sub in range(2):
                    h = 2 * pair + sub
                    sel = low if sub == 0 else jnp.logical_not(low)
                    m_prev = m_scr[b * ML_HEADS + h]
                    bc = bcols[:, ML_HEADS + h:ML_HEADS + h + 1]
                    br = brows[ML_HEADS + h:ML_HEADS + h + 1, :]
                    lic = g[:, h:h + 1]
                    lir = gt[h:h + 1, :]
                    log_d = jnp.where(causal, bc - br + lir, NEG_BIG)
                    m_inter = bc + m_prev
                    m_row = jnp.maximum(m_inter, jnp.max(log_d, axis=-1, keepdims=True))
                    d = jnp.exp(log_d - m_row)
                    inter = jnp.exp(m_inter - m_row)
                    qm = jnp.where(sel, qp, jnp.zeros_like(qp))
                    s = lax.dot_general(qm, kp, (((1,), (1,)), ((), ())),
                                        preferred_element_type=jnp.float32) * d
                    v_ext = jnp.concatenate([v_ref[b, r0:r0 + L, h * ML_DV:(h + 1) * ML_DV], one_col],
                                            axis=-1)
                    num = inter * jnp.dot(qm, c_bf, preferred_element_type=jnp.float32) \
                        + jnp.dot(_bf16(s), v_ext, preferred_element_type=jnp.float32)
                    den = num[:, ML_DV:ML_DV + 1]
                    hval = num[:, 0:ML_DV] / jnp.maximum(jnp.abs(den), jnp.exp(-m_row))
                    hn = hval * lax.rsqrt(jnp.mean(hval * hval, axis=-1, keepdims=True) + EPS)
                    hn = hn * ng_ref[:, h * ML_DV:(h + 1) * ML_DV]
                    out_ref[b, r0:r0 + L, h * ML_DV:(h + 1) * ML_DV] = _bf16(
                        hn * o_ref[b, r0:r0 + L, h * ML_DV:(h + 1) * ML_DV].astype(jnp.float32))
                    b_last = bc[L - 1:L, :]
                    log_w = b_last - bc + lic
                    m_next = jnp.maximum(b_last + m_prev, jnp.max(log_w, axis=0, keepdims=True))
                    w = jnp.exp(log_w - m_next)
                    decays.append(jnp.exp(b_last + m_prev - m_next))
                    m_scr[b * ML_HEADS + h] = m_next
                    kw = _bf16(jnp.where(sel, kp.astype(jnp.float32) * w, 0.0))
                    upd = lax.dot_general(kw, v_ext, (((0,), (0,)), ((), ())),
                                          preferred_element_type=jnp.float32)
                    new_c = upd if new_c is None else new_c + upd
                decay_rows = jnp.where(row2 < ML_DQK, decays[0], decays[1])
                c_scr[sidx] = decay_rows * c_pair + new_c


def _mlstm(qkm, vm, om, gates, gates_t, ng, batch, seq_len):
    n_steps = seq_len // ML_STEP
    blk = lambda c: (0, c, 0)
    return pl.pallas_call(
        _mlstm_kernel,
        grid=(n_steps,),
        in_specs=[
            pl.BlockSpec((batch, ML_STEP, 2 * ML_HEADS * ML_DQK), blk),
            pl.BlockSpec((batch, ML_STEP, ML_WIDTH), blk),
            pl.BlockSpec((batch, ML_STEP, ML_WIDTH), blk),
            pl.BlockSpec((batch, ML_STEP, LANES), blk),
            pl.BlockSpec((batch, SUBLANES, ML_STEP), lambda c: (0, 0, c)),
            pl.BlockSpec((1, ML_WIDTH), lambda c: (0, 0)),
        ],
        out_specs=pl.BlockSpec((batch, ML_STEP, ML_WIDTH), blk),
        out_shape=jax.ShapeDtypeStruct((batch, seq_len, ML_WIDTH), jnp.bfloat16),
        scratch_shapes=[
            pltpu.VMEM((batch * ML_HEADS // 2, 2 * ML_DQK, 2 * ML_DV), jnp.float32),
            pltpu.VMEM((batch * ML_HEADS, 1, 1), jnp.float32),
        ],
        compiler_params=pltpu.CompilerParams(dimension_semantics=("arbitrary",),
                                             vmem_limit_bytes=VMEM_LIMIT),
        name="mlstm_scan",
    )(qkm, vm, om, gates, gates_t, ng)


def _out_proj_router_kernel(x_ref, att_ref, hm_ref, wo_ref, g_ref, wrt_ref, brt_ref,
                            x1_ref, t_ref, row_ref, col_ref, cnt_ref):
    tm = x_ref.shape[0]
    hi = lax.Precision.HIGHEST
    x1 = x_ref[...] \
        + jnp.dot(att_ref[...], wo_ref[0:ATT_WIDTH, :], preferred_element_type=jnp.float32) \
        + jnp.dot(hm_ref[...], wo_ref[ATT_WIDTH:, :], preferred_element_type=jnp.float32)
    x1_ref[...] = x1
    tn = x1 * lax.rsqrt(jnp.mean(x1 * x1, axis=-1, keepdims=True) + EPS) * g_ref[...]
    t_ref[...] = _bf16(tn)
    logits = lax.dot_general(wrt_ref[...], tn, (((1,), (1,)), ((), ())), precision=hi,
                             preferred_element_type=jnp.float32) + brt_ref[...]
    el_all = logits[0:N_EXPERTS, :]
    gl = logits[N_EXPERTS:N_EXPERTS + SUBLANES, :]
    grow = lax.broadcasted_iota(jnp.int32, gl.shape, 0).astype(jnp.float32)
    gl = jnp.where(grow < N_GROUPS, gl, NEG_BIG)
    gmax = jnp.max(gl, axis=0, keepdims=True)
    grp = jnp.min(jnp.where(gl == gmax, grow, float(N_GROUPS)), axis=0, keepdims=True)
    p_grp = 1.0 / jnp.sum(jnp.exp(gl - gmax), axis=0, keepdims=True)
    erow = lax.broadcasted_iota(jnp.int32, el_all.shape, 0).astype(jnp.float32)
    egrp = jnp.floor(erow * (1.0 / EXPERTS_PER_GROUP))
    el = jnp.where(egrp == grp, el_all, NEG_BIG)
    e1 = jnp.max(el, axis=0, keepdims=True)
    i1 = jnp.min(jnp.where(el == e1, erow, float(N_EXPERTS)), axis=0, keepdims=True)
    el2 = jnp.where(erow == i1, NEG_BIG, el)
    e2 = jnp.max(el2, axis=0, keepdims=True)
    i2 = jnp.min(jnp.where(el2 == e2, erow, float(N_EXPERTS)), axis=0, keepdims=True)
    z2 = jnp.exp(e2 - e1)
    w1 = p_grp / (1.0 + z2)
    w2 = p_grp * z2 / (1.0 + z2)
    sel1 = erow == i1
    sel2 = erow == i2
    onehot = jnp.logical_or(sel1, sel2)
    t_r = lax.broadcasted_iota(jnp.int32, (tm, tm), 0)
    t_c = lax.broadcasted_iota(jnp.int32, (tm, tm), 1)
    before = (t_r < t_c).astype(jnp.bfloat16)
    rank = jnp.dot(onehot.astype(jnp.bfloat16), before, preferred_element_type=jnp.float32)
    cnt = jnp.sum(onehot.astype(jnp.float32), axis=1, keepdims=True)
    cnt_al = jnp.floor((cnt + (SEG_ALIGN - 1)) * (1.0 / SEG_ALIGN)) * SEG_ALIGN
    e_r = lax.broadcasted_iota(jnp.int32, (N_EXPERTS, N_EXPERTS), 0)
    e_c = lax.broadcasted_iota(jnp.int32, (N_EXPERTS, N_EXPERTS), 1)
    lstart = jnp.dot((e_c < e_r).astype(jnp.float32), jnp.broadcast_to(cnt_al, (N_EXPERTS, LANES)),
                     precision=hi, preferred_element_type=jnp.float32)[:, 0:1]
    slot = lstart + rank
    pos1 = jnp.sum(jnp.where(sel1, slot, 0.0), axis=0, keepdims=True)
    pos2 = jnp.sum(jnp.where(sel2, slot, 0.0), axis=0, keepdims=True)
    r8 = lax.broadcasted_iota(jnp.int32, (SUBLANES, tm), 0)
    info = jnp.where(r8 == ROW_POS0, pos1, jnp.where(r8 == ROW_POS1, pos2, jnp.where(
        r8 == ROW_W0, w1, jnp.where(r8 == ROW_W1, w2, jnp.where(
            r8 == ROW_E0, i1, jnp.where(r8 == ROW_E1, i2, 0.0))))))
    row_ref[...] = info

    def split3(w):
        h = _bf16(w).astype(jnp.float32)
        m = _bf16(w - h).astype(jnp.float32)
        return h, m, _bf16(w - h - m).astype(jnp.float32)

    w1h, w1m, w1l = split3(w1)
    w2h, w2m, w2l = split3(w2)
    parts = jnp.where(r8 == 0, w1h, jnp.where(r8 == 1, w1m, jnp.where(r8 == 2, w1l, jnp.where(
        r8 == 3, w2h, jnp.where(r8 == 4, w2m, jnp.where(r8 == 5, w2l, jnp.where(r8 == 6, i1, i2)))))))
    col_ref[...] = jnp.concatenate(
        [info, parts, jnp.zeros((LANES - 2 * SUBLANES, tm), jnp.float32)], axis=0).T
    cnt_ref[...] = jnp.broadcast_to(cnt, (N_EXPERTS, LANES)).astype(jnp.int32)


def _out_proj_router(x2, att, hm, wo, g, wrt, brt):
    t = x2.shape[0]
    tm = TM_PROJ
    row = lambda i: (i, 0)
    fix = lambda i: (0, 0)
    return pl.pallas_call(
        _out_proj_router_kernel,
        grid=(t // tm,),
        in_specs=[
            pl.BlockSpec((tm, D_MODEL), row),
            pl.BlockSpec((tm, ATT_WIDTH), row),
            pl.BlockSpec((tm, ML_WIDTH), row),
            pl.BlockSpec((D_MODEL, D_MODEL), fix),
            pl.BlockSpec((1, D_MODEL), fix),
            pl.BlockSpec((4 * SUBLANES, D_MODEL), fix),
            pl.BlockSpec((4 * SUBLANES, 1), fix),
        ],
        out_specs=[
            pl.BlockSpec((tm, D_MODEL), row),
            pl.BlockSpec((tm, D_MODEL), row),
            pl.BlockSpec((SUBLANES, tm), lambda i: (0, i)),
            pl.BlockSpec((tm, LANES), row),
            pl.BlockSpec((None, N_EXPERTS, LANES), lambda i: (i, 0, 0)),
        ],
        out_shape=(
            jax.ShapeDtypeStruct((t, D_MODEL), jnp.float32),
            jax.ShapeDtypeStruct((t, D_MODEL), jnp.bfloat16),
            jax.ShapeDtypeStruct((SUBLANES, t), jnp.float32),
            jax.ShapeDtypeStruct((t, LANES), jnp.float32),
            jax.ShapeDtypeStruct((t // tm, N_EXPERTS, LANES), jnp.int32),
        ),
        compiler_params=pltpu.CompilerParams(dimension_semantics=("arbitrary",),
                                             vmem_limit_bytes=VMEM_LIMIT),
        name="out_proj_router",
    )(x2, att, hm, wo, g, wrt, brt)


def _chunk_rows(j):
    return pl.ds(pl.multiple_of(j * SEG_ALIGN, SEG_ALIGN), SEG_ALIGN)


def _dispatch_kernel(nch_ref, dst_ref, nz_ref, zdst_ref, t_ref, col_ref, row_ref, xs_hbm, sd_hbm,
                     xbuf, sbuf, zx, zs, sem_x, sem_s, sem_z):
    i = pl.program_id(0)
    n = pl.num_programs(0)
    p = i % 2
    tm = t_ref.shape[0]

    def copies(tile, par, j):
        d = pl.ds(pl.multiple_of(dst_ref[tile, j], SEG_ALIGN), SEG_ALIGN)
        return (pltpu.make_async_copy(xbuf.at[par, _chunk_rows(j)], xs_hbm.at[d], sem_x.at[par]),
                pltpu.make_async_copy(sbuf.at[par, _chunk_rows(j)], sd_hbm.at[d], sem_s.at[par]))

    def zero_copies(j):
        d = pl.ds(pl.multiple_of(zdst_ref[j], SEG_ALIGN), SEG_ALIGN)
        return (pltpu.make_async_copy(zx, xs_hbm.at[d], sem_z.at[0]),
                pltpu.make_async_copy(zs, sd_hbm.at[d], sem_z.at[1]))

    def wait_tile(tile, par):
        def body(j, c):
            cx, cs = copies(tile, par, j)
            cx.wait()
            cs.wait()
            return c
        lax.fori_loop(0, nch_ref[tile], body, 0)

    @pl.when(i == 0)
    def _():
        zx[...] = jnp.zeros(zx.shape, zx.dtype)
        zs[...] = jnp.zeros(zs.shape, zs.dtype)

        def body(j, c):
            cx, cs = zero_copies(j)
            cx.start()
            cs.start()
            return c
        lax.fori_loop(0, nz_ref[0], body, 0)

    @pl.when(i >= 2)
    def _():
        wait_tile(i - 2, p)

    pos0 = row_ref[ROW_POS0:ROW_POS0 + 1, :].astype(jnp.int32)
    pos1 = row_ref[ROW_POS1:ROW_POS1 + 1, :].astype(jnp.int32)
    t = t_ref[...]
    side = _bf16(col_ref[...])
    rows = DISPATCH_ROWS
    for c in range(L_CAP // rows):
        r = lax.broadcasted_iota(jnp.int32, (rows, tm), 0) + c * rows
        perm = jnp.logical_or(r == pos0, r == pos1).astype(jnp.bfloat16)
        xbuf[p, c * rows:(c + 1) * rows, :] = _bf16(
            jnp.dot(perm, t, preferred_element_type=jnp.float32))
        sbuf[p, c * rows:(c + 1) * rows, :] = jnp.dot(perm, side, preferred_element_type=jnp.float32)

    def start_body(j, c):
        cx, cs = copies(i, p, j)
        cx.start()
        cs.start()
        return c
    lax.fori_loop(0, nch_ref[i], start_body, 0)

    @pl.when(i == n - 1)
    def _():
        @pl.when(n >= 2)
        def _():
            wait_tile(i - 1, 1 - p)
        wait_tile(i, p)

        def body(j, c):
            cx, cs = zero_copies(j)
            cx.wait()
            cs.wait()
            return c
        lax.fori_loop(0, nz_ref[0], body, 0)


def _dispatch(t, col, row, nch, dst, nz, zdst):
    n_tok = t.shape[0]
    r_cap, _ = _moe_capacity(n_tok)
    tm = TM_PROJ
    grid_spec = pltpu.PrefetchScalarGridSpec(
        num_scalar_prefetch=4,
        grid=(n_tok // tm,),
        in_specs=[
            pl.BlockSpec((tm, D_MODEL), lambda i, *_: (i, 0)),
            pl.BlockSpec((tm, LANES), lambda i, *_: (i, 0)),
            pl.BlockSpec((SUBLANES, tm), lambda i, *_: (0, i)),
        ],
        out_specs=[pl.BlockSpec(memory_space=pl.ANY), pl.BlockSpec(memory_space=pl.ANY)],
        scratch_shapes=[
            pltpu.VMEM((2, L_CAP, D_MODEL), jnp.bfloat16),
            pltpu.VMEM((2, L_CAP, LANES), jnp.float32),
            pltpu.VMEM((SEG_ALIGN, D_MODEL), jnp.bfloat16),
            pltpu.VMEM((SEG_ALIGN, LANES), jnp.float32),
            pltpu.SemaphoreType.DMA((2,)),
            pltpu.SemaphoreType.DMA((2,)),
            pltpu.SemaphoreType.DMA((2,)),
        ],
    )
    return pl.pallas_call(
        _dispatch_kernel,
        grid_spec=grid_spec,
        out_shape=(jax.ShapeDtypeStruct((r_cap, D_MODEL), jnp.bfloat16),
                   jax.ShapeDtypeStruct((r_cap, LANES), jnp.float32)),
        compiler_params=pltpu.CompilerParams(dimension_semantics=("arbitrary",),
                                             vmem_limit_bytes=VMEM_LIMIT),
        name="moe_dispatch",
    )(nch, dst, nz, zdst, t, col, row)


def _moe_kernel(blk_e_ref, nused_ref, xs_ref, sd_ref, wg_ref, wu_ref, wd_ref, ys_ref):
    b = pl.program_id(0)

    @pl.when(b < nused_ref[0])
    def _():
        x = xs_ref[...]
        a = jnp.dot(x, wg_ref[0], preferred_element_type=jnp.float32)
        u = jnp.dot(x, wu_ref[0], preferred_element_type=jnp.float32)
        h = a * jax.nn.sigmoid(a) * u
        y = jnp.dot(_bf16(h), wd_ref[0], preferred_element_type=jnp.float32)
        sd = sd_ref[...]
        e_blk = blk_e_ref[b].astype(jnp.float32)
        w0 = sd[:, COL_W0H:COL_W0H + 1] + sd[:, COL_W0H + 1:COL_W0H + 2] + sd[:, COL_W0H + 2:COL_W0H + 3]
        w1 = sd[:, COL_W1H:COL_W1H + 1] + sd[:, COL_W1H + 1:COL_W1H + 2] + sd[:, COL_W1H + 2:COL_W1H + 3]
        w = jnp.where(sd[:, COL_E0:COL_E0 + 1] == e_blk, w0, w1)
        ys_ref[...] = _bf16(y * w)


def _moe(xs, sd, blk_e, nused, wg, wu, wd):
    blk = lambda b, be, nu: (jnp.maximum(jnp.minimum(b, nu[0] - 1), 0), 0)
    wsel = lambda b, be, nu: (be[b], 0, 0)
    r_cap = xs.shape[0]
    grid_spec = pltpu.PrefetchScalarGridSpec(
        num_scalar_prefetch=2,
        grid=(r_cap // MOE_BM,),
        in_specs=[
            pl.BlockSpec((MOE_BM, D_MODEL), blk),
            pl.BlockSpec((MOE_BM, LANES), blk),
            pl.BlockSpec((1, D_MODEL, D_FF_EXPERT), wsel),
            pl.BlockSpec((1, D_MODEL, D_FF_EXPERT), wsel),
            pl.BlockSpec((1, D_FF_EXPERT, D_MODEL), wsel),
        ],
        out_specs=pl.BlockSpec((MOE_BM, D_MODEL), blk),
    )
    return pl.pallas_call(
        _moe_kernel,
        grid_spec=grid_spec,
        out_shape=jax.ShapeDtypeStruct((r_cap, D_MODEL), jnp.bfloat16),
        compiler_params=pltpu.CompilerParams(dimension_semantics=("arbitrary",),
                                             vmem_limit_bytes=VMEM_LIMIT),
        name="moe_experts",
    )(blk_e, nused, xs, sd, wg, wu, wd)


def _combine_kernel(nch_ref, dst_ref, x1_ref, col_ref, ys_hbm, out_ref, ybuf, sem):
    i = pl.program_id(0)
    n = pl.num_programs(0)
    p = i % 2
    tm = x1_ref.shape[0]

    def copy(tile, par, j):
        s = pl.ds(pl.multiple_of(dst_ref[tile, j], SEG_ALIGN), SEG_ALIGN)
        return pltpu.make_async_copy(ys_hbm.at[s], ybuf.at[par, _chunk_rows(j)], sem.at[par])

    def start_tile(tile, par):
        def body(j, c):
            copy(tile, par, j).start()
            return c
        lax.fori_loop(0, nch_ref[tile], body, 0)

    @pl.when(i == 0)
    def _():
        ybuf[...] = jnp.zeros(ybuf.shape, ybuf.dtype)
        start_tile(0, 0)

    @pl.when(i + 1 < n)
    def _():
        start_tile(i + 1, 1 - p)

    def wait_body(j, c):
        copy(i, p, j).wait()
        return c
    lax.fori_loop(0, nch_ref[i], wait_body, 0)

    rows = COMBINE_ROWS
    for c in range(tm // rows):
        col = col_ref[c * rows:(c + 1) * rows, :]
        pos0 = col[:, ROW_POS0:ROW_POS0 + 1].astype(jnp.int32)
        pos1 = col[:, ROW_POS1:ROW_POS1 + 1].astype(jnp.int32)
        l = lax.broadcasted_iota(jnp.int32, (rows, L_CAP), 1)
        perm = jnp.logical_or(l == pos0, l == pos1).astype(jnp.bfloat16)
        y = jnp.dot(perm, ybuf[p], preferred_element_type=jnp.float32)
        out_ref[c * rows:(c + 1) * rows, :] = x1_ref[c * rows:(c + 1) * rows, :] + y


def _combine(x1, col, ys, nch, dst):
    n_tok = x1.shape[0]
    tm = TM_PROJ
    grid_spec = pltpu.PrefetchScalarGridSpec(
        num_scalar_prefetch=2,
        grid=(n_tok // tm,),
        in_specs=[
            pl.BlockSpec((tm, D_MODEL), lambda i, *_: (i, 0)),
            pl.BlockSpec((tm, LANES), lambda i, *_: (i, 0)),
            pl.BlockSpec(memory_space=pl.ANY),
        ],
        out_specs=pl.BlockSpec((tm, D_MODEL), lambda i, *_: (i, 0)),
        scratch_shapes=[
            pltpu.VMEM((2, L_CAP, D_MODEL), jnp.bfloat16),
            pltpu.SemaphoreType.DMA((2,)),
        ],
    )
    return pl.pallas_call(
        _combine_kernel,
        grid_spec=grid_spec,
        out_shape=jax.ShapeDtypeStruct((n_tok, D_MODEL), jnp.float32),
        compiler_params=pltpu.CompilerParams(dimension_semantics=("arbitrary",),
                                             vmem_limit_bytes=VMEM_LIMIT),
        name="moe_combine",
    )(nch, dst, x1, col, ys)


def _routing_tables(counts, n_blk_cap):
    i32 = jnp.int32
    ca = ((counts + (SEG_ALIGN - 1)) // SEG_ALIGN) * SEG_ALIGN
    lend = jnp.cumsum(ca, axis=1)
    lstart = lend - ca
    nch = (lend[:, -1] // SEG_ALIGN).astype(i32)
    tot = jnp.sum(ca, axis=0)
    region = ((tot + (MOE_BM - 1)) // MOE_BM) * MOE_BM
    rend = jnp.cumsum(region)
    base = rend - region
    gstart = base[None, :] + jnp.cumsum(ca, axis=0) - ca
    j16 = jnp.arange(N_CHUNK, dtype=i32) * SEG_ALIGN
    e_of_j = jnp.minimum(jnp.sum(lend[:, None, :] <= j16[None, :, None], axis=2), N_EXPERTS - 1)
    dst = (jnp.take_along_axis(gstart - lstart, e_of_j, axis=1) + j16[None, :]).astype(i32)
    dst = jnp.where(j16[None, :] < lend[:, -1:], dst, 0)
    nused = (rend[-1] // MOE_BM).astype(i32)
    brow = jnp.arange(n_blk_cap, dtype=i32) * MOE_BM
    blk_e = jnp.minimum(jnp.sum(rend[None, :] <= brow[:, None], axis=1), N_EXPERTS - 1)
    blk_e = jnp.where(jnp.arange(n_blk_cap) < nused, blk_e, blk_e[nused - 1]).astype(i32)
    k16 = jnp.arange(MOE_BM // SEG_ALIGN - 1, dtype=i32) * SEG_ALIGN
    zrow = (base + tot)[:, None] + k16[None, :]
    zvalid = zrow < rend[:, None]
    order = jnp.argsort(jnp.logical_not(zvalid).reshape(-1), stable=True)
    zdst = jnp.where(zvalid, zrow, 0).reshape(-1)[order].astype(i32)
    nz = jnp.sum(zvalid).astype(i32)
    return nch, dst, nz[None], zdst, blk_e, nused[None]


def _t5_bucket_np(dist):
    max_exact = N_BUCKETS // 2
    d = np.maximum(dist, 1).astype(np.float32)
    large = max_exact + (np.log(d / max_exact) / math.log(MAX_DISTANCE / max_exact)
                         * (N_BUCKETS - max_exact)).astype(np.int32)
    large = np.minimum(large, N_BUCKETS - 1)
    return np.where(dist < max_exact, dist, large)


def _attention_bias(rel_bias):
    qi = np.arange(ATT_BLOCK)[:, None]
    kj = np.arange(2 * ATT_BLOCK)[None, :]
    dist = qi + ATT_BLOCK - kj
    in_window = (dist >= 0) & (dist < WINDOW)
    bucket = _t5_bucket_np(np.clip(dist, 0, WINDOW - 1))
    onehot = (bucket[None] == np.arange(N_BUCKETS)[:, None, None]).astype(np.float32)
    bias = jnp.einsum('nh,nqk->hqk', rel_bias.astype(jnp.float32), jnp.asarray(onehot),
                      precision=lax.Precision.HIGHEST)
    return jnp.where(jnp.asarray(in_window)[None], bias, NEG_BIG)


def _block_diag_mean(width, block):
    idx = np.arange(width) // block
    return jnp.asarray((idx[:, None] == idx[None, :]).astype(np.float32) / block, dtype=jnp.bfloat16)


def _pack_w_in(w):
    hd = ATT_HEAD_DIM
    o = 0
    q = w[:, o:o + ATT_WIDTH]; o += ATT_WIDTH
    k = w[:, o:o + ATT_KV_HEADS * hd]; o += ATT_KV_HEADS * hd
    v = w[:, o:o + ATT_KV_HEADS * hd]; o += ATT_KV_HEADS * hd
    qm = w[:, o:o + ML_HEADS * ML_DQK]; o += ML_HEADS * ML_DQK
    km = w[:, o:o + ML_HEADS * ML_DQK]; o += ML_HEADS * ML_DQK
    vm = w[:, o:o + ML_WIDTH]; o += ML_WIDTH
    om = w[:, o:o + ML_WIDTH]; o += ML_WIDTH
    gates = w[:, o:o + 2 * ML_HEADS]
    dup = lambda a: jnp.concatenate([a[:, 0:hd], a[:, 0:hd], a[:, hd:2 * hd], a[:, hd:2 * hd]], axis=1)
    gpad = jnp.pad(gates, ((0, 0), (0, LANES - 2 * ML_HEADS)))
    packed = jnp.concatenate([q, dup(k), dup(v), qm, km, vm, om, gpad], axis=1)
    return _bf16(packed), _bf16(gates.T)


def kernel(x, rel_bias, norm_mix_g, w_in, q_norm_g, k_norm_g, attn_sink, conv_w, conv_b, gate_b,
           mlstm_norm_g, w_out, norm_ffn_g, w_router_group, b_router_group, w_router_expert,
           b_router_expert, w_gate, w_up, w_down):
    batch, seq_len, _ = x.shape
    n_tok = batch * seq_len
    assert seq_len % TM_PROJ == 0 and seq_len % ML_STEP == 0
    f32 = jnp.float32
    bias = _attention_bias(rel_bias)
    bdq = _block_diag_mean(ATT_WIDTH, ATT_HEAD_DIM)
    bdk = _block_diag_mean(2 * LANES, ATT_HEAD_DIM)
    x2 = x.reshape(n_tok, D_MODEL)
    for l in range(DEPTH):
        w_pack, wgt = _pack_w_in(w_in[l])
        qg = (jnp.tile(q_norm_g[l].astype(f32), ATT_HEADS) * (ATT_HEAD_DIM ** -0.5))[None, :]
        kg = jnp.tile(k_norm_g[l].astype(f32), 2 * ATT_KV_HEADS)[None, :]
        gb = jnp.pad(gate_b[l].astype(f32), (0, LANES - 2 * ML_HEADS))[None, :]
        gbt = gate_b[l].astype(f32)[:, None]
        qn, kn, vd, qkm, vm, om, gates, gates_t = _in_proj(
            x2, norm_mix_g[l][None, :], w_pack, wgt, qg, kg, bdq, bdk,
            conv_w[l], conv_b[l][None, :], gb, gbt, batch, seq_len)
        att = _attention(qn, kn, vd, bias, attn_sink[l].astype(f32), batch, seq_len)
        r3 = lambda a: a.reshape(batch, seq_len, a.shape[-1])
        hm = _mlstm(r3(qkm), r3(vm), r3(om), r3(gates), gates_t,
                    mlstm_norm_g[l][None, :], batch, seq_len)
        n_rt = 4 * SUBLANES
        wrt = jnp.pad(jnp.concatenate([w_router_expert[l], w_router_group[l]], axis=1).astype(f32).T,
                      ((0, n_rt - N_EXPERTS - N_GROUPS), (0, 0)))
        brt = jnp.pad(jnp.concatenate([b_router_expert[l], b_router_group[l]]).astype(f32),
                      (0, n_rt - N_EXPERTS - N_GROUPS))[:, None]
        x1, t, row, col, cnt = _out_proj_router(x2, att, hm.reshape(n_tok, ML_WIDTH), _bf16(w_out[l]),
                                                norm_ffn_g[l][None, :], wrt, brt)
        _, n_blk_cap = _moe_capacity(n_tok)
        nch, dst, nz, zdst, blk_e, nused = _routing_tables(cnt[:, :, 0], n_blk_cap)
        xs, sd = _dispatch(t, col, row, nch, dst, nz, zdst)
        ys = _moe(xs, sd, blk_e, nused, _bf16(w_gate[l]), _bf16(w_up[l]), _bf16(w_down[l]))
        x2 = _combine(x1, col, ys, nch, dst)
    return x2.reshape(batch, seq_len, D_MODEL)
```

```python
import functools
import math

import jax
import jax.numpy as jnp
import numpy as np
from jax import lax
from jax.experimental import pallas as pl
from jax.experimental.pallas import tpu as pltpu

D_MODEL = 1024
DEPTH = 2
ATT_HEADS = 8
ATT_KV_HEADS = 2
ATT_HEAD_DIM = 64
ATT_WIDTH = ATT_HEADS * ATT_HEAD_DIM
WINDOW = 128
ATT_BLOCK = 128
N_BUCKETS = 32
MAX_DISTANCE = 128
ML_HEADS = 4
ML_DQK = 64
ML_DV = 128
ML_WIDTH = ML_HEADS * ML_DV
ML_CHUNK = 64
CONV_K = 4
N_GROUPS = 4
EXPERTS_PER_GROUP = 4
N_EXPERTS = N_GROUPS * EXPERTS_PER_GROUP
D_FF_EXPERT = 512
EPS = 1e-6

LANES = 128
SUBLANES = 8
NEG_BIG = -1e30
VMEM_LIMIT = 48 * 1024 * 1024

C_Q = 0
C_K = C_Q + ATT_WIDTH
C_V = C_K + 2 * LANES
C_QKM = C_V + 2 * LANES
C_VM = C_QKM + 2 * ML_HEADS * ML_DQK
C_OM = C_VM + ML_WIDTH
C_G = C_OM + ML_WIDTH
N_PACK = C_G + LANES

TM_IN = 1024
TM_PROJ = 512
ML_KCHUNK = 256
ML_STEP = ML_KCHUNK

SEG_ALIGN = 16
MOE_BM = 512
L_CAP = 2 * TM_PROJ + N_EXPERTS * SEG_ALIGN
N_CHUNK = L_CAP // SEG_ALIGN
DISPATCH_ROWS = 256
COMBINE_ROWS = 256
ROW_POS0, ROW_POS1, ROW_W0, ROW_W1, ROW_E0, ROW_E1 = 0, 1, 2, 3, 4, 5
COL_W0H, COL_W1H, COL_E0, COL_E1 = 8, 11, 14, 15


def _moe_capacity(n_tok):
    n_tiles = n_tok // TM_PROJ
    rows = 2 * n_tok + n_tiles * N_EXPERTS * (SEG_ALIGN - 1) + N_EXPERTS * (MOE_BM - SEG_ALIGN)
    n_blk = -(-rows // MOE_BM)
    return n_blk * MOE_BM, n_blk


def _bf16(a):
    return a.astype(jnp.bfloat16)


def _log_sigmoid(z):
    return jnp.minimum(z, 0.0) - jnp.log(1.0 + jnp.exp(-jnp.abs(z)))


def _in_proj_kernel(tiles_per_seq, x_ref, g_ref, w_ref, wgt_ref, qg_ref, kg_ref, bdq_ref, bdk_ref,
                    cw_ref, cb_ref, gb_ref, gbt_ref,
                    qn_ref, kn_ref, vd_ref, qkm_ref, vm_ref, om_ref, gate_ref, gatet_ref,
                    conv_scr):
    i = pl.program_id(0)
    tm = x_ref.shape[0]
    x = x_ref[...]
    hn = x * lax.rsqrt(jnp.mean(x * x, axis=-1, keepdims=True) + EPS) * g_ref[...]
    hb = _bf16(hn)

    def proj(c0, width):
        return jnp.dot(hb, w_ref[:, c0:c0 + width], preferred_element_type=jnp.float32)

    q = proj(C_Q, ATT_WIDTH)
    q_ms = jnp.dot(_bf16(q * q), bdq_ref[...], preferred_element_type=jnp.float32)
    qn_ref[...] = _bf16(q * lax.rsqrt(q_ms + EPS) * qg_ref[...])
    k = proj(C_K, 2 * LANES)
    k_ms = jnp.dot(_bf16(k * k), bdk_ref[...], preferred_element_type=jnp.float32)
    kn_ref[...] = _bf16(k * lax.rsqrt(k_ms + EPS) * kg_ref[...])
    vd_ref[...] = _bf16(proj(C_V, 2 * LANES))

    qk = proj(C_QKM, 2 * ML_HEADS * ML_DQK)

    @pl.when(i % tiles_per_seq == 0)
    def _():
        conv_scr[0:SUBLANES, :] = jnp.zeros((SUBLANES, qk.shape[1]), jnp.float32)

    @pl.when(i % tiles_per_seq != 0)
    def _():
        conv_scr[0:SUBLANES, :] = conv_scr[tm:tm + SUBLANES, :]

    conv_scr[SUBLANES:SUBLANES + tm, :] = qk
    y = qk * cw_ref[CONV_K - 1:CONV_K, :] + cb_ref[...]
    for j in range(CONV_K - 1):
        off = SUBLANES - (CONV_K - 1) + j
        y = y + conv_scr[off:off + tm, :] * cw_ref[j:j + 1, :]
    y = y * jax.nn.sigmoid(y)
    lane = lax.broadcasted_iota(jnp.int32, y.shape, 1)
    y = jnp.where(lane >= ML_HEADS * ML_DQK, y * (ML_DQK ** -0.5), y)
    qkm_ref[...] = _bf16(y)

    vm_ref[...] = _bf16(proj(C_VM, ML_WIDTH))
    om_ref[...] = _bf16(jax.nn.sigmoid(proj(C_OM, ML_WIDTH)))

    gp = proj(C_G, LANES) + gb_ref[...]
    glane = lax.broadcasted_iota(jnp.int32, gp.shape, 1)
    gate_ref[...] = jnp.where(glane >= ML_HEADS, _log_sigmoid(gp), gp)
    gt = lax.dot_general(wgt_ref[...], hb, (((1,), (1,)), ((), ())),
                         preferred_element_type=jnp.float32) + gbt_ref[...]
    grow = lax.broadcasted_iota(jnp.int32, gt.shape, 0)
    gatet_ref[...] = jnp.where(grow >= ML_HEADS, _log_sigmoid(gt), gt)


def _in_proj(x2, g, w_pack, wgt, qg, kg, bdq, bdk, cw, cb, gb, gbt, batch, seq_len):
    t = x2.shape[0]
    tm = TM_IN
    n = t // tm
    tps = seq_len // tm
    row = lambda i: (i, 0)
    fix = lambda i: (0, 0)
    out_shapes = (
        jax.ShapeDtypeStruct((t, ATT_WIDTH), jnp.bfloat16),
        jax.ShapeDtypeStruct((t, 2 * LANES), jnp.bfloat16),
        jax.ShapeDtypeStruct((t, 2 * LANES), jnp.bfloat16),
        jax.ShapeDtypeStruct((t, 2 * ML_HEADS * ML_DQK), jnp.bfloat16),
        jax.ShapeDtypeStruct((t, ML_WIDTH), jnp.bfloat16),
        jax.ShapeDtypeStruct((t, ML_WIDTH), jnp.bfloat16),
        jax.ShapeDtypeStruct((t, LANES), jnp.float32),
        jax.ShapeDtypeStruct((batch, SUBLANES, seq_len), jnp.float32),
    )
    in_specs = [
        pl.BlockSpec((tm, D_MODEL), row),
        pl.BlockSpec((1, D_MODEL), fix),
        pl.BlockSpec((D_MODEL, N_PACK), fix),
        pl.BlockSpec((SUBLANES, D_MODEL), fix),
        pl.BlockSpec((1, ATT_WIDTH), fix),
        pl.BlockSpec((1, 2 * LANES), fix),
        pl.BlockSpec((ATT_WIDTH, ATT_WIDTH), fix),
        pl.BlockSpec((2 * LANES, 2 * LANES), fix),
        pl.BlockSpec((CONV_K, 2 * ML_HEADS * ML_DQK), fix),
        pl.BlockSpec((1, 2 * ML_HEADS * ML_DQK), fix),
        pl.BlockSpec((1, LANES), fix),
        pl.BlockSpec((SUBLANES, 1), fix),
    ]
    out_specs = [
        pl.BlockSpec((tm, ATT_WIDTH), row),
        pl.BlockSpec((tm, 2 * LANES), row),
        pl.BlockSpec((tm, 2 * LANES), row),
        pl.BlockSpec((tm, 2 * ML_HEADS * ML_DQK), row),
        pl.BlockSpec((tm, ML_WIDTH), row),
        pl.BlockSpec((tm, ML_WIDTH), row),
        pl.BlockSpec((tm, LANES), row),
        pl.BlockSpec((None, SUBLANES, tm), lambda i: (i // tps, 0, i % tps)),
    ]
    return pl.pallas_call(
        functools.partial(_in_proj_kernel, seq_len // tm),
        grid=(n,),
        in_specs=in_specs,
        out_specs=out_specs,
        out_shape=out_shapes,
        scratch_shapes=[pltpu.VMEM((tm + 2 * SUBLANES, 2 * ML_HEADS * ML_DQK), jnp.float32)],
        compiler_params=pltpu.CompilerParams(dimension_semantics=("arbitrary",),
                                             vmem_limit_bytes=VMEM_LIMIT),
        name="in_proj",
    )(x2, g, w_pack, wgt, qg, kg, bdq, bdk, cw, cb, gb, gbt)


def _attn_kernel(sink_ref, q_ref, kp_ref, kc_ref, vp_ref, vc_ref, bias_ref, o_ref):
    i = pl.program_id(1)
    blk = q_ref.shape[0]
    lane = lax.broadcasted_iota(jnp.int32, (blk, LANES), 1)
    low = lane < ATT_HEAD_DIM
    col = lax.broadcasted_iota(jnp.int32, (blk, 2 * blk), 1)
    no_prev = jnp.logical_and(col < blk, i == 0)
    group = ATT_HEADS // ATT_KV_HEADS
    for pair in range(ATT_HEADS // 2):
        kv = (2 * pair) // group
        qp = q_ref[:, pair * LANES:(pair + 1) * LANES]
        kcat = jnp.concatenate([kp_ref[:, kv * LANES:(kv + 1) * LANES],
                                kc_ref[:, kv * LANES:(kv + 1) * LANES]], axis=0)
        vcat = jnp.concatenate([vp_ref[:, kv * LANES:(kv + 1) * LANES],
                                vc_ref[:, kv * LANES:(kv + 1) * LANES]], axis=0)
        halves = []
        for sub in range(2):
            h = 2 * pair + sub
            qm = jnp.where(low if sub == 0 else jnp.logical_not(low), qp, jnp.zeros_like(qp))
            s = lax.dot_general(qm, kcat, (((1,), (1,)), ((), ())),
                                preferred_element_type=jnp.float32)
            logits = jnp.where(no_prev, NEG_BIG, s + bias_ref[h])
            sink = sink_ref[h]
            m = jnp.maximum(jnp.max(logits, axis=-1, keepdims=True), sink)
            p = jnp.exp(logits - m)
            den = jnp.sum(p, axis=-1, keepdims=True) + jnp.exp(sink - m)
            o = jnp.dot(_bf16(p), vcat, preferred_element_type=jnp.float32)
            halves.append(o / den)
        o_ref[:, pair * LANES:(pair + 1) * LANES] = _bf16(jnp.where(low, halves[0], halves[1]))


def _attention(qn, kn, vd, bias, sink, batch, seq_len):
    nb = seq_len // ATT_BLOCK
    cur = lambda b, i, s: (b * nb + i, 0)
    prev = lambda b, i, s: (b * nb + jnp.maximum(i - 1, 0), 0)
    grid_spec = pltpu.PrefetchScalarGridSpec(
        num_scalar_prefetch=1,
        grid=(batch, nb),
        in_specs=[
            pl.BlockSpec((ATT_BLOCK, ATT_WIDTH), cur),
            pl.BlockSpec((ATT_BLOCK, 2 * LANES), prev),
            pl.BlockSpec((ATT_BLOCK, 2 * LANES), cur),
            pl.BlockSpec((ATT_BLOCK, 2 * LANES), prev),
            pl.BlockSpec((ATT_BLOCK, 2 * LANES), cur),
            pl.BlockSpec((ATT_HEADS, ATT_BLOCK, 2 * ATT_BLOCK), lambda b, i, s: (0, 0, 0)),
        ],
        out_specs=pl.BlockSpec((ATT_BLOCK, ATT_WIDTH), cur),
    )
    return pl.pallas_call(
        _attn_kernel,
        grid_spec=grid_spec,
        out_shape=jax.ShapeDtypeStruct((batch * seq_len, ATT_WIDTH), jnp.bfloat16),
        compiler_params=pltpu.CompilerParams(dimension_semantics=("arbitrary", "arbitrary"),
                                             vmem_limit_bytes=VMEM_LIMIT),
        name="swa_attention",
    )(sink, qn, kn, kn, vd, vd, bias)


def _mlstm_kernel(qk_ref, v_ref, o_ref, g_ref, gt_ref, ng_ref, out_ref, c_scr, m_scr):
    step = pl.program_id(0)
    batch = qk_ref.shape[0]
    L = ML_KCHUNK
    pairs = ML_HEADS // 2
    hi = lax.Precision.HIGHEST

    @pl.when(step == 0)
    def _():
        c_scr[...] = jnp.zeros(c_scr.shape, jnp.float32)
        m_scr[...] = jnp.zeros(m_scr.shape, jnp.float32)

    r_i = lax.broadcasted_iota(jnp.int32, (L, L), 0)
    c_i = lax.broadcasted_iota(jnp.int32, (L, L), 1)
    causal = c_i <= r_i
    tril = causal.astype(jnp.float32)
    triu = (r_i <= c_i).astype(jnp.float32)
    lane = lax.broadcasted_iota(jnp.int32, (L, LANES), 1)
    low = lane < ML_DQK
    one_col = (lane == 0).astype(jnp.bfloat16)
    row2 = lax.broadcasted_iota(jnp.int32, (2 * ML_DQK, 1), 0)

    for ch in range(ML_STEP // L):
        r0 = ch * L
        for b in range(batch):
            g = g_ref[b, r0:r0 + L, :]
            gt = gt_ref[b, :, r0:r0 + L]
            bcols = jnp.dot(tril, g, precision=hi, preferred_element_type=jnp.float32)
            brows = jnp.dot(gt, triu, precision=hi, preferred_element_type=jnp.float32)
            for pair in range(pairs):
                sidx = b * pairs + pair
                qp = qk_ref[b, r0:r0 + L, pair * LANES:(pair + 1) * LANES]
                kp = qk_ref[b, r0:r0 + L, (pairs + pair) * LANES:(pairs + pair + 1) * LANES]
                c_pair = c_scr[sidx]
                c_bf = _bf16(c_pair)
                new_c = None
                decays = []
                for sub in range(2):
                    h = 2 * pair + sub
                    sel = low if sub == 0 else jnp.logical_not(low)
                    m_prev = m_scr[b * ML_HEADS + h]
                    bc = bcols[:, ML_HEADS + h:ML_HEADS + h + 1]
                    br = brows[ML_HEADS + h:ML_HEADS + h + 1, :]
                    lic = g[:, h:h + 1]
                    lir = gt[h:h + 1, :]
                    log_d = jnp.where(causal, bc - br + lir, NEG_BIG)
                    m_inter = bc + m_prev
                    m_row = jnp.maximum(m_inter, jnp.max(log_d, axis=-1, keepdims=True))
                    d = jnp.exp(log_d - m_row)
                    inter = jnp.exp(m_inter - m_row)
                    qm = jnp.where(sel, qp, jnp.zeros_like(qp))
                    s = lax.dot_general(qm, kp, (((1,), (1,)), ((), ())),
                                        preferred_element_type=jnp.float32) * d
                    v_ext = jnp.concatenate([v_ref[b, r0:r0 + L, h * ML_DV:(h + 1) * ML_DV], one_col],
                                            axis=-1)
                    num = inter * jnp.dot(qm, c_bf, preferred_element_type=jnp.float32) \
                        + jnp.dot(_bf16(s), v_ext, preferred_element_type=jnp.float32)
                    den = num[:, ML_DV:ML_DV + 1]
                    hval = num[:, 0:ML_DV] / jnp.maximum(jnp.abs(den), jnp.exp(-m_row))
                    hn = hval * lax.rsqrt(jnp.mean(hval * hval, axis=-1, keepdims=True) + EPS)
                    hn = hn * ng_ref[:, h * ML_DV:(h + 1) * ML_DV]
                    out_ref[b, r0:r0 + L, h * ML_DV:(h + 1) * ML_DV] = _bf16(
                        hn * o_ref[b, r0:r0 + L, h * ML_DV:(h + 1) * ML_DV].astype(jnp.float32))
                    b_last = bc[L - 1:L, :]
                    log_w = b_last - bc + lic
                    m_next = jnp.maximum(b_last + m_prev, jnp.max(log_w, axis=0, keepdims=True))
                    w = jnp.exp(log_w - m_next)
                    decays.append(jnp.exp(b_last + m_prev - m_next))
                    m_scr[b * ML_HEADS + h] = m_next
                    kw = _bf16(jnp.where(sel, kp.astype(jnp.float32) * w, 0.0))
                    upd = lax.dot_general(kw, v_ext, (((0,), (0,)), ((), ())),
                                          preferred_element_type=jnp.float32)
                    new_c = upd if new_c is None else new_c + upd
                decay_rows = jnp.where(row2 < ML_DQK, decays[0], decays[1])
                c_scr[sidx] = decay_rows * c_pair + new_c


def _mlstm(qkm, vm, om, gates, gates_t, ng, batch, seq_len):
    n_steps = seq_len // ML_STEP
    blk = lambda c: (0, c, 0)
    return pl.pallas_call(
        _mlstm_kernel,
        grid=(n_steps,),
        in_specs=[
            pl.BlockSpec((batch, ML_STEP, 2 * ML_HEADS * ML_DQK), blk),
            pl.BlockSpec((batch, ML_STEP, ML_WIDTH), blk),
            pl.BlockSpec((batch, ML_STEP, ML_WIDTH), blk),
            pl.BlockSpec((batch, ML_STEP, LANES), blk),
            pl.BlockSpec((batch, SUBLANES, ML_STEP), lambda c: (0, 0, c)),
            pl.BlockSpec((1, ML_WIDTH), lambda c: (0, 0)),
        ],
        out_specs=pl.BlockSpec((batch, ML_STEP, ML_WIDTH), blk),
        out_shape=jax.ShapeDtypeStruct((batch, seq_len, ML_WIDTH), jnp.bfloat16),
        scratch_shapes=[
            pltpu.VMEM((batch * ML_HEADS // 2, 2 * ML_DQK, 2 * ML_DV), jnp.float32),
            pltpu.VMEM((batch * ML_HEADS, 1, 1), jnp.float32),
        ],
        compiler_params=pltpu.CompilerParams(dimension_semantics=("arbitrary",),
                                             vmem_limit_bytes=VMEM_LIMIT),
        name="mlstm_scan",
    )(qkm, vm, om, gates, gates_t, ng)


def _out_proj_router_kernel(x_ref, att_ref, hm_ref, wo_ref, g_ref, wrt_ref, brt_ref,
                            x1_ref, t_ref, row_ref, col_ref, cnt_ref):
    tm = x_ref.shape[0]
    hi = lax.Precision.HIGHEST
    x1 = x_ref[...] \
        + jnp.dot(att_ref[...], wo_ref[0:ATT_WIDTH, :], preferred_element_type=jnp.float32) \
        + jnp.dot(hm_ref[...], wo_ref[ATT_WIDTH:, :], preferred_element_type=jnp.float32)
    x1_ref[...] = x1
    tn = x1 * lax.rsqrt(jnp.mean(x1 * x1, axis=-1, keepdims=True) + EPS) * g_ref[...]
    t_ref[...] = _bf16(tn)
    logits = lax.dot_general(wrt_ref[...], tn, (((1,), (1,)), ((), ())), precision=hi,
                             preferred_element_type=jnp.float32) + brt_ref[...]
    el_all = logits[0:N_EXPERTS, :]
    gl = logits[N_EXPERTS:N_EXPERTS + SUBLANES, :]
    grow = lax.broadcasted_iota(jnp.int32, gl.shape, 0).astype(jnp.float32)
    gl = jnp.where(grow < N_GROUPS, gl, NEG_BIG)
    gmax = jnp.max(gl, axis=0, keepdims=True)
    grp = jnp.min(jnp.where(gl == gmax, grow, float(N_GROUPS)), axis=0, keepdims=True)
    p_grp = 1.0 / jnp.sum(jnp.exp(gl - gmax), axis=0, keepdims=True)
    erow = lax.broadcasted_iota(jnp.int32, el_all.shape, 0).astype(jnp.float32)
    egrp = jnp.floor(erow * (1.0 / EXPERTS_PER_GROUP))
    el = jnp.where(egrp == grp, el_all, NEG_BIG)
    e1 = jnp.max(el, axis=0, keepdims=True)
    i1 = jnp.min(jnp.where(el == e1, erow, float(N_EXPERTS)), axis=0, keepdims=True)
    el2 = jnp.where(erow == i1, NEG_BIG, el)
    e2 = jnp.max(el2, axis=0, keepdims=True)
    i2 = jnp.min(jnp.where(el2 == e2, erow, float(N_EXPERTS)), axis=0, keepdims=True)
    z2 = jnp.exp(e2 - e1)
    w1 = p_grp / (1.0 + z2)
    w2 = p_grp * z2 / (1.0 + z2)
    sel1 = erow == i1
    sel2 = erow == i2
    onehot = jnp.logical_or(sel1, sel2)
    t_r = lax.broadcasted_iota(jnp.int32, (tm, tm), 0)
    t_c = lax.broadcasted_iota(jnp.int32, (tm, tm), 1)
    before = (t_r < t_c).astype(jnp.bfloat16)
    rank = jnp.dot(onehot.astype(jnp.bfloat16), before, preferred_element_type=jnp.float32)
    cnt = jnp.sum(onehot.astype(jnp.float32), axis=1, keepdims=True)
    cnt_al = jnp.floor((cnt + (SEG_ALIGN - 1)) * (1.0 / SEG_ALIGN)) * SEG_ALIGN
    e_r = lax.broadcasted_iota(jnp.int32, (N_EXPERTS, N_EXPERTS), 0)
    e_c = lax.broadcasted_iota(jnp.int32, (N_EXPERTS, N_EXPERTS), 1)
    lstart = jnp.dot((e_c < e_r).astype(jnp.float32), jnp.broadcast_to(cnt_al, (N_EXPERTS, LANES)),
                     precision=hi, preferred_element_type=jnp.float32)[:, 0:1]
    slot = lstart + rank
    pos1 = jnp.sum(jnp.where(sel1, slot, 0.0), axis=0, keepdims=True)
    pos2 = jnp.sum(jnp.where(sel2, slot, 0.0), axis=0, keepdims=True)
    r8 = lax.broadcasted_iota(jnp.int32, (SUBLANES, tm), 0)
    info = jnp.where(r8 == ROW_POS0, pos1, jnp.where(r8 == ROW_POS1, pos2, jnp.where(
        r8 == ROW_W0, w1, jnp.where(r8 == ROW_W1, w2, jnp.where(
            r8 == ROW_E0, i1, jnp.where(r8 == ROW_E1, i2, 0.0))))))
    row_ref[...] = info

    def split3(w):
        h = _bf16(w).astype(jnp.float32)
        m = _bf16(w - h).astype(jnp.float32)
        return h, m, _bf16(w - h - m).astype(jnp.float32)

    w1h, w1m, w1l = split3(w1)
    w2h, w2m, w2l = split3(w2)
    parts = jnp.where(r8 == 0, w1h, jnp.where(r8 == 1, w1m, jnp.where(r8 == 2, w1l, jnp.where(
        r8 == 3, w2h, jnp.where(r8 == 4, w2m, jnp.where(r8 == 5, w2l, jnp.where(r8 == 6, i1, i2)))))))
    col_ref[...] = jnp.concatenate(
        [info, parts, jnp.zeros((LANES - 2 * SUBLANES, tm), jnp.float32)], axis=0).T
    cnt_ref[...] = jnp.broadcast_to(cnt, (N_EXPERTS, LANES)).astype(jnp.int32)


def _out_proj_router(x2, att, hm, wo, g, wrt, brt):
    t = x2.shape[0]
    tm = TM_PROJ
    row = lambda i: (i, 0)
    fix = lambda i: (0, 0)
    return pl.pallas_call(
        _out_proj_router_kernel,
        grid=(t // tm,),
        in_specs=[
            pl.BlockSpec((tm, D_MODEL), row),
            pl.BlockSpec((tm, ATT_WIDTH), row),
            pl.BlockSpec((tm, ML_WIDTH), row),
            pl.BlockSpec((D_MODEL, D_MODEL), fix),
            pl.BlockSpec((1, D_MODEL), fix),
            pl.BlockSpec((4 * SUBLANES, D_MODEL), fix),
            pl.BlockSpec((4 * SUBLANES, 1), fix),
        ],
        out_specs=[
            pl.BlockSpec((tm, D_MODEL), row),
            pl.BlockSpec((tm, D_MODEL), row),
            pl.BlockSpec((SUBLANES, tm), lambda i: (0, i)),
            pl.BlockSpec((tm, LANES), row),
            pl.BlockSpec((None, N_EXPERTS, LANES), lambda i: (i, 0, 0)),
        ],
        out_shape=(
            jax.ShapeDtypeStruct((t, D_MODEL), jnp.float32),
            jax.ShapeDtypeStruct((t, D_MODEL), jnp.bfloat16),
            jax.ShapeDtypeStruct((SUBLANES, t), jnp.float32),
            jax.ShapeDtypeStruct((t, LANES), jnp.float32),
            jax.ShapeDtypeStruct((t // tm, N_EXPERTS, LANES), jnp.int32),
        ),
        compiler_params=pltpu.CompilerParams(dimension_semantics=("arbitrary",),
                                             vmem_limit_bytes=VMEM_LIMIT),
        name="out_proj_router",
    )(x2, att, hm, wo, g, wrt, brt)


def _chunk_rows(j):
    return pl.ds(pl.multiple_of(j * SEG_ALIGN, SEG_ALIGN), SEG_ALIGN)


def _dispatch_kernel(nch_ref, dst_ref, nz_ref, zdst_ref, t_ref, col_ref, row_ref, xs_hbm, sd_hbm,
                     xbuf, sbuf, zx, zs, sem_x, sem_s, sem_z):
    i = pl.program_id(0)
    n = pl.num_programs(0)
    p = i % 2
    tm = t_ref.shape[0]

    def copies(tile, par, j):
        d = pl.ds(pl.multiple_of(dst_ref[tile, j], SEG_ALIGN), SEG_ALIGN)
        return (pltpu.make_async_copy(xbuf.at[par, _chunk_rows(j)], xs_hbm.at[d], sem_x.at[par]),
                pltpu.make_async_copy(sbuf.at[par, _chunk_rows(j)], sd_hbm.at[d], sem_s.at[par]))

    def zero_copies(j):
        d = pl.ds(pl.multiple_of(zdst_ref[j], SEG_ALIGN), SEG_ALIGN)
        return (pltpu.make_async_copy(zx, xs_hbm.at[d], sem_z.at[0]),
                pltpu.make_async_copy(zs, sd_hbm.at[d], sem_z.at[1]))

    def wait_tile(tile, par):
        def body(j, c):
            cx, cs = copies(tile, par, j)
            cx.wait()
            cs.wait()
            return c
        lax.fori_loop(0, nch_ref[tile], body, 0)

    @pl.when(i == 0)
    def _():
        zx[...] = jnp.zeros(zx.shape, zx.dtype)
        zs[...] = jnp.zeros(zs.shape, zs.dtype)

        def body(j, c):
            cx, cs = zero_copies(j)
            cx.start()
            cs.start()
            return c
        lax.fori_loop(0, nz_ref[0], body, 0)

    @pl.when(i >= 2)
    def _():
        wait_tile(i - 2, p)

    pos0 = row_ref[ROW_POS0:ROW_POS0 + 1, :].astype(jnp.int32)
    pos1 = row_ref[ROW_POS1:ROW_POS1 + 1, :].astype(jnp.int32)
    t = t_ref[...]
    side = _bf16(col_ref[...])
    rows = DISPATCH_ROWS
    for c in range(L_CAP // rows):
        r = lax.broadcasted_iota(jnp.int32, (rows, tm), 0) + c * rows
        perm = jnp.logical_or(r == pos0, r == pos1).astype(jnp.bfloat16)
        xbuf[p, c * rows:(c + 1) * rows, :] = _bf16(
            jnp.dot(perm, t, preferred_element_type=jnp.float32))
        sbuf[p, c * rows:(c + 1) * rows, :] = jnp.dot(perm, side, preferred_element_type=jnp.float32)

    def start_body(j, c):
        cx, cs = copies(i, p, j)
        cx.start()
        cs.start()
        return c
    lax.fori_loop(0, nch_ref[i], start_body, 0)

    @pl.when(i == n - 1)
    def _():
        @pl.when(n >= 2)
        def _():
            wait_tile(i - 1, 1 - p)
        wait_tile(i, p)

        def body(j, c):
            cx, cs = zero_copies(j)
            cx.wait()
            cs.wait()
            return c
        lax.fori_loop(0, nz_ref[0], body, 0)


def _dispatch(t, col, row, nch, dst, nz, zdst):
    n_tok = t.shape[0]
    r_cap, _ = _moe_capacity(n_tok)
    tm = TM_PROJ
    grid_spec = pltpu.PrefetchScalarGridSpec(
        num_scalar_prefetch=4,
        grid=(n_tok // tm,),
        in_specs=[
            pl.BlockSpec((tm, D_MODEL), lambda i, *_: (i, 0)),
            pl.BlockSpec((tm, LANES), lambda i, *_: (i, 0)),
            pl.BlockSpec((SUBLANES, tm), lambda i, *_: (0, i)),
        ],
        out_specs=[pl.BlockSpec(memory_space=pl.ANY), pl.BlockSpec(memory_space=pl.ANY)],
        scratch_shapes=[
            pltpu.VMEM((2, L_CAP, D_MODEL), jnp.bfloat16),
            pltpu.VMEM((2, L_CAP, LANES), jnp.float32),
            pltpu.VMEM((SEG_ALIGN, D_MODEL), jnp.bfloat16),
            pltpu.VMEM((SEG_ALIGN, LANES), jnp.float32),
            pltpu.SemaphoreType.DMA((2,)),
            pltpu.SemaphoreType.DMA((2,)),
            pltpu.SemaphoreType.DMA((2,)),
        ],
    )
    return pl.pallas_call(
        _dispatch_kernel,
        grid_spec=grid_spec,
        out_shape=(jax.ShapeDtypeStruct((r_cap, D_MODEL), jnp.bfloat16),
                   jax.ShapeDtypeStruct((r_cap, LANES), jnp.float32)),
        compiler_params=pltpu.CompilerParams(dimension_semantics=("arbitrary",),
                                             vmem_limit_bytes=VMEM_LIMIT),
        name="moe_dispatch",
    )(nch, dst, nz, zdst, t, col, row)


def _moe_kernel(blk_e_ref, nused_ref, xs_ref, sd_ref, wg_ref, wu_ref, wd_ref, ys_ref):
    b = pl.program_id(0)

    @pl.when(b < nused_ref[0])
    def _():
        x = xs_ref[...]
        a = jnp.dot(x, wg_ref[0], preferred_element_type=jnp.float32)
        u = jnp.dot(x, wu_ref[0], preferred_element_type=jnp.float32)
        h = a * jax.nn.sigmoid(a) * u
        y = jnp.dot(_bf16(h), wd_ref[0], preferred_element_type=jnp.float32)
        sd = sd_ref[...]
        e_blk = blk_e_ref[b].astype(jnp.float32)
        w0 = sd[:, COL_W0H:COL_W0H + 1] + sd[:, COL_W0H + 1:COL_W0H + 2] + sd[:, COL_W0H + 2:COL_W0H + 3]
        w1 = sd[:, COL_W1H:COL_W1H + 1] + sd[:, COL_W1H + 1:COL_W1H + 2] + sd[:, COL_W1H + 2:COL_W1H + 3]
        w = jnp.where(sd[:, COL_E0:COL_E0 + 1] == e_blk, w0, w1)
        ys_ref[...] = _bf16(y * w)


def _moe(xs, sd, blk_e, nused, wg, wu, wd):
    blk = lambda b, be, nu: (jnp.maximum(jnp.minimum(b, nu[0] - 1), 0), 0)
    wsel = lambda b, be, nu: (be[b], 0, 0)
    r_cap = xs.shape[0]
    grid_spec = pltpu.PrefetchScalarGridSpec(
        num_scalar_prefetch=2,
        grid=(r_cap // MOE_BM,),
        in_specs=[
            pl.BlockSpec((MOE_BM, D_MODEL), blk),
            pl.BlockSpec((MOE_BM, LANES), blk),
            pl.BlockSpec((1, D_MODEL, D_FF_EXPERT), wsel),
            pl.BlockSpec((1, D_MODEL, D_FF_EXPERT), wsel),
            pl.BlockSpec((1, D_FF_EXPERT, D_MODEL), wsel),
        ],
        out_specs=pl.BlockSpec((MOE_BM, D_MODEL), blk),
    )
    return pl.pallas_call(
        _moe_kernel,
        grid_spec=grid_spec,
        out_shape=jax.ShapeDtypeStruct((r_cap, D_MODEL), jnp.bfloat16),
        compiler_params=pltpu.CompilerParams(dimension_semantics=("arbitrary",),
                                             vmem_limit_bytes=VMEM_LIMIT),
        name="moe_experts",
    )(blk_e, nused, xs, sd, wg, wu, wd)


def _combine_kernel(nch_ref, dst_ref, x1_ref, col_ref, ys_hbm, out_ref, ybuf, sem):
    i = pl.program_id(0)
    n = pl.num_programs(0)
    p = i % 2
    tm = x1_ref.shape[0]

    def copy(tile, par, j):
        s = pl.ds(pl.multiple_of(dst_ref[tile, j], SEG_ALIGN), SEG_ALIGN)
        return pltpu.make_async_copy(ys_hbm.at[s], ybuf.at[par, _chunk_rows(j)], sem.at[par])

    def start_tile(tile, par):
        def body(j, c):
            copy(tile, par, j).start()
            return c
        lax.fori_loop(0, nch_ref[tile], body, 0)

    @pl.when(i == 0)
    def _():
        ybuf[...] = jnp.zeros(ybuf.shape, ybuf.dtype)
        start_tile(0, 0)

    @pl.when(i + 1 < n)
    def _():
        start_tile(i + 1, 1 - p)

    def wait_body(j, c):
        copy(i, p, j).wait()
        return c
    lax.fori_loop(0, nch_ref[i], wait_body, 0)

    rows = COMBINE_ROWS
    for c in range(tm // rows):
        col = col_ref[c * rows:(c + 1) * rows, :]
        pos0 = col[:, ROW_POS0:ROW_POS0 + 1].astype(jnp.int32)
        pos1 = col[:, ROW_POS1:ROW_POS1 + 1].astype(jnp.int32)
        l = lax.broadcasted_iota(jnp.int32, (rows, L_CAP), 1)
        perm = jnp.logical_or(l == pos0, l == pos1).astype(jnp.bfloat16)
        y = jnp.dot(perm, ybuf[p], preferred_element_type=jnp.float32)
        out_ref[c * rows:(c + 1) * rows, :] = x1_ref[c * rows:(c + 1) * rows, :] + y


def _combine(x1, col, ys, nch, dst):
    n_tok = x1.shape[0]
    tm = TM_PROJ
    grid_spec = pltpu.PrefetchScalarGridSpec(
        num_scalar_prefetch=2,
        grid=(n_tok // tm,),
        in_specs=[
            pl.BlockSpec((tm, D_MODEL), lambda i, *_: (i, 0)),
            pl.BlockSpec((tm, LANES), lambda i, *_: (i, 0)),
            pl.BlockSpec(memory_space=pl.ANY),
        ],
        out_specs=pl.BlockSpec((tm, D_MODEL), lambda i, *_: (i, 0)),
        scratch_shapes=[
            pltpu.VMEM((2, L_CAP, D_MODEL), jnp.bfloat16),
            pltpu.SemaphoreType.DMA((2,)),
        ],
    )
    return pl.pallas_call(
        _combine_kernel,
        grid_spec=grid_spec,
        out_shape=jax.ShapeDtypeStruct((n_tok, D_MODEL), jnp.float32),
        compiler_params=pltpu.CompilerParams(dimension_semantics=("arbitrary",),
                                             vmem_limit_bytes=VMEM_LIMIT),
        name="moe_combine",
    )(nch, dst, x1, col, ys)


def _routing_tables(counts, n_blk_cap):
    i32 = jnp.int32
    ca = ((counts + (SEG_ALIGN - 1)) // SEG_ALIGN) * SEG_ALIGN
    lend = jnp.cumsum(ca, axis=1)
    lstart = lend - ca
    nch = (lend[:, -1] // SEG_ALIGN).astype(i32)
    tot = jnp.sum(ca, axis=0)
    region = ((tot + (MOE_BM - 1)) // MOE_BM) * MOE_BM
    rend = jnp.cumsum(region)
    base = rend - region
    gstart = base[None, :] + jnp.cumsum(ca, axis=0) - ca
    j16 = jnp.arange(N_CHUNK, dtype=i32) * SEG_ALIGN
    e_of_j = jnp.minimum(jnp.sum(lend[:, None, :] <= j16[None, :, None], axis=2), N_EXPERTS - 1)
    dst = (jnp.take_along_axis(gstart - lstart, e_of_j, axis=1) + j16[None, :]).astype(i32)
    dst = jnp.where(j16[None, :] < lend[:, -1:], dst, 0)
    nused = (rend[-1] // MOE_BM).astype(i32)
    brow = jnp.arange(n_blk_cap, dtype=i32) * MOE_BM
    blk_e = jnp.minimum(jnp.sum(rend[None, :] <= brow[:, None], axis=1), N_EXPERTS - 1)
    blk_e = jnp.where(jnp.arange(n_blk_cap) < nused, blk_e, blk_e[nused - 1]).astype(i32)
    k16 = jnp.arange(MOE_BM // SEG_ALIGN - 1, dtype=i32) * SEG_ALIGN
    zrow = (base + tot)[:, None] + k16[None, :]
    zvalid = zrow < rend[:, None]
    order = jnp.argsort(jnp.logical_not(zvalid).reshape(-1), stable=True)
    zdst = jnp.where(zvalid, zrow, 0).reshape(-1)[order].astype(i32)
    nz = jnp.sum(zvalid).astype(i32)
    return nch, dst, nz[None], zdst, blk_e, nused[None]


def _t5_bucket_np(dist):
    max_exact = N_BUCKETS // 2
    d = np.maximum(dist, 1).astype(np.float32)
    large = max_exact + (np.log(d / max_exact) / math.log(MAX_DISTANCE / max_exact)
                         * (N_BUCKETS - max_exact)).astype(np.int32)
    large = np.minimum(large, N_BUCKETS - 1)
    return np.where(dist < max_exact, dist, large)


def _attention_bias(rel_bias):
    qi = np.arange(ATT_BLOCK)[:, None]
    kj = np.arange(2 * ATT_BLOCK)[None, :]
    dist = qi + ATT_BLOCK - kj
    in_window = (dist >= 0) & (dist < WINDOW)
    bucket = _t5_bucket_np(np.clip(dist, 0, WINDOW - 1))
    onehot = (bucket[None] == np.arange(N_BUCKETS)[:, None, None]).astype(np.float32)
    bias = jnp.einsum('nh,nqk->hqk', rel_bias.astype(jnp.float32), jnp.asarray(onehot),
                      precision=lax.Precision.HIGHEST)
    return jnp.where(jnp.asarray(in_window)[None], bias, NEG_BIG)


def _block_diag_mean(width, block):
    idx = np.arange(width) // block
    return jnp.asarray((idx[:, None] == idx[None, :]).astype(np.float32) / block, dtype=jnp.bfloat16)


def _pack_w_in(w):
    hd = ATT_HEAD_DIM
    o = 0
    q = w[:, o:o + ATT_WIDTH]; o += ATT_WIDTH
    k = w[:, o:o + ATT_KV_HEADS * hd]; o += ATT_KV_HEADS * hd
    v = w[:, o:o + ATT_KV_HEADS * hd]; o += ATT_KV_HEADS * hd
    qm = w[:, o:o + ML_HEADS * ML_DQK]; o += ML_HEADS * ML_DQK
    km = w[:, o:o + ML_HEADS * ML_DQK]; o += ML_HEADS * ML_DQK
    vm = w[:, o:o + ML_WIDTH]; o += ML_WIDTH
    om = w[:, o:o + ML_WIDTH]; o += ML_WIDTH
    gates = w[:, o:o + 2 * ML_HEADS]
    dup = lambda a: jnp.concatenate([a[:, 0:hd], a[:, 0:hd], a[:, hd:2 * hd], a[:, hd:2 * hd]], axis=1)
    gpad = jnp.pad(gates, ((0, 0), (0, LANES - 2 * ML_HEADS)))
    packed = jnp.concatenate([q, dup(k), dup(v), qm, km, vm, om, gpad], axis=1)
    return _bf16(packed), _bf16(gates.T)


def kernel(x, rel_bias, norm_mix_g, w_in, q_norm_g, k_norm_g, attn_sink, conv_w, conv_b, gate_b,
           mlstm_norm_g, w_out, norm_ffn_g, w_router_group, b_router_group, w_router_expert,
           b_router_expert, w_gate, w_up, w_down):
    batch, seq_len, _ = x.shape
    n_tok = batch * seq_len
    assert seq_len % TM_PROJ == 0 and seq_len % TM_IN == 0 and seq_len % ML_STEP == 0
    f32 = jnp.float32
    bias = _attention_bias(rel_bias)
    bdq = _block_diag_mean(ATT_WIDTH, ATT_HEAD_DIM)
    bdk = _block_diag_mean(2 * LANES, ATT_HEAD_DIM)
    x2 = x.reshape(n_tok, D_MODEL)
    for l in range(DEPTH):
        w_pack, wgt = _pack_w_in(w_in[l])
        qg = (jnp.tile(q_norm_g[l].astype(f32), ATT_HEADS) * (ATT_HEAD_DIM ** -0.5))[None, :]
        kg = jnp.tile(k_norm_g[l].astype(f32), 2 * ATT_KV_HEADS)[None, :]
        gb = jnp.pad(gate_b[l].astype(f32), (0, LANES - 2 * ML_HEADS))[None, :]
        gbt = gate_b[l].astype(f32)[:, None]
        qn, kn, vd, qkm, vm, om, gates, gates_t = _in_proj(
            x2, norm_mix_g[l][None, :], w_pack, wgt, qg, kg, bdq, bdk,
            conv_w[l], conv_b[l][None, :], gb, gbt, batch, seq_len)
        att = _attention(qn, kn, vd, bias, attn_sink[l].astype(f32), batch, seq_len)
        r3 = lambda a: a.reshape(batch, seq_len, a.shape[-1])
        hm = _mlstm(r3(qkm), r3(vm), r3(om), r3(gates), gates_t,
                    mlstm_norm_g[l][None, :], batch, seq_len)
        n_rt = 4 * SUBLANES
        wrt = jnp.pad(jnp.concatenate([w_router_expert[l], w_router_group[l]], axis=1).astype(f32).T,
                      ((0, n_rt - N_EXPERTS - N_GROUPS), (0, 0)))
        brt = jnp.pad(jnp.concatenate([b_router_expert[l], b_router_group[l]]).astype(f32),
                      (0, n_rt - N_EXPERTS - N_GROUPS))[:, None]
        x1, t, row, col, cnt = _out_proj_router(x2, att, hm.reshape(n_tok, ML_WIDTH), _bf16(w_out[l]),
                                                norm_ffn_g[l][None, :], wrt, brt)
        _, n_blk_cap = _moe_capacity(n_tok)
        nch, dst, nz, zdst, blk_e, nused = _routing_tables(cnt[:, :, 0], n_blk_cap)
        xs, sd = _dispatch(t, col, row, nch, dst, nz, zdst)
        ys = _moe(xs, sd, blk_e, nused, _bf16(w_gate[l]), _bf16(w_up[l]), _bf16(w_down[l]))
        x2 = _combine(x1, col, ys, nch, dst)
    return x2.reshape(batch, seq_len, D_MODEL)
```

```python
import functools
import math

import jax
import jax.numpy as jnp
import numpy as np
from jax import lax
from jax.experimental import pallas as pl
from jax.experimental.pallas import tpu as pltpu

D_MODEL = 1024
DEPTH = 2
ATT_HEADS = 8
ATT_KV_HEADS = 2
ATT_HEAD_DIM = 64
ATT_WIDTH = ATT_HEADS * ATT_HEAD_DIM
WINDOW = 128
ATT_BLOCK = 128
N_BUCKETS = 32
MAX_DISTANCE = 128
ML_HEADS = 4
ML_DQK = 64
ML_DV = 128
ML_WIDTH = ML_HEADS * ML_DV
ML_CHUNK = 64
CONV_K = 4
N_GROUPS = 4
EXPERTS_PER_GROUP = 4
N_EXPERTS = N_GROUPS * EXPERTS_PER_GROUP
D_FF_EXPERT = 512
EPS = 1e-6

LANES = 128
SUBLANES = 8
NEG_BIG = -1e30
VMEM_LIMIT = 48 * 1024 * 1024

C_Q = 0
C_K = C_Q + ATT_WIDTH
C_V = C_K + 2 * LANES
C_QKM = C_V + 2 * LANES
C_VM = C_QKM + 2 * ML_HEADS * ML_DQK
C_OM = C_VM + ML_WIDTH
C_G = C_OM + ML_WIDTH
N_PACK = C_G + LANES

TM_IN = 1024
TM_PROJ = 512
ML_KCHUNK = 256
ML_STEP = ML_KCHUNK

SEG_ALIGN = 16
MOE_BM = 512
L_CAP = 2 * TM_PROJ + N_EXPERTS * SEG_ALIGN
N_CHUNK = L_CAP // SEG_ALIGN
D_XS = D_MODEL + LANES
DISPATCH_ROWS = 256
COMBINE_ROWS = 256
ROW_POS0, ROW_POS1, ROW_W0, ROW_W1, ROW_E0, ROW_E1 = 0, 1, 2, 3, 4, 5
COL_W0H, COL_W1H, COL_E0, COL_E1 = 8, 11, 14, 15


def _moe_capacity(n_tok):
    n_tiles = n_tok // TM_PROJ
    rows = 2 * n_tok + n_tiles * N_EXPERTS * (SEG_ALIGN - 1) + N_EXPERTS * (MOE_BM - SEG_ALIGN)
    n_blk = -(-rows // MOE_BM)
    return n_blk * MOE_BM, n_blk


def _bf16(a):
    return a.astype(jnp.bfloat16)


def _log_sigmoid(z):
    return jnp.minimum(z, 0.0) - jnp.log(1.0 + jnp.exp(-jnp.abs(z)))


def _in_proj_kernel(tiles_per_seq, x_ref, g_ref, w_ref, wgt_ref, qg_ref, kg_ref, bdq_ref, bdk_ref,
                    cw_ref, cb_ref, gb_ref, gbt_ref,
                    qn_ref, kn_ref, vd_ref, qkm_ref, vm_ref, om_ref, gate_ref, gatet_ref,
                    conv_scr):
    i = pl.program_id(0)
    tm = x_ref.shape[0]
    x = x_ref[...]
    hn = x * lax.rsqrt(jnp.mean(x * x, axis=-1, keepdims=True) + EPS) * g_ref[...]
    hb = _bf16(hn)

    def proj(c0, width):
        return jnp.dot(hb, w_ref[:, c0:c0 + width], preferred_element_type=jnp.float32)

    q = proj(C_Q, ATT_WIDTH)
    q_ms = jnp.dot(_bf16(q * q), bdq_ref[...], preferred_element_type=jnp.float32)
    qn_ref[...] = _bf16(q * lax.rsqrt(q_ms + EPS) * qg_ref[...])
    k = proj(C_K, 2 * LANES)
    k_ms = jnp.dot(_bf16(k * k), bdk_ref[...], preferred_element_type=jnp.float32)
    kn_ref[...] = _bf16(k * lax.rsqrt(k_ms + EPS) * kg_ref[...])
    vd_ref[...] = _bf16(proj(C_V, 2 * LANES))

    qk = proj(C_QKM, 2 * ML_HEADS * ML_DQK)

    @pl.when(i % tiles_per_seq == 0)
    def _():
        conv_scr[0:SUBLANES, :] = jnp.zeros((SUBLANES, qk.shape[1]), jnp.float32)

    @pl.when(i % tiles_per_seq != 0)
    def _():
        conv_scr[0:SUBLANES, :] = conv_scr[tm:tm + SUBLANES, :]

    conv_scr[SUBLANES:SUBLANES + tm, :] = qk
    y = qk * cw_ref[CONV_K - 1:CONV_K, :] + cb_ref[...]
    for j in range(CONV_K - 1):
        off = SUBLANES - (CONV_K - 1) + j
        y = y + conv_scr[off:off + tm, :] * cw_ref[j:j + 1, :]
    y = y * jax.nn.sigmoid(y)
    lane = lax.broadcasted_iota(jnp.int32, y.shape, 1)
    y = jnp.where(lane >= ML_HEADS * ML_DQK, y * (ML_DQK ** -0.5), y)
    qkm_ref[...] = _bf16(y)

    vm_ref[...] = _bf16(proj(C_VM, ML_WIDTH))
    om_ref[...] = _bf16(jax.nn.sigmoid(proj(C_OM, ML_WIDTH)))

    gp = proj(C_G, LANES) + gb_ref[...]
    glane = lax.broadcasted_iota(jnp.int32, gp.shape, 1)
    gate_ref[...] = jnp.where(glane >= ML_HEADS, _log_sigmoid(gp), gp)
    gt = lax.dot_general(wgt_ref[...], hb, (((1,), (1,)), ((), ())),
                         preferred_element_type=jnp.float32) + gbt_ref[...]
    grow = lax.broadcasted_iota(jnp.int32, gt.shape, 0)
    gatet_ref[...] = jnp.where(grow >= ML_HEADS, _log_sigmoid(gt), gt)


def _in_proj(x2, g, w_pack, wgt, qg, kg, bdq, bdk, cw, cb, gb, gbt, batch, seq_len):
    t = x2.shape[0]
    tm = TM_IN
    n = t // tm
    tps = seq_len // tm
    row = lambda i: (i, 0)
    fix = lambda i: (0, 0)
    out_shapes = (
        jax.ShapeDtypeStruct((t, ATT_WIDTH), jnp.bfloat16),
        jax.ShapeDtypeStruct((t, 2 * LANES), jnp.bfloat16),
        jax.ShapeDtypeStruct((t, 2 * LANES), jnp.bfloat16),
        jax.ShapeDtypeStruct((t, 2 * ML_HEADS * ML_DQK), jnp.bfloat16),
        jax.ShapeDtypeStruct((t, ML_WIDTH), jnp.bfloat16),
        jax.ShapeDtypeStruct((t, ML_WIDTH), jnp.bfloat16),
        jax.ShapeDtypeStruct((t, LANES), jnp.float32),
        jax.ShapeDtypeStruct((batch, SUBLANES, seq_len), jnp.float32),
    )
    in_specs = [
        pl.BlockSpec((tm, D_MODEL), row),
        pl.BlockSpec((1, D_MODEL), fix),
        pl.BlockSpec((D_MODEL, N_PACK), fix),
        pl.BlockSpec((SUBLANES, D_MODEL), fix),
        pl.BlockSpec((1, ATT_WIDTH), fix),
        pl.BlockSpec((1, 2 * LANES), fix),
        pl.BlockSpec((ATT_WIDTH, ATT_WIDTH), fix),
        pl.BlockSpec((2 * LANES, 2 * LANES), fix),
        pl.BlockSpec((CONV_K, 2 * ML_HEADS * ML_DQK), fix),
        pl.BlockSpec((1, 2 * ML_HEADS * ML_DQK), fix),
        pl.BlockSpec((1, LANES), fix),
        pl.BlockSpec((SUBLANES, 1), fix),
    ]
    out_specs = [
        pl.BlockSpec((tm, ATT_WIDTH), row),
        pl.BlockSpec((tm, 2 * LANES), row),
        pl.BlockSpec((tm, 2 * LANES), row),
        pl.BlockSpec((tm, 2 * ML_HEADS * ML_DQK), row),
        pl.BlockSpec((tm, ML_WIDTH), row),
        pl.BlockSpec((tm, ML_WIDTH), row),
        pl.BlockSpec((tm, LANES), row),
        pl.BlockSpec((None, SUBLANES, tm), lambda i: (i // tps, 0, i % tps)),
    ]
    return pl.pallas_call(
        functools.partial(_in_proj_kernel, seq_len // tm),
        grid=(n,),
        in_specs=in_specs,
        out_specs=out_specs,
        out_shape=out_shapes,
        scratch_shapes=[pltpu.VMEM((tm + 2 * SUBLANES, 2 * ML_HEADS * ML_DQK), jnp.float32)],
        compiler_params=pltpu.CompilerParams(dimension_semantics=("arbitrary",),
                                             vmem_limit_bytes=VMEM_LIMIT),
        name="in_proj",
    )(x2, g, w_pack, wgt, qg, kg, bdq, bdk, cw, cb, gb, gbt)


def _attn_kernel(sink_ref, q_ref, kp_ref, kc_ref, vp_ref, vc_ref, bias_ref, o_ref):
    i = pl.program_id(1)
    blk = q_ref.shape[0]
    lane = lax.broadcasted_iota(jnp.int32, (blk, LANES), 1)
    low = lane < ATT_HEAD_DIM
    col = lax.broadcasted_iota(jnp.int32, (blk, 2 * blk), 1)
    no_prev = jnp.logical_and(col < blk, i == 0)
    group = ATT_HEADS // ATT_KV_HEADS
    for pair in range(ATT_HEADS // 2):
        kv = (2 * pair) // group
        qp = q_ref[:, pair * LANES:(pair + 1) * LANES]
        kcat = jnp.concatenate([kp_ref[:, kv * LANES:(kv + 1) * LANES],
                                kc_ref[:, kv * LANES:(kv + 1) * LANES]], axis=0)
        vcat = jnp.concatenate([vp_ref[:, kv * LANES:(kv + 1) * LANES],
                                vc_ref[:, kv * LANES:(kv + 1) * LANES]], axis=0)
        halves = []
        for sub in range(2):
            h = 2 * pair + sub
            qm = jnp.where(low if sub == 0 else jnp.logical_not(low), qp, jnp.zeros_like(qp))
            s = lax.dot_general(qm, kcat, (((1,), (1,)), ((), ())),
                                preferred_element_type=jnp.float32)
            logits = jnp.where(no_prev, NEG_BIG, s + bias_ref[h])
            sink = sink_ref[h]
            m = jnp.maximum(jnp.max(logits, axis=-1, keepdims=True), sink)
            p = jnp.exp(logits - m)
            den = jnp.sum(p, axis=-1, keepdims=True) + jnp.exp(sink - m)
            o = jnp.dot(_bf16(p), vcat, preferred_element_type=jnp.float32)
            halves.append(o / den)
        o_ref[:, pair * LANES:(pair + 1) * LANES] = _bf16(jnp.where(low, halves[0], halves[1]))


def _attention(qn, kn, vd, bias, sink, batch, seq_len):
    nb = seq_len // ATT_BLOCK
    cur = lambda b, i, s: (b * nb + i, 0)
    prev = lambda b, i, s: (b * nb + jnp.maximum(i - 1, 0), 0)
    grid_spec = pltpu.PrefetchScalarGridSpec(
        num_scalar_prefetch=1,
        grid=(batch, nb),
        in_specs=[
            pl.BlockSpec((ATT_BLOCK, ATT_WIDTH), cur),
            pl.BlockSpec((ATT_BLOCK, 2 * LANES), prev),
            pl.BlockSpec((ATT_BLOCK, 2 * LANES), cur),
            pl.BlockSpec((ATT_BLOCK, 2 * LANES), prev),
            pl.BlockSpec((ATT_BLOCK, 2 * LANES), cur),
            pl.BlockSpec((ATT_HEADS, ATT_BLOCK, 2 * ATT_BLOCK), lambda b, i, s: (0, 0, 0)),
        ],
        out_specs=pl.BlockSpec((ATT_BLOCK, ATT_WIDTH), cur),
    )
    return pl.pallas_call(
        _attn_kernel,
        grid_spec=grid_spec,
        out_shape=jax.ShapeDtypeStruct((batch * seq_len, ATT_WIDTH), jnp.bfloat16),
        compiler_params=pltpu.CompilerParams(dimension_semantics=("arbitrary", "arbitrary"),
                                             vmem_limit_bytes=VMEM_LIMIT),
        name="swa_attention",
    )(sink, qn, kn, kn, vd, vd, bias)


def _mlstm_kernel(qk_ref, v_ref, o_ref, g_ref, gt_ref, ng_ref, out_ref, c_scr, m_scr):
    step = pl.program_id(0)
    batch = qk_ref.shape[0]
    L = ML_KCHUNK
    pairs = ML_HEADS // 2
    hi = lax.Precision.HIGHEST

    @pl.when(step == 0)
    def _():
        c_scr[...] = jnp.zeros(c_scr.shape, jnp.float32)
        m_scr[...] = jnp.zeros(m_scr.shape, jnp.float32)

    r_i = lax.broadcasted_iota(jnp.int32, (L, L), 0)
    c_i = lax.broadcasted_iota(jnp.int32, (L, L), 1)
    causal = c_i <= r_i
    tril = causal.astype(jnp.float32)
    triu = (r_i <= c_i).astype(jnp.float32)
    lane = lax.broadcasted_iota(jnp.int32, (L, LANES), 1)
    low = lane < ML_DQK
    one_col = (lane == 0).astype(jnp.bfloat16)
    row2 = lax.broadcasted_iota(jnp.int32, (2 * ML_DQK, 1), 0)

    for ch in range(ML_STEP // L):
        r0 = ch * L
        for b in range(batch):
            g = g_ref[b, r0:r0 + L, :]
            gt = gt_ref[b, :, r0:r0 + L]
            bcols = jnp.dot(tril, g, precision=hi, preferred_element_type=jnp.float32)
            brows = jnp.dot(gt, triu, precision=hi, preferred_element_type=jnp.float32)
            for pair in range(pairs):
                sidx = b * pairs + pair
                qp = qk_ref[b, r0:r0 + L, pair * LANES:(pair + 1) * LANES]
                kp = qk_ref[b, r0:r0 + L, (pairs + pair) * LANES:(pairs + pair + 1) * LANES]
                c_pair = c_scr[sidx]
                c_bf = _bf16(c_pair)
                new_c = None
                decays = []
                for sub in range(2):
                    h = 2 * pair + sub
                    sel = low if sub == 0 else jnp.logical_not(low)
                    m_prev = m_scr[b * ML_HEADS + h]
                    bc = bcols[:, ML_HEADS + h:ML_HEADS + h + 1]
                    br = brows[ML_HEADS + h:ML_HEADS + h + 1, :]
                    lic = g[:, h:h + 1]
                    lir = gt[h:h + 1, :]
                    log_d = jnp.where(causal, bc - br + lir, NEG_BIG)
                    m_inter = bc + m_prev
                    m_row = jnp.maximum(m_inter, jnp.max(log_d, axis=-1, keepdims=True))
                    d = jnp.exp(log_d - m_row)
                    inter = jnp.exp(m_inter - m_row)
                    qm = jnp.where(sel, qp, jnp.zeros_like(qp))
                    s = lax.dot_general(qm, kp, (((1,), (1,)), ((), ())),
                                        preferred_element_type=jnp.float32) * d
                    v_ext = jnp.concatenate([v_ref[b, r0:r0 + L, h * ML_DV:(h + 1) * ML_DV], one_col],
                                            axis=-1)
                    num = inter * jnp.dot(qm, c_bf, preferred_element_type=jnp.float32) \
                        + jnp.dot(_bf16(s), v_ext, preferred_element_type=jnp.float32)
                    den = num[:, ML_DV:ML_DV + 1]
                    hval = num[:, 0:ML_DV] / jnp.maximum(jnp.abs(den), jnp.exp(-m_row))
                    hn = hval * lax.rsqrt(jnp.mean(hval * hval, axis=-1, keepdims=True) + EPS)
                    hn = hn * ng_ref[:, h * ML_DV:(h + 1) * ML_DV]
                    out_ref[b, r0:r0 + L, h * ML_DV:(h + 1) * ML_DV] = _bf16(
                        hn * o_ref[b, r0:r0 + L, h * ML_DV:(h + 1) * ML_DV].astype(jnp.float32))
                    b_last = bc[L - 1:L, :]
                    log_w = b_last - bc + lic
                    m_next = jnp.maximum(b_last + m_prev, jnp.max(log_w, axis=0, keepdims=True))
                    w = jnp.exp(log_w - m_next)
                    decays.append(jnp.exp(b_last + m_prev - m_next))
                    m_scr[b * ML_HEADS + h] = m_next
                    kw = _bf16(jnp.where(sel, kp.astype(jnp.float32) * w, 0.0))
                    upd = lax.dot_general(kw, v_ext, (((0,), (0,)), ((), ())),
                                          preferred_element_type=jnp.float32)
                    new_c = upd if new_c is None else new_c + upd
                decay_rows = jnp.where(row2 < ML_DQK, decays[0], decays[1])
                c_scr[sidx] = decay_rows * c_pair + new_c


def _mlstm(qkm, vm, om, gates, gates_t, ng, batch, seq_len):
    n_steps = seq_len // ML_STEP
    blk = lambda c: (0, c, 0)
    return pl.pallas_call(
        _mlstm_kernel,
        grid=(n_steps,),
        in_specs=[
            pl.BlockSpec((batch, ML_STEP, 2 * ML_HEADS * ML_DQK), blk),
            pl.BlockSpec((batch, ML_STEP, ML_WIDTH), blk),
            pl.BlockSpec((batch, ML_STEP, ML_WIDTH), blk),
            pl.BlockSpec((batch, ML_STEP, LANES), blk),
            pl.BlockSpec((batch, SUBLANES, ML_STEP), lambda c: (0, 0, c)),
            pl.BlockSpec((1, ML_WIDTH), lambda c: (0, 0)),
        ],
        out_specs=pl.BlockSpec((batch, ML_STEP, ML_WIDTH), blk),
        out_shape=jax.ShapeDtypeStruct((batch, seq_len, ML_WIDTH), jnp.bfloat16),
        scratch_shapes=[
            pltpu.VMEM((batch * ML_HEADS // 2, 2 * ML_DQK, 2 * ML_DV), jnp.float32),
            pltpu.VMEM((batch * ML_HEADS, 1, 1), jnp.float32),
        ],
        compiler_params=pltpu.CompilerParams(dimension_semantics=("arbitrary",),
                                             vmem_limit_bytes=VMEM_LIMIT),
        name="mlstm_scan",
    )(qkm, vm, om, gates, gates_t, ng)


def _out_proj_router_kernel(x_ref, att_ref, hm_ref, wo_ref, g_ref, wrt_ref, brt_ref,
                            x1_ref, t_ref, row_ref, col_ref, cnt_ref):
    tm = x_ref.shape[0]
    hi = lax.Precision.HIGHEST
    x1 = x_ref[...] \
        + jnp.dot(att_ref[...], wo_ref[0:ATT_WIDTH, :], preferred_element_type=jnp.float32) \
        + jnp.dot(hm_ref[...], wo_ref[ATT_WIDTH:, :], preferred_element_type=jnp.float32)
    x1_ref[...] = x1
    tn = x1 * lax.rsqrt(jnp.mean(x1 * x1, axis=-1, keepdims=True) + EPS) * g_ref[...]
    tn_hi = _bf16(tn)
    t_ref[...] = tn_hi
    tn_lo = _bf16(tn - tn_hi.astype(jnp.float32))
    nt = (((1,), (1,)), ((), ()))
    n_rt = wrt_ref.shape[0] // 2
    p_hi = lax.dot_general(wrt_ref[...], tn_hi, nt, preferred_element_type=jnp.float32)
    p_lo = lax.dot_general(wrt_ref[0:n_rt, :], tn_lo, nt, preferred_element_type=jnp.float32)
    logits = p_hi[0:n_rt, :] + p_hi[n_rt:, :] + p_lo + brt_ref[...]
    el_all = logits[0:N_EXPERTS, :]
    gl = logits[N_EXPERTS:N_EXPERTS + SUBLANES, :]
    grow = lax.broadcasted_iota(jnp.int32, gl.shape, 0).astype(jnp.float32)
    gl = jnp.where(grow < N_GROUPS, gl, NEG_BIG)
    gmax = jnp.max(gl, axis=0, keepdims=True)
    grp = jnp.min(jnp.where(gl == gmax, grow, float(N_GROUPS)), axis=0, keepdims=True)
    p_grp = 1.0 / jnp.sum(jnp.exp(gl - gmax), axis=0, keepdims=True)
    erow = lax.broadcasted_iota(jnp.int32, el_all.shape, 0).astype(jnp.float32)
    egrp = jnp.floor(erow * (1.0 / EXPERTS_PER_GROUP))
    el = jnp.where(egrp == grp, el_all, NEG_BIG)
    e1 = jnp.max(el, axis=0, keepdims=True)
    i1 = jnp.min(jnp.where(el == e1, erow, float(N_EXPERTS)), axis=0, keepdims=True)
    el2 = jnp.where(erow == i1, NEG_BIG, el)
    e2 = jnp.max(el2, axis=0, keepdims=True)
    i2 = jnp.min(jnp.where(el2 == e2, erow, float(N_EXPERTS)), axis=0, keepdims=True)
    z2 = jnp.exp(e2 - e1)
    w1 = p_grp / (1.0 + z2)
    w2 = p_grp * z2 / (1.0 + z2)
    sel1 = erow == i1
    sel2 = erow == i2
    onehot = jnp.logical_or(sel1, sel2)
    t_r = lax.broadcasted_iota(jnp.int32, (tm, tm), 0)
    t_c = lax.broadcasted_iota(jnp.int32, (tm, tm), 1)
    before = (t_r < t_c).astype(jnp.bfloat16)
    rank = jnp.dot(onehot.astype(jnp.bfloat16), before, preferred_element_type=jnp.float32)
    cnt = jnp.sum(onehot.astype(jnp.float32), axis=1, keepdims=True)
    cnt_al = jnp.floor((cnt + (SEG_ALIGN - 1)) * (1.0 / SEG_ALIGN)) * SEG_ALIGN
    e_r = lax.broadcasted_iota(jnp.int32, (N_EXPERTS, N_EXPERTS), 0)
    e_c = lax.broadcasted_iota(jnp.int32, (N_EXPERTS, N_EXPERTS), 1)
    lstart = jnp.dot((e_c < e_r).astype(jnp.float32), jnp.broadcast_to(cnt_al, (N_EXPERTS, LANES)),
                     precision=hi, preferred_element_type=jnp.float32)[:, 0:1]
    slot = lstart + rank
    pos1 = jnp.sum(jnp.where(sel1, slot, 0.0), axis=0, keepdims=True)
    pos2 = jnp.sum(jnp.where(sel2, slot, 0.0), axis=0, keepdims=True)
    r8 = lax.broadcasted_iota(jnp.int32, (SUBLANES, tm), 0)
    info = jnp.where(r8 == ROW_POS0, pos1, jnp.where(r8 == ROW_POS1, pos2, jnp.where(
        r8 == ROW_W0, w1, jnp.where(r8 == ROW_W1, w2, jnp.where(
            r8 == ROW_E0, i1, jnp.where(r8 == ROW_E1, i2, 0.0))))))
    row_ref[...] = info

    def split3(w):
        h = _bf16(w).astype(jnp.float32)
        m = _bf16(w - h).astype(jnp.float32)
        return h, m, _bf16(w - h - m).astype(jnp.float32)

    w1h, w1m, w1l = split3(w1)
    w2h, w2m, w2l = split3(w2)
    parts = jnp.where(r8 == 0, w1h, jnp.where(r8 == 1, w1m, jnp.where(r8 == 2, w1l, jnp.where(
        r8 == 3, w2h, jnp.where(r8 == 4, w2m, jnp.where(r8 == 5, w2l, jnp.where(r8 == 6, i1, i2)))))))
    col_ref[...] = jnp.concatenate(
        [info, parts, jnp.zeros((LANES - 2 * SUBLANES, tm), jnp.float32)], axis=0).T
    cnt_ref[...] = jnp.broadcast_to(cnt, (N_EXPERTS, LANES)).astype(jnp.int32)


def _out_proj_router(x2, att, hm, wo, g, wrt, brt):
    t = x2.shape[0]
    tm = TM_PROJ
    row = lambda i: (i, 0)
    fix = lambda i: (0, 0)
    return pl.pallas_call(
        _out_proj_router_kernel,
        grid=(t // tm,),
        in_specs=[
            pl.BlockSpec((tm, D_MODEL), row),
            pl.BlockSpec((tm, ATT_WIDTH), row),
            pl.BlockSpec((tm, ML_WIDTH), row),
            pl.BlockSpec((D_MODEL, D_MODEL), fix),
            pl.BlockSpec((1, D_MODEL), fix),
            pl.BlockSpec((8 * SUBLANES, D_MODEL), fix),
            pl.BlockSpec((4 * SUBLANES, 1), fix),
        ],
        out_specs=[
            pl.BlockSpec((tm, D_MODEL), row),
            pl.BlockSpec((tm, D_MODEL), row),
            pl.BlockSpec((SUBLANES, tm), lambda i: (0, i)),
            pl.BlockSpec((tm, LANES), row),
            pl.BlockSpec((None, N_EXPERTS, LANES), lambda i: (i, 0, 0)),
        ],
        out_shape=(
            jax.ShapeDtypeStruct((t, D_MODEL), jnp.float32),
            jax.ShapeDtypeStruct((t, D_MODEL), jnp.bfloat16),
            jax.ShapeDtypeStruct((SUBLANES, t), jnp.float32),
            jax.ShapeDtypeStruct((t, LANES), jnp.float32),
            jax.ShapeDtypeStruct((t // tm, N_EXPERTS, LANES), jnp.int32),
        ),
        compiler_params=pltpu.CompilerParams(dimension_semantics=("arbitrary",),
                                             vmem_limit_bytes=VMEM_LIMIT),
        name="out_proj_router",
    )(x2, att, hm, wo, g, wrt, brt)


def _chunk_rows(j):
    return pl.ds(pl.multiple_of(j * SEG_ALIGN, SEG_ALIGN), SEG_ALIGN)


def _dispatch_kernel(nch_ref, dst_ref, nz_ref, zdst_ref, t_ref, col_ref, row_ref, xs_hbm,
                     xbuf, zx, sem_x, sem_z):
    i = pl.program_id(0)
    n = pl.num_programs(0)
    p = i % 2
    tm = t_ref.shape[0]

    def copy(tile, par, j):
        d = pl.ds(pl.multiple_of(dst_ref[tile, j], SEG_ALIGN), SEG_ALIGN)
        return pltpu.make_async_copy(xbuf.at[par, _chunk_rows(j)], xs_hbm.at[d], sem_x.at[par])

    def zero_copy(j):
        d = pl.ds(pl.multiple_of(zdst_ref[j], SEG_ALIGN), SEG_ALIGN)
        return pltpu.make_async_copy(zx, xs_hbm.at[d], sem_z.at[0])

    def wait_tile(tile, par):
        def body(j, c):
            copy(tile, par, j).wait()
            return c
        lax.fori_loop(0, nch_ref[tile], body, 0)

    @pl.when(i == 0)
    def _():
        zx[...] = jnp.zeros(zx.shape, zx.dtype)

        def body(j, c):
            zero_copy(j).start()
            return c
        lax.fori_loop(0, nz_ref[0], body, 0)

    @pl.when(i >= 2)
    def _():
        wait_tile(i - 2, p)

    pos0 = row_ref[ROW_POS0:ROW_POS0 + 1, :].astype(jnp.int32)
    pos1 = row_ref[ROW_POS1:ROW_POS1 + 1, :].astype(jnp.int32)
    t = t_ref[...]
    side = _bf16(col_ref[...])
    rows = DISPATCH_ROWS
    for c in range(L_CAP // rows):
        r = lax.broadcasted_iota(jnp.int32, (rows, tm), 0) + c * rows
        perm = jnp.logical_or(r == pos0, r == pos1).astype(jnp.bfloat16)
        xbuf[p, c * rows:(c + 1) * rows, 0:D_MODEL] = _bf16(
            jnp.dot(perm, t, preferred_element_type=jnp.float32))
        xbuf[p, c * rows:(c + 1) * rows, D_MODEL:D_XS] = _bf16(
            jnp.dot(perm, side, preferred_element_type=jnp.float32))

    def start_body(j, c):
        copy(i, p, j).start()
        return c
    lax.fori_loop(0, nch_ref[i], start_body, 0)

    @pl.when(i == n - 1)
    def _():
        @pl.when(n >= 2)
        def _():
            wait_tile(i - 1, 1 - p)
        wait_tile(i, p)

        def body(j, c):
            zero_copy(j).wait()
            return c
        lax.fori_loop(0, nz_ref[0], body, 0)


def _dispatch(t, col, row, nch, dst, nz, zdst):
    n_tok = t.shape[0]
    r_cap, _ = _moe_capacity(n_tok)
    tm = TM_PROJ
    grid_spec = pltpu.PrefetchScalarGridSpec(
        num_scalar_prefetch=4,
        grid=(n_tok // tm,),
        in_specs=[
            pl.BlockSpec((tm, D_MODEL), lambda i, *_: (i, 0)),
            pl.BlockSpec((tm, LANES), lambda i, *_: (i, 0)),
            pl.BlockSpec((SUBLANES, tm), lambda i, *_: (0, i)),
        ],
        out_specs=pl.BlockSpec(memory_space=pl.ANY),
        scratch_shapes=[
            pltpu.VMEM((2, L_CAP, D_XS), jnp.bfloat16),
            pltpu.VMEM((SEG_ALIGN, D_XS), jnp.bfloat16),
            pltpu.SemaphoreType.DMA((2,)),
            pltpu.SemaphoreType.DMA((1,)),
        ],
    )
    return pl.pallas_call(
        _dispatch_kernel,
        grid_spec=grid_spec,
        out_shape=jax.ShapeDtypeStruct((r_cap, D_XS), jnp.bfloat16),
        compiler_params=pltpu.CompilerParams(dimension_semantics=("arbitrary",),
                                             vmem_limit_bytes=VMEM_LIMIT),
        name="moe_dispatch",
    )(nch, dst, nz, zdst, t, col, row)


def _moe_kernel(blk_e_ref, nused_ref, xs_ref, wg_ref, wu_ref, wd_ref, ys_ref):
    b = pl.program_id(0)

    @pl.when(b < nused_ref[0])
    def _():
        x = xs_ref[:, 0:D_MODEL]
        a = jnp.dot(x, wg_ref[0], preferred_element_type=jnp.float32)
        u = jnp.dot(x, wu_ref[0], preferred_element_type=jnp.float32)
        h = a * jax.nn.sigmoid(a) * u
        y = jnp.dot(_bf16(h), wd_ref[0], preferred_element_type=jnp.float32)
        sd = xs_ref[:, D_MODEL:D_XS].astype(jnp.float32)
        e_blk = blk_e_ref[b].astype(jnp.float32)
        w0 = sd[:, COL_W0H:COL_W0H + 1] + sd[:, COL_W0H + 1:COL_W0H + 2] + sd[:, COL_W0H + 2:COL_W0H + 3]
        w1 = sd[:, COL_W1H:COL_W1H + 1] + sd[:, COL_W1H + 1:COL_W1H + 2] + sd[:, COL_W1H + 2:COL_W1H + 3]
        w = jnp.where(sd[:, COL_E0:COL_E0 + 1] == e_blk, w0, w1)
        ys_ref[...] = _bf16(y * w)


def _moe(xs, blk_e, nused, wg, wu, wd):
    blk = lambda b, be, nu: (jnp.maximum(jnp.minimum(b, nu[0] - 1), 0), 0)
    wsel = lambda b, be, nu: (be[b], 0, 0)
    r_cap = xs.shape[0]
    grid_spec = pltpu.PrefetchScalarGridSpec(
        num_scalar_prefetch=2,
        grid=(r_cap // MOE_BM,),
        in_specs=[
            pl.BlockSpec((MOE_BM, D_XS), blk),
            pl.BlockSpec((1, D_MODEL, D_FF_EXPERT), wsel),
            pl.BlockSpec((1, D_MODEL, D_FF_EXPERT), wsel),
            pl.BlockSpec((1, D_FF_EXPERT, D_MODEL), wsel),
        ],
        out_specs=pl.BlockSpec((MOE_BM, D_MODEL), blk),
    )
    return pl.pallas_call(
        _moe_kernel,
        grid_spec=grid_spec,
        out_shape=jax.ShapeDtypeStruct((r_cap, D_MODEL), jnp.bfloat16),
        compiler_params=pltpu.CompilerParams(dimension_semantics=("arbitrary",),
                                             vmem_limit_bytes=VMEM_LIMIT),
        name="moe_experts",
    )(blk_e, nused, xs, wg, wu, wd)


def _combine_kernel(nch_ref, dst_ref, x1_ref, col_ref, ys_hbm, out_ref, ybuf, sem):
    i = pl.program_id(0)
    n = pl.num_programs(0)
    p = i % 2
    tm = x1_ref.shape[0]

    def copy(tile, par, j):
        s = pl.ds(pl.multiple_of(dst_ref[tile, j], SEG_ALIGN), SEG_ALIGN)
        return pltpu.make_async_copy(ys_hbm.at[s], ybuf.at[par, _chunk_rows(j)], sem.at[par])

    def start_tile(tile, par):
        def body(j, c):
            copy(tile, par, j).start()
            return c
        lax.fori_loop(0, nch_ref[tile], body, 0)

    @pl.when(i == 0)
    def _():
        ybuf[...] = jnp.zeros(ybuf.shape, ybuf.dtype)
        start_tile(0, 0)

    @pl.when(i + 1 < n)
    def _():
        start_tile(i + 1, 1 - p)

    def wait_body(j, c):
        copy(i, p, j).wait()
        return c
    lax.fori_loop(0, nch_ref[i], wait_body, 0)

    rows = COMBINE_ROWS
    for c in range(tm // rows):
        col = col_ref[c * rows:(c + 1) * rows, :]
        pos0 = col[:, ROW_POS0:ROW_POS0 + 1].astype(jnp.int32)
        pos1 = col[:, ROW_POS1:ROW_POS1 + 1].astype(jnp.int32)
        l = lax.broadcasted_iota(jnp.int32, (rows, L_CAP), 1)
        perm = jnp.logical_or(l == pos0, l == pos1).astype(jnp.bfloat16)
        y = jnp.dot(perm, ybuf[p], preferred_element_type=jnp.float32)
        out_ref[c * rows:(c + 1) * rows, :] = x1_ref[c * rows:(c + 1) * rows, :] + y


def _combine(x1, col, ys, nch, dst):
    n_tok = x1.shape[0]
    tm = TM_PROJ
    grid_spec = pltpu.PrefetchScalarGridSpec(
        num_scalar_prefetch=2,
        grid=(n_tok // tm,),
        in_specs=[
            pl.BlockSpec((tm, D_MODEL), lambda i, *_: (i, 0)),
            pl.BlockSpec((tm, LANES), lambda i, *_: (i, 0)),
            pl.BlockSpec(memory_space=pl.ANY),
        ],
        out_specs=pl.BlockSpec((tm, D_MODEL), lambda i, *_: (i, 0)),
        scratch_shapes=[
            pltpu.VMEM((2, L_CAP, D_MODEL), jnp.bfloat16),
            pltpu.SemaphoreType.DMA((2,)),
        ],
    )
    return pl.pallas_call(
        _combine_kernel,
        grid_spec=grid_spec,
        out_shape=jax.ShapeDtypeStruct((n_tok, D_MODEL), jnp.float32),
        compiler_params=pltpu.CompilerParams(dimension_semantics=("arbitrary",),
                                             vmem_limit_bytes=VMEM_LIMIT),
        name="moe_combine",
    )(nch, dst, x1, col, ys)


def _routing_tables(counts, n_blk_cap):
    i32 = jnp.int32
    ca = ((counts + (SEG_ALIGN - 1)) // SEG_ALIGN) * SEG_ALIGN
    lend = jnp.cumsum(ca, axis=1)
    lstart = lend - ca
    nch = (lend[:, -1] // SEG_ALIGN).astype(i32)
    tot = jnp.sum(ca, axis=0)
    region = ((tot + (MOE_BM - 1)) // MOE_BM) * MOE_BM
    rend = jnp.cumsum(region)
    base = rend - region
    gstart = base[None, :] + jnp.cumsum(ca, axis=0) - ca
    j16 = jnp.arange(N_CHUNK, dtype=i32) * SEG_ALIGN
    e_of_j = jnp.minimum(jnp.sum(lend[:, None, :] <= j16[None, :, None], axis=2), N_EXPERTS - 1)
    dst = (jnp.take_along_axis(gstart - lstart, e_of_j, axis=1) + j16[None, :]).astype(i32)
    dst = jnp.where(j16[None, :] < lend[:, -1:], dst, 0)
    nused = (rend[-1] // MOE_BM).astype(i32)
    brow = jnp.arange(n_blk_cap, dtype=i32) * MOE_BM
    blk_e = jnp.minimum(jnp.sum(rend[None, :] <= brow[:, None], axis=1), N_EXPERTS - 1)
    blk_e = jnp.where(jnp.arange(n_blk_cap) < nused, blk_e, blk_e[nused - 1]).astype(i32)
    k16 = jnp.arange(MOE_BM // SEG_ALIGN - 1, dtype=i32) * SEG_ALIGN
    zrow = (base + tot)[:, None] + k16[None, :]
    zvalid = zrow < rend[:, None]
    order = jnp.argsort(jnp.logical_not(zvalid).reshape(-1), stable=True)
    zdst = jnp.where(zvalid, zrow, 0).reshape(-1)[order].astype(i32)
    nz = jnp.sum(zvalid).astype(i32)
    return nch, dst, nz[None], zdst, blk_e, nused[None]


def _t5_bucket_np(dist):
    max_exact = N_BUCKETS // 2
    d = np.maximum(dist, 1).astype(np.float32)
    large = max_exact + (np.log(d / max_exact) / math.log(MAX_DISTANCE / max_exact)
                         * (N_BUCKETS - max_exact)).astype(np.int32)
    large = np.minimum(large, N_BUCKETS - 1)
    return np.where(dist < max_exact, dist, large)


def _attention_bias(rel_bias):
    qi = np.arange(ATT_BLOCK)[:, None]
    kj = np.arange(2 * ATT_BLOCK)[None, :]
    dist = qi + ATT_BLOCK - kj
    in_window = (dist >= 0) & (dist < WINDOW)
    bucket = _t5_bucket_np(np.clip(dist, 0, WINDOW - 1))
    onehot = (bucket[None] == np.arange(N_BUCKETS)[:, None, None]).astype(np.float32)
    bias = jnp.einsum('nh,nqk->hqk', rel_bias.astype(jnp.float32), jnp.asarray(onehot),
                      precision=lax.Precision.HIGHEST)
    return jnp.where(jnp.asarray(in_window)[None], bias, NEG_BIG)


def _block_diag_mean(width, block):
    idx = np.arange(width) // block
    return jnp.asarray((idx[:, None] == idx[None, :]).astype(np.float32) / block, dtype=jnp.bfloat16)


def _pack_w_in(w):
    hd = ATT_HEAD_DIM
    o = 0
    q = w[:, o:o + ATT_WIDTH]; o += ATT_WIDTH
    k = w[:, o:o + ATT_KV_HEADS * hd]; o += ATT_KV_HEADS * hd
    v = w[:, o:o + ATT_KV_HEADS * hd]; o += ATT_KV_HEADS * hd
    qm = w[:, o:o + ML_HEADS * ML_DQK]; o += ML_HEADS * ML_DQK
    km = w[:, o:o + ML_HEADS * ML_DQK]; o += ML_HEADS * ML_DQK
    vm = w[:, o:o + ML_WIDTH]; o += ML_WIDTH
    om = w[:, o:o + ML_WIDTH]; o += ML_WIDTH
    gates = w[:, o:o + 2 * ML_HEADS]
    dup = lambda a: jnp.concatenate([a[:, 0:hd], a[:, 0:hd], a[:, hd:2 * hd], a[:, hd:2 * hd]], axis=1)
    gpad = jnp.pad(gates, ((0, 0), (0, LANES - 2 * ML_HEADS)))
    packed = jnp.concatenate([q, dup(k), dup(v), qm, km, vm, om, gpad], axis=1)
    return _bf16(packed), _bf16(gates.T)


def kernel(x, rel_bias, norm_mix_g, w_in, q_norm_g, k_norm_g, attn_sink, conv_w, conv_b, gate_b,
           mlstm_norm_g, w_out, norm_ffn_g, w_router_group, b_router_group, w_router_expert,
           b_router_expert, w_gate, w_up, w_down):
    batch, seq_len, _ = x.shape
    n_tok = batch * seq_len
    assert seq_len % TM_PROJ == 0 and seq_len % TM_IN == 0 and seq_len % ML_STEP == 0
    f32 = jnp.float32
    bias = _attention_bias(rel_bias)
    bdq = _block_diag_mean(ATT_WIDTH, ATT_HEAD_DIM)
    bdk = _block_diag_mean(2 * LANES, ATT_HEAD_DIM)
    x2 = x.reshape(n_tok, D_MODEL)
    for l in range(DEPTH):
        w_pack, wgt = _pack_w_in(w_in[l])
        qg = (jnp.tile(q_norm_g[l].astype(f32), ATT_HEADS) * (ATT_HEAD_DIM ** -0.5))[None, :]
        kg = jnp.tile(k_norm_g[l].astype(f32), 2 * ATT_KV_HEADS)[None, :]
        gb = jnp.pad(gate_b[l].astype(f32), (0, LANES - 2 * ML_HEADS))[None, :]
        gbt = gate_b[l].astype(f32)[:, None]
        qn, kn, vd, qkm, vm, om, gates, gates_t = _in_proj(
            x2, norm_mix_g[l][None, :], w_pack, wgt, qg, kg, bdq, bdk,
            conv_w[l], conv_b[l][None, :], gb, gbt, batch, seq_len)
        att = _attention(qn, kn, vd, bias, attn_sink[l].astype(f32), batch, seq_len)
        r3 = lambda a: a.reshape(batch, seq_len, a.shape[-1])
        hm = _mlstm(r3(qkm), r3(vm), r3(om), r3(gates), gates_t,
                    mlstm_norm_g[l][None, :], batch, seq_len)
        n_rt = 4 * SUBLANES
        wrt = jnp.pad(jnp.concatenate([w_router_expert[l], w_router_group[l]], axis=1).astype(f32).T,
                      ((0, n_rt - N_EXPERTS - N_GROUPS), (0, 0)))
        brt = jnp.pad(jnp.concatenate([b_router_expert[l], b_router_group[l]]).astype(f32),
                      (0, n_rt - N_EXPERTS - N_GROUPS))[:, None]
        wrt_hi = _bf16(wrt)
        wrt = jnp.concatenate([wrt_hi, _bf16(wrt - wrt_hi.astype(f32))], axis=0)
        x1, t, row, col, cnt = _out_proj_router(x2, att, hm.reshape(n_tok, ML_WIDTH), _bf16(w_out[l]),
                                                norm_ffn_g[l][None, :], wrt, brt)
        _, n_blk_cap = _moe_capacity(n_tok)
        nch, dst, nz, zdst, blk_e, nused = _routing_tables(cnt[:, :, 0], n_blk_cap)
        xs = _dispatch(t, col, row, nch, dst, nz, zdst)
        ys = _moe(xs, blk_e, nused, _bf16(w_gate[l]), _bf16(w_up[l]), _bf16(w_down[l]))
        x2 = _combine(x1, col, ys, nch, dst)
    return x2.reshape(batch, seq_len, D_MODEL)
```

```python
import functools
import math

import jax
import jax.numpy as jnp
import numpy as np
from jax import lax
from jax.experimental import pallas as pl
from jax.experimental.pallas import tpu as pltpu

D_MODEL = 1024
DEPTH = 2
ATT_HEADS = 8
ATT_KV_HEADS = 2
ATT_HEAD_DIM = 64
ATT_WIDTH = ATT_HEADS * ATT_HEAD_DIM
WINDOW = 128
ATT_BLOCK = 128
N_BUCKETS = 32
MAX_DISTANCE = 128
ML_HEADS = 4
ML_DQK = 64
ML_DV = 128
ML_WIDTH = ML_HEADS * ML_DV
ML_CHUNK = 64
CONV_K = 4
N_GROUPS = 4
EXPERTS_PER_GROUP = 4
N_EXPERTS = N_GROUPS * EXPERTS_PER_GROUP
D_FF_EXPERT = 512
EPS = 1e-6

LANES = 128
SUBLANES = 8
NEG_BIG = -1e30
VMEM_LIMIT = 48 * 1024 * 1024

O_Q = 0
O_K = O_Q + ATT_WIDTH
O_V = O_K + ATT_KV_HEADS * ATT_HEAD_DIM
O_QM = O_V + ATT_KV_HEADS * ATT_HEAD_DIM
O_KM = O_QM + ML_HEADS * ML_DQK
O_VM = O_KM + ML_HEADS * ML_DQK
O_OM = O_VM + ML_WIDTH
O_G = O_OM + ML_WIDTH
N_IN = O_G + 2 * ML_HEADS

TM_IN = 1024
TM_PROJ = 512
ML_KCHUNK = 256
ML_STEP = ML_KCHUNK

SEG_ALIGN = 16
MOE_BM = 512
L_CAP = 2 * TM_PROJ + N_EXPERTS * SEG_ALIGN
N_CHUNK = L_CAP // SEG_ALIGN
D_XS = D_MODEL + LANES
DISPATCH_ROWS = 256
COMBINE_ROWS = 256
ROW_POS0, ROW_POS1, ROW_W0, ROW_W1, ROW_E0, ROW_E1 = 0, 1, 2, 3, 4, 5
COL_W0H, COL_W1H, COL_E0, COL_E1 = 8, 11, 14, 15


def _moe_capacity(n_tok):
    n_tiles = n_tok // TM_PROJ
    rows = 2 * n_tok + n_tiles * N_EXPERTS * (SEG_ALIGN - 1) + N_EXPERTS * (MOE_BM - SEG_ALIGN)
    n_blk = -(-rows // MOE_BM)
    return n_blk * MOE_BM, n_blk


def _bf16(a):
    return a.astype(jnp.bfloat16)


def _log_sigmoid(z):
    return jnp.minimum(z, 0.0) - jnp.log(1.0 + jnp.exp(-jnp.abs(z)))


def _in_proj_kernel(tiles_per_seq, x_ref, g_ref, w_ref, wgt_ref, qg_ref, kg_ref, bdq_ref, bdk_ref,
                    cw_ref, cb_ref, gbt_ref,
                    qn_ref, kn_ref, vd_ref, qkm_ref, vm_ref, om_ref, gatet_ref,
                    w_bf, conv_scr):
    i = pl.program_id(0)
    tm = x_ref.shape[0]

    @pl.when(i == 0)
    def _():
        w_bf[...] = _bf16(w_ref[...])

    x = x_ref[...]
    hn = x * lax.rsqrt(jnp.mean(x * x, axis=-1, keepdims=True) + EPS) * g_ref[...]
    hb = _bf16(hn)

    def proj(c0, width):
        return jnp.dot(hb, w_bf[:, c0:c0 + width], preferred_element_type=jnp.float32)

    q = proj(O_Q, ATT_WIDTH)
    q_ms = jnp.dot(_bf16(q * q), bdq_ref[...], preferred_element_type=jnp.float32)
    qn_ref[...] = _bf16(q * lax.rsqrt(q_ms + EPS) * qg_ref[...])
    kv = proj(O_K, 2 * LANES)
    k = kv[:, 0:LANES]
    v = kv[:, LANES:2 * LANES]
    k_ms = jnp.dot(_bf16(k * k), bdk_ref[...], preferred_element_type=jnp.float32)
    kn = k * lax.rsqrt(k_ms + EPS) * kg_ref[...]
    low = lax.broadcasted_iota(jnp.int32, kn.shape, 1) < ATT_HEAD_DIM

    def dup_heads(a):
        swapped = pltpu.roll(a, ATT_HEAD_DIM, axis=1)
        return jnp.concatenate([jnp.where(low, a, swapped), jnp.where(low, swapped, a)], axis=1)

    kn_ref[...] = _bf16(dup_heads(kn))
    vd_ref[...] = _bf16(dup_heads(v))

    qk = proj(O_QM, 2 * ML_HEADS * ML_DQK)

    @pl.when(i % tiles_per_seq == 0)
    def _():
        conv_scr[0:SUBLANES, :] = jnp.zeros((SUBLANES, qk.shape[1]), jnp.float32)

    @pl.when(i % tiles_per_seq != 0)
    def _():
        conv_scr[0:SUBLANES, :] = conv_scr[tm:tm + SUBLANES, :]

    conv_scr[SUBLANES:SUBLANES + tm, :] = qk
    y = qk * cw_ref[CONV_K - 1:CONV_K, :] + cb_ref[...]
    for j in range(CONV_K - 1):
        off = SUBLANES - (CONV_K - 1) + j
        y = y + conv_scr[off:off + tm, :] * cw_ref[j:j + 1, :]
    y = y * jax.nn.sigmoid(y)
    lane = lax.broadcasted_iota(jnp.int32, y.shape, 1)
    y = jnp.where(lane >= ML_HEADS * ML_DQK, y * (ML_DQK ** -0.5), y)
    qkm_ref[...] = _bf16(y)

    vm_ref[...] = _bf16(proj(O_VM, ML_WIDTH))
    om_ref[...] = _bf16(jax.nn.sigmoid(proj(O_OM, ML_WIDTH)))

    gt = lax.dot_general(wgt_ref[...], hb, (((1,), (1,)), ((), ())),
                         preferred_element_type=jnp.float32) + gbt_ref[...]
    grow = lax.broadcasted_iota(jnp.int32, gt.shape, 0)
    gatet_ref[...] = jnp.where(grow >= ML_HEADS, _log_sigmoid(gt), gt)


def _in_proj(x2, g, w_in, layer, wgt, qg, kg, bdq, bdk, cw, cb, gbt, batch, seq_len):
    t = x2.shape[0]
    tm = TM_IN
    n = t // tm
    tps = seq_len // tm
    row = lambda i: (i, 0)
    fix = lambda i: (0, 0)
    kv_w = 2 * ATT_KV_HEADS * ATT_HEAD_DIM
    out_shapes = (
        jax.ShapeDtypeStruct((t, ATT_WIDTH), jnp.bfloat16),
        jax.ShapeDtypeStruct((t, kv_w), jnp.bfloat16),
        jax.ShapeDtypeStruct((t, kv_w), jnp.bfloat16),
        jax.ShapeDtypeStruct((t, 2 * ML_HEADS * ML_DQK), jnp.bfloat16),
        jax.ShapeDtypeStruct((t, ML_WIDTH), jnp.bfloat16),
        jax.ShapeDtypeStruct((t, ML_WIDTH), jnp.bfloat16),
        jax.ShapeDtypeStruct((batch, SUBLANES, seq_len), jnp.float32),
    )
    in_specs = [
        pl.BlockSpec((tm, D_MODEL), row),
        pl.BlockSpec((1, D_MODEL), fix),
        pl.BlockSpec((None, D_MODEL, N_IN), lambda i: (layer, 0, 0), pipeline_mode=pl.Buffered(1)),
        pl.BlockSpec((SUBLANES, D_MODEL), fix),
        pl.BlockSpec((1, ATT_WIDTH), fix),
        pl.BlockSpec((1, LANES), fix),
        pl.BlockSpec((ATT_WIDTH, ATT_WIDTH), fix),
        pl.BlockSpec((LANES, LANES), fix),
        pl.BlockSpec((CONV_K, 2 * ML_HEADS * ML_DQK), fix),
        pl.BlockSpec((1, 2 * ML_HEADS * ML_DQK), fix),
        pl.BlockSpec((SUBLANES, 1), fix),
    ]
    out_specs = [
        pl.BlockSpec((tm, ATT_WIDTH), row),
        pl.BlockSpec((tm, kv_w), row),
        pl.BlockSpec((tm, kv_w), row),
        pl.BlockSpec((tm, 2 * ML_HEADS * ML_DQK), row),
        pl.BlockSpec((tm, ML_WIDTH), row),
        pl.BlockSpec((tm, ML_WIDTH), row),
        pl.BlockSpec((None, SUBLANES, tm), lambda i: (i // tps, 0, i % tps)),
    ]
    return pl.pallas_call(
        functools.partial(_in_proj_kernel, seq_len // tm),
        grid=(n,),
        in_specs=in_specs,
        out_specs=out_specs,
        out_shape=out_shapes,
        scratch_shapes=[pltpu.VMEM((D_MODEL, N_IN), jnp.bfloat16),
                        pltpu.VMEM((tm + 2 * SUBLANES, 2 * ML_HEADS * ML_DQK), jnp.float32)],
        compiler_params=pltpu.CompilerParams(dimension_semantics=("arbitrary",),
                                             vmem_limit_bytes=VMEM_LIMIT),
        name="in_proj",
    )(x2, g, w_in, wgt, qg, kg, bdq, bdk, cw, cb, gbt)


def _attn_kernel(sink_ref, q_ref, kp_ref, kc_ref, vp_ref, vc_ref, bias_ref, o_ref):
    i = pl.program_id(1)
    blk = q_ref.shape[0]
    lane = lax.broadcasted_iota(jnp.int32, (blk, LANES), 1)
    low = lane < ATT_HEAD_DIM
    col = lax.broadcasted_iota(jnp.int32, (blk, 2 * blk), 1)
    no_prev = jnp.logical_and(col < blk, i == 0)
    group = ATT_HEADS // ATT_KV_HEADS
    for pair in range(ATT_HEADS // 2):
        kv = (2 * pair) // group
        qp = q_ref[:, pair * LANES:(pair + 1) * LANES]
        kcat = jnp.concatenate([kp_ref[:, kv * LANES:(kv + 1) * LANES],
                                kc_ref[:, kv * LANES:(kv + 1) * LANES]], axis=0)
        vcat = jnp.concatenate([vp_ref[:, kv * LANES:(kv + 1) * LANES],
                                vc_ref[:, kv * LANES:(kv + 1) * LANES]], axis=0)
        halves = []
        for sub in range(2):
            h = 2 * pair + sub
            qm = jnp.where(low if sub == 0 else jnp.logical_not(low), qp, jnp.zeros_like(qp))
            s = lax.dot_general(qm, kcat, (((1,), (1,)), ((), ())),
                                preferred_element_type=jnp.float32)
            logits = jnp.where(no_prev, NEG_BIG, s + bias_ref[h])
            sink = sink_ref[h]
            m = jnp.maximum(jnp.max(logits, axis=-1, keepdims=True), sink)
            p = jnp.exp(logits - m)
            den = jnp.sum(p, axis=-1, keepdims=True) + jnp.exp(sink - m)
            o = jnp.dot(_bf16(p), vcat, preferred_element_type=jnp.float32)
            halves.append(o / den)
        o_ref[:, pair * LANES:(pair + 1) * LANES] = _bf16(jnp.where(low, halves[0], halves[1]))


def _attention(qn, kn, vd, bias, sink, batch, seq_len):
    nb = seq_len // ATT_BLOCK
    cur = lambda b, i, s: (b * nb + i, 0)
    prev = lambda b, i, s: (b * nb + jnp.maximum(i - 1, 0), 0)
    grid_spec = pltpu.PrefetchScalarGridSpec(
        num_scalar_prefetch=1,
        grid=(batch, nb),
        in_specs=[
            pl.BlockSpec((ATT_BLOCK, ATT_WIDTH), cur),
            pl.BlockSpec((ATT_BLOCK, 2 * LANES), prev),
            pl.BlockSpec((ATT_BLOCK, 2 * LANES), cur),
            pl.BlockSpec((ATT_BLOCK, 2 * LANES), prev),
            pl.BlockSpec((ATT_BLOCK, 2 * LANES), cur),
            pl.BlockSpec((ATT_HEADS, ATT_BLOCK, 2 * ATT_BLOCK), lambda b, i, s: (0, 0, 0)),
        ],
        out_specs=pl.BlockSpec((ATT_BLOCK, ATT_WIDTH), cur),
    )
    return pl.pallas_call(
        _attn_kernel,
        grid_spec=grid_spec,
        out_shape=jax.ShapeDtypeStruct((batch * seq_len, ATT_WIDTH), jnp.bfloat16),
        compiler_params=pltpu.CompilerParams(dimension_semantics=("arbitrary", "arbitrary"),
                                             vmem_limit_bytes=VMEM_LIMIT),
        name="swa_attention",
    )(sink, qn, kn, kn, vd, vd, bias)


def _mlstm_kernel(qk_ref, v_ref, o_ref, gt_ref, ng_ref, out_ref, c_scr, m_scr):
    step = pl.program_id(0)
    batch = qk_ref.shape[0]
    L = ML_KCHUNK
    pairs = ML_HEADS // 2
    hi = lax.Precision.HIGHEST

    @pl.when(step == 0)
    def _():
        c_scr[...] = jnp.zeros(c_scr.shape, jnp.float32)
        m_scr[...] = jnp.zeros(m_scr.shape, jnp.float32)

    r_i = lax.broadcasted_iota(jnp.int32, (L, L), 0)
    c_i = lax.broadcasted_iota(jnp.int32, (L, L), 1)
    causal = c_i <= r_i
    tril = causal.astype(jnp.float32)
    triu = (r_i <= c_i).astype(jnp.float32)
    lane = lax.broadcasted_iota(jnp.int32, (L, LANES), 1)
    low = lane < ML_DQK
    one_col = (lane == 0).astype(jnp.bfloat16)
    row2 = lax.broadcasted_iota(jnp.int32, (2 * ML_DQK, 1), 0)

    for ch in range(ML_STEP // L):
        r0 = ch * L
        for b in range(batch):
            gt = gt_ref[b, :, r0:r0 + L]
            g = jnp.concatenate([gt, jnp.zeros((LANES - SUBLANES, L), jnp.float32)], axis=0).T
            bcols = jnp.dot(tril, g, precision=hi, preferred_element_type=jnp.float32)
            brows = jnp.dot(gt, triu, precision=hi, preferred_element_type=jnp.float32)
            for pair in range(pairs):
                sidx = b * pairs + pair
                qp = qk_ref[b, r0:r0 + L, pair * LANES:(pair + 1) * LANES]
                kp = qk_ref[b, r0:r0 + L, (pairs + pair) * LANES:(pairs + pair + 1) * LANES]
                c_pair = c_scr[sidx]
                c_bf = _bf16(c_pair)
                new_c = None
                decays = []
                for sub in range(2):
                    h = 2 * pair + sub
                    sel = low if sub == 0 else jnp.logical_not(low)
                    m_prev = m_scr[b * ML_HEADS + h]
                    bc = bcols[:, ML_HEADS + h:ML_HEADS + h + 1]
                    br = brows[ML_HEADS + h:ML_HEADS + h + 1, :]
                    lic = g[:, h:h + 1]
                    lir = gt[h:h + 1, :]
                    log_d = jnp.where(causal, bc - br + lir, NEG_BIG)
                    m_inter = bc + m_prev
                    m_row = jnp.maximum(m_inter, jnp.max(log_d, axis=-1, keepdims=True))
                    d = jnp.exp(log_d - m_row)
                    inter = jnp.exp(m_inter - m_row)
                    qm = jnp.where(sel, qp, jnp.zeros_like(qp))
                    s = lax.dot_general(qm, kp, (((1,), (1,)), ((), ())),
                                        preferred_element_type=jnp.float32) * d
                    v_ext = jnp.concatenate([v_ref[b, r0:r0 + L, h * ML_DV:(h + 1) * ML_DV], one_col],
                                            axis=-1)
                    num = inter * jnp.dot(qm, c_bf, preferred_element_type=jnp.float32) \
                        + jnp.dot(_bf16(s), v_ext, preferred_element_type=jnp.float32)
                    den = num[:, ML_DV:ML_DV + 1]
                    hval = num[:, 0:ML_DV] / jnp.maximum(jnp.abs(den), jnp.exp(-m_row))
                    hn = hval * lax.rsqrt(jnp.mean(hval * hval, axis=-1, keepdims=True) + EPS)
                    hn = hn * ng_ref[:, h * ML_DV:(h + 1) * ML_DV]
                    out_ref[b, r0:r0 + L, h * ML_DV:(h + 1) * ML_DV] = _bf16(
                        hn * o_ref[b, r0:r0 + L, h * ML_DV:(h + 1) * ML_DV].astype(jnp.float32))
                    b_last = bc[L - 1:L, :]
                    log_w = b_last - bc + lic
                    m_next = jnp.maximum(b_last + m_prev, jnp.max(log_w, axis=0, keepdims=True))
                    w = jnp.exp(log_w - m_next)
                    decays.append(jnp.exp(b_last + m_prev - m_next))
                    m_scr[b * ML_HEADS + h] = m_next
                    kw = _bf16(jnp.where(sel, kp.astype(jnp.float32) * w, 0.0))
                    upd = lax.dot_general(kw, v_ext, (((0,), (0,)), ((), ())),
                                          preferred_element_type=jnp.float32)
                    new_c = upd if new_c is None else new_c + upd
                decay_rows = jnp.where(row2 < ML_DQK, decays[0], decays[1])
                c_scr[sidx] = decay_rows * c_pair + new_c


def _mlstm(qkm, vm, om, gates_t, ng, batch, seq_len):
    n_steps = seq_len // ML_STEP
    blk = lambda c: (0, c, 0)
    return pl.pallas_call(
        _mlstm_kernel,
        grid=(n_steps,),
        in_specs=[
            pl.BlockSpec((batch, ML_STEP, 2 * ML_HEADS * ML_DQK), blk),
            pl.BlockSpec((batch, ML_STEP, ML_WIDTH), blk),
            pl.BlockSpec((batch, ML_STEP, ML_WIDTH), blk),
            pl.BlockSpec((batch, SUBLANES, ML_STEP), lambda c: (0, 0, c)),
            pl.BlockSpec((1, ML_WIDTH), lambda c: (0, 0)),
        ],
        out_specs=pl.BlockSpec((batch, ML_STEP, ML_WIDTH), blk),
        out_shape=jax.ShapeDtypeStruct((batch, seq_len, ML_WIDTH), jnp.bfloat16),
        scratch_shapes=[
            pltpu.VMEM((batch * ML_HEADS // 2, 2 * ML_DQK, 2 * ML_DV), jnp.float32),
            pltpu.VMEM((batch * ML_HEADS, 1, 1), jnp.float32),
        ],
        compiler_params=pltpu.CompilerParams(dimension_semantics=("arbitrary",),
                                             vmem_limit_bytes=VMEM_LIMIT),
        name="mlstm_scan",
    )(qkm, vm, om, gates_t, ng)


def _out_proj_router_kernel(x_ref, att_ref, hm_ref, wo_ref, g_ref, wrt_ref, brt_ref,
                            x1_ref, t_ref, row_ref, col_ref, cnt_ref, wo_bf):
    tm = x_ref.shape[0]
    hi = lax.Precision.HIGHEST

    @pl.when(pl.program_id(0) == 0)
    def _():
        wo_bf[...] = _bf16(wo_ref[...])

    x1 = x_ref[...] \
        + jnp.dot(att_ref[...], wo_bf[0:ATT_WIDTH, :], preferred_element_type=jnp.float32) \
        + jnp.dot(hm_ref[...], wo_bf[ATT_WIDTH:, :], preferred_element_type=jnp.float32)
    x1_ref[...] = x1
    tn = x1 * lax.rsqrt(jnp.mean(x1 * x1, axis=-1, keepdims=True) + EPS) * g_ref[...]
    tn_hi = _bf16(tn)
    t_ref[...] = tn_hi
    tn_lo = _bf16(tn - tn_hi.astype(jnp.float32))
    nt = (((1,), (1,)), ((), ()))
    n_rt = wrt_ref.shape[0] // 2
    p_hi = lax.dot_general(wrt_ref[...], tn_hi, nt, preferred_element_type=jnp.float32)
    p_lo = lax.dot_general(wrt_ref[0:n_rt, :], tn_lo, nt, preferred_element_type=jnp.float32)
    logits = p_hi[0:n_rt, :] + p_hi[n_rt:, :] + p_lo + brt_ref[...]
    el_all = logits[0:N_EXPERTS, :]
    gl = logits[N_EXPERTS:N_EXPERTS + SUBLANES, :]
    grow = lax.broadcasted_iota(jnp.int32, gl.shape, 0).astype(jnp.float32)
    gl = jnp.where(grow < N_GROUPS, gl, NEG_BIG)
    gmax = jnp.max(gl, axis=0, keepdims=True)
    grp = jnp.min(jnp.where(gl == gmax, grow, float(N_GROUPS)), axis=0, keepdims=True)
    p_grp = 1.0 / jnp.sum(jnp.exp(gl - gmax), axis=0, keepdims=True)
    erow = lax.broadcasted_iota(jnp.int32, el_all.shape, 0).astype(jnp.float32)
    egrp = jnp.floor(erow * (1.0 / EXPERTS_PER_GROUP))
    el = jnp.where(egrp == grp, el_all, NEG_BIG)
    e1 = jnp.max(el, axis=0, keepdims=True)
    i1 = jnp.min(jnp.where(el == e1, erow, float(N_EXPERTS)), axis=0, keepdims=True)
    el2 = jnp.where(erow == i1, NEG_BIG, el)
    e2 = jnp.max(el2, axis=0, keepdims=True)
    i2 = jnp.min(jnp.where(el2 == e2, erow, float(N_EXPERTS)), axis=0, keepdims=True)
    z2 = jnp.exp(e2 - e1)
    w1 = p_grp / (1.0 + z2)
    w2 = p_grp * z2 / (1.0 + z2)
    sel1 = erow == i1
    sel2 = erow == i2
    onehot = jnp.logical_or(sel1, sel2)
    t_r = lax.broadcasted_iota(jnp.int32, (tm, tm), 0)
    t_c = lax.broadcasted_iota(jnp.int32, (tm, tm), 1)
    before = (t_r < t_c).astype(jnp.bfloat16)
    rank = jnp.dot(onehot.astype(jnp.bfloat16), before, preferred_element_type=jnp.float32)
    cnt = jnp.sum(onehot.astype(jnp.float32), axis=1, keepdims=True)
    cnt_al = jnp.floor((cnt + (SEG_ALIGN - 1)) * (1.0 / SEG_ALIGN)) * SEG_ALIGN
    e_r = lax.broadcasted_iota(jnp.int32, (N_EXPERTS, N_EXPERTS), 0)
    e_c = lax.broadcasted_iota(jnp.int32, (N_EXPERTS, N_EXPERTS), 1)
    lstart = jnp.dot((e_c < e_r).astype(jnp.float32), jnp.broadcast_to(cnt_al, (N_EXPERTS, LANES)),
                     precision=hi, preferred_element_type=jnp.float32)[:, 0:1]
    slot = lstart + rank
    pos1 = jnp.sum(jnp.where(sel1, slot, 0.0), axis=0, keepdims=True)
    pos2 = jnp.sum(jnp.where(sel2, slot, 0.0), axis=0, keepdims=True)
    r8 = lax.broadcasted_iota(jnp.int32, (SUBLANES, tm), 0)
    info = jnp.where(r8 == ROW_POS0, pos1, jnp.where(r8 == ROW_POS1, pos2, jnp.where(
        r8 == ROW_W0, w1, jnp.where(r8 == ROW_W1, w2, jnp.where(
            r8 == ROW_E0, i1, jnp.where(r8 == ROW_E1, i2, 0.0))))))
    row_ref[...] = info

    def split3(w):
        h = _bf16(w).astype(jnp.float32)
        m = _bf16(w - h).astype(jnp.float32)
        return h, m, _bf16(w - h - m).astype(jnp.float32)

    w1h, w1m, w1l = split3(w1)
    w2h, w2m, w2l = split3(w2)
    parts = jnp.where(r8 == 0, w1h, jnp.where(r8 == 1, w1m, jnp.where(r8 == 2, w1l, jnp.where(
        r8 == 3, w2h, jnp.where(r8 == 4, w2m, jnp.where(r8 == 5, w2l, jnp.where(r8 == 6, i1, i2)))))))
    col_ref[...] = jnp.concatenate(
        [info, parts, jnp.zeros((LANES - 2 * SUBLANES, tm), jnp.float32)], axis=0).T
    cnt_ref[...] = jnp.broadcast_to(cnt, (N_EXPERTS, LANES)).astype(jnp.int32)


def _out_proj_router(x2, att, hm, wo, layer, g, wrt, brt):
    t = x2.shape[0]
    tm = TM_PROJ
    row = lambda i: (i, 0)
    fix = lambda i: (0, 0)
    return pl.pallas_call(
        _out_proj_router_kernel,
        grid=(t // tm,),
        in_specs=[
            pl.BlockSpec((tm, D_MODEL), row),
            pl.BlockSpec((tm, ATT_WIDTH), row),
            pl.BlockSpec((tm, ML_WIDTH), row),
            pl.BlockSpec((None, D_MODEL, D_MODEL), lambda i: (layer, 0, 0), pipeline_mode=pl.Buffered(1)),
            pl.BlockSpec((1, D_MODEL), fix),
            pl.BlockSpec((8 * SUBLANES, D_MODEL), fix),
            pl.BlockSpec((4 * SUBLANES, 1), fix),
        ],
        out_specs=[
            pl.BlockSpec((tm, D_MODEL), row),
            pl.BlockSpec((tm, D_MODEL), row),
            pl.BlockSpec((SUBLANES, tm), lambda i: (0, i)),
            pl.BlockSpec((tm, LANES), row),
            pl.BlockSpec((None, N_EXPERTS, LANES), lambda i: (i, 0, 0)),
        ],
        out_shape=(
            jax.ShapeDtypeStruct((t, D_MODEL), jnp.float32),
            jax.ShapeDtypeStruct((t, D_MODEL), jnp.bfloat16),
            jax.ShapeDtypeStruct((SUBLANES, t), jnp.float32),
            jax.ShapeDtypeStruct((t, LANES), jnp.float32),
            jax.ShapeDtypeStruct((t // tm, N_EXPERTS, LANES), jnp.int32),
        ),
        scratch_shapes=[pltpu.VMEM((D_MODEL, D_MODEL), jnp.bfloat16)],
        compiler_params=pltpu.CompilerParams(dimension_semantics=("arbitrary",),
                                             vmem_limit_bytes=VMEM_LIMIT),
        name="out_proj_router",
    )(x2, att, hm, wo, g, wrt, brt)


def _chunk_rows(j):
    return pl.ds(pl.multiple_of(j * SEG_ALIGN, SEG_ALIGN), SEG_ALIGN)


def _dispatch_kernel(nch_ref, dst_ref, nz_ref, zdst_ref, t_ref, col_ref, row_ref, xs_hbm,
                     xbuf, zx, sem_x, sem_z):
    i = pl.program_id(0)
    n = pl.num_programs(0)
    p = i % 2
    tm = t_ref.shape[0]

    def copy(tile, par, j):
        d = pl.ds(pl.multiple_of(dst_ref[tile, j], SEG_ALIGN), SEG_ALIGN)
        return pltpu.make_async_copy(xbuf.at[par, _chunk_rows(j)], xs_hbm.at[d], sem_x.at[par])

    def zero_copy(j):
        d = pl.ds(pl.multiple_of(zdst_ref[j], SEG_ALIGN), SEG_ALIGN)
        return pltpu.make_async_copy(zx, xs_hbm.at[d], sem_z.at[0])

    def wait_tile(tile, par):
        def body(j, c):
            copy(tile, par, j).wait()
            return c
        lax.fori_loop(0, nch_ref[tile], body, 0)

    @pl.when(i == 0)
    def _():
        zx[...] = jnp.zeros(zx.shape, zx.dtype)

        def body(j, c):
            zero_copy(j).start()
            return c
        lax.fori_loop(0, nz_ref[0], body, 0)

    @pl.when(i >= 2)
    def _():
        wait_tile(i - 2, p)

    pos0 = row_ref[ROW_POS0:ROW_POS0 + 1, :].astype(jnp.int32)
    pos1 = row_ref[ROW_POS1:ROW_POS1 + 1, :].astype(jnp.int32)
    t = t_ref[...]
    side = _bf16(col_ref[...])
    rows = DISPATCH_ROWS
    for c in range(L_CAP // rows):
        r = lax.broadcasted_iota(jnp.int32, (rows, tm), 0) + c * rows
        perm = jnp.logical_or(r == pos0, r == pos1).astype(jnp.bfloat16)
        xbuf[p, c * rows:(c + 1) * rows, 0:D_MODEL] = _bf16(
            jnp.dot(perm, t, preferred_element_type=jnp.float32))
        xbuf[p, c * rows:(c + 1) * rows, D_MODEL:D_XS] = _bf16(
            jnp.dot(perm, side, preferred_element_type=jnp.float32))

    def start_body(j, c):
        copy(i, p, j).start()
        return c
    lax.fori_loop(0, nch_ref[i], start_body, 0)

    @pl.when(i == n - 1)
    def _():
        @pl.when(n >= 2)
        def _():
            wait_tile(i - 1, 1 - p)
        wait_tile(i, p)

        def body(j, c):
            zero_copy(j).wait()
            return c
        lax.fori_loop(0, nz_ref[0], body, 0)


def _dispatch(t, col, row, nch, dst, nz, zdst):
    n_tok = t.shape[0]
    r_cap, _ = _moe_capacity(n_tok)
    tm = TM_PROJ
    grid_spec = pltpu.PrefetchScalarGridSpec(
        num_scalar_prefetch=4,
        grid=(n_tok // tm,),
        in_specs=[
            pl.BlockSpec((tm, D_MODEL), lambda i, *_: (i, 0)),
            pl.BlockSpec((tm, LANES), lambda i, *_: (i, 0)),
            pl.BlockSpec((SUBLANES, tm), lambda i, *_: (0, i)),
        ],
        out_specs=pl.BlockSpec(memory_space=pl.ANY),
        scratch_shapes=[
            pltpu.VMEM((2, L_CAP, D_XS), jnp.bfloat16),
            pltpu.VMEM((SEG_ALIGN, D_XS), jnp.bfloat16),
            pltpu.SemaphoreType.DMA((2,)),
            pltpu.SemaphoreType.DMA((1,)),
        ],
    )
    return pl.pallas_call(
        _dispatch_kernel,
        grid_spec=grid_spec,
        out_shape=jax.ShapeDtypeStruct((r_cap, D_XS), jnp.bfloat16),
        compiler_params=pltpu.CompilerParams(dimension_semantics=("arbitrary",),
                                             vmem_limit_bytes=VMEM_LIMIT),
        name="moe_dispatch",
    )(nch, dst, nz, zdst, t, col, row)


def _moe_kernel(blk_e_ref, nused_ref, xs_ref, wg_ref, wu_ref, wd_ref, ys_ref, wg_bf, wu_bf, wd_bf):
    b = pl.program_id(0)
    new_expert = jnp.logical_or(b == 0, blk_e_ref[b] != blk_e_ref[jnp.maximum(b - 1, 0)])

    @pl.when(new_expert)
    def _():
        wg_bf[...] = _bf16(wg_ref[...])
        wu_bf[...] = _bf16(wu_ref[...])
        wd_bf[...] = _bf16(wd_ref[...])

    @pl.when(b < nused_ref[0])
    def _():
        x = xs_ref[:, 0:D_MODEL]
        a = jnp.dot(x, wg_bf[...], preferred_element_type=jnp.float32)
        u = jnp.dot(x, wu_bf[...], preferred_element_type=jnp.float32)
        h = a * jax.nn.sigmoid(a) * u
        y = jnp.dot(_bf16(h), wd_bf[...], preferred_element_type=jnp.float32)
        sd = xs_ref[:, D_MODEL:D_XS].astype(jnp.float32)
        e_blk = blk_e_ref[b].astype(jnp.float32)
        w0 = sd[:, COL_W0H:COL_W0H + 1] + sd[:, COL_W0H + 1:COL_W0H + 2] + sd[:, COL_W0H + 2:COL_W0H + 3]
        w1 = sd[:, COL_W1H:COL_W1H + 1] + sd[:, COL_W1H + 1:COL_W1H + 2] + sd[:, COL_W1H + 2:COL_W1H + 3]
        w = jnp.where(sd[:, COL_E0:COL_E0 + 1] == e_blk, w0, w1)
        ys_ref[...] = _bf16(y * w)


def _moe(xs, blk_e, nused, wg, wu, wd, layer):
    blk = lambda b, be, nu: (jnp.maximum(jnp.minimum(b, nu[0] - 1), 0), 0)
    wsel = lambda b, be, nu: (layer, be[b], 0, 0)
    r_cap = xs.shape[0]
    grid_spec = pltpu.PrefetchScalarGridSpec(
        num_scalar_prefetch=2,
        grid=(r_cap // MOE_BM,),
        in_specs=[
            pl.BlockSpec((MOE_BM, D_XS), blk),
            pl.BlockSpec((None, None, D_MODEL, D_FF_EXPERT), wsel),
            pl.BlockSpec((None, None, D_MODEL, D_FF_EXPERT), wsel),
            pl.BlockSpec((None, None, D_FF_EXPERT, D_MODEL), wsel),
        ],
        out_specs=pl.BlockSpec((MOE_BM, D_MODEL), blk),
        scratch_shapes=[
            pltpu.VMEM((D_MODEL, D_FF_EXPERT), jnp.bfloat16),
            pltpu.VMEM((D_MODEL, D_FF_EXPERT), jnp.bfloat16),
            pltpu.VMEM((D_FF_EXPERT, D_MODEL), jnp.bfloat16),
        ],
    )
    return pl.pallas_call(
        _moe_kernel,
        grid_spec=grid_spec,
        out_shape=jax.ShapeDtypeStruct((r_cap, D_MODEL), jnp.bfloat16),
        compiler_params=pltpu.CompilerParams(dimension_semantics=("arbitrary",),
                                             vmem_limit_bytes=VMEM_LIMIT),
        name="moe_experts",
    )(blk_e, nused, xs, wg, wu, wd)


def _combine_kernel(nch_ref, dst_ref, x1_ref, col_ref, ys_hbm, out_ref, ybuf, sem):
    i = pl.program_id(0)
    n = pl.num_programs(0)
    p = i % 2
    tm = x1_ref.shape[0]

    def copy(tile, par, j):
        s = pl.ds(pl.multiple_of(dst_ref[tile, j], SEG_ALIGN), SEG_ALIGN)
        return pltpu.make_async_copy(ys_hbm.at[s], ybuf.at[par, _chunk_rows(j)], sem.at[par])

    def start_tile(tile, par):
        def body(j, c):
            copy(tile, par, j).start()
            return c
        lax.fori_loop(0, nch_ref[tile], body, 0)

    @pl.when(i == 0)
    def _():
        ybuf[...] = jnp.zeros(ybuf.shape, ybuf.dtype)
        start_tile(0, 0)

    @pl.when(i + 1 < n)
    def _():
        start_tile(i + 1, 1 - p)

    def wait_body(j, c):
        copy(i, p, j).wait()
        return c
    lax.fori_loop(0, nch_ref[i], wait_body, 0)

    rows = COMBINE_ROWS
    for c in range(tm // rows):
        col = col_ref[c * rows:(c + 1) * rows, :]
        pos0 = col[:, ROW_POS0:ROW_POS0 + 1].astype(jnp.int32)
        pos1 = col[:, ROW_POS1:ROW_POS1 + 1].astype(jnp.int32)
        l = lax.broadcasted_iota(jnp.int32, (rows, L_CAP), 1)
        perm = jnp.logical_or(l == pos0, l == pos1).astype(jnp.bfloat16)
        y = jnp.dot(perm, ybuf[p], preferred_element_type=jnp.float32)
        out_ref[c * rows:(c + 1) * rows, :] = x1_ref[c * rows:(c + 1) * rows, :] + y


def _combine(x1, col, ys, nch, dst):
    n_tok = x1.shape[0]
    tm = TM_PROJ
    grid_spec = pltpu.PrefetchScalarGridSpec(
        num_scalar_prefetch=2,
        grid=(n_tok // tm,),
        in_specs=[
            pl.BlockSpec((tm, D_MODEL), lambda i, *_: (i, 0)),
            pl.BlockSpec((tm, LANES), lambda i, *_: (i, 0)),
            pl.BlockSpec(memory_space=pl.ANY),
        ],
        out_specs=pl.BlockSpec((tm, D_MODEL), lambda i, *_: (i, 0)),
        scratch_shapes=[
            pltpu.VMEM((2, L_CAP, D_MODEL), jnp.bfloat16),
            pltpu.SemaphoreType.DMA((2,)),
        ],
    )
    return pl.pallas_call(
        _combine_kernel,
        grid_spec=grid_spec,
        out_shape=jax.ShapeDtypeStruct((n_tok, D_MODEL), jnp.float32),
        compiler_params=pltpu.CompilerParams(dimension_semantics=("arbitrary",),
                                             vmem_limit_bytes=VMEM_LIMIT),
        name="moe_combine",
    )(nch, dst, x1, col, ys)


def _routing_tables(counts, n_blk_cap):
    i32 = jnp.int32
    ca = ((counts + (SEG_ALIGN - 1)) // SEG_ALIGN) * SEG_ALIGN
    lend = jnp.cumsum(ca, axis=1)
    lstart = lend - ca
    nch = (lend[:, -1] // SEG_ALIGN).astype(i32)
    tot = jnp.sum(ca, axis=0)
    region = ((tot + (MOE_BM - 1)) // MOE_BM) * MOE_BM
    rend = jnp.cumsum(region)
    base = rend - region
    gstart = base[None, :] + jnp.cumsum(ca, axis=0) - ca
    j16 = jnp.arange(N_CHUNK, dtype=i32) * SEG_ALIGN
    e_of_j = jnp.minimum(jnp.sum(lend[:, None, :] <= j16[None, :, None], axis=2), N_EXPERTS - 1)
    dst = (jnp.take_along_axis(gstart - lstart, e_of_j, axis=1) + j16[None, :]).astype(i32)
    dst = jnp.where(j16[None, :] < lend[:, -1:], dst, 0)
    nused = (rend[-1] // MOE_BM).astype(i32)
    brow = jnp.arange(n_blk_cap, dtype=i32) * MOE_BM
    blk_e = jnp.minimum(jnp.sum(rend[None, :] <= brow[:, None], axis=1), N_EXPERTS - 1)
    blk_e = jnp.where(jnp.arange(n_blk_cap) < nused, blk_e, blk_e[nused - 1]).astype(i32)
    k16 = jnp.arange(MOE_BM // SEG_ALIGN - 1, dtype=i32) * SEG_ALIGN
    zrow = (base + tot)[:, None] + k16[None, :]
    zvalid = zrow < rend[:, None]
    order = jnp.argsort(jnp.logical_not(zvalid).reshape(-1), stable=True)
    zdst = jnp.where(zvalid, zrow, 0).reshape(-1)[order].astype(i32)
    nz = jnp.sum(zvalid).astype(i32)
    return nch, dst, nz[None], zdst, blk_e, nused[None]


def _t5_bucket_np(dist):
    max_exact = N_BUCKETS // 2
    d = np.maximum(dist, 1).astype(np.float32)
    large = max_exact + (np.log(d / max_exact) / math.log(MAX_DISTANCE / max_exact)
                         * (N_BUCKETS - max_exact)).astype(np.int32)
    large = np.minimum(large, N_BUCKETS - 1)
    return np.where(dist < max_exact, dist, large)


def _attention_bias(rel_bias):
    qi = np.arange(ATT_BLOCK)[:, None]
    kj = np.arange(2 * ATT_BLOCK)[None, :]
    dist = qi + ATT_BLOCK - kj
    in_window = (dist >= 0) & (dist < WINDOW)
    bucket = _t5_bucket_np(np.clip(dist, 0, WINDOW - 1))
    onehot = (bucket[None] == np.arange(N_BUCKETS)[:, None, None]).astype(np.float32)
    bias = jnp.einsum('nh,nqk->hqk', rel_bias.astype(jnp.float32), jnp.asarray(onehot),
                      precision=lax.Precision.HIGHEST)
    return jnp.where(jnp.asarray(in_window)[None], bias, NEG_BIG)


def _block_diag_mean(width, block):
    idx = np.arange(width) // block
    return jnp.asarray((idx[:, None] == idx[None, :]).astype(np.float32) / block, dtype=jnp.bfloat16)


def kernel(x, rel_bias, norm_mix_g, w_in, q_norm_g, k_norm_g, attn_sink, conv_w, conv_b, gate_b,
           mlstm_norm_g, w_out, norm_ffn_g, w_router_group, b_router_group, w_router_expert,
           b_router_expert, w_gate, w_up, w_down):
    batch, seq_len, _ = x.shape
    n_tok = batch * seq_len
    assert seq_len % TM_PROJ == 0 and seq_len % TM_IN == 0 and seq_len % ML_STEP == 0
    f32 = jnp.float32
    bias = _attention_bias(rel_bias)
    bdq = _block_diag_mean(ATT_WIDTH, ATT_HEAD_DIM)
    bdk = _block_diag_mean(LANES, ATT_HEAD_DIM)
    x2 = x.reshape(n_tok, D_MODEL)
    for l in range(DEPTH):
        wgt = _bf16(w_in[l, :, O_G:N_IN].T)
        qg = (jnp.tile(q_norm_g[l].astype(f32), ATT_HEADS) * (ATT_HEAD_DIM ** -0.5))[None, :]
        kg = jnp.tile(k_norm_g[l].astype(f32), ATT_KV_HEADS)[None, :]
        gbt = gate_b[l].astype(f32)[:, None]
        qn, kn, vd, qkm, vm, om, gates_t = _in_proj(
            x2, norm_mix_g[l][None, :], w_in, l, wgt, qg, kg, bdq, bdk,
            conv_w[l], conv_b[l][None, :], gbt, batch, seq_len)
        att = _attention(qn, kn, vd, bias, attn_sink[l].astype(f32), batch, seq_len)
        r3 = lambda a: a.reshape(batch, seq_len, a.shape[-1])
        hm = _mlstm(r3(qkm), r3(vm), r3(om), gates_t, mlstm_norm_g[l][None, :], batch, seq_len)
        n_rt = 4 * SUBLANES
        wrt = jnp.pad(jnp.concatenate([w_router_expert[l], w_router_group[l]], axis=1).astype(f32).T,
                      ((0, n_rt - N_EXPERTS - N_GROUPS), (0, 0)))
        brt = jnp.pad(jnp.concatenate([b_router_expert[l], b_router_group[l]]).astype(f32),
                      (0, n_rt - N_EXPERTS - N_GROUPS))[:, None]
        wrt_hi = _bf16(wrt)
        wrt = jnp.concatenate([wrt_hi, _bf16(wrt - wrt_hi.astype(f32))], axis=0)
        x1, t, row, col, cnt = _out_proj_router(x2, att, hm.reshape(n_tok, ML_WIDTH), w_out, l,
                                                norm_ffn_g[l][None, :], wrt, brt)
        _, n_blk_cap = _moe_capacity(n_tok)
        nch, dst, nz, zdst, blk_e, nused = _routing_tables(cnt[:, :, 0], n_blk_cap)
        xs = _dispatch(t, col, row, nch, dst, nz, zdst)
        ys = _moe(xs, blk_e, nused, w_gate, w_up, w_down, l)
        x2 = _combine(x1, col, ys, nch, dst)
    return x2.reshape(batch, seq_len, D_MODEL)
```

```python
import functools
import math

import jax
import jax.numpy as jnp
import numpy as np
from jax import lax
from jax.experimental import pallas as pl
from jax.experimental.pallas import tpu as pltpu

D_MODEL = 1024
DEPTH = 2
ATT_HEADS = 8
ATT_KV_HEADS = 2
ATT_HEAD_DIM = 64
ATT_WIDTH = ATT_HEADS * ATT_HEAD_DIM
WINDOW = 128
ATT_BLOCK = 128
N_BUCKETS = 32
MAX_DISTANCE = 128
ML_HEADS = 4
ML_DQK = 64
ML_DV = 128
ML_WIDTH = ML_HEADS * ML_DV
ML_CHUNK = 64
CONV_K = 4
N_GROUPS = 4
EXPERTS_PER_GROUP = 4
N_EXPERTS = N_GROUPS * EXPERTS_PER_GROUP
D_FF_EXPERT = 512
EPS = 1e-6

LANES = 128
SUBLANES = 8
NEG_BIG = -1e30
VMEM_LIMIT = 48 * 1024 * 1024

O_Q = 0
O_K = O_Q + ATT_WIDTH
O_V = O_K + ATT_KV_HEADS * ATT_HEAD_DIM
O_QM = O_V + ATT_KV_HEADS * ATT_HEAD_DIM
O_KM = O_QM + ML_HEADS * ML_DQK
O_VM = O_KM + ML_HEADS * ML_DQK
O_OM = O_VM + ML_WIDTH
O_G = O_OM + ML_WIDTH
N_IN = O_G + 2 * ML_HEADS

TM_IN = 1024
IN_SUBBLOCKS = 2
TM_OUT = 1024
TM_PROJ = 512
ML_KCHUNK = 256
ML_STEP = ML_KCHUNK

SEG_ALIGN = 16
MOE_BM = 512
L_CAP = 2 * TM_PROJ + N_EXPERTS * SEG_ALIGN
N_CHUNK = L_CAP // SEG_ALIGN
D_XS = D_MODEL + LANES
DISPATCH_ROWS = 256
COMBINE_ROWS = 256
ROW_POS0, ROW_POS1, ROW_W0, ROW_W1, ROW_E0, ROW_E1 = 0, 1, 2, 3, 4, 5
COL_W0H, COL_W1H, COL_E0, COL_E1 = 8, 11, 14, 15


def _moe_capacity(n_tok):
    n_tiles = n_tok // TM_PROJ
    rows = 2 * n_tok + n_tiles * N_EXPERTS * (SEG_ALIGN - 1) + N_EXPERTS * (MOE_BM - SEG_ALIGN)
    n_blk = -(-rows // MOE_BM)
    return n_blk * MOE_BM, n_blk


def _bf16(a):
    return a.astype(jnp.bfloat16)


def _log_sigmoid(z):
    return jnp.minimum(z, 0.0) - jnp.log(1.0 + jnp.exp(-jnp.abs(z)))


def _in_proj_kernel(tiles_per_seq, x_ref, g_ref, w_ref, wgt_ref, qg_ref, kg_ref, bdq_ref, bdk_ref,
                    cw_ref, cb_ref, gbt_ref,
                    qn_ref, kn_ref, vd_ref, qkm_ref, vm_ref, om_ref, gatet_ref,
                    w_bf, conv_scr):
    i = pl.program_id(0)
    tm = x_ref.shape[0]

    @pl.when(i == 0)
    def _():
        w_bf[...] = _bf16(w_ref[...])

    conv_w = 2 * ML_HEADS * ML_DQK

    @pl.when(i % tiles_per_seq == 0)
    def _():
        conv_scr[0:SUBLANES, :] = jnp.zeros((SUBLANES, conv_w), jnp.float32)

    @pl.when(i % tiles_per_seq != 0)
    def _():
        conv_scr[0:SUBLANES, :] = conv_scr[tm:tm + SUBLANES, :]

    sub = tm // IN_SUBBLOCKS
    for sb in range(IN_SUBBLOCKS):
        rs = slice(sb * sub, (sb + 1) * sub)
        x = x_ref[rs, :]
        hn = x * lax.rsqrt(jnp.mean(x * x, axis=-1, keepdims=True) + EPS) * g_ref[...]
        hb = _bf16(hn)

        def proj(c0, width):
            return jnp.dot(hb, w_bf[:, c0:c0 + width], preferred_element_type=jnp.float32)

        q = proj(O_Q, ATT_WIDTH)
        q_ms = jnp.dot(_bf16(q * q), bdq_ref[...], preferred_element_type=jnp.float32)
        qn_ref[rs, :] = _bf16(q * lax.rsqrt(q_ms + EPS) * qg_ref[...])
        kv = proj(O_K, 2 * LANES)
        k = kv[:, 0:LANES]
        v = kv[:, LANES:2 * LANES]
        k_ms = jnp.dot(_bf16(k * k), bdk_ref[...], preferred_element_type=jnp.float32)
        kn = k * lax.rsqrt(k_ms + EPS) * kg_ref[...]
        low = lax.broadcasted_iota(jnp.int32, kn.shape, 1) < ATT_HEAD_DIM

        def dup_heads(a):
            swapped = pltpu.roll(a, ATT_HEAD_DIM, axis=1)
            return jnp.concatenate([jnp.where(low, a, swapped), jnp.where(low, swapped, a)], axis=1)

        kn_ref[rs, :] = _bf16(dup_heads(kn))
        vd_ref[rs, :] = _bf16(dup_heads(v))

        qk = proj(O_QM, conv_w)
        base = SUBLANES + sb * sub
        conv_scr[base:base + sub, :] = qk
        y = qk * cw_ref[CONV_K - 1:CONV_K, :] + cb_ref[...]
        for j in range(CONV_K - 1):
            off = base - (CONV_K - 1) + j
            y = y + conv_scr[off:off + sub, :] * cw_ref[j:j + 1, :]
        y = y * jax.nn.sigmoid(y)
        lane = lax.broadcasted_iota(jnp.int32, y.shape, 1)
        y = jnp.where(lane >= ML_HEADS * ML_DQK, y * (ML_DQK ** -0.5), y)
        qkm_ref[rs, :] = _bf16(y)

        vm_ref[rs, :] = _bf16(proj(O_VM, ML_WIDTH))
        om_ref[rs, :] = _bf16(jax.nn.sigmoid(proj(O_OM, ML_WIDTH)))

        gt = lax.dot_general(wgt_ref[...], hb, (((1,), (1,)), ((), ())),
                             preferred_element_type=jnp.float32) + gbt_ref[...]
        grow = lax.broadcasted_iota(jnp.int32, gt.shape, 0)
        gatet_ref[:, rs] = jnp.where(grow >= ML_HEADS, _log_sigmoid(gt), gt)


def _in_proj(x2, g, w_in, layer, wgt, qg, kg, bdq, bdk, cw, cb, gbt, batch, seq_len):
    t = x2.shape[0]
    tm = TM_IN
    n = t // tm
    tps = seq_len // tm
    row = lambda i: (i, 0)
    fix = lambda i: (0, 0)
    kv_w = 2 * ATT_KV_HEADS * ATT_HEAD_DIM
    out_shapes = (
        jax.ShapeDtypeStruct((t, ATT_WIDTH), jnp.bfloat16),
        jax.ShapeDtypeStruct((t, kv_w), jnp.bfloat16),
        jax.ShapeDtypeStruct((t, kv_w), jnp.bfloat16),
        jax.ShapeDtypeStruct((t, 2 * ML_HEADS * ML_DQK), jnp.bfloat16),
        jax.ShapeDtypeStruct((t, ML_WIDTH), jnp.bfloat16),
        jax.ShapeDtypeStruct((t, ML_WIDTH), jnp.bfloat16),
        jax.ShapeDtypeStruct((batch, SUBLANES, seq_len), jnp.float32),
    )
    in_specs = [
        pl.BlockSpec((tm, D_MODEL), row),
        pl.BlockSpec((1, D_MODEL), fix),
        pl.BlockSpec((None, D_MODEL, N_IN), lambda i: (layer, 0, 0), pipeline_mode=pl.Buffered(1)),
        pl.BlockSpec((SUBLANES, D_MODEL), fix),
        pl.BlockSpec((1, ATT_WIDTH), fix),
        pl.BlockSpec((1, LANES), fix),
        pl.BlockSpec((ATT_WIDTH, ATT_WIDTH), fix),
        pl.BlockSpec((LANES, LANES), fix),
        pl.BlockSpec((CONV_K, 2 * ML_HEADS * ML_DQK), fix),
        pl.BlockSpec((1, 2 * ML_HEADS * ML_DQK), fix),
        pl.BlockSpec((SUBLANES, 1), fix),
    ]
    out_specs = [
        pl.BlockSpec((tm, ATT_WIDTH), row),
        pl.BlockSpec((tm, kv_w), row),
        pl.BlockSpec((tm, kv_w), row),
        pl.BlockSpec((tm, 2 * ML_HEADS * ML_DQK), row),
        pl.BlockSpec((tm, ML_WIDTH), row),
        pl.BlockSpec((tm, ML_WIDTH), row),
        pl.BlockSpec((None, SUBLANES, tm), lambda i: (i // tps, 0, i % tps)),
    ]
    return pl.pallas_call(
        functools.partial(_in_proj_kernel, seq_len // tm),
        grid=(n,),
        in_specs=in_specs,
        out_specs=out_specs,
        out_shape=out_shapes,
        scratch_shapes=[pltpu.VMEM((D_MODEL, N_IN), jnp.bfloat16),
                        pltpu.VMEM((tm + 2 * SUBLANES, 2 * ML_HEADS * ML_DQK), jnp.float32)],
        compiler_params=pltpu.CompilerParams(dimension_semantics=("arbitrary",),
                                             vmem_limit_bytes=VMEM_LIMIT),
        name="in_proj",
    )(x2, g, w_in, wgt, qg, kg, bdq, bdk, cw, cb, gbt)


def _attn_kernel(sink_ref, q_ref, kp_ref, kc_ref, vp_ref, vc_ref, bias_ref, o_ref):
    i = pl.program_id(1)
    blk = q_ref.shape[0]
    lane = lax.broadcasted_iota(jnp.int32, (blk, LANES), 1)
    low = lane < ATT_HEAD_DIM
    col = lax.broadcasted_iota(jnp.int32, (blk, 2 * blk), 1)
    no_prev = jnp.logical_and(col < blk, i == 0)
    group = ATT_HEADS // ATT_KV_HEADS
    for pair in range(ATT_HEADS // 2):
        kv = (2 * pair) // group
        qp = q_ref[:, pair * LANES:(pair + 1) * LANES]
        kcat = jnp.concatenate([kp_ref[:, kv * LANES:(kv + 1) * LANES],
                                kc_ref[:, kv * LANES:(kv + 1) * LANES]], axis=0)
        vcat = jnp.concatenate([vp_ref[:, kv * LANES:(kv + 1) * LANES],
                                vc_ref[:, kv * LANES:(kv + 1) * LANES]], axis=0)
        halves = []
        for sub in range(2):
            h = 2 * pair + sub
            qm = jnp.where(low if sub == 0 else jnp.logical_not(low), qp, jnp.zeros_like(qp))
            s = lax.dot_general(qm, kcat, (((1,), (1,)), ((), ())),
                                preferred_element_type=jnp.float32)
            logits = jnp.where(no_prev, NEG_BIG, s + bias_ref[h])
            sink = sink_ref[h]
            m = jnp.maximum(jnp.max(logits, axis=-1, keepdims=True), sink)
            p = jnp.exp(logits - m)
            den = jnp.sum(p, axis=-1, keepdims=True) + jnp.exp(sink - m)
            o = jnp.dot(_bf16(p), vcat, preferred_element_type=jnp.float32)
            halves.append(o / den)
        o_ref[:, pair * LANES:(pair + 1) * LANES] = _bf16(jnp.where(low, halves[0], halves[1]))


def _attention(qn, kn, vd, bias, sink, batch, seq_len):
    nb = seq_len // ATT_BLOCK
    cur = lambda b, i, s: (b * nb + i, 0)
    prev = lambda b, i, s: (b * nb + jnp.maximum(i - 1, 0), 0)
    grid_spec = pltpu.PrefetchScalarGridSpec(
        num_scalar_prefetch=1,
        grid=(batch, nb),
        in_specs=[
            pl.BlockSpec((ATT_BLOCK, ATT_WIDTH), cur),
            pl.BlockSpec((ATT_BLOCK, 2 * LANES), prev),
            pl.BlockSpec((ATT_BLOCK, 2 * LANES), cur),
            pl.BlockSpec((ATT_BLOCK, 2 * LANES), prev),
            pl.BlockSpec((ATT_BLOCK, 2 * LANES), cur),
            pl.BlockSpec((ATT_HEADS, ATT_BLOCK, 2 * ATT_BLOCK), lambda b, i, s: (0, 0, 0)),
        ],
        out_specs=pl.BlockSpec((ATT_BLOCK, ATT_WIDTH), cur),
    )
    return pl.pallas_call(
        _attn_kernel,
        grid_spec=grid_spec,
        out_shape=jax.ShapeDtypeStruct((batch * seq_len, ATT_WIDTH), jnp.bfloat16),
        compiler_params=pltpu.CompilerParams(dimension_semantics=("arbitrary", "arbitrary"),
                                             vmem_limit_bytes=VMEM_LIMIT),
        name="swa_attention",
    )(sink, qn, kn, kn, vd, vd, bias)


def _mlstm_kernel(qk_ref, v_ref, o_ref, gt_ref, ng_ref, out_ref, c_scr, m_scr):
    step = pl.program_id(0)
    batch = qk_ref.shape[0]
    L = ML_KCHUNK
    pairs = ML_HEADS // 2
    hi = lax.Precision.HIGHEST

    @pl.when(step == 0)
    def _():
        c_scr[...] = jnp.zeros(c_scr.shape, jnp.float32)
        m_scr[...] = jnp.zeros(m_scr.shape, jnp.float32)

    r_i = lax.broadcasted_iota(jnp.int32, (L, L), 0)
    c_i = lax.broadcasted_iota(jnp.int32, (L, L), 1)
    causal = c_i <= r_i
    tril = causal.astype(jnp.float32)
    triu = (r_i <= c_i).astype(jnp.float32)
    lane = lax.broadcasted_iota(jnp.int32, (L, LANES), 1)
    low = lane < ML_DQK
    one_col = (lane == 0).astype(jnp.bfloat16)
    row2 = lax.broadcasted_iota(jnp.int32, (2 * ML_DQK, 1), 0)

    for ch in range(ML_STEP // L):
        r0 = ch * L
        for b in range(batch):
            gt = gt_ref[b, :, r0:r0 + L]
            g = jnp.concatenate([gt, jnp.zeros((LANES - SUBLANES, L), jnp.float32)], axis=0).T
            bcols = jnp.dot(tril, g, precision=hi, preferred_element_type=jnp.float32)
            brows = jnp.dot(gt, triu, precision=hi, preferred_element_type=jnp.float32)
            for pair in range(pairs):
                sidx = b * pairs + pair
                qp = qk_ref[b, r0:r0 + L, pair * LANES:(pair + 1) * LANES]
                kp = qk_ref[b, r0:r0 + L, (pairs + pair) * LANES:(pairs + pair + 1) * LANES]
                c_pair = c_scr[sidx]
                c_bf = _bf16(c_pair)
                new_c = None
                decays = []
                for sub in range(2):
                    h = 2 * pair + sub
                    sel = low if sub == 0 else jnp.logical_not(low)
                    m_prev = m_scr[b * ML_HEADS + h]
                    bc = bcols[:, ML_HEADS + h:ML_HEADS + h + 1]
                    br = brows[ML_HEADS + h:ML_HEADS + h + 1, :]
                    lic = g[:, h:h + 1]
                    lir = gt[h:h + 1, :]
                    log_d = jnp.where(causal, bc - br + lir, NEG_BIG)
                    m_inter = bc + m_prev
                    m_row = jnp.maximum(m_inter, jnp.max(log_d, axis=-1, keepdims=True))
                    d = jnp.exp(log_d - m_row)
                    inter = jnp.exp(m_inter - m_row)
                    qm = jnp.where(sel, qp, jnp.zeros_like(qp))
                    s = lax.dot_general(qm, kp, (((1,), (1,)), ((), ())),
                                        preferred_element_type=jnp.float32) * d
                    v_ext = jnp.concatenate([v_ref[b, r0:r0 + L, h * ML_DV:(h + 1) * ML_DV], one_col],
                                            axis=-1)
                    num = inter * jnp.dot(qm, c_bf, preferred_element_type=jnp.float32) \
                        + jnp.dot(_bf16(s), v_ext, preferred_element_type=jnp.float32)
                    den = num[:, ML_DV:ML_DV + 1]
                    hval = num[:, 0:ML_DV] / jnp.maximum(jnp.abs(den), jnp.exp(-m_row))
                    hn = hval * lax.rsqrt(jnp.mean(hval * hval, axis=-1, keepdims=True) + EPS)
                    hn = hn * ng_ref[:, h * ML_DV:(h + 1) * ML_DV]
                    out_ref[b, r0:r0 + L, h * ML_DV:(h + 1) * ML_DV] = _bf16(
                        hn * o_ref[b, r0:r0 + L, h * ML_DV:(h + 1) * ML_DV].astype(jnp.float32))
                    b_last = bc[L - 1:L, :]
                    log_w = b_last - bc + lic
                    m_next = jnp.maximum(b_last + m_prev, jnp.max(log_w, axis=0, keepdims=True))
                    w = jnp.exp(log_w - m_next)
                    decays.append(jnp.exp(b_last + m_prev - m_next))
                    m_scr[b * ML_HEADS + h] = m_next
                    kw = _bf16(jnp.where(sel, kp.astype(jnp.float32) * w, 0.0))
                    upd = lax.dot_general(kw, v_ext, (((0,), (0,)), ((), ())),
                                          preferred_element_type=jnp.float32)
                    new_c = upd if new_c is None else new_c + upd
                decay_rows = jnp.where(row2 < ML_DQK, decays[0], decays[1])
                c_scr[sidx] = decay_rows * c_pair + new_c


def _mlstm(qkm, vm, om, gates_t, ng, batch, seq_len):
    n_steps = seq_len // ML_STEP
    blk = lambda c: (0, c, 0)
    return pl.pallas_call(
        _mlstm_kernel,
        grid=(n_steps,),
        in_specs=[
            pl.BlockSpec((batch, ML_STEP, 2 * ML_HEADS * ML_DQK), blk),
            pl.BlockSpec((batch, ML_STEP, ML_WIDTH), blk),
            pl.BlockSpec((batch, ML_STEP, ML_WIDTH), blk),
            pl.BlockSpec((batch, SUBLANES, ML_STEP), lambda c: (0, 0, c)),
            pl.BlockSpec((1, ML_WIDTH), lambda c: (0, 0)),
        ],
        out_specs=pl.BlockSpec((batch, ML_STEP, ML_WIDTH), blk),
        out_shape=jax.ShapeDtypeStruct((batch, seq_len, ML_WIDTH), jnp.bfloat16),
        scratch_shapes=[
            pltpu.VMEM((batch * ML_HEADS // 2, 2 * ML_DQK, 2 * ML_DV), jnp.float32),
            pltpu.VMEM((batch * ML_HEADS, 1, 1), jnp.float32),
        ],
        compiler_params=pltpu.CompilerParams(dimension_semantics=("arbitrary",),
                                             vmem_limit_bytes=VMEM_LIMIT),
        name="mlstm_scan",
    )(qkm, vm, om, gates_t, ng)


def _out_proj_router_kernel(x_ref, att_ref, hm_ref, wo_ref, g_ref, wrt_ref, brt_ref,
                            x1_ref, t_ref, row_ref, col_ref, cnt_ref, wo_bf):
    tm = x_ref.shape[0]
    tr = TM_PROJ
    hi = lax.Precision.HIGHEST

    @pl.when(pl.program_id(0) == 0)
    def _():
        wo_bf[...] = _bf16(wo_ref[...])

    for sb in range(tm // TM_PROJ):
        rs = slice(sb * TM_PROJ, (sb + 1) * TM_PROJ)
        x1 = x_ref[rs, :] \
            + jnp.dot(att_ref[rs, :], wo_bf[0:ATT_WIDTH, :], preferred_element_type=jnp.float32) \
            + jnp.dot(hm_ref[rs, :], wo_bf[ATT_WIDTH:, :], preferred_element_type=jnp.float32)
        x1_ref[rs, :] = x1
        tn = x1 * lax.rsqrt(jnp.mean(x1 * x1, axis=-1, keepdims=True) + EPS) * g_ref[...]
        tn_hi = _bf16(tn)
        t_ref[rs, :] = tn_hi
        tn_lo = _bf16(tn - tn_hi.astype(jnp.float32))
        nt = (((1,), (1,)), ((), ()))
        n_rt = wrt_ref.shape[0] // 2
        p_hi = lax.dot_general(wrt_ref[...], tn_hi, nt, preferred_element_type=jnp.float32)
        p_lo = lax.dot_general(wrt_ref[0:n_rt, :], tn_lo, nt, preferred_element_type=jnp.float32)
        logits = p_hi[0:n_rt, :] + p_hi[n_rt:, :] + p_lo + brt_ref[...]
        el_all = logits[0:N_EXPERTS, :]
        gl = logits[N_EXPERTS:N_EXPERTS + SUBLANES, :]
        grow = lax.broadcasted_iota(jnp.int32, gl.shape, 0).astype(jnp.float32)
        gl = jnp.where(grow < N_GROUPS, gl, NEG_BIG)
        gmax = jnp.max(gl, axis=0, keepdims=True)
        grp = jnp.min(jnp.where(gl == gmax, grow, float(N_GROUPS)), axis=0, keepdims=True)
        p_grp = 1.0 / jnp.sum(jnp.exp(gl - gmax), axis=0, keepdims=True)
        erow = lax.broadcasted_iota(jnp.int32, el_all.shape, 0).astype(jnp.float32)
        egrp = jnp.floor(erow * (1.0 / EXPERTS_PER_GROUP))
        el = jnp.where(egrp == grp, el_all, NEG_BIG)
        e1 = jnp.max(el, axis=0, keepdims=True)
        i1 = jnp.min(jnp.where(el == e1, erow, float(N_EXPERTS)), axis=0, keepdims=True)
        el2 = jnp.where(erow == i1, NEG_BIG, el)
        e2 = jnp.max(el2, axis=0, keepdims=True)
        i2 = jnp.min(jnp.where(el2 == e2, erow, float(N_EXPERTS)), axis=0, keepdims=True)
        z2 = jnp.exp(e2 - e1)
        w1 = p_grp / (1.0 + z2)
        w2 = p_grp * z2 / (1.0 + z2)
        sel1 = erow == i1
        sel2 = erow == i2
        onehot = jnp.logical_or(sel1, sel2)
        t_r = lax.broadcasted_iota(jnp.int32, (tr, tr), 0)
        t_c = lax.broadcasted_iota(jnp.int32, (tr, tr), 1)
        before = (t_r < t_c).astype(jnp.bfloat16)
        rank = jnp.dot(onehot.astype(jnp.bfloat16), before, preferred_element_type=jnp.float32)
        cnt = jnp.sum(onehot.astype(jnp.float32), axis=1, keepdims=True)
        cnt_al = jnp.floor((cnt + (SEG_ALIGN - 1)) * (1.0 / SEG_ALIGN)) * SEG_ALIGN
        e_r = lax.broadcasted_iota(jnp.int32, (N_EXPERTS, N_EXPERTS), 0)
        e_c = lax.broadcasted_iota(jnp.int32, (N_EXPERTS, N_EXPERTS), 1)
        lstart = jnp.dot((e_c < e_r).astype(jnp.float32), jnp.broadcast_to(cnt_al, (N_EXPERTS, LANES)),
                         precision=hi, preferred_element_type=jnp.float32)[:, 0:1]
        slot = lstart + rank
        pos1 = jnp.sum(jnp.where(sel1, slot, 0.0), axis=0, keepdims=True)
        pos2 = jnp.sum(jnp.where(sel2, slot, 0.0), axis=0, keepdims=True)
        r8 = lax.broadcasted_iota(jnp.int32, (SUBLANES, tr), 0)
        info = jnp.where(r8 == ROW_POS0, pos1, jnp.where(r8 == ROW_POS1, pos2, jnp.where(
            r8 == ROW_W0, w1, jnp.where(r8 == ROW_W1, w2, jnp.where(
                r8 == ROW_E0, i1, jnp.where(r8 == ROW_E1, i2, 0.0))))))
        row_ref[:, rs] = info

        def split3(w):
            h = _bf16(w).astype(jnp.float32)
            m = _bf16(w - h).astype(jnp.float32)
            return h, m, _bf16(w - h - m).astype(jnp.float32)

        w1h, w1m, w1l = split3(w1)
        w2h, w2m, w2l = split3(w2)
        parts = jnp.where(r8 == 0, w1h, jnp.where(r8 == 1, w1m, jnp.where(r8 == 2, w1l, jnp.where(
            r8 == 3, w2h, jnp.where(r8 == 4, w2m, jnp.where(r8 == 5, w2l, jnp.where(r8 == 6, i1, i2)))))))
        col_ref[rs, :] = jnp.concatenate(
            [info, parts, jnp.zeros((LANES - 2 * SUBLANES, tr), jnp.float32)], axis=0).T
        cnt_ref[sb] = jnp.broadcast_to(cnt, (N_EXPERTS, LANES)).astype(jnp.int32)


def _out_proj_router(x2, att, hm, wo, layer, g, wrt, brt):
    t = x2.shape[0]
    tm = TM_OUT
    row = lambda i: (i, 0)
    fix = lambda i: (0, 0)
    return pl.pallas_call(
        _out_proj_router_kernel,
        grid=(t // tm,),
        in_specs=[
            pl.BlockSpec((tm, D_MODEL), row),
            pl.BlockSpec((tm, ATT_WIDTH), row),
            pl.BlockSpec((tm, ML_WIDTH), row),
            pl.BlockSpec((None, D_MODEL, D_MODEL), lambda i: (layer, 0, 0), pipeline_mode=pl.Buffered(1)),
            pl.BlockSpec((1, D_MODEL), fix),
            pl.BlockSpec((8 * SUBLANES, D_MODEL), fix),
            pl.BlockSpec((4 * SUBLANES, 1), fix),
        ],
        out_specs=[
            pl.BlockSpec((tm, D_MODEL), row),
            pl.BlockSpec((tm, D_MODEL), row),
            pl.BlockSpec((SUBLANES, tm), lambda i: (0, i)),
            pl.BlockSpec((tm, LANES), row),
            pl.BlockSpec((tm // TM_PROJ, N_EXPERTS, LANES), lambda i: (i, 0, 0)),
        ],
        out_shape=(
            jax.ShapeDtypeStruct((t, D_MODEL), jnp.float32),
            jax.ShapeDtypeStruct((t, D_MODEL), jnp.bfloat16),
            jax.ShapeDtypeStruct((SUBLANES, t), jnp.float32),
            jax.ShapeDtypeStruct((t, LANES), jnp.float32),
            jax.ShapeDtypeStruct((t // TM_PROJ, N_EXPERTS, LANES), jnp.int32),
        ),
        scratch_shapes=[pltpu.VMEM((D_MODEL, D_MODEL), jnp.bfloat16)],
        compiler_params=pltpu.CompilerParams(dimension_semantics=("arbitrary",),
                                             vmem_limit_bytes=VMEM_LIMIT),
        name="out_proj_router",
    )(x2, att, hm, wo, g, wrt, brt)


def _chunk_rows(j):
    return pl.ds(pl.multiple_of(j * SEG_ALIGN, SEG_ALIGN), SEG_ALIGN)


def _dispatch_kernel(nch_ref, dst_ref, nz_ref, zdst_ref, t_ref, col_ref, row_ref, xs_hbm,
                     xbuf, zx, sem_x, sem_z):
    i = pl.program_id(0)
    n = pl.num_programs(0)
    p = i % 2
    tm = t_ref.shape[0]

    def copy(tile, par, j):
        d = pl.ds(pl.multiple_of(dst_ref[tile, j], SEG_ALIGN), SEG_ALIGN)
        return pltpu.make_async_copy(xbuf.at[par, _chunk_rows(j)], xs_hbm.at[d], sem_x.at[par])

    def zero_copy(j):
        d = pl.ds(pl.multiple_of(zdst_ref[j], SEG_ALIGN), SEG_ALIGN)
        return pltpu.make_async_copy(zx, xs_hbm.at[d], sem_z.at[0])

    def wait_tile(tile, par):
        def body(j, c):
            copy(tile, par, j).wait()
            return c
        lax.fori_loop(0, nch_ref[tile], body, 0)

    @pl.when(i == 0)
    def _():
        zx[...] = jnp.zeros(zx.shape, zx.dtype)

        def body(j, c):
            zero_copy(j).start()
            return c
        lax.fori_loop(0, nz_ref[0], body, 0)

    @pl.when(i >= 2)
    def _():
        wait_tile(i - 2, p)

    pos0 = row_ref[ROW_POS0:ROW_POS0 + 1, :].astype(jnp.int32)
    pos1 = row_ref[ROW_POS1:ROW_POS1 + 1, :].astype(jnp.int32)
    t = t_ref[...]
    side = _bf16(col_ref[...])
    rows = DISPATCH_ROWS
    for c in range(L_CAP // rows):
        r = lax.broadcasted_iota(jnp.int32, (rows, tm), 0) + c * rows
        perm = jnp.logical_or(r == pos0, r == pos1).astype(jnp.bfloat16)
        xbuf[p, c * rows:(c + 1) * rows, 0:D_MODEL] = _bf16(
            jnp.dot(perm, t, preferred_element_type=jnp.float32))
        xbuf[p, c * rows:(c + 1) * rows, D_MODEL:D_XS] = _bf16(
            jnp.dot(perm, side, preferred_element_type=jnp.float32))

    def start_body(j, c):
        copy(i, p, j).start()
        return c
    lax.fori_loop(0, nch_ref[i], start_body, 0)

    @pl.when(i == n - 1)
    def _():
        @pl.when(n >= 2)
        def _():
            wait_tile(i - 1, 1 - p)
        wait_tile(i, p)

        def body(j, c):
            zero_copy(j).wait()
            return c
        lax.fori_loop(0, nz_ref[0], body, 0)


def _dispatch(t, col, row, nch, dst, nz, zdst):
    n_tok = t.shape[0]
    r_cap, _ = _moe_capacity(n_tok)
    tm = TM_PROJ
    grid_spec = pltpu.PrefetchScalarGridSpec(
        num_scalar_prefetch=4,
        grid=(n_tok // tm,),
        in_specs=[
            pl.BlockSpec((tm, D_MODEL), lambda i, *_: (i, 0)),
            pl.BlockSpec((tm, LANES), lambda i, *_: (i, 0)),
            pl.BlockSpec((SUBLANES, tm), lambda i, *_: (0, i)),
        ],
        out_specs=pl.BlockSpec(memory_space=pl.ANY),
        scratch_shapes=[
            pltpu.VMEM((2, L_CAP, D_XS), jnp.bfloat16),
            pltpu.VMEM((SEG_ALIGN, D_XS), jnp.bfloat16),
            pltpu.SemaphoreType.DMA((2,)),
            pltpu.SemaphoreType.DMA((1,)),
        ],
    )
    return pl.pallas_call(
        _dispatch_kernel,
        grid_spec=grid_spec,
        out_shape=jax.ShapeDtypeStruct((r_cap, D_XS), jnp.bfloat16),
        compiler_params=pltpu.CompilerParams(dimension_semantics=("arbitrary",),
                                             vmem_limit_bytes=VMEM_LIMIT),
        name="moe_dispatch",
    )(nch, dst, nz, zdst, t, col, row)


def _moe_kernel(blk_e_ref, nused_ref, xs_ref, wg_ref, wu_ref, wd_ref, ys_ref, wg_bf, wu_bf, wd_bf):
    b = pl.program_id(0)
    new_expert = jnp.logical_or(b == 0, blk_e_ref[b] != blk_e_ref[jnp.maximum(b - 1, 0)])

    @pl.when(new_expert)
    def _():
        wg_bf[...] = _bf16(wg_ref[...])
        wu_bf[...] = _bf16(wu_ref[...])
        wd_bf[...] = _bf16(wd_ref[...])

    @pl.when(b < nused_ref[0])
    def _():
        x = xs_ref[:, 0:D_MODEL]
        a = jnp.dot(x, wg_bf[...], preferred_element_type=jnp.float32)
        u = jnp.dot(x, wu_bf[...], preferred_element_type=jnp.float32)
        h = a * jax.nn.sigmoid(a) * u
        y = jnp.dot(_bf16(h), wd_bf[...], preferred_element_type=jnp.float32)
        sd = xs_ref[:, D_MODEL:D_XS].astype(jnp.float32)
        e_blk = blk_e_ref[b].astype(jnp.float32)
        w0 = sd[:, COL_W0H:COL_W0H + 1] + sd[:, COL_W0H + 1:COL_W0H + 2] + sd[:, COL_W0H + 2:COL_W0H + 3]
        w1 = sd[:, COL_W1H:COL_W1H + 1] + sd[:, COL_W1H + 1:COL_W1H + 2] + sd[:, COL_W1H + 2:COL_W1H + 3]
        w = jnp.where(sd[:, COL_E0:COL_E0 + 1] == e_blk, w0, w1)
        ys_ref[...] = _bf16(y * w)


def _moe(xs, blk_e, nused, wg, wu, wd, layer):
    blk = lambda b, be, nu: (jnp.maximum(jnp.minimum(b, nu[0] - 1), 0), 0)
    wsel = lambda b, be, nu: (layer, be[b], 0, 0)
    r_cap = xs.shape[0]
    grid_spec = pltpu.PrefetchScalarGridSpec(
        num_scalar_prefetch=2,
        grid=(r_cap // MOE_BM,),
        in_specs=[
            pl.BlockSpec((MOE_BM, D_XS), blk),
            pl.BlockSpec((None, None, D_MODEL, D_FF_EXPERT), wsel),
            pl.BlockSpec((None, None, D_MODEL, D_FF_EXPERT), wsel),
            pl.BlockSpec((None, None, D_FF_EXPERT, D_MODEL), wsel),
        ],
        out_specs=pl.BlockSpec((MOE_BM, D_MODEL), blk),
        scratch_shapes=[
            pltpu.VMEM((D_MODEL, D_FF_EXPERT), jnp.bfloat16),
            pltpu.VMEM((D_MODEL, D_FF_EXPERT), jnp.bfloat16),
            pltpu.VMEM((D_FF_EXPERT, D_MODEL), jnp.bfloat16),
        ],
    )
    return pl.pallas_call(
        _moe_kernel,
        grid_spec=grid_spec,
        out_shape=jax.ShapeDtypeStruct((r_cap, D_MODEL), jnp.bfloat16),
        compiler_params=pltpu.CompilerParams(dimension_semantics=("arbitrary",),
                                             vmem_limit_bytes=VMEM_LIMIT),
        name="moe_experts",
    )(blk_e, nused, xs, wg, wu, wd)


def _combine_kernel(nch_ref, dst_ref, x1_ref, col_ref, ys_hbm, out_ref, ybuf, sem):
    i = pl.program_id(0)
    n = pl.num_programs(0)
    p = i % 2
    tm = x1_ref.shape[0]

    def copy(tile, par, j):
        s = pl.ds(pl.multiple_of(dst_ref[tile, j], SEG_ALIGN), SEG_ALIGN)
        return pltpu.make_async_copy(ys_hbm.at[s], ybuf.at[par, _chunk_rows(j)], sem.at[par])

    def start_tile(tile, par):
        def body(j, c):
            copy(tile, par, j).start()
            return c
        lax.fori_loop(0, nch_ref[tile], body, 0)

    @pl.when(i == 0)
    def _():
        ybuf[...] = jnp.zeros(ybuf.shape, ybuf.dtype)
        start_tile(0, 0)

    @pl.when(i + 1 < n)
    def _():
        start_tile(i + 1, 1 - p)

    def wait_body(j, c):
        copy(i, p, j).wait()
        return c
    lax.fori_loop(0, nch_ref[i], wait_body, 0)

    rows = COMBINE_ROWS
    for c in range(tm // rows):
        col = col_ref[c * rows:(c + 1) * rows, :]
        pos0 = col[:, ROW_POS0:ROW_POS0 + 1].astype(jnp.int32)
        pos1 = col[:, ROW_POS1:ROW_POS1 + 1].astype(jnp.int32)
        l = lax.broadcasted_iota(jnp.int32, (rows, L_CAP), 1)
        perm = jnp.logical_or(l == pos0, l == pos1).astype(jnp.bfloat16)
        y = jnp.dot(perm, ybuf[p], preferred_element_type=jnp.float32)
        out_ref[c * rows:(c + 1) * rows, :] = x1_ref[c * rows:(c + 1) * rows, :] + y


def _combine(x1, col, ys, nch, dst):
    n_tok = x1.shape[0]
    tm = TM_PROJ
    grid_spec = pltpu.PrefetchScalarGridSpec(
        num_scalar_prefetch=2,
        grid=(n_tok // tm,),
        in_specs=[
            pl.BlockSpec((tm, D_MODEL), lambda i, *_: (i, 0)),
            pl.BlockSpec((tm, LANES), lambda i, *_: (i, 0)),
            pl.BlockSpec(memory_space=pl.ANY),
        ],
        out_specs=pl.BlockSpec((tm, D_MODEL), lambda i, *_: (i, 0)),
        scratch_shapes=[
            pltpu.VMEM((2, L_CAP, D_MODEL), jnp.bfloat16),
            pltpu.SemaphoreType.DMA((2,)),
        ],
    )
    return pl.pallas_call(
        _combine_kernel,
        grid_spec=grid_spec,
        out_shape=jax.ShapeDtypeStruct((n_tok, D_MODEL), jnp.float32),
        compiler_params=pltpu.CompilerParams(dimension_semantics=("arbitrary",),
                                             vmem_limit_bytes=VMEM_LIMIT),
        name="moe_combine",
    )(nch, dst, x1, col, ys)


def _routing_tables(counts, n_blk_cap):
    i32 = jnp.int32
    ca = ((counts + (SEG_ALIGN - 1)) // SEG_ALIGN) * SEG_ALIGN
    lend = jnp.cumsum(ca, axis=1)
    lstart = lend - ca
    nch = (lend[:, -1] // SEG_ALIGN).astype(i32)
    tot = jnp.sum(ca, axis=0)
    region = ((tot + (MOE_BM - 1)) // MOE_BM) * MOE_BM
    rend = jnp.cumsum(region)
    base = rend - region
    gstart = base[None, :] + jnp.cumsum(ca, axis=0) - ca
    j16 = jnp.arange(N_CHUNK, dtype=i32) * SEG_ALIGN
    e_of_j = jnp.minimum(jnp.sum(lend[:, None, :] <= j16[None, :, None], axis=2), N_EXPERTS - 1)
    dst = (jnp.take_along_axis(gstart - lstart, e_of_j, axis=1) + j16[None, :]).astype(i32)
    dst = jnp.where(j16[None, :] < lend[:, -1:], dst, 0)
    nused = (rend[-1] // MOE_BM).astype(i32)
    brow = jnp.arange(n_blk_cap, dtype=i32) * MOE_BM
    blk_e = jnp.minimum(jnp.sum(rend[None, :] <= brow[:, None], axis=1), N_EXPERTS - 1)
    blk_e = jnp.where(jnp.arange(n_blk_cap) < nused, blk_e, blk_e[nused - 1]).astype(i32)
    k16 = jnp.arange(MOE_BM // SEG_ALIGN - 1, dtype=i32) * SEG_ALIGN
    zrow = (base + tot)[:, None] + k16[None, :]
    zvalid = zrow < rend[:, None]
    order = jnp.argsort(jnp.logical_not(zvalid).reshape(-1), stable=True)
    zdst = jnp.where(zvalid, zrow, 0).reshape(-1)[order].astype(i32)
    nz = jnp.sum(zvalid).astype(i32)
    return nch, dst, nz[None], zdst, blk_e, nused[None]


def _t5_bucket_np(dist):
    max_exact = N_BUCKETS // 2
    d = np.maximum(dist, 1).astype(np.float32)
    large = max_exact + (np.log(d / max_exact) / math.log(MAX_DISTANCE / max_exact)
                         * (N_BUCKETS - max_exact)).astype(np.int32)
    large = np.minimum(large, N_BUCKETS - 1)
    return np.where(dist < max_exact, dist, large)


def _attention_bias(rel_bias):
    qi = np.arange(ATT_BLOCK)[:, None]
    kj = np.arange(2 * ATT_BLOCK)[None, :]
    dist = qi + ATT_BLOCK - kj
    in_window = (dist >= 0) & (dist < WINDOW)
    bucket = _t5_bucket_np(np.clip(dist, 0, WINDOW - 1))
    onehot = (bucket[None] == np.arange(N_BUCKETS)[:, None, None]).astype(np.float32)
    bias = jnp.einsum('nh,nqk->hqk', rel_bias.astype(jnp.float32), jnp.asarray(onehot),
                      precision=lax.Precision.HIGHEST)
    return jnp.where(jnp.asarray(in_window)[None], bias, NEG_BIG)


def _block_diag_mean(width, block):
    idx = np.arange(width) // block
    return jnp.asarray((idx[:, None] == idx[None, :]).astype(np.float32) / block, dtype=jnp.bfloat16)


def kernel(x, rel_bias, norm_mix_g, w_in, q_norm_g, k_norm_g, attn_sink, conv_w, conv_b, gate_b,
           mlstm_norm_g, w_out, norm_ffn_g, w_router_group, b_router_group, w_router_expert,
           b_router_expert, w_gate, w_up, w_down):
    batch, seq_len, _ = x.shape
    n_tok = batch * seq_len
    assert seq_len % TM_PROJ == 0 and seq_len % TM_IN == 0 and seq_len % ML_STEP == 0
    f32 = jnp.float32
    bias = _attention_bias(rel_bias)
    bdq = _block_diag_mean(ATT_WIDTH, ATT_HEAD_DIM)
    bdk = _block_diag_mean(LANES, ATT_HEAD_DIM)
    x2 = x.reshape(n_tok, D_MODEL)
    for l in range(DEPTH):
        wgt = _bf16(w_in[l, :, O_G:N_IN].T)
        qg = (jnp.tile(q_norm_g[l].astype(f32), ATT_HEADS) * (ATT_HEAD_DIM ** -0.5))[None, :]
        kg = jnp.tile(k_norm_g[l].astype(f32), ATT_KV_HEADS)[None, :]
        gbt = gate_b[l].astype(f32)[:, None]
        qn, kn, vd, qkm, vm, om, gates_t = _in_proj(
            x2, norm_mix_g[l][None, :], w_in, l, wgt, qg, kg, bdq, bdk,
            conv_w[l], conv_b[l][None, :], gbt, batch, seq_len)
        att = _attention(qn, kn, vd, bias, attn_sink[l].astype(f32), batch, seq_len)
        r3 = lambda a: a.reshape(batch, seq_len, a.shape[-1])
        hm = _mlstm(r3(qkm), r3(vm), r3(om), gates_t, mlstm_norm_g[l][None, :], batch, seq_len)
        n_rt = 4 * SUBLANES
        wrt = jnp.pad(jnp.concatenate([w_router_expert[l], w_router_group[l]], axis=1).astype(f32).T,
                      ((0, n_rt - N_EXPERTS - N_GROUPS), (0, 0)))
        brt = jnp.pad(jnp.concatenate([b_router_expert[l], b_router_group[l]]).astype(f32),
                      (0, n_rt - N_EXPERTS - N_GROUPS))[:, None]
        wrt_hi = _bf16(wrt)
        wrt = jnp.concatenate([wrt_hi, _bf16(wrt - wrt_hi.astype(f32))], axis=0)
        x1, t, row, col, cnt = _out_proj_router(x2, att, hm.reshape(n_tok, ML_WIDTH), w_out, l,
                                                norm_ffn_g[l][None, :], wrt, brt)
        _, n_blk_cap = _moe_capacity(n_tok)
        nch, dst, nz, zdst, blk_e, nused = _routing_tables(cnt[:, :, 0], n_blk_cap)
        xs = _dispatch(t, col, row, nch, dst, nz, zdst)
        ys = _moe(xs, blk_e, nused, w_gate, w_up, w_down, l)
        x2 = _combine(x1, col, ys, nch, dst)
    return x2.reshape(batch, seq_len, D_MODEL)
```

```python
import functools
import math

import jax
import jax.numpy as jnp
import numpy as np
from jax import lax
from jax.experimental import pallas as pl
from jax.experimental.pallas import tpu as pltpu

D_MODEL = 1024
DEPTH = 2
ATT_HEADS = 8
ATT_KV_HEADS = 2
ATT_HEAD_DIM = 64
ATT_WIDTH = ATT_HEADS * ATT_HEAD_DIM
WINDOW = 128
ATT_BLOCK = 128
N_BUCKETS = 32
MAX_DISTANCE = 128
ML_HEADS = 4
ML_DQK = 64
ML_DV = 128
ML_WIDTH = ML_HEADS * ML_DV
ML_CHUNK = 64
CONV_K = 4
N_GROUPS = 4
EXPERTS_PER_GROUP = 4
N_EXPERTS = N_GROUPS * EXPERTS_PER_GROUP
D_FF_EXPERT = 512
EPS = 1e-6

LANES = 128
SUBLANES = 8
NEG_BIG = -1e30
VMEM_LIMIT = 48 * 1024 * 1024

O_Q = 0
O_K = O_Q + ATT_WIDTH
O_V = O_K + ATT_KV_HEADS * ATT_HEAD_DIM
O_QM = O_V + ATT_KV_HEADS * ATT_HEAD_DIM
O_KM = O_QM + ML_HEADS * ML_DQK
O_VM = O_KM + ML_HEADS * ML_DQK
O_OM = O_VM + ML_WIDTH
O_G = O_OM + ML_WIDTH
N_IN = O_G + 2 * ML_HEADS

TM_IN = 1024
IN_SUBBLOCKS = 2
TM_OUT = 1024
TM_PROJ = 512
ML_KCHUNK = 256
ML_STEP = ML_KCHUNK

SEG_ALIGN = 16
MOE_BM = 512
L_CAP = 2 * TM_PROJ + N_EXPERTS * SEG_ALIGN
N_CHUNK = L_CAP // SEG_ALIGN
D_XS = D_MODEL + LANES
DISPATCH_ROWS = 256
COMBINE_ROWS = 256
ROW_POS0, ROW_POS1, ROW_W0, ROW_W1, ROW_E0, ROW_E1 = 0, 1, 2, 3, 4, 5
COL_W0H, COL_W1H, COL_E0, COL_E1 = 8, 11, 14, 15


def _moe_capacity(n_tok):
    n_tiles = n_tok // TM_PROJ
    rows = 2 * n_tok + n_tiles * N_EXPERTS * (SEG_ALIGN - 1) + N_EXPERTS * (MOE_BM - SEG_ALIGN)
    n_blk = -(-rows // MOE_BM)
    return n_blk * MOE_BM, n_blk


def _bf16(a):
    return a.astype(jnp.bfloat16)


def _split3(a):
    hi = _bf16(a)
    r1 = a - hi.astype(jnp.float32)
    mid = _bf16(r1)
    return hi, mid, _bf16(r1 - mid.astype(jnp.float32))


def _log_sigmoid(z):
    return jnp.minimum(z, 0.0) - jnp.log(1.0 + jnp.exp(-jnp.abs(z)))


def _in_proj_kernel(tiles_per_seq, x_ref, g_ref, w_ref, wgt_ref, qg_ref, kg_ref, bdq_ref, bdk_ref,
                    cw_ref, cb_ref, gbt_ref,
                    qn_ref, kn_ref, vd_ref, qkm_ref, vm_ref, om_ref, gatet_ref,
                    w_bf, conv_scr):
    i = pl.program_id(0)
    tm = x_ref.shape[0]

    @pl.when(i == 0)
    def _():
        w_bf[...] = _bf16(w_ref[...])

    conv_w = 2 * ML_HEADS * ML_DQK

    @pl.when(i % tiles_per_seq == 0)
    def _():
        conv_scr[0:SUBLANES, :] = jnp.zeros((SUBLANES, conv_w), jnp.float32)

    @pl.when(i % tiles_per_seq != 0)
    def _():
        conv_scr[0:SUBLANES, :] = conv_scr[tm:tm + SUBLANES, :]

    sub = tm // IN_SUBBLOCKS
    for sb in range(IN_SUBBLOCKS):
        rs = slice(sb * sub, (sb + 1) * sub)
        x = x_ref[rs, :]
        hn = x * lax.rsqrt(jnp.mean(x * x, axis=-1, keepdims=True) + EPS) * g_ref[...]
        hb = _bf16(hn)

        def proj(c0, width):
            return jnp.dot(hb, w_bf[:, c0:c0 + width], preferred_element_type=jnp.float32)

        q = proj(O_Q, ATT_WIDTH)
        q_ms = jnp.dot(_bf16(q * q), bdq_ref[...], preferred_element_type=jnp.float32)
        qn_ref[rs, :] = _bf16(q * lax.rsqrt(q_ms + EPS) * qg_ref[...])
        kv = proj(O_K, 2 * LANES)
        k = kv[:, 0:LANES]
        v = kv[:, LANES:2 * LANES]
        k_ms = jnp.dot(_bf16(k * k), bdk_ref[...], preferred_element_type=jnp.float32)
        kn = k * lax.rsqrt(k_ms + EPS) * kg_ref[...]
        low = lax.broadcasted_iota(jnp.int32, kn.shape, 1) < ATT_HEAD_DIM

        def dup_heads(a):
            swapped = pltpu.roll(a, ATT_HEAD_DIM, axis=1)
            return jnp.concatenate([jnp.where(low, a, swapped), jnp.where(low, swapped, a)], axis=1)

        kn_ref[rs, :] = _bf16(dup_heads(kn))
        vd_ref[rs, :] = _bf16(dup_heads(v))

        qk = proj(O_QM, conv_w)
        base = SUBLANES + sb * sub
        conv_scr[base:base + sub, :] = qk
        y = qk * cw_ref[CONV_K - 1:CONV_K, :] + cb_ref[...]
        for j in range(CONV_K - 1):
            off = base - (CONV_K - 1) + j
            y = y + conv_scr[off:off + sub, :] * cw_ref[j:j + 1, :]
        y = y * jax.nn.sigmoid(y)
        lane = lax.broadcasted_iota(jnp.int32, y.shape, 1)
        y = jnp.where(lane >= ML_HEADS * ML_DQK, y * (ML_DQK ** -0.5), y)
        qkm_ref[rs, :] = _bf16(y)

        vm_ref[rs, :] = _bf16(proj(O_VM, ML_WIDTH))
        om_ref[rs, :] = _bf16(jax.nn.sigmoid(proj(O_OM, ML_WIDTH)))

        gt = lax.dot_general(wgt_ref[...], hb, (((1,), (1,)), ((), ())),
                             preferred_element_type=jnp.float32) + gbt_ref[...]
        grow = lax.broadcasted_iota(jnp.int32, gt.shape, 0)
        gatet_ref[:, rs] = jnp.where(grow >= ML_HEADS, _log_sigmoid(gt), gt)


def _in_proj(x2, g, w_in, layer, wgt, qg, kg, bdq, bdk, cw, cb, gbt, batch, seq_len):
    t = x2.shape[0]
    tm = TM_IN
    n = t // tm
    tps = seq_len // tm
    row = lambda i: (i, 0)
    fix = lambda i: (0, 0)
    kv_w = 2 * ATT_KV_HEADS * ATT_HEAD_DIM
    out_shapes = (
        jax.ShapeDtypeStruct((t, ATT_WIDTH), jnp.bfloat16),
        jax.ShapeDtypeStruct((t, kv_w), jnp.bfloat16),
        jax.ShapeDtypeStruct((t, kv_w), jnp.bfloat16),
        jax.ShapeDtypeStruct((t, 2 * ML_HEADS * ML_DQK), jnp.bfloat16),
        jax.ShapeDtypeStruct((t, ML_WIDTH), jnp.bfloat16),
        jax.ShapeDtypeStruct((t, ML_WIDTH), jnp.bfloat16),
        jax.ShapeDtypeStruct((batch, SUBLANES, seq_len), jnp.float32),
    )
    in_specs = [
        pl.BlockSpec((tm, D_MODEL), row),
        pl.BlockSpec((1, D_MODEL), fix),
        pl.BlockSpec((None, D_MODEL, N_IN), lambda i: (layer, 0, 0), pipeline_mode=pl.Buffered(1)),
        pl.BlockSpec((SUBLANES, D_MODEL), fix),
        pl.BlockSpec((1, ATT_WIDTH), fix),
        pl.BlockSpec((1, LANES), fix),
        pl.BlockSpec((ATT_WIDTH, ATT_WIDTH), fix),
        pl.BlockSpec((LANES, LANES), fix),
        pl.BlockSpec((CONV_K, 2 * ML_HEADS * ML_DQK), fix),
        pl.BlockSpec((1, 2 * ML_HEADS * ML_DQK), fix),
        pl.BlockSpec((SUBLANES, 1), fix),
    ]
    out_specs = [
        pl.BlockSpec((tm, ATT_WIDTH), row),
        pl.BlockSpec((tm, kv_w), row),
        pl.BlockSpec((tm, kv_w), row),
        pl.BlockSpec((tm, 2 * ML_HEADS * ML_DQK), row),
        pl.BlockSpec((tm, ML_WIDTH), row),
        pl.BlockSpec((tm, ML_WIDTH), row),
        pl.BlockSpec((None, SUBLANES, tm), lambda i: (i // tps, 0, i % tps)),
    ]
    return pl.pallas_call(
        functools.partial(_in_proj_kernel, seq_len // tm),
        grid=(n,),
        in_specs=in_specs,
        out_specs=out_specs,
        out_shape=out_shapes,
        scratch_shapes=[pltpu.VMEM((D_MODEL, N_IN), jnp.bfloat16),
                        pltpu.VMEM((tm + 2 * SUBLANES, 2 * ML_HEADS * ML_DQK), jnp.float32)],
        compiler_params=pltpu.CompilerParams(dimension_semantics=("arbitrary",),
                                             vmem_limit_bytes=VMEM_LIMIT),
        name="in_proj",
    )(x2, g, w_in, wgt, qg, kg, bdq, bdk, cw, cb, gbt)


def _attn_kernel(sink_ref, q_ref, kp_ref, kc_ref, vp_ref, vc_ref, bias_ref, o_ref):
    i = pl.program_id(1)
    blk = q_ref.shape[0]
    lane = lax.broadcasted_iota(jnp.int32, (blk, LANES), 1)
    low = lane < ATT_HEAD_DIM
    col = lax.broadcasted_iota(jnp.int32, (blk, 2 * blk), 1)
    no_prev = jnp.logical_and(col < blk, i == 0)
    group = ATT_HEADS // ATT_KV_HEADS
    for pair in range(ATT_HEADS // 2):
        kv = (2 * pair) // group
        qp = q_ref[:, pair * LANES:(pair + 1) * LANES]
        kcat = jnp.concatenate([kp_ref[:, kv * LANES:(kv + 1) * LANES],
                                kc_ref[:, kv * LANES:(kv + 1) * LANES]], axis=0)
        vcat = jnp.concatenate([vp_ref[:, kv * LANES:(kv + 1) * LANES],
                                vc_ref[:, kv * LANES:(kv + 1) * LANES]], axis=0)
        halves = []
        for sub in range(2):
            h = 2 * pair + sub
            qm = jnp.where(low if sub == 0 else jnp.logical_not(low), qp, jnp.zeros_like(qp))
            s = lax.dot_general(qm, kcat, (((1,), (1,)), ((), ())),
                                preferred_element_type=jnp.float32)
            logits = jnp.where(no_prev, NEG_BIG, s + bias_ref[h])
            sink = sink_ref[h]
            m = jnp.maximum(jnp.max(logits, axis=-1, keepdims=True), sink)
            p = jnp.exp(logits - m)
            den = jnp.sum(p, axis=-1, keepdims=True) + jnp.exp(sink - m)
            o = jnp.dot(_bf16(p), vcat, preferred_element_type=jnp.float32)
            halves.append(o / den)
        o_ref[:, pair * LANES:(pair + 1) * LANES] = _bf16(jnp.where(low, halves[0], halves[1]))


def _attention(qn, kn, vd, bias, sink, batch, seq_len):
    nb = seq_len // ATT_BLOCK
    cur = lambda b, i, s: (b * nb + i, 0)
    prev = lambda b, i, s: (b * nb + jnp.maximum(i - 1, 0), 0)
    grid_spec = pltpu.PrefetchScalarGridSpec(
        num_scalar_prefetch=1,
        grid=(batch, nb),
        in_specs=[
            pl.BlockSpec((ATT_BLOCK, ATT_WIDTH), cur),
            pl.BlockSpec((ATT_BLOCK, 2 * LANES), prev),
            pl.BlockSpec((ATT_BLOCK, 2 * LANES), cur),
            pl.BlockSpec((ATT_BLOCK, 2 * LANES), prev),
            pl.BlockSpec((ATT_BLOCK, 2 * LANES), cur),
            pl.BlockSpec((ATT_HEADS, ATT_BLOCK, 2 * ATT_BLOCK), lambda b, i, s: (0, 0, 0)),
        ],
        out_specs=pl.BlockSpec((ATT_BLOCK, ATT_WIDTH), cur),
    )
    return pl.pallas_call(
        _attn_kernel,
        grid_spec=grid_spec,
        out_shape=jax.ShapeDtypeStruct((batch * seq_len, ATT_WIDTH), jnp.bfloat16),
        compiler_params=pltpu.CompilerParams(dimension_semantics=("arbitrary", "arbitrary"),
                                             vmem_limit_bytes=VMEM_LIMIT),
        name="swa_attention",
    )(sink, qn, kn, kn, vd, vd, bias)


def _mlstm_kernel(qk_ref, v_ref, o_ref, gt_ref, ng_ref, out_ref, c_scr, m_scr):
    step = pl.program_id(0)
    batch = qk_ref.shape[0]
    L = ML_KCHUNK
    pairs = ML_HEADS // 2

    @pl.when(step == 0)
    def _():
        c_scr[...] = jnp.zeros(c_scr.shape, jnp.float32)
        m_scr[...] = jnp.zeros(m_scr.shape, jnp.float32)

    r_i = lax.broadcasted_iota(jnp.int32, (L, L), 0)
    c_i = lax.broadcasted_iota(jnp.int32, (L, L), 1)
    causal = c_i <= r_i
    tril_bf = causal.astype(jnp.bfloat16)
    triu_bf = (r_i <= c_i).astype(jnp.bfloat16)
    mean_dv = jnp.full((ML_DV, ML_DV), 1.0 / ML_DV, jnp.bfloat16)
    lane = lax.broadcasted_iota(jnp.int32, (L, LANES), 1)
    low = lane < ML_DQK
    ones_dv = jnp.ones((L, ML_DV), jnp.bfloat16)
    row2 = lax.broadcasted_iota(jnp.int32, (2 * ML_DQK, 2 * ML_DV), 0)

    def twice(a):
        return jnp.concatenate([a, a], axis=1)

    m_state = [m_scr[k] for k in range(batch * ML_HEADS)]
    c_state = [c_scr[k] for k in range(batch * pairs)]

    for ch in range(ML_STEP // L):
        r0 = ch * L
        for b in range(batch):
            gt = gt_ref[b, :, r0:r0 + L]
            g = jnp.concatenate([gt, jnp.zeros((LANES - SUBLANES, L), jnp.float32)], axis=0).T
            bcols = sum(jnp.dot(tril_bf, part, preferred_element_type=jnp.float32) for part in _split3(g))
            brows = sum(jnp.dot(part, triu_bf, preferred_element_type=jnp.float32) for part in _split3(gt))
            for pair in range(pairs):
                sidx = b * pairs + pair
                qp = qk_ref[b, r0:r0 + L, pair * LANES:(pair + 1) * LANES]
                kp = qk_ref[b, r0:r0 + L, (pairs + pair) * LANES:(pairs + pair + 1) * LANES]
                c_pair = c_state[sidx]
                c_bf = _bf16(c_pair)
                new_c = None
                decays = []
                for sub in range(2):
                    h = 2 * pair + sub
                    sel = low if sub == 0 else jnp.logical_not(low)
                    m_prev = m_state[b * ML_HEADS + h]
                    bc = jnp.broadcast_to(bcols[:, ML_HEADS + h:ML_HEADS + h + 1], (L, LANES))
                    lic = jnp.broadcast_to(g[:, h:h + 1], (L, LANES))
                    br = brows[ML_HEADS + h:ML_HEADS + h + 1, :]
                    lir = gt[h:h + 1, :]
                    log_d = jnp.where(causal, twice(bc) - (br - lir), NEG_BIG)
                    m_inter = bc + m_prev
                    row_max = jnp.broadcast_to(jnp.max(log_d, axis=-1, keepdims=True), (L, LANES))
                    m_row = jnp.maximum(m_inter, row_max)
                    d = jnp.exp(log_d - twice(m_row))
                    inter = jnp.exp(m_inter - m_row)
                    qm = jnp.where(sel, qp, jnp.zeros_like(qp))
                    s = lax.dot_general(qm, kp, (((1,), (1,)), ((), ())),
                                        preferred_element_type=jnp.float32) * d
                    v_ext = jnp.concatenate([v_ref[b, r0:r0 + L, h * ML_DV:(h + 1) * ML_DV], ones_dv],
                                            axis=-1)
                    num = twice(inter) * jnp.dot(qm, c_bf, preferred_element_type=jnp.float32) \
                        + jnp.dot(_bf16(s), v_ext, preferred_element_type=jnp.float32)
                    den = num[:, ML_DV:2 * ML_DV]
                    hval = num[:, 0:ML_DV] / jnp.maximum(jnp.abs(den), jnp.exp(-m_row))
                    h_ms = jnp.dot(_bf16(hval * hval), mean_dv, preferred_element_type=jnp.float32)
                    hn = hval * lax.rsqrt(h_ms + EPS)
                    hn = hn * ng_ref[:, h * ML_DV:(h + 1) * ML_DV]
                    out_ref[b, r0:r0 + L, h * ML_DV:(h + 1) * ML_DV] = _bf16(
                        hn * o_ref[b, r0:r0 + L, h * ML_DV:(h + 1) * ML_DV].astype(jnp.float32))
                    b_last = bc[L - 1:L, :]
                    log_w = b_last - bc + lic
                    m_next = jnp.maximum(b_last + m_prev, jnp.max(log_w, axis=0, keepdims=True))
                    w = jnp.exp(log_w - m_next)
                    decays.append(jnp.exp(b_last + m_prev - m_next))
                    m_state[b * ML_HEADS + h] = m_next
                    kw = _bf16(jnp.where(sel, kp.astype(jnp.float32) * w, 0.0))
                    upd = lax.dot_general(kw, v_ext, (((0,), (0,)), ((), ())),
                                          preferred_element_type=jnp.float32)
                    new_c = upd if new_c is None else new_c + upd
                decay_rows = jnp.where(row2 < ML_DQK, twice(decays[0]), twice(decays[1]))
                c_state[sidx] = decay_rows * c_pair + new_c

    for k in range(batch * ML_HEADS):
        m_scr[k] = m_state[k]
    for k in range(batch * pairs):
        c_scr[k] = c_state[k]


def _mlstm(qkm, vm, om, gates_t, ng, batch, seq_len):
    n_steps = seq_len // ML_STEP
    blk = lambda c: (0, c, 0)
    return pl.pallas_call(
        _mlstm_kernel,
        grid=(n_steps,),
        in_specs=[
            pl.BlockSpec((batch, ML_STEP, 2 * ML_HEADS * ML_DQK), blk),
            pl.BlockSpec((batch, ML_STEP, ML_WIDTH), blk),
            pl.BlockSpec((batch, ML_STEP, ML_WIDTH), blk),
            pl.BlockSpec((batch, SUBLANES, ML_STEP), lambda c: (0, 0, c)),
            pl.BlockSpec((1, ML_WIDTH), lambda c: (0, 0)),
        ],
        out_specs=pl.BlockSpec((batch, ML_STEP, ML_WIDTH), blk),
        out_shape=jax.ShapeDtypeStruct((batch, seq_len, ML_WIDTH), jnp.bfloat16),
        scratch_shapes=[
            pltpu.VMEM((batch * ML_HEADS // 2, 2 * ML_DQK, 2 * ML_DV), jnp.float32),
            pltpu.VMEM((batch * ML_HEADS, 1, LANES), jnp.float32),
        ],
        compiler_params=pltpu.CompilerParams(dimension_semantics=("arbitrary",),
                                             vmem_limit_bytes=VMEM_LIMIT),
        name="mlstm_scan",
    )(qkm, vm, om, gates_t, ng)


def _out_proj_router_kernel(x_ref, att_ref, hm_ref, wo_ref, g_ref, wrt_ref, brt_ref,
                            x1_ref, t_ref, row_ref, col_ref, cnt_ref, wo_bf):
    tm = x_ref.shape[0]
    tr = TM_PROJ
    hi = lax.Precision.HIGHEST

    @pl.when(pl.program_id(0) == 0)
    def _():
        wo_bf[...] = _bf16(wo_ref[...])

    for sb in range(tm // TM_PROJ):
        rs = slice(sb * TM_PROJ, (sb + 1) * TM_PROJ)
        x1 = x_ref[rs, :] \
            + jnp.dot(att_ref[rs, :], wo_bf[0:ATT_WIDTH, :], preferred_element_type=jnp.float32) \
            + jnp.dot(hm_ref[rs, :], wo_bf[ATT_WIDTH:, :], preferred_element_type=jnp.float32)
        x1_ref[rs, :] = x1
        tn = x1 * lax.rsqrt(jnp.mean(x1 * x1, axis=-1, keepdims=True) + EPS) * g_ref[...]
        tn_hi = _bf16(tn)
        t_ref[rs, :] = tn_hi
        tn_lo = _bf16(tn - tn_hi.astype(jnp.float32))
        nt = (((1,), (1,)), ((), ()))
        n_rt = wrt_ref.shape[0] // 2
        p_hi = lax.dot_general(wrt_ref[...], tn_hi, nt, preferred_element_type=jnp.float32)
        p_lo = lax.dot_general(wrt_ref[0:n_rt, :], tn_lo, nt, preferred_element_type=jnp.float32)
        logits = p_hi[0:n_rt, :] + p_hi[n_rt:, :] + p_lo + brt_ref[...]
        el_all = logits[0:N_EXPERTS, :]
        gl = logits[N_EXPERTS:N_EXPERTS + SUBLANES, :]
        grow = lax.broadcasted_iota(jnp.int32, gl.shape, 0).astype(jnp.float32)
        gl = jnp.where(grow < N_GROUPS, gl, NEG_BIG)
        gmax = jnp.max(gl, axis=0, keepdims=True)
        grp = jnp.min(jnp.where(gl == gmax, grow, float(N_GROUPS)), axis=0, keepdims=True)
        p_grp = 1.0 / jnp.sum(jnp.exp(gl - gmax), axis=0, keepdims=True)
        erow = lax.broadcasted_iota(jnp.int32, el_all.shape, 0).astype(jnp.float32)
        egrp = jnp.floor(erow * (1.0 / EXPERTS_PER_GROUP))
        el = jnp.where(egrp == grp, el_all, NEG_BIG)
        e1 = jnp.max(el, axis=0, keepdims=True)
        i1 = jnp.min(jnp.where(el == e1, erow, float(N_EXPERTS)), axis=0, keepdims=True)
        el2 = jnp.where(erow == i1, NEG_BIG, el)
        e2 = jnp.max(el2, axis=0, keepdims=True)
        i2 = jnp.min(jnp.where(el2 == e2, erow, float(N_EXPERTS)), axis=0, keepdims=True)
        z2 = jnp.exp(e2 - e1)
        w1 = p_grp / (1.0 + z2)
        w2 = p_grp * z2 / (1.0 + z2)
        sel1 = erow == i1
        sel2 = erow == i2
        onehot = jnp.logical_or(sel1, sel2)
        t_r = lax.broadcasted_iota(jnp.int32, (tr, tr), 0)
        t_c = lax.broadcasted_iota(jnp.int32, (tr, tr), 1)
        before = (t_r < t_c).astype(jnp.bfloat16)
        rank = jnp.dot(onehot.astype(jnp.bfloat16), before, preferred_element_type=jnp.float32)
        cnt = jnp.sum(onehot.astype(jnp.float32), axis=1, keepdims=True)
        cnt_al = jnp.floor((cnt + (SEG_ALIGN - 1)) * (1.0 / SEG_ALIGN)) * SEG_ALIGN
        e_r = lax.broadcasted_iota(jnp.int32, (N_EXPERTS, N_EXPERTS), 0)
        e_c = lax.broadcasted_iota(jnp.int32, (N_EXPERTS, N_EXPERTS), 1)
        lstart = jnp.dot((e_c < e_r).astype(jnp.float32), jnp.broadcast_to(cnt_al, (N_EXPERTS, LANES)),
                         precision=hi, preferred_element_type=jnp.float32)[:, 0:1]
        slot = lstart + rank
        pos1 = jnp.sum(jnp.where(sel1, slot, 0.0), axis=0, keepdims=True)
        pos2 = jnp.sum(jnp.where(sel2, slot, 0.0), axis=0, keepdims=True)
        r8 = lax.broadcasted_iota(jnp.int32, (SUBLANES, tr), 0)
        info = jnp.where(r8 == ROW_POS0, pos1, jnp.where(r8 == ROW_POS1, pos2, jnp.where(
            r8 == ROW_W0, w1, jnp.where(r8 == ROW_W1, w2, jnp.where(
                r8 == ROW_E0, i1, jnp.where(r8 == ROW_E1, i2, 0.0))))))
        row_ref[:, rs] = info

        def split3(w):
            h = _bf16(w).astype(jnp.float32)
            m = _bf16(w - h).astype(jnp.float32)
            return h, m, _bf16(w - h - m).astype(jnp.float32)

        w1h, w1m, w1l = split3(w1)
        w2h, w2m, w2l = split3(w2)
        parts = jnp.where(r8 == 0, w1h, jnp.where(r8 == 1, w1m, jnp.where(r8 == 2, w1l, jnp.where(
            r8 == 3, w2h, jnp.where(r8 == 4, w2m, jnp.where(r8 == 5, w2l, jnp.where(r8 == 6, i1, i2)))))))
        col_ref[rs, :] = jnp.concatenate(
            [info, parts, jnp.zeros((LANES - 2 * SUBLANES, tr), jnp.float32)], axis=0).T
        cnt_ref[sb] = jnp.broadcast_to(cnt, (N_EXPERTS, LANES)).astype(jnp.int32)


def _out_proj_router(x2, att, hm, wo, layer, g, wrt, brt):
    t = x2.shape[0]
    tm = TM_OUT
    row = lambda i: (i, 0)
    fix = lambda i: (0, 0)
    return pl.pallas_call(
        _out_proj_router_kernel,
        grid=(t // tm,),
        in_specs=[
            pl.BlockSpec((tm, D_MODEL), row),
            pl.BlockSpec((tm, ATT_WIDTH), row),
            pl.BlockSpec((tm, ML_WIDTH), row),
            pl.BlockSpec((None, D_MODEL, D_MODEL), lambda i: (layer, 0, 0), pipeline_mode=pl.Buffered(1)),
            pl.BlockSpec((1, D_MODEL), fix),
            pl.BlockSpec((8 * SUBLANES, D_MODEL), fix),
            pl.BlockSpec((4 * SUBLANES, 1), fix),
        ],
        out_specs=[
            pl.BlockSpec((tm, D_MODEL), row),
            pl.BlockSpec((tm, D_MODEL), row),
            pl.BlockSpec((SUBLANES, tm), lambda i: (0, i)),
            pl.BlockSpec((tm, LANES), row),
            pl.BlockSpec((tm // TM_PROJ, N_EXPERTS, LANES), lambda i: (i, 0, 0)),
        ],
        out_shape=(
            jax.ShapeDtypeStruct((t, D_MODEL), jnp.float32),
            jax.ShapeDtypeStruct((t, D_MODEL), jnp.bfloat16),
            jax.ShapeDtypeStruct((SUBLANES, t), jnp.float32),
            jax.ShapeDtypeStruct((t, LANES), jnp.float32),
            jax.ShapeDtypeStruct((t // TM_PROJ, N_EXPERTS, LANES), jnp.int32),
        ),
        scratch_shapes=[pltpu.VMEM((D_MODEL, D_MODEL), jnp.bfloat16)],
        compiler_params=pltpu.CompilerParams(dimension_semantics=("arbitrary",),
                                             vmem_limit_bytes=VMEM_LIMIT),
        name="out_proj_router",
    )(x2, att, hm, wo, g, wrt, brt)


def _chunk_rows(j):
    return pl.ds(pl.multiple_of(j * SEG_ALIGN, SEG_ALIGN), SEG_ALIGN)


def _dispatch_kernel(nch_ref, dst_ref, nz_ref, zdst_ref, t_ref, col_ref, row_ref, xs_hbm,
                     xbuf, zx, sem_x, sem_z):
    i = pl.program_id(0)
    n = pl.num_programs(0)
    p = i % 2
    tm = t_ref.shape[0]

    def copy(tile, par, j):
        d = pl.ds(pl.multiple_of(dst_ref[tile, j], SEG_ALIGN), SEG_ALIGN)
        return pltpu.make_async_copy(xbuf.at[par, _chunk_rows(j)], xs_hbm.at[d], sem_x.at[par])

    def zero_copy(j):
        d = pl.ds(pl.multiple_of(zdst_ref[j], SEG_ALIGN), SEG_ALIGN)
        return pltpu.make_async_copy(zx, xs_hbm.at[d], sem_z.at[0])

    def wait_tile(tile, par):
        def body(j, c):
            copy(tile, par, j).wait()
            return c
        lax.fori_loop(0, nch_ref[tile], body, 0)

    @pl.when(i == 0)
    def _():
        zx[...] = jnp.zeros(zx.shape, zx.dtype)

        def body(j, c):
            zero_copy(j).start()
            return c
        lax.fori_loop(0, nz_ref[0], body, 0)

    @pl.when(i >= 2)
    def _():
        wait_tile(i - 2, p)

    pos0 = row_ref[ROW_POS0:ROW_POS0 + 1, :].astype(jnp.int32)
    pos1 = row_ref[ROW_POS1:ROW_POS1 + 1, :].astype(jnp.int32)
    t = t_ref[...]
    side = _bf16(col_ref[...])
    rows = DISPATCH_ROWS
    for c in range(L_CAP // rows):
        r = lax.broadcasted_iota(jnp.int32, (rows, tm), 0) + c * rows
        perm = jnp.logical_or(r == pos0, r == pos1).astype(jnp.bfloat16)
        xbuf[p, c * rows:(c + 1) * rows, 0:D_MODEL] = _bf16(
            jnp.dot(perm, t, preferred_element_type=jnp.float32))
        xbuf[p, c * rows:(c + 1) * rows, D_MODEL:D_XS] = _bf16(
            jnp.dot(perm, side, preferred_element_type=jnp.float32))

    def start_body(j, c):
        copy(i, p, j).start()
        return c
    lax.fori_loop(0, nch_ref[i], start_body, 0)

    @pl.when(i == n - 1)
    def _():
        @pl.when(n >= 2)
        def _():
            wait_tile(i - 1, 1 - p)
        wait_tile(i, p)

        def body(j, c):
            zero_copy(j).wait()
            return c
        lax.fori_loop(0, nz_ref[0], body, 0)


def _dispatch(t, col, row, nch, dst, nz, zdst):
    n_tok = t.shape[0]
    r_cap, _ = _moe_capacity(n_tok)
    tm = TM_PROJ
    grid_spec = pltpu.PrefetchScalarGridSpec(
        num_scalar_prefetch=4,
        grid=(n_tok // tm,),
        in_specs=[
            pl.BlockSpec((tm, D_MODEL), lambda i, *_: (i, 0)),
            pl.BlockSpec((tm, LANES), lambda i, *_: (i, 0)),
            pl.BlockSpec((SUBLANES, tm), lambda i, *_: (0, i)),
        ],
        out_specs=pl.BlockSpec(memory_space=pl.ANY),
        scratch_shapes=[
            pltpu.VMEM((2, L_CAP, D_XS), jnp.bfloat16),
            pltpu.VMEM((SEG_ALIGN, D_XS), jnp.bfloat16),
            pltpu.SemaphoreType.DMA((2,)),
            pltpu.SemaphoreType.DMA((1,)),
        ],
    )
    return pl.pallas_call(
        _dispatch_kernel,
        grid_spec=grid_spec,
        out_shape=jax.ShapeDtypeStruct((r_cap, D_XS), jnp.bfloat16),
        compiler_params=pltpu.CompilerParams(dimension_semantics=("arbitrary",),
                                             vmem_limit_bytes=VMEM_LIMIT),
        name="moe_dispatch",
    )(nch, dst, nz, zdst, t, col, row)


def _moe_kernel(blk_e_ref, nused_ref, xs_ref, wg_ref, wu_ref, wd_ref, ys_ref, wg_bf, wu_bf, wd_bf):
    b = pl.program_id(0)
    new_expert = jnp.logical_or(b == 0, blk_e_ref[b] != blk_e_ref[jnp.maximum(b - 1, 0)])

    @pl.when(new_expert)
    def _():
        wg_bf[...] = _bf16(wg_ref[...])
        wu_bf[...] = _bf16(wu_ref[...])
        wd_bf[...] = _bf16(wd_ref[...])

    @pl.when(b < nused_ref[0])
    def _():
        x = xs_ref[:, 0:D_MODEL]
        a = jnp.dot(x, wg_bf[...], preferred_element_type=jnp.float32)
        u = jnp.dot(x, wu_bf[...], preferred_element_type=jnp.float32)
        h = a * jax.nn.sigmoid(a) * u
        y = jnp.dot(_bf16(h), wd_bf[...], preferred_element_type=jnp.float32)
        sd = xs_ref[:, D_MODEL:D_XS].astype(jnp.float32)
        e_blk = blk_e_ref[b].astype(jnp.float32)
        w0 = sd[:, COL_W0H:COL_W0H + 1] + sd[:, COL_W0H + 1:COL_W0H + 2] + sd[:, COL_W0H + 2:COL_W0H + 3]
        w1 = sd[:, COL_W1H:COL_W1H + 1] + sd[:, COL_W1H + 1:COL_W1H + 2] + sd[:, COL_W1H + 2:COL_W1H + 3]
        w = jnp.where(sd[:, COL_E0:COL_E0 + 1] == e_blk, w0, w1)
        ys_ref[...] = _bf16(y * w)


def _moe(xs, blk_e, nused, wg, wu, wd, layer):
    blk = lambda b, be, nu: (jnp.maximum(jnp.minimum(b, nu[0] - 1), 0), 0)
    wsel = lambda b, be, nu: (layer, be[b], 0, 0)
    r_cap = xs.shape[0]
    grid_spec = pltpu.PrefetchScalarGridSpec(
        num_scalar_prefetch=2,
        grid=(r_cap // MOE_BM,),
        in_specs=[
            pl.BlockSpec((MOE_BM, D_XS), blk),
            pl.BlockSpec((None, None, D_MODEL, D_FF_EXPERT), wsel),
            pl.BlockSpec((None, None, D_MODEL, D_FF_EXPERT), wsel),
            pl.BlockSpec((None, None, D_FF_EXPERT, D_MODEL), wsel),
        ],
        out_specs=pl.BlockSpec((MOE_BM, D_MODEL), blk),
        scratch_shapes=[
            pltpu.VMEM((D_MODEL, D_FF_EXPERT), jnp.bfloat16),
            pltpu.VMEM((D_MODEL, D_FF_EXPERT), jnp.bfloat16),
            pltpu.VMEM((D_FF_EXPERT, D_MODEL), jnp.bfloat16),
        ],
    )
    return pl.pallas_call(
        _moe_kernel,
        grid_spec=grid_spec,
        out_shape=jax.ShapeDtypeStruct((r_cap, D_MODEL), jnp.bfloat16),
        compiler_params=pltpu.CompilerParams(dimension_semantics=("arbitrary",),
                                             vmem_limit_bytes=VMEM_LIMIT),
        name="moe_experts",
    )(blk_e, nused, xs, wg, wu, wd)


def _combine_kernel(nch_ref, dst_ref, x1_ref, col_ref, ys_hbm, out_ref, ybuf, sem):
    i = pl.program_id(0)
    n = pl.num_programs(0)
    p = i % 2
    tm = x1_ref.shape[0]

    def copy(tile, par, j):
        s = pl.ds(pl.multiple_of(dst_ref[tile, j], SEG_ALIGN), SEG_ALIGN)
        return pltpu.make_async_copy(ys_hbm.at[s], ybuf.at[par, _chunk_rows(j)], sem.at[par])

    def start_tile(tile, par):
        def body(j, c):
            copy(tile, par, j).start()
            return c
        lax.fori_loop(0, nch_ref[tile], body, 0)

    @pl.when(i == 0)
    def _():
        ybuf[...] = jnp.zeros(ybuf.shape, ybuf.dtype)
        start_tile(0, 0)

    @pl.when(i + 1 < n)
    def _():
        start_tile(i + 1, 1 - p)

    def wait_body(j, c):
        copy(i, p, j).wait()
        return c
    lax.fori_loop(0, nch_ref[i], wait_body, 0)

    rows = COMBINE_ROWS
    for c in range(tm // rows):
        col = col_ref[c * rows:(c + 1) * rows, :]
        pos0 = col[:, ROW_POS0:ROW_POS0 + 1].astype(jnp.int32)
        pos1 = col[:, ROW_POS1:ROW_POS1 + 1].astype(jnp.int32)
        l = lax.broadcasted_iota(jnp.int32, (rows, L_CAP), 1)
        perm = jnp.logical_or(l == pos0, l == pos1).astype(jnp.bfloat16)
        y = jnp.dot(perm, ybuf[p], preferred_element_type=jnp.float32)
        out_ref[c * rows:(c + 1) * rows, :] = x1_ref[c * rows:(c + 1) * rows, :] + y


def _combine(x1, col, ys, nch, dst):
    n_tok = x1.shape[0]
    tm = TM_PROJ
    grid_spec = pltpu.PrefetchScalarGridSpec(
        num_scalar_prefetch=2,
        grid=(n_tok // tm,),
        in_specs=[
            pl.BlockSpec((tm, D_MODEL), lambda i, *_: (i, 0)),
            pl.BlockSpec((tm, LANES), lambda i, *_: (i, 0)),
            pl.BlockSpec(memory_space=pl.ANY),
        ],
        out_specs=pl.BlockSpec((tm, D_MODEL), lambda i, *_: (i, 0)),
        scratch_shapes=[
            pltpu.VMEM((2, L_CAP, D_MODEL), jnp.bfloat16),
            pltpu.SemaphoreType.DMA((2,)),
        ],
    )
    return pl.pallas_call(
        _combine_kernel,
        grid_spec=grid_spec,
        out_shape=jax.ShapeDtypeStruct((n_tok, D_MODEL), jnp.float32),
        compiler_params=pltpu.CompilerParams(dimension_semantics=("arbitrary",),
                                             vmem_limit_bytes=VMEM_LIMIT),
        name="moe_combine",
    )(nch, dst, x1, col, ys)


def _routing_tables(counts, n_blk_cap):
    i32 = jnp.int32
    ca = ((counts + (SEG_ALIGN - 1)) // SEG_ALIGN) * SEG_ALIGN
    lend = jnp.cumsum(ca, axis=1)
    lstart = lend - ca
    nch = (lend[:, -1] // SEG_ALIGN).astype(i32)
    tot = jnp.sum(ca, axis=0)
    region = ((tot + (MOE_BM - 1)) // MOE_BM) * MOE_BM
    rend = jnp.cumsum(region)
    base = rend - region
    gstart = base[None, :] + jnp.cumsum(ca, axis=0) - ca
    j16 = jnp.arange(N_CHUNK, dtype=i32) * SEG_ALIGN
    e_of_j = jnp.minimum(jnp.sum(lend[:, None, :] <= j16[None, :, None], axis=2), N_EXPERTS - 1)
    dst = (jnp.take_along_axis(gstart - lstart, e_of_j, axis=1) + j16[None, :]).astype(i32)
    dst = jnp.where(j16[None, :] < lend[:, -1:], dst, 0)
    nused = (rend[-1] // MOE_BM).astype(i32)
    brow = jnp.arange(n_blk_cap, dtype=i32) * MOE_BM
    blk_e = jnp.minimum(jnp.sum(rend[None, :] <= brow[:, None], axis=1), N_EXPERTS - 1)
    blk_e = jnp.where(jnp.arange(n_blk_cap) < nused, blk_e, blk_e[nused - 1]).astype(i32)
    k16 = jnp.arange(MOE_BM // SEG_ALIGN - 1, dtype=i32) * SEG_ALIGN
    zrow = (base + tot)[:, None] + k16[None, :]
    zvalid = zrow < rend[:, None]
    order = jnp.argsort(jnp.logical_not(zvalid).reshape(-1), stable=True)
    zdst = jnp.where(zvalid, zrow, 0).reshape(-1)[order].astype(i32)
    nz = jnp.sum(zvalid).astype(i32)
    return nch, dst, nz[None], zdst, blk_e, nused[None]


def _t5_bucket_np(dist):
    max_exact = N_BUCKETS // 2
    d = np.maximum(dist, 1).astype(np.float32)
    large = max_exact + (np.log(d / max_exact) / math.log(MAX_DISTANCE / max_exact)
                         * (N_BUCKETS - max_exact)).astype(np.int32)
    large = np.minimum(large, N_BUCKETS - 1)
    return np.where(dist < max_exact, dist, large)


def _attention_bias(rel_bias):
    qi = np.arange(ATT_BLOCK)[:, None]
    kj = np.arange(2 * ATT_BLOCK)[None, :]
    dist = qi + ATT_BLOCK - kj
    in_window = (dist >= 0) & (dist < WINDOW)
    bucket = _t5_bucket_np(np.clip(dist, 0, WINDOW - 1))
    onehot = (bucket[None] == np.arange(N_BUCKETS)[:, None, None]).astype(np.float32)
    bias = jnp.einsum('nh,nqk->hqk', rel_bias.astype(jnp.float32), jnp.asarray(onehot),
                      precision=lax.Precision.HIGHEST)
    return jnp.where(jnp.asarray(in_window)[None], bias, NEG_BIG)


def _block_diag_mean(width, block):
    idx = np.arange(width) // block
    return jnp.asarray((idx[:, None] == idx[None, :]).astype(np.float32) / block, dtype=jnp.bfloat16)


def kernel(x, rel_bias, norm_mix_g, w_in, q_norm_g, k_norm_g, attn_sink, conv_w, conv_b, gate_b,
           mlstm_norm_g, w_out, norm_ffn_g, w_router_group, b_router_group, w_router_expert,
           b_router_expert, w_gate, w_up, w_down):
    batch, seq_len, _ = x.shape
    n_tok = batch * seq_len
    assert seq_len % TM_PROJ == 0 and seq_len % TM_IN == 0 and seq_len % ML_STEP == 0
    f32 = jnp.float32
    bias = _attention_bias(rel_bias)
    bdq = _block_diag_mean(ATT_WIDTH, ATT_HEAD_DIM)
    bdk = _block_diag_mean(LANES, ATT_HEAD_DIM)
    x2 = x.reshape(n_tok, D_MODEL)
    for l in range(DEPTH):
        wgt = _bf16(w_in[l, :, O_G:N_IN].T)
        qg = (jnp.tile(q_norm_g[l].astype(f32), ATT_HEADS) * (ATT_HEAD_DIM ** -0.5))[None, :]
        kg = jnp.tile(k_norm_g[l].astype(f32), ATT_KV_HEADS)[None, :]
        gbt = gate_b[l].astype(f32)[:, None]
        qn, kn, vd, qkm, vm, om, gates_t = _in_proj(
            x2, norm_mix_g[l][None, :], w_in, l, wgt, qg, kg, bdq, bdk,
            conv_w[l], conv_b[l][None, :], gbt, batch, seq_len)
        att = _attention(qn, kn, vd, bias, attn_sink[l].astype(f32), batch, seq_len)
        r3 = lambda a: a.reshape(batch, seq_len, a.shape[-1])
        hm = _mlstm(r3(qkm), r3(vm), r3(om), gates_t, mlstm_norm_g[l][None, :], batch, seq_len)
        n_rt = 4 * SUBLANES
        wrt = jnp.pad(jnp.concatenate([w_router_expert[l], w_router_group[l]], axis=1).astype(f32).T,
                      ((0, n_rt - N_EXPERTS - N_GROUPS), (0, 0)))
        brt = jnp.pad(jnp.concatenate([b_router_expert[l], b_router_group[l]]).astype(f32),
                      (0, n_rt - N_EXPERTS - N_GROUPS))[:, None]
        wrt_hi = _bf16(wrt)
        wrt = jnp.concatenate([wrt_hi, _bf16(wrt - wrt_hi.astype(f32))], axis=0)
        x1, t, row, col, cnt = _out_proj_router(x2, att, hm.reshape(n_tok, ML_WIDTH), w_out, l,
                                                norm_ffn_g[l][None, :], wrt, brt)
        _, n_blk_cap = _moe_capacity(n_tok)
        nch, dst, nz, zdst, blk_e, nused = _routing_tables(cnt[:, :, 0], n_blk_cap)
        xs = _dispatch(t, col, row, nch, dst, nz, zdst)
        ys = _moe(xs, blk_e, nused, w_gate, w_up, w_down, l)
        x2 = _combine(x1, col, ys, nch, dst)
    return x2.reshape(batch, seq_len, D_MODEL)
```

```python
import functools
import math

import jax
import jax.numpy as jnp
import numpy as np
from jax import lax
from jax.experimental import pallas as pl
from jax.experimental.pallas import tpu as pltpu

D_MODEL = 1024
DEPTH = 2
ATT_HEADS = 8
ATT_KV_HEADS = 2
ATT_HEAD_DIM = 64
ATT_WIDTH = ATT_HEADS * ATT_HEAD_DIM
WINDOW = 128
ATT_BLOCK = 128
N_BUCKETS = 32
MAX_DISTANCE = 128
ML_HEADS = 4
ML_DQK = 64
ML_DV = 128
ML_WIDTH = ML_HEADS * ML_DV
ML_CHUNK = 64
CONV_K = 4
N_GROUPS = 4
EXPERTS_PER_GROUP = 4
N_EXPERTS = N_GROUPS * EXPERTS_PER_GROUP
D_FF_EXPERT = 512
EPS = 1e-6

LANES = 128
SUBLANES = 8
NEG_BIG = -1e30
VMEM_LIMIT = 48 * 1024 * 1024

O_Q = 0
O_K = O_Q + ATT_WIDTH
O_V = O_K + ATT_KV_HEADS * ATT_HEAD_DIM
O_QM = O_V + ATT_KV_HEADS * ATT_HEAD_DIM
O_KM = O_QM + ML_HEADS * ML_DQK
O_VM = O_KM + ML_HEADS * ML_DQK
O_OM = O_VM + ML_WIDTH
O_G = O_OM + ML_WIDTH
N_IN = O_G + 2 * ML_HEADS

TM_IN = 1024
IN_SUBBLOCKS = 2
TM_OUT = 1024
TM_PROJ = 512
ML_KCHUNK = 256
ML_STEP = ML_KCHUNK

SEG_ALIGN = 16
MOE_BM = 512
L_CAP = 2 * TM_PROJ + N_EXPERTS * SEG_ALIGN
N_CHUNK = L_CAP // SEG_ALIGN
D_XS = D_MODEL + LANES
DISPATCH_ROWS = 256
COMBINE_ROWS = 256
ROW_POS0, ROW_POS1, ROW_W0, ROW_W1, ROW_E0, ROW_E1 = 0, 1, 2, 3, 4, 5
COL_W0H, COL_W1H, COL_E0, COL_E1 = 8, 11, 14, 15


def _moe_capacity(n_tok):
    n_tiles = n_tok // TM_PROJ
    rows = 2 * n_tok + n_tiles * N_EXPERTS * (SEG_ALIGN - 1) + N_EXPERTS * (MOE_BM - SEG_ALIGN)
    n_blk = -(-rows // MOE_BM)
    return n_blk * MOE_BM, n_blk


def _bf16(a):
    return a.astype(jnp.bfloat16)


def _split3(a):
    hi = _bf16(a)
    r1 = a - hi.astype(jnp.float32)
    mid = _bf16(r1)
    return hi, mid, _bf16(r1 - mid.astype(jnp.float32))


def _log_sigmoid(z):
    return jnp.minimum(z, 0.0) - jnp.log(1.0 + jnp.exp(-jnp.abs(z)))


def _in_proj_kernel(tiles_per_seq, x_ref, g_ref, w_ref, wgt_ref, qg_ref, kg_ref, bdq_ref, bdk_ref,
                    cw_ref, cb_ref, gbt_ref,
                    qn_ref, kn_ref, vd_ref, qkm_ref, vm_ref, om_ref, gatet_ref,
                    w_bf, conv_scr):
    i = pl.program_id(0)
    tm = x_ref.shape[0]

    @pl.when(i == 0)
    def _():
        w_bf[...] = _bf16(w_ref[...])

    conv_w = 2 * ML_HEADS * ML_DQK

    @pl.when(i % tiles_per_seq == 0)
    def _():
        conv_scr[0:SUBLANES, :] = jnp.zeros((SUBLANES, conv_w), jnp.float32)

    @pl.when(i % tiles_per_seq != 0)
    def _():
        conv_scr[0:SUBLANES, :] = conv_scr[tm:tm + SUBLANES, :]

    sub = tm // IN_SUBBLOCKS
    for sb in range(IN_SUBBLOCKS):
        rs = slice(sb * sub, (sb + 1) * sub)
        x = x_ref[rs, :]
        hn = x * lax.rsqrt(jnp.mean(x * x, axis=-1, keepdims=True) + EPS) * g_ref[...]
        hb = _bf16(hn)

        def proj(c0, width):
            return jnp.dot(hb, w_bf[:, c0:c0 + width], preferred_element_type=jnp.float32)

        q = proj(O_Q, ATT_WIDTH)
        q_ms = jnp.dot(_bf16(q * q), bdq_ref[...], preferred_element_type=jnp.float32)
        qn_ref[rs, :] = _bf16(q * lax.rsqrt(q_ms + EPS) * qg_ref[...])
        kv = proj(O_K, 2 * LANES)
        k = kv[:, 0:LANES]
        v = kv[:, LANES:2 * LANES]
        k_ms = jnp.dot(_bf16(k * k), bdk_ref[...], preferred_element_type=jnp.float32)
        kn = k * lax.rsqrt(k_ms + EPS) * kg_ref[...]
        low = lax.broadcasted_iota(jnp.int32, kn.shape, 1) < ATT_HEAD_DIM

        def dup_heads(a):
            swapped = pltpu.roll(a, ATT_HEAD_DIM, axis=1)
            return jnp.concatenate([jnp.where(low, a, swapped), jnp.where(low, swapped, a)], axis=1)

        kn_ref[rs, :] = _bf16(dup_heads(kn))
        vd_ref[rs, :] = _bf16(dup_heads(v))

        qk = proj(O_QM, conv_w)
        base = SUBLANES + sb * sub
        conv_scr[base:base + sub, :] = qk
        y = qk * cw_ref[CONV_K - 1:CONV_K, :] + cb_ref[...]
        for j in range(CONV_K - 1):
            off = base - (CONV_K - 1) + j
            y = y + conv_scr[off:off + sub, :] * cw_ref[j:j + 1, :]
        y = y * jax.nn.sigmoid(y)
        lane = lax.broadcasted_iota(jnp.int32, y.shape, 1)
        y = jnp.where(lane >= ML_HEADS * ML_DQK, y * (ML_DQK ** -0.5), y)
        qkm_ref[rs, :] = _bf16(y)

        vm_ref[rs, :] = _bf16(proj(O_VM, ML_WIDTH))
        om_ref[rs, :] = _bf16(jax.nn.sigmoid(proj(O_OM, ML_WIDTH)))

        gt = lax.dot_general(wgt_ref[...], hb, (((1,), (1,)), ((), ())),
                             preferred_element_type=jnp.float32) + gbt_ref[...]
        grow = lax.broadcasted_iota(jnp.int32, gt.shape, 0)
        gatet_ref[:, rs] = jnp.where(grow >= ML_HEADS, _log_sigmoid(gt), gt)


def _in_proj(x2, g, w_in, layer, wgt, qg, kg, bdq, bdk, cw, cb, gbt, batch, seq_len):
    t = x2.shape[0]
    tm = TM_IN
    n = t // tm
    tps = seq_len // tm
    row = lambda i: (i, 0)
    fix = lambda i: (0, 0)
    kv_w = 2 * ATT_KV_HEADS * ATT_HEAD_DIM
    out_shapes = (
        jax.ShapeDtypeStruct((t, ATT_WIDTH), jnp.bfloat16),
        jax.ShapeDtypeStruct((t, kv_w), jnp.bfloat16),
        jax.ShapeDtypeStruct((t, kv_w), jnp.bfloat16),
        jax.ShapeDtypeStruct((t, 2 * ML_HEADS * ML_DQK), jnp.bfloat16),
        jax.ShapeDtypeStruct((t, ML_WIDTH), jnp.bfloat16),
        jax.ShapeDtypeStruct((t, ML_WIDTH), jnp.bfloat16),
        jax.ShapeDtypeStruct((batch, SUBLANES, seq_len), jnp.float32),
    )
    in_specs = [
        pl.BlockSpec((tm, D_MODEL), row),
        pl.BlockSpec((1, D_MODEL), fix),
        pl.BlockSpec((None, D_MODEL, N_IN), lambda i: (layer, 0, 0), pipeline_mode=pl.Buffered(1)),
        pl.BlockSpec((SUBLANES, D_MODEL), fix),
        pl.BlockSpec((1, ATT_WIDTH), fix),
        pl.BlockSpec((1, LANES), fix),
        pl.BlockSpec((ATT_WIDTH, ATT_WIDTH), fix),
        pl.BlockSpec((LANES, LANES), fix),
        pl.BlockSpec((CONV_K, 2 * ML_HEADS * ML_DQK), fix),
        pl.BlockSpec((1, 2 * ML_HEADS * ML_DQK), fix),
        pl.BlockSpec((SUBLANES, 1), fix),
    ]
    out_specs = [
        pl.BlockSpec((tm, ATT_WIDTH), row),
        pl.BlockSpec((tm, kv_w), row),
        pl.BlockSpec((tm, kv_w), row),
        pl.BlockSpec((tm, 2 * ML_HEADS * ML_DQK), row),
        pl.BlockSpec((tm, ML_WIDTH), row),
        pl.BlockSpec((tm, ML_WIDTH), row),
        pl.BlockSpec((None, SUBLANES, tm), lambda i: (i // tps, 0, i % tps)),
    ]
    return pl.pallas_call(
        functools.partial(_in_proj_kernel, seq_len // tm),
        grid=(n,),
        in_specs=in_specs,
        out_specs=out_specs,
        out_shape=out_shapes,
        scratch_shapes=[pltpu.VMEM((D_MODEL, N_IN), jnp.bfloat16),
                        pltpu.VMEM((tm + 2 * SUBLANES, 2 * ML_HEADS * ML_DQK), jnp.float32)],
        compiler_params=pltpu.CompilerParams(dimension_semantics=("arbitrary",),
                                             vmem_limit_bytes=VMEM_LIMIT),
        name="in_proj",
    )(x2, g, w_in, wgt, qg, kg, bdq, bdk, cw, cb, gbt)


def _attn_kernel(sink_ref, q_ref, kp_ref, kc_ref, vp_ref, vc_ref, bias_ref, o_ref):
    i = pl.program_id(1)
    blk = q_ref.shape[0]
    lane = lax.broadcasted_iota(jnp.int32, (blk, LANES), 1)
    low = lane < ATT_HEAD_DIM
    first = (i == 0).astype(jnp.int32)
    group = ATT_HEADS // ATT_KV_HEADS
    for pair in range(ATT_HEADS // 2):
        kv = (2 * pair) // group
        qp = q_ref[:, pair * LANES:(pair + 1) * LANES]
        kcat = jnp.concatenate([kp_ref[:, kv * LANES:(kv + 1) * LANES],
                                kc_ref[:, kv * LANES:(kv + 1) * LANES]], axis=0)
        vcat = jnp.concatenate([
            jnp.concatenate([vp_ref[:, kv * LANES:(kv + 1) * LANES],
                             vc_ref[:, kv * LANES:(kv + 1) * LANES]], axis=0),
            jnp.ones((2 * blk, LANES), jnp.bfloat16)], axis=1)
        halves = []
        for sub in range(2):
            h = 2 * pair + sub
            qm = jnp.where(low if sub == 0 else jnp.logical_not(low), qp, jnp.zeros_like(qp))
            s = lax.dot_general(qm, kcat, (((1,), (1,)), ((), ())),
                                preferred_element_type=jnp.float32)
            logits = s + bias_ref[first, h]
            sink = sink_ref[h]
            row_max = jnp.broadcast_to(jnp.max(logits, axis=-1, keepdims=True), (blk, LANES))
            m = jnp.maximum(row_max, sink)
            p = jnp.exp(logits - jnp.concatenate([m, m], axis=1))
            o = jnp.dot(_bf16(p), vcat, preferred_element_type=jnp.float32)
            den = o[:, LANES:2 * LANES] + jnp.exp(sink - m)
            halves.append(o[:, 0:LANES] / den)
        o_ref[:, pair * LANES:(pair + 1) * LANES] = _bf16(jnp.where(low, halves[0], halves[1]))


def _attention(qn, kn, vd, bias, sink, batch, seq_len):
    nb = seq_len // ATT_BLOCK
    cur = lambda b, i, s: (b * nb + i, 0)
    prev = lambda b, i, s: (b * nb + jnp.maximum(i - 1, 0), 0)
    grid_spec = pltpu.PrefetchScalarGridSpec(
        num_scalar_prefetch=1,
        grid=(batch, nb),
        in_specs=[
            pl.BlockSpec((ATT_BLOCK, ATT_WIDTH), cur),
            pl.BlockSpec((ATT_BLOCK, 2 * LANES), prev),
            pl.BlockSpec((ATT_BLOCK, 2 * LANES), cur),
            pl.BlockSpec((ATT_BLOCK, 2 * LANES), prev),
            pl.BlockSpec((ATT_BLOCK, 2 * LANES), cur),
            pl.BlockSpec((2, ATT_HEADS, ATT_BLOCK, 2 * ATT_BLOCK), lambda b, i, s: (0, 0, 0, 0)),
        ],
        out_specs=pl.BlockSpec((ATT_BLOCK, ATT_WIDTH), cur),
    )
    return pl.pallas_call(
        _attn_kernel,
        grid_spec=grid_spec,
        out_shape=jax.ShapeDtypeStruct((batch * seq_len, ATT_WIDTH), jnp.bfloat16),
        compiler_params=pltpu.CompilerParams(dimension_semantics=("arbitrary", "arbitrary"),
                                             vmem_limit_bytes=VMEM_LIMIT),
        name="swa_attention",
    )(sink, qn, kn, kn, vd, vd, bias)


def _mlstm_kernel(qk_ref, v_ref, o_ref, gt_ref, ng_ref, out_ref, c_scr, m_scr):
    step = pl.program_id(0)
    batch = qk_ref.shape[0]
    L = ML_KCHUNK
    pairs = ML_HEADS // 2

    @pl.when(step == 0)
    def _():
        c_scr[...] = jnp.zeros(c_scr.shape, jnp.float32)
        m_scr[...] = jnp.zeros(m_scr.shape, jnp.float32)

    r_i = lax.broadcasted_iota(jnp.int32, (L, L), 0)
    c_i = lax.broadcasted_iota(jnp.int32, (L, L), 1)
    causal = c_i <= r_i
    tril_bf = causal.astype(jnp.bfloat16)
    triu_bf = (r_i <= c_i).astype(jnp.bfloat16)
    mean_dv = jnp.full((ML_DV, ML_DV), 1.0 / ML_DV, jnp.bfloat16)
    lane = lax.broadcasted_iota(jnp.int32, (L, LANES), 1)
    low = lane < ML_DQK
    ones_dv = jnp.ones((L, ML_DV), jnp.bfloat16)
    row2 = lax.broadcasted_iota(jnp.int32, (2 * ML_DQK, 2 * ML_DV), 0)

    def twice(a):
        return jnp.concatenate([a, a], axis=1)

    m_state = [m_scr[k] for k in range(batch * ML_HEADS)]
    c_state = [c_scr[k] for k in range(batch * pairs)]

    for ch in range(ML_STEP // L):
        r0 = ch * L
        for b in range(batch):
            gt = gt_ref[b, :, r0:r0 + L]
            g = jnp.concatenate([gt, jnp.zeros((LANES - SUBLANES, L), jnp.float32)], axis=0).T
            bcols = sum(jnp.dot(tril_bf, part, preferred_element_type=jnp.float32) for part in _split3(g))
            brows = sum(jnp.dot(part, triu_bf, preferred_element_type=jnp.float32) for part in _split3(gt))
            for pair in range(pairs):
                sidx = b * pairs + pair
                qp = qk_ref[b, r0:r0 + L, pair * LANES:(pair + 1) * LANES]
                kp = qk_ref[b, r0:r0 + L, (pairs + pair) * LANES:(pairs + pair + 1) * LANES]
                c_pair = c_state[sidx]
                c_bf = _bf16(c_pair)
                new_c = None
                decays = []
                for sub in range(2):
                    h = 2 * pair + sub
                    sel = low if sub == 0 else jnp.logical_not(low)
                    m_prev = m_state[b * ML_HEADS + h]
                    bc = jnp.broadcast_to(bcols[:, ML_HEADS + h:ML_HEADS + h + 1], (L, LANES))
                    lic = jnp.broadcast_to(g[:, h:h + 1], (L, LANES))
                    br = brows[ML_HEADS + h:ML_HEADS + h + 1, :]
                    lir = gt[h:h + 1, :]
                    log_d = jnp.where(causal, twice(bc) - (br - lir), NEG_BIG)
                    m_inter = bc + m_prev
                    row_max = jnp.broadcast_to(jnp.max(log_d, axis=-1, keepdims=True), (L, LANES))
                    m_row = jnp.maximum(m_inter, row_max)
                    d = jnp.exp(log_d - twice(m_row))
                    inter = jnp.exp(m_inter - m_row)
                    qm = jnp.where(sel, qp, jnp.zeros_like(qp))
                    s = lax.dot_general(qm, kp, (((1,), (1,)), ((), ())),
                                        preferred_element_type=jnp.float32) * d
                    v_ext = jnp.concatenate([v_ref[b, r0:r0 + L, h * ML_DV:(h + 1) * ML_DV], ones_dv],
                                            axis=-1)
                    num = twice(inter) * jnp.dot(qm, c_bf, preferred_element_type=jnp.float32) \
                        + jnp.dot(_bf16(s), v_ext, preferred_element_type=jnp.float32)
                    den = num[:, ML_DV:2 * ML_DV]
                    hval = num[:, 0:ML_DV] / jnp.maximum(jnp.abs(den), jnp.exp(-m_row))
                    h_ms = jnp.dot(_bf16(hval * hval), mean_dv, preferred_element_type=jnp.float32)
                    hn = hval * lax.rsqrt(h_ms + EPS)
                    hn = hn * ng_ref[:, h * ML_DV:(h + 1) * ML_DV]
                    out_ref[b, r0:r0 + L, h * ML_DV:(h + 1) * ML_DV] = _bf16(
                        hn * o_ref[b, r0:r0 + L, h * ML_DV:(h + 1) * ML_DV].astype(jnp.float32))
                    b_last = bc[L - 1:L, :]
                    log_w = b_last - bc + lic
                    m_next = jnp.maximum(b_last + m_prev, jnp.max(log_w, axis=0, keepdims=True))
                    w = jnp.exp(log_w - m_next)
                    decays.append(jnp.exp(b_last + m_prev - m_next))
                    m_state[b * ML_HEADS + h] = m_next
                    kw = _bf16(jnp.where(sel, kp.astype(jnp.float32) * w, 0.0))
                    upd = lax.dot_general(kw, v_ext, (((0,), (0,)), ((), ())),
                                          preferred_element_type=jnp.float32)
                    new_c = upd if new_c is None else new_c + upd
                decay_rows = jnp.where(row2 < ML_DQK, twice(decays[0]), twice(decays[1]))
                c_state[sidx] = decay_rows * c_pair + new_c

    for k in range(batch * ML_HEADS):
        m_scr[k] = m_state[k]
    for k in range(batch * pairs):
        c_scr[k] = c_state[k]


def _mlstm(qkm, vm, om, gates_t, ng, batch, seq_len):
    n_steps = seq_len // ML_STEP
    blk = lambda c: (0, c, 0)
    return pl.pallas_call(
        _mlstm_kernel,
        grid=(n_steps,),
        in_specs=[
            pl.BlockSpec((batch, ML_STEP, 2 * ML_HEADS * ML_DQK), blk),
            pl.BlockSpec((batch, ML_STEP, ML_WIDTH), blk),
            pl.BlockSpec((batch, ML_STEP, ML_WIDTH), blk),
            pl.BlockSpec((batch, SUBLANES, ML_STEP), lambda c: (0, 0, c)),
            pl.BlockSpec((1, ML_WIDTH), lambda c: (0, 0)),
        ],
        out_specs=pl.BlockSpec((batch, ML_STEP, ML_WIDTH), blk),
        out_shape=jax.ShapeDtypeStruct((batch, seq_len, ML_WIDTH), jnp.bfloat16),
        scratch_shapes=[
            pltpu.VMEM((batch * ML_HEADS // 2, 2 * ML_DQK, 2 * ML_DV), jnp.float32),
            pltpu.VMEM((batch * ML_HEADS, 1, LANES), jnp.float32),
        ],
        compiler_params=pltpu.CompilerParams(dimension_semantics=("arbitrary",),
                                             vmem_limit_bytes=VMEM_LIMIT),
        name="mlstm_scan",
    )(qkm, vm, om, gates_t, ng)


def _out_proj_router_kernel(x_ref, att_ref, hm_ref, wo_ref, g_ref, wrt_ref, brt_ref,
                            x1_ref, t_ref, row_ref, col_ref, cnt_ref, wo_bf):
    tm = x_ref.shape[0]
    tr = TM_PROJ
    hi = lax.Precision.HIGHEST

    @pl.when(pl.program_id(0) == 0)
    def _():
        wo_bf[...] = _bf16(wo_ref[...])

    for sb in range(tm // TM_PROJ):
        rs = slice(sb * TM_PROJ, (sb + 1) * TM_PROJ)
        x1 = x_ref[rs, :] \
            + jnp.dot(att_ref[rs, :], wo_bf[0:ATT_WIDTH, :], preferred_element_type=jnp.float32) \
            + jnp.dot(hm_ref[rs, :], wo_bf[ATT_WIDTH:, :], preferred_element_type=jnp.float32)
        x1_ref[rs, :] = x1
        tn = x1 * lax.rsqrt(jnp.mean(x1 * x1, axis=-1, keepdims=True) + EPS) * g_ref[...]
        tn_hi = _bf16(tn)
        t_ref[rs, :] = tn_hi
        tn_lo = _bf16(tn - tn_hi.astype(jnp.float32))
        nt = (((1,), (1,)), ((), ()))
        n_rt = wrt_ref.shape[0] // 2
        p_hi = lax.dot_general(wrt_ref[...], tn_hi, nt, preferred_element_type=jnp.float32)
        p_lo = lax.dot_general(wrt_ref[0:n_rt, :], tn_lo, nt, preferred_element_type=jnp.float32)
        logits = p_hi[0:n_rt, :] + p_hi[n_rt:, :] + p_lo + brt_ref[...]
        el_all = logits[0:N_EXPERTS, :]
        gl = logits[N_EXPERTS:N_EXPERTS + SUBLANES, :]
        grow = lax.broadcasted_iota(jnp.int32, gl.shape, 0).astype(jnp.float32)
        gl = jnp.where(grow < N_GROUPS, gl, NEG_BIG)
        gmax = jnp.max(gl, axis=0, keepdims=True)
        grp = jnp.min(jnp.where(gl == gmax, grow, float(N_GROUPS)), axis=0, keepdims=True)
        p_grp = 1.0 / jnp.sum(jnp.exp(gl - gmax), axis=0, keepdims=True)
        erow = lax.broadcasted_iota(jnp.int32, el_all.shape, 0).astype(jnp.float32)
        egrp = jnp.floor(erow * (1.0 / EXPERTS_PER_GROUP))
        el = jnp.where(egrp == grp, el_all, NEG_BIG)
        e1 = jnp.max(el, axis=0, keepdims=True)
        i1 = jnp.min(jnp.where(el == e1, erow, float(N_EXPERTS)), axis=0, keepdims=True)
        el2 = jnp.where(erow == i1, NEG_BIG, el)
        e2 = jnp.max(el2, axis=0, keepdims=True)
        i2 = jnp.min(jnp.where(el2 == e2, erow, float(N_EXPERTS)), axis=0, keepdims=True)
        z2 = jnp.exp(e2 - e1)
        w1 = p_grp / (1.0 + z2)
        w2 = p_grp * z2 / (1.0 + z2)
        sel1 = erow == i1
        sel2 = erow == i2
        onehot = jnp.logical_or(sel1, sel2)
        t_r = lax.broadcasted_iota(jnp.int32, (tr, tr), 0)
        t_c = lax.broadcasted_iota(jnp.int32, (tr, tr), 1)
        before = (t_r < t_c).astype(jnp.bfloat16)
        rank = jnp.dot(onehot.astype(jnp.bfloat16), before, preferred_element_type=jnp.float32)
        cnt = jnp.sum(onehot.astype(jnp.float32), axis=1, keepdims=True)
        cnt_al = jnp.floor((cnt + (SEG_ALIGN - 1)) * (1.0 / SEG_ALIGN)) * SEG_ALIGN
        e_r = lax.broadcasted_iota(jnp.int32, (N_EXPERTS, N_EXPERTS), 0)
        e_c = lax.broadcasted_iota(jnp.int32, (N_EXPERTS, N_EXPERTS), 1)
        lstart = jnp.dot((e_c < e_r).astype(jnp.float32), jnp.broadcast_to(cnt_al, (N_EXPERTS, LANES)),
                         precision=hi, preferred_element_type=jnp.float32)[:, 0:1]
        slot = lstart + rank
        pos1 = jnp.sum(jnp.where(sel1, slot, 0.0), axis=0, keepdims=True)
        pos2 = jnp.sum(jnp.where(sel2, slot, 0.0), axis=0, keepdims=True)
        r8 = lax.broadcasted_iota(jnp.int32, (SUBLANES, tr), 0)
        info = jnp.where(r8 == ROW_POS0, pos1, jnp.where(r8 == ROW_POS1, pos2, jnp.where(
            r8 == ROW_W0, w1, jnp.where(r8 == ROW_W1, w2, jnp.where(
                r8 == ROW_E0, i1, jnp.where(r8 == ROW_E1, i2, 0.0))))))
        row_ref[:, rs] = info

        def split3(w):
            h = _bf16(w).astype(jnp.float32)
            m = _bf16(w - h).astype(jnp.float32)
            return h, m, _bf16(w - h - m).astype(jnp.float32)

        w1h, w1m, w1l = split3(w1)
        w2h, w2m, w2l = split3(w2)
        parts = jnp.where(r8 == 0, w1h, jnp.where(r8 == 1, w1m, jnp.where(r8 == 2, w1l, jnp.where(
            r8 == 3, w2h, jnp.where(r8 == 4, w2m, jnp.where(r8 == 5, w2l, jnp.where(r8 == 6, i1, i2)))))))
        col_ref[rs, :] = jnp.concatenate(
            [info, parts, jnp.zeros((LANES - 2 * SUBLANES, tr), jnp.float32)], axis=0).T
        cnt_ref[sb] = jnp.broadcast_to(cnt, (N_EXPERTS, LANES)).astype(jnp.int32)


def _out_proj_router(x2, att, hm, wo, layer, g, wrt, brt):
    t = x2.shape[0]
    tm = TM_OUT
    row = lambda i: (i, 0)
    fix = lambda i: (0, 0)
    return pl.pallas_call(
        _out_proj_router_kernel,
        grid=(t // tm,),
        in_specs=[
            pl.BlockSpec((tm, D_MODEL), row),
            pl.BlockSpec((tm, ATT_WIDTH), row),
            pl.BlockSpec((tm, ML_WIDTH), row),
            pl.BlockSpec((None, D_MODEL, D_MODEL), lambda i: (layer, 0, 0), pipeline_mode=pl.Buffered(1)),
            pl.BlockSpec((1, D_MODEL), fix),
            pl.BlockSpec((8 * SUBLANES, D_MODEL), fix),
            pl.BlockSpec((4 * SUBLANES, 1), fix),
        ],
        out_specs=[
            pl.BlockSpec((tm, D_MODEL), row),
            pl.BlockSpec((tm, D_MODEL), row),
            pl.BlockSpec((SUBLANES, tm), lambda i: (0, i)),
            pl.BlockSpec((tm, LANES), row),
            pl.BlockSpec((tm // TM_PROJ, N_EXPERTS, LANES), lambda i: (i, 0, 0)),
        ],
        out_shape=(
            jax.ShapeDtypeStruct((t, D_MODEL), jnp.float32),
            jax.ShapeDtypeStruct((t, D_MODEL), jnp.bfloat16),
            jax.ShapeDtypeStruct((SUBLANES, t), jnp.float32),
            jax.ShapeDtypeStruct((t, LANES), jnp.float32),
            jax.ShapeDtypeStruct((t // TM_PROJ, N_EXPERTS, LANES), jnp.int32),
        ),
        scratch_shapes=[pltpu.VMEM((D_MODEL, D_MODEL), jnp.bfloat16)],
        compiler_params=pltpu.CompilerParams(dimension_semantics=("arbitrary",),
                                             vmem_limit_bytes=VMEM_LIMIT),
        name="out_proj_router",
    )(x2, att, hm, wo, g, wrt, brt)


def _chunk_rows(j):
    return pl.ds(pl.multiple_of(j * SEG_ALIGN, SEG_ALIGN), SEG_ALIGN)


def _dispatch_kernel(nch_ref, dst_ref, nz_ref, zdst_ref, t_ref, col_ref, row_ref, xs_hbm,
                     xbuf, zx, sem_x, sem_z):
    i = pl.program_id(0)
    n = pl.num_programs(0)
    p = i % 2
    tm = t_ref.shape[0]

    def copy(tile, par, j):
        d = pl.ds(pl.multiple_of(dst_ref[tile, j], SEG_ALIGN), SEG_ALIGN)
        return pltpu.make_async_copy(xbuf.at[par, _chunk_rows(j)], xs_hbm.at[d], sem_x.at[par])

    def zero_copy(e, j):
        d = pl.ds(pl.multiple_of(zdst_ref[e] + j * SEG_ALIGN, SEG_ALIGN), SEG_ALIGN)
        return pltpu.make_async_copy(zx, xs_hbm.at[d], sem_z.at[0])

    def for_zero_chunks(fn):
        for e in range(N_EXPERTS):
            def body(j, c, e=e):
                fn(zero_copy(e, j))
                return c
            lax.fori_loop(0, nz_ref[e], body, 0)

    def wait_tile(tile, par):
        def body(j, c):
            copy(tile, par, j).wait()
            return c
        lax.fori_loop(0, nch_ref[tile], body, 0)

    @pl.when(i == 0)
    def _():
        zx[...] = jnp.zeros(zx.shape, zx.dtype)
        for_zero_chunks(lambda cp: cp.start())

    @pl.when(i >= 2)
    def _():
        wait_tile(i - 2, p)

    pos0 = row_ref[ROW_POS0:ROW_POS0 + 1, :].astype(jnp.int32)
    pos1 = row_ref[ROW_POS1:ROW_POS1 + 1, :].astype(jnp.int32)
    t = t_ref[...]
    side = _bf16(col_ref[...])
    rows = DISPATCH_ROWS
    for c in range(L_CAP // rows):
        r = lax.broadcasted_iota(jnp.int32, (rows, tm), 0) + c * rows
        perm = jnp.logical_or(r == pos0, r == pos1).astype(jnp.bfloat16)
        xbuf[p, c * rows:(c + 1) * rows, 0:D_MODEL] = _bf16(
            jnp.dot(perm, t, preferred_element_type=jnp.float32))
        xbuf[p, c * rows:(c + 1) * rows, D_MODEL:D_XS] = _bf16(
            jnp.dot(perm, side, preferred_element_type=jnp.float32))

    def start_body(j, c):
        copy(i, p, j).start()
        return c
    lax.fori_loop(0, nch_ref[i], start_body, 0)

    @pl.when(i == n - 1)
    def _():
        @pl.when(n >= 2)
        def _():
            wait_tile(i - 1, 1 - p)
        wait_tile(i, p)
        for_zero_chunks(lambda cp: cp.wait())


def _dispatch(t, col, row, nch, dst, nz, zdst):
    n_tok = t.shape[0]
    r_cap, _ = _moe_capacity(n_tok)
    tm = TM_PROJ
    grid_spec = pltpu.PrefetchScalarGridSpec(
        num_scalar_prefetch=4,
        grid=(n_tok // tm,),
        in_specs=[
            pl.BlockSpec((tm, D_MODEL), lambda i, *_: (i, 0)),
            pl.BlockSpec((tm, LANES), lambda i, *_: (i, 0)),
            pl.BlockSpec((SUBLANES, tm), lambda i, *_: (0, i)),
        ],
        out_specs=pl.BlockSpec(memory_space=pl.ANY),
        scratch_shapes=[
            pltpu.VMEM((2, L_CAP, D_XS), jnp.bfloat16),
            pltpu.VMEM((SEG_ALIGN, D_XS), jnp.bfloat16),
            pltpu.SemaphoreType.DMA((2,)),
            pltpu.SemaphoreType.DMA((1,)),
        ],
    )
    return pl.pallas_call(
        _dispatch_kernel,
        grid_spec=grid_spec,
        out_shape=jax.ShapeDtypeStruct((r_cap, D_XS), jnp.bfloat16),
        compiler_params=pltpu.CompilerParams(dimension_semantics=("arbitrary",),
                                             vmem_limit_bytes=VMEM_LIMIT),
        name="moe_dispatch",
    )(nch, dst, nz, zdst, t, col, row)


def _moe_kernel(blk_e_ref, nused_ref, xs_ref, wg_ref, wu_ref, wd_ref, ys_ref, wg_bf, wu_bf, wd_bf):
    b = pl.program_id(0)
    new_expert = jnp.logical_or(b == 0, blk_e_ref[b] != blk_e_ref[jnp.maximum(b - 1, 0)])

    @pl.when(new_expert)
    def _():
        wg_bf[...] = _bf16(wg_ref[...])
        wu_bf[...] = _bf16(wu_ref[...])
        wd_bf[...] = _bf16(wd_ref[...])

    @pl.when(b < nused_ref[0])
    def _():
        x = xs_ref[:, 0:D_MODEL]
        a = jnp.dot(x, wg_bf[...], preferred_element_type=jnp.float32)
        u = jnp.dot(x, wu_bf[...], preferred_element_type=jnp.float32)
        h = a * jax.nn.sigmoid(a) * u
        y = jnp.dot(_bf16(h), wd_bf[...], preferred_element_type=jnp.float32)
        sd = xs_ref[:, D_MODEL:D_XS].astype(jnp.float32)
        e_blk = blk_e_ref[b].astype(jnp.float32)
        w0 = sd[:, COL_W0H:COL_W0H + 1] + sd[:, COL_W0H + 1:COL_W0H + 2] + sd[:, COL_W0H + 2:COL_W0H + 3]
        w1 = sd[:, COL_W1H:COL_W1H + 1] + sd[:, COL_W1H + 1:COL_W1H + 2] + sd[:, COL_W1H + 2:COL_W1H + 3]
        w = jnp.where(sd[:, COL_E0:COL_E0 + 1] == e_blk, w0, w1)
        ys_ref[...] = _bf16(y * w)


def _moe(xs, blk_e, nused, wg, wu, wd, layer):
    blk = lambda b, be, nu: (jnp.maximum(jnp.minimum(b, nu[0] - 1), 0), 0)
    wsel = lambda b, be, nu: (layer, be[b], 0, 0)
    r_cap = xs.shape[0]
    grid_spec = pltpu.PrefetchScalarGridSpec(
        num_scalar_prefetch=2,
        grid=(r_cap // MOE_BM,),
        in_specs=[
            pl.BlockSpec((MOE_BM, D_XS), blk),
            pl.BlockSpec((None, None, D_MODEL, D_FF_EXPERT), wsel),
            pl.BlockSpec((None, None, D_MODEL, D_FF_EXPERT), wsel),
            pl.BlockSpec((None, None, D_FF_EXPERT, D_MODEL), wsel),
        ],
        out_specs=pl.BlockSpec((MOE_BM, D_MODEL), blk),
        scratch_shapes=[
            pltpu.VMEM((D_MODEL, D_FF_EXPERT), jnp.bfloat16),
            pltpu.VMEM((D_MODEL, D_FF_EXPERT), jnp.bfloat16),
            pltpu.VMEM((D_FF_EXPERT, D_MODEL), jnp.bfloat16),
        ],
    )
    return pl.pallas_call(
        _moe_kernel,
        grid_spec=grid_spec,
        out_shape=jax.ShapeDtypeStruct((r_cap, D_MODEL), jnp.bfloat16),
        compiler_params=pltpu.CompilerParams(dimension_semantics=("arbitrary",),
                                             vmem_limit_bytes=VMEM_LIMIT),
        name="moe_experts",
    )(blk_e, nused, xs, wg, wu, wd)


def _combine_kernel(nch_ref, dst_ref, x1_ref, col_ref, ys_hbm, out_ref, ybuf, sem):
    i = pl.program_id(0)
    n = pl.num_programs(0)
    p = i % 2
    tm = x1_ref.shape[0]

    def copy(tile, par, j):
        s = pl.ds(pl.multiple_of(dst_ref[tile, j], SEG_ALIGN), SEG_ALIGN)
        return pltpu.make_async_copy(ys_hbm.at[s], ybuf.at[par, _chunk_rows(j)], sem.at[par])

    def start_tile(tile, par):
        def body(j, c):
            copy(tile, par, j).start()
            return c
        lax.fori_loop(0, nch_ref[tile], body, 0)

    @pl.when(i == 0)
    def _():
        ybuf[...] = jnp.zeros(ybuf.shape, ybuf.dtype)
        start_tile(0, 0)

    @pl.when(i + 1 < n)
    def _():
        start_tile(i + 1, 1 - p)

    def wait_body(j, c):
        copy(i, p, j).wait()
        return c
    lax.fori_loop(0, nch_ref[i], wait_body, 0)

    rows = COMBINE_ROWS
    for c in range(tm // rows):
        col = col_ref[c * rows:(c + 1) * rows, :]
        pos0 = col[:, ROW_POS0:ROW_POS0 + 1].astype(jnp.int32)
        pos1 = col[:, ROW_POS1:ROW_POS1 + 1].astype(jnp.int32)
        l = lax.broadcasted_iota(jnp.int32, (rows, L_CAP), 1)
        perm = jnp.logical_or(l == pos0, l == pos1).astype(jnp.bfloat16)
        y = jnp.dot(perm, ybuf[p], preferred_element_type=jnp.float32)
        out_ref[c * rows:(c + 1) * rows, :] = x1_ref[c * rows:(c + 1) * rows, :] + y


def _combine(x1, col, ys, nch, dst):
    n_tok = x1.shape[0]
    tm = TM_PROJ
    grid_spec = pltpu.PrefetchScalarGridSpec(
        num_scalar_prefetch=2,
        grid=(n_tok // tm,),
        in_specs=[
            pl.BlockSpec((tm, D_MODEL), lambda i, *_: (i, 0)),
            pl.BlockSpec((tm, LANES), lambda i, *_: (i, 0)),
            pl.BlockSpec(memory_space=pl.ANY),
        ],
        out_specs=pl.BlockSpec((tm, D_MODEL), lambda i, *_: (i, 0)),
        scratch_shapes=[
            pltpu.VMEM((2, L_CAP, D_MODEL), jnp.bfloat16),
            pltpu.SemaphoreType.DMA((2,)),
        ],
    )
    return pl.pallas_call(
        _combine_kernel,
        grid_spec=grid_spec,
        out_shape=jax.ShapeDtypeStruct((n_tok, D_MODEL), jnp.float32),
        compiler_params=pltpu.CompilerParams(dimension_semantics=("arbitrary",),
                                             vmem_limit_bytes=VMEM_LIMIT),
        name="moe_combine",
    )(nch, dst, x1, col, ys)


def _routing_tables(counts, n_blk_cap):
    i32 = jnp.int32
    ca = ((counts + (SEG_ALIGN - 1)) // SEG_ALIGN) * SEG_ALIGN
    lend = jnp.cumsum(ca, axis=1)
    lstart = lend - ca
    nch = (lend[:, -1] // SEG_ALIGN).astype(i32)
    tot = jnp.sum(ca, axis=0)
    region = ((tot + (MOE_BM - 1)) // MOE_BM) * MOE_BM
    rend = jnp.cumsum(region)
    base = rend - region
    gstart = base[None, :] + jnp.cumsum(ca, axis=0) - ca
    j16 = jnp.arange(N_CHUNK, dtype=i32) * SEG_ALIGN
    in_seg = jnp.logical_and(lstart[:, None, :] <= j16[None, :, None], j16[None, :, None] < lend[:, None, :])
    dst = (jnp.sum(jnp.where(in_seg, (gstart - lstart)[:, None, :], 0), axis=2) + j16[None, :]).astype(i32)
    nused = (rend[-1] // MOE_BM).astype(i32)
    brow = jnp.minimum(jnp.arange(n_blk_cap, dtype=i32), nused - 1) * MOE_BM
    blk_e = jnp.minimum(jnp.sum(rend[None, :] <= brow[:, None], axis=1), N_EXPERTS - 1).astype(i32)
    zstart = (base + tot).astype(i32)
    nzc = ((region - tot) // SEG_ALIGN).astype(i32)
    return nch, dst, nzc, zstart, blk_e, nused[None]


def _t5_bucket_np(dist):
    max_exact = N_BUCKETS // 2
    d = np.maximum(dist, 1).astype(np.float32)
    large = max_exact + (np.log(d / max_exact) / math.log(MAX_DISTANCE / max_exact)
                         * (N_BUCKETS - max_exact)).astype(np.int32)
    large = np.minimum(large, N_BUCKETS - 1)
    return np.where(dist < max_exact, dist, large)


def _attention_bias(rel_bias):
    qi = np.arange(ATT_BLOCK)[:, None]
    kj = np.arange(2 * ATT_BLOCK)[None, :]
    dist = qi + ATT_BLOCK - kj
    in_window = (dist >= 0) & (dist < WINDOW)
    bucket = _t5_bucket_np(np.clip(dist, 0, WINDOW - 1))
    onehot = (bucket[None] == np.arange(N_BUCKETS)[:, None, None]).astype(np.float32)
    bias = jnp.einsum('nh,nqk->hqk', rel_bias.astype(jnp.float32), jnp.asarray(onehot),
                      precision=lax.Precision.HIGHEST)
    bias = jnp.where(jnp.asarray(in_window)[None], bias, NEG_BIG)
    no_prev = jnp.asarray(kj < ATT_BLOCK)[None]
    return jnp.stack([bias, jnp.where(no_prev, NEG_BIG, bias)])


def _block_diag_mean(width, block):
    idx = np.arange(width) // block
    return jnp.asarray((idx[:, None] == idx[None, :]).astype(np.float32) / block, dtype=jnp.bfloat16)


def kernel(x, rel_bias, norm_mix_g, w_in, q_norm_g, k_norm_g, attn_sink, conv_w, conv_b, gate_b,
           mlstm_norm_g, w_out, norm_ffn_g, w_router_group, b_router_group, w_router_expert,
           b_router_expert, w_gate, w_up, w_down):
    batch, seq_len, _ = x.shape
    n_tok = batch * seq_len
    assert seq_len % TM_PROJ == 0 and seq_len % TM_IN == 0 and seq_len % ML_STEP == 0
    f32 = jnp.float32
    bias = _attention_bias(rel_bias)
    bdq = _block_diag_mean(ATT_WIDTH, ATT_HEAD_DIM)
    bdk = _block_diag_mean(LANES, ATT_HEAD_DIM)
    x2 = x.reshape(n_tok, D_MODEL)
    for l in range(DEPTH):
        wgt = _bf16(w_in[l, :, O_G:N_IN].T)
        qg = (jnp.tile(q_norm_g[l].astype(f32), ATT_HEADS) * (ATT_HEAD_DIM ** -0.5))[None, :]
        kg = jnp.tile(k_norm_g[l].astype(f32), ATT_KV_HEADS)[None, :]
        gbt = gate_b[l].astype(f32)[:, None]
        qn, kn, vd, qkm, vm, om, gates_t = _in_proj(
            x2, norm_mix_g[l][None, :], w_in, l, wgt, qg, kg, bdq, bdk,
            conv_w[l], conv_b[l][None, :], gbt, batch, seq_len)
        att = _attention(qn, kn, vd, bias, attn_sink[l].astype(f32), batch, seq_len)
        r3 = lambda a: a.reshape(batch, seq_len, a.shape[-1])
        hm = _mlstm(r3(qkm), r3(vm), r3(om), gates_t, mlstm_norm_g[l][None, :], batch, seq_len)
        n_rt = 4 * SUBLANES
        wrt = jnp.pad(jnp.concatenate([w_router_expert[l], w_router_group[l]], axis=1).astype(f32).T,
                      ((0, n_rt - N_EXPERTS - N_GROUPS), (0, 0)))
        brt = jnp.pad(jnp.concatenate([b_router_expert[l], b_router_group[l]]).astype(f32),
                      (0, n_rt - N_EXPERTS - N_GROUPS))[:, None]
        wrt_hi = _bf16(wrt)
        wrt = jnp.concatenate([wrt_hi, _bf16(wrt - wrt_hi.astype(f32))], axis=0)
        x1, t, row, col, cnt = _out_proj_router(x2, att, hm.reshape(n_tok, ML_WIDTH), w_out, l,
                                                norm_ffn_g[l][None, :], wrt, brt)
        _, n_blk_cap = _moe_capacity(n_tok)
        nch, dst, nz, zdst, blk_e, nused = _routing_tables(cnt[:, :, 0], n_blk_cap)
        xs = _dispatch(t, col, row, nch, dst, nz, zdst)
        ys = _moe(xs, blk_e, nused, w_gate, w_up, w_down, l)
        x2 = _combine(x1, col, ys, nch, dst)
    return x2.reshape(batch, seq_len, D_MODEL)
```

```python
import functools
import math

import jax
import jax.numpy as jnp
import numpy as np
from jax import lax
from jax.experimental import pallas as pl
from jax.experimental.pallas import tpu as pltpu

D_MODEL = 1024
DEPTH = 2
ATT_HEADS = 8
ATT_KV_HEADS = 2
ATT_HEAD_DIM = 64
ATT_WIDTH = ATT_HEADS * ATT_HEAD_DIM
WINDOW = 128
ATT_BLOCK = 128
N_BUCKETS = 32
MAX_DISTANCE = 128
ML_HEADS = 4
ML_DQK = 64
ML_DV = 128
ML_WIDTH = ML_HEADS * ML_DV
ML_CHUNK = 64
CONV_K = 4
N_GROUPS = 4
EXPERTS_PER_GROUP = 4
N_EXPERTS = N_GROUPS * EXPERTS_PER_GROUP
D_FF_EXPERT = 512
EPS = 1e-6

LANES = 128
SUBLANES = 8
NEG_BIG = -1e30
VMEM_LIMIT = 48 * 1024 * 1024

O_Q = 0
O_K = O_Q + ATT_WIDTH
O_V = O_K + ATT_KV_HEADS * ATT_HEAD_DIM
O_QM = O_V + ATT_KV_HEADS * ATT_HEAD_DIM
O_KM = O_QM + ML_HEADS * ML_DQK
O_VM = O_KM + ML_HEADS * ML_DQK
O_OM = O_VM + ML_WIDTH
O_G = O_OM + ML_WIDTH
N_IN = O_G + 2 * ML_HEADS

TM_IN = 1024
IN_SUBBLOCKS = 2
TM_OUT = 1024
TM_PROJ = 512
ML_KCHUNK = 256
ML_STEP = ML_KCHUNK

SEG_ALIGN = 16
MOE_BM = 512
L_CAP = 2 * TM_PROJ + N_EXPERTS * SEG_ALIGN
N_CHUNK = L_CAP // SEG_ALIGN
D_XS = D_MODEL + LANES
DISPATCH_ROWS = 256
COMBINE_ROWS = 256
ROW_POS0, ROW_POS1, ROW_W0, ROW_W1, ROW_E0, ROW_E1 = 0, 1, 2, 3, 4, 5
COL_W0H, COL_W1H, COL_E0, COL_E1 = 8, 11, 14, 15


def _moe_capacity(n_tok):
    n_tiles = n_tok // TM_PROJ
    rows = 2 * n_tok + n_tiles * N_EXPERTS * (SEG_ALIGN - 1) + N_EXPERTS * (MOE_BM - SEG_ALIGN)
    n_blk = -(-rows // MOE_BM)
    return n_blk * MOE_BM, n_blk


def _bf16(a):
    return a.astype(jnp.bfloat16)


def _split3(a):
    hi = _bf16(a)
    r1 = a - hi.astype(jnp.float32)
    mid = _bf16(r1)
    return hi, mid, _bf16(r1 - mid.astype(jnp.float32))


def _log_sigmoid(z):
    return jnp.minimum(z, 0.0) - jnp.log(1.0 + jnp.exp(-jnp.abs(z)))


def _in_proj_kernel(tiles_per_seq, x_ref, g_ref, w_ref, wgt_ref, qg_ref, kg_ref, bdq_ref, bdk_ref,
                    cw_ref, cb_ref, gbt_ref,
                    qn_ref, kn_ref, vd_ref, qkm_ref, vm_ref, om_ref, gatet_ref,
                    w_bf, conv_scr):
    i = pl.program_id(0)
    tm = x_ref.shape[0]

    @pl.when(i == 0)
    def _():
        w_bf[...] = _bf16(w_ref[...])

    conv_w = 2 * ML_HEADS * ML_DQK

    @pl.when(i % tiles_per_seq == 0)
    def _():
        conv_scr[0:SUBLANES, :] = jnp.zeros((SUBLANES, conv_w), jnp.float32)

    @pl.when(i % tiles_per_seq != 0)
    def _():
        conv_scr[0:SUBLANES, :] = conv_scr[tm:tm + SUBLANES, :]

    sub = tm // IN_SUBBLOCKS
    for sb in range(IN_SUBBLOCKS):
        rs = slice(sb * sub, (sb + 1) * sub)
        x = x_ref[rs, :]
        hn = x * lax.rsqrt(jnp.mean(x * x, axis=-1, keepdims=True) + EPS) * g_ref[...]
        hb = _bf16(hn)

        def proj(c0, width):
            return jnp.dot(hb, w_bf[:, c0:c0 + width], preferred_element_type=jnp.float32)

        q = proj(O_Q, ATT_WIDTH)
        q_ms = jnp.dot(_bf16(q * q), bdq_ref[...], preferred_element_type=jnp.float32)
        qn_ref[rs, :] = _bf16(q * lax.rsqrt(q_ms + EPS) * qg_ref[...])
        kv = proj(O_K, 2 * LANES)
        k = kv[:, 0:LANES]
        v = kv[:, LANES:2 * LANES]
        k_ms = jnp.dot(_bf16(k * k), bdk_ref[...], preferred_element_type=jnp.float32)
        kn = k * lax.rsqrt(k_ms + EPS) * kg_ref[...]
        low = lax.broadcasted_iota(jnp.int32, kn.shape, 1) < ATT_HEAD_DIM

        def dup_heads(a):
            swapped = pltpu.roll(a, ATT_HEAD_DIM, axis=1)
            return jnp.concatenate([jnp.where(low, a, swapped), jnp.where(low, swapped, a)], axis=1)

        kn_ref[rs, :] = _bf16(dup_heads(kn))
        vd_ref[rs, :] = _bf16(dup_heads(v))

        qk = proj(O_QM, conv_w)
        base = SUBLANES + sb * sub
        conv_scr[base:base + sub, :] = qk
        y = qk * cw_ref[CONV_K - 1:CONV_K, :] + cb_ref[...]
        for j in range(CONV_K - 1):
            off = base - (CONV_K - 1) + j
            y = y + conv_scr[off:off + sub, :] * cw_ref[j:j + 1, :]
        y = y * jax.nn.sigmoid(y)
        lane = lax.broadcasted_iota(jnp.int32, y.shape, 1)
        y = jnp.where(lane >= ML_HEADS * ML_DQK, y * (ML_DQK ** -0.5), y)
        qkm_ref[rs, :] = _bf16(y)

        vm_ref[rs, :] = _bf16(proj(O_VM, ML_WIDTH))
        om_ref[rs, :] = _bf16(jax.nn.sigmoid(proj(O_OM, ML_WIDTH)))

        gt = lax.dot_general(wgt_ref[...], hb, (((1,), (1,)), ((), ())),
                             preferred_element_type=jnp.float32) + gbt_ref[...]
        grow = lax.broadcasted_iota(jnp.int32, gt.shape, 0)
        gatet_ref[:, rs] = jnp.where(grow >= ML_HEADS, _log_sigmoid(gt), gt)


def _in_proj(x2, g, w_in, layer, wgt, qg, kg, bdq, bdk, cw, cb, gbt, batch, seq_len):
    t = x2.shape[0]
    tm = TM_IN
    n = t // tm
    tps = seq_len // tm
    row = lambda i: (i, 0)
    fix = lambda i: (0, 0)
    kv_w = 2 * ATT_KV_HEADS * ATT_HEAD_DIM
    out_shapes = (
        jax.ShapeDtypeStruct((t, ATT_WIDTH), jnp.bfloat16),
        jax.ShapeDtypeStruct((t, kv_w), jnp.bfloat16),
        jax.ShapeDtypeStruct((t, kv_w), jnp.bfloat16),
        jax.ShapeDtypeStruct((t, 2 * ML_HEADS * ML_DQK), jnp.bfloat16),
        jax.ShapeDtypeStruct((t, ML_WIDTH), jnp.bfloat16),
        jax.ShapeDtypeStruct((t, ML_WIDTH), jnp.bfloat16),
        jax.ShapeDtypeStruct((batch, SUBLANES, seq_len), jnp.float32),
    )
    in_specs = [
        pl.BlockSpec((tm, D_MODEL), row),
        pl.BlockSpec((1, D_MODEL), fix),
        pl.BlockSpec((None, D_MODEL, N_IN), lambda i: (layer, 0, 0), pipeline_mode=pl.Buffered(1)),
        pl.BlockSpec((SUBLANES, D_MODEL), fix),
        pl.BlockSpec((1, ATT_WIDTH), fix),
        pl.BlockSpec((1, LANES), fix),
        pl.BlockSpec((ATT_WIDTH, ATT_WIDTH), fix),
        pl.BlockSpec((LANES, LANES), fix),
        pl.BlockSpec((CONV_K, 2 * ML_HEADS * ML_DQK), fix),
        pl.BlockSpec((1, 2 * ML_HEADS * ML_DQK), fix),
        pl.BlockSpec((SUBLANES, 1), fix),
    ]
    out_specs = [
        pl.BlockSpec((tm, ATT_WIDTH), row),
        pl.BlockSpec((tm, kv_w), row),
        pl.BlockSpec((tm, kv_w), row),
        pl.BlockSpec((tm, 2 * ML_HEADS * ML_DQK), row),
        pl.BlockSpec((tm, ML_WIDTH), row),
        pl.BlockSpec((tm, ML_WIDTH), row),
        pl.BlockSpec((None, SUBLANES, tm), lambda i: (i // tps, 0, i % tps)),
    ]
    return pl.pallas_call(
        functools.partial(_in_proj_kernel, seq_len // tm),
        grid=(n,),
        in_specs=in_specs,
        out_specs=out_specs,
        out_shape=out_shapes,
        scratch_shapes=[pltpu.VMEM((D_MODEL, N_IN), jnp.bfloat16),
                        pltpu.VMEM((tm + 2 * SUBLANES, 2 * ML_HEADS * ML_DQK), jnp.float32)],
        compiler_params=pltpu.CompilerParams(dimension_semantics=("arbitrary",),
                                             vmem_limit_bytes=VMEM_LIMIT),
        name="in_proj",
    )(x2, g, w_in, wgt, qg, kg, bdq, bdk, cw, cb, gbt)


def _attn_kernel(sink_ref, q_ref, kp_ref, kc_ref, vp_ref, vc_ref, bias_ref, o_ref):
    i = pl.program_id(1)
    blk = q_ref.shape[0]
    lane = lax.broadcasted_iota(jnp.int32, (blk, LANES), 1)
    low = lane < ATT_HEAD_DIM
    first = (i == 0).astype(jnp.int32)
    group = ATT_HEADS // ATT_KV_HEADS
    for pair in range(ATT_HEADS // 2):
        kv = (2 * pair) // group
        qp = q_ref[:, pair * LANES:(pair + 1) * LANES]
        kcat = jnp.concatenate([kp_ref[:, kv * LANES:(kv + 1) * LANES],
                                kc_ref[:, kv * LANES:(kv + 1) * LANES]], axis=0)
        vcat = jnp.concatenate([
            jnp.concatenate([vp_ref[:, kv * LANES:(kv + 1) * LANES],
                             vc_ref[:, kv * LANES:(kv + 1) * LANES]], axis=0),
            jnp.ones((2 * blk, LANES), jnp.bfloat16)], axis=1)
        halves = []
        for sub in range(2):
            h = 2 * pair + sub
            qm = jnp.where(low if sub == 0 else jnp.logical_not(low), qp, jnp.zeros_like(qp))
            s = lax.dot_general(qm, kcat, (((1,), (1,)), ((), ())),
                                preferred_element_type=jnp.float32)
            logits = s + bias_ref[first, h]
            sink = sink_ref[h]
            row_max = jnp.broadcast_to(jnp.max(logits, axis=-1, keepdims=True), (blk, LANES))
            m = jnp.maximum(row_max, sink)
            p = jnp.exp(logits - jnp.concatenate([m, m], axis=1))
            o = jnp.dot(_bf16(p), vcat, preferred_element_type=jnp.float32)
            den = o[:, LANES:2 * LANES] + jnp.exp(sink - m)
            halves.append(o[:, 0:LANES] / den)
        o_ref[:, pair * LANES:(pair + 1) * LANES] = _bf16(jnp.where(low, halves[0], halves[1]))


def _attention(qn, kn, vd, bias, sink, batch, seq_len):
    nb = seq_len // ATT_BLOCK
    cur = lambda b, i, s: (b * nb + i, 0)
    prev = lambda b, i, s: (b * nb + jnp.maximum(i - 1, 0), 0)
    grid_spec = pltpu.PrefetchScalarGridSpec(
        num_scalar_prefetch=1,
        grid=(batch, nb),
        in_specs=[
            pl.BlockSpec((ATT_BLOCK, ATT_WIDTH), cur),
            pl.BlockSpec((ATT_BLOCK, 2 * LANES), prev),
            pl.BlockSpec((ATT_BLOCK, 2 * LANES), cur),
            pl.BlockSpec((ATT_BLOCK, 2 * LANES), prev),
            pl.BlockSpec((ATT_BLOCK, 2 * LANES), cur),
            pl.BlockSpec((2, ATT_HEADS, ATT_BLOCK, 2 * ATT_BLOCK), lambda b, i, s: (0, 0, 0, 0)),
        ],
        out_specs=pl.BlockSpec((ATT_BLOCK, ATT_WIDTH), cur),
    )
    return pl.pallas_call(
        _attn_kernel,
        grid_spec=grid_spec,
        out_shape=jax.ShapeDtypeStruct((batch * seq_len, ATT_WIDTH), jnp.bfloat16),
        compiler_params=pltpu.CompilerParams(dimension_semantics=("arbitrary", "arbitrary"),
                                             vmem_limit_bytes=VMEM_LIMIT),
        name="swa_attention",
    )(sink, qn, kn, kn, vd, vd, bias)


def _mlstm_kernel(qk_ref, v_ref, o_ref, gt_ref, ng_ref, out_ref, c_scr, m_scr):
    step = pl.program_id(0)
    batch = qk_ref.shape[0]
    L = ML_KCHUNK
    pairs = ML_HEADS // 2

    @pl.when(step == 0)
    def _():
        c_scr[...] = jnp.zeros(c_scr.shape, jnp.float32)
        m_scr[...] = jnp.zeros(m_scr.shape, jnp.float32)

    r_i = lax.broadcasted_iota(jnp.int32, (L, L), 0)
    c_i = lax.broadcasted_iota(jnp.int32, (L, L), 1)
    causal = c_i <= r_i
    tril_bf = causal.astype(jnp.bfloat16)
    triu_bf = (r_i <= c_i).astype(jnp.bfloat16)
    mean_dv = jnp.full((ML_DV, ML_DV), 1.0 / ML_DV, jnp.bfloat16)
    lane = lax.broadcasted_iota(jnp.int32, (L, LANES), 1)
    low = lane < ML_DQK
    ones_dv = jnp.ones((L, ML_DV), jnp.bfloat16)
    row2 = lax.broadcasted_iota(jnp.int32, (2 * ML_DQK, 2 * ML_DV), 0)

    def twice(a):
        return jnp.concatenate([a, a], axis=1)

    m_state = [m_scr[k] for k in range(batch * ML_HEADS)]
    c_state = [c_scr[k] for k in range(batch * pairs)]

    for ch in range(ML_STEP // L):
        r0 = ch * L
        for b in range(batch):
            gt = gt_ref[b, :, r0:r0 + L]
            g = jnp.concatenate([gt, jnp.zeros((LANES - SUBLANES, L), jnp.float32)], axis=0).T
            bcols = sum(jnp.dot(tril_bf, part, preferred_element_type=jnp.float32) for part in _split3(g))
            brows = sum(jnp.dot(part, triu_bf, preferred_element_type=jnp.float32) for part in _split3(gt))
            for pair in range(pairs):
                sidx = b * pairs + pair
                qp = qk_ref[b, r0:r0 + L, pair * LANES:(pair + 1) * LANES]
                kp = qk_ref[b, r0:r0 + L, (pairs + pair) * LANES:(pairs + pair + 1) * LANES]
                c_pair = c_state[sidx]
                c_bf = _bf16(c_pair)
                new_c = None
                decays = []
                for sub in range(2):
                    h = 2 * pair + sub
                    sel = low if sub == 0 else jnp.logical_not(low)
                    m_prev = m_state[b * ML_HEADS + h]
                    bc = jnp.broadcast_to(bcols[:, ML_HEADS + h:ML_HEADS + h + 1], (L, LANES))
                    lic = jnp.broadcast_to(g[:, h:h + 1], (L, LANES))
                    br = brows[ML_HEADS + h:ML_HEADS + h + 1, :]
                    lir = gt[h:h + 1, :]
                    log_d = jnp.where(causal, twice(bc) - (br - lir), NEG_BIG)
                    m_inter = bc + m_prev
                    row_max = jnp.broadcast_to(jnp.max(log_d, axis=-1, keepdims=True), (L, LANES))
                    m_row = jnp.maximum(m_inter, row_max)
                    d = jnp.exp(log_d - twice(m_row))
                    inter = jnp.exp(m_inter - m_row)
                    qm = jnp.where(sel, qp, jnp.zeros_like(qp))
                    s = lax.dot_general(qm, kp, (((1,), (1,)), ((), ())),
                                        preferred_element_type=jnp.float32) * d
                    v_ext = jnp.concatenate([v_ref[b, r0:r0 + L, h * ML_DV:(h + 1) * ML_DV], ones_dv],
                                            axis=-1)
                    num = twice(inter) * jnp.dot(qm, c_bf, preferred_element_type=jnp.float32) \
                        + jnp.dot(_bf16(s), v_ext, preferred_element_type=jnp.float32)
                    den = num[:, ML_DV:2 * ML_DV]
                    hval = num[:, 0:ML_DV] / jnp.maximum(jnp.abs(den), jnp.exp(-m_row))
                    h_ms = jnp.dot(_bf16(hval * hval), mean_dv, preferred_element_type=jnp.float32)
                    hn = hval * lax.rsqrt(h_ms + EPS)
                    hn = hn * ng_ref[:, h * ML_DV:(h + 1) * ML_DV]
                    out_ref[b, r0:r0 + L, h * ML_DV:(h + 1) * ML_DV] = _bf16(
                        hn * o_ref[b, r0:r0 + L, h * ML_DV:(h + 1) * ML_DV].astype(jnp.float32))
                    b_last = bc[L - 1:L, :]
                    log_w = b_last - bc + lic
                    m_next = jnp.maximum(b_last + m_prev, jnp.max(log_w, axis=0, keepdims=True))
                    w = jnp.exp(log_w - m_next)
                    decays.append(jnp.exp(b_last + m_prev - m_next))
                    m_state[b * ML_HEADS + h] = m_next
                    kw = _bf16(jnp.where(sel, kp.astype(jnp.float32) * w, 0.0))
                    upd = lax.dot_general(kw, v_ext, (((0,), (0,)), ((), ())),
                                          preferred_element_type=jnp.float32)
                    new_c = upd if new_c is None else new_c + upd
                decay_rows = jnp.where(row2 < ML_DQK, twice(decays[0]), twice(decays[1]))
                c_state[sidx] = decay_rows * c_pair + new_c

    for k in range(batch * ML_HEADS):
        m_scr[k] = m_state[k]
    for k in range(batch * pairs):
        c_scr[k] = c_state[k]


def _mlstm(qkm, vm, om, gates_t, ng, batch, seq_len):
    n_steps = seq_len // ML_STEP
    blk = lambda c: (0, c, 0)
    return pl.pallas_call(
        _mlstm_kernel,
        grid=(n_steps,),
        in_specs=[
            pl.BlockSpec((batch, ML_STEP, 2 * ML_HEADS * ML_DQK), blk),
            pl.BlockSpec((batch, ML_STEP, ML_WIDTH), blk),
            pl.BlockSpec((batch, ML_STEP, ML_WIDTH), blk),
            pl.BlockSpec((batch, SUBLANES, ML_STEP), lambda c: (0, 0, c)),
            pl.BlockSpec((1, ML_WIDTH), lambda c: (0, 0)),
        ],
        out_specs=pl.BlockSpec((batch, ML_STEP, ML_WIDTH), blk),
        out_shape=jax.ShapeDtypeStruct((batch, seq_len, ML_WIDTH), jnp.bfloat16),
        scratch_shapes=[
            pltpu.VMEM((batch * ML_HEADS // 2, 2 * ML_DQK, 2 * ML_DV), jnp.float32),
            pltpu.VMEM((batch * ML_HEADS, 1, LANES), jnp.float32),
        ],
        compiler_params=pltpu.CompilerParams(dimension_semantics=("arbitrary",),
                                             vmem_limit_bytes=VMEM_LIMIT),
        name="mlstm_scan",
    )(qkm, vm, om, gates_t, ng)


def _out_proj_router_kernel(x_ref, att_ref, hm_ref, wo_ref, g_ref, wrt_ref, brt_ref,
                            x1_ref, t_ref, row_ref, col_ref, cnt_ref, wo_bf):
    tm = x_ref.shape[0]
    tr = TM_PROJ
    hi = lax.Precision.HIGHEST

    @pl.when(pl.program_id(0) == 0)
    def _():
        wo_bf[...] = _bf16(wo_ref[...])

    for sb in range(tm // TM_PROJ):
        rs = slice(sb * TM_PROJ, (sb + 1) * TM_PROJ)
        x1 = x_ref[rs, :] \
            + jnp.dot(att_ref[rs, :], wo_bf[0:ATT_WIDTH, :], preferred_element_type=jnp.float32) \
            + jnp.dot(hm_ref[rs, :], wo_bf[ATT_WIDTH:, :], preferred_element_type=jnp.float32)
        x1_ref[rs, :] = x1
        tn = x1 * lax.rsqrt(jnp.mean(x1 * x1, axis=-1, keepdims=True) + EPS) * g_ref[...]
        tn_hi = _bf16(tn)
        t_ref[rs, :] = tn_hi
        tn_lo = _bf16(tn - tn_hi.astype(jnp.float32))
        nt = (((1,), (1,)), ((), ()))
        n_rt = wrt_ref.shape[0] // 2
        p_hi = lax.dot_general(wrt_ref[...], tn_hi, nt, preferred_element_type=jnp.float32)
        p_lo = lax.dot_general(wrt_ref[0:n_rt, :], tn_lo, nt, preferred_element_type=jnp.float32)
        logits = p_hi[0:n_rt, :] + p_hi[n_rt:, :] + p_lo + brt_ref[...]
        el_all = logits[0:N_EXPERTS, :]
        gl = logits[N_EXPERTS:N_EXPERTS + SUBLANES, :]
        grow = lax.broadcasted_iota(jnp.int32, gl.shape, 0).astype(jnp.float32)
        gl = jnp.where(grow < N_GROUPS, gl, NEG_BIG)
        gmax = jnp.max(gl, axis=0, keepdims=True)
        grp = jnp.min(jnp.where(gl == gmax, grow, float(N_GROUPS)), axis=0, keepdims=True)
        p_grp = 1.0 / jnp.sum(jnp.exp(gl - gmax), axis=0, keepdims=True)
        erow = lax.broadcasted_iota(jnp.int32, el_all.shape, 0).astype(jnp.float32)
        egrp = jnp.floor(erow * (1.0 / EXPERTS_PER_GROUP))
        el = jnp.where(egrp == grp, el_all, NEG_BIG)
        e1 = jnp.max(el, axis=0, keepdims=True)
        i1 = jnp.min(jnp.where(el == e1, erow, float(N_EXPERTS)), axis=0, keepdims=True)
        el2 = jnp.where(erow == i1, NEG_BIG, el)
        e2 = jnp.max(el2, axis=0, keepdims=True)
        i2 = jnp.min(jnp.where(el2 == e2, erow, float(N_EXPERTS)), axis=0, keepdims=True)
        z2 = jnp.exp(e2 - e1)
        w1 = p_grp / (1.0 + z2)
        w2 = p_grp * z2 / (1.0 + z2)
        sel1 = erow == i1
        sel2 = erow == i2
        onehot = jnp.logical_or(sel1, sel2)
        t_r = lax.broadcasted_iota(jnp.int32, (tr, tr), 0)
        t_c = lax.broadcasted_iota(jnp.int32, (tr, tr), 1)
        before = (t_r < t_c).astype(jnp.bfloat16)
        rank = jnp.dot(onehot.astype(jnp.bfloat16), before, preferred_element_type=jnp.float32)
        cnt = jnp.sum(onehot.astype(jnp.float32), axis=1, keepdims=True)
        cnt_al = jnp.floor((cnt + (SEG_ALIGN - 1)) * (1.0 / SEG_ALIGN)) * SEG_ALIGN
        e_r = lax.broadcasted_iota(jnp.int32, (N_EXPERTS, N_EXPERTS), 0)
        e_c = lax.broadcasted_iota(jnp.int32, (N_EXPERTS, N_EXPERTS), 1)
        lstart = jnp.dot((e_c < e_r).astype(jnp.float32), jnp.broadcast_to(cnt_al, (N_EXPERTS, LANES)),
                         precision=hi, preferred_element_type=jnp.float32)[:, 0:1]
        slot = lstart + rank
        pos1 = jnp.sum(jnp.where(sel1, slot, 0.0), axis=0, keepdims=True)
        pos2 = jnp.sum(jnp.where(sel2, slot, 0.0), axis=0, keepdims=True)
        r8 = lax.broadcasted_iota(jnp.int32, (SUBLANES, tr), 0)
        info = jnp.where(r8 == ROW_POS0, pos1, jnp.where(r8 == ROW_POS1, pos2, jnp.where(
            r8 == ROW_W0, w1, jnp.where(r8 == ROW_W1, w2, jnp.where(
                r8 == ROW_E0, i1, jnp.where(r8 == ROW_E1, i2, 0.0))))))
        row_ref[:, rs] = info

        def split3(w):
            h = _bf16(w).astype(jnp.float32)
            m = _bf16(w - h).astype(jnp.float32)
            return h, m, _bf16(w - h - m).astype(jnp.float32)

        w1h, w1m, w1l = split3(w1)
        w2h, w2m, w2l = split3(w2)
        parts = jnp.where(r8 == 0, w1h, jnp.where(r8 == 1, w1m, jnp.where(r8 == 2, w1l, jnp.where(
            r8 == 3, w2h, jnp.where(r8 == 4, w2m, jnp.where(r8 == 5, w2l, jnp.where(r8 == 6, i1, i2)))))))
        col_ref[rs, :] = jnp.concatenate(
            [info, parts, jnp.zeros((LANES - 2 * SUBLANES, tr), jnp.float32)], axis=0).T
        cnt_ref[sb] = jnp.broadcast_to(cnt, (N_EXPERTS, LANES)).astype(jnp.int32)


def _out_proj_router(x2, att, hm, wo, layer, g, wrt, brt):
    t = x2.shape[0]
    tm = TM_OUT
    row = lambda i: (i, 0)
    fix = lambda i: (0, 0)
    return pl.pallas_call(
        _out_proj_router_kernel,
        grid=(t // tm,),
        in_specs=[
            pl.BlockSpec((tm, D_MODEL), row),
            pl.BlockSpec((tm, ATT_WIDTH), row),
            pl.BlockSpec((tm, ML_WIDTH), row),
            pl.BlockSpec((None, D_MODEL, D_MODEL), lambda i: (layer, 0, 0), pipeline_mode=pl.Buffered(1)),
            pl.BlockSpec((1, D_MODEL), fix),
            pl.BlockSpec((8 * SUBLANES, D_MODEL), fix),
            pl.BlockSpec((4 * SUBLANES, 1), fix),
        ],
        out_specs=[
            pl.BlockSpec((tm, D_MODEL), row),
            pl.BlockSpec((tm, D_MODEL), row),
            pl.BlockSpec((SUBLANES, tm), lambda i: (0, i)),
            pl.BlockSpec((tm, LANES), row),
            pl.BlockSpec((tm // TM_PROJ, N_EXPERTS, LANES), lambda i: (i, 0, 0)),
        ],
        out_shape=(
            jax.ShapeDtypeStruct((t, D_MODEL), jnp.float32),
            jax.ShapeDtypeStruct((t, D_MODEL), jnp.bfloat16),
            jax.ShapeDtypeStruct((SUBLANES, t), jnp.float32),
            jax.ShapeDtypeStruct((t, LANES), jnp.float32),
            jax.ShapeDtypeStruct((t // TM_PROJ, N_EXPERTS, LANES), jnp.int32),
        ),
        scratch_shapes=[pltpu.VMEM((D_MODEL, D_MODEL), jnp.bfloat16)],
        compiler_params=pltpu.CompilerParams(dimension_semantics=("arbitrary",),
                                             vmem_limit_bytes=VMEM_LIMIT),
        name="out_proj_router",
    )(x2, att, hm, wo, g, wrt, brt)


def _chunk_rows(j):
    return pl.ds(pl.multiple_of(j * SEG_ALIGN, SEG_ALIGN), SEG_ALIGN)


def _dispatch_kernel(nch_ref, dst_ref, nz_ref, zdst_ref, t_ref, col_ref, row_ref, xs_hbm,
                     xbuf, zx, sem_x, sem_z):
    i = pl.program_id(0)
    n = pl.num_programs(0)
    p = i % 2
    tm = t_ref.shape[0]

    def copy(tile, par, j):
        d = pl.ds(pl.multiple_of(dst_ref[tile, j], SEG_ALIGN), SEG_ALIGN)
        return pltpu.make_async_copy(xbuf.at[par, _chunk_rows(j)], xs_hbm.at[d], sem_x.at[par])

    def zero_copy(e, j):
        d = pl.ds(pl.multiple_of(zdst_ref[e] + j * SEG_ALIGN, SEG_ALIGN), SEG_ALIGN)
        return pltpu.make_async_copy(zx, xs_hbm.at[d], sem_z.at[0])

    def for_zero_chunks(fn):
        for e in range(N_EXPERTS):
            def body(j, c, e=e):
                fn(zero_copy(e, j))
                return c
            lax.fori_loop(0, nz_ref[e], body, 0)

    def wait_tile(tile, par):
        rows = pl.ds(0, pl.multiple_of(nch_ref[tile] * SEG_ALIGN, SEG_ALIGN))
        pltpu.make_async_copy(xbuf.at[par, rows], xs_hbm.at[rows], sem_x.at[par]).wait()

    @pl.when(i == 0)
    def _():
        zx[...] = jnp.zeros(zx.shape, zx.dtype)
        for_zero_chunks(lambda cp: cp.start())

    @pl.when(i >= 2)
    def _():
        wait_tile(i - 2, p)

    pos0 = row_ref[ROW_POS0:ROW_POS0 + 1, :].astype(jnp.int32)
    pos1 = row_ref[ROW_POS1:ROW_POS1 + 1, :].astype(jnp.int32)
    t = t_ref[...]
    side = _bf16(col_ref[...])
    rows = DISPATCH_ROWS
    for c in range(L_CAP // rows):
        r = lax.broadcasted_iota(jnp.int32, (rows, tm), 0) + c * rows
        perm = jnp.logical_or(r == pos0, r == pos1).astype(jnp.bfloat16)
        xbuf[p, c * rows:(c + 1) * rows, 0:D_MODEL] = _bf16(
            jnp.dot(perm, t, preferred_element_type=jnp.float32))
        xbuf[p, c * rows:(c + 1) * rows, D_MODEL:D_XS] = _bf16(
            jnp.dot(perm, side, preferred_element_type=jnp.float32))

    def start_body(j, c):
        copy(i, p, j).start()
        return c
    lax.fori_loop(0, nch_ref[i], start_body, 0)

    @pl.when(i == n - 1)
    def _():
        @pl.when(n >= 2)
        def _():
            wait_tile(i - 1, 1 - p)
        wait_tile(i, p)
        for_zero_chunks(lambda cp: cp.wait())


def _dispatch(t, col, row, nch, dst, nz, zdst):
    n_tok = t.shape[0]
    r_cap, _ = _moe_capacity(n_tok)
    tm = TM_PROJ
    grid_spec = pltpu.PrefetchScalarGridSpec(
        num_scalar_prefetch=4,
        grid=(n_tok // tm,),
        in_specs=[
            pl.BlockSpec((tm, D_MODEL), lambda i, *_: (i, 0)),
            pl.BlockSpec((tm, LANES), lambda i, *_: (i, 0)),
            pl.BlockSpec((SUBLANES, tm), lambda i, *_: (0, i)),
        ],
        out_specs=pl.BlockSpec(memory_space=pl.ANY),
        scratch_shapes=[
            pltpu.VMEM((2, L_CAP, D_XS), jnp.bfloat16),
            pltpu.VMEM((SEG_ALIGN, D_XS), jnp.bfloat16),
            pltpu.SemaphoreType.DMA((2,)),
            pltpu.SemaphoreType.DMA((1,)),
        ],
    )
    return pl.pallas_call(
        _dispatch_kernel,
        grid_spec=grid_spec,
        out_shape=jax.ShapeDtypeStruct((r_cap, D_XS), jnp.bfloat16),
        compiler_params=pltpu.CompilerParams(dimension_semantics=("arbitrary",),
                                             vmem_limit_bytes=VMEM_LIMIT),
        name="moe_dispatch",
    )(nch, dst, nz, zdst, t, col, row)


def _moe_kernel(blk_e_ref, nused_ref, xs_ref, wg_ref, wu_ref, wd_ref, ys_ref, wg_bf, wu_bf, wd_bf):
    b = pl.program_id(0)
    new_expert = jnp.logical_or(b == 0, blk_e_ref[b] != blk_e_ref[jnp.maximum(b - 1, 0)])

    @pl.when(new_expert)
    def _():
        wg_bf[...] = _bf16(wg_ref[...])
        wu_bf[...] = _bf16(wu_ref[...])
        wd_bf[...] = _bf16(wd_ref[...])

    @pl.when(b < nused_ref[0])
    def _():
        x = xs_ref[:, 0:D_MODEL]
        a = jnp.dot(x, wg_bf[...], preferred_element_type=jnp.float32)
        u = jnp.dot(x, wu_bf[...], preferred_element_type=jnp.float32)
        h = a * jax.nn.sigmoid(a) * u
        y = jnp.dot(_bf16(h), wd_bf[...], preferred_element_type=jnp.float32)
        sd = xs_ref[:, D_MODEL:D_XS].astype(jnp.float32)
        e_blk = blk_e_ref[b].astype(jnp.float32)
        w0 = sd[:, COL_W0H:COL_W0H + 1] + sd[:, COL_W0H + 1:COL_W0H + 2] + sd[:, COL_W0H + 2:COL_W0H + 3]
        w1 = sd[:, COL_W1H:COL_W1H + 1] + sd[:, COL_W1H + 1:COL_W1H + 2] + sd[:, COL_W1H + 2:COL_W1H + 3]
        w = jnp.where(sd[:, COL_E0:COL_E0 + 1] == e_blk, w0, w1)
        ys_ref[...] = _bf16(y * w)


def _moe(xs, blk_e, nused, wg, wu, wd, layer):
    blk = lambda b, be, nu: (jnp.maximum(jnp.minimum(b, nu[0] - 1), 0), 0)
    wsel = lambda b, be, nu: (layer, be[b], 0, 0)
    r_cap = xs.shape[0]
    grid_spec = pltpu.PrefetchScalarGridSpec(
        num_scalar_prefetch=2,
        grid=(r_cap // MOE_BM,),
        in_specs=[
            pl.BlockSpec((MOE_BM, D_XS), blk),
            pl.BlockSpec((None, None, D_MODEL, D_FF_EXPERT), wsel),
            pl.BlockSpec((None, None, D_MODEL, D_FF_EXPERT), wsel),
            pl.BlockSpec((None, None, D_FF_EXPERT, D_MODEL), wsel),
        ],
        out_specs=pl.BlockSpec((MOE_BM, D_MODEL), blk),
        scratch_shapes=[
            pltpu.VMEM((D_MODEL, D_FF_EXPERT), jnp.bfloat16),
            pltpu.VMEM((D_MODEL, D_FF_EXPERT), jnp.bfloat16),
            pltpu.VMEM((D_FF_EXPERT, D_MODEL), jnp.bfloat16),
        ],
    )
    return pl.pallas_call(
        _moe_kernel,
        grid_spec=grid_spec,
        out_shape=jax.ShapeDtypeStruct((r_cap, D_MODEL), jnp.bfloat16),
        compiler_params=pltpu.CompilerParams(dimension_semantics=("arbitrary",),
                                             vmem_limit_bytes=VMEM_LIMIT),
        name="moe_experts",
    )(blk_e, nused, xs, wg, wu, wd)


def _combine_kernel(nch_ref, dst_ref, x1_ref, col_ref, ys_hbm, out_ref, ybuf, sem):
    i = pl.program_id(0)
    n = pl.num_programs(0)
    p = i % 2
    tm = x1_ref.shape[0]

    def copy(tile, par, j):
        s = pl.ds(pl.multiple_of(dst_ref[tile, j], SEG_ALIGN), SEG_ALIGN)
        return pltpu.make_async_copy(ys_hbm.at[s], ybuf.at[par, _chunk_rows(j)], sem.at[par])

    def start_tile(tile, par):
        def body(j, c):
            copy(tile, par, j).start()
            return c
        lax.fori_loop(0, nch_ref[tile], body, 0)

    @pl.when(i == 0)
    def _():
        ybuf[...] = jnp.zeros(ybuf.shape, ybuf.dtype)
        start_tile(0, 0)

    @pl.when(i + 1 < n)
    def _():
        start_tile(i + 1, 1 - p)

    rows = pl.ds(0, pl.multiple_of(nch_ref[i] * SEG_ALIGN, SEG_ALIGN))
    pltpu.make_async_copy(ys_hbm.at[rows], ybuf.at[p, rows], sem.at[p]).wait()

    rows = COMBINE_ROWS
    for c in range(tm // rows):
        col = col_ref[c * rows:(c + 1) * rows, :]
        pos0 = col[:, ROW_POS0:ROW_POS0 + 1].astype(jnp.int32)
        pos1 = col[:, ROW_POS1:ROW_POS1 + 1].astype(jnp.int32)
        l = lax.broadcasted_iota(jnp.int32, (rows, L_CAP), 1)
        perm = jnp.logical_or(l == pos0, l == pos1).astype(jnp.bfloat16)
        y = jnp.dot(perm, ybuf[p], preferred_element_type=jnp.float32)
        out_ref[c * rows:(c + 1) * rows, :] = x1_ref[c * rows:(c + 1) * rows, :] + y


def _combine(x1, col, ys, nch, dst):
    n_tok = x1.shape[0]
    tm = TM_PROJ
    grid_spec = pltpu.PrefetchScalarGridSpec(
        num_scalar_prefetch=2,
        grid=(n_tok // tm,),
        in_specs=[
            pl.BlockSpec((tm, D_MODEL), lambda i, *_: (i, 0)),
            pl.BlockSpec((tm, LANES), lambda i, *_: (i, 0)),
            pl.BlockSpec(memory_space=pl.ANY),
        ],
        out_specs=pl.BlockSpec((tm, D_MODEL), lambda i, *_: (i, 0)),
        scratch_shapes=[
            pltpu.VMEM((2, L_CAP, D_MODEL), jnp.bfloat16),
            pltpu.SemaphoreType.DMA((2,)),
        ],
    )
    return pl.pallas_call(
        _combine_kernel,
        grid_spec=grid_spec,
        out_shape=jax.ShapeDtypeStruct((n_tok, D_MODEL), jnp.float32),
        compiler_params=pltpu.CompilerParams(dimension_semantics=("arbitrary",),
                                             vmem_limit_bytes=VMEM_LIMIT),
        name="moe_combine",
    )(nch, dst, x1, col, ys)


def _routing_tables(counts, n_blk_cap):
    i32 = jnp.int32
    ca = ((counts + (SEG_ALIGN - 1)) // SEG_ALIGN) * SEG_ALIGN
    lend = jnp.cumsum(ca, axis=1)
    lstart = lend - ca
    nch = (lend[:, -1] // SEG_ALIGN).astype(i32)
    tot = jnp.sum(ca, axis=0)
    region = ((tot + (MOE_BM - 1)) // MOE_BM) * MOE_BM
    rend = jnp.cumsum(region)
    base = rend - region
    gstart = base[None, :] + jnp.cumsum(ca, axis=0) - ca
    j16 = jnp.arange(N_CHUNK, dtype=i32) * SEG_ALIGN
    in_seg = jnp.logical_and(lstart[:, None, :] <= j16[None, :, None], j16[None, :, None] < lend[:, None, :])
    dst = (jnp.sum(jnp.where(in_seg, (gstart - lstart)[:, None, :], 0), axis=2) + j16[None, :]).astype(i32)
    nused = (rend[-1] // MOE_BM).astype(i32)
    brow = jnp.minimum(jnp.arange(n_blk_cap, dtype=i32), nused - 1) * MOE_BM
    blk_e = jnp.minimum(jnp.sum(rend[None, :] <= brow[:, None], axis=1), N_EXPERTS - 1).astype(i32)
    zstart = (base + tot).astype(i32)
    nzc = ((region - tot) // SEG_ALIGN).astype(i32)
    return nch, dst, nzc, zstart, blk_e, nused[None]


def _t5_bucket_np(dist):
    max_exact = N_BUCKETS // 2
    d = np.maximum(dist, 1).astype(np.float32)
    large = max_exact + (np.log(d / max_exact) / math.log(MAX_DISTANCE / max_exact)
                         * (N_BUCKETS - max_exact)).astype(np.int32)
    large = np.minimum(large, N_BUCKETS - 1)
    return np.where(dist < max_exact, dist, large)


def _attention_bias(rel_bias):
    qi = np.arange(ATT_BLOCK)[:, None]
    kj = np.arange(2 * ATT_BLOCK)[None, :]
    dist = qi + ATT_BLOCK - kj
    in_window = (dist >= 0) & (dist < WINDOW)
    bucket = _t5_bucket_np(np.clip(dist, 0, WINDOW - 1))
    onehot = (bucket[None] == np.arange(N_BUCKETS)[:, None, None]).astype(np.float32)
    bias = jnp.einsum('nh,nqk->hqk', rel_bias.astype(jnp.float32), jnp.asarray(onehot),
                      precision=lax.Precision.HIGHEST)
    bias = jnp.where(jnp.asarray(in_window)[None], bias, NEG_BIG)
    no_prev = jnp.asarray(kj < ATT_BLOCK)[None]
    return jnp.stack([bias, jnp.where(no_prev, NEG_BIG, bias)])


def _block_diag_mean(width, block):
    idx = np.arange(width) // block
    return jnp.asarray((idx[:, None] == idx[None, :]).astype(np.float32) / block, dtype=jnp.bfloat16)


def kernel(x, rel_bias, norm_mix_g, w_in, q_norm_g, k_norm_g, attn_sink, conv_w, conv_b, gate_b,
           mlstm_norm_g, w_out, norm_ffn_g, w_router_group, b_router_group, w_router_expert,
           b_router_expert, w_gate, w_up, w_down):
    batch, seq_len, _ = x.shape
    n_tok = batch * seq_len
    assert seq_len % TM_PROJ == 0 and seq_len % TM_IN == 0 and seq_len % ML_STEP == 0
    f32 = jnp.float32
    bias = _attention_bias(rel_bias)
    bdq = _block_diag_mean(ATT_WIDTH, ATT_HEAD_DIM)
    bdk = _block_diag_mean(LANES, ATT_HEAD_DIM)
    x2 = x.reshape(n_tok, D_MODEL)
    for l in range(DEPTH):
        wgt = _bf16(w_in[l, :, O_G:N_IN].T)
        qg = (jnp.tile(q_norm_g[l].astype(f32), ATT_HEADS) * (ATT_HEAD_DIM ** -0.5))[None, :]
        kg = jnp.tile(k_norm_g[l].astype(f32), ATT_KV_HEADS)[None, :]
        gbt = gate_b[l].astype(f32)[:, None]
        qn, kn, vd, qkm, vm, om, gates_t = _in_proj(
            x2, norm_mix_g[l][None, :], w_in, l, wgt, qg, kg, bdq, bdk,
            conv_w[l], conv_b[l][None, :], gbt, batch, seq_len)
        att = _attention(qn, kn, vd, bias, attn_sink[l].astype(f32), batch, seq_len)
        r3 = lambda a: a.reshape(batch, seq_len, a.shape[-1])
        hm = _mlstm(r3(qkm), r3(vm), r3(om), gates_t, mlstm_norm_g[l][None, :], batch, seq_len)
        n_rt = 4 * SUBLANES
        wrt = jnp.pad(jnp.concatenate([w_router_expert[l], w_router_group[l]], axis=1).astype(f32).T,
                      ((0, n_rt - N_EXPERTS - N_GROUPS), (0, 0)))
        brt = jnp.pad(jnp.concatenate([b_router_expert[l], b_router_group[l]]).astype(f32),
                      (0, n_rt - N_EXPERTS - N_GROUPS))[:, None]
        wrt_hi = _bf16(wrt)
        wrt = jnp.concatenate([wrt_hi, _bf16(wrt - wrt_hi.astype(f32))], axis=0)
        x1, t, row, col, cnt = _out_proj_router(x2, att, hm.reshape(n_tok, ML_WIDTH), w_out, l,
                                                norm_ffn_g[l][None, :], wrt, brt)
        _, n_blk_cap = _moe_capacity(n_tok)
        nch, dst, nz, zdst, blk_e, nused = _routing_tables(cnt[:, :, 0], n_blk_cap)
        xs = _dispatch(t, col, row, nch, dst, nz, zdst)
        ys = _moe(xs, blk_e, nused, w_gate, w_up, w_down, l)
        x2 = _combine(x1, col, ys, nch, dst)
    return x2.reshape(batch, seq_len, D_MODEL)
```

```python
import functools
import math

import jax
import jax.numpy as jnp
import numpy as np
from jax import lax
from jax.experimental import pallas as pl
from jax.experimental.pallas import tpu as pltpu

D_MODEL = 1024
DEPTH = 2
ATT_HEADS = 8
ATT_KV_HEADS = 2
ATT_HEAD_DIM = 64
ATT_WIDTH = ATT_HEADS * ATT_HEAD_DIM
WINDOW = 128
ATT_BLOCK = 128
N_BUCKETS = 32
MAX_DISTANCE = 128
ML_HEADS = 4
ML_DQK = 64
ML_DV = 128
ML_WIDTH = ML_HEADS * ML_DV
ML_CHUNK = 64
CONV_K = 4
N_GROUPS = 4
EXPERTS_PER_GROUP = 4
N_EXPERTS = N_GROUPS * EXPERTS_PER_GROUP
D_FF_EXPERT = 512
EPS = 1e-6

LANES = 128
SUBLANES = 8
NEG_BIG = -1e30
LOG2E = math.log2(math.e)
VMEM_LIMIT = 48 * 1024 * 1024

O_Q = 0
O_K = O_Q + ATT_WIDTH
O_V = O_K + ATT_KV_HEADS * ATT_HEAD_DIM
O_QM = O_V + ATT_KV_HEADS * ATT_HEAD_DIM
O_KM = O_QM + ML_HEADS * ML_DQK
O_VM = O_KM + ML_HEADS * ML_DQK
O_OM = O_VM + ML_WIDTH
O_G = O_OM + ML_WIDTH
N_IN = O_G + 2 * ML_HEADS

ATT_STEP_BLOCKS = 4
TM_IN = 1024
IN_SUBBLOCKS = 2
TM_OUT = 1024
TM_PROJ = 512
ML_KCHUNK = 256
ML_STEP = ML_KCHUNK

SEG_ALIGN = 16
MOE_BM = 512
MOE_FF_SPLIT = 2
L_CAP = 2 * TM_PROJ + N_EXPERTS * SEG_ALIGN
N_CHUNK = L_CAP // SEG_ALIGN
D_XS = D_MODEL + LANES
DISPATCH_ROWS = 256
COMBINE_ROWS = 256
ROW_POS0, ROW_POS1, ROW_W0, ROW_W1, ROW_E0, ROW_E1 = 0, 1, 2, 3, 4, 5
COL_W0H, COL_W1H, COL_E0, COL_E1 = 8, 11, 14, 15


def _moe_capacity(n_tok):
    n_tiles = n_tok // TM_PROJ
    rows = 2 * n_tok + n_tiles * N_EXPERTS * (SEG_ALIGN - 1) + N_EXPERTS * (MOE_BM - SEG_ALIGN)
    n_blk = -(-rows // MOE_BM)
    return n_blk * MOE_BM, n_blk


def _bf16(a):
    return a.astype(jnp.bfloat16)


def _split3(a):
    hi = _bf16(a)
    r1 = a - hi.astype(jnp.float32)
    mid = _bf16(r1)
    return hi, mid, _bf16(r1 - mid.astype(jnp.float32))


def _log_sigmoid(z):
    return jnp.minimum(z, 0.0) - jnp.log(1.0 + jnp.exp(-jnp.abs(z)))


def _in_proj_kernel(tiles_per_seq, x_ref, g_ref, w_ref, wgt_ref, qg_ref, kg_ref, bdq_ref, bdk_ref,
                    cw_ref, cb_ref, gbt_ref,
                    qn_ref, kn_ref, vd_ref, qkm_ref, vm_ref, om_ref, gatet_ref,
                    w_bf, conv_scr):
    i = pl.program_id(0)
    tm = x_ref.shape[0]

    @pl.when(i == 0)
    def _():
        w_bf[...] = _bf16(w_ref[...])

    conv_w = 2 * ML_HEADS * ML_DQK

    @pl.when(i % tiles_per_seq == 0)
    def _():
        conv_scr[0:SUBLANES, :] = jnp.zeros((SUBLANES, conv_w), jnp.float32)

    @pl.when(i % tiles_per_seq != 0)
    def _():
        conv_scr[0:SUBLANES, :] = conv_scr[tm:tm + SUBLANES, :]

    sub = tm // IN_SUBBLOCKS
    for sb in range(IN_SUBBLOCKS):
        rs = slice(sb * sub, (sb + 1) * sub)
        x = x_ref[rs, :]
        hn = x * lax.rsqrt(jnp.mean(x * x, axis=-1, keepdims=True) + EPS) * g_ref[...]
        hb = _bf16(hn)

        def proj(c0, width):
            return jnp.dot(hb, w_bf[:, c0:c0 + width], preferred_element_type=jnp.float32)

        q = proj(O_Q, ATT_WIDTH)
        q_ms = jnp.dot(_bf16(q * q), bdq_ref[...], preferred_element_type=jnp.float32)
        qn_ref[rs, :] = _bf16(q * lax.rsqrt(q_ms + EPS) * qg_ref[...])
        kv = proj(O_K, 2 * LANES)
        k = kv[:, 0:LANES]
        v = kv[:, LANES:2 * LANES]
        k_ms = jnp.dot(_bf16(k * k), bdk_ref[...], preferred_element_type=jnp.float32)
        kn = k * lax.rsqrt(k_ms + EPS) * kg_ref[...]
        low = lax.broadcasted_iota(jnp.int32, kn.shape, 1) < ATT_HEAD_DIM

        def dup_heads(a):
            swapped = pltpu.roll(a, ATT_HEAD_DIM, axis=1)
            return jnp.concatenate([jnp.where(low, a, swapped), jnp.where(low, swapped, a)], axis=1)

        kn_ref[rs, :] = _bf16(dup_heads(kn))
        vd_ref[rs, :] = _bf16(dup_heads(v))

        qk = proj(O_QM, conv_w)
        base = SUBLANES + sb * sub
        conv_scr[base:base + sub, :] = qk
        y = qk * cw_ref[CONV_K - 1:CONV_K, :] + cb_ref[...]
        for j in range(CONV_K - 1):
            off = base - (CONV_K - 1) + j
            y = y + conv_scr[off:off + sub, :] * cw_ref[j:j + 1, :]
        y = y * jax.nn.sigmoid(y)
        lane = lax.broadcasted_iota(jnp.int32, y.shape, 1)
        y = jnp.where(lane >= ML_HEADS * ML_DQK, y * (ML_DQK ** -0.5), y)
        qkm_ref[rs, :] = _bf16(y)

        vm_ref[rs, :] = _bf16(proj(O_VM, ML_WIDTH))
        om_ref[rs, :] = _bf16(jax.nn.sigmoid(proj(O_OM, ML_WIDTH)))

        gt = lax.dot_general(wgt_ref[...], hb, (((1,), (1,)), ((), ())),
                             preferred_element_type=jnp.float32) + gbt_ref[...]
        grow = lax.broadcasted_iota(jnp.int32, gt.shape, 0)
        gatet_ref[:, rs] = jnp.where(grow >= ML_HEADS, _log_sigmoid(gt), gt)


def _in_proj(x2, g, w_in, layer, wgt, qg, kg, bdq, bdk, cw, cb, gbt, batch, seq_len):
    t = x2.shape[0]
    tm = TM_IN
    n = t // tm
    tps = seq_len // tm
    row = lambda i: (i, 0)
    fix = lambda i: (0, 0)
    kv_w = 2 * ATT_KV_HEADS * ATT_HEAD_DIM
    out_shapes = (
        jax.ShapeDtypeStruct((t, ATT_WIDTH), jnp.bfloat16),
        jax.ShapeDtypeStruct((t, kv_w), jnp.bfloat16),
        jax.ShapeDtypeStruct((t, kv_w), jnp.bfloat16),
        jax.ShapeDtypeStruct((t, 2 * ML_HEADS * ML_DQK), jnp.bfloat16),
        jax.ShapeDtypeStruct((t, ML_WIDTH), jnp.bfloat16),
        jax.ShapeDtypeStruct((t, ML_WIDTH), jnp.bfloat16),
        jax.ShapeDtypeStruct((batch, SUBLANES, seq_len), jnp.float32),
    )
    in_specs = [
        pl.BlockSpec((tm, D_MODEL), row),
        pl.BlockSpec((1, D_MODEL), fix),
        pl.BlockSpec((None, D_MODEL, N_IN), lambda i: (layer, 0, 0), pipeline_mode=pl.Buffered(1)),
        pl.BlockSpec((SUBLANES, D_MODEL), fix),
        pl.BlockSpec((1, ATT_WIDTH), fix),
        pl.BlockSpec((1, LANES), fix),
        pl.BlockSpec((ATT_WIDTH, ATT_WIDTH), fix),
        pl.BlockSpec((LANES, LANES), fix),
        pl.BlockSpec((CONV_K, 2 * ML_HEADS * ML_DQK), fix),
        pl.BlockSpec((1, 2 * ML_HEADS * ML_DQK), fix),
        pl.BlockSpec((SUBLANES, 1), fix),
    ]
    out_specs = [
        pl.BlockSpec((tm, ATT_WIDTH), row),
        pl.BlockSpec((tm, kv_w), row),
        pl.BlockSpec((tm, kv_w), row),
        pl.BlockSpec((tm, 2 * ML_HEADS * ML_DQK), row),
        pl.BlockSpec((tm, ML_WIDTH), row),
        pl.BlockSpec((tm, ML_WIDTH), row),
        pl.BlockSpec((None, SUBLANES, tm), lambda i: (i // tps, 0, i % tps)),
    ]
    return pl.pallas_call(
        functools.partial(_in_proj_kernel, seq_len // tm),
        grid=(n,),
        in_specs=in_specs,
        out_specs=out_specs,
        out_shape=out_shapes,
        scratch_shapes=[pltpu.VMEM((D_MODEL, N_IN), jnp.bfloat16),
                        pltpu.VMEM((tm + 2 * SUBLANES, 2 * ML_HEADS * ML_DQK), jnp.float32)],
        compiler_params=pltpu.CompilerParams(dimension_semantics=("arbitrary",),
                                             vmem_limit_bytes=VMEM_LIMIT),
        name="in_proj",
    )(x2, g, w_in, wgt, qg, kg, bdq, bdk, cw, cb, gbt)


def _attn_kernel(sink_ref, q_ref, kp_ref, kc_ref, vp_ref, vc_ref, bias_ref, o_ref):
    i = pl.program_id(1)
    blk = ATT_BLOCK
    lane = lax.broadcasted_iota(jnp.int32, (blk, LANES), 1)
    low = lane < ATT_HEAD_DIM
    group = ATT_HEADS // ATT_KV_HEADS
    ones = jnp.ones((2 * blk, LANES), jnp.bfloat16)
    for j in range(ATT_STEP_BLOCKS):
        rows = slice(j * blk, (j + 1) * blk)
        first = (i == 0).astype(jnp.int32) if j == 0 else 0
        for kv in range(ATT_KV_HEADS):
            lanes = slice(kv * LANES, (kv + 1) * LANES)
            if j == 0:
                kcat = jnp.concatenate([kp_ref[:, lanes], kc_ref[0:blk, lanes]], axis=0)
                vrows = jnp.concatenate([vp_ref[:, lanes], vc_ref[0:blk, lanes]], axis=0)
            else:
                kcat = kc_ref[(j - 1) * blk:(j + 1) * blk, lanes]
                vrows = vc_ref[(j - 1) * blk:(j + 1) * blk, lanes]
            vcat = jnp.concatenate([vrows, ones], axis=1)
            heads = range(kv * group, (kv + 1) * group)
            qs = []
            for h in heads:
                qp = q_ref[rows, (h // 2) * LANES:(h // 2 + 1) * LANES]
                qs.append(jnp.where(low if h % 2 == 0 else jnp.logical_not(low), qp, jnp.zeros_like(qp)))
            s_all = lax.dot_general(jnp.concatenate(qs, axis=0), kcat, (((1,), (1,)), ((), ())),
                                    preferred_element_type=jnp.float32)
            ps, ms = [], []
            for n, h in enumerate(heads):
                logits = s_all[n * blk:(n + 1) * blk, :] + bias_ref[first, h]
                row_max = jnp.broadcast_to(jnp.max(logits, axis=-1, keepdims=True), (blk, LANES))
                m = jnp.maximum(row_max, sink_ref[h])
                ps.append(_bf16(jnp.exp2(logits - jnp.concatenate([m, m], axis=1))))
                ms.append(m)
            o_all = jnp.dot(jnp.concatenate(ps, axis=0), vcat, preferred_element_type=jnp.float32)
            outs = []
            for n, h in enumerate(heads):
                o = o_all[n * blk:(n + 1) * blk, :]
                den = o[:, LANES:2 * LANES] + jnp.exp2(sink_ref[h] - ms[n])
                outs.append(o[:, 0:LANES] / den)
            for n in range(0, group, 2):
                pair = (kv * group + n) // 2
                o_ref[rows, pair * LANES:(pair + 1) * LANES] = _bf16(
                    jnp.where(low, outs[n], outs[n + 1]))


def _attention(qn, kn, vd, bias, sink, batch, seq_len):
    step = ATT_STEP_BLOCKS * ATT_BLOCK
    ns = seq_len // step
    cur = lambda b, i, s: (b * ns + i, 0)
    prev = lambda b, i, s: ((b * ns + i) * ATT_STEP_BLOCKS - jnp.minimum(i, 1), 0)
    grid_spec = pltpu.PrefetchScalarGridSpec(
        num_scalar_prefetch=1,
        grid=(batch, ns),
        in_specs=[
            pl.BlockSpec((step, ATT_WIDTH), cur),
            pl.BlockSpec((ATT_BLOCK, 2 * LANES), prev),
            pl.BlockSpec((step, 2 * LANES), cur),
            pl.BlockSpec((ATT_BLOCK, 2 * LANES), prev),
            pl.BlockSpec((step, 2 * LANES), cur),
            pl.BlockSpec((2, ATT_HEADS, ATT_BLOCK, 2 * ATT_BLOCK), lambda b, i, s: (0, 0, 0, 0)),
        ],
        out_specs=pl.BlockSpec((step, ATT_WIDTH), cur),
    )
    return pl.pallas_call(
        _attn_kernel,
        grid_spec=grid_spec,
        out_shape=jax.ShapeDtypeStruct((batch * seq_len, ATT_WIDTH), jnp.bfloat16),
        compiler_params=pltpu.CompilerParams(dimension_semantics=("arbitrary", "arbitrary"),
                                             vmem_limit_bytes=VMEM_LIMIT),
        name="swa_attention",
    )(sink, qn, kn, kn, vd, vd, bias)


def _mlstm_kernel(qk_ref, v_ref, o_ref, gt_ref, ng_ref, out_ref, c_scr, m_scr):
    step = pl.program_id(0)
    batch = qk_ref.shape[0]
    L = ML_KCHUNK
    pairs = ML_HEADS // 2

    @pl.when(step == 0)
    def _():
        c_scr[...] = jnp.zeros(c_scr.shape, jnp.float32)
        m_scr[...] = jnp.zeros(m_scr.shape, jnp.float32)

    r_i = lax.broadcasted_iota(jnp.int32, (L, L), 0)
    c_i = lax.broadcasted_iota(jnp.int32, (L, L), 1)
    causal = c_i <= r_i
    tril_bf = causal.astype(jnp.bfloat16)
    triu_bf = (r_i <= c_i).astype(jnp.bfloat16)
    mean_dv = jnp.full((ML_DV, ML_DV), 1.0 / ML_DV, jnp.bfloat16)
    lane = lax.broadcasted_iota(jnp.int32, (L, LANES), 1)
    low = lane < ML_DQK
    ones_dv = jnp.ones((L, ML_DV), jnp.bfloat16)
    row2 = lax.broadcasted_iota(jnp.int32, (2 * ML_DQK, 2 * ML_DV), 0)

    def twice(a):
        return jnp.concatenate([a, a], axis=1)

    m_state = [m_scr[k] for k in range(batch * ML_HEADS)]
    c_state = [c_scr[k] for k in range(batch * pairs)]

    for ch in range(ML_STEP // L):
        r0 = ch * L
        for b in range(batch):
            gt = gt_ref[b, :, r0:r0 + L]
            g = jnp.concatenate([gt, jnp.zeros((LANES - SUBLANES, L), jnp.float32)], axis=0).T
            bcols = sum(jnp.dot(tril_bf, part, preferred_element_type=jnp.float32) for part in _split3(g))
            brows = sum(jnp.dot(part, triu_bf, preferred_element_type=jnp.float32) for part in _split3(gt))
            for pair in range(pairs):
                sidx = b * pairs + pair
                qp = qk_ref[b, r0:r0 + L, pair * LANES:(pair + 1) * LANES]
                kp = qk_ref[b, r0:r0 + L, (pairs + pair) * LANES:(pairs + pair + 1) * LANES]
                c_pair = c_state[sidx]
                c_bf = _bf16(c_pair)
                new_c = None
                decays = []
                for sub in range(2):
                    h = 2 * pair + sub
                    sel = low if sub == 0 else jnp.logical_not(low)
                    m_prev = m_state[b * ML_HEADS + h]
                    bc = jnp.broadcast_to(bcols[:, ML_HEADS + h:ML_HEADS + h + 1], (L, LANES))
                    lic = jnp.broadcast_to(g[:, h:h + 1], (L, LANES))
                    br = brows[ML_HEADS + h:ML_HEADS + h + 1, :]
                    lir = gt[h:h + 1, :]
                    log_d = jnp.where(causal, twice(bc) - (br - lir), NEG_BIG)
                    m_inter = bc + m_prev
                    row_max = jnp.broadcast_to(jnp.max(log_d, axis=-1, keepdims=True), (L, LANES))
                    m_row = jnp.maximum(m_inter, row_max)
                    d = jnp.exp(log_d - twice(m_row))
                    inter = jnp.exp(m_inter - m_row)
                    qm = jnp.where(sel, qp, jnp.zeros_like(qp))
                    s = lax.dot_general(qm, kp, (((1,), (1,)), ((), ())),
                                        preferred_element_type=jnp.float32) * d
                    v_ext = jnp.concatenate([v_ref[b, r0:r0 + L, h * ML_DV:(h + 1) * ML_DV], ones_dv],
                                            axis=-1)
                    num = twice(inter) * jnp.dot(qm, c_bf, preferred_element_type=jnp.float32) \
                        + jnp.dot(_bf16(s), v_ext, preferred_element_type=jnp.float32)
                    den = num[:, ML_DV:2 * ML_DV]
                    hval = num[:, 0:ML_DV] / jnp.maximum(jnp.abs(den), jnp.exp(-m_row))
                    h_ms = jnp.dot(_bf16(hval * hval), mean_dv, preferred_element_type=jnp.float32)
                    hn = hval * lax.rsqrt(h_ms + EPS)
                    hn = hn * ng_ref[:, h * ML_DV:(h + 1) * ML_DV]
                    out_ref[b, r0:r0 + L, h * ML_DV:(h + 1) * ML_DV] = _bf16(
                        hn * o_ref[b, r0:r0 + L, h * ML_DV:(h + 1) * ML_DV].astype(jnp.float32))
                    b_last = bc[L - 1:L, :]
                    log_w = b_last - bc + lic
                    m_next = jnp.maximum(b_last + m_prev, jnp.max(log_w, axis=0, keepdims=True))
                    w = jnp.exp(log_w - m_next)
                    decays.append(jnp.exp(b_last + m_prev - m_next))
                    m_state[b * ML_HEADS + h] = m_next
                    kw = _bf16(jnp.where(sel, kp.astype(jnp.float32) * w, 0.0))
                    upd = lax.dot_general(kw, v_ext, (((0,), (0,)), ((), ())),
                                          preferred_element_type=jnp.float32)
                    new_c = upd if new_c is None else new_c + upd
                decay_rows = jnp.where(row2 < ML_DQK, twice(decays[0]), twice(decays[1]))
                c_state[sidx] = decay_rows * c_pair + new_c

    for k in range(batch * ML_HEADS):
        m_scr[k] = m_state[k]
    for k in range(batch * pairs):
        c_scr[k] = c_state[k]


def _mlstm(qkm, vm, om, gates_t, ng, batch, seq_len):
    n_steps = seq_len // ML_STEP
    blk = lambda c: (0, c, 0)
    return pl.pallas_call(
        _mlstm_kernel,
        grid=(n_steps,),
        in_specs=[
            pl.BlockSpec((batch, ML_STEP, 2 * ML_HEADS * ML_DQK), blk),
            pl.BlockSpec((batch, ML_STEP, ML_WIDTH), blk),
            pl.BlockSpec((batch, ML_STEP, ML_WIDTH), blk),
            pl.BlockSpec((batch, SUBLANES, ML_STEP), lambda c: (0, 0, c)),
            pl.BlockSpec((1, ML_WIDTH), lambda c: (0, 0)),
        ],
        out_specs=pl.BlockSpec((batch, ML_STEP, ML_WIDTH), blk),
        out_shape=jax.ShapeDtypeStruct((batch, seq_len, ML_WIDTH), jnp.bfloat16),
        scratch_shapes=[
            pltpu.VMEM((batch * ML_HEADS // 2, 2 * ML_DQK, 2 * ML_DV), jnp.float32),
            pltpu.VMEM((batch * ML_HEADS, 1, LANES), jnp.float32),
        ],
        compiler_params=pltpu.CompilerParams(dimension_semantics=("arbitrary",),
                                             vmem_limit_bytes=VMEM_LIMIT),
        name="mlstm_scan",
    )(qkm, vm, om, gates_t, ng)


def _out_proj_router_kernel(x_ref, att_ref, hm_ref, wo_ref, g_ref, wrt_ref, brt_ref,
                            x1_ref, t_ref, row_ref, col_ref, cnt_ref, wo_bf):
    tm = x_ref.shape[0]
    tr = TM_PROJ
    hi = lax.Precision.HIGHEST

    @pl.when(pl.program_id(0) == 0)
    def _():
        wo_bf[...] = _bf16(wo_ref[...])

    for sb in range(tm // TM_PROJ):
        rs = slice(sb * TM_PROJ, (sb + 1) * TM_PROJ)
        x1 = x_ref[rs, :] \
            + jnp.dot(att_ref[rs, :], wo_bf[0:ATT_WIDTH, :], preferred_element_type=jnp.float32) \
            + jnp.dot(hm_ref[rs, :], wo_bf[ATT_WIDTH:, :], preferred_element_type=jnp.float32)
        x1_ref[rs, :] = x1
        tn = x1 * lax.rsqrt(jnp.mean(x1 * x1, axis=-1, keepdims=True) + EPS) * g_ref[...]
        tn_hi = _bf16(tn)
        t_ref[rs, :] = tn_hi
        tn_lo = _bf16(tn - tn_hi.astype(jnp.float32))
        nt = (((1,), (1,)), ((), ()))
        n_rt = wrt_ref.shape[0] // 2
        p_hi = lax.dot_general(wrt_ref[...], tn_hi, nt, preferred_element_type=jnp.float32)
        p_lo = lax.dot_general(wrt_ref[0:n_rt, :], tn_lo, nt, preferred_element_type=jnp.float32)
        logits = p_hi[0:n_rt, :] + p_hi[n_rt:, :] + p_lo + brt_ref[...]
        el_all = logits[0:N_EXPERTS, :]
        gl = logits[N_EXPERTS:N_EXPERTS + SUBLANES, :]
        grow = lax.broadcasted_iota(jnp.int32, gl.shape, 0).astype(jnp.float32)
        gl = jnp.where(grow < N_GROUPS, gl, NEG_BIG)
        gmax = jnp.max(gl, axis=0, keepdims=True)
        grp = jnp.min(jnp.where(gl == gmax, grow, float(N_GROUPS)), axis=0, keepdims=True)
        p_grp = 1.0 / jnp.sum(jnp.exp(gl - gmax), axis=0, keepdims=True)
        erow = lax.broadcasted_iota(jnp.int32, el_all.shape, 0).astype(jnp.float32)
        egrp = jnp.floor(erow * (1.0 / EXPERTS_PER_GROUP))
        el = jnp.where(egrp == grp, el_all, NEG_BIG)
        e1 = jnp.max(el, axis=0, keepdims=True)
        i1 = jnp.min(jnp.where(el == e1, erow, float(N_EXPERTS)), axis=0, keepdims=True)
        el2 = jnp.where(erow == i1, NEG_BIG, el)
        e2 = jnp.max(el2, axis=0, keepdims=True)
        i2 = jnp.min(jnp.where(el2 == e2, erow, float(N_EXPERTS)), axis=0, keepdims=True)
        z2 = jnp.exp(e2 - e1)
        w1 = p_grp / (1.0 + z2)
        w2 = p_grp * z2 / (1.0 + z2)
        sel1 = erow == i1
        sel2 = erow == i2
        onehot = jnp.logical_or(sel1, sel2)
        t_r = lax.broadcasted_iota(jnp.int32, (tr, tr), 0)
        t_c = lax.broadcasted_iota(jnp.int32, (tr, tr), 1)
        before = (t_r < t_c).astype(jnp.bfloat16)
        rank = jnp.dot(onehot.astype(jnp.bfloat16), before, preferred_element_type=jnp.float32)
        cnt = jnp.sum(onehot.astype(jnp.float32), axis=1, keepdims=True)
        cnt_al = jnp.floor((cnt + (SEG_ALIGN - 1)) * (1.0 / SEG_ALIGN)) * SEG_ALIGN
        e_r = lax.broadcasted_iota(jnp.int32, (N_EXPERTS, N_EXPERTS), 0)
        e_c = lax.broadcasted_iota(jnp.int32, (N_EXPERTS, N_EXPERTS), 1)
        lstart = jnp.dot((e_c < e_r).astype(jnp.float32), jnp.broadcast_to(cnt_al, (N_EXPERTS, LANES)),
                         precision=hi, preferred_element_type=jnp.float32)[:, 0:1]
        slot = lstart + rank
        pos1 = jnp.sum(jnp.where(sel1, slot, 0.0), axis=0, keepdims=True)
        pos2 = jnp.sum(jnp.where(sel2, slot, 0.0), axis=0, keepdims=True)
        r8 = lax.broadcasted_iota(jnp.int32, (SUBLANES, tr), 0)
        info = jnp.where(r8 == ROW_POS0, pos1, jnp.where(r8 == ROW_POS1, pos2, jnp.where(
            r8 == ROW_W0, w1, jnp.where(r8 == ROW_W1, w2, jnp.where(
                r8 == ROW_E0, i1, jnp.where(r8 == ROW_E1, i2, 0.0))))))
        row_ref[:, rs] = info

        def split3(w):
            h = _bf16(w).astype(jnp.float32)
            m = _bf16(w - h).astype(jnp.float32)
            return h, m, _bf16(w - h - m).astype(jnp.float32)

        w1h, w1m, w1l = split3(w1)
        w2h, w2m, w2l = split3(w2)
        parts = jnp.where(r8 == 0, w1h, jnp.where(r8 == 1, w1m, jnp.where(r8 == 2, w1l, jnp.where(
            r8 == 3, w2h, jnp.where(r8 == 4, w2m, jnp.where(r8 == 5, w2l, jnp.where(r8 == 6, i1, i2)))))))
        col_ref[rs, :] = jnp.concatenate(
            [info, parts, jnp.zeros((LANES - 2 * SUBLANES, tr), jnp.float32)], axis=0).T
        cnt_ref[sb] = jnp.broadcast_to(cnt, (N_EXPERTS, LANES)).astype(jnp.int32)


def _out_proj_router(x2, att, hm, wo, layer, g, wrt, brt):
    t = x2.shape[0]
    tm = TM_OUT
    row = lambda i: (i, 0)
    fix = lambda i: (0, 0)
    return pl.pallas_call(
        _out_proj_router_kernel,
        grid=(t // tm,),
        in_specs=[
            pl.BlockSpec((tm, D_MODEL), row),
            pl.BlockSpec((tm, ATT_WIDTH), row),
            pl.BlockSpec((tm, ML_WIDTH), row),
            pl.BlockSpec((None, D_MODEL, D_MODEL), lambda i: (layer, 0, 0), pipeline_mode=pl.Buffered(1)),
            pl.BlockSpec((1, D_MODEL), fix),
            pl.BlockSpec((8 * SUBLANES, D_MODEL), fix),
            pl.BlockSpec((4 * SUBLANES, 1), fix),
        ],
        out_specs=[
            pl.BlockSpec((tm, D_MODEL), row),
            pl.BlockSpec((tm, D_MODEL), row),
            pl.BlockSpec((SUBLANES, tm), lambda i: (0, i)),
            pl.BlockSpec((tm, LANES), row),
            pl.BlockSpec((tm // TM_PROJ, N_EXPERTS, LANES), lambda i: (i, 0, 0)),
        ],
        out_shape=(
            jax.ShapeDtypeStruct((t, D_MODEL), jnp.float32),
            jax.ShapeDtypeStruct((t, D_MODEL), jnp.bfloat16),
            jax.ShapeDtypeStruct((SUBLANES, t), jnp.float32),
            jax.ShapeDtypeStruct((t, LANES), jnp.float32),
            jax.ShapeDtypeStruct((t // TM_PROJ, N_EXPERTS, LANES), jnp.int32),
        ),
        scratch_shapes=[pltpu.VMEM((D_MODEL, D_MODEL), jnp.bfloat16)],
        compiler_params=pltpu.CompilerParams(dimension_semantics=("arbitrary",),
                                             vmem_limit_bytes=VMEM_LIMIT),
        name="out_proj_router",
    )(x2, att, hm, wo, g, wrt, brt)


def _chunk_rows(j):
    return pl.ds(pl.multiple_of(j * SEG_ALIGN, SEG_ALIGN), SEG_ALIGN)


def _dispatch_kernel(nch_ref, dst_ref, nz_ref, zdst_ref, t_ref, col_ref, row_ref, xs_hbm,
                     xbuf, zx, sem_x, sem_z):
    i = pl.program_id(0)
    n = pl.num_programs(0)
    p = i % 2
    tm = t_ref.shape[0]

    def copy(tile, par, j):
        d = pl.ds(pl.multiple_of(dst_ref[tile, j], SEG_ALIGN), SEG_ALIGN)
        return pltpu.make_async_copy(xbuf.at[par, _chunk_rows(j)], xs_hbm.at[d], sem_x.at[par])

    def zero_copy(e, j):
        d = pl.ds(pl.multiple_of(zdst_ref[e] + j * SEG_ALIGN, SEG_ALIGN), SEG_ALIGN)
        return pltpu.make_async_copy(zx, xs_hbm.at[d], sem_z.at[0])

    def for_zero_chunks(fn):
        for e in range(N_EXPERTS):
            def body(j, c, e=e):
                fn(zero_copy(e, j))
                return c
            lax.fori_loop(0, nz_ref[e], body, 0)

    def wait_tile(tile, par):
        rows = pl.ds(0, pl.multiple_of(nch_ref[tile] * SEG_ALIGN, SEG_ALIGN))
        pltpu.make_async_copy(xbuf.at[par, rows], xs_hbm.at[rows], sem_x.at[par]).wait()

    @pl.when(i == 0)
    def _():
        zx[...] = jnp.zeros(zx.shape, zx.dtype)
        for_zero_chunks(lambda cp: cp.start())

    @pl.when(i >= 2)
    def _():
        wait_tile(i - 2, p)

    pos0 = row_ref[ROW_POS0:ROW_POS0 + 1, :].astype(jnp.int32)
    pos1 = row_ref[ROW_POS1:ROW_POS1 + 1, :].astype(jnp.int32)
    t = t_ref[...]
    side = _bf16(col_ref[...])
    rows = DISPATCH_ROWS
    for c in range(L_CAP // rows):
        r = lax.broadcasted_iota(jnp.int32, (rows, tm), 0) + c * rows
        perm = jnp.logical_or(r == pos0, r == pos1).astype(jnp.bfloat16)
        xbuf[p, c * rows:(c + 1) * rows, 0:D_MODEL] = _bf16(
            jnp.dot(perm, t, preferred_element_type=jnp.float32))
        xbuf[p, c * rows:(c + 1) * rows, D_MODEL:D_XS] = _bf16(
            jnp.dot(perm, side, preferred_element_type=jnp.float32))

    def start_body(j, c):
        copy(i, p, j).start()
        return c
    lax.fori_loop(0, nch_ref[i], start_body, 0)

    @pl.when(i == n - 1)
    def _():
        @pl.when(n >= 2)
        def _():
            wait_tile(i - 1, 1 - p)
        wait_tile(i, p)
        for_zero_chunks(lambda cp: cp.wait())


def _dispatch(t, col, row, nch, dst, nz, zdst):
    n_tok = t.shape[0]
    r_cap, _ = _moe_capacity(n_tok)
    tm = TM_PROJ
    grid_spec = pltpu.PrefetchScalarGridSpec(
        num_scalar_prefetch=4,
        grid=(n_tok // tm,),
        in_specs=[
            pl.BlockSpec((tm, D_MODEL), lambda i, *_: (i, 0)),
            pl.BlockSpec((tm, LANES), lambda i, *_: (i, 0)),
            pl.BlockSpec((SUBLANES, tm), lambda i, *_: (0, i)),
        ],
        out_specs=pl.BlockSpec(memory_space=pl.ANY),
        scratch_shapes=[
            pltpu.VMEM((2, L_CAP, D_XS), jnp.bfloat16),
            pltpu.VMEM((SEG_ALIGN, D_XS), jnp.bfloat16),
            pltpu.SemaphoreType.DMA((2,)),
            pltpu.SemaphoreType.DMA((1,)),
        ],
    )
    return pl.pallas_call(
        _dispatch_kernel,
        grid_spec=grid_spec,
        out_shape=jax.ShapeDtypeStruct((r_cap, D_XS), jnp.bfloat16),
        compiler_params=pltpu.CompilerParams(dimension_semantics=("arbitrary",),
                                             vmem_limit_bytes=VMEM_LIMIT),
        name="moe_dispatch",
    )(nch, dst, nz, zdst, t, col, row)


def _moe_kernel(blk_e_ref, nused_ref, nvalid_ref, xs_ref, wg_ref, wu_ref, wd_ref, ys_ref,
                wg_bf, wu_bf, wd_bf):
    b = pl.program_id(0)
    nv = nvalid_ref[b]
    new_expert = jnp.logical_or(b == 0, blk_e_ref[b] != blk_e_ref[jnp.maximum(b - 1, 0)])

    @pl.when(new_expert)
    def _():
        wg_bf[...] = _bf16(wg_ref[...])
        wu_bf[...] = _bf16(wu_ref[...])
        wd_bf[...] = _bf16(wd_ref[...])

    def compute(rows):
        x = xs_ref[0:rows, 0:D_MODEL]
        y = None
        fw = D_FF_EXPERT // MOE_FF_SPLIT
        for c in range(MOE_FF_SPLIT):
            cols = slice(c * fw, (c + 1) * fw)
            a = jnp.dot(x, wg_bf[:, cols], preferred_element_type=jnp.float32)
            u = jnp.dot(x, wu_bf[:, cols], preferred_element_type=jnp.float32)
            h = a * jax.nn.sigmoid(a) * u
            yc = jnp.dot(_bf16(h), wd_bf[cols, :], preferred_element_type=jnp.float32)
            y = yc if y is None else y + yc
        sd = xs_ref[0:rows, D_MODEL:D_XS].astype(jnp.float32)
        e_blk = blk_e_ref[b].astype(jnp.float32)
        w0 = sd[:, COL_W0H:COL_W0H + 1] + sd[:, COL_W0H + 1:COL_W0H + 2] + sd[:, COL_W0H + 2:COL_W0H + 3]
        w1 = sd[:, COL_W1H:COL_W1H + 1] + sd[:, COL_W1H + 1:COL_W1H + 2] + sd[:, COL_W1H + 2:COL_W1H + 3]
        w = jnp.where(sd[:, COL_E0:COL_E0 + 1] == e_blk, w0, w1)
        ys_ref[0:rows, :] = _bf16(y * w)

    half = MOE_BM // 2

    @pl.when(nv > half)
    def _():
        compute(MOE_BM)

    @pl.when(jnp.logical_and(nv > 0, nv <= half))
    def _():
        compute(half)
        ys_ref[half:, :] = jnp.zeros((MOE_BM - half, D_MODEL), ys_ref.dtype)


def _moe(xs, blk_e, nused, nvalid, wg, wu, wd, layer):
    blk = lambda b, be, nu, nv: (jnp.maximum(jnp.minimum(b, nu[0] - 1), 0), 0)
    wsel = lambda b, be, nu, nv: (layer, be[b], 0, 0)
    r_cap = xs.shape[0]
    grid_spec = pltpu.PrefetchScalarGridSpec(
        num_scalar_prefetch=3,
        grid=(r_cap // MOE_BM,),
        in_specs=[
            pl.BlockSpec((MOE_BM, D_XS), blk),
            pl.BlockSpec((None, None, D_MODEL, D_FF_EXPERT), wsel),
            pl.BlockSpec((None, None, D_MODEL, D_FF_EXPERT), wsel),
            pl.BlockSpec((None, None, D_FF_EXPERT, D_MODEL), wsel),
        ],
        out_specs=pl.BlockSpec((MOE_BM, D_MODEL), blk),
        scratch_shapes=[
            pltpu.VMEM((D_MODEL, D_FF_EXPERT), jnp.bfloat16),
            pltpu.VMEM((D_MODEL, D_FF_EXPERT), jnp.bfloat16),
            pltpu.VMEM((D_FF_EXPERT, D_MODEL), jnp.bfloat16),
        ],
    )
    return pl.pallas_call(
        _moe_kernel,
        grid_spec=grid_spec,
        out_shape=jax.ShapeDtypeStruct((r_cap, D_MODEL), jnp.bfloat16),
        compiler_params=pltpu.CompilerParams(dimension_semantics=("arbitrary",),
                                             vmem_limit_bytes=VMEM_LIMIT),
        name="moe_experts",
    )(blk_e, nused, nvalid, xs, wg, wu, wd)


def _combine_kernel(nch_ref, dst_ref, x1_ref, col_ref, ys_hbm, out_ref, ybuf, sem):
    i = pl.program_id(0)
    n = pl.num_programs(0)
    p = i % 2
    tm = x1_ref.shape[0]

    def copy(tile, par, j):
        s = pl.ds(pl.multiple_of(dst_ref[tile, j], SEG_ALIGN), SEG_ALIGN)
        return pltpu.make_async_copy(ys_hbm.at[s], ybuf.at[par, _chunk_rows(j)], sem.at[par])

    def start_tile(tile, par):
        def body(j, c):
            copy(tile, par, j).start()
            return c
        lax.fori_loop(0, nch_ref[tile], body, 0)

    @pl.when(i == 0)
    def _():
        ybuf[...] = jnp.zeros(ybuf.shape, ybuf.dtype)
        start_tile(0, 0)

    @pl.when(i + 1 < n)
    def _():
        start_tile(i + 1, 1 - p)

    rows = pl.ds(0, pl.multiple_of(nch_ref[i] * SEG_ALIGN, SEG_ALIGN))
    pltpu.make_async_copy(ys_hbm.at[rows], ybuf.at[p, rows], sem.at[p]).wait()

    rows = COMBINE_ROWS
    for c in range(tm // rows):
        col = col_ref[c * rows:(c + 1) * rows, :]
        pos0 = col[:, ROW_POS0:ROW_POS0 + 1].astype(jnp.int32)
        pos1 = col[:, ROW_POS1:ROW_POS1 + 1].astype(jnp.int32)
        l = lax.broadcasted_iota(jnp.int32, (rows, L_CAP), 1)
        perm = jnp.logical_or(l == pos0, l == pos1).astype(jnp.bfloat16)
        y = jnp.dot(perm, ybuf[p], preferred_element_type=jnp.float32)
        out_ref[c * rows:(c + 1) * rows, :] = x1_ref[c * rows:(c + 1) * rows, :] + y


def _combine(x1, col, ys, nch, dst):
    n_tok = x1.shape[0]
    tm = TM_PROJ
    grid_spec = pltpu.PrefetchScalarGridSpec(
        num_scalar_prefetch=2,
        grid=(n_tok // tm,),
        in_specs=[
            pl.BlockSpec((tm, D_MODEL), lambda i, *_: (i, 0)),
            pl.BlockSpec((tm, LANES), lambda i, *_: (i, 0)),
            pl.BlockSpec(memory_space=pl.ANY),
        ],
        out_specs=pl.BlockSpec((tm, D_MODEL), lambda i, *_: (i, 0)),
        scratch_shapes=[
            pltpu.VMEM((2, L_CAP, D_MODEL), jnp.bfloat16),
            pltpu.SemaphoreType.DMA((2,)),
        ],
    )
    return pl.pallas_call(
        _combine_kernel,
        grid_spec=grid_spec,
        out_shape=jax.ShapeDtypeStruct((n_tok, D_MODEL), jnp.float32),
        compiler_params=pltpu.CompilerParams(dimension_semantics=("arbitrary",),
                                             vmem_limit_bytes=VMEM_LIMIT),
        name="moe_combine",
    )(nch, dst, x1, col, ys)


def _routing_tables(counts, n_blk_cap):
    i32 = jnp.int32
    ca = ((counts + (SEG_ALIGN - 1)) // SEG_ALIGN) * SEG_ALIGN
    lend = jnp.cumsum(ca, axis=1)
    lstart = lend - ca
    nch = (lend[:, -1] // SEG_ALIGN).astype(i32)
    tot = jnp.sum(ca, axis=0)
    region = ((tot + (MOE_BM - 1)) // MOE_BM) * MOE_BM
    rend = jnp.cumsum(region)
    base = rend - region
    gstart = base[None, :] + jnp.cumsum(ca, axis=0) - ca
    j16 = jnp.arange(N_CHUNK, dtype=i32) * SEG_ALIGN
    in_seg = jnp.logical_and(lstart[:, None, :] <= j16[None, :, None], j16[None, :, None] < lend[:, None, :])
    dst = (jnp.sum(jnp.where(in_seg, (gstart - lstart)[:, None, :], 0), axis=2) + j16[None, :]).astype(i32)
    nused = (rend[-1] // MOE_BM).astype(i32)
    brow = jnp.minimum(jnp.arange(n_blk_cap, dtype=i32), nused - 1) * MOE_BM
    blk_e = jnp.minimum(jnp.sum(rend[None, :] <= brow[:, None], axis=1), N_EXPERTS - 1).astype(i32)
    zstart = (base + tot).astype(i32)
    nzc = ((region - tot) // SEG_ALIGN).astype(i32)
    blk_i = jnp.arange(n_blk_cap, dtype=i32)
    seg_end = jnp.sum(jnp.where(blk_e[:, None] == jnp.arange(N_EXPERTS)[None, :], (base + tot)[None, :], 0), axis=1)
    nvalid = jnp.where(blk_i < nused, jnp.clip(seg_end - blk_i * MOE_BM, 0, MOE_BM), 0).astype(i32)
    return nch, dst, nzc, zstart, blk_e, nused[None], nvalid


def _t5_bucket_np(dist):
    max_exact = N_BUCKETS // 2
    d = np.maximum(dist, 1).astype(np.float32)
    large = max_exact + (np.log(d / max_exact) / math.log(MAX_DISTANCE / max_exact)
                         * (N_BUCKETS - max_exact)).astype(np.int32)
    large = np.minimum(large, N_BUCKETS - 1)
    return np.where(dist < max_exact, dist, large)


def _attention_bias(rel_bias):
    qi = np.arange(ATT_BLOCK)[:, None]
    kj = np.arange(2 * ATT_BLOCK)[None, :]
    dist = qi + ATT_BLOCK - kj
    in_window = (dist >= 0) & (dist < WINDOW)
    bucket = _t5_bucket_np(np.clip(dist, 0, WINDOW - 1))
    onehot = (bucket[None] == np.arange(N_BUCKETS)[:, None, None]).astype(np.float32)
    bias = jnp.einsum('nh,nqk->hqk', rel_bias.astype(jnp.float32), jnp.asarray(onehot),
                      precision=lax.Precision.HIGHEST)
    bias = jnp.where(jnp.asarray(in_window)[None], bias * LOG2E, NEG_BIG)
    no_prev = jnp.asarray(kj < ATT_BLOCK)[None]
    return jnp.stack([bias, jnp.where(no_prev, NEG_BIG, bias)])


def _block_diag_mean(width, block):
    idx = np.arange(width) // block
    return jnp.asarray((idx[:, None] == idx[None, :]).astype(np.float32) / block, dtype=jnp.bfloat16)


def kernel(x, rel_bias, norm_mix_g, w_in, q_norm_g, k_norm_g, attn_sink, conv_w, conv_b, gate_b,
           mlstm_norm_g, w_out, norm_ffn_g, w_router_group, b_router_group, w_router_expert,
           b_router_expert, w_gate, w_up, w_down):
    batch, seq_len, _ = x.shape
    n_tok = batch * seq_len
    assert seq_len % TM_PROJ == 0 and seq_len % TM_IN == 0 and seq_len % ML_STEP == 0
    f32 = jnp.float32
    bias = _attention_bias(rel_bias)
    bdq = _block_diag_mean(ATT_WIDTH, ATT_HEAD_DIM)
    bdk = _block_diag_mean(LANES, ATT_HEAD_DIM)
    x2 = x.reshape(n_tok, D_MODEL)
    for l in range(DEPTH):
        wgt = _bf16(w_in[l, :, O_G:N_IN].T)
        qg = (jnp.tile(q_norm_g[l].astype(f32), ATT_HEADS) * (ATT_HEAD_DIM ** -0.5 * LOG2E))[None, :]
        kg = jnp.tile(k_norm_g[l].astype(f32), ATT_KV_HEADS)[None, :]
        gbt = gate_b[l].astype(f32)[:, None]
        qn, kn, vd, qkm, vm, om, gates_t = _in_proj(
            x2, norm_mix_g[l][None, :], w_in, l, wgt, qg, kg, bdq, bdk,
            conv_w[l], conv_b[l][None, :], gbt, batch, seq_len)
        att = _attention(qn, kn, vd, bias, attn_sink[l].astype(f32) * LOG2E, batch, seq_len)
        r3 = lambda a: a.reshape(batch, seq_len, a.shape[-1])
        hm = _mlstm(r3(qkm), r3(vm), r3(om), gates_t, mlstm_norm_g[l][None, :], batch, seq_len)
        n_rt = 4 * SUBLANES
        wrt = jnp.pad(jnp.concatenate([w_router_expert[l], w_router_group[l]], axis=1).astype(f32).T,
                      ((0, n_rt - N_EXPERTS - N_GROUPS), (0, 0)))
        brt = jnp.pad(jnp.concatenate([b_router_expert[l], b_router_group[l]]).astype(f32),
                      (0, n_rt - N_EXPERTS - N_GROUPS))[:, None]
        wrt_hi = _bf16(wrt)
        wrt = jnp.concatenate([wrt_hi, _bf16(wrt - wrt_hi.astype(f32))], axis=0)
        x1, t, row, col, cnt = _out_proj_router(x2, att, hm.reshape(n_tok, ML_WIDTH), w_out, l,
                                                norm_ffn_g[l][None, :], wrt, brt)
        _, n_blk_cap = _moe_capacity(n_tok)
        nch, dst, nz, zdst, blk_e, nused, nvalid = _routing_tables(cnt[:, :, 0], n_blk_cap)
        xs = _dispatch(t, col, row, nch, dst, nz, zdst)
        ys = _moe(xs, blk_e, nused, nvalid, w_gate, w_up, w_down, l)
        x2 = _combine(x1, col, ys, nch, dst)
    return x2.reshape(batch, seq_len, D_MODEL)
```

```python
import functools
import math

import jax
import jax.numpy as jnp
import numpy as np
from jax import lax
from jax.experimental import pallas as pl
from jax.experimental.pallas import tpu as pltpu

D_MODEL = 1024
DEPTH = 2
ATT_HEADS = 8
ATT_KV_HEADS = 2
ATT_HEAD_DIM = 64
ATT_WIDTH = ATT_HEADS * ATT_HEAD_DIM
WINDOW = 128
ATT_BLOCK = 128
N_BUCKETS = 32
MAX_DISTANCE = 128
ML_HEADS = 4
ML_DQK = 64
ML_DV = 128
ML_WIDTH = ML_HEADS * ML_DV
ML_CHUNK = 64
CONV_K = 4
N_GROUPS = 4
EXPERTS_PER_GROUP = 4
N_EXPERTS = N_GROUPS * EXPERTS_PER_GROUP
D_FF_EXPERT = 512
EPS = 1e-6

LANES = 128
SUBLANES = 8
NEG_BIG = -1e30
LOG2E = math.log2(math.e)
VMEM_LIMIT = 48 * 1024 * 1024

O_Q = 0
O_K = O_Q + ATT_WIDTH
O_V = O_K + ATT_KV_HEADS * ATT_HEAD_DIM
O_QM = O_V + ATT_KV_HEADS * ATT_HEAD_DIM
O_KM = O_QM + ML_HEADS * ML_DQK
O_VM = O_KM + ML_HEADS * ML_DQK
O_OM = O_VM + ML_WIDTH
O_G = O_OM + ML_WIDTH
N_IN = O_G + 2 * ML_HEADS

ATT_STEP_BLOCKS = 4
TM_IN = 1024
IN_SUBBLOCKS = 2
TM_OUT = 1024
TM_PROJ = 512
ML_KCHUNK = 256
ML_STEP = ML_KCHUNK

SEG_ALIGN = 16
MOE_BM = 512
MOE_FF_SPLIT = 2
L_CAP = 2 * TM_PROJ + N_EXPERTS * SEG_ALIGN
N_CHUNK = L_CAP // SEG_ALIGN
D_XS = D_MODEL + LANES
DUMP_ROWS = 2 * N_CHUNK * SEG_ALIGN
DISPATCH_ROWS = 256
COMBINE_ROWS = 256
ROW_POS0, ROW_POS1, ROW_W0, ROW_W1, ROW_E0, ROW_E1 = 0, 1, 2, 3, 4, 5
COL_W0H, COL_W1H, COL_E0, COL_E1 = 8, 11, 14, 15


def _moe_capacity(n_tok):
    n_tiles = n_tok // TM_PROJ
    rows = 2 * n_tok + n_tiles * N_EXPERTS * (SEG_ALIGN - 1) + N_EXPERTS * (MOE_BM - SEG_ALIGN)
    n_blk = -(-rows // MOE_BM)
    return n_blk * MOE_BM, n_blk


def _bf16(a):
    return a.astype(jnp.bfloat16)


def _split3(a):
    hi = _bf16(a)
    r1 = a - hi.astype(jnp.float32)
    mid = _bf16(r1)
    return hi, mid, _bf16(r1 - mid.astype(jnp.float32))


def _log_sigmoid(z):
    return jnp.minimum(z, 0.0) - jnp.log(1.0 + jnp.exp(-jnp.abs(z)))


def _in_proj_kernel(tiles_per_seq, x_ref, g_ref, w_ref, wgt_ref, qg_ref, kg_ref, bdq_ref, bdk_ref,
                    cw_ref, cb_ref, gbt_ref,
                    qn_ref, kn_ref, vd_ref, qkm_ref, vm_ref, om_ref, gatet_ref,
                    w_bf, conv_scr):
    i = pl.program_id(0)
    tm = x_ref.shape[0]

    @pl.when(i == 0)
    def _():
        w_bf[...] = _bf16(w_ref[...])

    conv_w = 2 * ML_HEADS * ML_DQK

    @pl.when(i % tiles_per_seq == 0)
    def _():
        conv_scr[0:SUBLANES, :] = jnp.zeros((SUBLANES, conv_w), jnp.float32)

    @pl.when(i % tiles_per_seq != 0)
    def _():
        conv_scr[0:SUBLANES, :] = conv_scr[tm:tm + SUBLANES, :]

    sub = tm // IN_SUBBLOCKS
    for sb in range(IN_SUBBLOCKS):
        rs = slice(sb * sub, (sb + 1) * sub)
        x = x_ref[rs, :]
        hn = x * lax.rsqrt(jnp.mean(x * x, axis=-1, keepdims=True) + EPS) * g_ref[...]
        hb = _bf16(hn)

        def proj(c0, width):
            return jnp.dot(hb, w_bf[:, c0:c0 + width], preferred_element_type=jnp.float32)

        q = proj(O_Q, ATT_WIDTH)
        q_ms = jnp.dot(_bf16(q * q), bdq_ref[...], preferred_element_type=jnp.float32)
        qn_ref[rs, :] = _bf16(q * lax.rsqrt(q_ms + EPS) * qg_ref[...])
        kv = proj(O_K, 2 * LANES)
        k = kv[:, 0:LANES]
        v = kv[:, LANES:2 * LANES]
        k_ms = jnp.dot(_bf16(k * k), bdk_ref[...], preferred_element_type=jnp.float32)
        kn = k * lax.rsqrt(k_ms + EPS) * kg_ref[...]
        low = lax.broadcasted_iota(jnp.int32, kn.shape, 1) < ATT_HEAD_DIM

        def dup_heads(a):
            swapped = pltpu.roll(a, ATT_HEAD_DIM, axis=1)
            return jnp.concatenate([jnp.where(low, a, swapped), jnp.where(low, swapped, a)], axis=1)

        kn_ref[rs, :] = _bf16(dup_heads(kn))
        vd_ref[rs, :] = _bf16(dup_heads(v))

        qk = proj(O_QM, conv_w)
        base = SUBLANES + sb * sub
        conv_scr[base:base + sub, :] = qk
        y = qk * cw_ref[CONV_K - 1:CONV_K, :] + cb_ref[...]
        for j in range(CONV_K - 1):
            off = base - (CONV_K - 1) + j
            y = y + conv_scr[off:off + sub, :] * cw_ref[j:j + 1, :]
        y = y * jax.nn.sigmoid(y)
        lane = lax.broadcasted_iota(jnp.int32, y.shape, 1)
        y = jnp.where(lane >= ML_HEADS * ML_DQK, y * (ML_DQK ** -0.5), y)
        qkm_ref[rs, :] = _bf16(y)

        vm_ref[rs, :] = _bf16(proj(O_VM, ML_WIDTH))
        om_ref[rs, :] = _bf16(jax.nn.sigmoid(proj(O_OM, ML_WIDTH)))

        gt = lax.dot_general(wgt_ref[...], hb, (((1,), (1,)), ((), ())),
                             preferred_element_type=jnp.float32) + gbt_ref[...]
        grow = lax.broadcasted_iota(jnp.int32, gt.shape, 0)
        gatet_ref[:, rs] = jnp.where(grow >= ML_HEADS, _log_sigmoid(gt), gt)


def _in_proj(x2, g, w_in, layer, wgt, qg, kg, bdq, bdk, cw, cb, gbt, batch, seq_len):
    t = x2.shape[0]
    tm = TM_IN
    n = t // tm
    tps = seq_len // tm
    row = lambda i: (i, 0)
    fix = lambda i: (0, 0)
    kv_w = 2 * ATT_KV_HEADS * ATT_HEAD_DIM
    out_shapes = (
        jax.ShapeDtypeStruct((t, ATT_WIDTH), jnp.bfloat16),
        jax.ShapeDtypeStruct((t, kv_w), jnp.bfloat16),
        jax.ShapeDtypeStruct((t, kv_w), jnp.bfloat16),
        jax.ShapeDtypeStruct((t, 2 * ML_HEADS * ML_DQK), jnp.bfloat16),
        jax.ShapeDtypeStruct((t, ML_WIDTH), jnp.bfloat16),
        jax.ShapeDtypeStruct((t, ML_WIDTH), jnp.bfloat16),
        jax.ShapeDtypeStruct((batch, SUBLANES, seq_len), jnp.float32),
    )
    in_specs = [
        pl.BlockSpec((tm, D_MODEL), row),
        pl.BlockSpec((1, D_MODEL), fix),
        pl.BlockSpec((None, D_MODEL, N_IN), lambda i: (layer, 0, 0), pipeline_mode=pl.Buffered(1)),
        pl.BlockSpec((SUBLANES, D_MODEL), fix),
        pl.BlockSpec((1, ATT_WIDTH), fix),
        pl.BlockSpec((1, LANES), fix),
        pl.BlockSpec((ATT_WIDTH, ATT_WIDTH), fix),
        pl.BlockSpec((LANES, LANES), fix),
        pl.BlockSpec((CONV_K, 2 * ML_HEADS * ML_DQK), fix),
        pl.BlockSpec((1, 2 * ML_HEADS * ML_DQK), fix),
        pl.BlockSpec((SUBLANES, 1), fix),
    ]
    out_specs = [
        pl.BlockSpec((tm, ATT_WIDTH), row),
        pl.BlockSpec((tm, kv_w), row),
        pl.BlockSpec((tm, kv_w), row),
        pl.BlockSpec((tm, 2 * ML_HEADS * ML_DQK), row),
        pl.BlockSpec((tm, ML_WIDTH), row),
        pl.BlockSpec((tm, ML_WIDTH), row),
        pl.BlockSpec((None, SUBLANES, tm), lambda i: (i // tps, 0, i % tps)),
    ]
    return pl.pallas_call(
        functools.partial(_in_proj_kernel, seq_len // tm),
        grid=(n,),
        in_specs=in_specs,
        out_specs=out_specs,
        out_shape=out_shapes,
        scratch_shapes=[pltpu.VMEM((D_MODEL, N_IN), jnp.bfloat16),
                        pltpu.VMEM((tm + 2 * SUBLANES, 2 * ML_HEADS * ML_DQK), jnp.float32)],
        compiler_params=pltpu.CompilerParams(dimension_semantics=("arbitrary",),
                                             vmem_limit_bytes=VMEM_LIMIT),
        name="in_proj",
    )(x2, g, w_in, wgt, qg, kg, bdq, bdk, cw, cb, gbt)


def _attn_kernel(sink_ref, q_ref, kp_ref, kc_ref, vp_ref, vc_ref, bias_ref, o_ref):
    i = pl.program_id(1)
    blk = ATT_BLOCK
    lane = lax.broadcasted_iota(jnp.int32, (blk, LANES), 1)
    low = lane < ATT_HEAD_DIM
    group = ATT_HEADS // ATT_KV_HEADS
    ones = jnp.ones((2 * blk, LANES), jnp.bfloat16)
    for j in range(ATT_STEP_BLOCKS):
        rows = slice(j * blk, (j + 1) * blk)
        first = (i == 0).astype(jnp.int32) if j == 0 else 0
        for kv in range(ATT_KV_HEADS):
            lanes = slice(kv * LANES, (kv + 1) * LANES)
            if j == 0:
                kcat = jnp.concatenate([kp_ref[:, lanes], kc_ref[0:blk, lanes]], axis=0)
                vrows = jnp.concatenate([vp_ref[:, lanes], vc_ref[0:blk, lanes]], axis=0)
            else:
                kcat = kc_ref[(j - 1) * blk:(j + 1) * blk, lanes]
                vrows = vc_ref[(j - 1) * blk:(j + 1) * blk, lanes]
            vcat = jnp.concatenate([vrows, ones], axis=1)
            heads = range(kv * group, (kv + 1) * group)
            qs = []
            for h in heads:
                qp = q_ref[rows, (h // 2) * LANES:(h // 2 + 1) * LANES]
                qs.append(jnp.where(low if h % 2 == 0 else jnp.logical_not(low), qp, jnp.zeros_like(qp)))
            s_all = lax.dot_general(jnp.concatenate(qs, axis=0), kcat, (((1,), (1,)), ((), ())),
                                    preferred_element_type=jnp.float32)
            ps, ms = [], []
            for n, h in enumerate(heads):
                logits = s_all[n * blk:(n + 1) * blk, :] + bias_ref[first, h]
                row_max = jnp.broadcast_to(jnp.max(logits, axis=-1, keepdims=True), (blk, LANES))
                m = jnp.maximum(row_max, sink_ref[h])
                ps.append(_bf16(jnp.exp2(logits - jnp.concatenate([m, m], axis=1))))
                ms.append(m)
            o_all = jnp.dot(jnp.concatenate(ps, axis=0), vcat, preferred_element_type=jnp.float32)
            outs = []
            for n, h in enumerate(heads):
                o = o_all[n * blk:(n + 1) * blk, :]
                den = o[:, LANES:2 * LANES] + jnp.exp2(sink_ref[h] - ms[n])
                outs.append(o[:, 0:LANES] / den)
            for n in range(0, group, 2):
                pair = (kv * group + n) // 2
                o_ref[rows, pair * LANES:(pair + 1) * LANES] = _bf16(
                    jnp.where(low, outs[n], outs[n + 1]))


def _attention(qn, kn, vd, bias, sink, batch, seq_len):
    step = ATT_STEP_BLOCKS * ATT_BLOCK
    ns = seq_len // step
    cur = lambda b, i, s: (b * ns + i, 0)
    prev = lambda b, i, s: ((b * ns + i) * ATT_STEP_BLOCKS - jnp.minimum(i, 1), 0)
    grid_spec = pltpu.PrefetchScalarGridSpec(
        num_scalar_prefetch=1,
        grid=(batch, ns),
        in_specs=[
            pl.BlockSpec((step, ATT_WIDTH), cur),
            pl.BlockSpec((ATT_BLOCK, 2 * LANES), prev),
            pl.BlockSpec((step, 2 * LANES), cur),
            pl.BlockSpec((ATT_BLOCK, 2 * LANES), prev),
            pl.BlockSpec((step, 2 * LANES), cur),
            pl.BlockSpec((2, ATT_HEADS, ATT_BLOCK, 2 * ATT_BLOCK), lambda b, i, s: (0, 0, 0, 0)),
        ],
        out_specs=pl.BlockSpec((step, ATT_WIDTH), cur),
    )
    return pl.pallas_call(
        _attn_kernel,
        grid_spec=grid_spec,
        out_shape=jax.ShapeDtypeStruct((batch * seq_len, ATT_WIDTH), jnp.bfloat16),
        compiler_params=pltpu.CompilerParams(dimension_semantics=("arbitrary", "arbitrary"),
                                             vmem_limit_bytes=VMEM_LIMIT),
        name="swa_attention",
    )(sink, qn, kn, kn, vd, vd, bias)


def _mlstm_kernel(qk_ref, v_ref, o_ref, gt_ref, ng_ref, out_ref, c_scr, m_scr):
    step = pl.program_id(0)
    batch = qk_ref.shape[0]
    L = ML_KCHUNK
    pairs = ML_HEADS // 2

    @pl.when(step == 0)
    def _():
        c_scr[...] = jnp.zeros(c_scr.shape, jnp.float32)
        m_scr[...] = jnp.zeros(m_scr.shape, jnp.float32)

    r_i = lax.broadcasted_iota(jnp.int32, (L, L), 0)
    c_i = lax.broadcasted_iota(jnp.int32, (L, L), 1)
    causal = c_i <= r_i
    tril_bf = causal.astype(jnp.bfloat16)
    triu_bf = (r_i <= c_i).astype(jnp.bfloat16)
    mean_dv = jnp.full((ML_DV, ML_DV), 1.0 / ML_DV, jnp.bfloat16)
    lane = lax.broadcasted_iota(jnp.int32, (L, LANES), 1)
    low = lane < ML_DQK
    ones_dv = jnp.ones((L, ML_DV), jnp.bfloat16)
    row2 = lax.broadcasted_iota(jnp.int32, (2 * ML_DQK, 2 * ML_DV), 0)

    def twice(a):
        return jnp.concatenate([a, a], axis=1)

    m_state = [m_scr[k] for k in range(batch * ML_HEADS)]
    c_state = [c_scr[k] for k in range(batch * pairs)]

    for ch in range(ML_STEP // L):
        r0 = ch * L
        for b in range(batch):
            gt = gt_ref[b, :, r0:r0 + L]
            g = jnp.concatenate([gt, jnp.zeros((LANES - SUBLANES, L), jnp.float32)], axis=0).T
            bcols = sum(jnp.dot(tril_bf, part, preferred_element_type=jnp.float32) for part in _split3(g))
            brows = sum(jnp.dot(part, triu_bf, preferred_element_type=jnp.float32) for part in _split3(gt))
            for pair in range(pairs):
                sidx = b * pairs + pair
                qp = qk_ref[b, r0:r0 + L, pair * LANES:(pair + 1) * LANES]
                kp = qk_ref[b, r0:r0 + L, (pairs + pair) * LANES:(pairs + pair + 1) * LANES]
                c_pair = c_state[sidx]
                c_bf = _bf16(c_pair)
                new_c = None
                decays = []
                for sub in range(2):
                    h = 2 * pair + sub
                    sel = low if sub == 0 else jnp.logical_not(low)
                    m_prev = m_state[b * ML_HEADS + h]
                    bc = jnp.broadcast_to(bcols[:, ML_HEADS + h:ML_HEADS + h + 1], (L, LANES))
                    lic = jnp.broadcast_to(g[:, h:h + 1], (L, LANES))
                    br = brows[ML_HEADS + h:ML_HEADS + h + 1, :]
                    lir = gt[h:h + 1, :]
                    log_d = jnp.where(causal, twice(bc) - (br - lir), NEG_BIG)
                    m_inter = bc + m_prev
                    row_max = jnp.broadcast_to(jnp.max(log_d, axis=-1, keepdims=True), (L, LANES))
                    m_row = jnp.maximum(m_inter, row_max)
                    d = jnp.exp(log_d - twice(m_row))
                    inter = jnp.exp(m_inter - m_row)
                    qm = jnp.where(sel, qp, jnp.zeros_like(qp))
                    s = lax.dot_general(qm, kp, (((1,), (1,)), ((), ())),
                                        preferred_element_type=jnp.float32) * d
                    v_ext = jnp.concatenate([v_ref[b, r0:r0 + L, h * ML_DV:(h + 1) * ML_DV], ones_dv],
                                            axis=-1)
                    num = twice(inter) * jnp.dot(qm, c_bf, preferred_element_type=jnp.float32) \
                        + jnp.dot(_bf16(s), v_ext, preferred_element_type=jnp.float32)
                    den = num[:, ML_DV:2 * ML_DV]
                    hval = num[:, 0:ML_DV] / jnp.maximum(jnp.abs(den), jnp.exp(-m_row))
                    h_ms = jnp.dot(_bf16(hval * hval), mean_dv, preferred_element_type=jnp.float32)
                    hn = hval * lax.rsqrt(h_ms + EPS)
                    hn = hn * ng_ref[:, h * ML_DV:(h + 1) * ML_DV]
                    out_ref[b, r0:r0 + L, h * ML_DV:(h + 1) * ML_DV] = _bf16(
                        hn * o_ref[b, r0:r0 + L, h * ML_DV:(h + 1) * ML_DV].astype(jnp.float32))
                    b_last = bc[L - 1:L, :]
                    log_w = b_last - bc + lic
                    m_next = jnp.maximum(b_last + m_prev, jnp.max(log_w, axis=0, keepdims=True))
                    w = jnp.exp(log_w - m_next)
                    decays.append(jnp.exp(b_last + m_prev - m_next))
                    m_state[b * ML_HEADS + h] = m_next
                    kw = _bf16(jnp.where(sel, kp.astype(jnp.float32) * w, 0.0))
                    upd = lax.dot_general(kw, v_ext, (((0,), (0,)), ((), ())),
                                          preferred_element_type=jnp.float32)
                    new_c = upd if new_c is None else new_c + upd
                decay_rows = jnp.where(row2 < ML_DQK, twice(decays[0]), twice(decays[1]))
                c_state[sidx] = decay_rows * c_pair + new_c

    for k in range(batch * ML_HEADS):
        m_scr[k] = m_state[k]
    for k in range(batch * pairs):
        c_scr[k] = c_state[k]


def _mlstm(qkm, vm, om, gates_t, ng, batch, seq_len):
    n_steps = seq_len // ML_STEP
    blk = lambda c: (0, c, 0)
    return pl.pallas_call(
        _mlstm_kernel,
        grid=(n_steps,),
        in_specs=[
            pl.BlockSpec((batch, ML_STEP, 2 * ML_HEADS * ML_DQK), blk),
            pl.BlockSpec((batch, ML_STEP, ML_WIDTH), blk),
            pl.BlockSpec((batch, ML_STEP, ML_WIDTH), blk),
            pl.BlockSpec((batch, SUBLANES, ML_STEP), lambda c: (0, 0, c)),
            pl.BlockSpec((1, ML_WIDTH), lambda c: (0, 0)),
        ],
        out_specs=pl.BlockSpec((batch, ML_STEP, ML_WIDTH), blk),
        out_shape=jax.ShapeDtypeStruct((batch, seq_len, ML_WIDTH), jnp.bfloat16),
        scratch_shapes=[
            pltpu.VMEM((batch * ML_HEADS // 2, 2 * ML_DQK, 2 * ML_DV), jnp.float32),
            pltpu.VMEM((batch * ML_HEADS, 1, LANES), jnp.float32),
        ],
        compiler_params=pltpu.CompilerParams(dimension_semantics=("arbitrary",),
                                             vmem_limit_bytes=VMEM_LIMIT),
        name="mlstm_scan",
    )(qkm, vm, om, gates_t, ng)


def _out_proj_router_kernel(x_ref, att_ref, hm_ref, wo_ref, g_ref, wrt_ref, brt_ref,
                            x1_ref, t_ref, row_ref, col_ref, cnt_ref, wo_bf):
    tm = x_ref.shape[0]
    tr = TM_PROJ
    hi = lax.Precision.HIGHEST

    @pl.when(pl.program_id(0) == 0)
    def _():
        wo_bf[...] = _bf16(wo_ref[...])

    for sb in range(tm // TM_PROJ):
        rs = slice(sb * TM_PROJ, (sb + 1) * TM_PROJ)
        x1 = x_ref[rs, :] \
            + jnp.dot(att_ref[rs, :], wo_bf[0:ATT_WIDTH, :], preferred_element_type=jnp.float32) \
            + jnp.dot(hm_ref[rs, :], wo_bf[ATT_WIDTH:, :], preferred_element_type=jnp.float32)
        x1_ref[rs, :] = x1
        tn = x1 * lax.rsqrt(jnp.mean(x1 * x1, axis=-1, keepdims=True) + EPS) * g_ref[...]
        tn_hi = _bf16(tn)
        t_ref[rs, :] = tn_hi
        tn_lo = _bf16(tn - tn_hi.astype(jnp.float32))
        nt = (((1,), (1,)), ((), ()))
        n_rt = wrt_ref.shape[0] // 2
        p_hi = lax.dot_general(wrt_ref[...], tn_hi, nt, preferred_element_type=jnp.float32)
        p_lo = lax.dot_general(wrt_ref[0:n_rt, :], tn_lo, nt, preferred_element_type=jnp.float32)
        logits = p_hi[0:n_rt, :] + p_hi[n_rt:, :] + p_lo + brt_ref[...]
        el_all = logits[0:N_EXPERTS, :]
        gl = logits[N_EXPERTS:N_EXPERTS + SUBLANES, :]
        grow = lax.broadcasted_iota(jnp.int32, gl.shape, 0).astype(jnp.float32)
        gl = jnp.where(grow < N_GROUPS, gl, NEG_BIG)
        gmax = jnp.max(gl, axis=0, keepdims=True)
        grp = jnp.min(jnp.where(gl == gmax, grow, float(N_GROUPS)), axis=0, keepdims=True)
        p_grp = 1.0 / jnp.sum(jnp.exp(gl - gmax), axis=0, keepdims=True)
        erow = lax.broadcasted_iota(jnp.int32, el_all.shape, 0).astype(jnp.float32)
        egrp = jnp.floor(erow * (1.0 / EXPERTS_PER_GROUP))
        el = jnp.where(egrp == grp, el_all, NEG_BIG)
        e1 = jnp.max(el, axis=0, keepdims=True)
        i1 = jnp.min(jnp.where(el == e1, erow, float(N_EXPERTS)), axis=0, keepdims=True)
        el2 = jnp.where(erow == i1, NEG_BIG, el)
        e2 = jnp.max(el2, axis=0, keepdims=True)
        i2 = jnp.min(jnp.where(el2 == e2, erow, float(N_EXPERTS)), axis=0, keepdims=True)
        z2 = jnp.exp(e2 - e1)
        w1 = p_grp / (1.0 + z2)
        w2 = p_grp * z2 / (1.0 + z2)
        sel1 = erow == i1
        sel2 = erow == i2
        onehot = jnp.logical_or(sel1, sel2)
        t_r = lax.broadcasted_iota(jnp.int32, (tr, tr), 0)
        t_c = lax.broadcasted_iota(jnp.int32, (tr, tr), 1)
        before = (t_r < t_c).astype(jnp.bfloat16)
        rank = jnp.dot(onehot.astype(jnp.bfloat16), before, preferred_element_type=jnp.float32)
        cnt = jnp.sum(onehot.astype(jnp.float32), axis=1, keepdims=True)
        cnt_al = jnp.floor((cnt + (SEG_ALIGN - 1)) * (1.0 / SEG_ALIGN)) * SEG_ALIGN
        e_r = lax.broadcasted_iota(jnp.int32, (N_EXPERTS, N_EXPERTS), 0)
        e_c = lax.broadcasted_iota(jnp.int32, (N_EXPERTS, N_EXPERTS), 1)
        lstart = jnp.dot((e_c < e_r).astype(jnp.float32), jnp.broadcast_to(cnt_al, (N_EXPERTS, LANES)),
                         precision=hi, preferred_element_type=jnp.float32)[:, 0:1]
        slot = lstart + rank
        pos1 = jnp.sum(jnp.where(sel1, slot, 0.0), axis=0, keepdims=True)
        pos2 = jnp.sum(jnp.where(sel2, slot, 0.0), axis=0, keepdims=True)
        r8 = lax.broadcasted_iota(jnp.int32, (SUBLANES, tr), 0)
        info = jnp.where(r8 == ROW_POS0, pos1, jnp.where(r8 == ROW_POS1, pos2, jnp.where(
            r8 == ROW_W0, w1, jnp.where(r8 == ROW_W1, w2, jnp.where(
                r8 == ROW_E0, i1, jnp.where(r8 == ROW_E1, i2, 0.0))))))
        row_ref[:, rs] = info

        def split3(w):
            h = _bf16(w).astype(jnp.float32)
            m = _bf16(w - h).astype(jnp.float32)
            return h, m, _bf16(w - h - m).astype(jnp.float32)

        w1h, w1m, w1l = split3(w1)
        w2h, w2m, w2l = split3(w2)
        parts = jnp.where(r8 == 0, w1h, jnp.where(r8 == 1, w1m, jnp.where(r8 == 2, w1l, jnp.where(
            r8 == 3, w2h, jnp.where(r8 == 4, w2m, jnp.where(r8 == 5, w2l, jnp.where(r8 == 6, i1, i2)))))))
        col_ref[rs, :] = jnp.concatenate(
            [info, parts, jnp.zeros((LANES - 2 * SUBLANES, tr), jnp.float32)], axis=0).T
        cnt_ref[sb] = jnp.broadcast_to(cnt, (N_EXPERTS, LANES)).astype(jnp.int32)


def _out_proj_router(x2, att, hm, wo, layer, g, wrt, brt):
    t = x2.shape[0]
    tm = TM_OUT
    row = lambda i: (i, 0)
    fix = lambda i: (0, 0)
    return pl.pallas_call(
        _out_proj_router_kernel,
        grid=(t // tm,),
        in_specs=[
            pl.BlockSpec((tm, D_MODEL), row),
            pl.BlockSpec((tm, ATT_WIDTH), row),
            pl.BlockSpec((tm, ML_WIDTH), row),
            pl.BlockSpec((None, D_MODEL, D_MODEL), lambda i: (layer, 0, 0), pipeline_mode=pl.Buffered(1)),
            pl.BlockSpec((1, D_MODEL), fix),
            pl.BlockSpec((8 * SUBLANES, D_MODEL), fix),
            pl.BlockSpec((4 * SUBLANES, 1), fix),
        ],
        out_specs=[
            pl.BlockSpec((tm, D_MODEL), row),
            pl.BlockSpec((tm, D_MODEL), row),
            pl.BlockSpec((SUBLANES, tm), lambda i: (0, i)),
            pl.BlockSpec((tm, LANES), row),
            pl.BlockSpec((tm // TM_PROJ, N_EXPERTS, LANES), lambda i: (i, 0, 0)),
        ],
        out_shape=(
            jax.ShapeDtypeStruct((t, D_MODEL), jnp.float32),
            jax.ShapeDtypeStruct((t, D_MODEL), jnp.bfloat16),
            jax.ShapeDtypeStruct((SUBLANES, t), jnp.float32),
            jax.ShapeDtypeStruct((t, LANES), jnp.float32),
            jax.ShapeDtypeStruct((t // TM_PROJ, N_EXPERTS, LANES), jnp.int32),
        ),
        scratch_shapes=[pltpu.VMEM((D_MODEL, D_MODEL), jnp.bfloat16)],
        compiler_params=pltpu.CompilerParams(dimension_semantics=("arbitrary",),
                                             vmem_limit_bytes=VMEM_LIMIT),
        name="out_proj_router",
    )(x2, att, hm, wo, g, wrt, brt)


def _dispatch_kernel(dst_ref, nz_ref, zdst_ref, t_ref, col_ref, row_ref, xs_hbm,
                     xbuf, zx, sem_x, sem_z):
    i = pl.program_id(0)
    n = pl.num_programs(0)
    p = i % 2
    tm = t_ref.shape[0]

    def copy(tile, par, j):
        d = pl.ds(pl.multiple_of(dst_ref[tile, j], SEG_ALIGN), SEG_ALIGN)
        return pltpu.make_async_copy(xbuf.at[par, j * SEG_ALIGN:(j + 1) * SEG_ALIGN], xs_hbm.at[d],
                                     sem_x.at[par])

    def zero_copy(e, j):
        d = pl.ds(pl.multiple_of(zdst_ref[e] + j * SEG_ALIGN, SEG_ALIGN), SEG_ALIGN)
        return pltpu.make_async_copy(zx, xs_hbm.at[d], sem_z.at[0])

    def for_zero_chunks(fn):
        for e in range(N_EXPERTS):
            def body(j, c, e=e):
                fn(zero_copy(e, j))
                return c
            lax.fori_loop(0, nz_ref[e], body, 0)

    def wait_tile(par):
        pltpu.make_async_copy(xbuf.at[par], xs_hbm.at[pl.ds(0, L_CAP)], sem_x.at[par]).wait()

    @pl.when(i == 0)
    def _():
        zx[...] = jnp.zeros(zx.shape, zx.dtype)
        for_zero_chunks(lambda cp: cp.start())

    @pl.when(i >= 2)
    def _():
        wait_tile(p)

    pos0 = row_ref[ROW_POS0:ROW_POS0 + 1, :].astype(jnp.int32)
    pos1 = row_ref[ROW_POS1:ROW_POS1 + 1, :].astype(jnp.int32)
    t = t_ref[...]
    side = _bf16(col_ref[...])
    rows = DISPATCH_ROWS
    for c in range(L_CAP // rows):
        r = lax.broadcasted_iota(jnp.int32, (rows, tm), 0) + c * rows
        perm = jnp.logical_or(r == pos0, r == pos1).astype(jnp.bfloat16)
        xbuf[p, c * rows:(c + 1) * rows, 0:D_MODEL] = _bf16(
            jnp.dot(perm, t, preferred_element_type=jnp.float32))
        xbuf[p, c * rows:(c + 1) * rows, D_MODEL:D_XS] = _bf16(
            jnp.dot(perm, side, preferred_element_type=jnp.float32))
        for j in range(c * rows // SEG_ALIGN, (c + 1) * rows // SEG_ALIGN):
            copy(i, p, j).start()

    @pl.when(i == n - 1)
    def _():
        @pl.when(n >= 2)
        def _():
            wait_tile(1 - p)
        wait_tile(p)
        for_zero_chunks(lambda cp: cp.wait())


def _dispatch(t, col, row, dst, nz, zdst):
    n_tok = t.shape[0]
    r_cap, _ = _moe_capacity(n_tok)
    tm = TM_PROJ
    grid_spec = pltpu.PrefetchScalarGridSpec(
        num_scalar_prefetch=3,
        grid=(n_tok // tm,),
        in_specs=[
            pl.BlockSpec((tm, D_MODEL), lambda i, *_: (i, 0)),
            pl.BlockSpec((tm, LANES), lambda i, *_: (i, 0)),
            pl.BlockSpec((SUBLANES, tm), lambda i, *_: (0, i)),
        ],
        out_specs=pl.BlockSpec(memory_space=pl.ANY),
        scratch_shapes=[
            pltpu.VMEM((2, L_CAP, D_XS), jnp.bfloat16),
            pltpu.VMEM((SEG_ALIGN, D_XS), jnp.bfloat16),
            pltpu.SemaphoreType.DMA((2,)),
            pltpu.SemaphoreType.DMA((1,)),
        ],
    )
    return pl.pallas_call(
        _dispatch_kernel,
        grid_spec=grid_spec,
        out_shape=jax.ShapeDtypeStruct((r_cap + DUMP_ROWS, D_XS), jnp.bfloat16),
        compiler_params=pltpu.CompilerParams(dimension_semantics=("arbitrary",),
                                             vmem_limit_bytes=VMEM_LIMIT),
        name="moe_dispatch",
    )(dst, nz, zdst, t, col, row)


def _moe_kernel(layer, blk_e_ref, nused_ref, nvalid_ref, next_e_ref, xs_ref, wg_hbm, wu_hbm, wd_hbm,
                ys_ref, wg_st, wu_st, wd_st, wg_bf, wu_bf, wd_bf, sem):
    b = pl.program_id(0)
    nv = nvalid_ref[b]
    e = blk_e_ref[b]
    new_expert = jnp.logical_or(b == 0, e != blk_e_ref[jnp.maximum(b - 1, 0)])

    def fetch(expert):
        return (pltpu.make_async_copy(wg_hbm.at[layer, expert], wg_st, sem.at[0]),
                pltpu.make_async_copy(wu_hbm.at[layer, expert], wu_st, sem.at[1]),
                pltpu.make_async_copy(wd_hbm.at[layer, expert], wd_st, sem.at[2]))

    @pl.when(b == 0)
    def _():
        for cp in fetch(e):
            cp.start()

    @pl.when(new_expert)
    def _():
        for cp in fetch(e):
            cp.wait()
        wg_bf[...] = _bf16(wg_st[...])
        wu_bf[...] = _bf16(wu_st[...])
        wd_bf[...] = _bf16(wd_st[...])

        @pl.when(next_e_ref[b] != e)
        def _():
            for cp in fetch(next_e_ref[b]):
                cp.start()

    def compute(rows):
        x = xs_ref[0:rows, 0:D_MODEL]
        y = None
        fw = D_FF_EXPERT // MOE_FF_SPLIT
        for c in range(MOE_FF_SPLIT):
            cols = slice(c * fw, (c + 1) * fw)
            a = jnp.dot(x, wg_bf[:, cols], preferred_element_type=jnp.float32)
            u = jnp.dot(x, wu_bf[:, cols], preferred_element_type=jnp.float32)
            h = a * jax.nn.sigmoid(a) * u
            yc = jnp.dot(_bf16(h), wd_bf[cols, :], preferred_element_type=jnp.float32)
            y = yc if y is None else y + yc
        sd = xs_ref[0:rows, D_MODEL:D_XS].astype(jnp.float32)
        e_blk = blk_e_ref[b].astype(jnp.float32)
        w0 = sd[:, COL_W0H:COL_W0H + 1] + sd[:, COL_W0H + 1:COL_W0H + 2] + sd[:, COL_W0H + 2:COL_W0H + 3]
        w1 = sd[:, COL_W1H:COL_W1H + 1] + sd[:, COL_W1H + 1:COL_W1H + 2] + sd[:, COL_W1H + 2:COL_W1H + 3]
        w = jnp.where(sd[:, COL_E0:COL_E0 + 1] == e_blk, w0, w1)
        ys_ref[0:rows, :] = _bf16(y * w)

    half = MOE_BM // 2

    @pl.when(nv > half)
    def _():
        compute(MOE_BM)

    @pl.when(jnp.logical_and(nv > 0, nv <= half))
    def _():
        compute(half)
        ys_ref[half:, :] = jnp.zeros((MOE_BM - half, D_MODEL), ys_ref.dtype)


def _moe(xs, blk_e, nused, nvalid, next_e, wg, wu, wd, layer):
    blk = lambda b, be, nu, nv, ne: (jnp.maximum(jnp.minimum(b, nu[0] - 1), 0), 0)
    r_cap = xs.shape[0] - DUMP_ROWS
    grid_spec = pltpu.PrefetchScalarGridSpec(
        num_scalar_prefetch=4,
        grid=(r_cap // MOE_BM,),
        in_specs=[
            pl.BlockSpec((MOE_BM, D_XS), blk),
            pl.BlockSpec(memory_space=pl.ANY),
            pl.BlockSpec(memory_space=pl.ANY),
            pl.BlockSpec(memory_space=pl.ANY),
        ],
        out_specs=pl.BlockSpec((MOE_BM, D_MODEL), blk),
        scratch_shapes=[
            pltpu.VMEM((D_MODEL, D_FF_EXPERT), jnp.float32),
            pltpu.VMEM((D_MODEL, D_FF_EXPERT), jnp.float32),
            pltpu.VMEM((D_FF_EXPERT, D_MODEL), jnp.float32),
            pltpu.VMEM((D_MODEL, D_FF_EXPERT), jnp.bfloat16),
            pltpu.VMEM((D_MODEL, D_FF_EXPERT), jnp.bfloat16),
            pltpu.VMEM((D_FF_EXPERT, D_MODEL), jnp.bfloat16),
            pltpu.SemaphoreType.DMA((3,)),
        ],
    )
    return pl.pallas_call(
        functools.partial(_moe_kernel, layer),
        grid_spec=grid_spec,
        out_shape=jax.ShapeDtypeStruct((r_cap, D_MODEL), jnp.bfloat16),
        compiler_params=pltpu.CompilerParams(dimension_semantics=("arbitrary",),
                                             vmem_limit_bytes=VMEM_LIMIT),
        name="moe_experts",
    )(blk_e, nused, nvalid, next_e, xs, wg, wu, wd)


def _combine_kernel(src_ref, x1_ref, col_ref, ys_hbm, out_ref, ybuf, sem):
    i = pl.program_id(0)
    n = pl.num_programs(0)
    p = i % 2
    tm = x1_ref.shape[0]

    def start_tile(tile, par):
        for j in range(N_CHUNK):
            s = pl.ds(pl.multiple_of(src_ref[tile, j], SEG_ALIGN), SEG_ALIGN)
            pltpu.make_async_copy(ys_hbm.at[s], ybuf.at[par, j * SEG_ALIGN:(j + 1) * SEG_ALIGN],
                                  sem.at[par]).start()

    @pl.when(i == 0)
    def _():
        start_tile(0, 0)

    start_tile(jnp.minimum(i + 1, n - 1), 1 - p)
    pltpu.make_async_copy(ys_hbm.at[pl.ds(0, L_CAP)], ybuf.at[p], sem.at[p]).wait()

    rows = COMBINE_ROWS
    for c in range(tm // rows):
        col = col_ref[c * rows:(c + 1) * rows, :]
        pos0 = col[:, ROW_POS0:ROW_POS0 + 1].astype(jnp.int32)
        pos1 = col[:, ROW_POS1:ROW_POS1 + 1].astype(jnp.int32)
        l = lax.broadcasted_iota(jnp.int32, (rows, L_CAP), 1)
        perm = jnp.logical_or(l == pos0, l == pos1).astype(jnp.bfloat16)
        y = jnp.dot(perm, ybuf[p], preferred_element_type=jnp.float32)
        out_ref[c * rows:(c + 1) * rows, :] = x1_ref[c * rows:(c + 1) * rows, :] + y

    @pl.when(i == n - 1)
    def _():
        pltpu.make_async_copy(ys_hbm.at[pl.ds(0, L_CAP)], ybuf.at[1 - p], sem.at[1 - p]).wait()


def _combine(x1, col, ys, src):
    n_tok = x1.shape[0]
    tm = TM_PROJ
    grid_spec = pltpu.PrefetchScalarGridSpec(
        num_scalar_prefetch=1,
        grid=(n_tok // tm,),
        in_specs=[
            pl.BlockSpec((tm, D_MODEL), lambda i, *_: (i, 0)),
            pl.BlockSpec((tm, LANES), lambda i, *_: (i, 0)),
            pl.BlockSpec(memory_space=pl.ANY),
        ],
        out_specs=pl.BlockSpec((tm, D_MODEL), lambda i, *_: (i, 0)),
        scratch_shapes=[
            pltpu.VMEM((2, L_CAP, D_MODEL), jnp.bfloat16),
            pltpu.SemaphoreType.DMA((2,)),
        ],
    )
    return pl.pallas_call(
        _combine_kernel,
        grid_spec=grid_spec,
        out_shape=jax.ShapeDtypeStruct((n_tok, D_MODEL), jnp.float32),
        compiler_params=pltpu.CompilerParams(dimension_semantics=("arbitrary",),
                                             vmem_limit_bytes=VMEM_LIMIT),
        name="moe_combine",
    )(src, x1, col, ys)


def _routing_tables(counts, n_blk_cap):
    r_cap = n_blk_cap * MOE_BM
    i32 = jnp.int32
    ca = ((counts + (SEG_ALIGN - 1)) // SEG_ALIGN) * SEG_ALIGN
    lend = jnp.cumsum(ca, axis=1)
    lstart = lend - ca
    tot = jnp.sum(ca, axis=0)
    region = ((tot + (MOE_BM - 1)) // MOE_BM) * MOE_BM
    rend = jnp.cumsum(region)
    base = rend - region
    gstart = base[None, :] + jnp.cumsum(ca, axis=0) - ca
    j16 = jnp.arange(N_CHUNK, dtype=i32) * SEG_ALIGN
    in_seg = jnp.logical_and(lstart[:, None, :] <= j16[None, :, None], j16[None, :, None] < lend[:, None, :])
    seg_row = (jnp.sum(jnp.where(in_seg, (gstart - lstart)[:, None, :], 0), axis=2) + j16[None, :]).astype(i32)
    used = j16[None, :] < lend[:, -1:]
    tile_par = (jnp.arange(counts.shape[0], dtype=i32) % 2)[:, None]
    park = r_cap + (tile_par * N_CHUNK + jnp.arange(N_CHUNK, dtype=i32)[None, :]) * SEG_ALIGN
    dst = jnp.where(used, seg_row, park).astype(i32)
    src = jnp.where(used, seg_row, 0).astype(i32)
    nused = (rend[-1] // MOE_BM).astype(i32)
    brow = jnp.minimum(jnp.arange(n_blk_cap, dtype=i32), nused - 1) * MOE_BM
    blk_e = jnp.minimum(jnp.sum(rend[None, :] <= brow[:, None], axis=1), N_EXPERTS - 1).astype(i32)
    zstart = (base + tot).astype(i32)
    nzc = ((region - tot) // SEG_ALIGN).astype(i32)
    blk_i = jnp.arange(n_blk_cap, dtype=i32)
    seg_end = jnp.sum(jnp.where(blk_e[:, None] == jnp.arange(N_EXPERTS)[None, :], (base + tot)[None, :], 0), axis=1)
    nvalid = jnp.where(blk_i < nused, jnp.clip(seg_end - blk_i * MOE_BM, 0, MOE_BM), 0).astype(i32)
    eid = jnp.arange(N_EXPERTS, dtype=i32)
    later = jnp.logical_and(eid[None, :] > eid[:, None], (region > 0)[None, :])
    nxt = jnp.min(jnp.where(later, eid[None, :], N_EXPERTS), axis=1)
    nxt = jnp.where(nxt < N_EXPERTS, nxt, eid)
    next_e = jnp.sum(jnp.where(blk_e[:, None] == eid[None, :], nxt[None, :], 0), axis=1).astype(i32)
    return dst, src, nzc, zstart, blk_e, nused[None], nvalid, next_e


def _t5_bucket_np(dist):
    max_exact = N_BUCKETS // 2
    d = np.maximum(dist, 1).astype(np.float32)
    large = max_exact + (np.log(d / max_exact) / math.log(MAX_DISTANCE / max_exact)
                         * (N_BUCKETS - max_exact)).astype(np.int32)
    large = np.minimum(large, N_BUCKETS - 1)
    return np.where(dist < max_exact, dist, large)


def _attention_bias(rel_bias):
    qi = np.arange(ATT_BLOCK)[:, None]
    kj = np.arange(2 * ATT_BLOCK)[None, :]
    dist = qi + ATT_BLOCK - kj
    in_window = (dist >= 0) & (dist < WINDOW)
    bucket = _t5_bucket_np(np.clip(dist, 0, WINDOW - 1))
    onehot = (bucket[None] == np.arange(N_BUCKETS)[:, None, None]).astype(np.float32)
    bias = jnp.einsum('nh,nqk->hqk', rel_bias.astype(jnp.float32), jnp.asarray(onehot),
                      precision=lax.Precision.HIGHEST)
    bias = jnp.where(jnp.asarray(in_window)[None], bias * LOG2E, NEG_BIG)
    no_prev = jnp.asarray(kj < ATT_BLOCK)[None]
    return jnp.stack([bias, jnp.where(no_prev, NEG_BIG, bias)])


def _block_diag_mean(width, block):
    idx = np.arange(width) // block
    return jnp.asarray((idx[:, None] == idx[None, :]).astype(np.float32) / block, dtype=jnp.bfloat16)


def kernel(x, rel_bias, norm_mix_g, w_in, q_norm_g, k_norm_g, attn_sink, conv_w, conv_b, gate_b,
           mlstm_norm_g, w_out, norm_ffn_g, w_router_group, b_router_group, w_router_expert,
           b_router_expert, w_gate, w_up, w_down):
    batch, seq_len, _ = x.shape
    n_tok = batch * seq_len
    assert seq_len % TM_PROJ == 0 and seq_len % TM_IN == 0 and seq_len % ML_STEP == 0
    f32 = jnp.float32
    bias = _attention_bias(rel_bias)
    bdq = _block_diag_mean(ATT_WIDTH, ATT_HEAD_DIM)
    bdk = _block_diag_mean(LANES, ATT_HEAD_DIM)
    x2 = x.reshape(n_tok, D_MODEL)
    for l in range(DEPTH):
        wgt = _bf16(w_in[l, :, O_G:N_IN].T)
        qg = (jnp.tile(q_norm_g[l].astype(f32), ATT_HEADS) * (ATT_HEAD_DIM ** -0.5 * LOG2E))[None, :]
        kg = jnp.tile(k_norm_g[l].astype(f32), ATT_KV_HEADS)[None, :]
        gbt = gate_b[l].astype(f32)[:, None]
        qn, kn, vd, qkm, vm, om, gates_t = _in_proj(
            x2, norm_mix_g[l][None, :], w_in, l, wgt, qg, kg, bdq, bdk,
            conv_w[l], conv_b[l][None, :], gbt, batch, seq_len)
        att = _attention(qn, kn, vd, bias, attn_sink[l].astype(f32) * LOG2E, batch, seq_len)
        r3 = lambda a: a.reshape(batch, seq_len, a.shape[-1])
        hm = _mlstm(r3(qkm), r3(vm), r3(om), gates_t, mlstm_norm_g[l][None, :], batch, seq_len)
        n_rt = 4 * SUBLANES
        wrt = jnp.pad(jnp.concatenate([w_router_expert[l], w_router_group[l]], axis=1).astype(f32).T,
                      ((0, n_rt - N_EXPERTS - N_GROUPS), (0, 0)))
        brt = jnp.pad(jnp.concatenate([b_router_expert[l], b_router_group[l]]).astype(f32),
                      (0, n_rt - N_EXPERTS - N_GROUPS))[:, None]
        wrt_hi = _bf16(wrt)
        wrt = jnp.concatenate([wrt_hi, _bf16(wrt - wrt_hi.astype(f32))], axis=0)
        x1, t, row, col, cnt = _out_proj_router(x2, att, hm.reshape(n_tok, ML_WIDTH), w_out, l,
                                                norm_ffn_g[l][None, :], wrt, brt)
        _, n_blk_cap = _moe_capacity(n_tok)
        dst, src, nz, zdst, blk_e, nused, nvalid, next_e = _routing_tables(cnt[:, :, 0], n_blk_cap)
        xs = _dispatch(t, col, row, dst, nz, zdst)
        ys = _moe(xs, blk_e, nused, nvalid, next_e, w_gate, w_up, w_down, l)
        x2 = _combine(x1, col, ys, src)
    return x2.reshape(batch, seq_len, D_MODEL)
```

```python
import functools
import math

import jax
import jax.numpy as jnp
import numpy as np
from jax import lax
from jax.experimental import pallas as pl
from jax.experimental.pallas import tpu as pltpu

D_MODEL = 1024
DEPTH = 2
ATT_HEADS = 8
ATT_KV_HEADS = 2
ATT_HEAD_DIM = 64
ATT_WIDTH = ATT_HEADS * ATT_HEAD_DIM
WINDOW = 128
ATT_BLOCK = 128
N_BUCKETS = 32
MAX_DISTANCE = 128
ML_HEADS = 4
ML_DQK = 64
ML_DV = 128
ML_WIDTH = ML_HEADS * ML_DV
ML_CHUNK = 64
CONV_K = 4
N_GROUPS = 4
EXPERTS_PER_GROUP = 4
N_EXPERTS = N_GROUPS * EXPERTS_PER_GROUP
D_FF_EXPERT = 512
EPS = 1e-6

LANES = 128
SUBLANES = 8
NEG_BIG = -1e30
LOG2E = math.log2(math.e)
VMEM_LIMIT = 48 * 1024 * 1024

O_Q = 0
O_K = O_Q + ATT_WIDTH
O_V = O_K + ATT_KV_HEADS * ATT_HEAD_DIM
O_QM = O_V + ATT_KV_HEADS * ATT_HEAD_DIM
O_KM = O_QM + ML_HEADS * ML_DQK
O_VM = O_KM + ML_HEADS * ML_DQK
O_OM = O_VM + ML_WIDTH
O_G = O_OM + ML_WIDTH
N_IN = O_G + 2 * ML_HEADS

ATT_STEP_BLOCKS = 4
TM_IN = 1024
IN_SUBBLOCKS = 2
TM_OUT = 1024
TM_PROJ = 512
ML_KCHUNK = 256
ML_STEP = ML_KCHUNK

SEG_ALIGN = 16
MOE_BM = 512
MOE_FF_SPLIT = 2
L_CAP = 2 * TM_PROJ + N_EXPERTS * SEG_ALIGN
N_CHUNK = L_CAP // SEG_ALIGN
D_XS = D_MODEL + LANES
DUMP_ROWS = 2 * N_CHUNK * SEG_ALIGN
DISPATCH_ROWS = 256
COMBINE_ROWS = 256
ROW_POS0, ROW_POS1, ROW_W0, ROW_W1, ROW_E0, ROW_E1 = 0, 1, 2, 3, 4, 5
COL_W0H, COL_W1H, COL_E0, COL_E1 = 8, 11, 14, 15


def _moe_capacity(n_tok):
    n_tiles = n_tok // TM_PROJ
    rows = 2 * n_tok + n_tiles * N_EXPERTS * (SEG_ALIGN - 1) + N_EXPERTS * (MOE_BM - SEG_ALIGN)
    n_blk = -(-rows // MOE_BM)
    return n_blk * MOE_BM, n_blk


def _bf16(a):
    return a.astype(jnp.bfloat16)


def _split3(a):
    hi = _bf16(a)
    r1 = a - hi.astype(jnp.float32)
    mid = _bf16(r1)
    return hi, mid, _bf16(r1 - mid.astype(jnp.float32))


def _log_sigmoid(z):
    return jnp.minimum(z, 0.0) - jnp.log(1.0 + jnp.exp(-jnp.abs(z)))


def _in_proj_kernel(tiles_per_seq, x_ref, g_ref, w_ref, wgt_ref, qg_ref, kg_ref, bdq_ref, bdk_ref,
                    cw_ref, cb_ref, gbt_ref,
                    qn_ref, kn_ref, vd_ref, qkm_ref, vm_ref, om_ref, gatet_ref,
                    w_bf, conv_scr):
    i = pl.program_id(0)
    tm = x_ref.shape[0]

    @pl.when(i == 0)
    def _():
        w_bf[...] = _bf16(w_ref[...])

    conv_w = 2 * ML_HEADS * ML_DQK

    @pl.when(i % tiles_per_seq == 0)
    def _():
        conv_scr[0:SUBLANES, :] = jnp.zeros((SUBLANES, conv_w), jnp.float32)

    @pl.when(i % tiles_per_seq != 0)
    def _():
        conv_scr[0:SUBLANES, :] = conv_scr[tm:tm + SUBLANES, :]

    sub = tm // IN_SUBBLOCKS
    for sb in range(IN_SUBBLOCKS):
        rs = slice(sb * sub, (sb + 1) * sub)
        x = x_ref[rs, :]
        hn = x * lax.rsqrt(jnp.mean(x * x, axis=-1, keepdims=True) + EPS) * g_ref[...]
        hb = _bf16(hn)

        def proj(c0, width):
            return jnp.dot(hb, w_bf[:, c0:c0 + width], preferred_element_type=jnp.float32)

        q = proj(O_Q, ATT_WIDTH)
        q_ms = jnp.dot(_bf16(q * q), bdq_ref[...], preferred_element_type=jnp.float32)
        qn_ref[rs, :] = _bf16(q * lax.rsqrt(q_ms + EPS) * qg_ref[...])
        kv = proj(O_K, 2 * LANES)
        k = kv[:, 0:LANES]
        v = kv[:, LANES:2 * LANES]
        k_ms = jnp.dot(_bf16(k * k), bdk_ref[...], preferred_element_type=jnp.float32)
        kn = k * lax.rsqrt(k_ms + EPS) * kg_ref[...]
        low = lax.broadcasted_iota(jnp.int32, kn.shape, 1) < ATT_HEAD_DIM

        def dup_heads(a):
            swapped = pltpu.roll(a, ATT_HEAD_DIM, axis=1)
            return jnp.concatenate([jnp.where(low, a, swapped), jnp.where(low, swapped, a)], axis=1)

        kn_ref[rs, :] = _bf16(dup_heads(kn))
        vd_ref[rs, :] = _bf16(dup_heads(v))

        qk = proj(O_QM, conv_w)
        base = SUBLANES + sb * sub
        conv_scr[base:base + sub, :] = qk
        y = qk * cw_ref[CONV_K - 1:CONV_K, :] + cb_ref[...]
        for j in range(CONV_K - 1):
            off = base - (CONV_K - 1) + j
            y = y + conv_scr[off:off + sub, :] * cw_ref[j:j + 1, :]
        y = y * jax.nn.sigmoid(y)
        lane = lax.broadcasted_iota(jnp.int32, y.shape, 1)
        y = jnp.where(lane >= ML_HEADS * ML_DQK, y * (ML_DQK ** -0.5), y)
        qkm_ref[rs, :] = _bf16(y)

        vm_ref[rs, :] = _bf16(proj(O_VM, ML_WIDTH))
        om_ref[rs, :] = _bf16(jax.nn.sigmoid(proj(O_OM, ML_WIDTH)))

        gt = lax.dot_general(wgt_ref[...], hb, (((1,), (1,)), ((), ())),
                             preferred_element_type=jnp.float32) + gbt_ref[...]
        grow = lax.broadcasted_iota(jnp.int32, gt.shape, 0)
        gatet_ref[:, rs] = jnp.where(grow >= ML_HEADS, _log_sigmoid(gt), gt)


def _in_proj(x2, g, w_in, layer, wgt, qg, kg, bdq, bdk, cw, cb, gbt, batch, seq_len):
    t = x2.shape[0]
    tm = TM_IN
    n = t // tm
    tps = seq_len // tm
    row = lambda i: (i, 0)
    fix = lambda i: (0, 0)
    kv_w = 2 * ATT_KV_HEADS * ATT_HEAD_DIM
    out_shapes = (
        jax.ShapeDtypeStruct((t, ATT_WIDTH), jnp.bfloat16),
        jax.ShapeDtypeStruct((t, kv_w), jnp.bfloat16),
        jax.ShapeDtypeStruct((t, kv_w), jnp.bfloat16),
        jax.ShapeDtypeStruct((t, 2 * ML_HEADS * ML_DQK), jnp.bfloat16),
        jax.ShapeDtypeStruct((t, ML_WIDTH), jnp.bfloat16),
        jax.ShapeDtypeStruct((t, ML_WIDTH), jnp.bfloat16),
        jax.ShapeDtypeStruct((batch, SUBLANES, seq_len), jnp.float32),
    )
    in_specs = [
        pl.BlockSpec((tm, D_MODEL), row),
        pl.BlockSpec((1, D_MODEL), fix),
        pl.BlockSpec((None, D_MODEL, N_IN), lambda i: (layer, 0, 0), pipeline_mode=pl.Buffered(1)),
        pl.BlockSpec((SUBLANES, D_MODEL), fix),
        pl.BlockSpec((1, ATT_WIDTH), fix),
        pl.BlockSpec((1, LANES), fix),
        pl.BlockSpec((ATT_WIDTH, ATT_WIDTH), fix),
        pl.BlockSpec((LANES, LANES), fix),
        pl.BlockSpec((CONV_K, 2 * ML_HEADS * ML_DQK), fix),
        pl.BlockSpec((1, 2 * ML_HEADS * ML_DQK), fix),
        pl.BlockSpec((SUBLANES, 1), fix),
    ]
    out_specs = [
        pl.BlockSpec((tm, ATT_WIDTH), row),
        pl.BlockSpec((tm, kv_w), row),
        pl.BlockSpec((tm, kv_w), row),
        pl.BlockSpec((tm, 2 * ML_HEADS * ML_DQK), row),
        pl.BlockSpec((tm, ML_WIDTH), row),
        pl.BlockSpec((tm, ML_WIDTH), row),
        pl.BlockSpec((None, SUBLANES, tm), lambda i: (i // tps, 0, i % tps)),
    ]
    return pl.pallas_call(
        functools.partial(_in_proj_kernel, seq_len // tm),
        grid=(n,),
        in_specs=in_specs,
        out_specs=out_specs,
        out_shape=out_shapes,
        scratch_shapes=[pltpu.VMEM((D_MODEL, N_IN), jnp.bfloat16),
                        pltpu.VMEM((tm + 2 * SUBLANES, 2 * ML_HEADS * ML_DQK), jnp.float32)],
        compiler_params=pltpu.CompilerParams(dimension_semantics=("arbitrary",),
                                             vmem_limit_bytes=VMEM_LIMIT),
        name="in_proj",
    )(x2, g, w_in, wgt, qg, kg, bdq, bdk, cw, cb, gbt)


def _attn_kernel(sink_ref, q_ref, kp_ref, kc_ref, vp_ref, vc_ref, bias_ref, o_ref):
    i = pl.program_id(1)
    blk = ATT_BLOCK
    lane = lax.broadcasted_iota(jnp.int32, (blk, LANES), 1)
    low = lane < ATT_HEAD_DIM
    group = ATT_HEADS // ATT_KV_HEADS
    ones = jnp.ones((2 * blk, LANES), jnp.bfloat16)
    for j in range(ATT_STEP_BLOCKS):
        rows = slice(j * blk, (j + 1) * blk)
        first = (i == 0).astype(jnp.int32) if j == 0 else 0
        for kv in range(ATT_KV_HEADS):
            lanes = slice(kv * LANES, (kv + 1) * LANES)
            if j == 0:
                kcat = jnp.concatenate([kp_ref[:, lanes], kc_ref[0:blk, lanes]], axis=0)
                vrows = jnp.concatenate([vp_ref[:, lanes], vc_ref[0:blk, lanes]], axis=0)
            else:
                kcat = kc_ref[(j - 1) * blk:(j + 1) * blk, lanes]
                vrows = vc_ref[(j - 1) * blk:(j + 1) * blk, lanes]
            vcat = jnp.concatenate([vrows, ones], axis=1)
            heads = range(kv * group, (kv + 1) * group)
            qs = []
            for h in heads:
                qp = q_ref[rows, (h // 2) * LANES:(h // 2 + 1) * LANES]
                qs.append(jnp.where(low if h % 2 == 0 else jnp.logical_not(low), qp, jnp.zeros_like(qp)))
            s_all = lax.dot_general(jnp.concatenate(qs, axis=0), kcat, (((1,), (1,)), ((), ())),
                                    preferred_element_type=jnp.float32)
            ps, ms = [], []
            for n, h in enumerate(heads):
                logits = s_all[n * blk:(n + 1) * blk, :] + bias_ref[first, h]
                row_max = jnp.broadcast_to(jnp.max(logits, axis=-1, keepdims=True), (blk, LANES))
                m = jnp.maximum(row_max, sink_ref[h])
                ps.append(_bf16(jnp.exp2(logits - jnp.concatenate([m, m], axis=1))))
                ms.append(m)
            o_all = jnp.dot(jnp.concatenate(ps, axis=0), vcat, preferred_element_type=jnp.float32)
            outs = []
            for n, h in enumerate(heads):
                o = o_all[n * blk:(n + 1) * blk, :]
                den = o[:, LANES:2 * LANES] + jnp.exp2(sink_ref[h] - ms[n])
                outs.append(o[:, 0:LANES] / den)
            for n in range(0, group, 2):
                pair = (kv * group + n) // 2
                o_ref[rows, pair * LANES:(pair + 1) * LANES] = _bf16(
                    jnp.where(low, outs[n], outs[n + 1]))


def _attention(qn, kn, vd, bias, sink, batch, seq_len):
    step = ATT_STEP_BLOCKS * ATT_BLOCK
    ns = seq_len // step
    cur = lambda b, i, s: (b * ns + i, 0)
    prev = lambda b, i, s: ((b * ns + i) * ATT_STEP_BLOCKS - jnp.minimum(i, 1), 0)
    grid_spec = pltpu.PrefetchScalarGridSpec(
        num_scalar_prefetch=1,
        grid=(batch, ns),
        in_specs=[
            pl.BlockSpec((step, ATT_WIDTH), cur),
            pl.BlockSpec((ATT_BLOCK, 2 * LANES), prev),
            pl.BlockSpec((step, 2 * LANES), cur),
            pl.BlockSpec((ATT_BLOCK, 2 * LANES), prev),
            pl.BlockSpec((step, 2 * LANES), cur),
            pl.BlockSpec((2, ATT_HEADS, ATT_BLOCK, 2 * ATT_BLOCK), lambda b, i, s: (0, 0, 0, 0)),
        ],
        out_specs=pl.BlockSpec((step, ATT_WIDTH), cur),
    )
    return pl.pallas_call(
        _attn_kernel,
        grid_spec=grid_spec,
        out_shape=jax.ShapeDtypeStruct((batch * seq_len, ATT_WIDTH), jnp.bfloat16),
        compiler_params=pltpu.CompilerParams(dimension_semantics=("arbitrary", "arbitrary"),
                                             vmem_limit_bytes=VMEM_LIMIT),
        name="swa_attention",
    )(sink, qn, kn, kn, vd, vd, bias)


def _mlstm_kernel(qk_ref, v_ref, o_ref, gt_ref, ng_ref, out_ref, c_scr, m_scr):
    step = pl.program_id(0)
    batch = qk_ref.shape[0]
    L = ML_KCHUNK
    pairs = ML_HEADS // 2

    @pl.when(step == 0)
    def _():
        c_scr[...] = jnp.zeros(c_scr.shape, jnp.float32)
        m_scr[...] = jnp.zeros(m_scr.shape, jnp.float32)

    r_i = lax.broadcasted_iota(jnp.int32, (L, L), 0)
    c_i = lax.broadcasted_iota(jnp.int32, (L, L), 1)
    causal = c_i <= r_i
    tril_bf = causal.astype(jnp.bfloat16)
    triu_bf = (r_i <= c_i).astype(jnp.bfloat16)
    mean_dv = jnp.full((ML_DV, ML_DV), 1.0 / ML_DV, jnp.bfloat16)
    lane = lax.broadcasted_iota(jnp.int32, (L, LANES), 1)
    low = lane < ML_DQK
    ones_dv = jnp.ones((L, ML_DV), jnp.bfloat16)
    row2 = lax.broadcasted_iota(jnp.int32, (2 * ML_DQK, 2 * ML_DV), 0)

    def twice(a):
        return jnp.concatenate([a, a], axis=1)

    m_state = [m_scr[k] for k in range(batch * ML_HEADS)]
    c_state = [c_scr[k] for k in range(batch * pairs)]

    for ch in range(ML_STEP // L):
        r0 = ch * L
        for b in range(batch):
            gt = gt_ref[b, :, r0:r0 + L]
            g = jnp.concatenate([gt, jnp.zeros((LANES - SUBLANES, L), jnp.float32)], axis=0).T
            bcols = sum(jnp.dot(tril_bf, part, preferred_element_type=jnp.float32) for part in _split3(g))
            brows = sum(jnp.dot(part, triu_bf, preferred_element_type=jnp.float32) for part in _split3(gt))
            for pair in range(pairs):
                sidx = b * pairs + pair
                qp = qk_ref[b, r0:r0 + L, pair * LANES:(pair + 1) * LANES]
                kp = qk_ref[b, r0:r0 + L, (pairs + pair) * LANES:(pairs + pair + 1) * LANES]
                c_pair = c_state[sidx]
                c_bf = _bf16(c_pair)
                new_c = None
                decays = []
                for sub in range(2):
                    h = 2 * pair + sub
                    sel = low if sub == 0 else jnp.logical_not(low)
                    m_prev = m_state[b * ML_HEADS + h]
                    bc = jnp.broadcast_to(bcols[:, ML_HEADS + h:ML_HEADS + h + 1], (L, LANES))
                    lic = jnp.broadcast_to(g[:, h:h + 1], (L, LANES))
                    br = brows[ML_HEADS + h:ML_HEADS + h + 1, :]
                    lir = gt[h:h + 1, :]
                    log_d = jnp.where(causal, twice(bc) - (br - lir), NEG_BIG)
                    m_inter = bc + m_prev
                    row_max = jnp.broadcast_to(jnp.max(log_d, axis=-1, keepdims=True), (L, LANES))
                    m_row = jnp.maximum(m_inter, row_max)
                    d = jnp.exp(log_d - twice(m_row))
                    inter = jnp.exp(m_inter - m_row)
                    qm = jnp.where(sel, qp, jnp.zeros_like(qp))
                    s = lax.dot_general(qm, kp, (((1,), (1,)), ((), ())),
                                        preferred_element_type=jnp.float32) * d
                    v_ext = jnp.concatenate([v_ref[b, r0:r0 + L, h * ML_DV:(h + 1) * ML_DV], ones_dv],
                                            axis=-1)
                    num = twice(inter) * jnp.dot(qm, c_bf, preferred_element_type=jnp.float32) \
                        + jnp.dot(_bf16(s), v_ext, preferred_element_type=jnp.float32)
                    den = num[:, ML_DV:2 * ML_DV]
                    hval = num[:, 0:ML_DV] / jnp.maximum(jnp.abs(den), jnp.exp(-m_row))
                    h_ms = jnp.dot(_bf16(hval * hval), mean_dv, preferred_element_type=jnp.float32)
                    hn = hval * lax.rsqrt(h_ms + EPS)
                    hn = hn * ng_ref[:, h * ML_DV:(h + 1) * ML_DV]
                    out_ref[b, r0:r0 + L, h * ML_DV:(h + 1) * ML_DV] = _bf16(
                        hn * o_ref[b, r0:r0 + L, h * ML_DV:(h + 1) * ML_DV].astype(jnp.float32))
                    b_last = bc[L - 1:L, :]
                    log_w = b_last - bc + lic
                    m_next = jnp.maximum(b_last + m_prev, jnp.max(log_w, axis=0, keepdims=True))
                    w = jnp.exp(log_w - m_next)
                    decays.append(jnp.exp(b_last + m_prev - m_next))
                    m_state[b * ML_HEADS + h] = m_next
                    kw = _bf16(jnp.where(sel, kp.astype(jnp.float32) * w, 0.0))
                    upd = lax.dot_general(kw, v_ext, (((0,), (0,)), ((), ())),
                                          preferred_element_type=jnp.float32)
                    new_c = upd if new_c is None else new_c + upd
                decay_rows = jnp.where(row2 < ML_DQK, twice(decays[0]), twice(decays[1]))
                c_state[sidx] = decay_rows * c_pair + new_c

    for k in range(batch * ML_HEADS):
        m_scr[k] = m_state[k]
    for k in range(batch * pairs):
        c_scr[k] = c_state[k]


def _mlstm(qkm, vm, om, gates_t, ng, batch, seq_len):
    n_steps = seq_len // ML_STEP
    blk = lambda c: (0, c, 0)
    return pl.pallas_call(
        _mlstm_kernel,
        grid=(n_steps,),
        in_specs=[
            pl.BlockSpec((batch, ML_STEP, 2 * ML_HEADS * ML_DQK), blk),
            pl.BlockSpec((batch, ML_STEP, ML_WIDTH), blk),
            pl.BlockSpec((batch, ML_STEP, ML_WIDTH), blk),
            pl.BlockSpec((batch, SUBLANES, ML_STEP), lambda c: (0, 0, c)),
            pl.BlockSpec((1, ML_WIDTH), lambda c: (0, 0)),
        ],
        out_specs=pl.BlockSpec((batch, ML_STEP, ML_WIDTH), blk),
        out_shape=jax.ShapeDtypeStruct((batch, seq_len, ML_WIDTH), jnp.bfloat16),
        scratch_shapes=[
            pltpu.VMEM((batch * ML_HEADS // 2, 2 * ML_DQK, 2 * ML_DV), jnp.float32),
            pltpu.VMEM((batch * ML_HEADS, 1, LANES), jnp.float32),
        ],
        compiler_params=pltpu.CompilerParams(dimension_semantics=("arbitrary",),
                                             vmem_limit_bytes=VMEM_LIMIT),
        name="mlstm_scan",
    )(qkm, vm, om, gates_t, ng)


def _out_proj_router_kernel(x_ref, att_ref, hm_ref, wo_ref, g_ref, wrt_ref, brt_ref,
                            x1_ref, t_ref, row_ref, col_ref, cnt_ref, wo_bf):
    tm = x_ref.shape[0]
    tr = TM_PROJ
    hi = lax.Precision.HIGHEST

    @pl.when(pl.program_id(0) == 0)
    def _():
        wo_bf[...] = _bf16(wo_ref[...])

    for sb in range(tm // TM_PROJ):
        rs = slice(sb * TM_PROJ, (sb + 1) * TM_PROJ)
        x1 = x_ref[rs, :] \
            + jnp.dot(att_ref[rs, :], wo_bf[0:ATT_WIDTH, :], preferred_element_type=jnp.float32) \
            + jnp.dot(hm_ref[rs, :], wo_bf[ATT_WIDTH:, :], preferred_element_type=jnp.float32)
        x1_ref[rs, :] = x1
        tn = x1 * lax.rsqrt(jnp.mean(x1 * x1, axis=-1, keepdims=True) + EPS) * g_ref[...]
        tn_hi = _bf16(tn)
        t_ref[rs, :] = tn_hi
        tn_lo = _bf16(tn - tn_hi.astype(jnp.float32))
        nt = (((1,), (1,)), ((), ()))
        n_rt = wrt_ref.shape[0] // 2
        p_hi = lax.dot_general(wrt_ref[...], tn_hi, nt, preferred_element_type=jnp.float32)
        p_lo = lax.dot_general(wrt_ref[0:n_rt, :], tn_lo, nt, preferred_element_type=jnp.float32)
        logits = p_hi[0:n_rt, :] + p_hi[n_rt:, :] + p_lo + brt_ref[...]
        el_all = logits[0:N_EXPERTS, :]
        gl = logits[N_EXPERTS:N_EXPERTS + SUBLANES, :]
        grow = lax.broadcasted_iota(jnp.int32, gl.shape, 0).astype(jnp.float32)
        gl = jnp.where(grow < N_GROUPS, gl, NEG_BIG)
        gmax = jnp.max(gl, axis=0, keepdims=True)
        grp = jnp.min(jnp.where(gl == gmax, grow, float(N_GROUPS)), axis=0, keepdims=True)
        p_grp = 1.0 / jnp.sum(jnp.exp(gl - gmax), axis=0, keepdims=True)
        erow = lax.broadcasted_iota(jnp.int32, el_all.shape, 0).astype(jnp.float32)
        egrp = jnp.floor(erow * (1.0 / EXPERTS_PER_GROUP))
        el = jnp.where(egrp == grp, el_all, NEG_BIG)
        e1 = jnp.max(el, axis=0, keepdims=True)
        i1 = jnp.min(jnp.where(el == e1, erow, float(N_EXPERTS)), axis=0, keepdims=True)
        el2 = jnp.where(erow == i1, NEG_BIG, el)
        e2 = jnp.max(el2, axis=0, keepdims=True)
        i2 = jnp.min(jnp.where(el2 == e2, erow, float(N_EXPERTS)), axis=0, keepdims=True)
        z2 = jnp.exp(e2 - e1)
        w1 = p_grp / (1.0 + z2)
        w2 = p_grp * z2 / (1.0 + z2)
        sel1 = erow == i1
        sel2 = erow == i2
        onehot = jnp.logical_or(sel1, sel2)
        t_r = lax.broadcasted_iota(jnp.int32, (tr, tr), 0)
        t_c = lax.broadcasted_iota(jnp.int32, (tr, tr), 1)
        before = (t_r < t_c).astype(jnp.bfloat16)
        rank = jnp.dot(onehot.astype(jnp.bfloat16), before, preferred_element_type=jnp.float32)
        cnt = jnp.sum(onehot.astype(jnp.float32), axis=1, keepdims=True)
        cnt_al = jnp.floor((cnt + (SEG_ALIGN - 1)) * (1.0 / SEG_ALIGN)) * SEG_ALIGN
        e_r = lax.broadcasted_iota(jnp.int32, (N_EXPERTS, N_EXPERTS), 0)
        e_c = lax.broadcasted_iota(jnp.int32, (N_EXPERTS, N_EXPERTS), 1)
        lstart = jnp.dot((e_c < e_r).astype(jnp.float32), jnp.broadcast_to(cnt_al, (N_EXPERTS, LANES)),
                         precision=hi, preferred_element_type=jnp.float32)[:, 0:1]
        slot = lstart + rank
        pos1 = jnp.sum(jnp.where(sel1, slot, 0.0), axis=0, keepdims=True)
        pos2 = jnp.sum(jnp.where(sel2, slot, 0.0), axis=0, keepdims=True)
        r8 = lax.broadcasted_iota(jnp.int32, (SUBLANES, tr), 0)
        info = jnp.where(r8 == ROW_POS0, pos1, jnp.where(r8 == ROW_POS1, pos2, jnp.where(
            r8 == ROW_W0, w1, jnp.where(r8 == ROW_W1, w2, jnp.where(
                r8 == ROW_E0, i1, jnp.where(r8 == ROW_E1, i2, 0.0))))))
        row_ref[:, rs] = info

        def split3(w):
            h = _bf16(w).astype(jnp.float32)
            m = _bf16(w - h).astype(jnp.float32)
            return h, m, _bf16(w - h - m).astype(jnp.float32)

        w1h, w1m, w1l = split3(w1)
        w2h, w2m, w2l = split3(w2)
        parts = jnp.where(r8 == 0, w1h, jnp.where(r8 == 1, w1m, jnp.where(r8 == 2, w1l, jnp.where(
            r8 == 3, w2h, jnp.where(r8 == 4, w2m, jnp.where(r8 == 5, w2l, jnp.where(r8 == 6, i1, i2)))))))
        col_ref[rs, :] = jnp.concatenate(
            [info, parts, jnp.zeros((LANES - 2 * SUBLANES, tr), jnp.float32)], axis=0).T
        cnt_ref[sb] = jnp.broadcast_to(cnt, (N_EXPERTS, LANES)).astype(jnp.int32)


def _out_proj_router(x2, att, hm, wo, layer, g, wrt, brt):
    t = x2.shape[0]
    tm = TM_OUT
    row = lambda i: (i, 0)
    fix = lambda i: (0, 0)
    return pl.pallas_call(
        _out_proj_router_kernel,
        grid=(t // tm,),
        in_specs=[
            pl.BlockSpec((tm, D_MODEL), row),
            pl.BlockSpec((tm, ATT_WIDTH), row),
            pl.BlockSpec((tm, ML_WIDTH), row),
            pl.BlockSpec((None, D_MODEL, D_MODEL), lambda i: (layer, 0, 0), pipeline_mode=pl.Buffered(1)),
            pl.BlockSpec((1, D_MODEL), fix),
            pl.BlockSpec((8 * SUBLANES, D_MODEL), fix),
            pl.BlockSpec((4 * SUBLANES, 1), fix),
        ],
        out_specs=[
            pl.BlockSpec((tm, D_MODEL), row),
            pl.BlockSpec((tm, D_MODEL), row),
            pl.BlockSpec((SUBLANES, tm), lambda i: (0, i)),
            pl.BlockSpec((tm, LANES), row),
            pl.BlockSpec((tm // TM_PROJ, N_EXPERTS, LANES), lambda i: (i, 0, 0)),
        ],
        out_shape=(
            jax.ShapeDtypeStruct((t, D_MODEL), jnp.float32),
            jax.ShapeDtypeStruct((t, D_MODEL), jnp.bfloat16),
            jax.ShapeDtypeStruct((SUBLANES, t), jnp.float32),
            jax.ShapeDtypeStruct((t, LANES), jnp.float32),
            jax.ShapeDtypeStruct((t // TM_PROJ, N_EXPERTS, LANES), jnp.int32),
        ),
        scratch_shapes=[pltpu.VMEM((D_MODEL, D_MODEL), jnp.bfloat16)],
        compiler_params=pltpu.CompilerParams(dimension_semantics=("arbitrary",),
                                             vmem_limit_bytes=VMEM_LIMIT),
        name="out_proj_router",
    )(x2, att, hm, wo, g, wrt, brt)


def _dispatch_kernel(dst_ref, nz_ref, zdst_ref, t_ref, col_ref, row_ref, xs_hbm,
                     xbuf, zx, sem_x, sem_z):
    i = pl.program_id(0)
    n = pl.num_programs(0)
    p = i % 2
    tm = t_ref.shape[0]

    def copy(tile, par, j):
        d = pl.ds(pl.multiple_of(dst_ref[tile, j], SEG_ALIGN), SEG_ALIGN)
        return pltpu.make_async_copy(xbuf.at[par, j * SEG_ALIGN:(j + 1) * SEG_ALIGN], xs_hbm.at[d],
                                     sem_x.at[par])

    def zero_copy(e, j):
        d = pl.ds(pl.multiple_of(zdst_ref[e] + j * SEG_ALIGN, SEG_ALIGN), SEG_ALIGN)
        return pltpu.make_async_copy(zx, xs_hbm.at[d], sem_z.at[0])

    def for_zero_chunks(fn):
        for e in range(N_EXPERTS):
            def body(j, c, e=e):
                fn(zero_copy(e, j))
                return c
            lax.fori_loop(0, nz_ref[e], body, 0)

    def wait_tile(par):
        pltpu.make_async_copy(xbuf.at[par], xs_hbm.at[pl.ds(0, L_CAP)], sem_x.at[par]).wait()

    @pl.when(i == 0)
    def _():
        zx[...] = jnp.zeros(zx.shape, zx.dtype)
        for_zero_chunks(lambda cp: cp.start())

    @pl.when(i >= 2)
    def _():
        wait_tile(p)

    pos0 = row_ref[ROW_POS0:ROW_POS0 + 1, :].astype(jnp.int32)
    pos1 = row_ref[ROW_POS1:ROW_POS1 + 1, :].astype(jnp.int32)
    t = t_ref[...]
    side = _bf16(col_ref[...])
    rows = DISPATCH_ROWS
    for c in range(L_CAP // rows):
        r = lax.broadcasted_iota(jnp.int32, (rows, tm), 0) + c * rows
        perm = jnp.logical_or(r == pos0, r == pos1).astype(jnp.bfloat16)
        xbuf[p, c * rows:(c + 1) * rows, 0:D_MODEL] = _bf16(
            jnp.dot(perm, t, preferred_element_type=jnp.float32))
        xbuf[p, c * rows:(c + 1) * rows, D_MODEL:D_XS] = _bf16(
            jnp.dot(perm, side, preferred_element_type=jnp.float32))
        for j in range(c * rows // SEG_ALIGN, (c + 1) * rows // SEG_ALIGN):
            copy(i, p, j).start()

    @pl.when(i == n - 1)
    def _():
        @pl.when(n >= 2)
        def _():
            wait_tile(1 - p)
        wait_tile(p)
        for_zero_chunks(lambda cp: cp.wait())


def _dispatch(t, col, row, dst, nz, zdst):
    n_tok = t.shape[0]
    r_cap, _ = _moe_capacity(n_tok)
    tm = TM_PROJ
    grid_spec = pltpu.PrefetchScalarGridSpec(
        num_scalar_prefetch=3,
        grid=(n_tok // tm,),
        in_specs=[
            pl.BlockSpec((tm, D_MODEL), lambda i, *_: (i, 0)),
            pl.BlockSpec((tm, LANES), lambda i, *_: (i, 0)),
            pl.BlockSpec((SUBLANES, tm), lambda i, *_: (0, i)),
        ],
        out_specs=pl.BlockSpec(memory_space=pl.ANY),
        scratch_shapes=[
            pltpu.VMEM((2, L_CAP, D_XS), jnp.bfloat16),
            pltpu.VMEM((SEG_ALIGN, D_XS), jnp.bfloat16),
            pltpu.SemaphoreType.DMA((2,)),
            pltpu.SemaphoreType.DMA((1,)),
        ],
    )
    return pl.pallas_call(
        _dispatch_kernel,
        grid_spec=grid_spec,
        out_shape=jax.ShapeDtypeStruct((r_cap + DUMP_ROWS, D_XS), jnp.bfloat16),
        compiler_params=pltpu.CompilerParams(dimension_semantics=("arbitrary",),
                                             vmem_limit_bytes=VMEM_LIMIT),
        name="moe_dispatch",
    )(dst, nz, zdst, t, col, row)


def _moe_kernel(layer, blk_e_ref, nused_ref, nvalid_ref, next_e_ref, xs_ref, wg_hbm, wu_hbm, wd_hbm,
                ys_ref, wg_st, wu_st, wd_st, wg_bf, wu_bf, wd_bf, sem):
    b = pl.program_id(0)
    nv = nvalid_ref[b]
    e = blk_e_ref[b]
    new_expert = jnp.logical_or(b == 0, e != blk_e_ref[jnp.maximum(b - 1, 0)])

    def fetch(expert):
        return (pltpu.make_async_copy(wg_hbm.at[layer, expert], wg_st, sem.at[0]),
                pltpu.make_async_copy(wu_hbm.at[layer, expert], wu_st, sem.at[1]),
                pltpu.make_async_copy(wd_hbm.at[layer, expert], wd_st, sem.at[2]))

    @pl.when(b == 0)
    def _():
        for cp in fetch(e):
            cp.start()

    @pl.when(new_expert)
    def _():
        for cp in fetch(e):
            cp.wait()
        wg_bf[...] = _bf16(wg_st[...])
        wu_bf[...] = _bf16(wu_st[...])
        wd_bf[...] = _bf16(wd_st[...])

        @pl.when(next_e_ref[b] != e)
        def _():
            for cp in fetch(next_e_ref[b]):
                cp.start()

    def compute(rows):
        x = xs_ref[0:rows, 0:D_MODEL]
        y = None
        fw = D_FF_EXPERT // MOE_FF_SPLIT
        for c in range(MOE_FF_SPLIT):
            cols = slice(c * fw, (c + 1) * fw)
            a = jnp.dot(x, wg_bf[:, cols], preferred_element_type=jnp.float32)
            u = jnp.dot(x, wu_bf[:, cols], preferred_element_type=jnp.float32)
            h = a * jax.nn.sigmoid(a) * u
            yc = jnp.dot(_bf16(h), wd_bf[cols, :], preferred_element_type=jnp.float32)
            y = yc if y is None else y + yc
        sd = xs_ref[0:rows, D_MODEL:D_XS].astype(jnp.float32)
        e_blk = blk_e_ref[b].astype(jnp.float32)
        w0 = sd[:, COL_W0H:COL_W0H + 1] + sd[:, COL_W0H + 1:COL_W0H + 2] + sd[:, COL_W0H + 2:COL_W0H + 3]
        w1 = sd[:, COL_W1H:COL_W1H + 1] + sd[:, COL_W1H + 1:COL_W1H + 2] + sd[:, COL_W1H + 2:COL_W1H + 3]
        w = jnp.where(sd[:, COL_E0:COL_E0 + 1] == e_blk, w0, w1)
        ys_ref[0:rows, :] = _bf16(y * w)

    half = MOE_BM // 2

    @pl.when(nv > half)
    def _():
        compute(MOE_BM)

    @pl.when(jnp.logical_and(nv > 0, nv <= half))
    def _():
        compute(half)
        ys_ref[half:, :] = jnp.zeros((MOE_BM - half, D_MODEL), ys_ref.dtype)


def _moe(xs, blk_e, nused, nvalid, next_e, wg, wu, wd, layer):
    blk = lambda b, be, nu, nv, ne: (jnp.maximum(jnp.minimum(b, nu[0] - 1), 0), 0)
    r_cap = xs.shape[0] - DUMP_ROWS
    grid_spec = pltpu.PrefetchScalarGridSpec(
        num_scalar_prefetch=4,
        grid=(r_cap // MOE_BM,),
        in_specs=[
            pl.BlockSpec((MOE_BM, D_XS), blk),
            pl.BlockSpec(memory_space=pl.ANY),
            pl.BlockSpec(memory_space=pl.ANY),
            pl.BlockSpec(memory_space=pl.ANY),
        ],
        out_specs=pl.BlockSpec((MOE_BM, D_MODEL), blk),
        scratch_shapes=[
            pltpu.VMEM((D_MODEL, D_FF_EXPERT), jnp.float32),
            pltpu.VMEM((D_MODEL, D_FF_EXPERT), jnp.float32),
            pltpu.VMEM((D_FF_EXPERT, D_MODEL), jnp.float32),
            pltpu.VMEM((D_MODEL, D_FF_EXPERT), jnp.bfloat16),
            pltpu.VMEM((D_MODEL, D_FF_EXPERT), jnp.bfloat16),
            pltpu.VMEM((D_FF_EXPERT, D_MODEL), jnp.bfloat16),
            pltpu.SemaphoreType.DMA((3,)),
        ],
    )
    return pl.pallas_call(
        functools.partial(_moe_kernel, layer),
        grid_spec=grid_spec,
        out_shape=jax.ShapeDtypeStruct((r_cap, D_MODEL), jnp.bfloat16),
        compiler_params=pltpu.CompilerParams(dimension_semantics=("arbitrary",),
                                             vmem_limit_bytes=VMEM_LIMIT),
        name="moe_experts",
    )(blk_e, nused, nvalid, next_e, xs, wg, wu, wd)


def _combine_kernel(src_ref, x1_ref, col_ref, ys_hbm, out_ref, ybuf, sem):
    i = pl.program_id(0)
    n = pl.num_programs(0)
    p = i % 2
    tm = x1_ref.shape[0]

    def start_tile(tile, par):
        for j in range(N_CHUNK):
            s = pl.ds(pl.multiple_of(src_ref[tile, j], SEG_ALIGN), SEG_ALIGN)
            pltpu.make_async_copy(ys_hbm.at[s], ybuf.at[par, j * SEG_ALIGN:(j + 1) * SEG_ALIGN],
                                  sem.at[par]).start()

    @pl.when(i == 0)
    def _():
        start_tile(0, 0)

    pltpu.make_async_copy(ys_hbm.at[pl.ds(0, L_CAP)], ybuf.at[p], sem.at[p]).wait()
    start_tile(jnp.minimum(i + 1, n - 1), 1 - p)

    rows = COMBINE_ROWS
    for c in range(tm // rows):
        col = col_ref[c * rows:(c + 1) * rows, :]
        pos0 = col[:, ROW_POS0:ROW_POS0 + 1].astype(jnp.int32)
        pos1 = col[:, ROW_POS1:ROW_POS1 + 1].astype(jnp.int32)
        l = lax.broadcasted_iota(jnp.int32, (rows, L_CAP), 1)
        perm = jnp.logical_or(l == pos0, l == pos1).astype(jnp.bfloat16)
        y = jnp.dot(perm, ybuf[p], preferred_element_type=jnp.float32)
        out_ref[c * rows:(c + 1) * rows, :] = x1_ref[c * rows:(c + 1) * rows, :] + y

    @pl.when(i == n - 1)
    def _():
        pltpu.make_async_copy(ys_hbm.at[pl.ds(0, L_CAP)], ybuf.at[1 - p], sem.at[1 - p]).wait()


def _combine(x1, col, ys, src):
    n_tok = x1.shape[0]
    tm = TM_PROJ
    grid_spec = pltpu.PrefetchScalarGridSpec(
        num_scalar_prefetch=1,
        grid=(n_tok // tm,),
        in_specs=[
            pl.BlockSpec((tm, D_MODEL), lambda i, *_: (i, 0)),
            pl.BlockSpec((tm, LANES), lambda i, *_: (i, 0)),
            pl.BlockSpec(memory_space=pl.ANY),
        ],
        out_specs=pl.BlockSpec((tm, D_MODEL), lambda i, *_: (i, 0)),
        scratch_shapes=[
            pltpu.VMEM((2, L_CAP, D_MODEL), jnp.bfloat16),
            pltpu.SemaphoreType.DMA((2,)),
        ],
    )
    return pl.pallas_call(
        _combine_kernel,
        grid_spec=grid_spec,
        out_shape=jax.ShapeDtypeStruct((n_tok, D_MODEL), jnp.float32),
        compiler_params=pltpu.CompilerParams(dimension_semantics=("arbitrary",),
                                             vmem_limit_bytes=VMEM_LIMIT),
        name="moe_combine",
    )(src, x1, col, ys)


def _routing_tables(counts, n_blk_cap):
    r_cap = n_blk_cap * MOE_BM
    i32 = jnp.int32
    ca = ((counts + (SEG_ALIGN - 1)) // SEG_ALIGN) * SEG_ALIGN
    lend = jnp.cumsum(ca, axis=1)
    lstart = lend - ca
    tot = jnp.sum(ca, axis=0)
    region = ((tot + (MOE_BM - 1)) // MOE_BM) * MOE_BM
    rend = jnp.cumsum(region)
    base = rend - region
    gstart = base[None, :] + jnp.cumsum(ca, axis=0) - ca
    j16 = jnp.arange(N_CHUNK, dtype=i32) * SEG_ALIGN
    in_seg = jnp.logical_and(lstart[:, None, :] <= j16[None, :, None], j16[None, :, None] < lend[:, None, :])
    seg_row = (jnp.sum(jnp.where(in_seg, (gstart - lstart)[:, None, :], 0), axis=2) + j16[None, :]).astype(i32)
    used = j16[None, :] < lend[:, -1:]
    tile_par = (jnp.arange(counts.shape[0], dtype=i32) % 2)[:, None]
    park = r_cap + (tile_par * N_CHUNK + jnp.arange(N_CHUNK, dtype=i32)[None, :]) * SEG_ALIGN
    dst = jnp.where(used, seg_row, park).astype(i32)
    src = jnp.where(used, seg_row, 0).astype(i32)
    nused = (rend[-1] // MOE_BM).astype(i32)
    brow = jnp.minimum(jnp.arange(n_blk_cap, dtype=i32), nused - 1) * MOE_BM
    blk_e = jnp.minimum(jnp.sum(rend[None, :] <= brow[:, None], axis=1), N_EXPERTS - 1).astype(i32)
    zstart = (base + tot).astype(i32)
    nzc = ((region - tot) // SEG_ALIGN).astype(i32)
    blk_i = jnp.arange(n_blk_cap, dtype=i32)
    seg_end = jnp.sum(jnp.where(blk_e[:, None] == jnp.arange(N_EXPERTS)[None, :], (base + tot)[None, :], 0), axis=1)
    nvalid = jnp.where(blk_i < nused, jnp.clip(seg_end - blk_i * MOE_BM, 0, MOE_BM), 0).astype(i32)
    eid = jnp.arange(N_EXPERTS, dtype=i32)
    later = jnp.logical_and(eid[None, :] > eid[:, None], (region > 0)[None, :])
    nxt = jnp.min(jnp.where(later, eid[None, :], N_EXPERTS), axis=1)
    nxt = jnp.where(nxt < N_EXPERTS, nxt, eid)
    next_e = jnp.sum(jnp.where(blk_e[:, None] == eid[None, :], nxt[None, :], 0), axis=1).astype(i32)
    return dst, src, nzc, zstart, blk_e, nused[None], nvalid, next_e


def _t5_bucket_np(dist):
    max_exact = N_BUCKETS // 2
    d = np.maximum(dist, 1).astype(np.float32)
    large = max_exact + (np.log(d / max_exact) / math.log(MAX_DISTANCE / max_exact)
                         * (N_BUCKETS - max_exact)).astype(np.int32)
    large = np.minimum(large, N_BUCKETS - 1)
    return np.where(dist < max_exact, dist, large)


def _attention_bias(rel_bias):
    qi = np.arange(ATT_BLOCK)[:, None]
    kj = np.arange(2 * ATT_BLOCK)[None, :]
    dist = qi + ATT_BLOCK - kj
    in_window = (dist >= 0) & (dist < WINDOW)
    bucket = _t5_bucket_np(np.clip(dist, 0, WINDOW - 1))
    onehot = (bucket[None] == np.arange(N_BUCKETS)[:, None, None]).astype(np.float32)
    bias = jnp.einsum('nh,nqk->hqk', rel_bias.astype(jnp.float32), jnp.asarray(onehot),
                      precision=lax.Precision.HIGHEST)
    bias = jnp.where(jnp.asarray(in_window)[None], bias * LOG2E, NEG_BIG)
    no_prev = jnp.asarray(kj < ATT_BLOCK)[None]
    return jnp.stack([bias, jnp.where(no_prev, NEG_BIG, bias)])


def _block_diag_mean(width, block):
    idx = np.arange(width) // block
    return jnp.asarray((idx[:, None] == idx[None, :]).astype(np.float32) / block, dtype=jnp.bfloat16)


def kernel(x, rel_bias, norm_mix_g, w_in, q_norm_g, k_norm_g, attn_sink, conv_w, conv_b, gate_b,
           mlstm_norm_g, w_out, norm_ffn_g, w_router_group, b_router_group, w_router_expert,
           b_router_expert, w_gate, w_up, w_down):
    batch, seq_len, _ = x.shape
    n_tok = batch * seq_len
    assert seq_len % TM_PROJ == 0 and seq_len % TM_IN == 0 and seq_len % ML_STEP == 0
    f32 = jnp.float32
    bias = _attention_bias(rel_bias)
    bdq = _block_diag_mean(ATT_WIDTH, ATT_HEAD_DIM)
    bdk = _block_diag_mean(LANES, ATT_HEAD_DIM)
    x2 = x.reshape(n_tok, D_MODEL)
    for l in range(DEPTH):
        wgt = _bf16(w_in[l, :, O_G:N_IN].T)
        qg = (jnp.tile(q_norm_g[l].astype(f32), ATT_HEADS) * (ATT_HEAD_DIM ** -0.5 * LOG2E))[None, :]
        kg = jnp.tile(k_norm_g[l].astype(f32), ATT_KV_HEADS)[None, :]
        gbt = gate_b[l].astype(f32)[:, None]
        qn, kn, vd, qkm, vm, om, gates_t = _in_proj(
            x2, norm_mix_g[l][None, :], w_in, l, wgt, qg, kg, bdq, bdk,
            conv_w[l], conv_b[l][None, :], gbt, batch, seq_len)
        att = _attention(qn, kn, vd, bias, attn_sink[l].astype(f32) * LOG2E, batch, seq_len)
        r3 = lambda a: a.reshape(batch, seq_len, a.shape[-1])
        hm = _mlstm(r3(qkm), r3(vm), r3(om), gates_t, mlstm_norm_g[l][None, :], batch, seq_len)
        n_rt = 4 * SUBLANES
        wrt = jnp.pad(jnp.concatenate([w_router_expert[l], w_router_group[l]], axis=1).astype(f32).T,
                      ((0, n_rt - N_EXPERTS - N_GROUPS), (0, 0)))
        brt = jnp.pad(jnp.concatenate([b_router_expert[l], b_router_group[l]]).astype(f32),
                      (0, n_rt - N_EXPERTS - N_GROUPS))[:, None]
        wrt_hi = _bf16(wrt)
        wrt = jnp.concatenate([wrt_hi, _bf16(wrt - wrt_hi.astype(f32))], axis=0)
        x1, t, row, col, cnt = _out_proj_router(x2, att, hm.reshape(n_tok, ML_WIDTH), w_out, l,
                                                norm_ffn_g[l][None, :], wrt, brt)
        _, n_blk_cap = _moe_capacity(n_tok)
        dst, src, nz, zdst, blk_e, nused, nvalid, next_e = _routing_tables(cnt[:, :, 0], n_blk_cap)
        xs = _dispatch(t, col, row, dst, nz, zdst)
        ys = _moe(xs, blk_e, nused, nvalid, next_e, w_gate, w_up, w_down, l)
        x2 = _combine(x1, col, ys, src)
    return x2.reshape(batch, seq_len, D_MODEL)
```

```python
import functools
import math

import jax
import jax.numpy as jnp
import numpy as np
from jax import lax
from jax.experimental import pallas as pl
from jax.experimental.pallas import tpu as pltpu

D_MODEL = 1024
DEPTH = 2
ATT_HEADS = 8
ATT_KV_HEADS = 2
ATT_HEAD_DIM = 64
ATT_WIDTH = ATT_HEADS * ATT_HEAD_DIM
WINDOW = 128
ATT_BLOCK = 128
N_BUCKETS = 32
MAX_DISTANCE = 128
ML_HEADS = 4
ML_DQK = 64
ML_DV = 128
ML_WIDTH = ML_HEADS * ML_DV
ML_CHUNK = 64
CONV_K = 4
N_GROUPS = 4
EXPERTS_PER_GROUP = 4
N_EXPERTS = N_GROUPS * EXPERTS_PER_GROUP
D_FF_EXPERT = 512
EPS = 1e-6

LANES = 128
SUBLANES = 8
NEG_BIG = -1e30
LOG2E = math.log2(math.e)
VMEM_LIMIT = 48 * 1024 * 1024
FRONT_VMEM_LIMIT = 56 * 1024 * 1024

O_Q = 0
O_K = O_Q + ATT_WIDTH
O_V = O_K + ATT_KV_HEADS * ATT_HEAD_DIM
O_QM = O_V + ATT_KV_HEADS * ATT_HEAD_DIM
O_KM = O_QM + ML_HEADS * ML_DQK
O_VM = O_KM + ML_HEADS * ML_DQK
O_OM = O_VM + ML_WIDTH
O_G = O_OM + ML_WIDTH
N_IN = O_G + 2 * ML_HEADS

ATT_STEP_BLOCKS = 4
TM_OUT = 1024
TM_PROJ = 512
ML_KCHUNK = 256
ML_STEP = ML_KCHUNK

SEG_ALIGN = 16
MOE_BM = 512
MOE_FF_SPLIT = 2
L_CAP = 2 * TM_PROJ + N_EXPERTS * SEG_ALIGN
N_CHUNK = L_CAP // SEG_ALIGN
D_XS = D_MODEL + LANES
DUMP_ROWS = 2 * N_CHUNK * SEG_ALIGN
DISPATCH_ROWS = 256
COMBINE_ROWS = 256
COMBINE_BUFS = 3
ROW_POS0, ROW_POS1, ROW_W0, ROW_W1, ROW_E0, ROW_E1 = 0, 1, 2, 3, 4, 5
COL_W0H, COL_W1H, COL_E0, COL_E1 = 8, 11, 14, 15


def _moe_capacity(n_tok):
    n_tiles = n_tok // TM_PROJ
    rows = 2 * n_tok + n_tiles * N_EXPERTS * (SEG_ALIGN - 1) + N_EXPERTS * (MOE_BM - SEG_ALIGN)
    n_blk = -(-rows // MOE_BM)
    return n_blk * MOE_BM, n_blk


def _bf16(a):
    return a.astype(jnp.bfloat16)


def _split3(a):
    hi = _bf16(a)
    r1 = a - hi.astype(jnp.float32)
    mid = _bf16(r1)
    return hi, mid, _bf16(r1 - mid.astype(jnp.float32))


def _log_sigmoid(z):
    return jnp.minimum(z, 0.0) - jnp.log(1.0 + jnp.exp(-jnp.abs(z)))


def _front_kernel(x_ref, g_ref, w_ref, wgt_ref, qg_ref, kg_ref, bdq_ref, bdk_ref, cw_ref, cb_ref,
                  gbt_ref, ng_ref,
                  qn_ref, kn_ref, vd_ref, hm_ref,
                  w_bf, conv_scr, qkm_s, vm_s, om_s, gt_s, c_scr, m_scr):
    s = pl.program_id(0)
    batch = x_ref.shape[0]
    L = ML_KCHUNK
    wr = s % 2
    rd = 1 - wr

    @pl.when(s == 0)
    def _():
        w_bf[...] = _bf16(w_ref[...])
        conv_scr[:, 0:SUBLANES, :] = jnp.zeros((batch, SUBLANES, conv_scr.shape[2]), jnp.float32)
        qkm_s[1] = jnp.zeros(qkm_s.shape[1:], qkm_s.dtype)
        vm_s[1] = jnp.zeros(vm_s.shape[1:], vm_s.dtype)
        om_s[1] = jnp.zeros(om_s.shape[1:], om_s.dtype)
        gt_s[1] = jnp.zeros(gt_s.shape[1:], gt_s.dtype)

    @pl.when(s > 0)
    def _():
        conv_scr[:, 0:SUBLANES, :] = conv_scr[:, L:L + SUBLANES, :]

    @pl.when(s <= 1)
    def _():
        c_scr[...] = jnp.zeros(c_scr.shape, jnp.float32)
        m_scr[...] = jnp.zeros(m_scr.shape, jnp.float32)

    scan_prep, scan_head, scan_finish = _mlstm_pieces(
        qkm_s.at[rd], vm_s.at[rd], om_s.at[rd], gt_s.at[rd], ng_ref, hm_ref, c_scr, m_scr, batch)
    norm_part, proj_parts = _project_pieces(
        x_ref, g_ref, w_bf, wgt_ref, qg_ref, kg_ref, bdq_ref, bdk_ref, cw_ref, cb_ref, gbt_ref,
        qn_ref, kn_ref, vd_ref, qkm_s.at[wr], vm_s.at[wr], om_s.at[wr], gt_s.at[wr], conv_scr)
    for b in range(batch):
        prep = scan_prep(b)
        scan_head(b, 0, prep)
        scan_head(b, 1, prep)
        hb = norm_part(b)
        proj_parts[0](b, hb)
        proj_parts[1](b, hb)
        scan_head(b, 2, prep)
        scan_head(b, 3, prep)
        for part in proj_parts[2:]:
            part(b, hb)
    scan_finish()


def _project_pieces(x_ref, g_ref, w_bf, wgt_ref, qg_ref, kg_ref, bdq_ref, bdk_ref, cw_ref, cb_ref,
                    gbt_ref, qn_ref, kn_ref, vd_ref, qkm_ref, vm_ref, om_ref, gatet_ref, conv_scr):
    L = ML_KCHUNK
    conv_w = 2 * ML_HEADS * ML_DQK

    def proj(hb, c0, width):
        return jnp.dot(hb, w_bf[:, c0:c0 + width], preferred_element_type=jnp.float32)

    def norm_part(b):
        x = x_ref[b]
        hn = x * lax.rsqrt(jnp.mean(x * x, axis=-1, keepdims=True) + EPS) * g_ref[...]
        return _bf16(hn)

    def q_part(b, hb):
        q = proj(hb, O_Q, ATT_WIDTH)
        q_ms = jnp.dot(_bf16(q * q), bdq_ref[...], preferred_element_type=jnp.float32)
        qn_ref[b] = _bf16(q * lax.rsqrt(q_ms + EPS) * qg_ref[...])

    def kv_part(b, hb):
        kv = proj(hb, O_K, 2 * LANES)
        k = kv[:, 0:LANES]
        v = kv[:, LANES:2 * LANES]
        k_ms = jnp.dot(_bf16(k * k), bdk_ref[...], preferred_element_type=jnp.float32)
        kn = k * lax.rsqrt(k_ms + EPS) * kg_ref[...]
        low = lax.broadcasted_iota(jnp.int32, kn.shape, 1) < ATT_HEAD_DIM

        def dup_heads(a):
            swapped = pltpu.roll(a, ATT_HEAD_DIM, axis=1)
            return jnp.concatenate([jnp.where(low, a, swapped), jnp.where(low, swapped, a)], axis=1)

        kn_ref[b] = _bf16(dup_heads(kn))
        vd_ref[b] = _bf16(dup_heads(v))

    def conv_part(b, hb):
        qk = proj(hb, O_QM, conv_w)
        conv_scr[b, SUBLANES:SUBLANES + L, :] = qk
        y = qk * cw_ref[CONV_K - 1:CONV_K, :] + cb_ref[...]
        for j in range(CONV_K - 1):
            off = SUBLANES - (CONV_K - 1) + j
            y = y + conv_scr[b, off:off + L, :] * cw_ref[j:j + 1, :]
        y = y * jax.nn.sigmoid(y)
        lane = lax.broadcasted_iota(jnp.int32, y.shape, 1)
        y = jnp.where(lane >= ML_HEADS * ML_DQK, y * (ML_DQK ** -0.5), y)
        qkm_ref[b] = _bf16(y)

    def vm_part(b, hb):
        vm_ref[b] = _bf16(proj(hb, O_VM, ML_WIDTH))

    def om_part(b, hb):
        om_ref[b] = _bf16(jax.nn.sigmoid(proj(hb, O_OM, ML_WIDTH)))

        gt = lax.dot_general(wgt_ref[...], hb, (((1,), (1,)), ((), ())),
                             preferred_element_type=jnp.float32) + gbt_ref[...]
        grow = lax.broadcasted_iota(jnp.int32, gt.shape, 0)
        gatet_ref[b] = jnp.where(grow >= ML_HEADS, _log_sigmoid(gt), gt)

    return norm_part, (q_part, kv_part, conv_part, vm_part, om_part)


def _front(x3, g, w_in, layer, wgt, qg, kg, bdq, bdk, cw, cb, gbt, ng):
    batch, seq_len, _ = x3.shape
    L = ML_KCHUNK
    nc = seq_len // L
    fix = lambda s: (0, 0)
    cur = lambda s: (0, jnp.minimum(s, nc - 1), 0)
    prev = lambda s: (0, jnp.maximum(s - 1, 0), 0)
    kv_w = 2 * ATT_KV_HEADS * ATT_HEAD_DIM
    conv_w = 2 * ML_HEADS * ML_DQK
    return pl.pallas_call(
        _front_kernel,
        grid=(nc + 1,),
        in_specs=[
            pl.BlockSpec((batch, L, D_MODEL), cur),
            pl.BlockSpec((1, D_MODEL), fix),
            pl.BlockSpec((None, D_MODEL, N_IN), lambda s: (layer, 0, 0), pipeline_mode=pl.Buffered(1)),
            pl.BlockSpec((SUBLANES, D_MODEL), fix),
            pl.BlockSpec((1, ATT_WIDTH), fix),
            pl.BlockSpec((1, LANES), fix),
            pl.BlockSpec((ATT_WIDTH, ATT_WIDTH), fix),
            pl.BlockSpec((LANES, LANES), fix),
            pl.BlockSpec((CONV_K, conv_w), fix),
            pl.BlockSpec((1, conv_w), fix),
            pl.BlockSpec((SUBLANES, 1), fix),
            pl.BlockSpec((1, ML_WIDTH), fix),
        ],
        out_specs=[
            pl.BlockSpec((batch, L, ATT_WIDTH), cur),
            pl.BlockSpec((batch, L, kv_w), cur),
            pl.BlockSpec((batch, L, kv_w), cur),
            pl.BlockSpec((batch, L, ML_WIDTH), prev),
        ],
        out_shape=(
            jax.ShapeDtypeStruct((batch, seq_len, ATT_WIDTH), jnp.bfloat16),
            jax.ShapeDtypeStruct((batch, seq_len, kv_w), jnp.bfloat16),
            jax.ShapeDtypeStruct((batch, seq_len, kv_w), jnp.bfloat16),
            jax.ShapeDtypeStruct((batch, seq_len, ML_WIDTH), jnp.bfloat16),
        ),
        scratch_shapes=[
            pltpu.VMEM((D_MODEL, N_IN), jnp.bfloat16),
            pltpu.VMEM((batch, L + 2 * SUBLANES, conv_w), jnp.float32),
            pltpu.VMEM((2, batch, L, conv_w), jnp.bfloat16),
            pltpu.VMEM((2, batch, L, ML_WIDTH), jnp.bfloat16),
            pltpu.VMEM((2, batch, L, ML_WIDTH), jnp.bfloat16),
            pltpu.VMEM((2, batch, SUBLANES, L), jnp.float32),
            pltpu.VMEM((batch * ML_HEADS // 2, 2 * ML_DQK, 2 * ML_DV), jnp.float32),
            pltpu.VMEM((batch * ML_HEADS, 1, LANES), jnp.float32),
        ],
        compiler_params=pltpu.CompilerParams(dimension_semantics=("arbitrary",),
                                             vmem_limit_bytes=FRONT_VMEM_LIMIT),
        name="proj_mlstm",
    )(x3, g, w_in, wgt, qg, kg, bdq, bdk, cw, cb, gbt, ng)


def _attn_kernel(sink_ref, q_ref, kp_ref, kc_ref, vp_ref, vc_ref, bias_ref, o_ref):
    i = pl.program_id(1)
    blk = ATT_BLOCK
    lane = lax.broadcasted_iota(jnp.int32, (blk, LANES), 1)
    low = lane < ATT_HEAD_DIM
    group = ATT_HEADS // ATT_KV_HEADS
    ones = jnp.ones((2 * blk, LANES), jnp.bfloat16)
    for j in range(ATT_STEP_BLOCKS):
        rows = slice(j * blk, (j + 1) * blk)
        first = (i == 0).astype(jnp.int32) if j == 0 else 0
        for kv in range(ATT_KV_HEADS):
            lanes = slice(kv * LANES, (kv + 1) * LANES)
            if j == 0:
                kcat = jnp.concatenate([kp_ref[:, lanes], kc_ref[0:blk, lanes]], axis=0)
                vrows = jnp.concatenate([vp_ref[:, lanes], vc_ref[0:blk, lanes]], axis=0)
            else:
                kcat = kc_ref[(j - 1) * blk:(j + 1) * blk, lanes]
                vrows = vc_ref[(j - 1) * blk:(j + 1) * blk, lanes]
            vcat = jnp.concatenate([vrows, ones], axis=1)
            heads = range(kv * group, (kv + 1) * group)
            qs = []
            for h in heads:
                qp = q_ref[rows, (h // 2) * LANES:(h // 2 + 1) * LANES]
                qs.append(jnp.where(low if h % 2 == 0 else jnp.logical_not(low), qp, jnp.zeros_like(qp)))
            s_all = lax.dot_general(jnp.concatenate(qs, axis=0), kcat, (((1,), (1,)), ((), ())),
                                    preferred_element_type=jnp.float32)
            ps, ms = [], []
            for n, h in enumerate(heads):
                logits = s_all[n * blk:(n + 1) * blk, :] + bias_ref[first, h]
                row_max = jnp.broadcast_to(jnp.max(logits, axis=-1, keepdims=True), (blk, LANES))
                m = jnp.maximum(row_max, sink_ref[h])
                ps.append(_bf16(jnp.exp2(logits - jnp.concatenate([m, m], axis=1))))
                ms.append(m)
            o_all = jnp.dot(jnp.concatenate(ps, axis=0), vcat, preferred_element_type=jnp.float32)
            outs = []
            for n, h in enumerate(heads):
                o = o_all[n * blk:(n + 1) * blk, :]
                den = o[:, LANES:2 * LANES] + jnp.exp2(sink_ref[h] - ms[n])
                outs.append(o[:, 0:LANES] / den)
            for n in range(0, group, 2):
                pair = (kv * group + n) // 2
                o_ref[rows, pair * LANES:(pair + 1) * LANES] = _bf16(
                    jnp.where(low, outs[n], outs[n + 1]))


def _attention(qn, kn, vd, bias, sink, batch, seq_len):
    step = ATT_STEP_BLOCKS * ATT_BLOCK
    ns = seq_len // step
    cur = lambda b, i, s: (b * ns + i, 0)
    prev = lambda b, i, s: ((b * ns + i) * ATT_STEP_BLOCKS - jnp.minimum(i, 1), 0)
    grid_spec = pltpu.PrefetchScalarGridSpec(
        num_scalar_prefetch=1,
        grid=(batch, ns),
        in_specs=[
            pl.BlockSpec((step, ATT_WIDTH), cur),
            pl.BlockSpec((ATT_BLOCK, 2 * LANES), prev),
            pl.BlockSpec((step, 2 * LANES), cur),
            pl.BlockSpec((ATT_BLOCK, 2 * LANES), prev),
            pl.BlockSpec((step, 2 * LANES), cur),
            pl.BlockSpec((2, ATT_HEADS, ATT_BLOCK, 2 * ATT_BLOCK), lambda b, i, s: (0, 0, 0, 0)),
        ],
        out_specs=pl.BlockSpec((step, ATT_WIDTH), cur),
    )
    return pl.pallas_call(
        _attn_kernel,
        grid_spec=grid_spec,
        out_shape=jax.ShapeDtypeStruct((batch * seq_len, ATT_WIDTH), jnp.bfloat16),
        compiler_params=pltpu.CompilerParams(dimension_semantics=("arbitrary", "arbitrary"),
                                             vmem_limit_bytes=VMEM_LIMIT),
        name="swa_attention",
    )(sink, qn, kn, kn, vd, vd, bias)


def _mlstm_pieces(qk_ref, v_ref, o_ref, gt_ref, ng_ref, out_ref, c_scr, m_scr, batch):
    L = ML_KCHUNK
    pairs = ML_HEADS // 2
    r_i = lax.broadcasted_iota(jnp.int32, (L, L), 0)
    c_i = lax.broadcasted_iota(jnp.int32, (L, L), 1)
    causal = c_i <= r_i
    tril_bf = causal.astype(jnp.bfloat16)
    triu_bf = (r_i <= c_i).astype(jnp.bfloat16)
    mean_dv = jnp.full((ML_DV, ML_DV), 1.0 / ML_DV, jnp.bfloat16)
    lane = lax.broadcasted_iota(jnp.int32, (L, LANES), 1)
    low = lane < ML_DQK
    ones_dv = jnp.ones((L, ML_DV), jnp.bfloat16)
    row2 = lax.broadcasted_iota(jnp.int32, (2 * ML_DQK, 2 * ML_DV), 0)

    def twice(a):
        return jnp.concatenate([a, a], axis=1)

    m_state = [m_scr[k] for k in range(batch * ML_HEADS)]
    c_state = [c_scr[k] for k in range(batch * pairs)]

    def batch_prep(b):
        gt = gt_ref[b]
        g = jnp.concatenate([gt, jnp.zeros((LANES - SUBLANES, L), jnp.float32)], axis=0).T
        bcols = sum(jnp.dot(tril_bf, part, preferred_element_type=jnp.float32) for part in _split3(g))
        brows = sum(jnp.dot(part, triu_bf, preferred_element_type=jnp.float32) for part in _split3(gt))
        return gt, g, bcols, brows

    carry = {}

    def head_piece(b, h, prep):
        gt, g, bcols, brows = prep
        pair, sub = divmod(h, 2)
        sidx = b * pairs + pair
        qp = qk_ref[b, :, pair * LANES:(pair + 1) * LANES]
        kp = qk_ref[b, :, (pairs + pair) * LANES:(pairs + pair + 1) * LANES]
        c_pair = c_state[sidx]
        c_bf = _bf16(c_pair)
        if True:
            sel = low if sub == 0 else jnp.logical_not(low)
            m_prev = m_state[b * ML_HEADS + h]
            bc = jnp.broadcast_to(bcols[:, ML_HEADS + h:ML_HEADS + h + 1], (L, LANES))
            lic = jnp.broadcast_to(g[:, h:h + 1], (L, LANES))
            br = brows[ML_HEADS + h:ML_HEADS + h + 1, :]
            lir = gt[h:h + 1, :]
            log_d = jnp.where(causal, twice(bc) - (br - lir), NEG_BIG)
            m_inter = bc + m_prev
            row_max = jnp.broadcast_to(jnp.max(log_d, axis=-1, keepdims=True), (L, LANES))
            m_row = jnp.maximum(m_inter, row_max)
            d = jnp.exp(log_d - twice(m_row))
            inter = jnp.exp(m_inter - m_row)
            qm = jnp.where(sel, qp, jnp.zeros_like(qp))
            s = lax.dot_general(qm, kp, (((1,), (1,)), ((), ())),
                                preferred_element_type=jnp.float32) * d
            v_ext = jnp.concatenate([v_ref[b, :, h * ML_DV:(h + 1) * ML_DV], ones_dv],
                                    axis=-1)
            num = twice(inter) * jnp.dot(qm, c_bf, preferred_element_type=jnp.float32) \
                + jnp.dot(_bf16(s), v_ext, preferred_element_type=jnp.float32)
            den = num[:, ML_DV:2 * ML_DV]
            hval = num[:, 0:ML_DV] / jnp.maximum(jnp.abs(den), jnp.exp(-m_row))
            h_ms = jnp.dot(_bf16(hval * hval), mean_dv, preferred_element_type=jnp.float32)
            hn = hval * lax.rsqrt(h_ms + EPS)
            hn = hn * ng_ref[:, h * ML_DV:(h + 1) * ML_DV]
            out_ref[b, :, h * ML_DV:(h + 1) * ML_DV] = _bf16(
                hn * o_ref[b, :, h * ML_DV:(h + 1) * ML_DV].astype(jnp.float32))
            b_last = bc[L - 1:L, :]
            log_w = b_last - bc + lic
            m_next = jnp.maximum(b_last + m_prev, jnp.max(log_w, axis=0, keepdims=True))
            w = jnp.exp(log_w - m_next)
            decay = jnp.exp(b_last + m_prev - m_next)
            m_state[b * ML_HEADS + h] = m_next
            kw = _bf16(jnp.where(sel, kp.astype(jnp.float32) * w, 0.0))
            upd = lax.dot_general(kw, v_ext, (((0,), (0,)), ((), ())),
                                  preferred_element_type=jnp.float32)
        if sub == 0:
            carry[sidx] = (decay, upd)
        else:
            decay0, upd0 = carry.pop(sidx)
            decay_rows = jnp.where(row2 < ML_DQK, twice(decay0), twice(decay))
            c_state[sidx] = decay_rows * c_pair + (upd0 + upd)

    def finish():
        for k in range(batch * ML_HEADS):
            m_scr[k] = m_state[k]
        for k in range(batch * pairs):
            c_scr[k] = c_state[k]

    return batch_prep, head_piece, finish


def _out_proj_router_kernel(x_ref, att_ref, hm_ref, wo_ref, g_ref, wrt_ref, brt_ref,
                            x1_ref, t_ref, row_ref, col_ref, cnt_ref, wo_bf):
    tm = x_ref.shape[0]
    tr = TM_PROJ
    hi = lax.Precision.HIGHEST

    @pl.when(pl.program_id(0) == 0)
    def _():
        wo_bf[...] = _bf16(wo_ref[...])

    for sb in range(tm // TM_PROJ):
        rs = slice(sb * TM_PROJ, (sb + 1) * TM_PROJ)
        x1 = x_ref[rs, :] \
            + jnp.dot(att_ref[rs, :], wo_bf[0:ATT_WIDTH, :], preferred_element_type=jnp.float32) \
            + jnp.dot(hm_ref[rs, :], wo_bf[ATT_WIDTH:, :], preferred_element_type=jnp.float32)
        x1_ref[rs, :] = x1
        tn = x1 * lax.rsqrt(jnp.mean(x1 * x1, axis=-1, keepdims=True) + EPS) * g_ref[...]
        tn_hi = _bf16(tn)
        t_ref[rs, :] = tn_hi
        tn_lo = _bf16(tn - tn_hi.astype(jnp.float32))
        nt = (((1,), (1,)), ((), ()))
        n_rt = wrt_ref.shape[0] // 2
        p_hi = lax.dot_general(wrt_ref[...], tn_hi, nt, preferred_element_type=jnp.float32)
        p_lo = lax.dot_general(wrt_ref[0:n_rt, :], tn_lo, nt, preferred_element_type=jnp.float32)
        logits = p_hi[0:n_rt, :] + p_hi[n_rt:, :] + p_lo + brt_ref[...]
        el_all = logits[0:N_EXPERTS, :]
        gl = logits[N_EXPERTS:N_EXPERTS + SUBLANES, :]
        grow = lax.broadcasted_iota(jnp.int32, gl.shape, 0).astype(jnp.float32)
        gl = jnp.where(grow < N_GROUPS, gl, NEG_BIG)
        gmax = jnp.max(gl, axis=0, keepdims=True)
        grp = jnp.min(jnp.where(gl == gmax, grow, float(N_GROUPS)), axis=0, keepdims=True)
        p_grp = 1.0 / jnp.sum(jnp.exp(gl - gmax), axis=0, keepdims=True)
        erow = lax.broadcasted_iota(jnp.int32, el_all.shape, 0).astype(jnp.float32)
        egrp = jnp.floor(erow * (1.0 / EXPERTS_PER_GROUP))
        el = jnp.where(egrp == grp, el_all, NEG_BIG)
        e1 = jnp.max(el, axis=0, keepdims=True)
        i1 = jnp.min(jnp.where(el == e1, erow, float(N_EXPERTS)), axis=0, keepdims=True)
        el2 = jnp.where(erow == i1, NEG_BIG, el)
        e2 = jnp.max(el2, axis=0, keepdims=True)
        i2 = jnp.min(jnp.where(el2 == e2, erow, float(N_EXPERTS)), axis=0, keepdims=True)
        z2 = jnp.exp(e2 - e1)
        w1 = p_grp / (1.0 + z2)
        w2 = p_grp * z2 / (1.0 + z2)
        sel1 = erow == i1
        sel2 = erow == i2
        onehot = jnp.logical_or(sel1, sel2)
        t_r = lax.broadcasted_iota(jnp.int32, (tr, tr), 0)
        t_c = lax.broadcasted_iota(jnp.int32, (tr, tr), 1)
        before = (t_r < t_c).astype(jnp.bfloat16)
        rank = jnp.dot(onehot.astype(jnp.bfloat16), before, preferred_element_type=jnp.float32)
        cnt = jnp.sum(onehot.astype(jnp.float32), axis=1, keepdims=True)
        cnt_al = jnp.floor((cnt + (SEG_ALIGN - 1)) * (1.0 / SEG_ALIGN)) * SEG_ALIGN
        e_r = lax.broadcasted_iota(jnp.int32, (N_EXPERTS, N_EXPERTS), 0)
        e_c = lax.broadcasted_iota(jnp.int32, (N_EXPERTS, N_EXPERTS), 1)
        lstart = jnp.dot((e_c < e_r).astype(jnp.float32), jnp.broadcast_to(cnt_al, (N_EXPERTS, LANES)),
                         precision=hi, preferred_element_type=jnp.float32)[:, 0:1]
        slot = lstart + rank
        pos1 = jnp.sum(jnp.where(sel1, slot, 0.0), axis=0, keepdims=True)
        pos2 = jnp.sum(jnp.where(sel2, slot, 0.0), axis=0, keepdims=True)
        r8 = lax.broadcasted_iota(jnp.int32, (SUBLANES, tr), 0)
        info = jnp.where(r8 == ROW_POS0, pos1, jnp.where(r8 == ROW_POS1, pos2, jnp.where(
            r8 == ROW_W0, w1, jnp.where(r8 == ROW_W1, w2, jnp.where(
                r8 == ROW_E0, i1, jnp.where(r8 == ROW_E1, i2, 0.0))))))
        row_ref[:, rs] = info

        def split3(w):
            h = _bf16(w).astype(jnp.float32)
            m = _bf16(w - h).astype(jnp.float32)
            return h, m, _bf16(w - h - m).astype(jnp.float32)

        w1h, w1m, w1l = split3(w1)
        w2h, w2m, w2l = split3(w2)
        parts = jnp.where(r8 == 0, w1h, jnp.where(r8 == 1, w1m, jnp.where(r8 == 2, w1l, jnp.where(
            r8 == 3, w2h, jnp.where(r8 == 4, w2m, jnp.where(r8 == 5, w2l, jnp.where(r8 == 6, i1, i2)))))))
        col_ref[rs, :] = jnp.concatenate(
            [info, parts, jnp.zeros((LANES - 2 * SUBLANES, tr), jnp.float32)], axis=0).T
        cnt_ref[sb] = jnp.broadcast_to(cnt, (N_EXPERTS, LANES)).astype(jnp.int32)


def _out_proj_router(x2, att, hm, wo, layer, g, wrt, brt):
    t = x2.shape[0]
    tm = TM_OUT
    row = lambda i: (i, 0)
    fix = lambda i: (0, 0)
    return pl.pallas_call(
        _out_proj_router_kernel,
        grid=(t // tm,),
        in_specs=[
            pl.BlockSpec((tm, D_MODEL), row),
            pl.BlockSpec((tm, ATT_WIDTH), row),
            pl.BlockSpec((tm, ML_WIDTH), row),
            pl.BlockSpec((None, D_MODEL, D_MODEL), lambda i: (layer, 0, 0), pipeline_mode=pl.Buffered(1)),
            pl.BlockSpec((1, D_MODEL), fix),
            pl.BlockSpec((8 * SUBLANES, D_MODEL), fix),
            pl.BlockSpec((4 * SUBLANES, 1), fix),
        ],
        out_specs=[
            pl.BlockSpec((tm, D_MODEL), row),
            pl.BlockSpec((tm, D_MODEL), row),
            pl.BlockSpec((SUBLANES, tm), lambda i: (0, i)),
            pl.BlockSpec((tm, LANES), row),
            pl.BlockSpec((tm // TM_PROJ, N_EXPERTS, LANES), lambda i: (i, 0, 0)),
        ],
        out_shape=(
            jax.ShapeDtypeStruct((t, D_MODEL), jnp.float32),
            jax.ShapeDtypeStruct((t, D_MODEL), jnp.bfloat16),
            jax.ShapeDtypeStruct((SUBLANES, t), jnp.float32),
            jax.ShapeDtypeStruct((t, LANES), jnp.float32),
            jax.ShapeDtypeStruct((t // TM_PROJ, N_EXPERTS, LANES), jnp.int32),
        ),
        scratch_shapes=[pltpu.VMEM((D_MODEL, D_MODEL), jnp.bfloat16)],
        compiler_params=pltpu.CompilerParams(dimension_semantics=("arbitrary",),
                                             vmem_limit_bytes=VMEM_LIMIT),
        name="out_proj_router",
    )(x2, att, hm, wo, g, wrt, brt)


def _dispatch_kernel(dst_ref, nz_ref, zdst_ref, t_ref, col_ref, row_ref, xs_hbm,
                     xbuf, zx, sem_x, sem_z):
    i = pl.program_id(0)
    n = pl.num_programs(0)
    p = i % 2
    tm = t_ref.shape[0]

    def copy(tile, par, j):
        d = pl.ds(pl.multiple_of(dst_ref[tile, j], SEG_ALIGN), SEG_ALIGN)
        return pltpu.make_async_copy(xbuf.at[par, j * SEG_ALIGN:(j + 1) * SEG_ALIGN], xs_hbm.at[d],
                                     sem_x.at[par])

    def zero_copy(e, j):
        d = pl.ds(pl.multiple_of(zdst_ref[e] + j * SEG_ALIGN, SEG_ALIGN), SEG_ALIGN)
        return pltpu.make_async_copy(zx, xs_hbm.at[d], sem_z.at[0])

    def for_zero_chunks(fn):
        for e in range(N_EXPERTS):
            def body(j, c, e=e):
                fn(zero_copy(e, j))
                return c
            lax.fori_loop(0, nz_ref[e], body, 0)

    def wait_tile(par):
        pltpu.make_async_copy(xbuf.at[par], xs_hbm.at[pl.ds(0, L_CAP)], sem_x.at[par]).wait()

    @pl.when(i == 0)
    def _():
        zx[...] = jnp.zeros(zx.shape, zx.dtype)
        for_zero_chunks(lambda cp: cp.start())

    @pl.when(i >= 2)
    def _():
        wait_tile(p)

    pos0 = row_ref[ROW_POS0:ROW_POS0 + 1, :].astype(jnp.int32)
    pos1 = row_ref[ROW_POS1:ROW_POS1 + 1, :].astype(jnp.int32)
    t = t_ref[...]
    side = _bf16(col_ref[...])
    rows = DISPATCH_ROWS
    for c in range(L_CAP // rows):
        r = lax.broadcasted_iota(jnp.int32, (rows, tm), 0) + c * rows
        perm = jnp.logical_or(r == pos0, r == pos1).astype(jnp.bfloat16)
        xbuf[p, c * rows:(c + 1) * rows, 0:D_MODEL] = _bf16(
            jnp.dot(perm, t, preferred_element_type=jnp.float32))
        xbuf[p, c * rows:(c + 1) * rows, D_MODEL:D_XS] = _bf16(
            jnp.dot(perm, side, preferred_element_type=jnp.float32))
        for j in range(c * rows // SEG_ALIGN, (c + 1) * rows // SEG_ALIGN):
            copy(i, p, j).start()

    @pl.when(i == n - 1)
    def _():
        @pl.when(n >= 2)
        def _():
            wait_tile(1 - p)
        wait_tile(p)
        for_zero_chunks(lambda cp: cp.wait())


def _dispatch(t, col, row, dst, nz, zdst):
    n_tok = t.shape[0]
    r_cap, _ = _moe_capacity(n_tok)
    tm = TM_PROJ
    grid_spec = pltpu.PrefetchScalarGridSpec(
        num_scalar_prefetch=3,
        grid=(n_tok // tm,),
        in_specs=[
            pl.BlockSpec((tm, D_MODEL), lambda i, *_: (i, 0)),
            pl.BlockSpec((tm, LANES), lambda i, *_: (i, 0)),
            pl.BlockSpec((SUBLANES, tm), lambda i, *_: (0, i)),
        ],
        out_specs=pl.BlockSpec(memory_space=pl.ANY),
        scratch_shapes=[
            pltpu.VMEM((2, L_CAP, D_XS), jnp.bfloat16),
            pltpu.VMEM((SEG_ALIGN, D_XS), jnp.bfloat16),
            pltpu.SemaphoreType.DMA((2,)),
            pltpu.SemaphoreType.DMA((1,)),
        ],
    )
    return pl.pallas_call(
        _dispatch_kernel,
        grid_spec=grid_spec,
        out_shape=jax.ShapeDtypeStruct((r_cap + DUMP_ROWS, D_XS), jnp.bfloat16),
        compiler_params=pltpu.CompilerParams(dimension_semantics=("arbitrary",),
                                             vmem_limit_bytes=VMEM_LIMIT),
        name="moe_dispatch",
    )(dst, nz, zdst, t, col, row)


def _moe_kernel(layer, blk_e_ref, nused_ref, nvalid_ref, next_e_ref, xs_ref, wg_hbm, wu_hbm, wd_hbm,
                ys_ref, wg_st, wu_st, wd_st, wg_bf, wu_bf, wd_bf, sem):
    b = pl.program_id(0)
    nv = nvalid_ref[b]
    e = blk_e_ref[b]
    new_expert = jnp.logical_or(b == 0, e != blk_e_ref[jnp.maximum(b - 1, 0)])

    def fetch(expert):
        return (pltpu.make_async_copy(wg_hbm.at[layer, expert], wg_st, sem.at[0]),
                pltpu.make_async_copy(wu_hbm.at[layer, expert], wu_st, sem.at[1]),
                pltpu.make_async_copy(wd_hbm.at[layer, expert], wd_st, sem.at[2]))

    @pl.when(b == 0)
    def _():
        for cp in fetch(e):
            cp.start()

    @pl.when(new_expert)
    def _():
        for cp in fetch(e):
            cp.wait()
        wg_bf[...] = _bf16(wg_st[...])
        wu_bf[...] = _bf16(wu_st[...])
        wd_bf[...] = _bf16(wd_st[...])

        @pl.when(next_e_ref[b] != e)
        def _():
            for cp in fetch(next_e_ref[b]):
                cp.start()

    def compute(rows):
        x = xs_ref[0:rows, 0:D_MODEL]
        y = None
        fw = D_FF_EXPERT // MOE_FF_SPLIT
        for c in range(MOE_FF_SPLIT):
            cols = slice(c * fw, (c + 1) * fw)
            a = jnp.dot(x, wg_bf[:, cols], preferred_element_type=jnp.float32)
            u = jnp.dot(x, wu_bf[:, cols], preferred_element_type=jnp.float32)
            h = a * jax.nn.sigmoid(a) * u
            yc = jnp.dot(_bf16(h), wd_bf[cols, :], preferred_element_type=jnp.float32)
            y = yc if y is None else y + yc
        sd = xs_ref[0:rows, D_MODEL:D_XS].astype(jnp.float32)
        e_blk = blk_e_ref[b].astype(jnp.float32)
        w0 = sd[:, COL_W0H:COL_W0H + 1] + sd[:, COL_W0H + 1:COL_W0H + 2] + sd[:, COL_W0H + 2:COL_W0H + 3]
        w1 = sd[:, COL_W1H:COL_W1H + 1] + sd[:, COL_W1H + 1:COL_W1H + 2] + sd[:, COL_W1H + 2:COL_W1H + 3]
        w = jnp.where(sd[:, COL_E0:COL_E0 + 1] == e_blk, w0, w1)
        ys_ref[0:rows, :] = _bf16(y * w)

    half = MOE_BM // 2

    @pl.when(nv > half)
    def _():
        compute(MOE_BM)

    @pl.when(jnp.logical_and(nv > 0, nv <= half))
    def _():
        compute(half)
        ys_ref[half:, :] = jnp.zeros((MOE_BM - half, D_MODEL), ys_ref.dtype)


def _moe(xs, blk_e, nused, nvalid, next_e, wg, wu, wd, layer):
    blk = lambda b, be, nu, nv, ne: (jnp.maximum(jnp.minimum(b, nu[0] - 1), 0), 0)
    r_cap = xs.shape[0] - DUMP_ROWS
    grid_spec = pltpu.PrefetchScalarGridSpec(
        num_scalar_prefetch=4,
        grid=(r_cap // MOE_BM,),
        in_specs=[
            pl.BlockSpec((MOE_BM, D_XS), blk),
            pl.BlockSpec(memory_space=pl.ANY),
            pl.BlockSpec(memory_space=pl.ANY),
            pl.BlockSpec(memory_space=pl.ANY),
        ],
        out_specs=pl.BlockSpec((MOE_BM, D_MODEL), blk),
        scratch_shapes=[
            pltpu.VMEM((D_MODEL, D_FF_EXPERT), jnp.float32),
            pltpu.VMEM((D_MODEL, D_FF_EXPERT), jnp.float32),
            pltpu.VMEM((D_FF_EXPERT, D_MODEL), jnp.float32),
            pltpu.VMEM((D_MODEL, D_FF_EXPERT), jnp.bfloat16),
            pltpu.VMEM((D_MODEL, D_FF_EXPERT), jnp.bfloat16),
            pltpu.VMEM((D_FF_EXPERT, D_MODEL), jnp.bfloat16),
            pltpu.SemaphoreType.DMA((3,)),
        ],
    )
    return pl.pallas_call(
        functools.partial(_moe_kernel, layer),
        grid_spec=grid_spec,
        out_shape=jax.ShapeDtypeStruct((r_cap, D_MODEL), jnp.bfloat16),
        compiler_params=pltpu.CompilerParams(dimension_semantics=("arbitrary",),
                                             vmem_limit_bytes=VMEM_LIMIT),
        name="moe_experts",
    )(blk_e, nused, nvalid, next_e, xs, wg, wu, wd)


def _combine_kernel(src_ref, x1_ref, col_ref, ys_hbm, out_ref, ybuf, sem):
    i = pl.program_id(0)
    n = pl.num_programs(0)
    p = i % COMBINE_BUFS
    tm = x1_ref.shape[0]

    def start_tile(tile, par):
        for j in range(N_CHUNK):
            s = pl.ds(pl.multiple_of(src_ref[tile, j], SEG_ALIGN), SEG_ALIGN)
            pltpu.make_async_copy(ys_hbm.at[s], ybuf.at[par, j * SEG_ALIGN:(j + 1) * SEG_ALIGN],
                                  sem.at[par]).start()

    def wait_buf(par):
        pltpu.make_async_copy(ys_hbm.at[pl.ds(0, L_CAP)], ybuf.at[par], sem.at[par]).wait()

    @pl.when(i == 0)
    def _():
        start_tile(0, 0)
        start_tile(jnp.minimum(1, n - 1), 1)

    wait_buf(p)
    start_tile(jnp.minimum(i + 2, n - 1), (i + 2) % COMBINE_BUFS)

    rows = COMBINE_ROWS
    for c in range(tm // rows):
        col = col_ref[c * rows:(c + 1) * rows, :]
        pos0 = col[:, ROW_POS0:ROW_POS0 + 1].astype(jnp.int32)
        pos1 = col[:, ROW_POS1:ROW_POS1 + 1].astype(jnp.int32)
        l = lax.broadcasted_iota(jnp.int32, (rows, L_CAP), 1)
        perm = jnp.logical_or(l == pos0, l == pos1).astype(jnp.bfloat16)
        y = jnp.dot(perm, ybuf[p], preferred_element_type=jnp.float32)
        out_ref[c * rows:(c + 1) * rows, :] = x1_ref[c * rows:(c + 1) * rows, :] + y

    @pl.when(i == n - 1)
    def _():
        wait_buf((i + 1) % COMBINE_BUFS)
        wait_buf((i + 2) % COMBINE_BUFS)


def _combine(x1, col, ys, src):
    n_tok = x1.shape[0]
    tm = TM_PROJ
    grid_spec = pltpu.PrefetchScalarGridSpec(
        num_scalar_prefetch=1,
        grid=(n_tok // tm,),
        in_specs=[
            pl.BlockSpec((tm, D_MODEL), lambda i, *_: (i, 0)),
            pl.BlockSpec((tm, LANES), lambda i, *_: (i, 0)),
            pl.BlockSpec(memory_space=pl.ANY),
        ],
        out_specs=pl.BlockSpec((tm, D_MODEL), lambda i, *_: (i, 0)),
        scratch_shapes=[
            pltpu.VMEM((COMBINE_BUFS, L_CAP, D_MODEL), jnp.bfloat16),
            pltpu.SemaphoreType.DMA((COMBINE_BUFS,)),
        ],
    )
    return pl.pallas_call(
        _combine_kernel,
        grid_spec=grid_spec,
        out_shape=jax.ShapeDtypeStruct((n_tok, D_MODEL), jnp.float32),
        compiler_params=pltpu.CompilerParams(dimension_semantics=("arbitrary",),
                                             vmem_limit_bytes=VMEM_LIMIT),
        name="moe_combine",
    )(src, x1, col, ys)


def _routing_tables(counts, n_blk_cap):
    r_cap = n_blk_cap * MOE_BM
    i32 = jnp.int32
    ca = ((counts + (SEG_ALIGN - 1)) // SEG_ALIGN) * SEG_ALIGN
    lend = jnp.cumsum(ca, axis=1)
    lstart = lend - ca
    tot = jnp.sum(ca, axis=0)
    region = ((tot + (MOE_BM - 1)) // MOE_BM) * MOE_BM
    rend = jnp.cumsum(region)
    base = rend - region
    gstart = base[None, :] + jnp.cumsum(ca, axis=0) - ca
    j16 = jnp.arange(N_CHUNK, dtype=i32) * SEG_ALIGN
    in_seg = jnp.logical_and(lstart[:, None, :] <= j16[None, :, None], j16[None, :, None] < lend[:, None, :])
    seg_row = (jnp.sum(jnp.where(in_seg, (gstart - lstart)[:, None, :], 0), axis=2) + j16[None, :]).astype(i32)
    used = j16[None, :] < lend[:, -1:]
    tile_par = (jnp.arange(counts.shape[0], dtype=i32) % 2)[:, None]
    park = r_cap + (tile_par * N_CHUNK + jnp.arange(N_CHUNK, dtype=i32)[None, :]) * SEG_ALIGN
    dst = jnp.where(used, seg_row, park).astype(i32)
    src = jnp.where(used, seg_row, 0).astype(i32)
    nused = (rend[-1] // MOE_BM).astype(i32)
    brow = jnp.minimum(jnp.arange(n_blk_cap, dtype=i32), nused - 1) * MOE_BM
    blk_e = jnp.minimum(jnp.sum(rend[None, :] <= brow[:, None], axis=1), N_EXPERTS - 1).astype(i32)
    zstart = (base + tot).astype(i32)
    nzc = ((region - tot) // SEG_ALIGN).astype(i32)
    blk_i = jnp.arange(n_blk_cap, dtype=i32)
    seg_end = jnp.sum(jnp.where(blk_e[:, None] == jnp.arange(N_EXPERTS)[None, :], (base + tot)[None, :], 0), axis=1)
    nvalid = jnp.where(blk_i < nused, jnp.clip(seg_end - blk_i * MOE_BM, 0, MOE_BM), 0).astype(i32)
    eid = jnp.arange(N_EXPERTS, dtype=i32)
    later = jnp.logical_and(eid[None, :] > eid[:, None], (region > 0)[None, :])
    nxt = jnp.min(jnp.where(later, eid[None, :], N_EXPERTS), axis=1)
    nxt = jnp.where(nxt < N_EXPERTS, nxt, eid)
    next_e = jnp.sum(jnp.where(blk_e[:, None] == eid[None, :], nxt[None, :], 0), axis=1).astype(i32)
    return dst, src, nzc, zstart, blk_e, nused[None], nvalid, next_e


def _t5_bucket_np(dist):
    max_exact = N_BUCKETS // 2
    d = np.maximum(dist, 1).astype(np.float32)
    large = max_exact + (np.log(d / max_exact) / math.log(MAX_DISTANCE / max_exact)
                         * (N_BUCKETS - max_exact)).astype(np.int32)
    large = np.minimum(large, N_BUCKETS - 1)
    return np.where(dist < max_exact, dist, large)


def _attention_bias(rel_bias):
    qi = np.arange(ATT_BLOCK)[:, None]
    kj = np.arange(2 * ATT_BLOCK)[None, :]
    dist = qi + ATT_BLOCK - kj
    in_window = (dist >= 0) & (dist < WINDOW)
    bucket = _t5_bucket_np(np.clip(dist, 0, WINDOW - 1))
    onehot = (bucket[None] == np.arange(N_BUCKETS)[:, None, None]).astype(np.float32)
    bias = jnp.einsum('nh,nqk->hqk', rel_bias.astype(jnp.float32), jnp.asarray(onehot),
                      precision=lax.Precision.HIGHEST)
    bias = jnp.where(jnp.asarray(in_window)[None], bias * LOG2E, NEG_BIG)
    no_prev = jnp.asarray(kj < ATT_BLOCK)[None]
    return jnp.stack([bias, jnp.where(no_prev, NEG_BIG, bias)])


def _block_diag_mean(width, block):
    idx = np.arange(width) // block
    return jnp.asarray((idx[:, None] == idx[None, :]).astype(np.float32) / block, dtype=jnp.bfloat16)


def kernel(x, rel_bias, norm_mix_g, w_in, q_norm_g, k_norm_g, attn_sink, conv_w, conv_b, gate_b,
           mlstm_norm_g, w_out, norm_ffn_g, w_router_group, b_router_group, w_router_expert,
           b_router_expert, w_gate, w_up, w_down):
    batch, seq_len, _ = x.shape
    n_tok = batch * seq_len
    assert seq_len % TM_OUT == 0 and seq_len % ML_KCHUNK == 0
    f32 = jnp.float32
    bias = _attention_bias(rel_bias)
    bdq = _block_diag_mean(ATT_WIDTH, ATT_HEAD_DIM)
    bdk = _block_diag_mean(LANES, ATT_HEAD_DIM)
    x2 = x.reshape(n_tok, D_MODEL)
    for l in range(DEPTH):
        wgt = _bf16(w_in[l, :, O_G:N_IN].T)
        qg = (jnp.tile(q_norm_g[l].astype(f32), ATT_HEADS) * (ATT_HEAD_DIM ** -0.5 * LOG2E))[None, :]
        kg = jnp.tile(k_norm_g[l].astype(f32), ATT_KV_HEADS)[None, :]
        gbt = gate_b[l].astype(f32)[:, None]
        qn, kn, vd, hm = _front(
            x2.reshape(batch, seq_len, D_MODEL), norm_mix_g[l][None, :], w_in, l, wgt, qg, kg, bdq, bdk,
            conv_w[l], conv_b[l][None, :], gbt, mlstm_norm_g[l][None, :])
        flat = lambda a: a.reshape(n_tok, a.shape[-1])
        att = _attention(flat(qn), flat(kn), flat(vd), bias, attn_sink[l].astype(f32) * LOG2E,
                         batch, seq_len)
        n_rt = 4 * SUBLANES
        wrt = jnp.pad(jnp.concatenate([w_router_expert[l], w_router_group[l]], axis=1).astype(f32).T,
                      ((0, n_rt - N_EXPERTS - N_GROUPS), (0, 0)))
        brt = jnp.pad(jnp.concatenate([b_router_expert[l], b_router_group[l]]).astype(f32),
                      (0, n_rt - N_EXPERTS - N_GROUPS))[:, None]
        wrt_hi = _bf16(wrt)
        wrt = jnp.concatenate([wrt_hi, _bf16(wrt - wrt_hi.astype(f32))], axis=0)
        x1, t, row, col, cnt = _out_proj_router(x2, att, hm.reshape(n_tok, ML_WIDTH), w_out, l,
                                                norm_ffn_g[l][None, :], wrt, brt)
        _, n_blk_cap = _moe_capacity(n_tok)
        dst, src, nz, zdst, blk_e, nused, nvalid, next_e = _routing_tables(cnt[:, :, 0], n_blk_cap)
        xs = _dispatch(t, col, row, dst, nz, zdst)
        ys = _moe(xs, blk_e, nused, nvalid, next_e, w_gate, w_up, w_down, l)
        x2 = _combine(x1, col, ys, src)
    return x2.reshape(batch, seq_len, D_MODEL)
```

```python
import functools
import math

import jax
import jax.numpy as jnp
import numpy as np
from jax import lax
from jax.experimental import pallas as pl
from jax.experimental.pallas import tpu as pltpu

D_MODEL = 1024
DEPTH = 2
ATT_HEADS = 8
ATT_KV_HEADS = 2
ATT_HEAD_DIM = 64
ATT_WIDTH = ATT_HEADS * ATT_HEAD_DIM
WINDOW = 128
ATT_BLOCK = 128
N_BUCKETS = 32
MAX_DISTANCE = 128
ML_HEADS = 4
ML_DQK = 64
ML_DV = 128
ML_WIDTH = ML_HEADS * ML_DV
ML_CHUNK = 64
CONV_K = 4
N_GROUPS = 4
EXPERTS_PER_GROUP = 4
N_EXPERTS = N_GROUPS * EXPERTS_PER_GROUP
D_FF_EXPERT = 512
EPS = 1e-6

LANES = 128
SUBLANES = 8
NEG_BIG = -1e30
LOG2E = math.log2(math.e)
VMEM_LIMIT = 48 * 1024 * 1024
FRONT_VMEM_LIMIT = 56 * 1024 * 1024

O_Q = 0
O_K = O_Q + ATT_WIDTH
O_V = O_K + ATT_KV_HEADS * ATT_HEAD_DIM
O_QM = O_V + ATT_KV_HEADS * ATT_HEAD_DIM
O_KM = O_QM + ML_HEADS * ML_DQK
O_VM = O_KM + ML_HEADS * ML_DQK
O_OM = O_VM + ML_WIDTH
O_G = O_OM + ML_WIDTH
N_IN = O_G + 2 * ML_HEADS

ATT_STEP_BLOCKS = 4
TM_OUT = 1024
TM_PROJ = 512
ML_KCHUNK = 256

SEG_ALIGN = 16
MOE_BM = 512
MOE_FF_SPLIT = 2
L_CAP = 2 * TM_PROJ + N_EXPERTS * SEG_ALIGN
N_CHUNK = L_CAP // SEG_ALIGN
D_XS = D_MODEL + LANES
DUMP_ROWS = 2 * N_CHUNK * SEG_ALIGN
DISPATCH_ROWS = 256
COMBINE_ROWS = 256
COMBINE_BUFS = 3
ROW_POS0, ROW_POS1, ROW_W0, ROW_W1, ROW_E0, ROW_E1 = 0, 1, 2, 3, 4, 5
COL_W0H, COL_W1H, COL_E0, COL_E1 = 8, 11, 14, 15


def _moe_capacity(n_tok):
    n_tiles = n_tok // TM_PROJ
    rows = 2 * n_tok + n_tiles * N_EXPERTS * (SEG_ALIGN - 1) + N_EXPERTS * (MOE_BM - SEG_ALIGN)
    n_blk = -(-rows // MOE_BM)
    return n_blk * MOE_BM, n_blk


def _bf16(a):
    return a.astype(jnp.bfloat16)


def _split3(a):
    hi = _bf16(a)
    r1 = a - hi.astype(jnp.float32)
    mid = _bf16(r1)
    return hi, mid, _bf16(r1 - mid.astype(jnp.float32))


def _log_sigmoid(z):
    return jnp.minimum(z, 0.0) - jnp.log(1.0 + jnp.exp(-jnp.abs(z)))


def _front_kernel(x_ref, g_ref, w_ref, wgt_ref, qg_ref, kg_ref, bdq_ref, bdk_ref, cw_ref, cb_ref,
                  gbt_ref, ng_ref,
                  qn_ref, kn_ref, vd_ref, hm_ref,
                  w_bf, conv_scr, qkm_s, vm_s, om_s, gt_s, c_scr, m_scr):
    s = pl.program_id(0)
    batch = x_ref.shape[0]
    L = ML_KCHUNK
    wr = s % 2
    rd = 1 - wr

    n_chunks = pl.num_programs(0) - 1

    @pl.when(s == 0)
    def _():
        w_bf[...] = _bf16(w_ref[...])
        conv_scr[:, 0:SUBLANES, :] = jnp.zeros((batch, SUBLANES, conv_scr.shape[2]), jnp.float32)
        c_scr[...] = jnp.zeros(c_scr.shape, jnp.float32)
        m_scr[...] = jnp.zeros(m_scr.shape, jnp.float32)

    @pl.when(s > 0)
    def _():
        conv_scr[:, 0:SUBLANES, :] = conv_scr[:, L:L + SUBLANES, :]

    def scan_pieces():
        return _mlstm_pieces(qkm_s.at[rd], vm_s.at[rd], om_s.at[rd], gt_s.at[rd], ng_ref, hm_ref,
                             c_scr, m_scr, batch)

    def proj_pieces():
        return _project_pieces(x_ref, g_ref, w_bf, wgt_ref, qg_ref, kg_ref, bdq_ref, bdk_ref, cw_ref,
                               cb_ref, gbt_ref, qn_ref, kn_ref, vd_ref, qkm_s.at[wr], vm_s.at[wr],
                               om_s.at[wr], gt_s.at[wr], conv_scr)

    @pl.when(s == 0)
    def _():
        norm_part, proj_parts = proj_pieces()
        for b in range(batch):
            hb = norm_part(b)
            for part in proj_parts:
                part(b, hb)

    @pl.when(jnp.logical_and(s > 0, s < n_chunks))
    def _():
        scan_prep, scan_head, scan_finish = scan_pieces()
        norm_part, proj_parts = proj_pieces()
        for b in range(batch):
            prep = scan_prep(b)
            scan_head(b, 0, prep)
            scan_head(b, 1, prep)
            hb = norm_part(b)
            proj_parts[0](b, hb)
            proj_parts[1](b, hb)
            scan_head(b, 2, prep)
            scan_head(b, 3, prep)
            for part in proj_parts[2:]:
                part(b, hb)
        scan_finish()

    @pl.when(s == n_chunks)
    def _():
        scan_prep, scan_head, scan_finish = scan_pieces()
        for b in range(batch):
            prep = scan_prep(b)
            for h in range(ML_HEADS):
                scan_head(b, h, prep)
        scan_finish()


def _project_pieces(x_ref, g_ref, w_bf, wgt_ref, qg_ref, kg_ref, bdq_ref, bdk_ref, cw_ref, cb_ref,
                    gbt_ref, qn_ref, kn_ref, vd_ref, qkm_ref, vm_ref, om_ref, gatet_ref, conv_scr):
    L = ML_KCHUNK
    conv_w = 2 * ML_HEADS * ML_DQK

    def proj(hb, c0, width):
        return jnp.dot(hb, w_bf[:, c0:c0 + width], preferred_element_type=jnp.float32)

    def norm_part(b):
        x = x_ref[b]
        hn = x * lax.rsqrt(jnp.mean(x * x, axis=-1, keepdims=True) + EPS) * g_ref[...]
        return _bf16(hn)

    def q_part(b, hb):
        q = proj(hb, O_Q, ATT_WIDTH)
        q_ms = jnp.dot(_bf16(q * q), bdq_ref[...], preferred_element_type=jnp.float32)
        qn_ref[b] = _bf16(q * lax.rsqrt(q_ms + EPS) * qg_ref[...])

    def kv_part(b, hb):
        kv = proj(hb, O_K, 2 * LANES)
        k = kv[:, 0:LANES]
        v = kv[:, LANES:2 * LANES]
        k_ms = jnp.dot(_bf16(k * k), bdk_ref[...], preferred_element_type=jnp.float32)
        kn = k * lax.rsqrt(k_ms + EPS) * kg_ref[...]
        low = lax.broadcasted_iota(jnp.int32, kn.shape, 1) < ATT_HEAD_DIM

        def dup_heads(a):
            swapped = pltpu.roll(a, ATT_HEAD_DIM, axis=1)
            return jnp.concatenate([jnp.where(low, a, swapped), jnp.where(low, swapped, a)], axis=1)

        kn_ref[b] = _bf16(dup_heads(kn))
        vd_ref[b] = _bf16(dup_heads(v))

    def conv_part(b, hb):
        qk = proj(hb, O_QM, conv_w)
        conv_scr[b, SUBLANES:SUBLANES + L, :] = qk
        y = qk * cw_ref[CONV_K - 1:CONV_K, :] + cb_ref[...]
        for j in range(CONV_K - 1):
            off = SUBLANES - (CONV_K - 1) + j
            y = y + conv_scr[b, off:off + L, :] * cw_ref[j:j + 1, :]
        y = y * jax.nn.sigmoid(y)
        lane = lax.broadcasted_iota(jnp.int32, y.shape, 1)
        y = jnp.where(lane >= ML_HEADS * ML_DQK, y * (ML_DQK ** -0.5), y)
        qkm_ref[b] = _bf16(y)

    def vm_part(b, hb):
        vm_ref[b] = _bf16(proj(hb, O_VM, ML_WIDTH))

    def om_part(b, hb):
        om_ref[b] = _bf16(jax.nn.sigmoid(proj(hb, O_OM, ML_WIDTH)))

        gt = lax.dot_general(wgt_ref[...], hb, (((1,), (1,)), ((), ())),
                             preferred_element_type=jnp.float32) + gbt_ref[...]
        grow = lax.broadcasted_iota(jnp.int32, gt.shape, 0)
        gatet_ref[b] = jnp.where(grow >= ML_HEADS, _log_sigmoid(gt), gt) * LOG2E

    return norm_part, (q_part, kv_part, conv_part, vm_part, om_part)


def _front(x3, g, w_in, layer, wgt, qg, kg, bdq, bdk, cw, cb, gbt, ng):
    batch, seq_len, _ = x3.shape
    L = ML_KCHUNK
    nc = seq_len // L
    fix = lambda s: (0, 0)
    cur = lambda s: (0, jnp.minimum(s, nc - 1), 0)
    prev = lambda s: (0, jnp.maximum(s - 1, 0), 0)
    kv_w = 2 * ATT_KV_HEADS * ATT_HEAD_DIM
    conv_w = 2 * ML_HEADS * ML_DQK
    return pl.pallas_call(
        _front_kernel,
        grid=(nc + 1,),
        in_specs=[
            pl.BlockSpec((batch, L, D_MODEL), cur),
            pl.BlockSpec((1, D_MODEL), fix),
            pl.BlockSpec((None, D_MODEL, N_IN), lambda s: (layer, 0, 0), pipeline_mode=pl.Buffered(1)),
            pl.BlockSpec((SUBLANES, D_MODEL), fix),
            pl.BlockSpec((1, ATT_WIDTH), fix),
            pl.BlockSpec((1, LANES), fix),
            pl.BlockSpec((ATT_WIDTH, ATT_WIDTH), fix),
            pl.BlockSpec((LANES, LANES), fix),
            pl.BlockSpec((CONV_K, conv_w), fix),
            pl.BlockSpec((1, conv_w), fix),
            pl.BlockSpec((SUBLANES, 1), fix),
            pl.BlockSpec((1, ML_WIDTH), fix),
        ],
        out_specs=[
            pl.BlockSpec((batch, L, ATT_WIDTH), cur),
            pl.BlockSpec((batch, L, kv_w), cur),
            pl.BlockSpec((batch, L, kv_w), cur),
            pl.BlockSpec((batch, L, ML_WIDTH), prev),
        ],
        out_shape=(
            jax.ShapeDtypeStruct((batch, seq_len, ATT_WIDTH), jnp.bfloat16),
            jax.ShapeDtypeStruct((batch, seq_len, kv_w), jnp.bfloat16),
            jax.ShapeDtypeStruct((batch, seq_len, kv_w), jnp.bfloat16),
            jax.ShapeDtypeStruct((batch, seq_len, ML_WIDTH), jnp.bfloat16),
        ),
        scratch_shapes=[
            pltpu.VMEM((D_MODEL, N_IN), jnp.bfloat16),
            pltpu.VMEM((batch, L + 2 * SUBLANES, conv_w), jnp.float32),
            pltpu.VMEM((2, batch, L, conv_w), jnp.bfloat16),
            pltpu.VMEM((2, batch, L, ML_WIDTH), jnp.bfloat16),
            pltpu.VMEM((2, batch, L, ML_WIDTH), jnp.bfloat16),
            pltpu.VMEM((2, batch, SUBLANES, L), jnp.float32),
            pltpu.VMEM((batch * ML_HEADS // 2, 2 * ML_DQK, 2 * ML_DV), jnp.float32),
            pltpu.VMEM((batch * ML_HEADS, 1, LANES), jnp.float32),
        ],
        compiler_params=pltpu.CompilerParams(dimension_semantics=("arbitrary",),
                                             vmem_limit_bytes=FRONT_VMEM_LIMIT),
        name="proj_mlstm",
    )(x3, g, w_in, wgt, qg, kg, bdq, bdk, cw, cb, gbt, ng)


def _attn_kernel(sink_ref, q_ref, kp_ref, kc_ref, vp_ref, vc_ref, bias_ref, o_ref):
    i = pl.program_id(1)
    blk = ATT_BLOCK
    lane = lax.broadcasted_iota(jnp.int32, (blk, LANES), 1)
    low = lane < ATT_HEAD_DIM
    group = ATT_HEADS // ATT_KV_HEADS
    ones = jnp.ones((2 * blk, LANES), jnp.bfloat16)
    for j in range(ATT_STEP_BLOCKS):
        rows = slice(j * blk, (j + 1) * blk)
        first = (i == 0).astype(jnp.int32) if j == 0 else 0
        for kv in range(ATT_KV_HEADS):
            lanes = slice(kv * LANES, (kv + 1) * LANES)
            if j == 0:
                kcat = jnp.concatenate([kp_ref[:, lanes], kc_ref[0:blk, lanes]], axis=0)
                vrows = jnp.concatenate([vp_ref[:, lanes], vc_ref[0:blk, lanes]], axis=0)
            else:
                kcat = kc_ref[(j - 1) * blk:(j + 1) * blk, lanes]
                vrows = vc_ref[(j - 1) * blk:(j + 1) * blk, lanes]
            vcat = jnp.concatenate([vrows, ones], axis=1)
            heads = range(kv * group, (kv + 1) * group)
            qs = []
            for h in heads:
                qp = q_ref[rows, (h // 2) * LANES:(h // 2 + 1) * LANES]
                qs.append(jnp.where(low if h % 2 == 0 else jnp.logical_not(low), qp, jnp.zeros_like(qp)))
            s_all = lax.dot_general(jnp.concatenate(qs, axis=0), kcat, (((1,), (1,)), ((), ())),
                                    preferred_element_type=jnp.float32)
            ps, ms = [], []
            for n, h in enumerate(heads):
                logits = s_all[n * blk:(n + 1) * blk, :] + bias_ref[first, h]
                row_max = jnp.broadcast_to(jnp.max(logits, axis=-1, keepdims=True), (blk, LANES))
                m = jnp.maximum(row_max, sink_ref[h])
                ps.append(_bf16(jnp.exp2(logits - jnp.concatenate([m, m], axis=1))))
                ms.append(m)
            o_all = jnp.dot(jnp.concatenate(ps, axis=0), vcat, preferred_element_type=jnp.float32)
            outs = []
            for n, h in enumerate(heads):
                o = o_all[n * blk:(n + 1) * blk, :]
                den = o[:, LANES:2 * LANES] + jnp.exp2(sink_ref[h] - ms[n])
                outs.append(o[:, 0:LANES] / den)
            for n in range(0, group, 2):
                pair = (kv * group + n) // 2
                o_ref[rows, pair * LANES:(pair + 1) * LANES] = _bf16(
                    jnp.where(low, outs[n], outs[n + 1]))


def _attention(qn, kn, vd, bias, sink, batch, seq_len):
    step = ATT_STEP_BLOCKS * ATT_BLOCK
    ns = seq_len // step
    cur = lambda b, i, s: (b * ns + i, 0)
    prev = lambda b, i, s: ((b * ns + i) * ATT_STEP_BLOCKS - jnp.minimum(i, 1), 0)
    grid_spec = pltpu.PrefetchScalarGridSpec(
        num_scalar_prefetch=1,
        grid=(batch, ns),
        in_specs=[
            pl.BlockSpec((step, ATT_WIDTH), cur),
            pl.BlockSpec((ATT_BLOCK, 2 * LANES), prev),
            pl.BlockSpec((step, 2 * LANES), cur),
            pl.BlockSpec((ATT_BLOCK, 2 * LANES), prev),
            pl.BlockSpec((step, 2 * LANES), cur),
            pl.BlockSpec((2, ATT_HEADS, ATT_BLOCK, 2 * ATT_BLOCK), lambda b, i, s: (0, 0, 0, 0)),
        ],
        out_specs=pl.BlockSpec((step, ATT_WIDTH), cur),
    )
    return pl.pallas_call(
        _attn_kernel,
        grid_spec=grid_spec,
        out_shape=jax.ShapeDtypeStruct((batch * seq_len, ATT_WIDTH), jnp.bfloat16),
        compiler_params=pltpu.CompilerParams(dimension_semantics=("arbitrary", "arbitrary"),
                                             vmem_limit_bytes=VMEM_LIMIT),
        name="swa_attention",
    )(sink, qn, kn, kn, vd, vd, bias)


def _mlstm_pieces(qk_ref, v_ref, o_ref, gt_ref, ng_ref, out_ref, c_scr, m_scr, batch):
    L = ML_KCHUNK
    pairs = ML_HEADS // 2
    r_i = lax.broadcasted_iota(jnp.int32, (L, L), 0)
    c_i = lax.broadcasted_iota(jnp.int32, (L, L), 1)
    causal = c_i <= r_i
    tril_bf = causal.astype(jnp.bfloat16)
    triu_bf = (r_i <= c_i).astype(jnp.bfloat16)
    mean_dv = jnp.full((ML_DV, ML_DV), 1.0 / ML_DV, jnp.bfloat16)
    lane = lax.broadcasted_iota(jnp.int32, (L, LANES), 1)
    low = lane < ML_DQK
    ones_dv = jnp.ones((L, ML_DV), jnp.bfloat16)
    row2 = lax.broadcasted_iota(jnp.int32, (2 * ML_DQK, 2 * ML_DV), 0)

    def twice(a):
        return jnp.concatenate([a, a], axis=1)

    m_state = [m_scr[k] for k in range(batch * ML_HEADS)]
    c_state = [c_scr[k] for k in range(batch * pairs)]

    def batch_prep(b):
        gt = gt_ref[b]
        g = jnp.concatenate([gt, jnp.zeros((LANES - SUBLANES, L), jnp.float32)], axis=0).T
        bcols = sum(jnp.dot(tril_bf, part, preferred_element_type=jnp.float32) for part in _split3(g))
        brows = sum(jnp.dot(part, triu_bf, preferred_element_type=jnp.float32) for part in _split3(gt))
        return gt, g, bcols, brows

    carry = {}

    def head_piece(b, h, prep):
        gt, g, bcols, brows = prep
        pair, sub = divmod(h, 2)
        sidx = b * pairs + pair
        qp = qk_ref[b, :, pair * LANES:(pair + 1) * LANES]
        kp = qk_ref[b, :, (pairs + pair) * LANES:(pairs + pair + 1) * LANES]
        c_pair = c_state[sidx]
        c_bf = _bf16(c_pair)
        sel = low if sub == 0 else jnp.logical_not(low)
        m_prev = m_state[b * ML_HEADS + h]
        bc = jnp.broadcast_to(bcols[:, ML_HEADS + h:ML_HEADS + h + 1], (L, LANES))
        lic = jnp.broadcast_to(g[:, h:h + 1], (L, LANES))
        br = brows[ML_HEADS + h:ML_HEADS + h + 1, :]
        lir = gt[h:h + 1, :]
        log_d = jnp.where(causal, twice(bc) - (br - lir), NEG_BIG)
        m_inter = bc + m_prev
        row_max = jnp.broadcast_to(jnp.max(log_d, axis=-1, keepdims=True), (L, LANES))
        m_row = jnp.maximum(m_inter, row_max)
        d = jnp.exp2(log_d - twice(m_row))
        inter = jnp.exp2(m_inter - m_row)
        qm = jnp.where(sel, qp, jnp.zeros_like(qp))
        s = lax.dot_general(qm, kp, (((1,), (1,)), ((), ())),
                            preferred_element_type=jnp.float32) * d
        v_ext = jnp.concatenate([v_ref[b, :, h * ML_DV:(h + 1) * ML_DV], ones_dv],
                                axis=-1)
        num = twice(inter) * jnp.dot(qm, c_bf, preferred_element_type=jnp.float32) \
            + jnp.dot(_bf16(s), v_ext, preferred_element_type=jnp.float32)
        den = num[:, ML_DV:2 * ML_DV]
        hval = num[:, 0:ML_DV] / jnp.maximum(jnp.abs(den), jnp.exp2(-m_row))
        h_ms = jnp.dot(_bf16(hval * hval), mean_dv, preferred_element_type=jnp.float32)
        hn = hval * lax.rsqrt(h_ms + EPS)
        hn = hn * ng_ref[:, h * ML_DV:(h + 1) * ML_DV]
        out_ref[b, :, h * ML_DV:(h + 1) * ML_DV] = _bf16(
            hn * o_ref[b, :, h * ML_DV:(h + 1) * ML_DV].astype(jnp.float32))
        b_last = bc[L - 1:L, :]
        log_w = b_last - bc + lic
        m_next = jnp.maximum(b_last + m_prev, jnp.max(log_w, axis=0, keepdims=True))
        w = jnp.exp2(log_w - m_next)
        decay = jnp.exp2(b_last + m_prev - m_next)
        m_state[b * ML_HEADS + h] = m_next
        kw = _bf16(jnp.where(sel, kp.astype(jnp.float32) * w, 0.0))
        upd = lax.dot_general(kw, v_ext, (((0,), (0,)), ((), ())),
                              preferred_element_type=jnp.float32)
        if sub == 0:
            carry[sidx] = (decay, upd)
        else:
            decay0, upd0 = carry.pop(sidx)
            decay_rows = jnp.where(row2 < ML_DQK, twice(decay0), twice(decay))
            c_state[sidx] = decay_rows * c_pair + (upd0 + upd)

    def finish():
        for k in range(batch * ML_HEADS):
            m_scr[k] = m_state[k]
        for k in range(batch * pairs):
            c_scr[k] = c_state[k]

    return batch_prep, head_piece, finish


def _out_proj_router_kernel(x_ref, att_ref, hm_ref, wo_ref, g_ref, wrt_ref, brt_ref,
                            x1_ref, t_ref, row_ref, col_ref, cnt_ref, wo_bf):
    tm = x_ref.shape[0]
    tr = TM_PROJ
    hi = lax.Precision.HIGHEST

    @pl.when(pl.program_id(0) == 0)
    def _():
        wo_bf[...] = _bf16(wo_ref[...])

    for sb in range(tm // TM_PROJ):
        rs = slice(sb * TM_PROJ, (sb + 1) * TM_PROJ)
        mixed = jnp.concatenate([att_ref[rs, :], hm_ref[rs, :]], axis=1)
        x1 = x_ref[rs, :] + jnp.dot(mixed, wo_bf[...], preferred_element_type=jnp.float32)
        x1_ref[rs, :] = x1
        tn = x1 * lax.rsqrt(jnp.mean(x1 * x1, axis=-1, keepdims=True) + EPS) * g_ref[...]
        tn_hi = _bf16(tn)
        t_ref[rs, :] = tn_hi
        tn_lo = _bf16(tn - tn_hi.astype(jnp.float32))
        nt = (((1,), (1,)), ((), ()))
        n_rt = wrt_ref.shape[0] // 2
        p_hi = lax.dot_general(wrt_ref[...], tn_hi, nt, preferred_element_type=jnp.float32)
        p_lo = lax.dot_general(wrt_ref[0:n_rt, :], tn_lo, nt, preferred_element_type=jnp.float32)
        logits = p_hi[0:n_rt, :] + p_hi[n_rt:, :] + p_lo + brt_ref[...]
        el_all = logits[0:N_EXPERTS, :]
        gl = logits[N_EXPERTS:N_EXPERTS + SUBLANES, :]
        grow = lax.broadcasted_iota(jnp.int32, gl.shape, 0).astype(jnp.float32)
        gl = jnp.where(grow < N_GROUPS, gl, NEG_BIG)
        gmax = jnp.max(gl, axis=0, keepdims=True)
        grp = jnp.min(jnp.where(gl == gmax, grow, float(N_GROUPS)), axis=0, keepdims=True)
        p_grp = 1.0 / jnp.sum(jnp.exp(gl - gmax), axis=0, keepdims=True)
        erow = lax.broadcasted_iota(jnp.int32, el_all.shape, 0).astype(jnp.float32)
        egrp = jnp.floor(erow * (1.0 / EXPERTS_PER_GROUP))
        el = jnp.where(egrp == grp, el_all, NEG_BIG)
        e1 = jnp.max(el, axis=0, keepdims=True)
        i1 = jnp.min(jnp.where(el == e1, erow, float(N_EXPERTS)), axis=0, keepdims=True)
        el2 = jnp.where(erow == i1, NEG_BIG, el)
        e2 = jnp.max(el2, axis=0, keepdims=True)
        i2 = jnp.min(jnp.where(el2 == e2, erow, float(N_EXPERTS)), axis=0, keepdims=True)
        z2 = jnp.exp(e2 - e1)
        w1 = p_grp / (1.0 + z2)
        w2 = p_grp * z2 / (1.0 + z2)
        sel1 = erow == i1
        sel2 = erow == i2
        onehot = jnp.logical_or(sel1, sel2)
        t_r = lax.broadcasted_iota(jnp.int32, (tr, tr), 0)
        t_c = lax.broadcasted_iota(jnp.int32, (tr, tr), 1)
        before = (t_r < t_c).astype(jnp.bfloat16)
        rank = jnp.dot(onehot.astype(jnp.bfloat16), before, preferred_element_type=jnp.float32)
        cnt = jnp.sum(onehot.astype(jnp.float32), axis=1, keepdims=True)
        cnt_al = jnp.floor((cnt + (SEG_ALIGN - 1)) * (1.0 / SEG_ALIGN)) * SEG_ALIGN
        e_r = lax.broadcasted_iota(jnp.int32, (N_EXPERTS, N_EXPERTS), 0)
        e_c = lax.broadcasted_iota(jnp.int32, (N_EXPERTS, N_EXPERTS), 1)
        lstart = jnp.dot((e_c < e_r).astype(jnp.float32), jnp.broadcast_to(cnt_al, (N_EXPERTS, LANES)),
                         precision=hi, preferred_element_type=jnp.float32)[:, 0:1]
        slot = lstart + rank
        pos1 = jnp.sum(jnp.where(sel1, slot, 0.0), axis=0, keepdims=True)
        pos2 = jnp.sum(jnp.where(sel2, slot, 0.0), axis=0, keepdims=True)
        r8 = lax.broadcasted_iota(jnp.int32, (SUBLANES, tr), 0)
        info = jnp.where(r8 == ROW_POS0, pos1, jnp.where(r8 == ROW_POS1, pos2, jnp.where(
            r8 == ROW_W0, w1, jnp.where(r8 == ROW_W1, w2, jnp.where(
                r8 == ROW_E0, i1, jnp.where(r8 == ROW_E1, i2, 0.0))))))
        row_ref[:, rs] = info

        def split3(w):
            h = _bf16(w).astype(jnp.float32)
            m = _bf16(w - h).astype(jnp.float32)
            return h, m, _bf16(w - h - m).astype(jnp.float32)

        w1h, w1m, w1l = split3(w1)
        w2h, w2m, w2l = split3(w2)
        parts = jnp.where(r8 == 0, w1h, jnp.where(r8 == 1, w1m, jnp.where(r8 == 2, w1l, jnp.where(
            r8 == 3, w2h, jnp.where(r8 == 4, w2m, jnp.where(r8 == 5, w2l, jnp.where(r8 == 6, i1, i2)))))))
        col_ref[rs, :] = jnp.concatenate(
            [info, parts, jnp.zeros((LANES - 2 * SUBLANES, tr), jnp.float32)], axis=0).T
        cnt_ref[sb] = jnp.broadcast_to(cnt, (N_EXPERTS, LANES)).astype(jnp.int32)


def _out_proj_router(x2, att, hm, wo, layer, g, wrt, brt):
    t = x2.shape[0]
    tm = TM_OUT
    row = lambda i: (i, 0)
    fix = lambda i: (0, 0)
    return pl.pallas_call(
        _out_proj_router_kernel,
        grid=(t // tm,),
        in_specs=[
            pl.BlockSpec((tm, D_MODEL), row),
            pl.BlockSpec((tm, ATT_WIDTH), row),
            pl.BlockSpec((tm, ML_WIDTH), row),
            pl.BlockSpec((None, D_MODEL, D_MODEL), lambda i: (layer, 0, 0), pipeline_mode=pl.Buffered(1)),
            pl.BlockSpec((1, D_MODEL), fix),
            pl.BlockSpec((8 * SUBLANES, D_MODEL), fix),
            pl.BlockSpec((4 * SUBLANES, 1), fix),
        ],
        out_specs=[
            pl.BlockSpec((tm, D_MODEL), row),
            pl.BlockSpec((tm, D_MODEL), row),
            pl.BlockSpec((SUBLANES, tm), lambda i: (0, i)),
            pl.BlockSpec((tm, LANES), row),
            pl.BlockSpec((tm // TM_PROJ, N_EXPERTS, LANES), lambda i: (i, 0, 0)),
        ],
        out_shape=(
            jax.ShapeDtypeStruct((t, D_MODEL), jnp.float32),
            jax.ShapeDtypeStruct((t, D_MODEL), jnp.bfloat16),
            jax.ShapeDtypeStruct((SUBLANES, t), jnp.float32),
            jax.ShapeDtypeStruct((t, LANES), jnp.float32),
            jax.ShapeDtypeStruct((t // TM_PROJ, N_EXPERTS, LANES), jnp.int32),
        ),
        scratch_shapes=[pltpu.VMEM((D_MODEL, D_MODEL), jnp.bfloat16)],
        compiler_params=pltpu.CompilerParams(dimension_semantics=("arbitrary",),
                                             vmem_limit_bytes=VMEM_LIMIT),
        name="out_proj_router",
    )(x2, att, hm, wo, g, wrt, brt)


def _dispatch_kernel(dst_ref, nz_ref, zdst_ref, t_ref, col_ref, row_ref, xs_hbm,
                     xbuf, zx, sem_x, sem_z):
    i = pl.program_id(0)
    n = pl.num_programs(0)
    p = i % 2
    tm = t_ref.shape[0]

    def copy(tile, par, j):
        d = pl.ds(pl.multiple_of(dst_ref[tile, j], SEG_ALIGN), SEG_ALIGN)
        return pltpu.make_async_copy(xbuf.at[par, j * SEG_ALIGN:(j + 1) * SEG_ALIGN], xs_hbm.at[d],
                                     sem_x.at[par])

    def zero_copy(e, j):
        d = pl.ds(pl.multiple_of(zdst_ref[e] + j * SEG_ALIGN, SEG_ALIGN), SEG_ALIGN)
        return pltpu.make_async_copy(zx, xs_hbm.at[d], sem_z.at[0])

    def for_zero_chunks(fn):
        for e in range(N_EXPERTS):
            def body(j, c, e=e):
                fn(zero_copy(e, j))
                return c
            lax.fori_loop(0, nz_ref[e], body, 0)

    def wait_tile(par):
        pltpu.make_async_copy(xbuf.at[par], xs_hbm.at[pl.ds(0, L_CAP)], sem_x.at[par]).wait()

    @pl.when(i == 0)
    def _():
        zx[...] = jnp.zeros(zx.shape, zx.dtype)
        for_zero_chunks(lambda cp: cp.start())

    @pl.when(i >= 2)
    def _():
        wait_tile(p)

    pos0 = row_ref[ROW_POS0:ROW_POS0 + 1, :].astype(jnp.int32)
    pos1 = row_ref[ROW_POS1:ROW_POS1 + 1, :].astype(jnp.int32)
    t = t_ref[...]
    side = _bf16(col_ref[...])
    rows = DISPATCH_ROWS
    for c in range(L_CAP // rows):
        r = lax.broadcasted_iota(jnp.int32, (rows, tm), 0) + c * rows
        perm = jnp.logical_or(r == pos0, r == pos1).astype(jnp.bfloat16)
        xbuf[p, c * rows:(c + 1) * rows, 0:D_MODEL] = _bf16(
            jnp.dot(perm, t, preferred_element_type=jnp.float32))
        xbuf[p, c * rows:(c + 1) * rows, D_MODEL:D_XS] = _bf16(
            jnp.dot(perm, side, preferred_element_type=jnp.float32))
        for j in range(c * rows // SEG_ALIGN, (c + 1) * rows // SEG_ALIGN):
            copy(i, p, j).start()

    @pl.when(i == n - 1)
    def _():
        @pl.when(n >= 2)
        def _():
            wait_tile(1 - p)
        wait_tile(p)
        for_zero_chunks(lambda cp: cp.wait())


def _dispatch(t, col, row, dst, nz, zdst):
    n_tok = t.shape[0]
    r_cap, _ = _moe_capacity(n_tok)
    tm = TM_PROJ
    grid_spec = pltpu.PrefetchScalarGridSpec(
        num_scalar_prefetch=3,
        grid=(n_tok // tm,),
        in_specs=[
            pl.BlockSpec((tm, D_MODEL), lambda i, *_: (i, 0)),
            pl.BlockSpec((tm, LANES), lambda i, *_: (i, 0)),
            pl.BlockSpec((SUBLANES, tm), lambda i, *_: (0, i)),
        ],
        out_specs=pl.BlockSpec(memory_space=pl.ANY),
        scratch_shapes=[
            pltpu.VMEM((2, L_CAP, D_XS), jnp.bfloat16),
            pltpu.VMEM((SEG_ALIGN, D_XS), jnp.bfloat16),
            pltpu.SemaphoreType.DMA((2,)),
            pltpu.SemaphoreType.DMA((1,)),
        ],
    )
    return pl.pallas_call(
        _dispatch_kernel,
        grid_spec=grid_spec,
        out_shape=jax.ShapeDtypeStruct((r_cap + DUMP_ROWS, D_XS), jnp.bfloat16),
        compiler_params=pltpu.CompilerParams(dimension_semantics=("arbitrary",),
                                             vmem_limit_bytes=VMEM_LIMIT),
        name="moe_dispatch",
    )(dst, nz, zdst, t, col, row)


def _moe_kernel(layer, blk_e_ref, nused_ref, nvalid_ref, next_e_ref, xs_ref, wg_hbm, wu_hbm, wd_hbm,
                ys_ref, wg_st, wu_st, wd_st, wg_bf, wu_bf, wd_bf, sem):
    b = pl.program_id(0)
    nv = nvalid_ref[b]
    e = blk_e_ref[b]
    new_expert = jnp.logical_or(b == 0, e != blk_e_ref[jnp.maximum(b - 1, 0)])

    def fetch(expert):
        return (pltpu.make_async_copy(wg_hbm.at[layer, expert], wg_st, sem.at[0]),
                pltpu.make_async_copy(wu_hbm.at[layer, expert], wu_st, sem.at[1]),
                pltpu.make_async_copy(wd_hbm.at[layer, expert], wd_st, sem.at[2]))

    @pl.when(b == 0)
    def _():
        for cp in fetch(e):
            cp.start()

    @pl.when(new_expert)
    def _():
        for cp in fetch(e):
            cp.wait()
        wg_bf[...] = _bf16(wg_st[...])
        wu_bf[...] = _bf16(wu_st[...])
        wd_bf[...] = _bf16(wd_st[...])

        @pl.when(next_e_ref[b] != e)
        def _():
            for cp in fetch(next_e_ref[b]):
                cp.start()

    def compute(rows):
        x = xs_ref[0:rows, 0:D_MODEL]
        y = None
        fw = D_FF_EXPERT // MOE_FF_SPLIT
        for c in range(MOE_FF_SPLIT):
            cols = slice(c * fw, (c + 1) * fw)
            a = jnp.dot(x, wg_bf[:, cols], preferred_element_type=jnp.float32)
            u = jnp.dot(x, wu_bf[:, cols], preferred_element_type=jnp.float32)
            h = a * jax.nn.sigmoid(a) * u
            yc = jnp.dot(_bf16(h), wd_bf[cols, :], preferred_element_type=jnp.float32)
            y = yc if y is None else y + yc
        sd = xs_ref[0:rows, D_MODEL:D_XS].astype(jnp.float32)
        e_blk = blk_e_ref[b].astype(jnp.float32)
        w0 = sd[:, COL_W0H:COL_W0H + 1] + sd[:, COL_W0H + 1:COL_W0H + 2] + sd[:, COL_W0H + 2:COL_W0H + 3]
        w1 = sd[:, COL_W1H:COL_W1H + 1] + sd[:, COL_W1H + 1:COL_W1H + 2] + sd[:, COL_W1H + 2:COL_W1H + 3]
        w = jnp.where(sd[:, COL_E0:COL_E0 + 1] == e_blk, w0, w1)
        ys_ref[0:rows, :] = _bf16(y * w)

    half = MOE_BM // 2

    @pl.when(nv > half)
    def _():
        compute(MOE_BM)

    @pl.when(jnp.logical_and(nv > 0, nv <= half))
    def _():
        compute(half)
        ys_ref[half:, :] = jnp.zeros((MOE_BM - half, D_MODEL), ys_ref.dtype)


def _moe(xs, blk_e, nused, nvalid, next_e, wg, wu, wd, layer):
    blk = lambda b, be, nu, nv, ne: (jnp.maximum(jnp.minimum(b, nu[0] - 1), 0), 0)
    r_cap = xs.shape[0] - DUMP_ROWS
    grid_spec = pltpu.PrefetchScalarGridSpec(
        num_scalar_prefetch=4,
        grid=(r_cap // MOE_BM,),
        in_specs=[
            pl.BlockSpec((MOE_BM, D_XS), blk),
            pl.BlockSpec(memory_space=pl.ANY),
            pl.BlockSpec(memory_space=pl.ANY),
            pl.BlockSpec(memory_space=pl.ANY),
        ],
        out_specs=pl.BlockSpec((MOE_BM, D_MODEL), blk),
        scratch_shapes=[
            pltpu.VMEM((D_MODEL, D_FF_EXPERT), jnp.float32),
            pltpu.VMEM((D_MODEL, D_FF_EXPERT), jnp.float32),
            pltpu.VMEM((D_FF_EXPERT, D_MODEL), jnp.float32),
            pltpu.VMEM((D_MODEL, D_FF_EXPERT), jnp.bfloat16),
            pltpu.VMEM((D_MODEL, D_FF_EXPERT), jnp.bfloat16),
            pltpu.VMEM((D_FF_EXPERT, D_MODEL), jnp.bfloat16),
            pltpu.SemaphoreType.DMA((3,)),
        ],
    )
    return pl.pallas_call(
        functools.partial(_moe_kernel, layer),
        grid_spec=grid_spec,
        out_shape=jax.ShapeDtypeStruct((r_cap, D_MODEL), jnp.bfloat16),
        compiler_params=pltpu.CompilerParams(dimension_semantics=("arbitrary",),
                                             vmem_limit_bytes=VMEM_LIMIT),
        name="moe_experts",
    )(blk_e, nused, nvalid, next_e, xs, wg, wu, wd)


def _combine_kernel(src_ref, x1_ref, col_ref, ys_hbm, out_ref, ybuf, sem):
    i = pl.program_id(0)
    n = pl.num_programs(0)
    p = i % COMBINE_BUFS
    tm = x1_ref.shape[0]

    def start_tile(tile, par):
        for j in range(N_CHUNK):
            s = pl.ds(pl.multiple_of(src_ref[tile, j], SEG_ALIGN), SEG_ALIGN)
            pltpu.make_async_copy(ys_hbm.at[s], ybuf.at[par, j * SEG_ALIGN:(j + 1) * SEG_ALIGN],
                                  sem.at[par]).start()

    def wait_buf(par):
        pltpu.make_async_copy(ys_hbm.at[pl.ds(0, L_CAP)], ybuf.at[par], sem.at[par]).wait()

    @pl.when(i == 0)
    def _():
        start_tile(0, 0)
        start_tile(jnp.minimum(1, n - 1), 1)

    wait_buf(p)
    start_tile(jnp.minimum(i + 2, n - 1), (i + 2) % COMBINE_BUFS)

    rows = COMBINE_ROWS
    for c in range(tm // rows):
        col = col_ref[c * rows:(c + 1) * rows, :]
        pos0 = col[:, ROW_POS0:ROW_POS0 + 1].astype(jnp.int32)
        pos1 = col[:, ROW_POS1:ROW_POS1 + 1].astype(jnp.int32)
        l = lax.broadcasted_iota(jnp.int32, (rows, L_CAP), 1)
        perm = jnp.logical_or(l == pos0, l == pos1).astype(jnp.bfloat16)
        y = jnp.dot(perm, ybuf[p], preferred_element_type=jnp.float32)
        out_ref[c * rows:(c + 1) * rows, :] = x1_ref[c * rows:(c + 1) * rows, :] + y

    @pl.when(i == n - 1)
    def _():
        wait_buf((i + 1) % COMBINE_BUFS)
        wait_buf((i + 2) % COMBINE_BUFS)


def _combine(x1, col, ys, src):
    n_tok = x1.shape[0]
    tm = TM_PROJ
    grid_spec = pltpu.PrefetchScalarGridSpec(
        num_scalar_prefetch=1,
        grid=(n_tok // tm,),
        in_specs=[
            pl.BlockSpec((tm, D_MODEL), lambda i, *_: (i, 0)),
            pl.BlockSpec((tm, LANES), lambda i, *_: (i, 0)),
            pl.BlockSpec(memory_space=pl.ANY),
        ],
        out_specs=pl.BlockSpec((tm, D_MODEL), lambda i, *_: (i, 0)),
        scratch_shapes=[
            pltpu.VMEM((COMBINE_BUFS, L_CAP, D_MODEL), jnp.bfloat16),
            pltpu.SemaphoreType.DMA((COMBINE_BUFS,)),
        ],
    )
    return pl.pallas_call(
        _combine_kernel,
        grid_spec=grid_spec,
        out_shape=jax.ShapeDtypeStruct((n_tok, D_MODEL), jnp.float32),
        compiler_params=pltpu.CompilerParams(dimension_semantics=("arbitrary",),
                                             vmem_limit_bytes=VMEM_LIMIT),
        name="moe_combine",
    )(src, x1, col, ys)


def _routing_tables(counts, n_blk_cap):
    r_cap = n_blk_cap * MOE_BM
    i32 = jnp.int32
    ca = ((counts + (SEG_ALIGN - 1)) // SEG_ALIGN) * SEG_ALIGN
    lend = jnp.cumsum(ca, axis=1)
    lstart = lend - ca
    tot = jnp.sum(ca, axis=0)
    region = ((tot + (MOE_BM - 1)) // MOE_BM) * MOE_BM
    rend = jnp.cumsum(region)
    base = rend - region
    gstart = base[None, :] + jnp.cumsum(ca, axis=0) - ca
    j16 = jnp.arange(N_CHUNK, dtype=i32) * SEG_ALIGN
    in_seg = jnp.logical_and(lstart[:, None, :] <= j16[None, :, None], j16[None, :, None] < lend[:, None, :])
    seg_row = (jnp.sum(jnp.where(in_seg, (gstart - lstart)[:, None, :], 0), axis=2) + j16[None, :]).astype(i32)
    used = j16[None, :] < lend[:, -1:]
    tile_par = (jnp.arange(counts.shape[0], dtype=i32) % 2)[:, None]
    park = r_cap + (tile_par * N_CHUNK + jnp.arange(N_CHUNK, dtype=i32)[None, :]) * SEG_ALIGN
    dst = jnp.where(used, seg_row, park).astype(i32)
    src = jnp.where(used, seg_row, 0).astype(i32)
    nused = (rend[-1] // MOE_BM).astype(i32)
    brow = jnp.minimum(jnp.arange(n_blk_cap, dtype=i32), nused - 1) * MOE_BM
    blk_e = jnp.minimum(jnp.sum(rend[None, :] <= brow[:, None], axis=1), N_EXPERTS - 1).astype(i32)
    zstart = (base + tot).astype(i32)
    nzc = ((region - tot) // SEG_ALIGN).astype(i32)
    blk_i = jnp.arange(n_blk_cap, dtype=i32)
    seg_end = jnp.sum(jnp.where(blk_e[:, None] == jnp.arange(N_EXPERTS)[None, :], (base + tot)[None, :], 0), axis=1)
    nvalid = jnp.where(blk_i < nused, jnp.clip(seg_end - blk_i * MOE_BM, 0, MOE_BM), 0).astype(i32)
    eid = jnp.arange(N_EXPERTS, dtype=i32)
    later = jnp.logical_and(eid[None, :] > eid[:, None], (region > 0)[None, :])
    nxt = jnp.min(jnp.where(later, eid[None, :], N_EXPERTS), axis=1)
    nxt = jnp.where(nxt < N_EXPERTS, nxt, eid)
    next_e = jnp.sum(jnp.where(blk_e[:, None] == eid[None, :], nxt[None, :], 0), axis=1).astype(i32)
    return dst, src, nzc, zstart, blk_e, nused[None], nvalid, next_e


def _t5_bucket_np(dist):
    max_exact = N_BUCKETS // 2
    d = np.maximum(dist, 1).astype(np.float32)
    large = max_exact + (np.log(d / max_exact) / math.log(MAX_DISTANCE / max_exact)
                         * (N_BUCKETS - max_exact)).astype(np.int32)
    large = np.minimum(large, N_BUCKETS - 1)
    return np.where(dist < max_exact, dist, large)


def _attention_bias(rel_bias):
    qi = np.arange(ATT_BLOCK)[:, None]
    kj = np.arange(2 * ATT_BLOCK)[None, :]
    dist = qi + ATT_BLOCK - kj
    in_window = (dist >= 0) & (dist < WINDOW)
    bucket = _t5_bucket_np(np.clip(dist, 0, WINDOW - 1))
    onehot = (bucket[None] == np.arange(N_BUCKETS)[:, None, None]).astype(np.float32)
    bias = jnp.einsum('nh,nqk->hqk', rel_bias.astype(jnp.float32), jnp.asarray(onehot),
                      precision=lax.Precision.HIGHEST)
    bias = jnp.where(jnp.asarray(in_window)[None], bias * LOG2E, NEG_BIG)
    no_prev = jnp.asarray(kj < ATT_BLOCK)[None]
    return jnp.stack([bias, jnp.where(no_prev, NEG_BIG, bias)])


def _block_diag_mean(width, block):
    idx = np.arange(width) // block
    return jnp.asarray((idx[:, None] == idx[None, :]).astype(np.float32) / block, dtype=jnp.bfloat16)


def kernel(x, rel_bias, norm_mix_g, w_in, q_norm_g, k_norm_g, attn_sink, conv_w, conv_b, gate_b,
           mlstm_norm_g, w_out, norm_ffn_g, w_router_group, b_router_group, w_router_expert,
           b_router_expert, w_gate, w_up, w_down):
    batch, seq_len, _ = x.shape
    n_tok = batch * seq_len
    assert seq_len % TM_OUT == 0 and seq_len % ML_KCHUNK == 0
    f32 = jnp.float32
    bias = _attention_bias(rel_bias)
    bdq = _block_diag_mean(ATT_WIDTH, ATT_HEAD_DIM)
    bdk = _block_diag_mean(LANES, ATT_HEAD_DIM)
    x2 = x.reshape(n_tok, D_MODEL)
    for l in range(DEPTH):
        wgt = _bf16(w_in[l, :, O_G:N_IN].T)
        qg = (jnp.tile(q_norm_g[l].astype(f32), ATT_HEADS) * (ATT_HEAD_DIM ** -0.5 * LOG2E))[None, :]
        kg = jnp.tile(k_norm_g[l].astype(f32), ATT_KV_HEADS)[None, :]
        gbt = gate_b[l].astype(f32)[:, None]
        qn, kn, vd, hm = _front(
            x2.reshape(batch, seq_len, D_MODEL), norm_mix_g[l][None, :], w_in, l, wgt, qg, kg, bdq, bdk,
            conv_w[l], conv_b[l][None, :], gbt, mlstm_norm_g[l][None, :])
        flat = lambda a: a.reshape(n_tok, a.shape[-1])
        att = _attention(flat(qn), flat(kn), flat(vd), bias, attn_sink[l].astype(f32) * LOG2E,
                         batch, seq_len)
        n_rt = 4 * SUBLANES
        wrt = jnp.pad(jnp.concatenate([w_router_expert[l], w_router_group[l]], axis=1).astype(f32).T,
                      ((0, n_rt - N_EXPERTS - N_GROUPS), (0, 0)))
        brt = jnp.pad(jnp.concatenate([b_router_expert[l], b_router_group[l]]).astype(f32),
                      (0, n_rt - N_EXPERTS - N_GROUPS))[:, None]
        wrt_hi = _bf16(wrt)
        wrt = jnp.concatenate([wrt_hi, _bf16(wrt - wrt_hi.astype(f32))], axis=0)
        x1, t, row, col, cnt = _out_proj_router(x2, att, hm.reshape(n_tok, ML_WIDTH), w_out, l,
                                                norm_ffn_g[l][None, :], wrt, brt)
        _, n_blk_cap = _moe_capacity(n_tok)
        dst, src, nz, zdst, blk_e, nused, nvalid, next_e = _routing_tables(cnt[:, :, 0], n_blk_cap)
        xs = _dispatch(t, col, row, dst, nz, zdst)
        ys = _moe(xs, blk_e, nused, nvalid, next_e, w_gate, w_up, w_down, l)
        x2 = _combine(x1, col, ys, src)
    return x2.reshape(batch, seq_len, D_MODEL)
```

```python
import functools
import math

import jax
import jax.numpy as jnp
import numpy as np
from jax import lax
from jax.experimental import pallas as pl
from jax.experimental.pallas import tpu as pltpu

D_MODEL = 1024
DEPTH = 2
ATT_HEADS = 8
ATT_KV_HEADS = 2
ATT_HEAD_DIM = 64
ATT_WIDTH = ATT_HEADS * ATT_HEAD_DIM
WINDOW = 128
ATT_BLOCK = 128
N_BUCKETS = 32
MAX_DISTANCE = 128
ML_HEADS = 4
ML_DQK = 64
ML_DV = 128
ML_WIDTH = ML_HEADS * ML_DV
ML_CHUNK = 64
CONV_K = 4
N_GROUPS = 4
EXPERTS_PER_GROUP = 4
N_EXPERTS = N_GROUPS * EXPERTS_PER_GROUP
D_FF_EXPERT = 512
EPS = 1e-6

LANES = 128
SUBLANES = 8
NEG_BIG = -1e30
LOG2E = math.log2(math.e)
VMEM_LIMIT = 48 * 1024 * 1024
FRONT_VMEM_LIMIT = 56 * 1024 * 1024

O_Q = 0
O_K = O_Q + ATT_WIDTH
O_V = O_K + ATT_KV_HEADS * ATT_HEAD_DIM
O_QM = O_V + ATT_KV_HEADS * ATT_HEAD_DIM
O_KM = O_QM + ML_HEADS * ML_DQK
O_VM = O_KM + ML_HEADS * ML_DQK
O_OM = O_VM + ML_WIDTH
O_G = O_OM + ML_WIDTH
N_IN = O_G + 2 * ML_HEADS

ATT_STEP_BLOCKS = 4
TM_OUT = 1024
TM_PROJ = 512
ML_KCHUNK = 256

SEG_ALIGN = 16
MOE_BM = 512
MOE_FF_SPLIT = 2
L_CAP = 2 * TM_PROJ + N_EXPERTS * SEG_ALIGN
N_CHUNK = L_CAP // SEG_ALIGN
D_XS = D_MODEL + LANES
DUMP_ROWS = 2 * N_CHUNK * SEG_ALIGN
DISPATCH_ROWS = 256
COMBINE_ROWS = 256
COMBINE_BUFS = 3
ROW_POS0, ROW_POS1, ROW_W0, ROW_W1, ROW_E0, ROW_E1 = 0, 1, 2, 3, 4, 5
COL_W0H, COL_W1H, COL_E0, COL_E1 = 8, 11, 14, 15


def _moe_capacity(n_tok):
    n_tiles = n_tok // TM_PROJ
    rows = 2 * n_tok + n_tiles * N_EXPERTS * (SEG_ALIGN - 1) + N_EXPERTS * (MOE_BM - SEG_ALIGN)
    n_blk = -(-rows // MOE_BM)
    return n_blk * MOE_BM, n_blk


def _bf16(a):
    return a.astype(jnp.bfloat16)


def _split3(a):
    hi = _bf16(a)
    r1 = a - hi.astype(jnp.float32)
    mid = _bf16(r1)
    return hi, mid, _bf16(r1 - mid.astype(jnp.float32))


def _log_sigmoid(z):
    return jnp.minimum(z, 0.0) - jnp.log(1.0 + jnp.exp(-jnp.abs(z)))


def _front_kernel(x_ref, g_ref, w_ref, wgt_ref, qg_ref, kg_ref, bdq_ref, bdk_ref, cw_ref, cb_ref,
                  gbt_ref, ng_ref,
                  qn_ref, kn_ref, vd_ref, hm_ref,
                  w_bf, conv_scr, qkm_s, vm_s, om_s, gt_s, c_scr, m_scr):
    s = pl.program_id(0)
    batch = x_ref.shape[0]
    L = ML_KCHUNK
    wr = s % 2
    rd = 1 - wr

    n_chunks = pl.num_programs(0) - 1

    @pl.when(s == 0)
    def _():
        w_bf[...] = _bf16(w_ref[...])
        conv_scr[:, 0:SUBLANES, :] = jnp.zeros((batch, SUBLANES, conv_scr.shape[2]), jnp.float32)
        c_scr[...] = jnp.zeros(c_scr.shape, jnp.float32)
        m_scr[...] = jnp.zeros(m_scr.shape, jnp.float32)

    @pl.when(s > 0)
    def _():
        conv_scr[:, 0:SUBLANES, :] = conv_scr[:, L:L + SUBLANES, :]

    def scan_pieces():
        return _mlstm_pieces(qkm_s.at[rd], vm_s.at[rd], om_s.at[rd], gt_s.at[rd], ng_ref, hm_ref,
                             c_scr, m_scr, batch)

    def proj_pieces():
        return _project_pieces(x_ref, g_ref, w_bf, wgt_ref, qg_ref, kg_ref, bdq_ref, bdk_ref, cw_ref,
                               cb_ref, gbt_ref, qn_ref, kn_ref, vd_ref, qkm_s.at[wr], vm_s.at[wr],
                               om_s.at[wr], gt_s.at[wr], conv_scr)

    @pl.when(s == 0)
    def _():
        norm_part, proj_parts = proj_pieces()
        hbs = [norm_part(b) for b in range(batch)]
        for part in proj_parts:
            for b in range(batch):
                part(b, hbs[b])

    @pl.when(jnp.logical_and(s > 0, s < n_chunks))
    def _():
        scan_prep, scan_head, scan_finish = scan_pieces()
        norm_part, proj_parts = proj_pieces()
        preps = [scan_prep(b) for b in range(batch)]
        hbs = [norm_part(b) for b in range(batch)]
        for stage in range(len(proj_parts)):
            if stage < ML_HEADS:
                for b in range(batch):
                    scan_head(b, stage, preps[b])
            for b in range(batch):
                proj_parts[stage](b, hbs[b])
        scan_finish()

    @pl.when(s == n_chunks)
    def _():
        scan_prep, scan_head, scan_finish = scan_pieces()
        for b in range(batch):
            prep = scan_prep(b)
            for h in range(ML_HEADS):
                scan_head(b, h, prep)
        scan_finish()


def _project_pieces(x_ref, g_ref, w_bf, wgt_ref, qg_ref, kg_ref, bdq_ref, bdk_ref, cw_ref, cb_ref,
                    gbt_ref, qn_ref, kn_ref, vd_ref, qkm_ref, vm_ref, om_ref, gatet_ref, conv_scr):
    L = ML_KCHUNK
    conv_w = 2 * ML_HEADS * ML_DQK

    def proj(hb, c0, width):
        return jnp.dot(hb, w_bf[:, c0:c0 + width], preferred_element_type=jnp.float32)

    def norm_part(b):
        x = x_ref[b]
        hn = x * lax.rsqrt(jnp.mean(x * x, axis=-1, keepdims=True) + EPS) * g_ref[...]
        return _bf16(hn)

    def q_part(b, hb):
        q = proj(hb, O_Q, ATT_WIDTH)
        q_ms = jnp.dot(_bf16(q * q), bdq_ref[...], preferred_element_type=jnp.float32)
        qn_ref[b] = _bf16(q * lax.rsqrt(q_ms + EPS) * qg_ref[...])

    def kv_part(b, hb):
        kv = proj(hb, O_K, 2 * LANES)
        k = kv[:, 0:LANES]
        v = kv[:, LANES:2 * LANES]
        k_ms = jnp.dot(_bf16(k * k), bdk_ref[...], preferred_element_type=jnp.float32)
        kn = k * lax.rsqrt(k_ms + EPS) * kg_ref[...]
        low = lax.broadcasted_iota(jnp.int32, kn.shape, 1) < ATT_HEAD_DIM

        def dup_heads(a):
            swapped = pltpu.roll(a, ATT_HEAD_DIM, axis=1)
            return jnp.concatenate([jnp.where(low, a, swapped), jnp.where(low, swapped, a)], axis=1)

        kn_ref[b] = _bf16(dup_heads(kn))
        vd_ref[b] = _bf16(dup_heads(v))

    def conv_part(b, hb):
        qk = proj(hb, O_QM, conv_w)
        conv_scr[b, SUBLANES:SUBLANES + L, :] = qk
        y = qk * cw_ref[CONV_K - 1:CONV_K, :] + cb_ref[...]
        for j in range(CONV_K - 1):
            off = SUBLANES - (CONV_K - 1) + j
            y = y + conv_scr[b, off:off + L, :] * cw_ref[j:j + 1, :]
        y = y * jax.nn.sigmoid(y)
        lane = lax.broadcasted_iota(jnp.int32, y.shape, 1)
        y = jnp.where(lane >= ML_HEADS * ML_DQK, y * (ML_DQK ** -0.5), y)
        qkm_ref[b] = _bf16(y)

    def vm_part(b, hb):
        vm_ref[b] = _bf16(proj(hb, O_VM, ML_WIDTH))

    def om_part(b, hb):
        om_ref[b] = _bf16(jax.nn.sigmoid(proj(hb, O_OM, ML_WIDTH)))

        gt = lax.dot_general(wgt_ref[...], hb, (((1,), (1,)), ((), ())),
                             preferred_element_type=jnp.float32) + gbt_ref[...]
        grow = lax.broadcasted_iota(jnp.int32, gt.shape, 0)
        gatet_ref[b] = jnp.where(grow >= ML_HEADS, _log_sigmoid(gt), gt) * LOG2E

    return norm_part, (q_part, kv_part, conv_part, vm_part, om_part)


def _front(x3, g, w_in, layer, wgt, qg, kg, bdq, bdk, cw, cb, gbt, ng):
    batch, seq_len, _ = x3.shape
    L = ML_KCHUNK
    nc = seq_len // L
    fix = lambda s: (0, 0)
    cur = lambda s: (0, jnp.minimum(s, nc - 1), 0)
    prev = lambda s: (0, jnp.maximum(s - 1, 0), 0)
    kv_w = 2 * ATT_KV_HEADS * ATT_HEAD_DIM
    conv_w = 2 * ML_HEADS * ML_DQK
    return pl.pallas_call(
        _front_kernel,
        grid=(nc + 1,),
        in_specs=[
            pl.BlockSpec((batch, L, D_MODEL), cur),
            pl.BlockSpec((1, D_MODEL), fix),
            pl.BlockSpec((None, D_MODEL, N_IN), lambda s: (layer, 0, 0), pipeline_mode=pl.Buffered(1)),
            pl.BlockSpec((SUBLANES, D_MODEL), fix),
            pl.BlockSpec((1, ATT_WIDTH), fix),
            pl.BlockSpec((1, LANES), fix),
            pl.BlockSpec((ATT_WIDTH, ATT_WIDTH), fix),
            pl.BlockSpec((LANES, LANES), fix),
            pl.BlockSpec((CONV_K, conv_w), fix),
            pl.BlockSpec((1, conv_w), fix),
            pl.BlockSpec((SUBLANES, 1), fix),
            pl.BlockSpec((1, ML_WIDTH), fix),
        ],
        out_specs=[
            pl.BlockSpec((batch, L, ATT_WIDTH), cur),
            pl.BlockSpec((batch, L, kv_w), cur),
            pl.BlockSpec((batch, L, kv_w), cur),
            pl.BlockSpec((batch, L, ML_WIDTH), prev),
        ],
        out_shape=(
            jax.ShapeDtypeStruct((batch, seq_len, ATT_WIDTH), jnp.bfloat16),
            jax.ShapeDtypeStruct((batch, seq_len, kv_w), jnp.bfloat16),
            jax.ShapeDtypeStruct((batch, seq_len, kv_w), jnp.bfloat16),
            jax.ShapeDtypeStruct((batch, seq_len, ML_WIDTH), jnp.bfloat16),
        ),
        scratch_shapes=[
            pltpu.VMEM((D_MODEL, N_IN), jnp.bfloat16),
            pltpu.VMEM((batch, L + 2 * SUBLANES, conv_w), jnp.float32),
            pltpu.VMEM((2, batch, L, conv_w), jnp.bfloat16),
            pltpu.VMEM((2, batch, L, ML_WIDTH), jnp.bfloat16),
            pltpu.VMEM((2, batch, L, ML_WIDTH), jnp.bfloat16),
            pltpu.VMEM((2, batch, SUBLANES, L), jnp.float32),
            pltpu.VMEM((batch * ML_HEADS // 2, 2 * ML_DQK, 2 * ML_DV), jnp.float32),
            pltpu.VMEM((batch * ML_HEADS, 1, LANES), jnp.float32),
        ],
        compiler_params=pltpu.CompilerParams(dimension_semantics=("arbitrary",),
                                             vmem_limit_bytes=FRONT_VMEM_LIMIT),
        name="proj_mlstm",
    )(x3, g, w_in, wgt, qg, kg, bdq, bdk, cw, cb, gbt, ng)


def _attn_kernel(sink_ref, q_ref, kp_ref, kc_ref, vp_ref, vc_ref, bias_ref, o_ref):
    i = pl.program_id(1)
    blk = ATT_BLOCK
    lane = lax.broadcasted_iota(jnp.int32, (blk, LANES), 1)
    low = lane < ATT_HEAD_DIM
    group = ATT_HEADS // ATT_KV_HEADS
    ones = jnp.ones((2 * blk, LANES), jnp.bfloat16)
    for j in range(ATT_STEP_BLOCKS):
        rows = slice(j * blk, (j + 1) * blk)
        first = (i == 0).astype(jnp.int32) if j == 0 else 0
        for kv in range(ATT_KV_HEADS):
            lanes = slice(kv * LANES, (kv + 1) * LANES)
            if j == 0:
                kcat = jnp.concatenate([kp_ref[:, lanes], kc_ref[0:blk, lanes]], axis=0)
                vrows = jnp.concatenate([vp_ref[:, lanes], vc_ref[0:blk, lanes]], axis=0)
            else:
                kcat = kc_ref[(j - 1) * blk:(j + 1) * blk, lanes]
                vrows = vc_ref[(j - 1) * blk:(j + 1) * blk, lanes]
            vcat = jnp.concatenate([vrows, ones], axis=1)
            heads = range(kv * group, (kv + 1) * group)
            qs = []
            for h in heads:
                qp = q_ref[rows, (h // 2) * LANES:(h // 2 + 1) * LANES]
                qs.append(jnp.where(low if h % 2 == 0 else jnp.logical_not(low), qp, jnp.zeros_like(qp)))
            s_all = lax.dot_general(jnp.concatenate(qs, axis=0), kcat, (((1,), (1,)), ((), ())),
                                    preferred_element_type=jnp.float32)
            ps, ms = [], []
            for n, h in enumerate(heads):
                logits = s_all[n * blk:(n + 1) * blk, :] + bias_ref[first, h]
                row_max = jnp.broadcast_to(jnp.max(logits, axis=-1, keepdims=True), (blk, LANES))
                m = jnp.maximum(row_max, sink_ref[h])
                ps.append(_bf16(jnp.exp2(logits - jnp.concatenate([m, m], axis=1))))
                ms.append(m)
            o_all = jnp.dot(jnp.concatenate(ps, axis=0), vcat, preferred_element_type=jnp.float32)
            outs = []
            for n, h in enumerate(heads):
                o = o_all[n * blk:(n + 1) * blk, :]
                den = o[:, LANES:2 * LANES] + jnp.exp2(sink_ref[h] - ms[n])
                outs.append(o[:, 0:LANES] / den)
            for n in range(0, group, 2):
                pair = (kv * group + n) // 2
                o_ref[rows, pair * LANES:(pair + 1) * LANES] = _bf16(
                    jnp.where(low, outs[n], outs[n + 1]))


def _attention(qn, kn, vd, bias, sink, batch, seq_len):
    step = ATT_STEP_BLOCKS * ATT_BLOCK
    ns = seq_len // step
    cur = lambda b, i, s: (b * ns + i, 0)
    prev = lambda b, i, s: ((b * ns + i) * ATT_STEP_BLOCKS - jnp.minimum(i, 1), 0)
    grid_spec = pltpu.PrefetchScalarGridSpec(
        num_scalar_prefetch=1,
        grid=(batch, ns),
        in_specs=[
            pl.BlockSpec((step, ATT_WIDTH), cur),
            pl.BlockSpec((ATT_BLOCK, 2 * LANES), prev),
            pl.BlockSpec((step, 2 * LANES), cur),
            pl.BlockSpec((ATT_BLOCK, 2 * LANES), prev),
            pl.BlockSpec((step, 2 * LANES), cur),
            pl.BlockSpec((2, ATT_HEADS, ATT_BLOCK, 2 * ATT_BLOCK), lambda b, i, s: (0, 0, 0, 0)),
        ],
        out_specs=pl.BlockSpec((step, ATT_WIDTH), cur),
    )
    return pl.pallas_call(
        _attn_kernel,
        grid_spec=grid_spec,
        out_shape=jax.ShapeDtypeStruct((batch * seq_len, ATT_WIDTH), jnp.bfloat16),
        compiler_params=pltpu.CompilerParams(dimension_semantics=("arbitrary", "arbitrary"),
                                             vmem_limit_bytes=VMEM_LIMIT),
        name="swa_attention",
    )(sink, qn, kn, kn, vd, vd, bias)


def _mlstm_pieces(qk_ref, v_ref, o_ref, gt_ref, ng_ref, out_ref, c_scr, m_scr, batch):
    L = ML_KCHUNK
    pairs = ML_HEADS // 2
    r_i = lax.broadcasted_iota(jnp.int32, (L, L), 0)
    c_i = lax.broadcasted_iota(jnp.int32, (L, L), 1)
    causal = c_i <= r_i
    tril_bf = causal.astype(jnp.bfloat16)
    triu_bf = (r_i <= c_i).astype(jnp.bfloat16)
    mean_dv = jnp.full((ML_DV, ML_DV), 1.0 / ML_DV, jnp.bfloat16)
    lane = lax.broadcasted_iota(jnp.int32, (L, LANES), 1)
    low = lane < ML_DQK
    ones_dv = jnp.ones((L, ML_DV), jnp.bfloat16)
    row2 = lax.broadcasted_iota(jnp.int32, (2 * ML_DQK, 2 * ML_DV), 0)

    def twice(a):
        return jnp.concatenate([a, a], axis=1)

    m_state = [m_scr[k] for k in range(batch * ML_HEADS)]
    c_state = [c_scr[k] for k in range(batch * pairs)]

    def batch_prep(b):
        gt = gt_ref[b]
        g = jnp.concatenate([gt, jnp.zeros((LANES - SUBLANES, L), jnp.float32)], axis=0).T
        bcols = sum(jnp.dot(tril_bf, part, preferred_element_type=jnp.float32) for part in _split3(g))
        brows = sum(jnp.dot(part, triu_bf, preferred_element_type=jnp.float32) for part in _split3(gt))
        return gt, g, bcols, brows

    carry = {}

    def head_piece(b, h, prep):
        gt, g, bcols, brows = prep
        pair, sub = divmod(h, 2)
        sidx = b * pairs + pair
        qp = qk_ref[b, :, pair * LANES:(pair + 1) * LANES]
        kp = qk_ref[b, :, (pairs + pair) * LANES:(pairs + pair + 1) * LANES]
        c_pair = c_state[sidx]
        c_bf = _bf16(c_pair)
        sel = low if sub == 0 else jnp.logical_not(low)
        m_prev = m_state[b * ML_HEADS + h]
        bc = jnp.broadcast_to(bcols[:, ML_HEADS + h:ML_HEADS + h + 1], (L, LANES))
        lic = jnp.broadcast_to(g[:, h:h + 1], (L, LANES))
        br = brows[ML_HEADS + h:ML_HEADS + h + 1, :]
        lir = gt[h:h + 1, :]
        log_d = jnp.where(causal, twice(bc) - (br - lir), NEG_BIG)
        m_inter = bc + m_prev
        row_max = jnp.broadcast_to(jnp.max(log_d, axis=-1, keepdims=True), (L, LANES))
        m_row = jnp.maximum(m_inter, row_max)
        d = jnp.exp2(log_d - twice(m_row))
        inter = jnp.exp2(m_inter - m_row)
        qm = jnp.where(sel, qp, jnp.zeros_like(qp))
        s = lax.dot_general(qm, kp, (((1,), (1,)), ((), ())),
                            preferred_element_type=jnp.float32) * d
        v_ext = jnp.concatenate([v_ref[b, :, h * ML_DV:(h + 1) * ML_DV], ones_dv],
                                axis=-1)
        num = twice(inter) * jnp.dot(qm, c_bf, preferred_element_type=jnp.float32) \
            + jnp.dot(_bf16(s), v_ext, preferred_element_type=jnp.float32)
        den = num[:, ML_DV:2 * ML_DV]
        hval = num[:, 0:ML_DV] / jnp.maximum(jnp.abs(den), jnp.exp2(-m_row))
        h_ms = jnp.dot(_bf16(hval * hval), mean_dv, preferred_element_type=jnp.float32)
        hn = hval * lax.rsqrt(h_ms + EPS)
        hn = hn * ng_ref[:, h * ML_DV:(h + 1) * ML_DV]
        out_ref[b, :, h * ML_DV:(h + 1) * ML_DV] = _bf16(
            hn * o_ref[b, :, h * ML_DV:(h + 1) * ML_DV].astype(jnp.float32))
        b_last = bc[L - 1:L, :]
        log_w = b_last - bc + lic
        m_next = jnp.maximum(b_last + m_prev, jnp.max(log_w, axis=0, keepdims=True))
        w = jnp.exp2(log_w - m_next)
        decay = jnp.exp2(b_last + m_prev - m_next)
        m_state[b * ML_HEADS + h] = m_next
        kw = _bf16(jnp.where(sel, kp.astype(jnp.float32) * w, 0.0))
        upd = lax.dot_general(kw, v_ext, (((0,), (0,)), ((), ())),
                              preferred_element_type=jnp.float32)
        if sub == 0:
            carry[sidx] = (decay, upd)
        else:
            decay0, upd0 = carry.pop(sidx)
            decay_rows = jnp.where(row2 < ML_DQK, twice(decay0), twice(decay))
            c_state[sidx] = decay_rows * c_pair + (upd0 + upd)

    def finish():
        for k in range(batch * ML_HEADS):
            m_scr[k] = m_state[k]
        for k in range(batch * pairs):
            c_scr[k] = c_state[k]

    return batch_prep, head_piece, finish


def _out_proj_router_kernel(x_ref, att_ref, hm_ref, wo_ref, g_ref, wrt_ref, brt_ref,
                            x1_ref, t_ref, row_ref, col_ref, cnt_ref, wo_bf):
    tm = x_ref.shape[0]
    tr = TM_PROJ
    hi = lax.Precision.HIGHEST

    @pl.when(pl.program_id(0) == 0)
    def _():
        wo_bf[...] = _bf16(wo_ref[...])

    logits_all = []
    x1_all = []
    for sb in range(tm // TM_PROJ):
        rs = slice(sb * TM_PROJ, (sb + 1) * TM_PROJ)
        mixed = jnp.concatenate([att_ref[rs, :], hm_ref[rs, :]], axis=1)
        x1_all.append(x_ref[rs, :] + jnp.dot(mixed, wo_bf[...], preferred_element_type=jnp.float32))
    for sb in range(tm // TM_PROJ):
        rs = slice(sb * TM_PROJ, (sb + 1) * TM_PROJ)
        x1 = x1_all[sb]
        x1_ref[rs, :] = x1
        tn = x1 * lax.rsqrt(jnp.mean(x1 * x1, axis=-1, keepdims=True) + EPS) * g_ref[...]
        tn_hi = _bf16(tn)
        t_ref[rs, :] = tn_hi
        tn_lo = _bf16(tn - tn_hi.astype(jnp.float32))
        nt = (((1,), (1,)), ((), ()))
        n_rt = wrt_ref.shape[0] // 2
        p_hi = lax.dot_general(wrt_ref[...], tn_hi, nt, preferred_element_type=jnp.float32)
        p_lo = lax.dot_general(wrt_ref[0:n_rt, :], tn_lo, nt, preferred_element_type=jnp.float32)
        logits_all.append(p_hi[0:n_rt, :] + p_hi[n_rt:, :] + p_lo + brt_ref[...])

    for sb in range(tm // TM_PROJ):
        rs = slice(sb * TM_PROJ, (sb + 1) * TM_PROJ)
        logits = logits_all[sb]
        el_all = logits[0:N_EXPERTS, :]
        gl = logits[N_EXPERTS:N_EXPERTS + SUBLANES, :]
        grow = lax.broadcasted_iota(jnp.int32, gl.shape, 0).astype(jnp.float32)
        gl = jnp.where(grow < N_GROUPS, gl, NEG_BIG)
        gmax = jnp.max(gl, axis=0, keepdims=True)
        grp = jnp.min(jnp.where(gl == gmax, grow, float(N_GROUPS)), axis=0, keepdims=True)
        p_grp = 1.0 / jnp.sum(jnp.exp(gl - gmax), axis=0, keepdims=True)
        erow = lax.broadcasted_iota(jnp.int32, el_all.shape, 0).astype(jnp.float32)
        egrp = jnp.floor(erow * (1.0 / EXPERTS_PER_GROUP))
        el = jnp.where(egrp == grp, el_all, NEG_BIG)
        e1 = jnp.max(el, axis=0, keepdims=True)
        i1 = jnp.min(jnp.where(el == e1, erow, float(N_EXPERTS)), axis=0, keepdims=True)
        el2 = jnp.where(erow == i1, NEG_BIG, el)
        e2 = jnp.max(el2, axis=0, keepdims=True)
        i2 = jnp.min(jnp.where(el2 == e2, erow, float(N_EXPERTS)), axis=0, keepdims=True)
        z2 = jnp.exp(e2 - e1)
        w1 = p_grp / (1.0 + z2)
        w2 = p_grp * z2 / (1.0 + z2)
        sel1 = erow == i1
        sel2 = erow == i2
        onehot = jnp.logical_or(sel1, sel2)
        t_r = lax.broadcasted_iota(jnp.int32, (tr, tr), 0)
        t_c = lax.broadcasted_iota(jnp.int32, (tr, tr), 1)
        before = (t_r < t_c).astype(jnp.bfloat16)
        rank = jnp.dot(onehot.astype(jnp.bfloat16), before, preferred_element_type=jnp.float32)
        cnt = jnp.sum(onehot.astype(jnp.float32), axis=1, keepdims=True)
        cnt_al = jnp.floor((cnt + (SEG_ALIGN - 1)) * (1.0 / SEG_ALIGN)) * SEG_ALIGN
        e_r = lax.broadcasted_iota(jnp.int32, (N_EXPERTS, N_EXPERTS), 0)
        e_c = lax.broadcasted_iota(jnp.int32, (N_EXPERTS, N_EXPERTS), 1)
        lstart = jnp.dot((e_c < e_r).astype(jnp.float32), jnp.broadcast_to(cnt_al, (N_EXPERTS, LANES)),
                         precision=hi, preferred_element_type=jnp.float32)[:, 0:1]
        slot = lstart + rank
        pos1 = jnp.sum(jnp.where(sel1, slot, 0.0), axis=0, keepdims=True)
        pos2 = jnp.sum(jnp.where(sel2, slot, 0.0), axis=0, keepdims=True)
        r8 = lax.broadcasted_iota(jnp.int32, (SUBLANES, tr), 0)
        info = jnp.where(r8 == ROW_POS0, pos1, jnp.where(r8 == ROW_POS1, pos2, jnp.where(
            r8 == ROW_W0, w1, jnp.where(r8 == ROW_W1, w2, jnp.where(
                r8 == ROW_E0, i1, jnp.where(r8 == ROW_E1, i2, 0.0))))))
        row_ref[:, rs] = info

        def split3(w):
            h = _bf16(w).astype(jnp.float32)
            m = _bf16(w - h).astype(jnp.float32)
            return h, m, _bf16(w - h - m).astype(jnp.float32)

        w1h, w1m, w1l = split3(w1)
        w2h, w2m, w2l = split3(w2)
        parts = jnp.where(r8 == 0, w1h, jnp.where(r8 == 1, w1m, jnp.where(r8 == 2, w1l, jnp.where(
            r8 == 3, w2h, jnp.where(r8 == 4, w2m, jnp.where(r8 == 5, w2l, jnp.where(r8 == 6, i1, i2)))))))
        col_ref[rs, :] = jnp.concatenate(
            [info, parts, jnp.zeros((LANES - 2 * SUBLANES, tr), jnp.float32)], axis=0).T
        cnt_ref[sb] = jnp.broadcast_to(cnt, (N_EXPERTS, LANES)).astype(jnp.int32)


def _out_proj_router(x2, att, hm, wo, layer, g, wrt, brt):
    t = x2.shape[0]
    tm = TM_OUT
    row = lambda i: (i, 0)
    fix = lambda i: (0, 0)
    return pl.pallas_call(
        _out_proj_router_kernel,
        grid=(t // tm,),
        in_specs=[
            pl.BlockSpec((tm, D_MODEL), row),
            pl.BlockSpec((tm, ATT_WIDTH), row),
            pl.BlockSpec((tm, ML_WIDTH), row),
            pl.BlockSpec((None, D_MODEL, D_MODEL), lambda i: (layer, 0, 0), pipeline_mode=pl.Buffered(1)),
            pl.BlockSpec((1, D_MODEL), fix),
            pl.BlockSpec((8 * SUBLANES, D_MODEL), fix),
            pl.BlockSpec((4 * SUBLANES, 1), fix),
        ],
        out_specs=[
            pl.BlockSpec((tm, D_MODEL), row),
            pl.BlockSpec((tm, D_MODEL), row),
            pl.BlockSpec((SUBLANES, tm), lambda i: (0, i)),
            pl.BlockSpec((tm, LANES), row),
            pl.BlockSpec((tm // TM_PROJ, N_EXPERTS, LANES), lambda i: (i, 0, 0)),
        ],
        out_shape=(
            jax.ShapeDtypeStruct((t, D_MODEL), jnp.float32),
            jax.ShapeDtypeStruct((t, D_MODEL), jnp.bfloat16),
            jax.ShapeDtypeStruct((SUBLANES, t), jnp.float32),
            jax.ShapeDtypeStruct((t, LANES), jnp.float32),
            jax.ShapeDtypeStruct((t // TM_PROJ, N_EXPERTS, LANES), jnp.int32),
        ),
        scratch_shapes=[pltpu.VMEM((D_MODEL, D_MODEL), jnp.bfloat16)],
        compiler_params=pltpu.CompilerParams(dimension_semantics=("arbitrary",),
                                             vmem_limit_bytes=VMEM_LIMIT),
        name="out_proj_router",
    )(x2, att, hm, wo, g, wrt, brt)


def _dispatch_kernel(dst_ref, nz_ref, zdst_ref, t_ref, col_ref, row_ref, xs_hbm,
                     xbuf, zx, sem_x, sem_z):
    i = pl.program_id(0)
    n = pl.num_programs(0)
    p = i % 2
    tm = t_ref.shape[0]

    def copy(tile, par, j):
        d = pl.ds(pl.multiple_of(dst_ref[tile, j], SEG_ALIGN), SEG_ALIGN)
        return pltpu.make_async_copy(xbuf.at[par, j * SEG_ALIGN:(j + 1) * SEG_ALIGN], xs_hbm.at[d],
                                     sem_x.at[par])

    def zero_copy(e, j):
        d = pl.ds(pl.multiple_of(zdst_ref[e] + j * SEG_ALIGN, SEG_ALIGN), SEG_ALIGN)
        return pltpu.make_async_copy(zx, xs_hbm.at[d], sem_z.at[0])

    def for_zero_chunks(fn):
        for e in range(N_EXPERTS):
            def body(j, c, e=e):
                fn(zero_copy(e, j))
                return c
            lax.fori_loop(0, nz_ref[e], body, 0)

    def wait_tile(par):
        pltpu.make_async_copy(xbuf.at[par], xs_hbm.at[pl.ds(0, L_CAP)], sem_x.at[par]).wait()

    @pl.when(i == 0)
    def _():
        zx[...] = jnp.zeros(zx.shape, zx.dtype)
        for_zero_chunks(lambda cp: cp.start())

    @pl.when(i >= 2)
    def _():
        wait_tile(p)

    pos0 = row_ref[ROW_POS0:ROW_POS0 + 1, :].astype(jnp.int32)
    pos1 = row_ref[ROW_POS1:ROW_POS1 + 1, :].astype(jnp.int32)
    t = t_ref[...]
    side = _bf16(col_ref[...])
    rows = DISPATCH_ROWS
    for c in range(L_CAP // rows):
        r = lax.broadcasted_iota(jnp.int32, (rows, tm), 0) + c * rows
        perm = jnp.logical_or(r == pos0, r == pos1).astype(jnp.bfloat16)
        xbuf[p, c * rows:(c + 1) * rows, 0:D_MODEL] = _bf16(
            jnp.dot(perm, t, preferred_element_type=jnp.float32))
        xbuf[p, c * rows:(c + 1) * rows, D_MODEL:D_XS] = _bf16(
            jnp.dot(perm, side, preferred_element_type=jnp.float32))
        for j in range(c * rows // SEG_ALIGN, (c + 1) * rows // SEG_ALIGN):
            copy(i, p, j).start()

    @pl.when(i == n - 1)
    def _():
        @pl.when(n >= 2)
        def _():
            wait_tile(1 - p)
        wait_tile(p)
        for_zero_chunks(lambda cp: cp.wait())


def _dispatch(t, col, row, dst, nz, zdst):
    n_tok = t.shape[0]
    r_cap, _ = _moe_capacity(n_tok)
    tm = TM_PROJ
    grid_spec = pltpu.PrefetchScalarGridSpec(
        num_scalar_prefetch=3,
        grid=(n_tok // tm,),
        in_specs=[
            pl.BlockSpec((tm, D_MODEL), lambda i, *_: (i, 0)),
            pl.BlockSpec((tm, LANES), lambda i, *_: (i, 0)),
            pl.BlockSpec((SUBLANES, tm), lambda i, *_: (0, i)),
        ],
        out_specs=pl.BlockSpec(memory_space=pl.ANY),
        scratch_shapes=[
            pltpu.VMEM((2, L_CAP, D_XS), jnp.bfloat16),
            pltpu.VMEM((SEG_ALIGN, D_XS), jnp.bfloat16),
            pltpu.SemaphoreType.DMA((2,)),
            pltpu.SemaphoreType.DMA((1,)),
        ],
    )
    return pl.pallas_call(
        _dispatch_kernel,
        grid_spec=grid_spec,
        out_shape=jax.ShapeDtypeStruct((r_cap + DUMP_ROWS, D_XS), jnp.bfloat16),
        compiler_params=pltpu.CompilerParams(dimension_semantics=("arbitrary",),
                                             vmem_limit_bytes=VMEM_LIMIT),
        name="moe_dispatch",
    )(dst, nz, zdst, t, col, row)


def _moe_kernel(layer, blk_e_ref, nused_ref, nvalid_ref, next_e_ref, xs_ref, wg_hbm, wu_hbm, wd_hbm,
                ys_ref, wg_st, wu_st, wd_st, wg_bf, wu_bf, wd_bf, sem):
    b = pl.program_id(0)
    nv = nvalid_ref[b]
    e = blk_e_ref[b]
    new_expert = jnp.logical_or(b == 0, e != blk_e_ref[jnp.maximum(b - 1, 0)])

    def fetch(expert):
        return (pltpu.make_async_copy(wg_hbm.at[layer, expert], wg_st, sem.at[0]),
                pltpu.make_async_copy(wu_hbm.at[layer, expert], wu_st, sem.at[1]),
                pltpu.make_async_copy(wd_hbm.at[layer, expert], wd_st, sem.at[2]))

    @pl.when(b == 0)
    def _():
        for cp in fetch(e):
            cp.start()

    @pl.when(new_expert)
    def _():
        for cp in fetch(e):
            cp.wait()
        wg_bf[...] = _bf16(wg_st[...])
        wu_bf[...] = _bf16(wu_st[...])
        wd_bf[...] = _bf16(wd_st[...])

        @pl.when(next_e_ref[b] != e)
        def _():
            for cp in fetch(next_e_ref[b]):
                cp.start()

    def compute(rows):
        x = xs_ref[0:rows, 0:D_MODEL]
        y = None
        fw = D_FF_EXPERT // MOE_FF_SPLIT
        for c in range(MOE_FF_SPLIT):
            cols = slice(c * fw, (c + 1) * fw)
            a = jnp.dot(x, wg_bf[:, cols], preferred_element_type=jnp.float32)
            u = jnp.dot(x, wu_bf[:, cols], preferred_element_type=jnp.float32)
            h = a * jax.nn.sigmoid(a) * u
            yc = jnp.dot(_bf16(h), wd_bf[cols, :], preferred_element_type=jnp.float32)
            y = yc if y is None else y + yc
        sd = xs_ref[0:rows, D_MODEL:D_XS].astype(jnp.float32)
        e_blk = blk_e_ref[b].astype(jnp.float32)
        w0 = sd[:, COL_W0H:COL_W0H + 1] + sd[:, COL_W0H + 1:COL_W0H + 2] + sd[:, COL_W0H + 2:COL_W0H + 3]
        w1 = sd[:, COL_W1H:COL_W1H + 1] + sd[:, COL_W1H + 1:COL_W1H + 2] + sd[:, COL_W1H + 2:COL_W1H + 3]
        w = jnp.where(sd[:, COL_E0:COL_E0 + 1] == e_blk, w0, w1)
        ys_ref[0:rows, :] = _bf16(y * w)

    half = MOE_BM // 2

    @pl.when(nv > half)
    def _():
        compute(MOE_BM)

    @pl.when(jnp.logical_and(nv > 0, nv <= half))
    def _():
        compute(half)
        ys_ref[half:, :] = jnp.zeros((MOE_BM - half, D_MODEL), ys_ref.dtype)


def _moe(xs, blk_e, nused, nvalid, next_e, wg, wu, wd, layer):
    blk = lambda b, be, nu, nv, ne: (jnp.maximum(jnp.minimum(b, nu[0] - 1), 0), 0)
    r_cap = xs.shape[0] - DUMP_ROWS
    grid_spec = pltpu.PrefetchScalarGridSpec(
        num_scalar_prefetch=4,
        grid=(r_cap // MOE_BM,),
        in_specs=[
            pl.BlockSpec((MOE_BM, D_XS), blk),
            pl.BlockSpec(memory_space=pl.ANY),
            pl.BlockSpec(memory_space=pl.ANY),
            pl.BlockSpec(memory_space=pl.ANY),
        ],
        out_specs=pl.BlockSpec((MOE_BM, D_MODEL), blk),
        scratch_shapes=[
            pltpu.VMEM((D_MODEL, D_FF_EXPERT), jnp.float32),
            pltpu.VMEM((D_MODEL, D_FF_EXPERT), jnp.float32),
            pltpu.VMEM((D_FF_EXPERT, D_MODEL), jnp.float32),
            pltpu.VMEM((D_MODEL, D_FF_EXPERT), jnp.bfloat16),
            pltpu.VMEM((D_MODEL, D_FF_EXPERT), jnp.bfloat16),
            pltpu.VMEM((D_FF_EXPERT, D_MODEL), jnp.bfloat16),
            pltpu.SemaphoreType.DMA((3,)),
        ],
    )
    return pl.pallas_call(
        functools.partial(_moe_kernel, layer),
        grid_spec=grid_spec,
        out_shape=jax.ShapeDtypeStruct((r_cap, D_MODEL), jnp.bfloat16),
        compiler_params=pltpu.CompilerParams(dimension_semantics=("arbitrary",),
                                             vmem_limit_bytes=VMEM_LIMIT),
        name="moe_experts",
    )(blk_e, nused, nvalid, next_e, xs, wg, wu, wd)


def _combine_kernel(src_ref, x1_ref, col_ref, ys_hbm, out_ref, ybuf, sem):
    i = pl.program_id(0)
    n = pl.num_programs(0)
    p = i % COMBINE_BUFS
    tm = x1_ref.shape[0]

    def start_tile(tile, par):
        for j in range(N_CHUNK):
            s = pl.ds(pl.multiple_of(src_ref[tile, j], SEG_ALIGN), SEG_ALIGN)
            pltpu.make_async_copy(ys_hbm.at[s], ybuf.at[par, j * SEG_ALIGN:(j + 1) * SEG_ALIGN],
                                  sem.at[par]).start()

    def wait_buf(par):
        pltpu.make_async_copy(ys_hbm.at[pl.ds(0, L_CAP)], ybuf.at[par], sem.at[par]).wait()

    @pl.when(i == 0)
    def _():
        start_tile(0, 0)
        start_tile(jnp.minimum(1, n - 1), 1)

    wait_buf(p)
    start_tile(jnp.minimum(i + 2, n - 1), (i + 2) % COMBINE_BUFS)

    rows = COMBINE_ROWS
    for c in range(tm // rows):
        col = col_ref[c * rows:(c + 1) * rows, :]
        pos0 = col[:, ROW_POS0:ROW_POS0 + 1].astype(jnp.int32)
        pos1 = col[:, ROW_POS1:ROW_POS1 + 1].astype(jnp.int32)
        l = lax.broadcasted_iota(jnp.int32, (rows, L_CAP), 1)
        perm = jnp.logical_or(l == pos0, l == pos1).astype(jnp.bfloat16)
        y = jnp.dot(perm, ybuf[p], preferred_element_type=jnp.float32)
        out_ref[c * rows:(c + 1) * rows, :] = x1_ref[c * rows:(c + 1) * rows, :] + y

    @pl.when(i == n - 1)
    def _():
        wait_buf((i + 1) % COMBINE_BUFS)
        wait_buf((i + 2) % COMBINE_BUFS)


def _combine(x1, col, ys, src):
    n_tok = x1.shape[0]
    tm = TM_PROJ
    grid_spec = pltpu.PrefetchScalarGridSpec(
        num_scalar_prefetch=1,
        grid=(n_tok // tm,),
        in_specs=[
            pl.BlockSpec((tm, D_MODEL), lambda i, *_: (i, 0)),
            pl.BlockSpec((tm, LANES), lambda i, *_: (i, 0)),
            pl.BlockSpec(memory_space=pl.ANY),
        ],
        out_specs=pl.BlockSpec((tm, D_MODEL), lambda i, *_: (i, 0)),
        scratch_shapes=[
            pltpu.VMEM((COMBINE_BUFS, L_CAP, D_MODEL), jnp.bfloat16),
            pltpu.SemaphoreType.DMA((COMBINE_BUFS,)),
        ],
    )
    return pl.pallas_call(
        _combine_kernel,
        grid_spec=grid_spec,
        out_shape=jax.ShapeDtypeStruct((n_tok, D_MODEL), jnp.float32),
        compiler_params=pltpu.CompilerParams(dimension_semantics=("arbitrary",),
                                             vmem_limit_bytes=VMEM_LIMIT),
        name="moe_combine",
    )(src, x1, col, ys)


def _routing_tables(counts, n_blk_cap):
    r_cap = n_blk_cap * MOE_BM
    i32 = jnp.int32
    ca = ((counts + (SEG_ALIGN - 1)) // SEG_ALIGN) * SEG_ALIGN
    lend = jnp.cumsum(ca, axis=1)
    lstart = lend - ca
    tot = jnp.sum(ca, axis=0)
    region = ((tot + (MOE_BM - 1)) // MOE_BM) * MOE_BM
    rend = jnp.cumsum(region)
    base = rend - region
    gstart = base[None, :] + jnp.cumsum(ca, axis=0) - ca
    j16 = jnp.arange(N_CHUNK, dtype=i32) * SEG_ALIGN
    in_seg = jnp.logical_and(lstart[:, None, :] <= j16[None, :, None], j16[None, :, None] < lend[:, None, :])
    seg_row = (jnp.sum(jnp.where(in_seg, (gstart - lstart)[:, None, :], 0), axis=2) + j16[None, :]).astype(i32)
    used = j16[None, :] < lend[:, -1:]
    tile_par = (jnp.arange(counts.shape[0], dtype=i32) % 2)[:, None]
    park = r_cap + (tile_par * N_CHUNK + jnp.arange(N_CHUNK, dtype=i32)[None, :]) * SEG_ALIGN
    dst = jnp.where(used, seg_row, park).astype(i32)
    src = jnp.where(used, seg_row, 0).astype(i32)
    nused = (rend[-1] // MOE_BM).astype(i32)
    brow = jnp.minimum(jnp.arange(n_blk_cap, dtype=i32), nused - 1) * MOE_BM
    blk_e = jnp.minimum(jnp.sum(rend[None, :] <= brow[:, None], axis=1), N_EXPERTS - 1).astype(i32)
    zstart = (base + tot).astype(i32)
    nzc = ((region - tot) // SEG_ALIGN).astype(i32)
    blk_i = jnp.arange(n_blk_cap, dtype=i32)
    seg_end = jnp.sum(jnp.where(blk_e[:, None] == jnp.arange(N_EXPERTS)[None, :], (base + tot)[None, :], 0), axis=1)
    nvalid = jnp.where(blk_i < nused, jnp.clip(seg_end - blk_i * MOE_BM, 0, MOE_BM), 0).astype(i32)
    eid = jnp.arange(N_EXPERTS, dtype=i32)
    later = jnp.logical_and(eid[None, :] > eid[:, None], (region > 0)[None, :])
    nxt = jnp.min(jnp.where(later, eid[None, :], N_EXPERTS), axis=1)
    nxt = jnp.where(nxt < N_EXPERTS, nxt, eid)
    next_e = jnp.sum(jnp.where(blk_e[:, None] == eid[None, :], nxt[None, :], 0), axis=1).astype(i32)
    return dst, src, nzc, zstart, blk_e, nused[None], nvalid, next_e


def _t5_bucket_np(dist):
    max_exact = N_BUCKETS // 2
    d = np.maximum(dist, 1).astype(np.float32)
    large = max_exact + (np.log(d / max_exact) / math.log(MAX_DISTANCE / max_exact)
                         * (N_BUCKETS - max_exact)).astype(np.int32)
    large = np.minimum(large, N_BUCKETS - 1)
    return np.where(dist < max_exact, dist, large)


def _attention_bias(rel_bias):
    qi = np.arange(ATT_BLOCK)[:, None]
    kj = np.arange(2 * ATT_BLOCK)[None, :]
    dist = qi + ATT_BLOCK - kj
    in_window = (dist >= 0) & (dist < WINDOW)
    bucket = _t5_bucket_np(np.clip(dist, 0, WINDOW - 1))
    onehot = (bucket[None] == np.arange(N_BUCKETS)[:, None, None]).astype(np.float32)
    bias = jnp.einsum('nh,nqk->hqk', rel_bias.astype(jnp.float32), jnp.asarray(onehot),
                      precision=lax.Precision.HIGHEST)
    bias = jnp.where(jnp.asarray(in_window)[None], bias * LOG2E, NEG_BIG)
    no_prev = jnp.asarray(kj < ATT_BLOCK)[None]
    return jnp.stack([bias, jnp.where(no_prev, NEG_BIG, bias)])


def _block_diag_mean(width, block):
    idx = np.arange(width) // block
    return jnp.asarray((idx[:, None] == idx[None, :]).astype(np.float32) / block, dtype=jnp.bfloat16)


def kernel(x, rel_bias, norm_mix_g, w_in, q_norm_g, k_norm_g, attn_sink, conv_w, conv_b, gate_b,
           mlstm_norm_g, w_out, norm_ffn_g, w_router_group, b_router_group, w_router_expert,
           b_router_expert, w_gate, w_up, w_down):
    batch, seq_len, _ = x.shape
    n_tok = batch * seq_len
    assert seq_len % TM_OUT == 0 and seq_len % ML_KCHUNK == 0
    f32 = jnp.float32
    bias = _attention_bias(rel_bias)
    bdq = _block_diag_mean(ATT_WIDTH, ATT_HEAD_DIM)
    bdk = _block_diag_mean(LANES, ATT_HEAD_DIM)
    x2 = x.reshape(n_tok, D_MODEL)
    for l in range(DEPTH):
        wgt = _bf16(w_in[l, :, O_G:N_IN].T)
        qg = (jnp.tile(q_norm_g[l].astype(f32), ATT_HEADS) * (ATT_HEAD_DIM ** -0.5 * LOG2E))[None, :]
        kg = jnp.tile(k_norm_g[l].astype(f32), ATT_KV_HEADS)[None, :]
        gbt = gate_b[l].astype(f32)[:, None]
        qn, kn, vd, hm = _front(
            x2.reshape(batch, seq_len, D_MODEL), norm_mix_g[l][None, :], w_in, l, wgt, qg, kg, bdq, bdk,
            conv_w[l], conv_b[l][None, :], gbt, mlstm_norm_g[l][None, :])
        flat = lambda a: a.reshape(n_tok, a.shape[-1])
        att = _attention(flat(qn), flat(kn), flat(vd), bias, attn_sink[l].astype(f32) * LOG2E,
                         batch, seq_len)
        n_rt = 4 * SUBLANES
        wrt = jnp.pad(jnp.concatenate([w_router_expert[l], w_router_group[l]], axis=1).astype(f32).T,
                      ((0, n_rt - N_EXPERTS - N_GROUPS), (0, 0)))
        brt = jnp.pad(jnp.concatenate([b_router_expert[l], b_router_group[l]]).astype(f32),
                      (0, n_rt - N_EXPERTS - N_GROUPS))[:, None]
        wrt_hi = _bf16(wrt)
        wrt = jnp.concatenate([wrt_hi, _bf16(wrt - wrt_hi.astype(f32))], axis=0)
        x1, t, row, col, cnt = _out_proj_router(x2, att, hm.reshape(n_tok, ML_WIDTH), w_out, l,
                                                norm_ffn_g[l][None, :], wrt, brt)
        _, n_blk_cap = _moe_capacity(n_tok)
        dst, src, nz, zdst, blk_e, nused, nvalid, next_e = _routing_tables(cnt[:, :, 0], n_blk_cap)
        xs = _dispatch(t, col, row, dst, nz, zdst)
        ys = _moe(xs, blk_e, nused, nvalid, next_e, w_gate, w_up, w_down, l)
        x2 = _combine(x1, col, ys, src)
    return x2.reshape(batch, seq_len, D_MODEL)
```

```python
import functools
import math

import jax
import jax.numpy as jnp
import numpy as np
from jax import lax
from jax.experimental import pallas as pl
from jax.experimental.pallas import tpu as pltpu

D_MODEL = 1024
DEPTH = 2
ATT_HEADS = 8
ATT_KV_HEADS = 2
ATT_HEAD_DIM = 64
ATT_WIDTH = ATT_HEADS * ATT_HEAD_DIM
WINDOW = 128
ATT_BLOCK = 128
N_BUCKETS = 32
MAX_DISTANCE = 128
ML_HEADS = 4
ML_DQK = 64
ML_DV = 128
ML_WIDTH = ML_HEADS * ML_DV
ML_CHUNK = 64
CONV_K = 4
N_GROUPS = 4
EXPERTS_PER_GROUP = 4
N_EXPERTS = N_GROUPS * EXPERTS_PER_GROUP
D_FF_EXPERT = 512
EPS = 1e-6

LANES = 128
SUBLANES = 8
NEG_BIG = -1e30
LOG2E = math.log2(math.e)
VMEM_LIMIT = 48 * 1024 * 1024
FRONT_VMEM_LIMIT = 56 * 1024 * 1024

O_Q = 0
O_K = O_Q + ATT_WIDTH
O_V = O_K + ATT_KV_HEADS * ATT_HEAD_DIM
O_QM = O_V + ATT_KV_HEADS * ATT_HEAD_DIM
O_KM = O_QM + ML_HEADS * ML_DQK
O_VM = O_KM + ML_HEADS * ML_DQK
O_OM = O_VM + ML_WIDTH
O_G = O_OM + ML_WIDTH
N_IN = O_G + 2 * ML_HEADS

ATT_STEP_BLOCKS = 4
TM_OUT = 1024
TM_PROJ = 512
ML_KCHUNK = 256

SEG_ALIGN = 16
MOE_BM = 512
MOE_FF_SPLIT = 2
L_CAP = 2 * TM_PROJ + N_EXPERTS * SEG_ALIGN
N_CHUNK = L_CAP // SEG_ALIGN
D_XS = D_MODEL + LANES
DUMP_ROWS = 2 * N_CHUNK * SEG_ALIGN
DISPATCH_ROWS = 256
COMBINE_ROWS = 256
COMBINE_BUFS = 3
ROW_POS0, ROW_POS1, ROW_W0, ROW_W1, ROW_E0, ROW_E1 = 0, 1, 2, 3, 4, 5
COL_W0H, COL_W1H, COL_E0, COL_E1 = 8, 11, 14, 15


def _moe_capacity(n_tok):
    n_tiles = n_tok // TM_PROJ
    rows = 2 * n_tok + n_tiles * N_EXPERTS * (SEG_ALIGN - 1) + N_EXPERTS * (MOE_BM - SEG_ALIGN)
    n_blk = -(-rows // MOE_BM)
    return n_blk * MOE_BM, n_blk


def _bf16(a):
    return a.astype(jnp.bfloat16)


def _split3(a):
    hi = _bf16(a)
    r1 = a - hi.astype(jnp.float32)
    mid = _bf16(r1)
    return hi, mid, _bf16(r1 - mid.astype(jnp.float32))


def _log_sigmoid(z):
    return jnp.minimum(z, 0.0) - jnp.log(1.0 + jnp.exp(-jnp.abs(z)))


def _front_kernel(x_ref, g_ref, w_ref, wgt_ref, qg_ref, kg_ref, bdq_ref, bdk_ref, cw_ref, cb_ref,
                  gbt_ref, ng_ref,
                  qn_ref, kn_ref, vd_ref, hm_ref,
                  w_bf, conv_scr, qkm_s, vm_s, om_s, gt_s, c_scr, m_scr):
    s = pl.program_id(0)
    batch = x_ref.shape[0]
    L = ML_KCHUNK
    wr = s % 2
    rd = 1 - wr

    n_chunks = pl.num_programs(0) - 1

    @pl.when(s == 0)
    def _():
        w_bf[...] = _bf16(w_ref[...])
        conv_scr[:, 0:SUBLANES, :] = jnp.zeros((batch, SUBLANES, conv_scr.shape[2]), jnp.float32)
        c_scr[...] = jnp.zeros(c_scr.shape, jnp.float32)
        m_scr[...] = jnp.zeros(m_scr.shape, jnp.float32)

    @pl.when(s > 0)
    def _():
        conv_scr[:, 0:SUBLANES, :] = conv_scr[:, L:L + SUBLANES, :]

    def scan_pieces():
        return _mlstm_pieces(qkm_s.at[rd], vm_s.at[rd], om_s.at[rd], gt_s.at[rd], ng_ref, hm_ref,
                             c_scr, m_scr, batch)

    def proj_pieces():
        return _project_pieces(x_ref, g_ref, w_bf, wgt_ref, qg_ref, kg_ref, bdq_ref, bdk_ref, cw_ref,
                               cb_ref, gbt_ref, qn_ref, kn_ref, vd_ref, qkm_s.at[wr], vm_s.at[wr],
                               om_s.at[wr], gt_s.at[wr], conv_scr)

    @pl.when(s == 0)
    def _():
        norm_part, proj_parts = proj_pieces()
        hbs = [norm_part(b) for b in range(batch)]
        for part in proj_parts:
            for b in range(batch):
                part(b, hbs[b])

    @pl.when(jnp.logical_and(s > 0, s < n_chunks))
    def _():
        scan_prep, scan_gates, scan_head, scan_finish = scan_pieces()
        norm_part, proj_parts = proj_pieces()
        preps = [scan_prep(b) for b in range(batch)]
        hbs = [norm_part(b) for b in range(batch)]
        gates = [scan_gates(b, 0, preps[b]) for b in range(batch)]
        for stage in range(len(proj_parts)):
            if stage < ML_HEADS:
                nxt = []
                for b in range(batch):
                    scan_head(b, stage, gates[b])
                    if stage + 1 < ML_HEADS:
                        nxt.append(scan_gates(b, stage + 1, preps[b]))
                gates = nxt
            for b in range(batch):
                proj_parts[stage](b, hbs[b])
        scan_finish()

    @pl.when(s == n_chunks)
    def _():
        scan_prep, scan_gates, scan_head, scan_finish = scan_pieces()
        preps = [scan_prep(b) for b in range(batch)]
        gates = [scan_gates(b, 0, preps[b]) for b in range(batch)]
        for h in range(ML_HEADS):
            nxt = []
            for b in range(batch):
                scan_head(b, h, gates[b])
                if h + 1 < ML_HEADS:
                    nxt.append(scan_gates(b, h + 1, preps[b]))
            gates = nxt
        scan_finish()


def _project_pieces(x_ref, g_ref, w_bf, wgt_ref, qg_ref, kg_ref, bdq_ref, bdk_ref, cw_ref, cb_ref,
                    gbt_ref, qn_ref, kn_ref, vd_ref, qkm_ref, vm_ref, om_ref, gatet_ref, conv_scr):
    L = ML_KCHUNK
    conv_w = 2 * ML_HEADS * ML_DQK

    def proj(hb, c0, width):
        return jnp.dot(hb, w_bf[:, c0:c0 + width], preferred_element_type=jnp.float32)

    def norm_part(b):
        x = x_ref[b]
        hn = x * lax.rsqrt(jnp.mean(x * x, axis=-1, keepdims=True) + EPS) * g_ref[...]
        return _bf16(hn)

    def q_part(b, hb):
        q = proj(hb, O_Q, ATT_WIDTH)
        q_ms = jnp.dot(_bf16(q * q), bdq_ref[...], preferred_element_type=jnp.float32)
        qn_ref[b] = _bf16(q * lax.rsqrt(q_ms + EPS) * qg_ref[...])

    def kv_part(b, hb):
        kv = proj(hb, O_K, 2 * LANES)
        k = kv[:, 0:LANES]
        v = kv[:, LANES:2 * LANES]
        k_ms = jnp.dot(_bf16(k * k), bdk_ref[...], preferred_element_type=jnp.float32)
        kn = k * lax.rsqrt(k_ms + EPS) * kg_ref[...]
        low = lax.broadcasted_iota(jnp.int32, kn.shape, 1) < ATT_HEAD_DIM

        def dup_heads(a):
            swapped = pltpu.roll(a, ATT_HEAD_DIM, axis=1)
            return jnp.concatenate([jnp.where(low, a, swapped), jnp.where(low, swapped, a)], axis=1)

        kn_ref[b] = _bf16(dup_heads(kn))
        vd_ref[b] = _bf16(dup_heads(v))

    def conv_part(b, hb):
        qk = proj(hb, O_QM, conv_w)
        conv_scr[b, SUBLANES:SUBLANES + L, :] = qk
        y = qk * cw_ref[CONV_K - 1:CONV_K, :] + cb_ref[...]
        for j in range(CONV_K - 1):
            off = SUBLANES - (CONV_K - 1) + j
            y = y + conv_scr[b, off:off + L, :] * cw_ref[j:j + 1, :]
        y = y * jax.nn.sigmoid(y)
        lane = lax.broadcasted_iota(jnp.int32, y.shape, 1)
        y = jnp.where(lane >= ML_HEADS * ML_DQK, y * (ML_DQK ** -0.5), y)
        qkm_ref[b] = _bf16(y)

    def vm_part(b, hb):
        vm_ref[b] = _bf16(proj(hb, O_VM, ML_WIDTH))

    def om_part(b, hb):
        om_ref[b] = _bf16(jax.nn.sigmoid(proj(hb, O_OM, ML_WIDTH)))

        gt = lax.dot_general(wgt_ref[...], hb, (((1,), (1,)), ((), ())),
                             preferred_element_type=jnp.float32) + gbt_ref[...]
        grow = lax.broadcasted_iota(jnp.int32, gt.shape, 0)
        gatet_ref[b] = jnp.where(grow >= ML_HEADS, _log_sigmoid(gt), gt) * LOG2E

    return norm_part, (q_part, kv_part, conv_part, vm_part, om_part)


def _front(x3, g, w_in, layer, wgt, qg, kg, bdq, bdk, cw, cb, gbt, ng):
    batch, seq_len, _ = x3.shape
    L = ML_KCHUNK
    nc = seq_len // L
    fix = lambda s: (0, 0)
    cur = lambda s: (0, jnp.minimum(s, nc - 1), 0)
    prev = lambda s: (0, jnp.maximum(s - 1, 0), 0)
    kv_w = 2 * ATT_KV_HEADS * ATT_HEAD_DIM
    conv_w = 2 * ML_HEADS * ML_DQK
    return pl.pallas_call(
        _front_kernel,
        grid=(nc + 1,),
        in_specs=[
            pl.BlockSpec((batch, L, D_MODEL), cur),
            pl.BlockSpec((1, D_MODEL), fix),
            pl.BlockSpec((None, D_MODEL, N_IN), lambda s: (layer, 0, 0), pipeline_mode=pl.Buffered(1)),
            pl.BlockSpec((SUBLANES, D_MODEL), fix),
            pl.BlockSpec((1, ATT_WIDTH), fix),
            pl.BlockSpec((1, LANES), fix),
            pl.BlockSpec((ATT_WIDTH, ATT_WIDTH), fix),
            pl.BlockSpec((LANES, LANES), fix),
            pl.BlockSpec((CONV_K, conv_w), fix),
            pl.BlockSpec((1, conv_w), fix),
            pl.BlockSpec((SUBLANES, 1), fix),
            pl.BlockSpec((1, ML_WIDTH), fix),
        ],
        out_specs=[
            pl.BlockSpec((batch, L, ATT_WIDTH), cur),
            pl.BlockSpec((batch, L, kv_w), cur),
            pl.BlockSpec((batch, L, kv_w), cur),
            pl.BlockSpec((batch, L, ML_WIDTH), prev),
        ],
        out_shape=(
            jax.ShapeDtypeStruct((batch, seq_len, ATT_WIDTH), jnp.bfloat16),
            jax.ShapeDtypeStruct((batch, seq_len, kv_w), jnp.bfloat16),
            jax.ShapeDtypeStruct((batch, seq_len, kv_w), jnp.bfloat16),
            jax.ShapeDtypeStruct((batch, seq_len, ML_WIDTH), jnp.bfloat16),
        ),
        scratch_shapes=[
            pltpu.VMEM((D_MODEL, N_IN), jnp.bfloat16),
            pltpu.VMEM((batch, L + 2 * SUBLANES, conv_w), jnp.float32),
            pltpu.VMEM((2, batch, L, conv_w), jnp.bfloat16),
            pltpu.VMEM((2, batch, L, ML_WIDTH), jnp.bfloat16),
            pltpu.VMEM((2, batch, L, ML_WIDTH), jnp.bfloat16),
            pltpu.VMEM((2, batch, SUBLANES, L), jnp.float32),
            pltpu.VMEM((batch * ML_HEADS // 2, 2 * ML_DQK, 2 * ML_DV), jnp.float32),
            pltpu.VMEM((batch * ML_HEADS, 1, LANES), jnp.float32),
        ],
        compiler_params=pltpu.CompilerParams(dimension_semantics=("arbitrary",),
                                             vmem_limit_bytes=FRONT_VMEM_LIMIT),
        name="proj_mlstm",
    )(x3, g, w_in, wgt, qg, kg, bdq, bdk, cw, cb, gbt, ng)


def _attn_kernel(sink_ref, q_ref, kp_ref, kc_ref, vp_ref, vc_ref, bias_ref, o_ref):
    i = pl.program_id(1)
    blk = ATT_BLOCK
    lane = lax.broadcasted_iota(jnp.int32, (blk, LANES), 1)
    low = lane < ATT_HEAD_DIM
    group = ATT_HEADS // ATT_KV_HEADS
    ones = jnp.ones((2 * blk, LANES), jnp.bfloat16)
    for j in range(ATT_STEP_BLOCKS):
        rows = slice(j * blk, (j + 1) * blk)
        first = (i == 0).astype(jnp.int32) if j == 0 else 0
        for kv in range(ATT_KV_HEADS):
            lanes = slice(kv * LANES, (kv + 1) * LANES)
            if j == 0:
                kcat = jnp.concatenate([kp_ref[:, lanes], kc_ref[0:blk, lanes]], axis=0)
                vrows = jnp.concatenate([vp_ref[:, lanes], vc_ref[0:blk, lanes]], axis=0)
            else:
                kcat = kc_ref[(j - 1) * blk:(j + 1) * blk, lanes]
                vrows = vc_ref[(j - 1) * blk:(j + 1) * blk, lanes]
            vcat = jnp.concatenate([vrows, ones], axis=1)
            heads = range(kv * group, (kv + 1) * group)
            qs = []
            for h in heads:
                qp = q_ref[rows, (h // 2) * LANES:(h // 2 + 1) * LANES]
                qs.append(jnp.where(low if h % 2 == 0 else jnp.logical_not(low), qp, jnp.zeros_like(qp)))
            s_all = lax.dot_general(jnp.concatenate(qs, axis=0), kcat, (((1,), (1,)), ((), ())),
                                    preferred_element_type=jnp.float32)
            ps, ms = [], []
            for n, h in enumerate(heads):
                logits = s_all[n * blk:(n + 1) * blk, :] + bias_ref[first, h]
                row_max = jnp.broadcast_to(jnp.max(logits, axis=-1, keepdims=True), (blk, LANES))
                m = jnp.maximum(row_max, sink_ref[h])
                ps.append(_bf16(jnp.exp2(logits - jnp.concatenate([m, m], axis=1))))
                ms.append(m)
            o_all = jnp.dot(jnp.concatenate(ps, axis=0), vcat, preferred_element_type=jnp.float32)
            outs = []
            for n, h in enumerate(heads):
                o = o_all[n * blk:(n + 1) * blk, :]
                den = o[:, LANES:2 * LANES] + jnp.exp2(sink_ref[h] - ms[n])
                outs.append(o[:, 0:LANES] / den)
            for n in range(0, group, 2):
                pair = (kv * group + n) // 2
                o_ref[rows, pair * LANES:(pair + 1) * LANES] = _bf16(
                    jnp.where(low, outs[n], outs[n + 1]))


def _attention(qn, kn, vd, bias, sink, batch, seq_len):
    step = ATT_STEP_BLOCKS * ATT_BLOCK
    ns = seq_len // step
    cur = lambda b, i, s: (b * ns + i, 0)
    prev = lambda b, i, s: ((b * ns + i) * ATT_STEP_BLOCKS - jnp.minimum(i, 1), 0)
    grid_spec = pltpu.PrefetchScalarGridSpec(
        num_scalar_prefetch=1,
        grid=(batch, ns),
        in_specs=[
            pl.BlockSpec((step, ATT_WIDTH), cur),
            pl.BlockSpec((ATT_BLOCK, 2 * LANES), prev),
            pl.BlockSpec((step, 2 * LANES), cur),
            pl.BlockSpec((ATT_BLOCK, 2 * LANES), prev),
            pl.BlockSpec((step, 2 * LANES), cur),
            pl.BlockSpec((2, ATT_HEADS, ATT_BLOCK, 2 * ATT_BLOCK), lambda b, i, s: (0, 0, 0, 0)),
        ],
        out_specs=pl.BlockSpec((step, ATT_WIDTH), cur),
    )
    return pl.pallas_call(
        _attn_kernel,
        grid_spec=grid_spec,
        out_shape=jax.ShapeDtypeStruct((batch * seq_len, ATT_WIDTH), jnp.bfloat16),
        compiler_params=pltpu.CompilerParams(dimension_semantics=("arbitrary", "arbitrary"),
                                             vmem_limit_bytes=VMEM_LIMIT),
        name="swa_attention",
    )(sink, qn, kn, kn, vd, vd, bias)


def _mlstm_pieces(qk_ref, v_ref, o_ref, gt_ref, ng_ref, out_ref, c_scr, m_scr, batch):
    L = ML_KCHUNK
    pairs = ML_HEADS // 2
    r_i = lax.broadcasted_iota(jnp.int32, (L, L), 0)
    c_i = lax.broadcasted_iota(jnp.int32, (L, L), 1)
    causal = c_i <= r_i
    tril_bf = causal.astype(jnp.bfloat16)
    triu_bf = (r_i <= c_i).astype(jnp.bfloat16)
    mean_dv = jnp.full((ML_DV, ML_DV), 1.0 / ML_DV, jnp.bfloat16)
    lane = lax.broadcasted_iota(jnp.int32, (L, LANES), 1)
    low = lane < ML_DQK
    ones_dv = jnp.ones((L, ML_DV), jnp.bfloat16)
    row2 = lax.broadcasted_iota(jnp.int32, (2 * ML_DQK, 2 * ML_DV), 0)

    def twice(a):
        return jnp.concatenate([a, a], axis=1)

    m_state = [m_scr[k] for k in range(batch * ML_HEADS)]
    c_state = [c_scr[k] for k in range(batch * pairs)]

    def batch_prep(b):
        gt = gt_ref[b]
        g = jnp.concatenate([gt, jnp.zeros((LANES - SUBLANES, L), jnp.float32)], axis=0).T
        bcols = sum(jnp.dot(tril_bf, part, preferred_element_type=jnp.float32) for part in _split3(g))
        brows = sum(jnp.dot(part, triu_bf, preferred_element_type=jnp.float32) for part in _split3(gt))
        return gt, g, bcols, brows

    carry = {}

    def gate_piece(b, h, prep):
        gt, g, bcols, brows = prep
        m_prev = m_state[b * ML_HEADS + h]
        bc = jnp.broadcast_to(bcols[:, ML_HEADS + h:ML_HEADS + h + 1], (L, LANES))
        lic = jnp.broadcast_to(g[:, h:h + 1], (L, LANES))
        br = brows[ML_HEADS + h:ML_HEADS + h + 1, :]
        lir = gt[h:h + 1, :]
        log_d = jnp.where(causal, twice(bc) - (br - lir), NEG_BIG)
        m_inter = bc + m_prev
        row_max = jnp.broadcast_to(jnp.max(log_d, axis=-1, keepdims=True), (L, LANES))
        m_row = jnp.maximum(m_inter, row_max)
        d = jnp.exp2(log_d - twice(m_row))
        inter = jnp.exp2(m_inter - m_row)
        b_last = bc[L - 1:L, :]
        log_w = b_last - bc + lic
        m_next = jnp.maximum(b_last + m_prev, jnp.max(log_w, axis=0, keepdims=True))
        w = jnp.exp2(log_w - m_next)
        decay = jnp.exp2(b_last + m_prev - m_next)
        m_state[b * ML_HEADS + h] = m_next
        return d, inter, jnp.exp2(-m_row), w, decay

    def head_piece(b, h, gates):
        d, inter, floor, w, decay = gates
        pair, sub = divmod(h, 2)
        sidx = b * pairs + pair
        qp = qk_ref[b, :, pair * LANES:(pair + 1) * LANES]
        kp = qk_ref[b, :, (pairs + pair) * LANES:(pairs + pair + 1) * LANES]
        c_pair = c_state[sidx]
        c_bf = _bf16(c_pair)
        sel = low if sub == 0 else jnp.logical_not(low)
        qm = jnp.where(sel, qp, jnp.zeros_like(qp))
        s = lax.dot_general(qm, kp, (((1,), (1,)), ((), ())),
                            preferred_element_type=jnp.float32) * d
        v_ext = jnp.concatenate([v_ref[b, :, h * ML_DV:(h + 1) * ML_DV], ones_dv],
                                axis=-1)
        num = twice(inter) * jnp.dot(qm, c_bf, preferred_element_type=jnp.float32) \
            + jnp.dot(_bf16(s), v_ext, preferred_element_type=jnp.float32)
        den = num[:, ML_DV:2 * ML_DV]
        hval = num[:, 0:ML_DV] / jnp.maximum(jnp.abs(den), floor)
        h_ms = jnp.dot(_bf16(hval * hval), mean_dv, preferred_element_type=jnp.float32)
        hn = hval * lax.rsqrt(h_ms + EPS)
        hn = hn * ng_ref[:, h * ML_DV:(h + 1) * ML_DV]
        out_ref[b, :, h * ML_DV:(h + 1) * ML_DV] = _bf16(
            hn * o_ref[b, :, h * ML_DV:(h + 1) * ML_DV].astype(jnp.float32))
        kw = _bf16(jnp.where(sel, kp.astype(jnp.float32) * w, 0.0))
        upd = lax.dot_general(kw, v_ext, (((0,), (0,)), ((), ())),
                              preferred_element_type=jnp.float32)
        if sub == 0:
            carry[sidx] = (decay, upd)
        else:
            decay0, upd0 = carry.pop(sidx)
            decay_rows = jnp.where(row2 < ML_DQK, twice(decay0), twice(decay))
            c_state[sidx] = decay_rows * c_pair + (upd0 + upd)

    def finish():
        for k in range(batch * ML_HEADS):
            m_scr[k] = m_state[k]
        for k in range(batch * pairs):
            c_scr[k] = c_state[k]

    return batch_prep, gate_piece, head_piece, finish


def _out_proj_router_kernel(x_ref, att_ref, hm_ref, wo_ref, g_ref, wrt_ref, brt_ref,
                            x1_ref, t_ref, row_ref, col_ref, cnt_ref, wo_bf):
    tm = x_ref.shape[0]
    tr = TM_PROJ
    hi = lax.Precision.HIGHEST

    @pl.when(pl.program_id(0) == 0)
    def _():
        wo_bf[...] = _bf16(wo_ref[...])

    logits_all = []
    x1_all = []
    for sb in range(tm // TM_PROJ):
        rs = slice(sb * TM_PROJ, (sb + 1) * TM_PROJ)
        mixed = jnp.concatenate([att_ref[rs, :], hm_ref[rs, :]], axis=1)
        x1_all.append(x_ref[rs, :] + jnp.dot(mixed, wo_bf[...], preferred_element_type=jnp.float32))
    for sb in range(tm // TM_PROJ):
        rs = slice(sb * TM_PROJ, (sb + 1) * TM_PROJ)
        x1 = x1_all[sb]
        x1_ref[rs, :] = x1
        tn = x1 * lax.rsqrt(jnp.mean(x1 * x1, axis=-1, keepdims=True) + EPS) * g_ref[...]
        tn_hi = _bf16(tn)
        t_ref[rs, :] = tn_hi
        tn_lo = _bf16(tn - tn_hi.astype(jnp.float32))
        nt = (((1,), (1,)), ((), ()))
        n_rt = wrt_ref.shape[0] // 2
        p_hi = lax.dot_general(wrt_ref[...], tn_hi, nt, preferred_element_type=jnp.float32)
        p_lo = lax.dot_general(wrt_ref[0:n_rt, :], tn_lo, nt, preferred_element_type=jnp.float32)
        logits_all.append(p_hi[0:n_rt, :] + p_hi[n_rt:, :] + p_lo + brt_ref[...])

    for sb in range(tm // TM_PROJ):
        rs = slice(sb * TM_PROJ, (sb + 1) * TM_PROJ)
        logits = logits_all[sb]
        el_all = logits[0:N_EXPERTS, :]
        gl = logits[N_EXPERTS:N_EXPERTS + SUBLANES, :]
        grow = lax.broadcasted_iota(jnp.int32, gl.shape, 0).astype(jnp.float32)
        gl = jnp.where(grow < N_GROUPS, gl, NEG_BIG)
        gmax = jnp.max(gl, axis=0, keepdims=True)
        grp = jnp.min(jnp.where(gl == gmax, grow, float(N_GROUPS)), axis=0, keepdims=True)
        p_grp = 1.0 / jnp.sum(jnp.exp(gl - gmax), axis=0, keepdims=True)
        erow = lax.broadcasted_iota(jnp.int32, el_all.shape, 0).astype(jnp.float32)
        egrp = jnp.floor(erow * (1.0 / EXPERTS_PER_GROUP))
        el = jnp.where(egrp == grp, el_all, NEG_BIG)
        e1 = jnp.max(el, axis=0, keepdims=True)
        i1 = jnp.min(jnp.where(el == e1, erow, float(N_EXPERTS)), axis=0, keepdims=True)
        el2 = jnp.where(erow == i1, NEG_BIG, el)
        e2 = jnp.max(el2, axis=0, keepdims=True)
        i2 = jnp.min(jnp.where(el2 == e2, erow, float(N_EXPERTS)), axis=0, keepdims=True)
        z2 = jnp.exp(e2 - e1)
        w1 = p_grp / (1.0 + z2)
        w2 = p_grp * z2 / (1.0 + z2)
        sel1 = erow == i1
        sel2 = erow == i2
        onehot = jnp.logical_or(sel1, sel2)
        t_r = lax.broadcasted_iota(jnp.int32, (tr, tr), 0)
        t_c = lax.broadcasted_iota(jnp.int32, (tr, tr), 1)
        before = (t_r < t_c).astype(jnp.bfloat16)
        rank = jnp.dot(onehot.astype(jnp.bfloat16), before, preferred_element_type=jnp.float32)
        cnt = jnp.sum(onehot.astype(jnp.float32), axis=1, keepdims=True)
        cnt_al = jnp.floor((cnt + (SEG_ALIGN - 1)) * (1.0 / SEG_ALIGN)) * SEG_ALIGN
        e_r = lax.broadcasted_iota(jnp.int32, (N_EXPERTS, N_EXPERTS), 0)
        e_c = lax.broadcasted_iota(jnp.int32, (N_EXPERTS, N_EXPERTS), 1)
        lstart = jnp.dot((e_c < e_r).astype(jnp.float32), jnp.broadcast_to(cnt_al, (N_EXPERTS, LANES)),
                         precision=hi, preferred_element_type=jnp.float32)[:, 0:1]
        slot = lstart + rank
        pos1 = jnp.sum(jnp.where(sel1, slot, 0.0), axis=0, keepdims=True)
        pos2 = jnp.sum(jnp.where(sel2, slot, 0.0), axis=0, keepdims=True)
        r8 = lax.broadcasted_iota(jnp.int32, (SUBLANES, tr), 0)
        info = jnp.where(r8 == ROW_POS0, pos1, jnp.where(r8 == ROW_POS1, pos2, jnp.where(
            r8 == ROW_W0, w1, jnp.where(r8 == ROW_W1, w2, jnp.where(
                r8 == ROW_E0, i1, jnp.where(r8 == ROW_E1, i2, 0.0))))))
        row_ref[:, rs] = info

        def split3(w):
            h = _bf16(w).astype(jnp.float32)
            m = _bf16(w - h).astype(jnp.float32)
            return h, m, _bf16(w - h - m).astype(jnp.float32)

        w1h, w1m, w1l = split3(w1)
        w2h, w2m, w2l = split3(w2)
        parts = jnp.where(r8 == 0, w1h, jnp.where(r8 == 1, w1m, jnp.where(r8 == 2, w1l, jnp.where(
            r8 == 3, w2h, jnp.where(r8 == 4, w2m, jnp.where(r8 == 5, w2l, jnp.where(r8 == 6, i1, i2)))))))
        col_ref[rs, :] = jnp.concatenate(
            [info, parts, jnp.zeros((LANES - 2 * SUBLANES, tr), jnp.float32)], axis=0).T
        cnt_ref[sb] = jnp.broadcast_to(cnt, (N_EXPERTS, LANES)).astype(jnp.int32)


def _out_proj_router(x2, att, hm, wo, layer, g, wrt, brt):
    t = x2.shape[0]
    tm = TM_OUT
    row = lambda i: (i, 0)
    fix = lambda i: (0, 0)
    return pl.pallas_call(
        _out_proj_router_kernel,
        grid=(t // tm,),
        in_specs=[
            pl.BlockSpec((tm, D_MODEL), row),
            pl.BlockSpec((tm, ATT_WIDTH), row),
            pl.BlockSpec((tm, ML_WIDTH), row),
            pl.BlockSpec((None, D_MODEL, D_MODEL), lambda i: (layer, 0, 0), pipeline_mode=pl.Buffered(1)),
            pl.BlockSpec((1, D_MODEL), fix),
            pl.BlockSpec((8 * SUBLANES, D_MODEL), fix),
            pl.BlockSpec((4 * SUBLANES, 1), fix),
        ],
        out_specs=[
            pl.BlockSpec((tm, D_MODEL), row),
            pl.BlockSpec((tm, D_MODEL), row),
            pl.BlockSpec((SUBLANES, tm), lambda i: (0, i)),
            pl.BlockSpec((tm, LANES), row),
            pl.BlockSpec((tm // TM_PROJ, N_EXPERTS, LANES), lambda i: (i, 0, 0)),
        ],
        out_shape=(
            jax.ShapeDtypeStruct((t, D_MODEL), jnp.float32),
            jax.ShapeDtypeStruct((t, D_MODEL), jnp.bfloat16),
            jax.ShapeDtypeStruct((SUBLANES, t), jnp.float32),
            jax.ShapeDtypeStruct((t, LANES), jnp.float32),
            jax.ShapeDtypeStruct((t // TM_PROJ, N_EXPERTS, LANES), jnp.int32),
        ),
        scratch_shapes=[pltpu.VMEM((D_MODEL, D_MODEL), jnp.bfloat16)],
        compiler_params=pltpu.CompilerParams(dimension_semantics=("arbitrary",),
                                             vmem_limit_bytes=VMEM_LIMIT),
        name="out_proj_router",
    )(x2, att, hm, wo, g, wrt, brt)


def _dispatch_kernel(dst_ref, nz_ref, zdst_ref, t_ref, col_ref, row_ref, xs_hbm,
                     xbuf, zx, sem_x, sem_z):
    i = pl.program_id(0)
    n = pl.num_programs(0)
    p = i % 2
    tm = t_ref.shape[0]

    def copy(tile, par, j):
        d = pl.ds(pl.multiple_of(dst_ref[tile, j], SEG_ALIGN), SEG_ALIGN)
        return pltpu.make_async_copy(xbuf.at[par, j * SEG_ALIGN:(j + 1) * SEG_ALIGN], xs_hbm.at[d],
                                     sem_x.at[par])

    def zero_copy(e, j):
        d = pl.ds(pl.multiple_of(zdst_ref[e] + j * SEG_ALIGN, SEG_ALIGN), SEG_ALIGN)
        return pltpu.make_async_copy(zx, xs_hbm.at[d], sem_z.at[0])

    def for_zero_chunks(fn):
        for e in range(N_EXPERTS):
            def body(j, c, e=e):
                fn(zero_copy(e, j))
                return c
            lax.fori_loop(0, nz_ref[e], body, 0)

    def wait_tile(par):
        pltpu.make_async_copy(xbuf.at[par], xs_hbm.at[pl.ds(0, L_CAP)], sem_x.at[par]).wait()

    @pl.when(i == 0)
    def _():
        zx[...] = jnp.zeros(zx.shape, zx.dtype)
        for_zero_chunks(lambda cp: cp.start())

    @pl.when(i >= 2)
    def _():
        wait_tile(p)

    pos0 = row_ref[ROW_POS0:ROW_POS0 + 1, :].astype(jnp.int32)
    pos1 = row_ref[ROW_POS1:ROW_POS1 + 1, :].astype(jnp.int32)
    t = t_ref[...]
    side = _bf16(col_ref[...])
    rows = DISPATCH_ROWS
    for c in range(L_CAP // rows):
        r = lax.broadcasted_iota(jnp.int32, (rows, tm), 0) + c * rows
        perm = jnp.logical_or(r == pos0, r == pos1).astype(jnp.bfloat16)
        xbuf[p, c * rows:(c + 1) * rows, 0:D_MODEL] = _bf16(
            jnp.dot(perm, t, preferred_element_type=jnp.float32))
        xbuf[p, c * rows:(c + 1) * rows, D_MODEL:D_XS] = _bf16(
            jnp.dot(perm, side, preferred_element_type=jnp.float32))
        for j in range(c * rows // SEG_ALIGN, (c + 1) * rows // SEG_ALIGN):
            copy(i, p, j).start()

    @pl.when(i == n - 1)
    def _():
        @pl.when(n >= 2)
        def _():
            wait_tile(1 - p)
        wait_tile(p)
        for_zero_chunks(lambda cp: cp.wait())


def _dispatch(t, col, row, dst, nz, zdst):
    n_tok = t.shape[0]
    r_cap, _ = _moe_capacity(n_tok)
    tm = TM_PROJ
    grid_spec = pltpu.PrefetchScalarGridSpec(
        num_scalar_prefetch=3,
        grid=(n_tok // tm,),
        in_specs=[
            pl.BlockSpec((tm, D_MODEL), lambda i, *_: (i, 0)),
            pl.BlockSpec((tm, LANES), lambda i, *_: (i, 0)),
            pl.BlockSpec((SUBLANES, tm), lambda i, *_: (0, i)),
        ],
        out_specs=pl.BlockSpec(memory_space=pl.ANY),
        scratch_shapes=[
            pltpu.VMEM((2, L_CAP, D_XS), jnp.bfloat16),
            pltpu.VMEM((SEG_ALIGN, D_XS), jnp.bfloat16),
            pltpu.SemaphoreType.DMA((2,)),
            pltpu.SemaphoreType.DMA((1,)),
        ],
    )
    return pl.pallas_call(
        _dispatch_kernel,
        grid_spec=grid_spec,
        out_shape=jax.ShapeDtypeStruct((r_cap + DUMP_ROWS, D_XS), jnp.bfloat16),
        compiler_params=pltpu.CompilerParams(dimension_semantics=("arbitrary",),
                                             vmem_limit_bytes=VMEM_LIMIT),
        name="moe_dispatch",
    )(dst, nz, zdst, t, col, row)


def _moe_kernel(layer, blk_e_ref, nused_ref, nvalid_ref, next_e_ref, xs_ref, wg_hbm, wu_hbm, wd_hbm,
                ys_ref, wg_st, wu_st, wd_st, wg_bf, wu_bf, wd_bf, sem):
    b = pl.program_id(0)
    nv = nvalid_ref[b]
    e = blk_e_ref[b]
    new_expert = jnp.logical_or(b == 0, e != blk_e_ref[jnp.maximum(b - 1, 0)])

    def fetch(expert):
        return (pltpu.make_async_copy(wg_hbm.at[layer, expert], wg_st, sem.at[0]),
                pltpu.make_async_copy(wu_hbm.at[layer, expert], wu_st, sem.at[1]),
                pltpu.make_async_copy(wd_hbm.at[layer, expert], wd_st, sem.at[2]))

    @pl.when(b == 0)
    def _():
        for cp in fetch(e):
            cp.start()

    @pl.when(new_expert)
    def _():
        for cp in fetch(e):
            cp.wait()
        wg_bf[...] = _bf16(wg_st[...])
        wu_bf[...] = _bf16(wu_st[...])
        wd_bf[...] = _bf16(wd_st[...])

        @pl.when(next_e_ref[b] != e)
        def _():
            for cp in fetch(next_e_ref[b]):
                cp.start()

    def compute(rows):
        x = xs_ref[0:rows, 0:D_MODEL]
        y = None
        fw = D_FF_EXPERT // MOE_FF_SPLIT
        for c in range(MOE_FF_SPLIT):
            cols = slice(c * fw, (c + 1) * fw)
            a = jnp.dot(x, wg_bf[:, cols], preferred_element_type=jnp.float32)
            u = jnp.dot(x, wu_bf[:, cols], preferred_element_type=jnp.float32)
            h = a * jax.nn.sigmoid(a) * u
            yc = jnp.dot(_bf16(h), wd_bf[cols, :], preferred_element_type=jnp.float32)
            y = yc if y is None else y + yc
        sd = xs_ref[0:rows, D_MODEL:D_XS].astype(jnp.float32)
        e_blk = blk_e_ref[b].astype(jnp.float32)
        w0 = sd[:, COL_W0H:COL_W0H + 1] + sd[:, COL_W0H + 1:COL_W0H + 2] + sd[:, COL_W0H + 2:COL_W0H + 3]
        w1 = sd[:, COL_W1H:COL_W1H + 1] + sd[:, COL_W1H + 1:COL_W1H + 2] + sd[:, COL_W1H + 2:COL_W1H + 3]
        w = jnp.where(sd[:, COL_E0:COL_E0 + 1] == e_blk, w0, w1)
        ys_ref[0:rows, :] = _bf16(y * w)

    half = MOE_BM // 2

    @pl.when(nv > half)
    def _():
        compute(MOE_BM)

    @pl.when(jnp.logical_and(nv > 0, nv <= half))
    def _():
        compute(half)
        ys_ref[half:, :] = jnp.zeros((MOE_BM - half, D_MODEL), ys_ref.dtype)


def _moe(xs, blk_e, nused, nvalid, next_e, wg, wu, wd, layer):
    blk = lambda b, be, nu, nv, ne: (jnp.maximum(jnp.minimum(b, nu[0] - 1), 0), 0)
    r_cap = xs.shape[0] - DUMP_ROWS
    grid_spec = pltpu.PrefetchScalarGridSpec(
        num_scalar_prefetch=4,
        grid=(r_cap // MOE_BM,),
        in_specs=[
            pl.BlockSpec((MOE_BM, D_XS), blk),
            pl.BlockSpec(memory_space=pl.ANY),
            pl.BlockSpec(memory_space=pl.ANY),
            pl.BlockSpec(memory_space=pl.ANY),
        ],
        out_specs=pl.BlockSpec((MOE_BM, D_MODEL), blk),
        scratch_shapes=[
            pltpu.VMEM((D_MODEL, D_FF_EXPERT), jnp.float32),
            pltpu.VMEM((D_MODEL, D_FF_EXPERT), jnp.float32),
            pltpu.VMEM((D_FF_EXPERT, D_MODEL), jnp.float32),
            pltpu.VMEM((D_MODEL, D_FF_EXPERT), jnp.bfloat16),
            pltpu.VMEM((D_MODEL, D_FF_EXPERT), jnp.bfloat16),
            pltpu.VMEM((D_FF_EXPERT, D_MODEL), jnp.bfloat16),
            pltpu.SemaphoreType.DMA((3,)),
        ],
    )
    return pl.pallas_call(
        functools.partial(_moe_kernel, layer),
        grid_spec=grid_spec,
        out_shape=jax.ShapeDtypeStruct((r_cap, D_MODEL), jnp.bfloat16),
        compiler_params=pltpu.CompilerParams(dimension_semantics=("arbitrary",),
                                             vmem_limit_bytes=VMEM_LIMIT),
        name="moe_experts",
    )(blk_e, nused, nvalid, next_e, xs, wg, wu, wd)


def _combine_kernel(src_ref, x1_ref, col_ref, ys_hbm, out_ref, ybuf, sem):
    i = pl.program_id(0)
    n = pl.num_programs(0)
    p = i % COMBINE_BUFS
    tm = x1_ref.shape[0]

    def start_tile(tile, par):
        for j in range(N_CHUNK):
            s = pl.ds(pl.multiple_of(src_ref[tile, j], SEG_ALIGN), SEG_ALIGN)
            pltpu.make_async_copy(ys_hbm.at[s], ybuf.at[par, j * SEG_ALIGN:(j + 1) * SEG_ALIGN],
                                  sem.at[par]).start()

    def wait_buf(par):
        pltpu.make_async_copy(ys_hbm.at[pl.ds(0, L_CAP)], ybuf.at[par], sem.at[par]).wait()

    @pl.when(i == 0)
    def _():
        start_tile(0, 0)
        start_tile(jnp.minimum(1, n - 1), 1)

    wait_buf(p)
    start_tile(jnp.minimum(i + 2, n - 1), (i + 2) % COMBINE_BUFS)

    rows = COMBINE_ROWS
    for c in range(tm // rows):
        col = col_ref[c * rows:(c + 1) * rows, :]
        pos0 = col[:, ROW_POS0:ROW_POS0 + 1].astype(jnp.int32)
        pos1 = col[:, ROW_POS1:ROW_POS1 + 1].astype(jnp.int32)
        l = lax.broadcasted_iota(jnp.int32, (rows, L_CAP), 1)
        perm = jnp.logical_or(l == pos0, l == pos1).astype(jnp.bfloat16)
        y = jnp.dot(perm, ybuf[p], preferred_element_type=jnp.float32)
        out_ref[c * rows:(c + 1) * rows, :] = x1_ref[c * rows:(c + 1) * rows, :] + y

    @pl.when(i == n - 1)
    def _():
        wait_buf((i + 1) % COMBINE_BUFS)
        wait_buf((i + 2) % COMBINE_BUFS)


def _combine(x1, col, ys, src):
    n_tok = x1.shape[0]
    tm = TM_PROJ
    grid_spec = pltpu.PrefetchScalarGridSpec(
        num_scalar_prefetch=1,
        grid=(n_tok // tm,),
        in_specs=[
            pl.BlockSpec((tm, D_MODEL), lambda i, *_: (i, 0)),
            pl.BlockSpec((tm, LANES), lambda i, *_: (i, 0)),
            pl.BlockSpec(memory_space=pl.ANY),
        ],
        out_specs=pl.BlockSpec((tm, D_MODEL), lambda i, *_: (i, 0)),
        scratch_shapes=[
            pltpu.VMEM((COMBINE_BUFS, L_CAP, D_MODEL), jnp.bfloat16),
            pltpu.SemaphoreType.DMA((COMBINE_BUFS,)),
        ],
    )
    return pl.pallas_call(
        _combine_kernel,
        grid_spec=grid_spec,
        out_shape=jax.ShapeDtypeStruct((n_tok, D_MODEL), jnp.float32),
        compiler_params=pltpu.CompilerParams(dimension_semantics=("arbitrary",),
                                             vmem_limit_bytes=VMEM_LIMIT),
        name="moe_combine",
    )(src, x1, col, ys)


def _routing_tables(counts, n_blk_cap):
    r_cap = n_blk_cap * MOE_BM
    i32 = jnp.int32
    ca = ((counts + (SEG_ALIGN - 1)) // SEG_ALIGN) * SEG_ALIGN
    lend = jnp.cumsum(ca, axis=1)
    lstart = lend - ca
    tot = jnp.sum(ca, axis=0)
    region = ((tot + (MOE_BM - 1)) // MOE_BM) * MOE_BM
    rend = jnp.cumsum(region)
    base = rend - region
    gstart = base[None, :] + jnp.cumsum(ca, axis=0) - ca
    j16 = jnp.arange(N_CHUNK, dtype=i32) * SEG_ALIGN
    in_seg = jnp.logical_and(lstart[:, None, :] <= j16[None, :, None], j16[None, :, None] < lend[:, None, :])
    seg_row = (jnp.sum(jnp.where(in_seg, (gstart - lstart)[:, None, :], 0), axis=2) + j16[None, :]).astype(i32)
    used = j16[None, :] < lend[:, -1:]
    tile_par = (jnp.arange(counts.shape[0], dtype=i32) % 2)[:, None]
    park = r_cap + (tile_par * N_CHUNK + jnp.arange(N_CHUNK, dtype=i32)[None, :]) * SEG_ALIGN
    dst = jnp.where(used, seg_row, park).astype(i32)
    src = jnp.where(used, seg_row, 0).astype(i32)
    nused = (rend[-1] // MOE_BM).astype(i32)
    brow = jnp.minimum(jnp.arange(n_blk_cap, dtype=i32), nused - 1) * MOE_BM
    blk_e = jnp.minimum(jnp.sum(rend[None, :] <= brow[:, None], axis=1), N_EXPERTS - 1).astype(i32)
    zstart = (base + tot).astype(i32)
    nzc = ((region - tot) // SEG_ALIGN).astype(i32)
    blk_i = jnp.arange(n_blk_cap, dtype=i32)
    seg_end = jnp.sum(jnp.where(blk_e[:, None] == jnp.arange(N_EXPERTS)[None, :], (base + tot)[None, :], 0), axis=1)
    nvalid = jnp.where(blk_i < nused, jnp.clip(seg_end - blk_i * MOE_BM, 0, MOE_BM), 0).astype(i32)
    eid = jnp.arange(N_EXPERTS, dtype=i32)
    later = jnp.logical_and(eid[None, :] > eid[:, None], (region > 0)[None, :])
    nxt = jnp.min(jnp.where(later, eid[None, :], N_EXPERTS), axis=1)
    nxt = jnp.where(nxt < N_EXPERTS, nxt, eid)
    next_e = jnp.sum(jnp.where(blk_e[:, None] == eid[None, :], nxt[None, :], 0), axis=1).astype(i32)
    return dst, src, nzc, zstart, blk_e, nused[None], nvalid, next_e


def _t5_bucket_np(dist):
    max_exact = N_BUCKETS // 2
    d = np.maximum(dist, 1).astype(np.float32)
    large = max_exact + (np.log(d / max_exact) / math.log(MAX_DISTANCE / max_exact)
                         * (N_BUCKETS - max_exact)).astype(np.int32)
    large = np.minimum(large, N_BUCKETS - 1)
    return np.where(dist < max_exact, dist, large)


def _attention_bias(rel_bias):
    qi = np.arange(ATT_BLOCK)[:, None]
    kj = np.arange(2 * ATT_BLOCK)[None, :]
    dist = qi + ATT_BLOCK - kj
    in_window = (dist >= 0) & (dist < WINDOW)
    bucket = _t5_bucket_np(np.clip(dist, 0, WINDOW - 1))
    onehot = (bucket[None] == np.arange(N_BUCKETS)[:, None, None]).astype(np.float32)
    bias = jnp.einsum('nh,nqk->hqk', rel_bias.astype(jnp.float32), jnp.asarray(onehot),
                      precision=lax.Precision.HIGHEST)
    bias = jnp.where(jnp.asarray(in_window)[None], bias * LOG2E, NEG_BIG)
    no_prev = jnp.asarray(kj < ATT_BLOCK)[None]
    return jnp.stack([bias, jnp.where(no_prev, NEG_BIG, bias)])


def _block_diag_mean(width, block):
    idx = np.arange(width) // block
    return jnp.asarray((idx[:, None] == idx[None, :]).astype(np.float32) / block, dtype=jnp.bfloat16)


def kernel(x, rel_bias, norm_mix_g, w_in, q_norm_g, k_norm_g, attn_sink, conv_w, conv_b, gate_b,
           mlstm_norm_g, w_out, norm_ffn_g, w_router_group, b_router_group, w_router_expert,
           b_router_expert, w_gate, w_up, w_down):
    batch, seq_len, _ = x.shape
    n_tok = batch * seq_len
    assert seq_len % TM_OUT == 0 and seq_len % ML_KCHUNK == 0
    f32 = jnp.float32
    bias = _attention_bias(rel_bias)
    bdq = _block_diag_mean(ATT_WIDTH, ATT_HEAD_DIM)
    bdk = _block_diag_mean(LANES, ATT_HEAD_DIM)
    x2 = x.reshape(n_tok, D_MODEL)
    for l in range(DEPTH):
        wgt = _bf16(w_in[l, :, O_G:N_IN].T)
        qg = (jnp.tile(q_norm_g[l].astype(f32), ATT_HEADS) * (ATT_HEAD_DIM ** -0.5 * LOG2E))[None, :]
        kg = jnp.tile(k_norm_g[l].astype(f32), ATT_KV_HEADS)[None, :]
        gbt = gate_b[l].astype(f32)[:, None]
        qn, kn, vd, hm = _front(
            x2.reshape(batch, seq_len, D_MODEL), norm_mix_g[l][None, :], w_in, l, wgt, qg, kg, bdq, bdk,
            conv_w[l], conv_b[l][None, :], gbt, mlstm_norm_g[l][None, :])
        flat = lambda a: a.reshape(n_tok, a.shape[-1])
        att = _attention(flat(qn), flat(kn), flat(vd), bias, attn_sink[l].astype(f32) * LOG2E,
                         batch, seq_len)
        n_rt = 4 * SUBLANES
        wrt = jnp.pad(jnp.concatenate([w_router_expert[l], w_router_group[l]], axis=1).astype(f32).T,
                      ((0, n_rt - N_EXPERTS - N_GROUPS), (0, 0)))
        brt = jnp.pad(jnp.concatenate([b_router_expert[l], b_router_group[l]]).astype(f32),
                      (0, n_rt - N_EXPERTS - N_GROUPS))[:, None]
        wrt_hi = _bf16(wrt)
        wrt = jnp.concatenate([wrt_hi, _bf16(wrt - wrt_hi.astype(f32))], axis=0)
        x1, t, row, col, cnt = _out_proj_router(x2, att, hm.reshape(n_tok, ML_WIDTH), w_out, l,
                                                norm_ffn_g[l][None, :], wrt, brt)
        _, n_blk_cap = _moe_capacity(n_tok)
        dst, src, nz, zdst, blk_e, nused, nvalid, next_e = _routing_tables(cnt[:, :, 0], n_blk_cap)
        xs = _dispatch(t, col, row, dst, nz, zdst)
        ys = _moe(xs, blk_e, nused, nvalid, next_e, w_gate, w_up, w_down, l)
        x2 = _combine(x1, col, ys, src)
    return x2.reshape(batch, seq_len, D_MODEL)
```

```python
import functools
import math

import jax
import jax.numpy as jnp
import numpy as np
from jax import lax
from jax.experimental import pallas as pl
from jax.experimental.pallas import tpu as pltpu

D_MODEL = 1024
DEPTH = 2
ATT_HEADS = 8
ATT_KV_HEADS = 2
ATT_HEAD_DIM = 64
ATT_WIDTH = ATT_HEADS * ATT_HEAD_DIM
WINDOW = 128
ATT_BLOCK = 128
N_BUCKETS = 32
MAX_DISTANCE = 128
ML_HEADS = 4
ML_DQK = 64
ML_DV = 128
ML_WIDTH = ML_HEADS * ML_DV
ML_CHUNK = 64
CONV_K = 4
N_GROUPS = 4
EXPERTS_PER_GROUP = 4
N_EXPERTS = N_GROUPS * EXPERTS_PER_GROUP
D_FF_EXPERT = 512
EPS = 1e-6

LANES = 128
SUBLANES = 8
NEG_BIG = -1e30
LOG2E = math.log2(math.e)
VMEM_LIMIT = 48 * 1024 * 1024
FRONT_VMEM_LIMIT = 56 * 1024 * 1024

O_Q = 0
O_K = O_Q + ATT_WIDTH
O_V = O_K + ATT_KV_HEADS * ATT_HEAD_DIM
O_QM = O_V + ATT_KV_HEADS * ATT_HEAD_DIM
O_KM = O_QM + ML_HEADS * ML_DQK
O_VM = O_KM + ML_HEADS * ML_DQK
O_OM = O_VM + ML_WIDTH
O_G = O_OM + ML_WIDTH
N_IN = O_G + 2 * ML_HEADS

ATT_STEP_BLOCKS = 4
TM_OUT = 1024
TM_PROJ = 512
ML_KCHUNK = 256

SEG_ALIGN = 16
MOE_BM = 1024
MOE_TAIL = 256
MOE_FF_SPLIT = 2
L_CAP = 2 * TM_PROJ + N_EXPERTS * SEG_ALIGN
N_CHUNK = L_CAP // SEG_ALIGN
D_XS = D_MODEL + LANES
DUMP_ROWS = 2 * N_CHUNK * SEG_ALIGN
DISPATCH_ROWS = 256
COMBINE_ROWS = 256
COMBINE_BUFS = 3
ROW_POS0, ROW_POS1, ROW_W0, ROW_W1, ROW_E0, ROW_E1 = 0, 1, 2, 3, 4, 5
COL_W0H, COL_W1H, COL_E0, COL_E1 = 8, 11, 14, 15


def _moe_capacity(n_tok):
    n_tiles = n_tok // TM_PROJ
    rows = 2 * n_tok + n_tiles * N_EXPERTS * (SEG_ALIGN - 1) + N_EXPERTS * (MOE_BM - SEG_ALIGN)
    n_blk = -(-rows // MOE_BM)
    return n_blk * MOE_BM, n_blk


def _bf16(a):
    return a.astype(jnp.bfloat16)


def _split3(a):
    hi = _bf16(a)
    r1 = a - hi.astype(jnp.float32)
    mid = _bf16(r1)
    return hi, mid, _bf16(r1 - mid.astype(jnp.float32))


def _log_sigmoid(z):
    return jnp.minimum(z, 0.0) - jnp.log(1.0 + jnp.exp(-jnp.abs(z)))


def _front_kernel(x_ref, g_ref, w_ref, wgt_ref, qg_ref, kg_ref, bdq_ref, bdk_ref, cw_ref, cb_ref,
                  gbt_ref, ng_ref,
                  qn_ref, kn_ref, vd_ref, hm_ref,
                  w_bf, conv_scr, qkm_s, vm_s, om_s, gt_s, c_scr, m_scr):
    s = pl.program_id(0)
    batch = x_ref.shape[0]
    L = ML_KCHUNK
    wr = s % 2
    rd = 1 - wr

    n_chunks = pl.num_programs(0) - 1

    @pl.when(s == 0)
    def _():
        w_bf[...] = _bf16(w_ref[...])
        conv_scr[:, 0:SUBLANES, :] = jnp.zeros((batch, SUBLANES, conv_scr.shape[2]), jnp.float32)
        c_scr[...] = jnp.zeros(c_scr.shape, jnp.float32)
        m_scr[...] = jnp.zeros(m_scr.shape, jnp.float32)

    @pl.when(s > 0)
    def _():
        conv_scr[:, 0:SUBLANES, :] = conv_scr[:, L:L + SUBLANES, :]

    def scan_pieces():
        return _mlstm_pieces(qkm_s.at[rd], vm_s.at[rd], om_s.at[rd], gt_s.at[rd], ng_ref, hm_ref,
                             c_scr, m_scr, batch)

    def proj_pieces():
        return _project_pieces(x_ref, g_ref, w_bf, wgt_ref, qg_ref, kg_ref, bdq_ref, bdk_ref, cw_ref,
                               cb_ref, gbt_ref, qn_ref, kn_ref, vd_ref, qkm_s.at[wr], vm_s.at[wr],
                               om_s.at[wr], gt_s.at[wr], conv_scr)

    @pl.when(s == 0)
    def _():
        norm_part, proj_parts = proj_pieces()
        hbs = [norm_part(b) for b in range(batch)]
        for part in proj_parts:
            for b in range(batch):
                part(b, hbs[b])

    @pl.when(jnp.logical_and(s > 0, s < n_chunks))
    def _():
        scan_prep, scan_gates, scan_head, scan_finish = scan_pieces()
        norm_part, proj_parts = proj_pieces()
        preps = [scan_prep(b) for b in range(batch)]
        hbs = [norm_part(b) for b in range(batch)]
        gates = [scan_gates(b, 0, preps[b]) for b in range(batch)]
        for stage in range(len(proj_parts)):
            if stage < ML_HEADS:
                nxt = []
                for b in range(batch):
                    scan_head(b, stage, gates[b])
                    if stage + 1 < ML_HEADS:
                        nxt.append(scan_gates(b, stage + 1, preps[b]))
                gates = nxt
            for b in range(batch):
                proj_parts[stage](b, hbs[b])
        scan_finish()

    @pl.when(s == n_chunks)
    def _():
        scan_prep, scan_gates, scan_head, scan_finish = scan_pieces()
        preps = [scan_prep(b) for b in range(batch)]
        gates = [scan_gates(b, 0, preps[b]) for b in range(batch)]
        for h in range(ML_HEADS):
            nxt = []
            for b in range(batch):
                scan_head(b, h, gates[b])
                if h + 1 < ML_HEADS:
                    nxt.append(scan_gates(b, h + 1, preps[b]))
            gates = nxt
        scan_finish()


def _project_pieces(x_ref, g_ref, w_bf, wgt_ref, qg_ref, kg_ref, bdq_ref, bdk_ref, cw_ref, cb_ref,
                    gbt_ref, qn_ref, kn_ref, vd_ref, qkm_ref, vm_ref, om_ref, gatet_ref, conv_scr):
    L = ML_KCHUNK
    conv_w = 2 * ML_HEADS * ML_DQK

    def proj(hb, c0, width):
        return jnp.dot(hb, w_bf[:, c0:c0 + width], preferred_element_type=jnp.float32)

    def norm_part(b):
        x = x_ref[b]
        hn = x * lax.rsqrt(jnp.mean(x * x, axis=-1, keepdims=True) + EPS) * g_ref[...]
        return _bf16(hn)

    def q_part(b, hb):
        q = proj(hb, O_Q, ATT_WIDTH)
        q_ms = jnp.dot(_bf16(q * q), bdq_ref[...], preferred_element_type=jnp.float32)
        qn_ref[b] = _bf16(q * lax.rsqrt(q_ms + EPS) * qg_ref[...])

    def kv_part(b, hb):
        kv = proj(hb, O_K, 2 * LANES)
        k = kv[:, 0:LANES]
        v = kv[:, LANES:2 * LANES]
        k_ms = jnp.dot(_bf16(k * k), bdk_ref[...], preferred_element_type=jnp.float32)
        kn = k * lax.rsqrt(k_ms + EPS) * kg_ref[...]
        low = lax.broadcasted_iota(jnp.int32, kn.shape, 1) < ATT_HEAD_DIM

        def dup_heads(a):
            swapped = pltpu.roll(a, ATT_HEAD_DIM, axis=1)
            return jnp.concatenate([jnp.where(low, a, swapped), jnp.where(low, swapped, a)], axis=1)

        kn_ref[b] = _bf16(dup_heads(kn))
        vd_ref[b] = _bf16(dup_heads(v))

    def conv_part(b, hb):
        qk = proj(hb, O_QM, conv_w)
        conv_scr[b, SUBLANES:SUBLANES + L, :] = qk
        y = qk * cw_ref[CONV_K - 1:CONV_K, :] + cb_ref[...]
        for j in range(CONV_K - 1):
            off = SUBLANES - (CONV_K - 1) + j
            y = y + conv_scr[b, off:off + L, :] * cw_ref[j:j + 1, :]
        y = y * jax.nn.sigmoid(y)
        lane = lax.broadcasted_iota(jnp.int32, y.shape, 1)
        y = jnp.where(lane >= ML_HEADS * ML_DQK, y * (ML_DQK ** -0.5), y)
        qkm_ref[b] = _bf16(y)

    def vm_part(b, hb):
        vm_ref[b] = _bf16(proj(hb, O_VM, ML_WIDTH))

    def om_part(b, hb):
        om_ref[b] = _bf16(jax.nn.sigmoid(proj(hb, O_OM, ML_WIDTH)))

        gt = lax.dot_general(wgt_ref[...], hb, (((1,), (1,)), ((), ())),
                             preferred_element_type=jnp.float32) + gbt_ref[...]
        grow = lax.broadcasted_iota(jnp.int32, gt.shape, 0)
        gatet_ref[b] = jnp.where(grow >= ML_HEADS, _log_sigmoid(gt), gt) * LOG2E

    return norm_part, (q_part, kv_part, conv_part, vm_part, om_part)


def _front(x3, g, w_in, layer, wgt, qg, kg, bdq, bdk, cw, cb, gbt, ng):
    batch, seq_len, _ = x3.shape
    L = ML_KCHUNK
    nc = seq_len // L
    fix = lambda s: (0, 0)
    cur = lambda s: (0, jnp.minimum(s, nc - 1), 0)
    prev = lambda s: (0, jnp.maximum(s - 1, 0), 0)
    kv_w = 2 * ATT_KV_HEADS * ATT_HEAD_DIM
    conv_w = 2 * ML_HEADS * ML_DQK
    return pl.pallas_call(
        _front_kernel,
        grid=(nc + 1,),
        in_specs=[
            pl.BlockSpec((batch, L, D_MODEL), cur),
            pl.BlockSpec((1, D_MODEL), fix),
            pl.BlockSpec((None, D_MODEL, N_IN), lambda s: (layer, 0, 0), pipeline_mode=pl.Buffered(1)),
            pl.BlockSpec((SUBLANES, D_MODEL), fix),
            pl.BlockSpec((1, ATT_WIDTH), fix),
            pl.BlockSpec((1, LANES), fix),
            pl.BlockSpec((ATT_WIDTH, ATT_WIDTH), fix),
            pl.BlockSpec((LANES, LANES), fix),
            pl.BlockSpec((CONV_K, conv_w), fix),
            pl.BlockSpec((1, conv_w), fix),
            pl.BlockSpec((SUBLANES, 1), fix),
            pl.BlockSpec((1, ML_WIDTH), fix),
        ],
        out_specs=[
            pl.BlockSpec((batch, L, ATT_WIDTH), cur),
            pl.BlockSpec((batch, L, kv_w), cur),
            pl.BlockSpec((batch, L, kv_w), cur),
            pl.BlockSpec((batch, L, ML_WIDTH), prev),
        ],
        out_shape=(
            jax.ShapeDtypeStruct((batch, seq_len, ATT_WIDTH), jnp.bfloat16),
            jax.ShapeDtypeStruct((batch, seq_len, kv_w), jnp.bfloat16),
            jax.ShapeDtypeStruct((batch, seq_len, kv_w), jnp.bfloat16),
            jax.ShapeDtypeStruct((batch, seq_len, ML_WIDTH), jnp.bfloat16),
        ),
        scratch_shapes=[
            pltpu.VMEM((D_MODEL, N_IN), jnp.bfloat16),
            pltpu.VMEM((batch, L + 2 * SUBLANES, conv_w), jnp.float32),
            pltpu.VMEM((2, batch, L, conv_w), jnp.bfloat16),
            pltpu.VMEM((2, batch, L, ML_WIDTH), jnp.bfloat16),
            pltpu.VMEM((2, batch, L, ML_WIDTH), jnp.bfloat16),
            pltpu.VMEM((2, batch, SUBLANES, L), jnp.float32),
            pltpu.VMEM((batch * ML_HEADS // 2, 2 * ML_DQK, 2 * ML_DV), jnp.float32),
            pltpu.VMEM((batch * ML_HEADS, 1, LANES), jnp.float32),
        ],
        compiler_params=pltpu.CompilerParams(dimension_semantics=("arbitrary",),
                                             vmem_limit_bytes=FRONT_VMEM_LIMIT),
        name="proj_mlstm",
    )(x3, g, w_in, wgt, qg, kg, bdq, bdk, cw, cb, gbt, ng)


def _attn_kernel(sink_ref, q_ref, kp_ref, kc_ref, vp_ref, vc_ref, bias_ref, o_ref):
    i = pl.program_id(1)
    blk = ATT_BLOCK
    lane = lax.broadcasted_iota(jnp.int32, (blk, LANES), 1)
    low = lane < ATT_HEAD_DIM
    group = ATT_HEADS // ATT_KV_HEADS
    ones = jnp.ones((2 * blk, LANES), jnp.bfloat16)
    for j in range(ATT_STEP_BLOCKS):
        rows = slice(j * blk, (j + 1) * blk)
        first = (i == 0).astype(jnp.int32) if j == 0 else 0
        for kv in range(ATT_KV_HEADS):
            lanes = slice(kv * LANES, (kv + 1) * LANES)
            if j == 0:
                kcat = jnp.concatenate([kp_ref[:, lanes], kc_ref[0:blk, lanes]], axis=0)
                vrows = jnp.concatenate([vp_ref[:, lanes], vc_ref[0:blk, lanes]], axis=0)
            else:
                kcat = kc_ref[(j - 1) * blk:(j + 1) * blk, lanes]
                vrows = vc_ref[(j - 1) * blk:(j + 1) * blk, lanes]
            vcat = jnp.concatenate([vrows, ones], axis=1)
            heads = range(kv * group, (kv + 1) * group)
            qs = []
            for h in heads:
                qp = q_ref[rows, (h // 2) * LANES:(h // 2 + 1) * LANES]
                qs.append(jnp.where(low if h % 2 == 0 else jnp.logical_not(low), qp, jnp.zeros_like(qp)))
            s_all = lax.dot_general(jnp.concatenate(qs, axis=0), kcat, (((1,), (1,)), ((), ())),
                                    preferred_element_type=jnp.float32)
            ps, ms = [], []
            for n, h in enumerate(heads):
                logits = s_all[n * blk:(n + 1) * blk, :] + bias_ref[first, h]
                row_max = jnp.broadcast_to(jnp.max(logits, axis=-1, keepdims=True), (blk, LANES))
                m = jnp.maximum(row_max, sink_ref[h])
                ps.append(_bf16(jnp.exp2(logits - jnp.concatenate([m, m], axis=1))))
                ms.append(m)
            o_all = jnp.dot(jnp.concatenate(ps, axis=0), vcat, preferred_element_type=jnp.float32)
            outs = []
            for n, h in enumerate(heads):
                o = o_all[n * blk:(n + 1) * blk, :]
                den = o[:, LANES:2 * LANES] + jnp.exp2(sink_ref[h] - ms[n])
                outs.append(o[:, 0:LANES] / den)
            for n in range(0, group, 2):
                pair = (kv * group + n) // 2
                o_ref[rows, pair * LANES:(pair + 1) * LANES] = _bf16(
                    jnp.where(low, outs[n], outs[n + 1]))


def _attention(qn, kn, vd, bias, sink, batch, seq_len):
    step = ATT_STEP_BLOCKS * ATT_BLOCK
    ns = seq_len // step
    cur = lambda b, i, s: (b * ns + i, 0)
    prev = lambda b, i, s: ((b * ns + i) * ATT_STEP_BLOCKS - jnp.minimum(i, 1), 0)
    grid_spec = pltpu.PrefetchScalarGridSpec(
        num_scalar_prefetch=1,
        grid=(batch, ns),
        in_specs=[
            pl.BlockSpec((step, ATT_WIDTH), cur),
            pl.BlockSpec((ATT_BLOCK, 2 * LANES), prev),
            pl.BlockSpec((step, 2 * LANES), cur),
            pl.BlockSpec((ATT_BLOCK, 2 * LANES), prev),
            pl.BlockSpec((step, 2 * LANES), cur),
            pl.BlockSpec((2, ATT_HEADS, ATT_BLOCK, 2 * ATT_BLOCK), lambda b, i, s: (0, 0, 0, 0)),
        ],
        out_specs=pl.BlockSpec((step, ATT_WIDTH), cur),
    )
    return pl.pallas_call(
        _attn_kernel,
        grid_spec=grid_spec,
        out_shape=jax.ShapeDtypeStruct((batch * seq_len, ATT_WIDTH), jnp.bfloat16),
        compiler_params=pltpu.CompilerParams(dimension_semantics=("arbitrary", "arbitrary"),
                                             vmem_limit_bytes=VMEM_LIMIT),
        name="swa_attention",
    )(sink, qn, kn, kn, vd, vd, bias)


def _mlstm_pieces(qk_ref, v_ref, o_ref, gt_ref, ng_ref, out_ref, c_scr, m_scr, batch):
    L = ML_KCHUNK
    pairs = ML_HEADS // 2
    r_i = lax.broadcasted_iota(jnp.int32, (L, L), 0)
    c_i = lax.broadcasted_iota(jnp.int32, (L, L), 1)
    causal = c_i <= r_i
    tril_bf = causal.astype(jnp.bfloat16)
    triu_bf = (r_i <= c_i).astype(jnp.bfloat16)
    mean_dv = jnp.full((ML_DV, ML_DV), 1.0 / ML_DV, jnp.bfloat16)
    lane = lax.broadcasted_iota(jnp.int32, (L, LANES), 1)
    low = lane < ML_DQK
    ones_dv = jnp.ones((L, ML_DV), jnp.bfloat16)
    row2 = lax.broadcasted_iota(jnp.int32, (2 * ML_DQK, 2 * ML_DV), 0)

    def twice(a):
        return jnp.concatenate([a, a], axis=1)

    m_state = [m_scr[k] for k in range(batch * ML_HEADS)]
    c_state = [c_scr[k] for k in range(batch * pairs)]

    def batch_prep(b):
        gt = gt_ref[b]
        g = jnp.concatenate([gt, jnp.zeros((LANES - SUBLANES, L), jnp.float32)], axis=0).T
        bcols = sum(jnp.dot(tril_bf, part, preferred_element_type=jnp.float32) for part in _split3(g))
        brows = sum(jnp.dot(part, triu_bf, preferred_element_type=jnp.float32) for part in _split3(gt))
        return gt, g, bcols, brows

    carry = {}

    def gate_piece(b, h, prep):
        gt, g, bcols, brows = prep
        m_prev = m_state[b * ML_HEADS + h]
        bc = jnp.broadcast_to(bcols[:, ML_HEADS + h:ML_HEADS + h + 1], (L, LANES))
        lic = jnp.broadcast_to(g[:, h:h + 1], (L, LANES))
        br = brows[ML_HEADS + h:ML_HEADS + h + 1, :]
        lir = gt[h:h + 1, :]
        log_d = jnp.where(causal, twice(bc) - (br - lir), NEG_BIG)
        m_inter = bc + m_prev
        row_max = jnp.broadcast_to(jnp.max(log_d, axis=-1, keepdims=True), (L, LANES))
        m_row = jnp.maximum(m_inter, row_max)
        d = jnp.exp2(log_d - twice(m_row))
        inter = jnp.exp2(m_inter - m_row)
        b_last = bc[L - 1:L, :]
        log_w = b_last - bc + lic
        m_next = jnp.maximum(b_last + m_prev, jnp.max(log_w, axis=0, keepdims=True))
        w = jnp.exp2(log_w - m_next)
        decay = jnp.exp2(b_last + m_prev - m_next)
        m_state[b * ML_HEADS + h] = m_next
        return d, inter, jnp.exp2(-m_row), w, decay

    def head_piece(b, h, gates):
        d, inter, floor, w, decay = gates
        pair, sub = divmod(h, 2)
        sidx = b * pairs + pair
        qp = qk_ref[b, :, pair * LANES:(pair + 1) * LANES]
        kp = qk_ref[b, :, (pairs + pair) * LANES:(pairs + pair + 1) * LANES]
        c_pair = c_state[sidx]
        c_bf = _bf16(c_pair)
        sel = low if sub == 0 else jnp.logical_not(low)
        qm = jnp.where(sel, qp, jnp.zeros_like(qp))
        s = lax.dot_general(qm, kp, (((1,), (1,)), ((), ())),
                            preferred_element_type=jnp.float32) * d
        v_ext = jnp.concatenate([v_ref[b, :, h * ML_DV:(h + 1) * ML_DV], ones_dv],
                                axis=-1)
        num = twice(inter) * jnp.dot(qm, c_bf, preferred_element_type=jnp.float32) \
            + jnp.dot(_bf16(s), v_ext, preferred_element_type=jnp.float32)
        den = num[:, ML_DV:2 * ML_DV]
        hval = num[:, 0:ML_DV] / jnp.maximum(jnp.abs(den), floor)
        h_ms = jnp.dot(_bf16(hval * hval), mean_dv, preferred_element_type=jnp.float32)
        hn = hval * lax.rsqrt(h_ms + EPS)
        hn = hn * ng_ref[:, h * ML_DV:(h + 1) * ML_DV]
        out_ref[b, :, h * ML_DV:(h + 1) * ML_DV] = _bf16(
            hn * o_ref[b, :, h * ML_DV:(h + 1) * ML_DV].astype(jnp.float32))
        kw = _bf16(jnp.where(sel, kp.astype(jnp.float32) * w, 0.0))
        upd = lax.dot_general(kw, v_ext, (((0,), (0,)), ((), ())),
                              preferred_element_type=jnp.float32)
        if sub == 0:
            carry[sidx] = (decay, upd)
        else:
            decay0, upd0 = carry.pop(sidx)
            decay_rows = jnp.where(row2 < ML_DQK, twice(decay0), twice(decay))
            c_state[sidx] = decay_rows * c_pair + (upd0 + upd)

    def finish():
        for k in range(batch * ML_HEADS):
            m_scr[k] = m_state[k]
        for k in range(batch * pairs):
            c_scr[k] = c_state[k]

    return batch_prep, gate_piece, head_piece, finish


def _out_proj_router_kernel(x_ref, att_ref, hm_ref, wo_ref, g_ref, wrt_ref, brt_ref,
                            x1_ref, t_ref, row_ref, col_ref, cnt_ref, wo_bf):
    tm = x_ref.shape[0]
    tr = TM_PROJ
    hi = lax.Precision.HIGHEST

    @pl.when(pl.program_id(0) == 0)
    def _():
        wo_bf[...] = _bf16(wo_ref[...])

    logits_all = []
    x1_all = []
    for sb in range(tm // TM_PROJ):
        rs = slice(sb * TM_PROJ, (sb + 1) * TM_PROJ)
        mixed = jnp.concatenate([att_ref[rs, :], hm_ref[rs, :]], axis=1)
        x1_all.append(x_ref[rs, :] + jnp.dot(mixed, wo_bf[...], preferred_element_type=jnp.float32))
    for sb in range(tm // TM_PROJ):
        rs = slice(sb * TM_PROJ, (sb + 1) * TM_PROJ)
        x1 = x1_all[sb]
        x1_ref[rs, :] = x1
        tn = x1 * lax.rsqrt(jnp.mean(x1 * x1, axis=-1, keepdims=True) + EPS) * g_ref[...]
        tn_hi = _bf16(tn)
        t_ref[rs, :] = tn_hi
        tn_lo = _bf16(tn - tn_hi.astype(jnp.float32))
        nt = (((1,), (1,)), ((), ()))
        n_rt = wrt_ref.shape[0] // 2
        p_hi = lax.dot_general(wrt_ref[...], tn_hi, nt, preferred_element_type=jnp.float32)
        p_lo = lax.dot_general(wrt_ref[0:n_rt, :], tn_lo, nt, preferred_element_type=jnp.float32)
        logits_all.append(p_hi[0:n_rt, :] + p_hi[n_rt:, :] + p_lo + brt_ref[...])

    for sb in range(tm // TM_PROJ):
        rs = slice(sb * TM_PROJ, (sb + 1) * TM_PROJ)
        logits = logits_all[sb]
        el_all = logits[0:N_EXPERTS, :]
        gl = logits[N_EXPERTS:N_EXPERTS + SUBLANES, :]
        grow = lax.broadcasted_iota(jnp.int32, gl.shape, 0).astype(jnp.float32)
        gl = jnp.where(grow < N_GROUPS, gl, NEG_BIG)
        gmax = jnp.max(gl, axis=0, keepdims=True)
        grp = jnp.min(jnp.where(gl == gmax, grow, float(N_GROUPS)), axis=0, keepdims=True)
        p_grp = 1.0 / jnp.sum(jnp.exp(gl - gmax), axis=0, keepdims=True)
        erow = lax.broadcasted_iota(jnp.int32, el_all.shape, 0).astype(jnp.float32)
        egrp = jnp.floor(erow * (1.0 / EXPERTS_PER_GROUP))
        el = jnp.where(egrp == grp, el_all, NEG_BIG)
        e1 = jnp.max(el, axis=0, keepdims=True)
        i1 = jnp.min(jnp.where(el == e1, erow, float(N_EXPERTS)), axis=0, keepdims=True)
        el2 = jnp.where(erow == i1, NEG_BIG, el)
        e2 = jnp.max(el2, axis=0, keepdims=True)
        i2 = jnp.min(jnp.where(el2 == e2, erow, float(N_EXPERTS)), axis=0, keepdims=True)
        z2 = jnp.exp(e2 - e1)
        w1 = p_grp / (1.0 + z2)
        w2 = p_grp * z2 / (1.0 + z2)
        sel1 = erow == i1
        sel2 = erow == i2
        onehot = jnp.logical_or(sel1, sel2)
        t_r = lax.broadcasted_iota(jnp.int32, (tr, tr), 0)
        t_c = lax.broadcasted_iota(jnp.int32, (tr, tr), 1)
        before = (t_r < t_c).astype(jnp.bfloat16)
        rank = jnp.dot(onehot.astype(jnp.bfloat16), before, preferred_element_type=jnp.float32)
        cnt = jnp.sum(onehot.astype(jnp.float32), axis=1, keepdims=True)
        cnt_al = jnp.floor((cnt + (SEG_ALIGN - 1)) * (1.0 / SEG_ALIGN)) * SEG_ALIGN
        e_r = lax.broadcasted_iota(jnp.int32, (N_EXPERTS, N_EXPERTS), 0)
        e_c = lax.broadcasted_iota(jnp.int32, (N_EXPERTS, N_EXPERTS), 1)
        lstart = jnp.dot((e_c < e_r).astype(jnp.float32), jnp.broadcast_to(cnt_al, (N_EXPERTS, LANES)),
                         precision=hi, preferred_element_type=jnp.float32)[:, 0:1]
        slot = lstart + rank
        pos1 = jnp.sum(jnp.where(sel1, slot, 0.0), axis=0, keepdims=True)
        pos2 = jnp.sum(jnp.where(sel2, slot, 0.0), axis=0, keepdims=True)
        r8 = lax.broadcasted_iota(jnp.int32, (SUBLANES, tr), 0)
        info = jnp.where(r8 == ROW_POS0, pos1, jnp.where(r8 == ROW_POS1, pos2, jnp.where(
            r8 == ROW_W0, w1, jnp.where(r8 == ROW_W1, w2, jnp.where(
                r8 == ROW_E0, i1, jnp.where(r8 == ROW_E1, i2, 0.0))))))
        row_ref[:, rs] = info

        def split3(w):
            h = _bf16(w).astype(jnp.float32)
            m = _bf16(w - h).astype(jnp.float32)
            return h, m, _bf16(w - h - m).astype(jnp.float32)

        w1h, w1m, w1l = split3(w1)
        w2h, w2m, w2l = split3(w2)
        parts = jnp.where(r8 == 0, w1h, jnp.where(r8 == 1, w1m, jnp.where(r8 == 2, w1l, jnp.where(
            r8 == 3, w2h, jnp.where(r8 == 4, w2m, jnp.where(r8 == 5, w2l, jnp.where(r8 == 6, i1, i2)))))))
        col_ref[rs, :] = jnp.concatenate(
            [info, parts, jnp.zeros((LANES - 2 * SUBLANES, tr), jnp.float32)], axis=0).T
        cnt_ref[sb] = jnp.broadcast_to(cnt, (N_EXPERTS, LANES)).astype(jnp.int32)


def _out_proj_router(x2, att, hm, wo, layer, g, wrt, brt):
    t = x2.shape[0]
    tm = TM_OUT
    row = lambda i: (i, 0)
    fix = lambda i: (0, 0)
    return pl.pallas_call(
        _out_proj_router_kernel,
        grid=(t // tm,),
        in_specs=[
            pl.BlockSpec((tm, D_MODEL), row),
            pl.BlockSpec((tm, ATT_WIDTH), row),
            pl.BlockSpec((tm, ML_WIDTH), row),
            pl.BlockSpec((None, D_MODEL, D_MODEL), lambda i: (layer, 0, 0), pipeline_mode=pl.Buffered(1)),
            pl.BlockSpec((1, D_MODEL), fix),
            pl.BlockSpec((8 * SUBLANES, D_MODEL), fix),
            pl.BlockSpec((4 * SUBLANES, 1), fix),
        ],
        out_specs=[
            pl.BlockSpec((tm, D_MODEL), row),
            pl.BlockSpec((tm, D_MODEL), row),
            pl.BlockSpec((SUBLANES, tm), lambda i: (0, i)),
            pl.BlockSpec((tm, LANES), row),
            pl.BlockSpec((tm // TM_PROJ, N_EXPERTS, LANES), lambda i: (i, 0, 0)),
        ],
        out_shape=(
            jax.ShapeDtypeStruct((t, D_MODEL), jnp.float32),
            jax.ShapeDtypeStruct((t, D_MODEL), jnp.bfloat16),
            jax.ShapeDtypeStruct((SUBLANES, t), jnp.float32),
            jax.ShapeDtypeStruct((t, LANES), jnp.float32),
            jax.ShapeDtypeStruct((t // TM_PROJ, N_EXPERTS, LANES), jnp.int32),
        ),
        scratch_shapes=[pltpu.VMEM((D_MODEL, D_MODEL), jnp.bfloat16)],
        compiler_params=pltpu.CompilerParams(dimension_semantics=("arbitrary",),
                                             vmem_limit_bytes=VMEM_LIMIT),
        name="out_proj_router",
    )(x2, att, hm, wo, g, wrt, brt)


def _dispatch_kernel(dst_ref, nz_ref, zdst_ref, t_ref, col_ref, row_ref, xs_hbm,
                     xbuf, zx, sem_x, sem_z):
    i = pl.program_id(0)
    n = pl.num_programs(0)
    p = i % 2
    tm = t_ref.shape[0]

    def copy(tile, par, j):
        d = pl.ds(pl.multiple_of(dst_ref[tile, j], SEG_ALIGN), SEG_ALIGN)
        return pltpu.make_async_copy(xbuf.at[par, j * SEG_ALIGN:(j + 1) * SEG_ALIGN], xs_hbm.at[d],
                                     sem_x.at[par])

    def zero_copy(e, j):
        d = pl.ds(pl.multiple_of(zdst_ref[e] + j * SEG_ALIGN, SEG_ALIGN), SEG_ALIGN)
        return pltpu.make_async_copy(zx, xs_hbm.at[d], sem_z.at[0])

    def for_zero_chunks(fn):
        for e in range(N_EXPERTS):
            def body(j, c, e=e):
                fn(zero_copy(e, j))
                return c
            lax.fori_loop(0, nz_ref[e], body, 0)

    def wait_tile(par):
        pltpu.make_async_copy(xbuf.at[par], xs_hbm.at[pl.ds(0, L_CAP)], sem_x.at[par]).wait()

    @pl.when(i == 0)
    def _():
        zx[...] = jnp.zeros(zx.shape, zx.dtype)
        for_zero_chunks(lambda cp: cp.start())

    @pl.when(i >= 2)
    def _():
        wait_tile(p)

    pos0 = row_ref[ROW_POS0:ROW_POS0 + 1, :].astype(jnp.int32)
    pos1 = row_ref[ROW_POS1:ROW_POS1 + 1, :].astype(jnp.int32)
    t = t_ref[...]
    side = _bf16(col_ref[...])
    rows = DISPATCH_ROWS
    for c in range(L_CAP // rows):
        r = lax.broadcasted_iota(jnp.int32, (rows, tm), 0) + c * rows
        perm = jnp.logical_or(r == pos0, r == pos1).astype(jnp.bfloat16)
        xbuf[p, c * rows:(c + 1) * rows, 0:D_MODEL] = _bf16(
            jnp.dot(perm, t, preferred_element_type=jnp.float32))
        xbuf[p, c * rows:(c + 1) * rows, D_MODEL:D_XS] = _bf16(
            jnp.dot(perm, side, preferred_element_type=jnp.float32))
        for j in range(c * rows // SEG_ALIGN, (c + 1) * rows // SEG_ALIGN):
            copy(i, p, j).start()

    @pl.when(i == n - 1)
    def _():
        @pl.when(n >= 2)
        def _():
            wait_tile(1 - p)
        wait_tile(p)
        for_zero_chunks(lambda cp: cp.wait())


def _dispatch(t, col, row, dst, nz, zdst):
    n_tok = t.shape[0]
    r_cap, _ = _moe_capacity(n_tok)
    tm = TM_PROJ
    grid_spec = pltpu.PrefetchScalarGridSpec(
        num_scalar_prefetch=3,
        grid=(n_tok // tm,),
        in_specs=[
            pl.BlockSpec((tm, D_MODEL), lambda i, *_: (i, 0)),
            pl.BlockSpec((tm, LANES), lambda i, *_: (i, 0)),
            pl.BlockSpec((SUBLANES, tm), lambda i, *_: (0, i)),
        ],
        out_specs=pl.BlockSpec(memory_space=pl.ANY),
        scratch_shapes=[
            pltpu.VMEM((2, L_CAP, D_XS), jnp.bfloat16),
            pltpu.VMEM((SEG_ALIGN, D_XS), jnp.bfloat16),
            pltpu.SemaphoreType.DMA((2,)),
            pltpu.SemaphoreType.DMA((1,)),
        ],
    )
    return pl.pallas_call(
        _dispatch_kernel,
        grid_spec=grid_spec,
        out_shape=jax.ShapeDtypeStruct((r_cap + DUMP_ROWS, D_XS), jnp.bfloat16),
        compiler_params=pltpu.CompilerParams(dimension_semantics=("arbitrary",),
                                             vmem_limit_bytes=VMEM_LIMIT),
        name="moe_dispatch",
    )(dst, nz, zdst, t, col, row)


def _moe_kernel(layer, blk_e_ref, nused_ref, nvalid_ref, next_e_ref, xs_ref, wg_hbm, wu_hbm, wd_hbm,
                ys_ref, wg_st, wu_st, wd_st, wg_bf, wu_bf, wd_bf, sem):
    b = pl.program_id(0)
    nv = nvalid_ref[b]
    e = blk_e_ref[b]
    new_expert = jnp.logical_or(b == 0, e != blk_e_ref[jnp.maximum(b - 1, 0)])

    def fetch(expert):
        return (pltpu.make_async_copy(wg_hbm.at[layer, expert], wg_st, sem.at[0]),
                pltpu.make_async_copy(wu_hbm.at[layer, expert], wu_st, sem.at[1]),
                pltpu.make_async_copy(wd_hbm.at[layer, expert], wd_st, sem.at[2]))

    @pl.when(b == 0)
    def _():
        for cp in fetch(e):
            cp.start()

    @pl.when(new_expert)
    def _():
        for cp in fetch(e):
            cp.wait()
        wg_bf[...] = _bf16(wg_st[...])
        wu_bf[...] = _bf16(wu_st[...])
        wd_bf[...] = _bf16(wd_st[...])

        @pl.when(next_e_ref[b] != e)
        def _():
            for cp in fetch(next_e_ref[b]):
                cp.start()

    def compute(rows):
        x = xs_ref[0:rows, 0:D_MODEL]
        y = None
        fw = D_FF_EXPERT // MOE_FF_SPLIT
        for c in range(MOE_FF_SPLIT):
            cols = slice(c * fw, (c + 1) * fw)
            a = jnp.dot(x, wg_bf[:, cols], preferred_element_type=jnp.float32)
            u = jnp.dot(x, wu_bf[:, cols], preferred_element_type=jnp.float32)
            h = a * jax.nn.sigmoid(a) * u
            yc = jnp.dot(_bf16(h), wd_bf[cols, :], preferred_element_type=jnp.float32)
            y = yc if y is None else y + yc
        sd = xs_ref[0:rows, D_MODEL:D_XS].astype(jnp.float32)
        e_blk = blk_e_ref[b].astype(jnp.float32)
        w0 = sd[:, COL_W0H:COL_W0H + 1] + sd[:, COL_W0H + 1:COL_W0H + 2] + sd[:, COL_W0H + 2:COL_W0H + 3]
        w1 = sd[:, COL_W1H:COL_W1H + 1] + sd[:, COL_W1H + 1:COL_W1H + 2] + sd[:, COL_W1H + 2:COL_W1H + 3]
        w = jnp.where(sd[:, COL_E0:COL_E0 + 1] == e_blk, w0, w1)
        ys_ref[0:rows, :] = _bf16(y * w)

    for k in range(1, MOE_BM // MOE_TAIL + 1):
        rows = k * MOE_TAIL

        @pl.when(jnp.logical_and(nv > rows - MOE_TAIL, nv <= rows))
        def _(rows=rows):
            compute(rows)
            if rows < MOE_BM:
                ys_ref[rows:, :] = jnp.zeros((MOE_BM - rows, D_MODEL), ys_ref.dtype)


def _moe(xs, blk_e, nused, nvalid, next_e, wg, wu, wd, layer):
    blk = lambda b, be, nu, nv, ne: (jnp.maximum(jnp.minimum(b, nu[0] - 1), 0), 0)
    r_cap = xs.shape[0] - DUMP_ROWS
    grid_spec = pltpu.PrefetchScalarGridSpec(
        num_scalar_prefetch=4,
        grid=(r_cap // MOE_BM,),
        in_specs=[
            pl.BlockSpec((MOE_BM, D_XS), blk),
            pl.BlockSpec(memory_space=pl.ANY),
            pl.BlockSpec(memory_space=pl.ANY),
            pl.BlockSpec(memory_space=pl.ANY),
        ],
        out_specs=pl.BlockSpec((MOE_BM, D_MODEL), blk),
        scratch_shapes=[
            pltpu.VMEM((D_MODEL, D_FF_EXPERT), jnp.float32),
            pltpu.VMEM((D_MODEL, D_FF_EXPERT), jnp.float32),
            pltpu.VMEM((D_FF_EXPERT, D_MODEL), jnp.float32),
            pltpu.VMEM((D_MODEL, D_FF_EXPERT), jnp.bfloat16),
            pltpu.VMEM((D_MODEL, D_FF_EXPERT), jnp.bfloat16),
            pltpu.VMEM((D_FF_EXPERT, D_MODEL), jnp.bfloat16),
            pltpu.SemaphoreType.DMA((3,)),
        ],
    )
    return pl.pallas_call(
        functools.partial(_moe_kernel, layer),
        grid_spec=grid_spec,
        out_shape=jax.ShapeDtypeStruct((r_cap, D_MODEL), jnp.bfloat16),
        compiler_params=pltpu.CompilerParams(dimension_semantics=("arbitrary",),
                                             vmem_limit_bytes=VMEM_LIMIT),
        name="moe_experts",
    )(blk_e, nused, nvalid, next_e, xs, wg, wu, wd)


def _combine_kernel(src_ref, x1_ref, col_ref, ys_hbm, out_ref, ybuf, sem):
    i = pl.program_id(0)
    n = pl.num_programs(0)
    p = i % COMBINE_BUFS
    tm = x1_ref.shape[0]

    def start_tile(tile, par):
        for j in range(N_CHUNK):
            s = pl.ds(pl.multiple_of(src_ref[tile, j], SEG_ALIGN), SEG_ALIGN)
            pltpu.make_async_copy(ys_hbm.at[s], ybuf.at[par, j * SEG_ALIGN:(j + 1) * SEG_ALIGN],
                                  sem.at[par]).start()

    def wait_buf(par):
        pltpu.make_async_copy(ys_hbm.at[pl.ds(0, L_CAP)], ybuf.at[par], sem.at[par]).wait()

    @pl.when(i == 0)
    def _():
        start_tile(0, 0)
        start_tile(jnp.minimum(1, n - 1), 1)

    wait_buf(p)
    start_tile(jnp.minimum(i + 2, n - 1), (i + 2) % COMBINE_BUFS)

    rows = COMBINE_ROWS
    for c in range(tm // rows):
        col = col_ref[c * rows:(c + 1) * rows, :]
        pos0 = col[:, ROW_POS0:ROW_POS0 + 1].astype(jnp.int32)
        pos1 = col[:, ROW_POS1:ROW_POS1 + 1].astype(jnp.int32)
        l = lax.broadcasted_iota(jnp.int32, (rows, L_CAP), 1)
        perm = jnp.logical_or(l == pos0, l == pos1).astype(jnp.bfloat16)
        y = jnp.dot(perm, ybuf[p], preferred_element_type=jnp.float32)
        out_ref[c * rows:(c + 1) * rows, :] = x1_ref[c * rows:(c + 1) * rows, :] + y

    @pl.when(i == n - 1)
    def _():
        wait_buf((i + 1) % COMBINE_BUFS)
        wait_buf((i + 2) % COMBINE_BUFS)


def _combine(x1, col, ys, src):
    n_tok = x1.shape[0]
    tm = TM_PROJ
    grid_spec = pltpu.PrefetchScalarGridSpec(
        num_scalar_prefetch=1,
        grid=(n_tok // tm,),
        in_specs=[
            pl.BlockSpec((tm, D_MODEL), lambda i, *_: (i, 0)),
            pl.BlockSpec((tm, LANES), lambda i, *_: (i, 0)),
            pl.BlockSpec(memory_space=pl.ANY),
        ],
        out_specs=pl.BlockSpec((tm, D_MODEL), lambda i, *_: (i, 0)),
        scratch_shapes=[
            pltpu.VMEM((COMBINE_BUFS, L_CAP, D_MODEL), jnp.bfloat16),
            pltpu.SemaphoreType.DMA((COMBINE_BUFS,)),
        ],
    )
    return pl.pallas_call(
        _combine_kernel,
        grid_spec=grid_spec,
        out_shape=jax.ShapeDtypeStruct((n_tok, D_MODEL), jnp.float32),
        compiler_params=pltpu.CompilerParams(dimension_semantics=("arbitrary",),
                                             vmem_limit_bytes=VMEM_LIMIT),
        name="moe_combine",
    )(src, x1, col, ys)


def _routing_tables(counts, n_blk_cap):
    r_cap = n_blk_cap * MOE_BM
    i32 = jnp.int32
    ca = ((counts + (SEG_ALIGN - 1)) // SEG_ALIGN) * SEG_ALIGN
    lend = jnp.cumsum(ca, axis=1)
    lstart = lend - ca
    tot = jnp.sum(ca, axis=0)
    region = ((tot + (MOE_BM - 1)) // MOE_BM) * MOE_BM
    rend = jnp.cumsum(region)
    base = rend - region
    gstart = base[None, :] + jnp.cumsum(ca, axis=0) - ca
    j16 = jnp.arange(N_CHUNK, dtype=i32) * SEG_ALIGN
    in_seg = jnp.logical_and(lstart[:, None, :] <= j16[None, :, None], j16[None, :, None] < lend[:, None, :])
    seg_row = (jnp.sum(jnp.where(in_seg, (gstart - lstart)[:, None, :], 0), axis=2) + j16[None, :]).astype(i32)
    used = j16[None, :] < lend[:, -1:]
    tile_par = (jnp.arange(counts.shape[0], dtype=i32) % 2)[:, None]
    park = r_cap + (tile_par * N_CHUNK + jnp.arange(N_CHUNK, dtype=i32)[None, :]) * SEG_ALIGN
    dst = jnp.where(used, seg_row, park).astype(i32)
    src = jnp.where(used, seg_row, 0).astype(i32)
    nused = (rend[-1] // MOE_BM).astype(i32)
    brow = jnp.minimum(jnp.arange(n_blk_cap, dtype=i32), nused - 1) * MOE_BM
    blk_e = jnp.minimum(jnp.sum(rend[None, :] <= brow[:, None], axis=1), N_EXPERTS - 1).astype(i32)
    zstart = (base + tot).astype(i32)
    nzc = ((region - tot) // SEG_ALIGN).astype(i32)
    blk_i = jnp.arange(n_blk_cap, dtype=i32)
    seg_end = jnp.sum(jnp.where(blk_e[:, None] == jnp.arange(N_EXPERTS)[None, :], (base + tot)[None, :], 0), axis=1)
    nvalid = jnp.where(blk_i < nused, jnp.clip(seg_end - blk_i * MOE_BM, 0, MOE_BM), 0).astype(i32)
    eid = jnp.arange(N_EXPERTS, dtype=i32)
    later = jnp.logical_and(eid[None, :] > eid[:, None], (region > 0)[None, :])
    nxt = jnp.min(jnp.where(later, eid[None, :], N_EXPERTS), axis=1)
    nxt = jnp.where(nxt < N_EXPERTS, nxt, eid)
    next_e = jnp.sum(jnp.where(blk_e[:, None] == eid[None, :], nxt[None, :], 0), axis=1).astype(i32)
    return dst, src, nzc, zstart, blk_e, nused[None], nvalid, next_e


def _t5_bucket_np(dist):
    max_exact = N_BUCKETS // 2
    d = np.maximum(dist, 1).astype(np.float32)
    large = max_exact + (np.log(d / max_exact) / math.log(MAX_DISTANCE / max_exact)
                         * (N_BUCKETS - max_exact)).astype(np.int32)
    large = np.minimum(large, N_BUCKETS - 1)
    return np.where(dist < max_exact, dist, large)


def _attention_bias(rel_bias):
    qi = np.arange(ATT_BLOCK)[:, None]
    kj = np.arange(2 * ATT_BLOCK)[None, :]
    dist = qi + ATT_BLOCK - kj
    in_window = (dist >= 0) & (dist < WINDOW)
    bucket = _t5_bucket_np(np.clip(dist, 0, WINDOW - 1))
    onehot = (bucket[None] == np.arange(N_BUCKETS)[:, None, None]).astype(np.float32)
    bias = jnp.einsum('nh,nqk->hqk', rel_bias.astype(jnp.float32), jnp.asarray(onehot),
                      precision=lax.Precision.HIGHEST)
    bias = jnp.where(jnp.asarray(in_window)[None], bias * LOG2E, NEG_BIG)
    no_prev = jnp.asarray(kj < ATT_BLOCK)[None]
    return jnp.stack([bias, jnp.where(no_prev, NEG_BIG, bias)])


def _block_diag_mean(width, block):
    idx = np.arange(width) // block
    return jnp.asarray((idx[:, None] == idx[None, :]).astype(np.float32) / block, dtype=jnp.bfloat16)


def kernel(x, rel_bias, norm_mix_g, w_in, q_norm_g, k_norm_g, attn_sink, conv_w, conv_b, gate_b,
           mlstm_norm_g, w_out, norm_ffn_g, w_router_group, b_router_group, w_router_expert,
           b_router_expert, w_gate, w_up, w_down):
    batch, seq_len, _ = x.shape
    n_tok = batch * seq_len
    assert seq_len % TM_OUT == 0 and seq_len % ML_KCHUNK == 0
    f32 = jnp.float32
    bias = _attention_bias(rel_bias)
    bdq = _block_diag_mean(ATT_WIDTH, ATT_HEAD_DIM)
    bdk = _block_diag_mean(LANES, ATT_HEAD_DIM)
    x2 = x.reshape(n_tok, D_MODEL)
    for l in range(DEPTH):
        wgt = _bf16(w_in[l, :, O_G:N_IN].T)
        qg = (jnp.tile(q_norm_g[l].astype(f32), ATT_HEADS) * (ATT_HEAD_DIM ** -0.5 * LOG2E))[None, :]
        kg = jnp.tile(k_norm_g[l].astype(f32), ATT_KV_HEADS)[None, :]
        gbt = gate_b[l].astype(f32)[:, None]
        qn, kn, vd, hm = _front(
            x2.reshape(batch, seq_len, D_MODEL), norm_mix_g[l][None, :], w_in, l, wgt, qg, kg, bdq, bdk,
            conv_w[l], conv_b[l][None, :], gbt, mlstm_norm_g[l][None, :])
        flat = lambda a: a.reshape(n_tok, a.shape[-1])
        att = _attention(flat(qn), flat(kn), flat(vd), bias, attn_sink[l].astype(f32) * LOG2E,
                         batch, seq_len)
        n_rt = 4 * SUBLANES
        wrt = jnp.pad(jnp.concatenate([w_router_expert[l], w_router_group[l]], axis=1).astype(f32).T,
                      ((0, n_rt - N_EXPERTS - N_GROUPS), (0, 0)))
        brt = jnp.pad(jnp.concatenate([b_router_expert[l], b_router_group[l]]).astype(f32),
                      (0, n_rt - N_EXPERTS - N_GROUPS))[:, None]
        wrt_hi = _bf16(wrt)
        wrt = jnp.concatenate([wrt_hi, _bf16(wrt - wrt_hi.astype(f32))], axis=0)
        x1, t, row, col, cnt = _out_proj_router(x2, att, hm.reshape(n_tok, ML_WIDTH), w_out, l,
                                                norm_ffn_g[l][None, :], wrt, brt)
        _, n_blk_cap = _moe_capacity(n_tok)
        dst, src, nz, zdst, blk_e, nused, nvalid, next_e = _routing_tables(cnt[:, :, 0], n_blk_cap)
        xs = _dispatch(t, col, row, dst, nz, zdst)
        ys = _moe(xs, blk_e, nused, nvalid, next_e, w_gate, w_up, w_down, l)
        x2 = _combine(x1, col, ys, src)
    return x2.reshape(batch, seq_len, D_MODEL)
```

```python
import functools
import math

import jax
import jax.numpy as jnp
import numpy as np
from jax import lax
from jax.experimental import pallas as pl
from jax.experimental.pallas import tpu as pltpu

D_MODEL = 1024
DEPTH = 2
ATT_HEADS = 8
ATT_KV_HEADS = 2
ATT_HEAD_DIM = 64
ATT_WIDTH = ATT_HEADS * ATT_HEAD_DIM
WINDOW = 128
ATT_BLOCK = 128
N_BUCKETS = 32
MAX_DISTANCE = 128
ML_HEADS = 4
ML_DQK = 64
ML_DV = 128
ML_WIDTH = ML_HEADS * ML_DV
ML_CHUNK = 64
CONV_K = 4
N_GROUPS = 4
EXPERTS_PER_GROUP = 4
N_EXPERTS = N_GROUPS * EXPERTS_PER_GROUP
D_FF_EXPERT = 512
EPS = 1e-6

LANES = 128
SUBLANES = 8
NEG_BIG = -1e30
LOG2E = math.log2(math.e)
VMEM_LIMIT = 48 * 1024 * 1024
FRONT_VMEM_LIMIT = 56 * 1024 * 1024

O_Q = 0
O_K = O_Q + ATT_WIDTH
O_V = O_K + ATT_KV_HEADS * ATT_HEAD_DIM
O_QM = O_V + ATT_KV_HEADS * ATT_HEAD_DIM
O_KM = O_QM + ML_HEADS * ML_DQK
O_VM = O_KM + ML_HEADS * ML_DQK
O_OM = O_VM + ML_WIDTH
O_G = O_OM + ML_WIDTH
N_IN = O_G + 2 * ML_HEADS

ATT_STEP_BLOCKS = 4
TM_OUT = 1024
TM_PROJ = 512
ML_KCHUNK = 256

SEG_ALIGN = 16
MOE_BM = 1024
MOE_TAIL = 256
MOE_FF_SPLIT = 2
L_CAP = 2 * TM_PROJ + N_EXPERTS * SEG_ALIGN
N_CHUNK = L_CAP // SEG_ALIGN
D_XS = D_MODEL + LANES
DUMP_ROWS = 2 * N_CHUNK * SEG_ALIGN
DISPATCH_ROWS = 256
COMBINE_ROWS = 256
COMBINE_BUFS = 3
ROW_POS0, ROW_POS1, ROW_W0, ROW_W1, ROW_E0, ROW_E1 = 0, 1, 2, 3, 4, 5
COL_W0H, COL_W1H, COL_E0, COL_E1 = 8, 11, 14, 15


def _moe_capacity(n_tok):
    n_tiles = n_tok // TM_PROJ
    rows = 2 * n_tok + n_tiles * N_EXPERTS * (SEG_ALIGN - 1) + N_EXPERTS * (MOE_BM - SEG_ALIGN)
    n_blk = -(-rows // MOE_BM)
    return n_blk * MOE_BM, n_blk


def _bf16(a):
    return a.astype(jnp.bfloat16)


def _split3(a):
    hi = _bf16(a)
    r1 = a - hi.astype(jnp.float32)
    mid = _bf16(r1)
    return hi, mid, _bf16(r1 - mid.astype(jnp.float32))


def _log_sigmoid(z):
    return jnp.minimum(z, 0.0) - jnp.log(1.0 + jnp.exp(-jnp.abs(z)))


def _front_kernel(x_ref, g_ref, w_ref, qg_ref, kg_ref, bdq_ref, bdk_ref, cw_ref, cb_ref,
                  gbt_ref, ng_ref,
                  qn_ref, kn_ref, vd_ref, hm_ref,
                  w_bf, wgt_ref, conv_scr, qkm_s, vm_s, om_s, gt_s, c_scr, m_scr):
    s = pl.program_id(0)
    batch = x_ref.shape[0]
    L = ML_KCHUNK
    wr = s % 2
    rd = 1 - wr

    n_chunks = pl.num_programs(0) - 1

    @pl.when(s == 0)
    def _():
        w_bf[...] = _bf16(w_ref[...])
        gate_cols = jnp.concatenate([w_ref[:, O_G:N_IN],
                                     jnp.zeros((D_MODEL, LANES - (N_IN - O_G)), jnp.float32)], axis=1)
        wgt_ref[...] = _bf16(gate_cols.T[0:SUBLANES, :])
        conv_scr[:, 0:SUBLANES, :] = jnp.zeros((batch, SUBLANES, conv_scr.shape[2]), jnp.float32)
        c_scr[...] = jnp.zeros(c_scr.shape, jnp.float32)
        m_scr[...] = jnp.zeros(m_scr.shape, jnp.float32)

    @pl.when(s > 0)
    def _():
        conv_scr[:, 0:SUBLANES, :] = conv_scr[:, L:L + SUBLANES, :]

    def scan_pieces():
        return _mlstm_pieces(qkm_s.at[rd], vm_s.at[rd], om_s.at[rd], gt_s.at[rd], ng_ref, hm_ref,
                             c_scr, m_scr, batch)

    def proj_pieces():
        return _project_pieces(x_ref, g_ref, w_bf, wgt_ref, qg_ref, kg_ref, bdq_ref, bdk_ref, cw_ref,
                               cb_ref, gbt_ref, qn_ref, kn_ref, vd_ref, qkm_s.at[wr], vm_s.at[wr],
                               om_s.at[wr], gt_s.at[wr], conv_scr)

    @pl.when(s == 0)
    def _():
        norm_part, proj_parts = proj_pieces()
        hbs = [norm_part(b) for b in range(batch)]
        for part in proj_parts:
            for b in range(batch):
                part(b, hbs[b])

    @pl.when(jnp.logical_and(s > 0, s < n_chunks))
    def _():
        scan_prep, scan_gates, scan_head, scan_finish = scan_pieces()
        norm_part, proj_parts = proj_pieces()
        preps = [scan_prep(b) for b in range(batch)]
        hbs = [norm_part(b) for b in range(batch)]
        gates = [scan_gates(b, 0, preps[b]) for b in range(batch)]
        for stage in range(len(proj_parts)):
            if stage < ML_HEADS:
                nxt = []
                for b in range(batch):
                    scan_head(b, stage, gates[b])
                    if stage + 1 < ML_HEADS:
                        nxt.append(scan_gates(b, stage + 1, preps[b]))
                gates = nxt
            for b in range(batch):
                proj_parts[stage](b, hbs[b])
        scan_finish()

    @pl.when(s == n_chunks)
    def _():
        scan_prep, scan_gates, scan_head, scan_finish = scan_pieces()
        preps = [scan_prep(b) for b in range(batch)]
        gates = [scan_gates(b, 0, preps[b]) for b in range(batch)]
        for h in range(ML_HEADS):
            nxt = []
            for b in range(batch):
                scan_head(b, h, gates[b])
                if h + 1 < ML_HEADS:
                    nxt.append(scan_gates(b, h + 1, preps[b]))
            gates = nxt
        scan_finish()


def _project_pieces(x_ref, g_ref, w_bf, wgt_ref, qg_ref, kg_ref, bdq_ref, bdk_ref, cw_ref, cb_ref,
                    gbt_ref, qn_ref, kn_ref, vd_ref, qkm_ref, vm_ref, om_ref, gatet_ref, conv_scr):
    L = ML_KCHUNK
    conv_w = 2 * ML_HEADS * ML_DQK

    def proj(hb, c0, width):
        return jnp.dot(hb, w_bf[:, c0:c0 + width], preferred_element_type=jnp.float32)

    def norm_part(b):
        x = x_ref[b]
        hn = x * lax.rsqrt(jnp.mean(x * x, axis=-1, keepdims=True) + EPS) * g_ref[...]
        return _bf16(hn)

    def q_part(b, hb):
        q = proj(hb, O_Q, ATT_WIDTH)
        q_ms = jnp.dot(_bf16(q * q), bdq_ref[...], preferred_element_type=jnp.float32)
        qn_ref[b] = _bf16(q * lax.rsqrt(q_ms + EPS) * qg_ref[...])

    def kv_part(b, hb):
        kv = proj(hb, O_K, 2 * LANES)
        k = kv[:, 0:LANES]
        v = kv[:, LANES:2 * LANES]
        k_ms = jnp.dot(_bf16(k * k), bdk_ref[...], preferred_element_type=jnp.float32)
        kn = k * lax.rsqrt(k_ms + EPS) * kg_ref[...]
        low = lax.broadcasted_iota(jnp.int32, kn.shape, 1) < ATT_HEAD_DIM

        def dup_heads(a):
            swapped = pltpu.roll(a, ATT_HEAD_DIM, axis=1)
            return jnp.concatenate([jnp.where(low, a, swapped), jnp.where(low, swapped, a)], axis=1)

        kn_ref[b] = _bf16(dup_heads(kn))
        vd_ref[b] = _bf16(dup_heads(v))

    def conv_part(b, hb):
        qk = proj(hb, O_QM, conv_w)
        conv_scr[b, SUBLANES:SUBLANES + L, :] = qk
        y = qk * cw_ref[CONV_K - 1:CONV_K, :] + cb_ref[...]
        for j in range(CONV_K - 1):
            off = SUBLANES - (CONV_K - 1) + j
            y = y + conv_scr[b, off:off + L, :] * cw_ref[j:j + 1, :]
        y = y * jax.nn.sigmoid(y)
        lane = lax.broadcasted_iota(jnp.int32, y.shape, 1)
        y = jnp.where(lane >= ML_HEADS * ML_DQK, y * (ML_DQK ** -0.5), y)
        qkm_ref[b] = _bf16(y)

    def vm_part(b, hb):
        vm_ref[b] = _bf16(proj(hb, O_VM, ML_WIDTH))

    def om_part(b, hb):
        om_ref[b] = _bf16(jax.nn.sigmoid(proj(hb, O_OM, ML_WIDTH)))

        gt = lax.dot_general(wgt_ref[...], hb, (((1,), (1,)), ((), ())),
                             preferred_element_type=jnp.float32) + gbt_ref[...]
        grow = lax.broadcasted_iota(jnp.int32, gt.shape, 0)
        gatet_ref[b] = jnp.where(grow >= ML_HEADS, _log_sigmoid(gt), gt) * LOG2E

    return norm_part, (q_part, kv_part, conv_part, vm_part, om_part)


def _front(x3, g, w_in, layer, qg, kg, bdq, bdk, cw, cb, gbt, ng):
    batch, seq_len, _ = x3.shape
    L = ML_KCHUNK
    nc = seq_len // L
    fix = lambda s: (0, 0)
    cur = lambda s: (0, jnp.minimum(s, nc - 1), 0)
    prev = lambda s: (0, jnp.maximum(s - 1, 0), 0)
    kv_w = 2 * ATT_KV_HEADS * ATT_HEAD_DIM
    conv_w = 2 * ML_HEADS * ML_DQK
    return pl.pallas_call(
        _front_kernel,
        grid=(nc + 1,),
        in_specs=[
            pl.BlockSpec((batch, L, D_MODEL), cur),
            pl.BlockSpec((1, D_MODEL), fix),
            pl.BlockSpec((None, D_MODEL, N_IN), lambda s: (layer, 0, 0), pipeline_mode=pl.Buffered(1)),
            pl.BlockSpec((1, ATT_WIDTH), fix),
            pl.BlockSpec((1, LANES), fix),
            pl.BlockSpec((ATT_WIDTH, ATT_WIDTH), fix),
            pl.BlockSpec((LANES, LANES), fix),
            pl.BlockSpec((CONV_K, conv_w), fix),
            pl.BlockSpec((1, conv_w), fix),
            pl.BlockSpec((SUBLANES, 1), fix),
            pl.BlockSpec((1, ML_WIDTH), fix),
        ],
        out_specs=[
            pl.BlockSpec((batch, L, ATT_WIDTH), cur),
            pl.BlockSpec((batch, L, kv_w), cur),
            pl.BlockSpec((batch, L, kv_w), cur),
            pl.BlockSpec((batch, L, ML_WIDTH), prev),
        ],
        out_shape=(
            jax.ShapeDtypeStruct((batch, seq_len, ATT_WIDTH), jnp.bfloat16),
            jax.ShapeDtypeStruct((batch, seq_len, kv_w), jnp.bfloat16),
            jax.ShapeDtypeStruct((batch, seq_len, kv_w), jnp.bfloat16),
            jax.ShapeDtypeStruct((batch, seq_len, ML_WIDTH), jnp.bfloat16),
        ),
        scratch_shapes=[
            pltpu.VMEM((D_MODEL, N_IN), jnp.bfloat16),
            pltpu.VMEM((SUBLANES, D_MODEL), jnp.bfloat16),
            pltpu.VMEM((batch, L + 2 * SUBLANES, conv_w), jnp.float32),
            pltpu.VMEM((2, batch, L, conv_w), jnp.bfloat16),
            pltpu.VMEM((2, batch, L, ML_WIDTH), jnp.bfloat16),
            pltpu.VMEM((2, batch, L, ML_WIDTH), jnp.bfloat16),
            pltpu.VMEM((2, batch, SUBLANES, L), jnp.float32),
            pltpu.VMEM((batch * ML_HEADS // 2, 2 * ML_DQK, 2 * ML_DV), jnp.float32),
            pltpu.VMEM((batch * ML_HEADS, 1, LANES), jnp.float32),
        ],
        compiler_params=pltpu.CompilerParams(dimension_semantics=("arbitrary",),
                                             vmem_limit_bytes=FRONT_VMEM_LIMIT),
        name="proj_mlstm",
    )(x3, g, w_in, qg, kg, bdq, bdk, cw, cb, gbt, ng)


def _attn_kernel(sink_ref, q_ref, kp_ref, kc_ref, vp_ref, vc_ref, bias_ref, o_ref):
    i = pl.program_id(1)
    blk = ATT_BLOCK
    lane = lax.broadcasted_iota(jnp.int32, (blk, LANES), 1)
    low = lane < ATT_HEAD_DIM
    group = ATT_HEADS // ATT_KV_HEADS
    ones = jnp.ones((2 * blk, LANES), jnp.bfloat16)
    for j in range(ATT_STEP_BLOCKS):
        rows = slice(j * blk, (j + 1) * blk)
        first = (i == 0).astype(jnp.int32) if j == 0 else 0
        for kv in range(ATT_KV_HEADS):
            lanes = slice(kv * LANES, (kv + 1) * LANES)
            if j == 0:
                kcat = jnp.concatenate([kp_ref[:, lanes], kc_ref[0:blk, lanes]], axis=0)
                vrows = jnp.concatenate([vp_ref[:, lanes], vc_ref[0:blk, lanes]], axis=0)
            else:
                kcat = kc_ref[(j - 1) * blk:(j + 1) * blk, lanes]
                vrows = vc_ref[(j - 1) * blk:(j + 1) * blk, lanes]
            vcat = jnp.concatenate([vrows, ones], axis=1)
            heads = range(kv * group, (kv + 1) * group)
            qs = []
            for h in heads:
                qp = q_ref[rows, (h // 2) * LANES:(h // 2 + 1) * LANES]
                qs.append(jnp.where(low if h % 2 == 0 else jnp.logical_not(low), qp, jnp.zeros_like(qp)))
            s_all = lax.dot_general(jnp.concatenate(qs, axis=0), kcat, (((1,), (1,)), ((), ())),
                                    preferred_element_type=jnp.float32)
            ps, ms = [], []
            for n, h in enumerate(heads):
                logits = s_all[n * blk:(n + 1) * blk, :] + bias_ref[first, h]
                row_max = jnp.broadcast_to(jnp.max(logits, axis=-1, keepdims=True), (blk, LANES))
                m = jnp.maximum(row_max, sink_ref[h])
                ps.append(_bf16(jnp.exp2(logits - jnp.concatenate([m, m], axis=1))))
                ms.append(m)
            o_all = jnp.dot(jnp.concatenate(ps, axis=0), vcat, preferred_element_type=jnp.float32)
            outs = []
            for n, h in enumerate(heads):
                o = o_all[n * blk:(n + 1) * blk, :]
                den = o[:, LANES:2 * LANES] + jnp.exp2(sink_ref[h] - ms[n])
                outs.append(o[:, 0:LANES] / den)
            for n in range(0, group, 2):
                pair = (kv * group + n) // 2
                o_ref[rows, pair * LANES:(pair + 1) * LANES] = _bf16(
                    jnp.where(low, outs[n], outs[n + 1]))


def _attention(qn, kn, vd, bias, sink, batch, seq_len):
    step = ATT_STEP_BLOCKS * ATT_BLOCK
    ns = seq_len // step
    cur = lambda b, i, s: (b * ns + i, 0)
    prev = lambda b, i, s: ((b * ns + i) * ATT_STEP_BLOCKS - jnp.minimum(i, 1), 0)
    grid_spec = pltpu.PrefetchScalarGridSpec(
        num_scalar_prefetch=1,
        grid=(batch, ns),
        in_specs=[
            pl.BlockSpec((step, ATT_WIDTH), cur),
            pl.BlockSpec((ATT_BLOCK, 2 * LANES), prev),
            pl.BlockSpec((step, 2 * LANES), cur),
            pl.BlockSpec((ATT_BLOCK, 2 * LANES), prev),
            pl.BlockSpec((step, 2 * LANES), cur),
            pl.BlockSpec((2, ATT_HEADS, ATT_BLOCK, 2 * ATT_BLOCK), lambda b, i, s: (0, 0, 0, 0)),
        ],
        out_specs=pl.BlockSpec((step, ATT_WIDTH), cur),
    )
    return pl.pallas_call(
        _attn_kernel,
        grid_spec=grid_spec,
        out_shape=jax.ShapeDtypeStruct((batch * seq_len, ATT_WIDTH), jnp.bfloat16),
        compiler_params=pltpu.CompilerParams(dimension_semantics=("arbitrary", "arbitrary"),
                                             vmem_limit_bytes=VMEM_LIMIT),
        name="swa_attention",
    )(sink, qn, kn, kn, vd, vd, bias)


def _mlstm_pieces(qk_ref, v_ref, o_ref, gt_ref, ng_ref, out_ref, c_scr, m_scr, batch):
    L = ML_KCHUNK
    pairs = ML_HEADS // 2
    r_i = lax.broadcasted_iota(jnp.int32, (L, L), 0)
    c_i = lax.broadcasted_iota(jnp.int32, (L, L), 1)
    causal = c_i <= r_i
    tril_bf = causal.astype(jnp.bfloat16)
    triu_bf = (r_i <= c_i).astype(jnp.bfloat16)
    mean_dv = jnp.full((ML_DV, ML_DV), 1.0 / ML_DV, jnp.bfloat16)
    lane = lax.broadcasted_iota(jnp.int32, (L, LANES), 1)
    low = lane < ML_DQK
    ones_dv = jnp.ones((L, ML_DV), jnp.bfloat16)
    row2 = lax.broadcasted_iota(jnp.int32, (2 * ML_DQK, 2 * ML_DV), 0)

    def twice(a):
        return jnp.concatenate([a, a], axis=1)

    m_state = [m_scr[k] for k in range(batch * ML_HEADS)]
    c_state = [c_scr[k] for k in range(batch * pairs)]

    def batch_prep(b):
        gt = gt_ref[b]
        g = jnp.concatenate([gt, jnp.zeros((LANES - SUBLANES, L), jnp.float32)], axis=0).T
        bcols = sum(jnp.dot(tril_bf, part, preferred_element_type=jnp.float32) for part in _split3(g))
        brows = sum(jnp.dot(part, triu_bf, preferred_element_type=jnp.float32) for part in _split3(gt))
        return gt, g, bcols, brows

    carry = {}

    def gate_piece(b, h, prep):
        gt, g, bcols, brows = prep
        m_prev = m_state[b * ML_HEADS + h]
        bc = jnp.broadcast_to(bcols[:, ML_HEADS + h:ML_HEADS + h + 1], (L, LANES))
        lic = jnp.broadcast_to(g[:, h:h + 1], (L, LANES))
        br = brows[ML_HEADS + h:ML_HEADS + h + 1, :]
        lir = gt[h:h + 1, :]
        log_d = jnp.where(causal, twice(bc) - (br - lir), NEG_BIG)
        m_inter = bc + m_prev
        row_max = jnp.broadcast_to(jnp.max(log_d, axis=-1, keepdims=True), (L, LANES))
        m_row = jnp.maximum(m_inter, row_max)
        d = jnp.exp2(log_d - twice(m_row))
        inter = jnp.exp2(m_inter - m_row)
        b_last = bc[L - 1:L, :]
        log_w = b_last - bc + lic
        m_next = jnp.maximum(b_last + m_prev, jnp.max(log_w, axis=0, keepdims=True))
        w = jnp.exp2(log_w - m_next)
        decay = jnp.exp2(b_last + m_prev - m_next)
        m_state[b * ML_HEADS + h] = m_next
        return d, inter, jnp.exp2(-m_row), w, decay

    def head_piece(b, h, gates):
        d, inter, floor, w, decay = gates
        pair, sub = divmod(h, 2)
        sidx = b * pairs + pair
        qp = qk_ref[b, :, pair * LANES:(pair + 1) * LANES]
        kp = qk_ref[b, :, (pairs + pair) * LANES:(pairs + pair + 1) * LANES]
        c_pair = c_state[sidx]
        c_bf = _bf16(c_pair)
        sel = low if sub == 0 else jnp.logical_not(low)
        qm = jnp.where(sel, qp, jnp.zeros_like(qp))
        s = lax.dot_general(qm, kp, (((1,), (1,)), ((), ())),
                            preferred_element_type=jnp.float32) * d
        v_ext = jnp.concatenate([v_ref[b, :, h * ML_DV:(h + 1) * ML_DV], ones_dv],
                                axis=-1)
        num = twice(inter) * jnp.dot(qm, c_bf, preferred_element_type=jnp.float32) \
            + jnp.dot(_bf16(s), v_ext, preferred_element_type=jnp.float32)
        den = num[:, ML_DV:2 * ML_DV]
        hval = num[:, 0:ML_DV] / jnp.maximum(jnp.abs(den), floor)
        h_ms = jnp.dot(_bf16(hval * hval), mean_dv, preferred_element_type=jnp.float32)
        hn = hval * lax.rsqrt(h_ms + EPS)
        hn = hn * ng_ref[:, h * ML_DV:(h + 1) * ML_DV]
        out_ref[b, :, h * ML_DV:(h + 1) * ML_DV] = _bf16(
            hn * o_ref[b, :, h * ML_DV:(h + 1) * ML_DV].astype(jnp.float32))
        kw = _bf16(jnp.where(sel, kp.astype(jnp.float32) * w, 0.0))
        upd = lax.dot_general(kw, v_ext, (((0,), (0,)), ((), ())),
                              preferred_element_type=jnp.float32)
        if sub == 0:
            carry[sidx] = (decay, upd)
        else:
            decay0, upd0 = carry.pop(sidx)
            decay_rows = jnp.where(row2 < ML_DQK, twice(decay0), twice(decay))
            c_state[sidx] = decay_rows * c_pair + (upd0 + upd)

    def finish():
        for k in range(batch * ML_HEADS):
            m_scr[k] = m_state[k]
        for k in range(batch * pairs):
            c_scr[k] = c_state[k]

    return batch_prep, gate_piece, head_piece, finish


def _out_proj_router_kernel(x_ref, att_ref, hm_ref, wo_ref, g_ref, wrt_ref, brt_ref,
                            x1_ref, t_ref, row_ref, col_ref, cnt_ref, wo_bf):
    tm = x_ref.shape[0]
    tr = TM_PROJ
    hi = lax.Precision.HIGHEST

    @pl.when(pl.program_id(0) == 0)
    def _():
        wo_bf[...] = _bf16(wo_ref[...])

    logits_all = []
    x1_all = []
    for sb in range(tm // TM_PROJ):
        rs = slice(sb * TM_PROJ, (sb + 1) * TM_PROJ)
        mixed = jnp.concatenate([att_ref[rs, :], hm_ref[rs, :]], axis=1)
        x1_all.append(x_ref[rs, :] + jnp.dot(mixed, wo_bf[...], preferred_element_type=jnp.float32))
    for sb in range(tm // TM_PROJ):
        rs = slice(sb * TM_PROJ, (sb + 1) * TM_PROJ)
        x1 = x1_all[sb]
        x1_ref[rs, :] = x1
        tn = x1 * lax.rsqrt(jnp.mean(x1 * x1, axis=-1, keepdims=True) + EPS) * g_ref[...]
        tn_hi = _bf16(tn)
        t_ref[rs, :] = tn_hi
        tn_lo = _bf16(tn - tn_hi.astype(jnp.float32))
        nt = (((1,), (1,)), ((), ()))
        n_rt = wrt_ref.shape[0] // 2
        p_hi = lax.dot_general(wrt_ref[...], tn_hi, nt, preferred_element_type=jnp.float32)
        p_lo = lax.dot_general(wrt_ref[0:n_rt, :], tn_lo, nt, preferred_element_type=jnp.float32)
        logits_all.append(p_hi[0:n_rt, :] + p_hi[n_rt:, :] + p_lo + brt_ref[...])

    for sb in range(tm // TM_PROJ):
        rs = slice(sb * TM_PROJ, (sb + 1) * TM_PROJ)
        logits = logits_all[sb]
        el_all = logits[0:N_EXPERTS, :]
        gl = logits[N_EXPERTS:N_EXPERTS + SUBLANES, :]
        grow = lax.broadcasted_iota(jnp.int32, gl.shape, 0).astype(jnp.float32)
        gl = jnp.where(grow < N_GROUPS, gl, NEG_BIG)
        gmax = jnp.max(gl, axis=0, keepdims=True)
        grp = jnp.min(jnp.where(gl == gmax, grow, float(N_GROUPS)), axis=0, keepdims=True)
        p_grp = 1.0 / jnp.sum(jnp.exp(gl - gmax), axis=0, keepdims=True)
        erow = lax.broadcasted_iota(jnp.int32, el_all.shape, 0).astype(jnp.float32)
        egrp = jnp.floor(erow * (1.0 / EXPERTS_PER_GROUP))
        el = jnp.where(egrp == grp, el_all, NEG_BIG)
        e1 = jnp.max(el, axis=0, keepdims=True)
        i1 = jnp.min(jnp.where(el == e1, erow, float(N_EXPERTS)), axis=0, keepdims=True)
        el2 = jnp.where(erow == i1, NEG_BIG, el)
        e2 = jnp.max(el2, axis=0, keepdims=True)
        i2 = jnp.min(jnp.where(el2 == e2, erow, float(N_EXPERTS)), axis=0, keepdims=True)
        z2 = jnp.exp(e2 - e1)
        w1 = p_grp / (1.0 + z2)
        w2 = p_grp * z2 / (1.0 + z2)
        sel1 = erow == i1
        sel2 = erow == i2
        onehot = jnp.logical_or(sel1, sel2)
        t_r = lax.broadcasted_iota(jnp.int32, (tr, tr), 0)
        t_c = lax.broadcasted_iota(jnp.int32, (tr, tr), 1)
        before = (t_r < t_c).astype(jnp.bfloat16)
        rank = jnp.dot(onehot.astype(jnp.bfloat16), before, preferred_element_type=jnp.float32)
        cnt = jnp.sum(onehot.astype(jnp.float32), axis=1, keepdims=True)
        cnt_al = jnp.floor((cnt + (SEG_ALIGN - 1)) * (1.0 / SEG_ALIGN)) * SEG_ALIGN
        e_r = lax.broadcasted_iota(jnp.int32, (N_EXPERTS, N_EXPERTS), 0)
        e_c = lax.broadcasted_iota(jnp.int32, (N_EXPERTS, N_EXPERTS), 1)
        lstart = jnp.dot((e_c < e_r).astype(jnp.float32), jnp.broadcast_to(cnt_al, (N_EXPERTS, LANES)),
                         precision=hi, preferred_element_type=jnp.float32)[:, 0:1]
        slot = lstart + rank
        pos1 = jnp.sum(jnp.where(sel1, slot, 0.0), axis=0, keepdims=True)
        pos2 = jnp.sum(jnp.where(sel2, slot, 0.0), axis=0, keepdims=True)
        r8 = lax.broadcasted_iota(jnp.int32, (SUBLANES, tr), 0)
        info = jnp.where(r8 == ROW_POS0, pos1, jnp.where(r8 == ROW_POS1, pos2, jnp.where(
            r8 == ROW_W0, w1, jnp.where(r8 == ROW_W1, w2, jnp.where(
                r8 == ROW_E0, i1, jnp.where(r8 == ROW_E1, i2, 0.0))))))
        row_ref[:, rs] = info

        def split3(w):
            h = _bf16(w).astype(jnp.float32)
            m = _bf16(w - h).astype(jnp.float32)
            return h, m, _bf16(w - h - m).astype(jnp.float32)

        w1h, w1m, w1l = split3(w1)
        w2h, w2m, w2l = split3(w2)
        parts = jnp.where(r8 == 0, w1h, jnp.where(r8 == 1, w1m, jnp.where(r8 == 2, w1l, jnp.where(
            r8 == 3, w2h, jnp.where(r8 == 4, w2m, jnp.where(r8 == 5, w2l, jnp.where(r8 == 6, i1, i2)))))))
        col_ref[rs, :] = jnp.concatenate(
            [info, parts, jnp.zeros((LANES - 2 * SUBLANES, tr), jnp.float32)], axis=0).T
        cnt_ref[sb] = jnp.broadcast_to(cnt, (N_EXPERTS, LANES)).astype(jnp.int32)


def _out_proj_router(x2, att, hm, wo, layer, g, wrt, brt):
    t = x2.shape[0]
    tm = TM_OUT
    row = lambda i: (i, 0)
    fix = lambda i: (0, 0)
    return pl.pallas_call(
        _out_proj_router_kernel,
        grid=(t // tm,),
        in_specs=[
            pl.BlockSpec((tm, D_MODEL), row),
            pl.BlockSpec((tm, ATT_WIDTH), row),
            pl.BlockSpec((tm, ML_WIDTH), row),
            pl.BlockSpec((None, D_MODEL, D_MODEL), lambda i: (layer, 0, 0), pipeline_mode=pl.Buffered(1)),
            pl.BlockSpec((1, D_MODEL), fix),
            pl.BlockSpec((8 * SUBLANES, D_MODEL), fix),
            pl.BlockSpec((4 * SUBLANES, 1), fix),
        ],
        out_specs=[
            pl.BlockSpec((tm, D_MODEL), row),
            pl.BlockSpec((tm, D_MODEL), row),
            pl.BlockSpec((SUBLANES, tm), lambda i: (0, i)),
            pl.BlockSpec((tm, LANES), row),
            pl.BlockSpec((tm // TM_PROJ, N_EXPERTS, LANES), lambda i: (i, 0, 0)),
        ],
        out_shape=(
            jax.ShapeDtypeStruct((t, D_MODEL), jnp.float32),
            jax.ShapeDtypeStruct((t, D_MODEL), jnp.bfloat16),
            jax.ShapeDtypeStruct((SUBLANES, t), jnp.float32),
            jax.ShapeDtypeStruct((t, LANES), jnp.float32),
            jax.ShapeDtypeStruct((t // TM_PROJ, N_EXPERTS, LANES), jnp.int32),
        ),
        scratch_shapes=[pltpu.VMEM((D_MODEL, D_MODEL), jnp.bfloat16)],
        compiler_params=pltpu.CompilerParams(dimension_semantics=("arbitrary",),
                                             vmem_limit_bytes=VMEM_LIMIT),
        name="out_proj_router",
    )(x2, att, hm, wo, g, wrt, brt)


def _dispatch_kernel(dst_ref, nz_ref, zdst_ref, t_ref, col_ref, row_ref, xs_hbm,
                     xbuf, zx, sem_x, sem_z):
    i = pl.program_id(0)
    n = pl.num_programs(0)
    p = i % 2
    tm = t_ref.shape[0]

    def copy(tile, par, j):
        d = pl.ds(pl.multiple_of(dst_ref[tile, j], SEG_ALIGN), SEG_ALIGN)
        return pltpu.make_async_copy(xbuf.at[par, j * SEG_ALIGN:(j + 1) * SEG_ALIGN], xs_hbm.at[d],
                                     sem_x.at[par])

    def zero_copy(e, j):
        d = pl.ds(pl.multiple_of(zdst_ref[e] + j * SEG_ALIGN, SEG_ALIGN), SEG_ALIGN)
        return pltpu.make_async_copy(zx, xs_hbm.at[d], sem_z.at[0])

    def for_zero_chunks(fn):
        for e in range(N_EXPERTS):
            def body(j, c, e=e):
                fn(zero_copy(e, j))
                return c
            lax.fori_loop(0, nz_ref[e], body, 0)

    def wait_tile(par):
        pltpu.make_async_copy(xbuf.at[par], xs_hbm.at[pl.ds(0, L_CAP)], sem_x.at[par]).wait()

    @pl.when(i == 0)
    def _():
        zx[...] = jnp.zeros(zx.shape, zx.dtype)
        for_zero_chunks(lambda cp: cp.start())

    @pl.when(i >= 2)
    def _():
        wait_tile(p)

    pos0 = row_ref[ROW_POS0:ROW_POS0 + 1, :].astype(jnp.int32)
    pos1 = row_ref[ROW_POS1:ROW_POS1 + 1, :].astype(jnp.int32)
    t = t_ref[...]
    side = _bf16(col_ref[...])
    rows = DISPATCH_ROWS
    for c in range(L_CAP // rows):
        r = lax.broadcasted_iota(jnp.int32, (rows, tm), 0) + c * rows
        perm = jnp.logical_or(r == pos0, r == pos1).astype(jnp.bfloat16)
        xbuf[p, c * rows:(c + 1) * rows, 0:D_MODEL] = _bf16(
            jnp.dot(perm, t, preferred_element_type=jnp.float32))
        xbuf[p, c * rows:(c + 1) * rows, D_MODEL:D_XS] = _bf16(
            jnp.dot(perm, side, preferred_element_type=jnp.float32))
        for j in range(c * rows // SEG_ALIGN, (c + 1) * rows // SEG_ALIGN):
            copy(i, p, j).start()

    @pl.when(i == n - 1)
    def _():
        @pl.when(n >= 2)
        def _():
            wait_tile(1 - p)
        wait_tile(p)
        for_zero_chunks(lambda cp: cp.wait())


def _dispatch(t, col, row, dst, nz, zdst):
    n_tok = t.shape[0]
    r_cap, _ = _moe_capacity(n_tok)
    tm = TM_PROJ
    grid_spec = pltpu.PrefetchScalarGridSpec(
        num_scalar_prefetch=3,
        grid=(n_tok // tm,),
        in_specs=[
            pl.BlockSpec((tm, D_MODEL), lambda i, *_: (i, 0)),
            pl.BlockSpec((tm, LANES), lambda i, *_: (i, 0)),
            pl.BlockSpec((SUBLANES, tm), lambda i, *_: (0, i)),
        ],
        out_specs=pl.BlockSpec(memory_space=pl.ANY),
        scratch_shapes=[
            pltpu.VMEM((2, L_CAP, D_XS), jnp.bfloat16),
            pltpu.VMEM((SEG_ALIGN, D_XS), jnp.bfloat16),
            pltpu.SemaphoreType.DMA((2,)),
            pltpu.SemaphoreType.DMA((1,)),
        ],
    )
    return pl.pallas_call(
        _dispatch_kernel,
        grid_spec=grid_spec,
        out_shape=jax.ShapeDtypeStruct((r_cap + DUMP_ROWS, D_XS), jnp.bfloat16),
        compiler_params=pltpu.CompilerParams(dimension_semantics=("arbitrary",),
                                             vmem_limit_bytes=VMEM_LIMIT),
        name="moe_dispatch",
    )(dst, nz, zdst, t, col, row)


def _moe_kernel(layer, blk_e_ref, nused_ref, nvalid_ref, next_e_ref, xs_ref, wg_hbm, wu_hbm, wd_hbm,
                ys_ref, wg_st, wu_st, wd_st, wg_bf, wu_bf, wd_bf, sem):
    b = pl.program_id(0)
    nv = nvalid_ref[b]
    e = blk_e_ref[b]
    new_expert = jnp.logical_or(b == 0, e != blk_e_ref[jnp.maximum(b - 1, 0)])

    def fetch(expert):
        return (pltpu.make_async_copy(wg_hbm.at[layer, expert], wg_st, sem.at[0]),
                pltpu.make_async_copy(wu_hbm.at[layer, expert], wu_st, sem.at[1]),
                pltpu.make_async_copy(wd_hbm.at[layer, expert], wd_st, sem.at[2]))

    @pl.when(b == 0)
    def _():
        for cp in fetch(e):
            cp.start()

    @pl.when(new_expert)
    def _():
        for cp in fetch(e):
            cp.wait()
        wg_bf[...] = _bf16(wg_st[...])
        wu_bf[...] = _bf16(wu_st[...])
        wd_bf[...] = _bf16(wd_st[...])

        @pl.when(next_e_ref[b] != e)
        def _():
            for cp in fetch(next_e_ref[b]):
                cp.start()

    def compute(rows):
        x = xs_ref[0:rows, 0:D_MODEL]
        y = None
        fw = D_FF_EXPERT // MOE_FF_SPLIT
        for c in range(MOE_FF_SPLIT):
            cols = slice(c * fw, (c + 1) * fw)
            a = jnp.dot(x, wg_bf[:, cols], preferred_element_type=jnp.float32)
            u = jnp.dot(x, wu_bf[:, cols], preferred_element_type=jnp.float32)
            h = a * jax.nn.sigmoid(a) * u
            yc = jnp.dot(_bf16(h), wd_bf[cols, :], preferred_element_type=jnp.float32)
            y = yc if y is None else y + yc
        sd = xs_ref[0:rows, D_MODEL:D_XS].astype(jnp.float32)
        e_blk = blk_e_ref[b].astype(jnp.float32)
        w0 = sd[:, COL_W0H:COL_W0H + 1] + sd[:, COL_W0H + 1:COL_W0H + 2] + sd[:, COL_W0H + 2:COL_W0H + 3]
        w1 = sd[:, COL_W1H:COL_W1H + 1] + sd[:, COL_W1H + 1:COL_W1H + 2] + sd[:, COL_W1H + 2:COL_W1H + 3]
        w = jnp.where(sd[:, COL_E0:COL_E0 + 1] == e_blk, w0, w1)
        ys_ref[0:rows, :] = _bf16(y * w)

    for k in range(1, MOE_BM // MOE_TAIL + 1):
        rows = k * MOE_TAIL

        @pl.when(jnp.logical_and(nv > rows - MOE_TAIL, nv <= rows))
        def _(rows=rows):
            compute(rows)
            if rows < MOE_BM:
                ys_ref[rows:, :] = jnp.zeros((MOE_BM - rows, D_MODEL), ys_ref.dtype)


def _moe(xs, blk_e, nused, nvalid, next_e, wg, wu, wd, layer):
    blk = lambda b, be, nu, nv, ne: (jnp.maximum(jnp.minimum(b, nu[0] - 1), 0), 0)
    r_cap = xs.shape[0] - DUMP_ROWS
    grid_spec = pltpu.PrefetchScalarGridSpec(
        num_scalar_prefetch=4,
        grid=(r_cap // MOE_BM,),
        in_specs=[
            pl.BlockSpec((MOE_BM, D_XS), blk),
            pl.BlockSpec(memory_space=pl.ANY),
            pl.BlockSpec(memory_space=pl.ANY),
            pl.BlockSpec(memory_space=pl.ANY),
        ],
        out_specs=pl.BlockSpec((MOE_BM, D_MODEL), blk),
        scratch_shapes=[
            pltpu.VMEM((D_MODEL, D_FF_EXPERT), jnp.float32),
            pltpu.VMEM((D_MODEL, D_FF_EXPERT), jnp.float32),
            pltpu.VMEM((D_FF_EXPERT, D_MODEL), jnp.float32),
            pltpu.VMEM((D_MODEL, D_FF_EXPERT), jnp.bfloat16),
            pltpu.VMEM((D_MODEL, D_FF_EXPERT), jnp.bfloat16),
            pltpu.VMEM((D_FF_EXPERT, D_MODEL), jnp.bfloat16),
            pltpu.SemaphoreType.DMA((3,)),
        ],
    )
    return pl.pallas_call(
        functools.partial(_moe_kernel, layer),
        grid_spec=grid_spec,
        out_shape=jax.ShapeDtypeStruct((r_cap, D_MODEL), jnp.bfloat16),
        compiler_params=pltpu.CompilerParams(dimension_semantics=("arbitrary",),
                                             vmem_limit_bytes=VMEM_LIMIT),
        name="moe_experts",
    )(blk_e, nused, nvalid, next_e, xs, wg, wu, wd)


def _combine_kernel(src_ref, x1_ref, col_ref, ys_hbm, out_ref, ybuf, sem):
    i = pl.program_id(0)
    n = pl.num_programs(0)
    p = i % COMBINE_BUFS
    tm = x1_ref.shape[0]

    def start_tile(tile, par):
        for j in range(N_CHUNK):
            s = pl.ds(pl.multiple_of(src_ref[tile, j], SEG_ALIGN), SEG_ALIGN)
            pltpu.make_async_copy(ys_hbm.at[s], ybuf.at[par, j * SEG_ALIGN:(j + 1) * SEG_ALIGN],
                                  sem.at[par]).start()

    def wait_buf(par):
        pltpu.make_async_copy(ys_hbm.at[pl.ds(0, L_CAP)], ybuf.at[par], sem.at[par]).wait()

    @pl.when(i == 0)
    def _():
        start_tile(0, 0)
        start_tile(jnp.minimum(1, n - 1), 1)

    wait_buf(p)
    start_tile(jnp.minimum(i + 2, n - 1), (i + 2) % COMBINE_BUFS)

    rows = COMBINE_ROWS
    for c in range(tm // rows):
        col = col_ref[c * rows:(c + 1) * rows, :]
        pos0 = col[:, ROW_POS0:ROW_POS0 + 1].astype(jnp.int32)
        pos1 = col[:, ROW_POS1:ROW_POS1 + 1].astype(jnp.int32)
        l = lax.broadcasted_iota(jnp.int32, (rows, L_CAP), 1)
        perm = jnp.logical_or(l == pos0, l == pos1).astype(jnp.bfloat16)
        y = jnp.dot(perm, ybuf[p], preferred_element_type=jnp.float32)
        out_ref[c * rows:(c + 1) * rows, :] = x1_ref[c * rows:(c + 1) * rows, :] + y

    @pl.when(i == n - 1)
    def _():
        wait_buf((i + 1) % COMBINE_BUFS)
        wait_buf((i + 2) % COMBINE_BUFS)


def _combine(x1, col, ys, src):
    n_tok = x1.shape[0]
    tm = TM_PROJ
    grid_spec = pltpu.PrefetchScalarGridSpec(
        num_scalar_prefetch=1,
        grid=(n_tok // tm,),
        in_specs=[
            pl.BlockSpec((tm, D_MODEL), lambda i, *_: (i, 0)),
            pl.BlockSpec((tm, LANES), lambda i, *_: (i, 0)),
            pl.BlockSpec(memory_space=pl.ANY),
        ],
        out_specs=pl.BlockSpec((tm, D_MODEL), lambda i, *_: (i, 0)),
        scratch_shapes=[
            pltpu.VMEM((COMBINE_BUFS, L_CAP, D_MODEL), jnp.bfloat16),
            pltpu.SemaphoreType.DMA((COMBINE_BUFS,)),
        ],
    )
    return pl.pallas_call(
        _combine_kernel,
        grid_spec=grid_spec,
        out_shape=jax.ShapeDtypeStruct((n_tok, D_MODEL), jnp.float32),
        compiler_params=pltpu.CompilerParams(dimension_semantics=("arbitrary",),
                                             vmem_limit_bytes=VMEM_LIMIT),
        name="moe_combine",
    )(src, x1, col, ys)


def _routing_tables(counts, n_blk_cap):
    r_cap = n_blk_cap * MOE_BM
    i32 = jnp.int32
    ca = ((counts + (SEG_ALIGN - 1)) // SEG_ALIGN) * SEG_ALIGN
    lend = jnp.cumsum(ca, axis=1)
    lstart = lend - ca
    tot = jnp.sum(ca, axis=0)
    region = ((tot + (MOE_BM - 1)) // MOE_BM) * MOE_BM
    rend = jnp.cumsum(region)
    base = rend - region
    gstart = base[None, :] + jnp.cumsum(ca, axis=0) - ca
    j16 = jnp.arange(N_CHUNK, dtype=i32) * SEG_ALIGN
    in_seg = jnp.logical_and(lstart[:, None, :] <= j16[None, :, None], j16[None, :, None] < lend[:, None, :])
    seg_row = (jnp.sum(jnp.where(in_seg, (gstart - lstart)[:, None, :], 0), axis=2) + j16[None, :]).astype(i32)
    used = j16[None, :] < lend[:, -1:]
    tile_par = (jnp.arange(counts.shape[0], dtype=i32) % 2)[:, None]
    park = r_cap + (tile_par * N_CHUNK + jnp.arange(N_CHUNK, dtype=i32)[None, :]) * SEG_ALIGN
    dst = jnp.where(used, seg_row, park).astype(i32)
    src = jnp.where(used, seg_row, 0).astype(i32)
    nused = (rend[-1] // MOE_BM).astype(i32)
    brow = jnp.minimum(jnp.arange(n_blk_cap, dtype=i32), nused - 1) * MOE_BM
    blk_e = jnp.minimum(jnp.sum(rend[None, :] <= brow[:, None], axis=1), N_EXPERTS - 1).astype(i32)
    zstart = (base + tot).astype(i32)
    nzc = ((region - tot) // SEG_ALIGN).astype(i32)
    blk_i = jnp.arange(n_blk_cap, dtype=i32)
    seg_end = jnp.sum(jnp.where(blk_e[:, None] == jnp.arange(N_EXPERTS)[None, :], (base + tot)[None, :], 0), axis=1)
    nvalid = jnp.where(blk_i < nused, jnp.clip(seg_end - blk_i * MOE_BM, 0, MOE_BM), 0).astype(i32)
    eid = jnp.arange(N_EXPERTS, dtype=i32)
    later = jnp.logical_and(eid[None, :] > eid[:, None], (region > 0)[None, :])
    nxt = jnp.min(jnp.where(later, eid[None, :], N_EXPERTS), axis=1)
    nxt = jnp.where(nxt < N_EXPERTS, nxt, eid)
    next_e = jnp.sum(jnp.where(blk_e[:, None] == eid[None, :], nxt[None, :], 0), axis=1).astype(i32)
    return dst, src, nzc, zstart, blk_e, nused[None], nvalid, next_e


def _t5_bucket_np(dist):
    max_exact = N_BUCKETS // 2
    d = np.maximum(dist, 1).astype(np.float32)
    large = max_exact + (np.log(d / max_exact) / math.log(MAX_DISTANCE / max_exact)
                         * (N_BUCKETS - max_exact)).astype(np.int32)
    large = np.minimum(large, N_BUCKETS - 1)
    return np.where(dist < max_exact, dist, large)


def _attention_bias(rel_bias):
    qi = np.arange(ATT_BLOCK)[:, None]
    kj = np.arange(2 * ATT_BLOCK)[None, :]
    dist = qi + ATT_BLOCK - kj
    in_window = (dist >= 0) & (dist < WINDOW)
    bucket = _t5_bucket_np(np.clip(dist, 0, WINDOW - 1))
    onehot = (bucket[None] == np.arange(N_BUCKETS)[:, None, None]).astype(np.float32)
    bias = jnp.einsum('nh,nqk->hqk', rel_bias.astype(jnp.float32), jnp.asarray(onehot),
                      precision=lax.Precision.HIGHEST)
    bias = jnp.where(jnp.asarray(in_window)[None], bias * LOG2E, NEG_BIG)
    no_prev = jnp.asarray(kj < ATT_BLOCK)[None]
    return jnp.stack([bias, jnp.where(no_prev, NEG_BIG, bias)])


def _block_diag_mean(width, block):
    idx = np.arange(width) // block
    return jnp.asarray((idx[:, None] == idx[None, :]).astype(np.float32) / block, dtype=jnp.bfloat16)


def kernel(x, rel_bias, norm_mix_g, w_in, q_norm_g, k_norm_g, attn_sink, conv_w, conv_b, gate_b,
           mlstm_norm_g, w_out, norm_ffn_g, w_router_group, b_router_group, w_router_expert,
           b_router_expert, w_gate, w_up, w_down):
    batch, seq_len, _ = x.shape
    n_tok = batch * seq_len
    assert seq_len % TM_OUT == 0 and seq_len % ML_KCHUNK == 0
    f32 = jnp.float32
    bias = _attention_bias(rel_bias)
    bdq = _block_diag_mean(ATT_WIDTH, ATT_HEAD_DIM)
    bdk = _block_diag_mean(LANES, ATT_HEAD_DIM)
    x2 = x.reshape(n_tok, D_MODEL)
    for l in range(DEPTH):
        qg = (jnp.tile(q_norm_g[l].astype(f32), ATT_HEADS) * (ATT_HEAD_DIM ** -0.5 * LOG2E))[None, :]
        kg = jnp.tile(k_norm_g[l].astype(f32), ATT_KV_HEADS)[None, :]
        gbt = gate_b[l].astype(f32)[:, None]
        qn, kn, vd, hm = _front(
            x2.reshape(batch, seq_len, D_MODEL), norm_mix_g[l][None, :], w_in, l, qg, kg, bdq, bdk,
            conv_w[l], conv_b[l][None, :], gbt, mlstm_norm_g[l][None, :])
        flat = lambda a: a.reshape(n_tok, a.shape[-1])
        att = _attention(flat(qn), flat(kn), flat(vd), bias, attn_sink[l].astype(f32) * LOG2E,
                         batch, seq_len)
        n_rt = 4 * SUBLANES
        wrt = jnp.pad(jnp.concatenate([w_router_expert[l], w_router_group[l]], axis=1).astype(f32).T,
                      ((0, n_rt - N_EXPERTS - N_GROUPS), (0, 0)))
        brt = jnp.pad(jnp.concatenate([b_router_expert[l], b_router_group[l]]).astype(f32),
                      (0, n_rt - N_EXPERTS - N_GROUPS))[:, None]
        wrt_hi = _bf16(wrt)
        wrt = jnp.concatenate([wrt_hi, _bf16(wrt - wrt_hi.astype(f32))], axis=0)
        x1, t, row, col, cnt = _out_proj_router(x2, att, hm.reshape(n_tok, ML_WIDTH), w_out, l,
                                                norm_ffn_g[l][None, :], wrt, brt)
        _, n_blk_cap = _moe_capacity(n_tok)
        dst, src, nz, zdst, blk_e, nused, nvalid, next_e = _routing_tables(cnt[:, :, 0], n_blk_cap)
        xs = _dispatch(t, col, row, dst, nz, zdst)
        ys = _moe(xs, blk_e, nused, nvalid, next_e, w_gate, w_up, w_down, l)
        x2 = _combine(x1, col, ys, src)
    return x2.reshape(batch, seq_len, D_MODEL)
```

```python
import functools
import math

import jax
import jax.numpy as jnp
import numpy as np
from jax import lax
from jax.experimental import pallas as pl
from jax.experimental.pallas import tpu as pltpu

D_MODEL = 1024
DEPTH = 2
ATT_HEADS = 8
ATT_KV_HEADS = 2
ATT_HEAD_DIM = 64
ATT_WIDTH = ATT_HEADS * ATT_HEAD_DIM
WINDOW = 128
ATT_BLOCK = 128
N_BUCKETS = 32
MAX_DISTANCE = 128
ML_HEADS = 4
ML_DQK = 64
ML_DV = 128
ML_WIDTH = ML_HEADS * ML_DV
ML_CHUNK = 64
CONV_K = 4
N_GROUPS = 4
EXPERTS_PER_GROUP = 4
N_EXPERTS = N_GROUPS * EXPERTS_PER_GROUP
D_FF_EXPERT = 512
EPS = 1e-6

LANES = 128
SUBLANES = 8
NEG_BIG = -1e30
LOG2E = math.log2(math.e)
VMEM_LIMIT = 48 * 1024 * 1024
FRONT_VMEM_LIMIT = 56 * 1024 * 1024

O_Q = 0
O_K = O_Q + ATT_WIDTH
O_V = O_K + ATT_KV_HEADS * ATT_HEAD_DIM
O_QM = O_V + ATT_KV_HEADS * ATT_HEAD_DIM
O_KM = O_QM + ML_HEADS * ML_DQK
O_VM = O_KM + ML_HEADS * ML_DQK
O_OM = O_VM + ML_WIDTH
O_G = O_OM + ML_WIDTH
N_IN = O_G + 2 * ML_HEADS

ATT_STEP_BLOCKS = 4
TM_OUT = 1024
TM_PROJ = 512
ML_KCHUNK = 256

SEG_ALIGN = 16
MOE_BM = 1024
MOE_TAIL = 256
MOE_FF_SPLIT = 2
L_CAP = 2 * TM_PROJ + N_EXPERTS * SEG_ALIGN
N_CHUNK = L_CAP // SEG_ALIGN
D_XS = D_MODEL + LANES
DUMP_ROWS = 2 * N_CHUNK * SEG_ALIGN
DISPATCH_ROWS = 256
COMBINE_ROWS = 256
COMBINE_BUFS = 3
ROW_POS0, ROW_POS1, ROW_W0, ROW_W1, ROW_E0, ROW_E1 = 0, 1, 2, 3, 4, 5
COL_W0H, COL_W1H, COL_E0, COL_E1 = 8, 11, 14, 15


def _moe_capacity(n_tok):
    n_tiles = n_tok // TM_PROJ
    rows = 2 * n_tok + n_tiles * N_EXPERTS * (SEG_ALIGN - 1) + N_EXPERTS * (MOE_BM - SEG_ALIGN)
    n_blk = -(-rows // MOE_BM)
    return n_blk * MOE_BM, n_blk


def _bf16(a):
    return a.astype(jnp.bfloat16)


def _split3(a):
    hi = _bf16(a)
    r1 = a - hi.astype(jnp.float32)
    mid = _bf16(r1)
    return hi, mid, _bf16(r1 - mid.astype(jnp.float32))


def _log_sigmoid(z):
    return jnp.minimum(z, 0.0) - jnp.log(1.0 + jnp.exp(-jnp.abs(z)))


def _front_kernel(x_ref, g_ref, w_ref, qg_ref, kg_ref, bdq_ref, bdk_ref, cw_ref, cb_ref,
                  gbt_ref, ng_ref,
                  qn_ref, kn_ref, vd_ref, hm_ref,
                  w_bf, wgt_ref, conv_scr, qkm_s, vm_s, om_s, gt_s, c_scr, m_scr):
    s = pl.program_id(0)
    batch = x_ref.shape[0]
    L = ML_KCHUNK
    wr = s % 2
    rd = 1 - wr

    n_chunks = pl.num_programs(0) - 1

    @pl.when(s == 0)
    def _():
        w_bf[...] = _bf16(w_ref[...])
        wgt_ref[...] = _bf16(w_ref[O_G:N_IN, :])
        conv_scr[:, 0:SUBLANES, :] = jnp.zeros((batch, SUBLANES, conv_scr.shape[2]), jnp.float32)
        c_scr[...] = jnp.zeros(c_scr.shape, jnp.float32)
        m_scr[...] = jnp.zeros(m_scr.shape, jnp.float32)

    @pl.when(s > 0)
    def _():
        conv_scr[:, 0:SUBLANES, :] = conv_scr[:, L:L + SUBLANES, :]

    def scan_pieces():
        return _mlstm_pieces(qkm_s.at[rd], vm_s.at[rd], om_s.at[rd], gt_s.at[rd], ng_ref, hm_ref,
                             c_scr, m_scr, batch)

    def proj_pieces():
        return _project_pieces(x_ref, g_ref, w_bf, wgt_ref, qg_ref, kg_ref, bdq_ref, bdk_ref, cw_ref,
                               cb_ref, gbt_ref, qn_ref, kn_ref, vd_ref, qkm_s.at[wr], vm_s.at[wr],
                               om_s.at[wr], gt_s.at[wr], conv_scr)

    @pl.when(s == 0)
    def _():
        norm_part, proj_parts = proj_pieces()
        hbs = [norm_part(b) for b in range(batch)]
        for part in proj_parts:
            for b in range(batch):
                part(b, hbs[b])

    @pl.when(jnp.logical_and(s > 0, s < n_chunks))
    def _():
        scan_prep, scan_gates, scan_head, scan_finish = scan_pieces()
        norm_part, proj_parts = proj_pieces()
        preps = [scan_prep(b) for b in range(batch)]
        hbs = [norm_part(b) for b in range(batch)]
        gates = [scan_gates(b, 0, preps[b]) for b in range(batch)]
        for stage in range(len(proj_parts)):
            if stage < ML_HEADS:
                nxt = []
                for b in range(batch):
                    scan_head(b, stage, gates[b])
                    if stage + 1 < ML_HEADS:
                        nxt.append(scan_gates(b, stage + 1, preps[b]))
                gates = nxt
            for b in range(batch):
                proj_parts[stage](b, hbs[b])
        scan_finish()

    @pl.when(s == n_chunks)
    def _():
        scan_prep, scan_gates, scan_head, scan_finish = scan_pieces()
        preps = [scan_prep(b) for b in range(batch)]
        gates = [scan_gates(b, 0, preps[b]) for b in range(batch)]
        for h in range(ML_HEADS):
            nxt = []
            for b in range(batch):
                scan_head(b, h, gates[b])
                if h + 1 < ML_HEADS:
                    nxt.append(scan_gates(b, h + 1, preps[b]))
            gates = nxt
        scan_finish()


def _project_pieces(x_ref, g_ref, w_bf, wgt_ref, qg_ref, kg_ref, bdq_ref, bdk_ref, cw_ref, cb_ref,
                    gbt_ref, qn_ref, kn_ref, vd_ref, qkm_ref, vm_ref, om_ref, gatet_ref, conv_scr):
    L = ML_KCHUNK
    conv_w = 2 * ML_HEADS * ML_DQK

    def proj(hb, c0, width):
        return lax.dot_general(hb, w_bf[c0:c0 + width, :], (((1,), (1,)), ((), ())),
                               preferred_element_type=jnp.float32)

    def norm_part(b):
        x = x_ref[b]
        hn = x * lax.rsqrt(jnp.mean(x * x, axis=-1, keepdims=True) + EPS) * g_ref[...]
        return _bf16(hn)

    def q_part(b, hb):
        q = proj(hb, O_Q, ATT_WIDTH)
        q_ms = jnp.dot(_bf16(q * q), bdq_ref[...], preferred_element_type=jnp.float32)
        qn_ref[b] = _bf16(q * lax.rsqrt(q_ms + EPS) * qg_ref[...])

    def kv_part(b, hb):
        kv = proj(hb, O_K, 2 * LANES)
        k = kv[:, 0:LANES]
        v = kv[:, LANES:2 * LANES]
        k_ms = jnp.dot(_bf16(k * k), bdk_ref[...], preferred_element_type=jnp.float32)
        kn = k * lax.rsqrt(k_ms + EPS) * kg_ref[...]
        low = lax.broadcasted_iota(jnp.int32, kn.shape, 1) < ATT_HEAD_DIM

        def dup_heads(a):
            swapped = pltpu.roll(a, ATT_HEAD_DIM, axis=1)
            return jnp.concatenate([jnp.where(low, a, swapped), jnp.where(low, swapped, a)], axis=1)

        kn_ref[b] = _bf16(dup_heads(kn))
        vd_ref[b] = _bf16(dup_heads(v))

    def conv_part(b, hb):
        qk = proj(hb, O_QM, conv_w)
        conv_scr[b, SUBLANES:SUBLANES + L, :] = qk
        y = qk * cw_ref[CONV_K - 1:CONV_K, :] + cb_ref[...]
        for j in range(CONV_K - 1):
            off = SUBLANES - (CONV_K - 1) + j
            y = y + conv_scr[b, off:off + L, :] * cw_ref[j:j + 1, :]
        y = y * jax.nn.sigmoid(y)
        lane = lax.broadcasted_iota(jnp.int32, y.shape, 1)
        y = jnp.where(lane >= ML_HEADS * ML_DQK, y * (ML_DQK ** -0.5), y)
        qkm_ref[b] = _bf16(y)

    def vm_part(b, hb):
        vm_ref[b] = _bf16(proj(hb, O_VM, ML_WIDTH))

    def om_part(b, hb):
        om_ref[b] = _bf16(jax.nn.sigmoid(proj(hb, O_OM, ML_WIDTH)))

        gt = lax.dot_general(wgt_ref[...], hb, (((1,), (1,)), ((), ())),
                             preferred_element_type=jnp.float32) + gbt_ref[...]
        grow = lax.broadcasted_iota(jnp.int32, gt.shape, 0)
        gatet_ref[b] = jnp.where(grow >= ML_HEADS, _log_sigmoid(gt), gt) * LOG2E

    return norm_part, (q_part, kv_part, conv_part, vm_part, om_part)


def _front(x3, g, w_in, layer, qg, kg, bdq, bdk, cw, cb, gbt, ng):
    batch, seq_len, _ = x3.shape
    L = ML_KCHUNK
    nc = seq_len // L
    fix = lambda s: (0, 0)
    cur = lambda s: (0, jnp.minimum(s, nc - 1), 0)
    prev = lambda s: (0, jnp.maximum(s - 1, 0), 0)
    kv_w = 2 * ATT_KV_HEADS * ATT_HEAD_DIM
    conv_w = 2 * ML_HEADS * ML_DQK
    return pl.pallas_call(
        _front_kernel,
        grid=(nc + 1,),
        in_specs=[
            pl.BlockSpec((batch, L, D_MODEL), cur),
            pl.BlockSpec((1, D_MODEL), fix),
            pl.BlockSpec((None, N_IN, D_MODEL), lambda s: (layer, 0, 0), pipeline_mode=pl.Buffered(1)),
            pl.BlockSpec((1, ATT_WIDTH), fix),
            pl.BlockSpec((1, LANES), fix),
            pl.BlockSpec((ATT_WIDTH, ATT_WIDTH), fix),
            pl.BlockSpec((LANES, LANES), fix),
            pl.BlockSpec((CONV_K, conv_w), fix),
            pl.BlockSpec((1, conv_w), fix),
            pl.BlockSpec((SUBLANES, 1), fix),
            pl.BlockSpec((1, ML_WIDTH), fix),
        ],
        out_specs=[
            pl.BlockSpec((batch, L, ATT_WIDTH), cur),
            pl.BlockSpec((batch, L, kv_w), cur),
            pl.BlockSpec((batch, L, kv_w), cur),
            pl.BlockSpec((batch, L, ML_WIDTH), prev),
        ],
        out_shape=(
            jax.ShapeDtypeStruct((batch, seq_len, ATT_WIDTH), jnp.bfloat16),
            jax.ShapeDtypeStruct((batch, seq_len, kv_w), jnp.bfloat16),
            jax.ShapeDtypeStruct((batch, seq_len, kv_w), jnp.bfloat16),
            jax.ShapeDtypeStruct((batch, seq_len, ML_WIDTH), jnp.bfloat16),
        ),
        scratch_shapes=[
            pltpu.VMEM((N_IN, D_MODEL), jnp.bfloat16),
            pltpu.VMEM((SUBLANES, D_MODEL), jnp.bfloat16),
            pltpu.VMEM((batch, L + 2 * SUBLANES, conv_w), jnp.float32),
            pltpu.VMEM((2, batch, L, conv_w), jnp.bfloat16),
            pltpu.VMEM((2, batch, L, ML_WIDTH), jnp.bfloat16),
            pltpu.VMEM((2, batch, L, ML_WIDTH), jnp.bfloat16),
            pltpu.VMEM((2, batch, SUBLANES, L), jnp.float32),
            pltpu.VMEM((batch * ML_HEADS // 2, 2 * ML_DQK, 2 * ML_DV), jnp.float32),
            pltpu.VMEM((batch * ML_HEADS, 1, LANES), jnp.float32),
        ],
        compiler_params=pltpu.CompilerParams(dimension_semantics=("arbitrary",),
                                             vmem_limit_bytes=FRONT_VMEM_LIMIT),
        name="proj_mlstm",
    )(x3, g, w_in, qg, kg, bdq, bdk, cw, cb, gbt, ng)


def _attn_kernel(sink_ref, q_ref, kp_ref, kc_ref, vp_ref, vc_ref, bias_ref, o_ref):
    i = pl.program_id(1)
    blk = ATT_BLOCK
    lane = lax.broadcasted_iota(jnp.int32, (blk, LANES), 1)
    low = lane < ATT_HEAD_DIM
    group = ATT_HEADS // ATT_KV_HEADS
    ones = jnp.ones((2 * blk, LANES), jnp.bfloat16)
    for j in range(ATT_STEP_BLOCKS):
        rows = slice(j * blk, (j + 1) * blk)
        first = (i == 0).astype(jnp.int32) if j == 0 else 0
        for kv in range(ATT_KV_HEADS):
            lanes = slice(kv * LANES, (kv + 1) * LANES)
            if j == 0:
                kcat = jnp.concatenate([kp_ref[:, lanes], kc_ref[0:blk, lanes]], axis=0)
                vrows = jnp.concatenate([vp_ref[:, lanes], vc_ref[0:blk, lanes]], axis=0)
            else:
                kcat = kc_ref[(j - 1) * blk:(j + 1) * blk, lanes]
                vrows = vc_ref[(j - 1) * blk:(j + 1) * blk, lanes]
            vcat = jnp.concatenate([vrows, ones], axis=1)
            heads = range(kv * group, (kv + 1) * group)
            qs = []
            for h in heads:
                qp = q_ref[rows, (h // 2) * LANES:(h // 2 + 1) * LANES]
                qs.append(jnp.where(low if h % 2 == 0 else jnp.logical_not(low), qp, jnp.zeros_like(qp)))
            s_all = lax.dot_general(jnp.concatenate(qs, axis=0), kcat, (((1,), (1,)), ((), ())),
                                    preferred_element_type=jnp.float32)
            ps, ms = [], []
            for n, h in enumerate(heads):
                logits = s_all[n * blk:(n + 1) * blk, :] + bias_ref[first, h]
                row_max = jnp.broadcast_to(jnp.max(logits, axis=-1, keepdims=True), (blk, LANES))
                m = jnp.maximum(row_max, sink_ref[h])
                ps.append(_bf16(jnp.exp2(logits - jnp.concatenate([m, m], axis=1))))
                ms.append(m)
            o_all = jnp.dot(jnp.concatenate(ps, axis=0), vcat, preferred_element_type=jnp.float32)
            outs = []
            for n, h in enumerate(heads):
                o = o_all[n * blk:(n + 1) * blk, :]
                den = o[:, LANES:2 * LANES] + jnp.exp2(sink_ref[h] - ms[n])
                outs.append(o[:, 0:LANES] / den)
            for n in range(0, group, 2):
                pair = (kv * group + n) // 2
                o_ref[rows, pair * LANES:(pair + 1) * LANES] = _bf16(
                    jnp.where(low, outs[n], outs[n + 1]))


def _attention(qn, kn, vd, bias, sink, batch, seq_len):
    step = ATT_STEP_BLOCKS * ATT_BLOCK
    ns = seq_len // step
    cur = lambda b, i, s: (b * ns + i, 0)
    prev = lambda b, i, s: ((b * ns + i) * ATT_STEP_BLOCKS - jnp.minimum(i, 1), 0)
    grid_spec = pltpu.PrefetchScalarGridSpec(
        num_scalar_prefetch=1,
        grid=(batch, ns),
        in_specs=[
            pl.BlockSpec((step, ATT_WIDTH), cur),
            pl.BlockSpec((ATT_BLOCK, 2 * LANES), prev),
            pl.BlockSpec((step, 2 * LANES), cur),
            pl.BlockSpec((ATT_BLOCK, 2 * LANES), prev),
            pl.BlockSpec((step, 2 * LANES), cur),
            pl.BlockSpec((2, ATT_HEADS, ATT_BLOCK, 2 * ATT_BLOCK), lambda b, i, s: (0, 0, 0, 0)),
        ],
        out_specs=pl.BlockSpec((step, ATT_WIDTH), cur),
    )
    return pl.pallas_call(
        _attn_kernel,
        grid_spec=grid_spec,
        out_shape=jax.ShapeDtypeStruct((batch * seq_len, ATT_WIDTH), jnp.bfloat16),
        compiler_params=pltpu.CompilerParams(dimension_semantics=("arbitrary", "arbitrary"),
                                             vmem_limit_bytes=VMEM_LIMIT),
        name="swa_attention",
    )(sink, qn, kn, kn, vd, vd, bias)


def _mlstm_pieces(qk_ref, v_ref, o_ref, gt_ref, ng_ref, out_ref, c_scr, m_scr, batch):
    L = ML_KCHUNK
    pairs = ML_HEADS // 2
    r_i = lax.broadcasted_iota(jnp.int32, (L, L), 0)
    c_i = lax.broadcasted_iota(jnp.int32, (L, L), 1)
    causal = c_i <= r_i
    tril_bf = causal.astype(jnp.bfloat16)
    triu_bf = (r_i <= c_i).astype(jnp.bfloat16)
    mean_dv = jnp.full((ML_DV, ML_DV), 1.0 / ML_DV, jnp.bfloat16)
    lane = lax.broadcasted_iota(jnp.int32, (L, LANES), 1)
    low = lane < ML_DQK
    ones_dv = jnp.ones((L, ML_DV), jnp.bfloat16)
    row2 = lax.broadcasted_iota(jnp.int32, (2 * ML_DQK, 2 * ML_DV), 0)

    def twice(a):
        return jnp.concatenate([a, a], axis=1)

    m_state = [m_scr[k] for k in range(batch * ML_HEADS)]
    c_state = [c_scr[k] for k in range(batch * pairs)]

    def batch_prep(b):
        gt = gt_ref[b]
        g = jnp.concatenate([gt, jnp.zeros((LANES - SUBLANES, L), jnp.float32)], axis=0).T
        bcols = sum(jnp.dot(tril_bf, part, preferred_element_type=jnp.float32) for part in _split3(g))
        brows = sum(jnp.dot(part, triu_bf, preferred_element_type=jnp.float32) for part in _split3(gt))
        return gt, g, bcols, brows

    carry = {}

    def gate_piece(b, h, prep):
        gt, g, bcols, brows = prep
        m_prev = m_state[b * ML_HEADS + h]
        bc = jnp.broadcast_to(bcols[:, ML_HEADS + h:ML_HEADS + h + 1], (L, LANES))
        lic = jnp.broadcast_to(g[:, h:h + 1], (L, LANES))
        br = brows[ML_HEADS + h:ML_HEADS + h + 1, :]
        lir = gt[h:h + 1, :]
        log_d = jnp.where(causal, twice(bc) - (br - lir), NEG_BIG)
        m_inter = bc + m_prev
        row_max = jnp.broadcast_to(jnp.max(log_d, axis=-1, keepdims=True), (L, LANES))
        m_row = jnp.maximum(m_inter, row_max)
        d = jnp.exp2(log_d - twice(m_row))
        inter = jnp.exp2(m_inter - m_row)
        b_last = bc[L - 1:L, :]
        log_w = b_last - bc + lic
        m_next = jnp.maximum(b_last + m_prev, jnp.max(log_w, axis=0, keepdims=True))
        w = jnp.exp2(log_w - m_next)
        decay = jnp.exp2(b_last + m_prev - m_next)
        m_state[b * ML_HEADS + h] = m_next
        return d, inter, jnp.exp2(-m_row), w, decay

    def head_piece(b, h, gates):
        d, inter, floor, w, decay = gates
        pair, sub = divmod(h, 2)
        sidx = b * pairs + pair
        qp = qk_ref[b, :, pair * LANES:(pair + 1) * LANES]
        kp = qk_ref[b, :, (pairs + pair) * LANES:(pairs + pair + 1) * LANES]
        c_pair = c_state[sidx]
        c_bf = _bf16(c_pair)
        sel = low if sub == 0 else jnp.logical_not(low)
        qm = jnp.where(sel, qp, jnp.zeros_like(qp))
        s = lax.dot_general(qm, kp, (((1,), (1,)), ((), ())),
                            preferred_element_type=jnp.float32) * d
        v_ext = jnp.concatenate([v_ref[b, :, h * ML_DV:(h + 1) * ML_DV], ones_dv],
                                axis=-1)
        num = twice(inter) * jnp.dot(qm, c_bf, preferred_element_type=jnp.float32) \
            + jnp.dot(_bf16(s), v_ext, preferred_element_type=jnp.float32)
        den = num[:, ML_DV:2 * ML_DV]
        hval = num[:, 0:ML_DV] / jnp.maximum(jnp.abs(den), floor)
        h_ms = jnp.dot(_bf16(hval * hval), mean_dv, preferred_element_type=jnp.float32)
        hn = hval * lax.rsqrt(h_ms + EPS)
        hn = hn * ng_ref[:, h * ML_DV:(h + 1) * ML_DV]
        out_ref[b, :, h * ML_DV:(h + 1) * ML_DV] = _bf16(
            hn * o_ref[b, :, h * ML_DV:(h + 1) * ML_DV].astype(jnp.float32))
        kw = _bf16(jnp.where(sel, kp.astype(jnp.float32) * w, 0.0))
        upd = lax.dot_general(kw, v_ext, (((0,), (0,)), ((), ())),
                              preferred_element_type=jnp.float32)
        if sub == 0:
            carry[sidx] = (decay, upd)
        else:
            decay0, upd0 = carry.pop(sidx)
            decay_rows = jnp.where(row2 < ML_DQK, twice(decay0), twice(decay))
            c_state[sidx] = decay_rows * c_pair + (upd0 + upd)

    def finish():
        for k in range(batch * ML_HEADS):
            m_scr[k] = m_state[k]
        for k in range(batch * pairs):
            c_scr[k] = c_state[k]

    return batch_prep, gate_piece, head_piece, finish


def _out_proj_router_kernel(x_ref, att_ref, hm_ref, wo_ref, g_ref, wrt_ref, brt_ref,
                            x1_ref, t_ref, row_ref, col_ref, cnt_ref, wo_bf):
    tm = x_ref.shape[0]
    tr = TM_PROJ
    hi = lax.Precision.HIGHEST

    @pl.when(pl.program_id(0) == 0)
    def _():
        wo_bf[...] = _bf16(wo_ref[...])

    logits_all = []
    x1_all = []
    for sb in range(tm // TM_PROJ):
        rs = slice(sb * TM_PROJ, (sb + 1) * TM_PROJ)
        mixed = jnp.concatenate([att_ref[rs, :], hm_ref[rs, :]], axis=1)
        x1_all.append(x_ref[rs, :] + jnp.dot(mixed, wo_bf[...], preferred_element_type=jnp.float32))
    for sb in range(tm // TM_PROJ):
        rs = slice(sb * TM_PROJ, (sb + 1) * TM_PROJ)
        x1 = x1_all[sb]
        x1_ref[rs, :] = x1
        tn = x1 * lax.rsqrt(jnp.mean(x1 * x1, axis=-1, keepdims=True) + EPS) * g_ref[...]
        tn_hi = _bf16(tn)
        t_ref[rs, :] = tn_hi
        tn_lo = _bf16(tn - tn_hi.astype(jnp.float32))
        nt = (((1,), (1,)), ((), ()))
        n_rt = wrt_ref.shape[0] // 2
        p_hi = lax.dot_general(wrt_ref[...], tn_hi, nt, preferred_element_type=jnp.float32)
        p_lo = lax.dot_general(wrt_ref[0:n_rt, :], tn_lo, nt, preferred_element_type=jnp.float32)
        logits_all.append(p_hi[0:n_rt, :] + p_hi[n_rt:, :] + p_lo + brt_ref[...])

    for sb in range(tm // TM_PROJ):
        rs = slice(sb * TM_PROJ, (sb + 1) * TM_PROJ)
        logits = logits_all[sb]
        el_all = logits[0:N_EXPERTS, :]
        gl = logits[N_EXPERTS:N_EXPERTS + SUBLANES, :]
        grow = lax.broadcasted_iota(jnp.int32, gl.shape, 0).astype(jnp.float32)
        gl = jnp.where(grow < N_GROUPS, gl, NEG_BIG)
        gmax = jnp.max(gl, axis=0, keepdims=True)
        grp = jnp.min(jnp.where(gl == gmax, grow, float(N_GROUPS)), axis=0, keepdims=True)
        p_grp = 1.0 / jnp.sum(jnp.exp(gl - gmax), axis=0, keepdims=True)
        erow = lax.broadcasted_iota(jnp.int32, el_all.shape, 0).astype(jnp.float32)
        egrp = jnp.floor(erow * (1.0 / EXPERTS_PER_GROUP))
        el = jnp.where(egrp == grp, el_all, NEG_BIG)
        e1 = jnp.max(el, axis=0, keepdims=True)
        i1 = jnp.min(jnp.where(el == e1, erow, float(N_EXPERTS)), axis=0, keepdims=True)
        el2 = jnp.where(erow == i1, NEG_BIG, el)
        e2 = jnp.max(el2, axis=0, keepdims=True)
        i2 = jnp.min(jnp.where(el2 == e2, erow, float(N_EXPERTS)), axis=0, keepdims=True)
        z2 = jnp.exp(e2 - e1)
        w1 = p_grp / (1.0 + z2)
        w2 = p_grp * z2 / (1.0 + z2)
        sel1 = erow == i1
        sel2 = erow == i2
        onehot = jnp.logical_or(sel1, sel2)
        t_r = lax.broadcasted_iota(jnp.int32, (tr, tr), 0)
        t_c = lax.broadcasted_iota(jnp.int32, (tr, tr), 1)
        before = (t_r < t_c).astype(jnp.bfloat16)
        rank = jnp.dot(onehot.astype(jnp.bfloat16), before, preferred_element_type=jnp.float32)
        cnt = jnp.sum(onehot.astype(jnp.float32), axis=1, keepdims=True)
        cnt_al = jnp.floor((cnt + (SEG_ALIGN - 1)) * (1.0 / SEG_ALIGN)) * SEG_ALIGN
        e_r = lax.broadcasted_iota(jnp.int32, (N_EXPERTS, N_EXPERTS), 0)
        e_c = lax.broadcasted_iota(jnp.int32, (N_EXPERTS, N_EXPERTS), 1)
        lstart = jnp.dot((e_c < e_r).astype(jnp.float32), jnp.broadcast_to(cnt_al, (N_EXPERTS, LANES)),
                         precision=hi, preferred_element_type=jnp.float32)[:, 0:1]
        slot = lstart + rank
        pos1 = jnp.sum(jnp.where(sel1, slot, 0.0), axis=0, keepdims=True)
        pos2 = jnp.sum(jnp.where(sel2, slot, 0.0), axis=0, keepdims=True)
        r8 = lax.broadcasted_iota(jnp.int32, (SUBLANES, tr), 0)
        info = jnp.where(r8 == ROW_POS0, pos1, jnp.where(r8 == ROW_POS1, pos2, jnp.where(
            r8 == ROW_W0, w1, jnp.where(r8 == ROW_W1, w2, jnp.where(
                r8 == ROW_E0, i1, jnp.where(r8 == ROW_E1, i2, 0.0))))))
        row_ref[:, rs] = info

        def split3(w):
            h = _bf16(w).astype(jnp.float32)
            m = _bf16(w - h).astype(jnp.float32)
            return h, m, _bf16(w - h - m).astype(jnp.float32)

        w1h, w1m, w1l = split3(w1)
        w2h, w2m, w2l = split3(w2)
        parts = jnp.where(r8 == 0, w1h, jnp.where(r8 == 1, w1m, jnp.where(r8 == 2, w1l, jnp.where(
            r8 == 3, w2h, jnp.where(r8 == 4, w2m, jnp.where(r8 == 5, w2l, jnp.where(r8 == 6, i1, i2)))))))
        col_ref[rs, :] = jnp.concatenate(
            [info, parts, jnp.zeros((LANES - 2 * SUBLANES, tr), jnp.float32)], axis=0).T
        cnt_ref[sb] = jnp.broadcast_to(cnt, (N_EXPERTS, LANES)).astype(jnp.int32)


def _out_proj_router(x2, att, hm, wo, layer, g, wrt, brt):
    t = x2.shape[0]
    tm = TM_OUT
    row = lambda i: (i, 0)
    fix = lambda i: (0, 0)
    return pl.pallas_call(
        _out_proj_router_kernel,
        grid=(t // tm,),
        in_specs=[
            pl.BlockSpec((tm, D_MODEL), row),
            pl.BlockSpec((tm, ATT_WIDTH), row),
            pl.BlockSpec((tm, ML_WIDTH), row),
            pl.BlockSpec((None, D_MODEL, D_MODEL), lambda i: (layer, 0, 0), pipeline_mode=pl.Buffered(1)),
            pl.BlockSpec((1, D_MODEL), fix),
            pl.BlockSpec((8 * SUBLANES, D_MODEL), fix),
            pl.BlockSpec((4 * SUBLANES, 1), fix),
        ],
        out_specs=[
            pl.BlockSpec((tm, D_MODEL), row),
            pl.BlockSpec((tm, D_MODEL), row),
            pl.BlockSpec((SUBLANES, tm), lambda i: (0, i)),
            pl.BlockSpec((tm, LANES), row),
            pl.BlockSpec((tm // TM_PROJ, N_EXPERTS, LANES), lambda i: (i, 0, 0)),
        ],
        out_shape=(
            jax.ShapeDtypeStruct((t, D_MODEL), jnp.float32),
            jax.ShapeDtypeStruct((t, D_MODEL), jnp.bfloat16),
            jax.ShapeDtypeStruct((SUBLANES, t), jnp.float32),
            jax.ShapeDtypeStruct((t, LANES), jnp.float32),
            jax.ShapeDtypeStruct((t // TM_PROJ, N_EXPERTS, LANES), jnp.int32),
        ),
        scratch_shapes=[pltpu.VMEM((D_MODEL, D_MODEL), jnp.bfloat16)],
        compiler_params=pltpu.CompilerParams(dimension_semantics=("arbitrary",),
                                             vmem_limit_bytes=VMEM_LIMIT),
        name="out_proj_router",
    )(x2, att, hm, wo, g, wrt, brt)


def _dispatch_kernel(dst_ref, nz_ref, zdst_ref, t_ref, col_ref, row_ref, xs_hbm,
                     xbuf, zx, sem_x, sem_z):
    i = pl.program_id(0)
    n = pl.num_programs(0)
    p = i % 2
    tm = t_ref.shape[0]

    def copy(tile, par, j):
        d = pl.ds(pl.multiple_of(dst_ref[tile, j], SEG_ALIGN), SEG_ALIGN)
        return pltpu.make_async_copy(xbuf.at[par, j * SEG_ALIGN:(j + 1) * SEG_ALIGN], xs_hbm.at[d],
                                     sem_x.at[par])

    def zero_copy(e, j):
        d = pl.ds(pl.multiple_of(zdst_ref[e] + j * SEG_ALIGN, SEG_ALIGN), SEG_ALIGN)
        return pltpu.make_async_copy(zx, xs_hbm.at[d], sem_z.at[0])

    def for_zero_chunks(fn):
        for e in range(N_EXPERTS):
            def body(j, c, e=e):
                fn(zero_copy(e, j))
                return c
            lax.fori_loop(0, nz_ref[e], body, 0)

    def wait_tile(par):
        pltpu.make_async_copy(xbuf.at[par], xs_hbm.at[pl.ds(0, L_CAP)], sem_x.at[par]).wait()

    @pl.when(i == 0)
    def _():
        zx[...] = jnp.zeros(zx.shape, zx.dtype)
        for_zero_chunks(lambda cp: cp.start())

    @pl.when(i >= 2)
    def _():
        wait_tile(p)

    pos0 = row_ref[ROW_POS0:ROW_POS0 + 1, :].astype(jnp.int32)
    pos1 = row_ref[ROW_POS1:ROW_POS1 + 1, :].astype(jnp.int32)
    t = t_ref[...]
    side = _bf16(col_ref[...])
    rows = DISPATCH_ROWS
    for c in range(L_CAP // rows):
        r = lax.broadcasted_iota(jnp.int32, (rows, tm), 0) + c * rows
        perm = jnp.logical_or(r == pos0, r == pos1).astype(jnp.bfloat16)
        xbuf[p, c * rows:(c + 1) * rows, 0:D_MODEL] = _bf16(
            jnp.dot(perm, t, preferred_element_type=jnp.float32))
        xbuf[p, c * rows:(c + 1) * rows, D_MODEL:D_XS] = _bf16(
            jnp.dot(perm, side, preferred_element_type=jnp.float32))
        for j in range(c * rows // SEG_ALIGN, (c + 1) * rows // SEG_ALIGN):
            copy(i, p, j).start()

    @pl.when(i == n - 1)
    def _():
        @pl.when(n >= 2)
        def _():
            wait_tile(1 - p)
        wait_tile(p)
        for_zero_chunks(lambda cp: cp.wait())


def _dispatch(t, col, row, dst, nz, zdst):
    n_tok = t.shape[0]
    r_cap, _ = _moe_capacity(n_tok)
    tm = TM_PROJ
    grid_spec = pltpu.PrefetchScalarGridSpec(
        num_scalar_prefetch=3,
        grid=(n_tok // tm,),
        in_specs=[
            pl.BlockSpec((tm, D_MODEL), lambda i, *_: (i, 0)),
            pl.BlockSpec((tm, LANES), lambda i, *_: (i, 0)),
            pl.BlockSpec((SUBLANES, tm), lambda i, *_: (0, i)),
        ],
        out_specs=pl.BlockSpec(memory_space=pl.ANY),
        scratch_shapes=[
            pltpu.VMEM((2, L_CAP, D_XS), jnp.bfloat16),
            pltpu.VMEM((SEG_ALIGN, D_XS), jnp.bfloat16),
            pltpu.SemaphoreType.DMA((2,)),
            pltpu.SemaphoreType.DMA((1,)),
        ],
    )
    return pl.pallas_call(
        _dispatch_kernel,
        grid_spec=grid_spec,
        out_shape=jax.ShapeDtypeStruct((r_cap + DUMP_ROWS, D_XS), jnp.bfloat16),
        compiler_params=pltpu.CompilerParams(dimension_semantics=("arbitrary",),
                                             vmem_limit_bytes=VMEM_LIMIT),
        name="moe_dispatch",
    )(dst, nz, zdst, t, col, row)


def _moe_kernel(layer, blk_e_ref, nused_ref, nvalid_ref, next_e_ref, xs_ref, wg_hbm, wu_hbm, wd_hbm,
                ys_ref, wg_st, wu_st, wd_st, wg_bf, wu_bf, wd_bf, sem):
    b = pl.program_id(0)
    nv = nvalid_ref[b]
    e = blk_e_ref[b]
    new_expert = jnp.logical_or(b == 0, e != blk_e_ref[jnp.maximum(b - 1, 0)])

    def fetch(expert):
        return (pltpu.make_async_copy(wg_hbm.at[layer, expert], wg_st, sem.at[0]),
                pltpu.make_async_copy(wu_hbm.at[layer, expert], wu_st, sem.at[1]),
                pltpu.make_async_copy(wd_hbm.at[layer, expert], wd_st, sem.at[2]))

    @pl.when(b == 0)
    def _():
        for cp in fetch(e):
            cp.start()

    @pl.when(new_expert)
    def _():
        for cp in fetch(e):
            cp.wait()
        wg_bf[...] = _bf16(wg_st[...])
        wu_bf[...] = _bf16(wu_st[...])
        wd_bf[...] = _bf16(wd_st[...])

        @pl.when(next_e_ref[b] != e)
        def _():
            for cp in fetch(next_e_ref[b]):
                cp.start()

    def compute(rows):
        x = xs_ref[0:rows, 0:D_MODEL]
        y = None
        fw = D_FF_EXPERT // MOE_FF_SPLIT
        for c in range(MOE_FF_SPLIT):
            cols = slice(c * fw, (c + 1) * fw)
            a = jnp.dot(x, wg_bf[:, cols], preferred_element_type=jnp.float32)
            u = jnp.dot(x, wu_bf[:, cols], preferred_element_type=jnp.float32)
            h = a * jax.nn.sigmoid(a) * u
            yc = jnp.dot(_bf16(h), wd_bf[cols, :], preferred_element_type=jnp.float32)
            y = yc if y is None else y + yc
        sd = xs_ref[0:rows, D_MODEL:D_XS].astype(jnp.float32)
        e_blk = blk_e_ref[b].astype(jnp.float32)
        w0 = sd[:, COL_W0H:COL_W0H + 1] + sd[:, COL_W0H + 1:COL_W0H + 2] + sd[:, COL_W0H + 2:COL_W0H + 3]
        w1 = sd[:, COL_W1H:COL_W1H + 1] + sd[:, COL_W1H + 1:COL_W1H + 2] + sd[:, COL_W1H + 2:COL_W1H + 3]
        w = jnp.where(sd[:, COL_E0:COL_E0 + 1] == e_blk, w0, w1)
        ys_ref[0:rows, :] = _bf16(y * w)

    for k in range(1, MOE_BM // MOE_TAIL + 1):
        rows = k * MOE_TAIL

        @pl.when(jnp.logical_and(nv > rows - MOE_TAIL, nv <= rows))
        def _(rows=rows):
            compute(rows)
            if rows < MOE_BM:
                ys_ref[rows:, :] = jnp.zeros((MOE_BM - rows, D_MODEL), ys_ref.dtype)


def _moe(xs, blk_e, nused, nvalid, next_e, wg, wu, wd, layer):
    blk = lambda b, be, nu, nv, ne: (jnp.maximum(jnp.minimum(b, nu[0] - 1), 0), 0)
    r_cap = xs.shape[0] - DUMP_ROWS
    grid_spec = pltpu.PrefetchScalarGridSpec(
        num_scalar_prefetch=4,
        grid=(r_cap // MOE_BM,),
        in_specs=[
            pl.BlockSpec((MOE_BM, D_XS), blk),
            pl.BlockSpec(memory_space=pl.ANY),
            pl.BlockSpec(memory_space=pl.ANY),
            pl.BlockSpec(memory_space=pl.ANY),
        ],
        out_specs=pl.BlockSpec((MOE_BM, D_MODEL), blk),
        scratch_shapes=[
            pltpu.VMEM((D_MODEL, D_FF_EXPERT), jnp.float32),
            pltpu.VMEM((D_MODEL, D_FF_EXPERT), jnp.float32),
            pltpu.VMEM((D_FF_EXPERT, D_MODEL), jnp.float32),
            pltpu.VMEM((D_MODEL, D_FF_EXPERT), jnp.bfloat16),
            pltpu.VMEM((D_MODEL, D_FF_EXPERT), jnp.bfloat16),
            pltpu.VMEM((D_FF_EXPERT, D_MODEL), jnp.bfloat16),
            pltpu.SemaphoreType.DMA((3,)),
        ],
    )
    return pl.pallas_call(
        functools.partial(_moe_kernel, layer),
        grid_spec=grid_spec,
        out_shape=jax.ShapeDtypeStruct((r_cap, D_MODEL), jnp.bfloat16),
        compiler_params=pltpu.CompilerParams(dimension_semantics=("arbitrary",),
                                             vmem_limit_bytes=VMEM_LIMIT),
        name="moe_experts",
    )(blk_e, nused, nvalid, next_e, xs, wg, wu, wd)


def _combine_kernel(src_ref, x1_ref, col_ref, ys_hbm, out_ref, ybuf, sem):
    i = pl.program_id(0)
    n = pl.num_programs(0)
    p = i % COMBINE_BUFS
    tm = x1_ref.shape[0]

    def start_tile(tile, par):
        for j in range(N_CHUNK):
            s = pl.ds(pl.multiple_of(src_ref[tile, j], SEG_ALIGN), SEG_ALIGN)
            pltpu.make_async_copy(ys_hbm.at[s], ybuf.at[par, j * SEG_ALIGN:(j + 1) * SEG_ALIGN],
                                  sem.at[par]).start()

    def wait_buf(par):
        pltpu.make_async_copy(ys_hbm.at[pl.ds(0, L_CAP)], ybuf.at[par], sem.at[par]).wait()

    @pl.when(i == 0)
    def _():
        start_tile(0, 0)
        start_tile(jnp.minimum(1, n - 1), 1)

    wait_buf(p)
    start_tile(jnp.minimum(i + 2, n - 1), (i + 2) % COMBINE_BUFS)

    rows = COMBINE_ROWS
    for c in range(tm // rows):
        col = col_ref[c * rows:(c + 1) * rows, :]
        pos0 = col[:, ROW_POS0:ROW_POS0 + 1].astype(jnp.int32)
        pos1 = col[:, ROW_POS1:ROW_POS1 + 1].astype(jnp.int32)
        l = lax.broadcasted_iota(jnp.int32, (rows, L_CAP), 1)
        perm = jnp.logical_or(l == pos0, l == pos1).astype(jnp.bfloat16)
        y = jnp.dot(perm, ybuf[p], preferred_element_type=jnp.float32)
        out_ref[c * rows:(c + 1) * rows, :] = x1_ref[c * rows:(c + 1) * rows, :] + y

    @pl.when(i == n - 1)
    def _():
        wait_buf((i + 1) % COMBINE_BUFS)
        wait_buf((i + 2) % COMBINE_BUFS)


def _combine(x1, col, ys, src):
    n_tok = x1.shape[0]
    tm = TM_PROJ
    grid_spec = pltpu.PrefetchScalarGridSpec(
        num_scalar_prefetch=1,
        grid=(n_tok // tm,),
        in_specs=[
            pl.BlockSpec((tm, D_MODEL), lambda i, *_: (i, 0)),
            pl.BlockSpec((tm, LANES), lambda i, *_: (i, 0)),
            pl.BlockSpec(memory_space=pl.ANY),
        ],
        out_specs=pl.BlockSpec((tm, D_MODEL), lambda i, *_: (i, 0)),
        scratch_shapes=[
            pltpu.VMEM((COMBINE_BUFS, L_CAP, D_MODEL), jnp.bfloat16),
            pltpu.SemaphoreType.DMA((COMBINE_BUFS,)),
        ],
    )
    return pl.pallas_call(
        _combine_kernel,
        grid_spec=grid_spec,
        out_shape=jax.ShapeDtypeStruct((n_tok, D_MODEL), jnp.float32),
        compiler_params=pltpu.CompilerParams(dimension_semantics=("arbitrary",),
                                             vmem_limit_bytes=VMEM_LIMIT),
        name="moe_combine",
    )(src, x1, col, ys)


def _routing_tables(counts, n_blk_cap):
    r_cap = n_blk_cap * MOE_BM
    i32 = jnp.int32
    ca = ((counts + (SEG_ALIGN - 1)) // SEG_ALIGN) * SEG_ALIGN
    lend = jnp.cumsum(ca, axis=1)
    lstart = lend - ca
    tot = jnp.sum(ca, axis=0)
    region = ((tot + (MOE_BM - 1)) // MOE_BM) * MOE_BM
    rend = jnp.cumsum(region)
    base = rend - region
    gstart = base[None, :] + jnp.cumsum(ca, axis=0) - ca
    j16 = jnp.arange(N_CHUNK, dtype=i32) * SEG_ALIGN
    in_seg = jnp.logical_and(lstart[:, None, :] <= j16[None, :, None], j16[None, :, None] < lend[:, None, :])
    seg_row = (jnp.sum(jnp.where(in_seg, (gstart - lstart)[:, None, :], 0), axis=2) + j16[None, :]).astype(i32)
    used = j16[None, :] < lend[:, -1:]
    tile_par = (jnp.arange(counts.shape[0], dtype=i32) % 2)[:, None]
    park = r_cap + (tile_par * N_CHUNK + jnp.arange(N_CHUNK, dtype=i32)[None, :]) * SEG_ALIGN
    dst = jnp.where(used, seg_row, park).astype(i32)
    src = jnp.where(used, seg_row, 0).astype(i32)
    nused = (rend[-1] // MOE_BM).astype(i32)
    brow = jnp.minimum(jnp.arange(n_blk_cap, dtype=i32), nused - 1) * MOE_BM
    blk_e = jnp.minimum(jnp.sum(rend[None, :] <= brow[:, None], axis=1), N_EXPERTS - 1).astype(i32)
    zstart = (base + tot).astype(i32)
    nzc = ((region - tot) // SEG_ALIGN).astype(i32)
    blk_i = jnp.arange(n_blk_cap, dtype=i32)
    seg_end = jnp.sum(jnp.where(blk_e[:, None] == jnp.arange(N_EXPERTS)[None, :], (base + tot)[None, :], 0), axis=1)
    nvalid = jnp.where(blk_i < nused, jnp.clip(seg_end - blk_i * MOE_BM, 0, MOE_BM), 0).astype(i32)
    eid = jnp.arange(N_EXPERTS, dtype=i32)
    later = jnp.logical_and(eid[None, :] > eid[:, None], (region > 0)[None, :])
    nxt = jnp.min(jnp.where(later, eid[None, :], N_EXPERTS), axis=1)
    nxt = jnp.where(nxt < N_EXPERTS, nxt, eid)
    next_e = jnp.sum(jnp.where(blk_e[:, None] == eid[None, :], nxt[None, :], 0), axis=1).astype(i32)
    return dst, src, nzc, zstart, blk_e, nused[None], nvalid, next_e


def _t5_bucket_np(dist):
    max_exact = N_BUCKETS // 2
    d = np.maximum(dist, 1).astype(np.float32)
    large = max_exact + (np.log(d / max_exact) / math.log(MAX_DISTANCE / max_exact)
                         * (N_BUCKETS - max_exact)).astype(np.int32)
    large = np.minimum(large, N_BUCKETS - 1)
    return np.where(dist < max_exact, dist, large)


def _attention_bias(rel_bias):
    qi = np.arange(ATT_BLOCK)[:, None]
    kj = np.arange(2 * ATT_BLOCK)[None, :]
    dist = qi + ATT_BLOCK - kj
    in_window = (dist >= 0) & (dist < WINDOW)
    bucket = _t5_bucket_np(np.clip(dist, 0, WINDOW - 1))
    onehot = (bucket[None] == np.arange(N_BUCKETS)[:, None, None]).astype(np.float32)
    bias = jnp.einsum('nh,nqk->hqk', rel_bias.astype(jnp.float32), jnp.asarray(onehot),
                      precision=lax.Precision.HIGHEST)
    bias = jnp.where(jnp.asarray(in_window)[None], bias * LOG2E, NEG_BIG)
    no_prev = jnp.asarray(kj < ATT_BLOCK)[None]
    return jnp.stack([bias, jnp.where(no_prev, NEG_BIG, bias)])


def _block_diag_mean(width, block):
    idx = np.arange(width) // block
    return jnp.asarray((idx[:, None] == idx[None, :]).astype(np.float32) / block, dtype=jnp.bfloat16)


def kernel(x, rel_bias, norm_mix_g, w_in, q_norm_g, k_norm_g, attn_sink, conv_w, conv_b, gate_b,
           mlstm_norm_g, w_out, norm_ffn_g, w_router_group, b_router_group, w_router_expert,
           b_router_expert, w_gate, w_up, w_down):
    batch, seq_len, _ = x.shape
    n_tok = batch * seq_len
    assert seq_len % TM_OUT == 0 and seq_len % ML_KCHUNK == 0
    f32 = jnp.float32
    bias = _attention_bias(rel_bias)
    bdq = _block_diag_mean(ATT_WIDTH, ATT_HEAD_DIM)
    bdk = _block_diag_mean(LANES, ATT_HEAD_DIM)
    x2 = x.reshape(n_tok, D_MODEL)
    w_in_t = jnp.swapaxes(w_in, 1, 2)
    for l in range(DEPTH):
        qg = (jnp.tile(q_norm_g[l].astype(f32), ATT_HEADS) * (ATT_HEAD_DIM ** -0.5 * LOG2E))[None, :]
        kg = jnp.tile(k_norm_g[l].astype(f32), ATT_KV_HEADS)[None, :]
        gbt = gate_b[l].astype(f32)[:, None]
        qn, kn, vd, hm = _front(
            x2.reshape(batch, seq_len, D_MODEL), norm_mix_g[l][None, :], w_in_t, l, qg, kg, bdq, bdk,
            conv_w[l], conv_b[l][None, :], gbt, mlstm_norm_g[l][None, :])
        flat = lambda a: a.reshape(n_tok, a.shape[-1])
        att = _attention(flat(qn), flat(kn), flat(vd), bias, attn_sink[l].astype(f32) * LOG2E,
                         batch, seq_len)
        n_rt = 4 * SUBLANES
        wrt = jnp.pad(jnp.concatenate([w_router_expert[l], w_router_group[l]], axis=1).astype(f32).T,
                      ((0, n_rt - N_EXPERTS - N_GROUPS), (0, 0)))
        brt = jnp.pad(jnp.concatenate([b_router_expert[l], b_router_group[l]]).astype(f32),
                      (0, n_rt - N_EXPERTS - N_GROUPS))[:, None]
        wrt_hi = _bf16(wrt)
        wrt = jnp.concatenate([wrt_hi, _bf16(wrt - wrt_hi.astype(f32))], axis=0)
        x1, t, row, col, cnt = _out_proj_router(x2, att, hm.reshape(n_tok, ML_WIDTH), w_out, l,
                                                norm_ffn_g[l][None, :], wrt, brt)
        _, n_blk_cap = _moe_capacity(n_tok)
        dst, src, nz, zdst, blk_e, nused, nvalid, next_e = _routing_tables(cnt[:, :, 0], n_blk_cap)
        xs = _dispatch(t, col, row, dst, nz, zdst)
        ys = _moe(xs, blk_e, nused, nvalid, next_e, w_gate, w_up, w_down, l)
        x2 = _combine(x1, col, ys, src)
    return x2.reshape(batch, seq_len, D_MODEL)
```

```python
import functools
import math

import jax
import jax.numpy as jnp
import numpy as np
from jax import lax
from jax.experimental import pallas as pl
from jax.experimental.pallas import tpu as pltpu

D_MODEL = 1024
DEPTH = 2
ATT_HEADS = 8
ATT_KV_HEADS = 2
ATT_HEAD_DIM = 64
ATT_WIDTH = ATT_HEADS * ATT_HEAD_DIM
WINDOW = 128
ATT_BLOCK = 128
N_BUCKETS = 32
MAX_DISTANCE = 128
ML_HEADS = 4
ML_DQK = 64
ML_DV = 128
ML_WIDTH = ML_HEADS * ML_DV
ML_CHUNK = 64
CONV_K = 4
N_GROUPS = 4
EXPERTS_PER_GROUP = 4
N_EXPERTS = N_GROUPS * EXPERTS_PER_GROUP
D_FF_EXPERT = 512
EPS = 1e-6

LANES = 128
SUBLANES = 8
NEG_BIG = -1e30
LOG2E = math.log2(math.e)
VMEM_LIMIT = 48 * 1024 * 1024
FRONT_VMEM_LIMIT = 56 * 1024 * 1024

O_Q = 0
O_K = O_Q + ATT_WIDTH
O_V = O_K + ATT_KV_HEADS * ATT_HEAD_DIM
O_QM = O_V + ATT_KV_HEADS * ATT_HEAD_DIM
O_KM = O_QM + ML_HEADS * ML_DQK
O_VM = O_KM + ML_HEADS * ML_DQK
O_OM = O_VM + ML_WIDTH
O_G = O_OM + ML_WIDTH
N_IN = O_G + 2 * ML_HEADS

ATT_STEP_BLOCKS = 8
TM_OUT = 1024
TM_PROJ = 512
ML_KCHUNK = 256

SEG_ALIGN = 16
MOE_BM = 1024
MOE_TAIL = 256
MOE_FF_SPLIT = 2
L_CAP = 2 * TM_PROJ + N_EXPERTS * SEG_ALIGN
N_CHUNK = L_CAP // SEG_ALIGN
D_XS = D_MODEL + LANES
DUMP_ROWS = 2 * N_CHUNK * SEG_ALIGN
DISPATCH_ROWS = 256
COMBINE_ROWS = 256
COMBINE_BUFS = 3
ROW_POS0, ROW_POS1, ROW_W0, ROW_W1, ROW_E0, ROW_E1 = 0, 1, 2, 3, 4, 5
COL_W0H, COL_W1H, COL_E0, COL_E1 = 8, 11, 14, 15


def _moe_capacity(n_tok):
    n_tiles = n_tok // TM_PROJ
    rows = 2 * n_tok + n_tiles * N_EXPERTS * (SEG_ALIGN - 1) + N_EXPERTS * (MOE_BM - SEG_ALIGN)
    n_blk = -(-rows // MOE_BM)
    return n_blk * MOE_BM, n_blk


def _bf16(a):
    return a.astype(jnp.bfloat16)


def _split3(a):
    hi = _bf16(a)
    r1 = a - hi.astype(jnp.float32)
    mid = _bf16(r1)
    return hi, mid, _bf16(r1 - mid.astype(jnp.float32))


def _log_sigmoid(z):
    return jnp.minimum(z, 0.0) - jnp.log(1.0 + jnp.exp(-jnp.abs(z)))


def _front_kernel(x_ref, g_ref, w_ref, qg_ref, kg_ref, bdq_ref, bdk_ref, cw_ref, cb_ref,
                  gbt_ref, ng_ref,
                  qn_ref, kn_ref, vd_ref, hm_ref,
                  w_bf, wgt_ref, conv_scr, qkm_s, vm_s, om_s, gt_s, c_scr, m_scr):
    s = pl.program_id(0)
    batch = x_ref.shape[0]
    L = ML_KCHUNK
    wr = s % 2
    rd = 1 - wr

    n_chunks = pl.num_programs(0) - 1

    @pl.when(s == 0)
    def _():
        w_bf[...] = _bf16(w_ref[...])
        gate_cols = jnp.concatenate([w_ref[:, O_G:N_IN],
                                     jnp.zeros((D_MODEL, LANES - (N_IN - O_G)), jnp.float32)], axis=1)
        wgt_ref[...] = _bf16(gate_cols.T[0:SUBLANES, :])
        conv_scr[:, 0:SUBLANES, :] = jnp.zeros((batch, SUBLANES, conv_scr.shape[2]), jnp.float32)
        c_scr[...] = jnp.zeros(c_scr.shape, jnp.float32)
        m_scr[...] = jnp.zeros(m_scr.shape, jnp.float32)

    @pl.when(s > 0)
    def _():
        conv_scr[:, 0:SUBLANES, :] = conv_scr[:, L:L + SUBLANES, :]

    def scan_pieces():
        return _mlstm_pieces(qkm_s.at[rd], vm_s.at[rd], om_s.at[rd], gt_s.at[rd], ng_ref, hm_ref,
                             c_scr, m_scr, batch)

    def proj_pieces():
        return _project_pieces(x_ref, g_ref, w_bf, wgt_ref, qg_ref, kg_ref, bdq_ref, bdk_ref, cw_ref,
                               cb_ref, gbt_ref, qn_ref, kn_ref, vd_ref, qkm_s.at[wr], vm_s.at[wr],
                               om_s.at[wr], gt_s.at[wr], conv_scr)

    @pl.when(s == 0)
    def _():
        norm_part, proj_parts = proj_pieces()
        hbs = [norm_part(b) for b in range(batch)]
        for part in proj_parts:
            for b in range(batch):
                part(b, hbs[b])

    @pl.when(jnp.logical_and(s > 0, s < n_chunks))
    def _():
        scan_prep, scan_gates, scan_head, scan_finish = scan_pieces()
        norm_part, proj_parts = proj_pieces()
        preps = [scan_prep(b) for b in range(batch)]
        hbs = [norm_part(b) for b in range(batch)]
        gates = [scan_gates(b, 0, preps[b]) for b in range(batch)]
        for stage in range(len(proj_parts)):
            if stage < ML_HEADS:
                nxt = []
                for b in range(batch):
                    scan_head(b, stage, gates[b])
                    if stage + 1 < ML_HEADS:
                        nxt.append(scan_gates(b, stage + 1, preps[b]))
                gates = nxt
            for b in range(batch):
                proj_parts[stage](b, hbs[b])
        scan_finish()

    @pl.when(s == n_chunks)
    def _():
        scan_prep, scan_gates, scan_head, scan_finish = scan_pieces()
        preps = [scan_prep(b) for b in range(batch)]
        gates = [scan_gates(b, 0, preps[b]) for b in range(batch)]
        for h in range(ML_HEADS):
            nxt = []
            for b in range(batch):
                scan_head(b, h, gates[b])
                if h + 1 < ML_HEADS:
                    nxt.append(scan_gates(b, h + 1, preps[b]))
            gates = nxt
        scan_finish()


def _project_pieces(x_ref, g_ref, w_bf, wgt_ref, qg_ref, kg_ref, bdq_ref, bdk_ref, cw_ref, cb_ref,
                    gbt_ref, qn_ref, kn_ref, vd_ref, qkm_ref, vm_ref, om_ref, gatet_ref, conv_scr):
    L = ML_KCHUNK
    conv_w = 2 * ML_HEADS * ML_DQK

    def proj(hb, c0, width):
        return jnp.dot(hb, w_bf[:, c0:c0 + width], preferred_element_type=jnp.float32)

    def norm_part(b):
        x = x_ref[b]
        hn = x * lax.rsqrt(jnp.mean(x * x, axis=-1, keepdims=True) + EPS) * g_ref[...]
        return _bf16(hn)

    def q_part(b, hb):
        q = proj(hb, O_Q, ATT_WIDTH)
        q_ms = jnp.dot(_bf16(q * q), bdq_ref[...], preferred_element_type=jnp.float32)
        qn_ref[b] = _bf16(q * lax.rsqrt(q_ms + EPS) * qg_ref[...])

    def kv_part(b, hb):
        kv = proj(hb, O_K, 2 * LANES)
        k = kv[:, 0:LANES]
        v = kv[:, LANES:2 * LANES]
        k_ms = jnp.dot(_bf16(k * k), bdk_ref[...], preferred_element_type=jnp.float32)
        kn = k * lax.rsqrt(k_ms + EPS) * kg_ref[...]
        low = lax.broadcasted_iota(jnp.int32, kn.shape, 1) < ATT_HEAD_DIM

        def dup_heads(a):
            swapped = pltpu.roll(a, ATT_HEAD_DIM, axis=1)
            return jnp.concatenate([jnp.where(low, a, swapped), jnp.where(low, swapped, a)], axis=1)

        kn_ref[b] = _bf16(dup_heads(kn))
        vd_ref[b] = _bf16(dup_heads(v))

    def conv_part(b, hb):
        qk = proj(hb, O_QM, conv_w)
        conv_scr[b, SUBLANES:SUBLANES + L, :] = qk
        y = qk * cw_ref[CONV_K - 1:CONV_K, :] + cb_ref[...]
        for j in range(CONV_K - 1):
            off = SUBLANES - (CONV_K - 1) + j
            y = y + conv_scr[b, off:off + L, :] * cw_ref[j:j + 1, :]
        y = y * jax.nn.sigmoid(y)
        lane = lax.broadcasted_iota(jnp.int32, y.shape, 1)
        y = jnp.where(lane >= ML_HEADS * ML_DQK, y * (ML_DQK ** -0.5), y)
        qkm_ref[b] = _bf16(y)

    def vm_part(b, hb):
        vm_ref[b] = _bf16(proj(hb, O_VM, ML_WIDTH))

    def om_part(b, hb):
        om_ref[b] = _bf16(jax.nn.sigmoid(proj(hb, O_OM, ML_WIDTH)))

        gt = lax.dot_general(wgt_ref[...], hb, (((1,), (1,)), ((), ())),
                             preferred_element_type=jnp.float32) + gbt_ref[...]
        grow = lax.broadcasted_iota(jnp.int32, gt.shape, 0)
        gatet_ref[b] = jnp.where(grow >= ML_HEADS, _log_sigmoid(gt), gt) * LOG2E

    return norm_part, (q_part, kv_part, conv_part, vm_part, om_part)


def _front(x3, g, w_in, layer, qg, kg, bdq, bdk, cw, cb, gbt, ng):
    batch, seq_len, _ = x3.shape
    L = ML_KCHUNK
    nc = seq_len // L
    fix = lambda s: (0, 0)
    cur = lambda s: (0, jnp.minimum(s, nc - 1), 0)
    prev = lambda s: (0, jnp.maximum(s - 1, 0), 0)
    kv_w = 2 * ATT_KV_HEADS * ATT_HEAD_DIM
    conv_w = 2 * ML_HEADS * ML_DQK
    return pl.pallas_call(
        _front_kernel,
        grid=(nc + 1,),
        in_specs=[
            pl.BlockSpec((batch, L, D_MODEL), cur),
            pl.BlockSpec((1, D_MODEL), fix),
            pl.BlockSpec((None, D_MODEL, N_IN), lambda s: (layer, 0, 0), pipeline_mode=pl.Buffered(1)),
            pl.BlockSpec((1, ATT_WIDTH), fix),
            pl.BlockSpec((1, LANES), fix),
            pl.BlockSpec((ATT_WIDTH, ATT_WIDTH), fix),
            pl.BlockSpec((LANES, LANES), fix),
            pl.BlockSpec((CONV_K, conv_w), fix),
            pl.BlockSpec((1, conv_w), fix),
            pl.BlockSpec((SUBLANES, 1), fix),
            pl.BlockSpec((1, ML_WIDTH), fix),
        ],
        out_specs=[
            pl.BlockSpec((batch, L, ATT_WIDTH), cur),
            pl.BlockSpec((batch, L, kv_w), cur),
            pl.BlockSpec((batch, L, kv_w), cur),
            pl.BlockSpec((batch, L, ML_WIDTH), prev),
        ],
        out_shape=(
            jax.ShapeDtypeStruct((batch, seq_len, ATT_WIDTH), jnp.bfloat16),
            jax.ShapeDtypeStruct((batch, seq_len, kv_w), jnp.bfloat16),
            jax.ShapeDtypeStruct((batch, seq_len, kv_w), jnp.bfloat16),
            jax.ShapeDtypeStruct((batch, seq_len, ML_WIDTH), jnp.bfloat16),
        ),
        scratch_shapes=[
            pltpu.VMEM((D_MODEL, N_IN), jnp.bfloat16),
            pltpu.VMEM((SUBLANES, D_MODEL), jnp.bfloat16),
            pltpu.VMEM((batch, L + 2 * SUBLANES, conv_w), jnp.float32),
            pltpu.VMEM((2, batch, L, conv_w), jnp.bfloat16),
            pltpu.VMEM((2, batch, L, ML_WIDTH), jnp.bfloat16),
            pltpu.VMEM((2, batch, L, ML_WIDTH), jnp.bfloat16),
            pltpu.VMEM((2, batch, SUBLANES, L), jnp.float32),
            pltpu.VMEM((batch * ML_HEADS // 2, 2 * ML_DQK, 2 * ML_DV), jnp.float32),
            pltpu.VMEM((batch * ML_HEADS, 1, LANES), jnp.float32),
        ],
        compiler_params=pltpu.CompilerParams(dimension_semantics=("arbitrary",),
                                             vmem_limit_bytes=FRONT_VMEM_LIMIT),
        name="proj_mlstm",
    )(x3, g, w_in, qg, kg, bdq, bdk, cw, cb, gbt, ng)


def _attn_kernel(sink_ref, q_ref, kp_ref, kc_ref, vp_ref, vc_ref, bias_ref, o_ref):
    i = pl.program_id(1)
    blk = ATT_BLOCK
    lane = lax.broadcasted_iota(jnp.int32, (blk, LANES), 1)
    low = lane < ATT_HEAD_DIM
    group = ATT_HEADS // ATT_KV_HEADS
    ones = jnp.ones((2 * blk, LANES), jnp.bfloat16)
    for j in range(ATT_STEP_BLOCKS):
        rows = slice(j * blk, (j + 1) * blk)
        first = (i == 0).astype(jnp.int32) if j == 0 else 0
        for kv in range(ATT_KV_HEADS):
            lanes = slice(kv * LANES, (kv + 1) * LANES)
            if j == 0:
                kcat = jnp.concatenate([kp_ref[:, lanes], kc_ref[0:blk, lanes]], axis=0)
                vrows = jnp.concatenate([vp_ref[:, lanes], vc_ref[0:blk, lanes]], axis=0)
            else:
                kcat = kc_ref[(j - 1) * blk:(j + 1) * blk, lanes]
                vrows = vc_ref[(j - 1) * blk:(j + 1) * blk, lanes]
            vcat = jnp.concatenate([vrows, ones], axis=1)
            heads = range(kv * group, (kv + 1) * group)
            qs = []
            for h in heads:
                qp = q_ref[rows, (h // 2) * LANES:(h // 2 + 1) * LANES]
                qs.append(jnp.where(low if h % 2 == 0 else jnp.logical_not(low), qp, jnp.zeros_like(qp)))
            s_all = lax.dot_general(jnp.concatenate(qs, axis=0), kcat, (((1,), (1,)), ((), ())),
                                    preferred_element_type=jnp.float32)
            ps, ms = [], []
            for n, h in enumerate(heads):
                logits = s_all[n * blk:(n + 1) * blk, :] + bias_ref[first, h]
                row_max = jnp.broadcast_to(jnp.max(logits, axis=-1, keepdims=True), (blk, LANES))
                m = jnp.maximum(row_max, sink_ref[h])
                ps.append(_bf16(jnp.exp2(logits - jnp.concatenate([m, m], axis=1))))
                ms.append(m)
            o_all = jnp.dot(jnp.concatenate(ps, axis=0), vcat, preferred_element_type=jnp.float32)
            outs = []
            for n, h in enumerate(heads):
                o = o_all[n * blk:(n + 1) * blk, :]
                den = o[:, LANES:2 * LANES] + jnp.exp2(sink_ref[h] - ms[n])
                outs.append(o[:, 0:LANES] / den)
            for n in range(0, group, 2):
                pair = (kv * group + n) // 2
                o_ref[rows, pair * LANES:(pair + 1) * LANES] = _bf16(
                    jnp.where(low, outs[n], outs[n + 1]))


def _attention(qn, kn, vd, bias, sink, batch, seq_len):
    step = ATT_STEP_BLOCKS * ATT_BLOCK
    ns = seq_len // step
    cur = lambda b, i, s: (b * ns + i, 0)
    prev = lambda b, i, s: ((b * ns + i) * ATT_STEP_BLOCKS - jnp.minimum(i, 1), 0)
    grid_spec = pltpu.PrefetchScalarGridSpec(
        num_scalar_prefetch=1,
        grid=(batch, ns),
        in_specs=[
            pl.BlockSpec((step, ATT_WIDTH), cur),
            pl.BlockSpec((ATT_BLOCK, 2 * LANES), prev),
            pl.BlockSpec((step, 2 * LANES), cur),
            pl.BlockSpec((ATT_BLOCK, 2 * LANES), prev),
            pl.BlockSpec((step, 2 * LANES), cur),
            pl.BlockSpec((2, ATT_HEADS, ATT_BLOCK, 2 * ATT_BLOCK), lambda b, i, s: (0, 0, 0, 0)),
        ],
        out_specs=pl.BlockSpec((step, ATT_WIDTH), cur),
    )
    return pl.pallas_call(
        _attn_kernel,
        grid_spec=grid_spec,
        out_shape=jax.ShapeDtypeStruct((batch * seq_len, ATT_WIDTH), jnp.bfloat16),
        compiler_params=pltpu.CompilerParams(dimension_semantics=("arbitrary", "arbitrary"),
                                             vmem_limit_bytes=VMEM_LIMIT),
        name="swa_attention",
    )(sink, qn, kn, kn, vd, vd, bias)


def _mlstm_pieces(qk_ref, v_ref, o_ref, gt_ref, ng_ref, out_ref, c_scr, m_scr, batch):
    L = ML_KCHUNK
    pairs = ML_HEADS // 2
    r_i = lax.broadcasted_iota(jnp.int32, (L, L), 0)
    c_i = lax.broadcasted_iota(jnp.int32, (L, L), 1)
    causal = c_i <= r_i
    tril_bf = causal.astype(jnp.bfloat16)
    triu_bf = (r_i <= c_i).astype(jnp.bfloat16)
    mean_dv = jnp.full((ML_DV, ML_DV), 1.0 / ML_DV, jnp.bfloat16)
    lane = lax.broadcasted_iota(jnp.int32, (L, LANES), 1)
    low = lane < ML_DQK
    ones_dv = jnp.ones((L, ML_DV), jnp.bfloat16)
    row2 = lax.broadcasted_iota(jnp.int32, (2 * ML_DQK, 2 * ML_DV), 0)

    def twice(a):
        return jnp.concatenate([a, a], axis=1)

    m_state = [m_scr[k] for k in range(batch * ML_HEADS)]
    c_state = [c_scr[k] for k in range(batch * pairs)]

    def batch_prep(b):
        gt = gt_ref[b]
        g = jnp.concatenate([gt, jnp.zeros((LANES - SUBLANES, L), jnp.float32)], axis=0).T
        bcols = sum(jnp.dot(tril_bf, part, preferred_element_type=jnp.float32) for part in _split3(g))
        brows = sum(jnp.dot(part, triu_bf, preferred_element_type=jnp.float32) for part in _split3(gt))
        return gt, g, bcols, brows

    carry = {}

    def gate_piece(b, h, prep):
        gt, g, bcols, brows = prep
        m_prev = m_state[b * ML_HEADS + h]
        bc = jnp.broadcast_to(bcols[:, ML_HEADS + h:ML_HEADS + h + 1], (L, LANES))
        lic = jnp.broadcast_to(g[:, h:h + 1], (L, LANES))
        br = brows[ML_HEADS + h:ML_HEADS + h + 1, :]
        lir = gt[h:h + 1, :]
        log_d = jnp.where(causal, twice(bc) - (br - lir), NEG_BIG)
        m_inter = bc + m_prev
        row_max = jnp.broadcast_to(jnp.max(log_d, axis=-1, keepdims=True), (L, LANES))
        m_row = jnp.maximum(m_inter, row_max)
        d = jnp.exp2(log_d - twice(m_row))
        inter = jnp.exp2(m_inter - m_row)
        b_last = bc[L - 1:L, :]
        log_w = b_last - bc + lic
        m_next = jnp.maximum(b_last + m_prev, jnp.max(log_w, axis=0, keepdims=True))
        w = jnp.exp2(log_w - m_next)
        decay = jnp.exp2(b_last + m_prev - m_next)
        m_state[b * ML_HEADS + h] = m_next
        return d, inter, jnp.exp2(-m_row), w, decay

    def head_piece(b, h, gates):
        d, inter, floor, w, decay = gates
        pair, sub = divmod(h, 2)
        sidx = b * pairs + pair
        qp = qk_ref[b, :, pair * LANES:(pair + 1) * LANES]
        kp = qk_ref[b, :, (pairs + pair) * LANES:(pairs + pair + 1) * LANES]
        c_pair = c_state[sidx]
        c_bf = _bf16(c_pair)
        sel = low if sub == 0 else jnp.logical_not(low)
        qm = jnp.where(sel, qp, jnp.zeros_like(qp))
        s = lax.dot_general(qm, kp, (((1,), (1,)), ((), ())),
                            preferred_element_type=jnp.float32) * d
        v_ext = jnp.concatenate([v_ref[b, :, h * ML_DV:(h + 1) * ML_DV], ones_dv],
                                axis=-1)
        num = twice(inter) * jnp.dot(qm, c_bf, preferred_element_type=jnp.float32) \
            + jnp.dot(_bf16(s), v_ext, preferred_element_type=jnp.float32)
        den = num[:, ML_DV:2 * ML_DV]
        hval = num[:, 0:ML_DV] / jnp.maximum(jnp.abs(den), floor)
        h_ms = jnp.dot(_bf16(hval * hval), mean_dv, preferred_element_type=jnp.float32)
        hn = hval * lax.rsqrt(h_ms + EPS)
        hn = hn * ng_ref[:, h * ML_DV:(h + 1) * ML_DV]
        out_ref[b, :, h * ML_DV:(h + 1) * ML_DV] = _bf16(
            hn * o_ref[b, :, h * ML_DV:(h + 1) * ML_DV].astype(jnp.float32))
        kw = _bf16(jnp.where(sel, kp.astype(jnp.float32) * w, 0.0))
        upd = lax.dot_general(kw, v_ext, (((0,), (0,)), ((), ())),
                              preferred_element_type=jnp.float32)
        if sub == 0:
            carry[sidx] = (decay, upd)
        else:
            decay0, upd0 = carry.pop(sidx)
            decay_rows = jnp.where(row2 < ML_DQK, twice(decay0), twice(decay))
            c_state[sidx] = decay_rows * c_pair + (upd0 + upd)

    def finish():
        for k in range(batch * ML_HEADS):
            m_scr[k] = m_state[k]
        for k in range(batch * pairs):
            c_scr[k] = c_state[k]

    return batch_prep, gate_piece, head_piece, finish


def _out_proj_router_kernel(x_ref, att_ref, hm_ref, wo_ref, g_ref, wrt_ref, brt_ref,
                            x1_ref, t_ref, row_ref, col_ref, cnt_ref, wo_bf):
    tm = x_ref.shape[0]
    tr = TM_PROJ
    hi = lax.Precision.HIGHEST

    @pl.when(pl.program_id(0) == 0)
    def _():
        wo_bf[...] = _bf16(wo_ref[...])

    logits_all = []
    x1_all = []
    for sb in range(tm // TM_PROJ):
        rs = slice(sb * TM_PROJ, (sb + 1) * TM_PROJ)
        mixed = jnp.concatenate([att_ref[rs, :], hm_ref[rs, :]], axis=1)
        x1_all.append(x_ref[rs, :] + jnp.dot(mixed, wo_bf[...], preferred_element_type=jnp.float32))
    for sb in range(tm // TM_PROJ):
        rs = slice(sb * TM_PROJ, (sb + 1) * TM_PROJ)
        x1 = x1_all[sb]
        x1_ref[rs, :] = x1
        tn = x1 * lax.rsqrt(jnp.mean(x1 * x1, axis=-1, keepdims=True) + EPS) * g_ref[...]
        tn_hi = _bf16(tn)
        t_ref[rs, :] = tn_hi
        tn_lo = _bf16(tn - tn_hi.astype(jnp.float32))
        nt = (((1,), (1,)), ((), ()))
        n_rt = wrt_ref.shape[0] // 2
        p_hi = lax.dot_general(wrt_ref[...], tn_hi, nt, preferred_element_type=jnp.float32)
        p_lo = lax.dot_general(wrt_ref[0:n_rt, :], tn_lo, nt, preferred_element_type=jnp.float32)
        logits_all.append(p_hi[0:n_rt, :] + p_hi[n_rt:, :] + p_lo + brt_ref[...])

    for sb in range(tm // TM_PROJ):
        rs = slice(sb * TM_PROJ, (sb + 1) * TM_PROJ)
        logits = logits_all[sb]
        el_all = logits[0:N_EXPERTS, :]
        gl = logits[N_EXPERTS:N_EXPERTS + SUBLANES, :]
        grow = lax.broadcasted_iota(jnp.int32, gl.shape, 0).astype(jnp.float32)
        gl = jnp.where(grow < N_GROUPS, gl, NEG_BIG)
        gmax = jnp.max(gl, axis=0, keepdims=True)
        grp = jnp.min(jnp.where(gl == gmax, grow, float(N_GROUPS)), axis=0, keepdims=True)
        p_grp = 1.0 / jnp.sum(jnp.exp(gl - gmax), axis=0, keepdims=True)
        erow = lax.broadcasted_iota(jnp.int32, el_all.shape, 0).astype(jnp.float32)
        egrp = jnp.floor(erow * (1.0 / EXPERTS_PER_GROUP))
        el = jnp.where(egrp == grp, el_all, NEG_BIG)
        e1 = jnp.max(el, axis=0, keepdims=True)
        i1 = jnp.min(jnp.where(el == e1, erow, float(N_EXPERTS)), axis=0, keepdims=True)
        el2 = jnp.where(erow == i1, NEG_BIG, el)
        e2 = jnp.max(el2, axis=0, keepdims=True)
        i2 = jnp.min(jnp.where(el2 == e2, erow, float(N_EXPERTS)), axis=0, keepdims=True)
        z2 = jnp.exp(e2 - e1)
        w1 = p_grp / (1.0 + z2)
        w2 = p_grp * z2 / (1.0 + z2)
        sel1 = erow == i1
        sel2 = erow == i2
        onehot = jnp.logical_or(sel1, sel2)
        t_r = lax.broadcasted_iota(jnp.int32, (tr, tr), 0)
        t_c = lax.broadcasted_iota(jnp.int32, (tr, tr), 1)
        before = (t_r < t_c).astype(jnp.bfloat16)
        rank = jnp.dot(onehot.astype(jnp.bfloat16), before, preferred_element_type=jnp.float32)
        cnt = jnp.sum(onehot.astype(jnp.float32), axis=1, keepdims=True)
        cnt_al = jnp.floor((cnt + (SEG_ALIGN - 1)) * (1.0 / SEG_ALIGN)) * SEG_ALIGN
        e_r = lax.broadcasted_iota(jnp.int32, (N_EXPERTS, N_EXPERTS), 0)
        e_c = lax.broadcasted_iota(jnp.int32, (N_EXPERTS, N_EXPERTS), 1)
        lstart = jnp.dot((e_c < e_r).astype(jnp.float32), jnp.broadcast_to(cnt_al, (N_EXPERTS, LANES)),
                         precision=hi, preferred_element_type=jnp.float32)[:, 0:1]
        slot = lstart + rank
        pos1 = jnp.sum(jnp.where(sel1, slot, 0.0), axis=0, keepdims=True)
        pos2 = jnp.sum(jnp.where(sel2, slot, 0.0), axis=0, keepdims=True)
        r8 = lax.broadcasted_iota(jnp.int32, (SUBLANES, tr), 0)
        info = jnp.where(r8 == ROW_POS0, pos1, jnp.where(r8 == ROW_POS1, pos2, jnp.where(
            r8 == ROW_W0, w1, jnp.where(r8 == ROW_W1, w2, jnp.where(
                r8 == ROW_E0, i1, jnp.where(r8 == ROW_E1, i2, 0.0))))))
        row_ref[:, rs] = info

        def split3(w):
            h = _bf16(w).astype(jnp.float32)
            m = _bf16(w - h).astype(jnp.float32)
            return h, m, _bf16(w - h - m).astype(jnp.float32)

        w1h, w1m, w1l = split3(w1)
        w2h, w2m, w2l = split3(w2)
        parts = jnp.where(r8 == 0, w1h, jnp.where(r8 == 1, w1m, jnp.where(r8 == 2, w1l, jnp.where(
            r8 == 3, w2h, jnp.where(r8 == 4, w2m, jnp.where(r8 == 5, w2l, jnp.where(r8 == 6, i1, i2)))))))
        col_ref[rs, :] = jnp.concatenate(
            [info, parts, jnp.zeros((LANES - 2 * SUBLANES, tr), jnp.float32)], axis=0).T
        cnt_ref[sb] = jnp.broadcast_to(cnt, (N_EXPERTS, LANES)).astype(jnp.int32)


def _out_proj_router(x2, att, hm, wo, layer, g, wrt, brt):
    t = x2.shape[0]
    tm = TM_OUT
    row = lambda i: (i, 0)
    fix = lambda i: (0, 0)
    return pl.pallas_call(
        _out_proj_router_kernel,
        grid=(t // tm,),
        in_specs=[
            pl.BlockSpec((tm, D_MODEL), row),
            pl.BlockSpec((tm, ATT_WIDTH), row),
            pl.BlockSpec((tm, ML_WIDTH), row),
            pl.BlockSpec((None, D_MODEL, D_MODEL), lambda i: (layer, 0, 0), pipeline_mode=pl.Buffered(1)),
            pl.BlockSpec((1, D_MODEL), fix),
            pl.BlockSpec((8 * SUBLANES, D_MODEL), fix),
            pl.BlockSpec((4 * SUBLANES, 1), fix),
        ],
        out_specs=[
            pl.BlockSpec((tm, D_MODEL), row),
            pl.BlockSpec((tm, D_MODEL), row),
            pl.BlockSpec((SUBLANES, tm), lambda i: (0, i)),
            pl.BlockSpec((tm, LANES), row),
            pl.BlockSpec((tm // TM_PROJ, N_EXPERTS, LANES), lambda i: (i, 0, 0)),
        ],
        out_shape=(
            jax.ShapeDtypeStruct((t, D_MODEL), jnp.float32),
            jax.ShapeDtypeStruct((t, D_MODEL), jnp.bfloat16),
            jax.ShapeDtypeStruct((SUBLANES, t), jnp.float32),
            jax.ShapeDtypeStruct((t, LANES), jnp.float32),
            jax.ShapeDtypeStruct((t // TM_PROJ, N_EXPERTS, LANES), jnp.int32),
        ),
        scratch_shapes=[pltpu.VMEM((D_MODEL, D_MODEL), jnp.bfloat16)],
        compiler_params=pltpu.CompilerParams(dimension_semantics=("arbitrary",),
                                             vmem_limit_bytes=VMEM_LIMIT),
        name="out_proj_router",
    )(x2, att, hm, wo, g, wrt, brt)


def _dispatch_kernel(dst_ref, nz_ref, zdst_ref, t_ref, col_ref, row_ref, xs_hbm,
                     xbuf, zx, sem_x, sem_z):
    i = pl.program_id(0)
    n = pl.num_programs(0)
    p = i % 2
    tm = t_ref.shape[0]

    def copy(tile, par, j):
        d = pl.ds(pl.multiple_of(dst_ref[tile, j], SEG_ALIGN), SEG_ALIGN)
        return pltpu.make_async_copy(xbuf.at[par, j * SEG_ALIGN:(j + 1) * SEG_ALIGN], xs_hbm.at[d],
                                     sem_x.at[par])

    def zero_copy(e, j):
        d = pl.ds(pl.multiple_of(zdst_ref[e] + j * SEG_ALIGN, SEG_ALIGN), SEG_ALIGN)
        return pltpu.make_async_copy(zx, xs_hbm.at[d], sem_z.at[0])

    def for_zero_chunks(fn):
        for e in range(N_EXPERTS):
            def body(j, c, e=e):
                fn(zero_copy(e, j))
                return c
            lax.fori_loop(0, nz_ref[e], body, 0)

    def wait_tile(par):
        pltpu.make_async_copy(xbuf.at[par], xs_hbm.at[pl.ds(0, L_CAP)], sem_x.at[par]).wait()

    @pl.when(i == 0)
    def _():
        zx[...] = jnp.zeros(zx.shape, zx.dtype)
        for_zero_chunks(lambda cp: cp.start())

    @pl.when(i >= 2)
    def _():
        wait_tile(p)

    pos0 = row_ref[ROW_POS0:ROW_POS0 + 1, :].astype(jnp.int32)
    pos1 = row_ref[ROW_POS1:ROW_POS1 + 1, :].astype(jnp.int32)
    t = t_ref[...]
    side = _bf16(col_ref[...])
    rows = DISPATCH_ROWS
    for c in range(L_CAP // rows):
        r = lax.broadcasted_iota(jnp.int32, (rows, tm), 0) + c * rows
        perm = jnp.logical_or(r == pos0, r == pos1).astype(jnp.bfloat16)
        xbuf[p, c * rows:(c + 1) * rows, 0:D_MODEL] = _bf16(
            jnp.dot(perm, t, preferred_element_type=jnp.float32))
        xbuf[p, c * rows:(c + 1) * rows, D_MODEL:D_XS] = _bf16(
            jnp.dot(perm, side, preferred_element_type=jnp.float32))
        for j in range(c * rows // SEG_ALIGN, (c + 1) * rows // SEG_ALIGN):
            copy(i, p, j).start()

    @pl.when(i == n - 1)
    def _():
        @pl.when(n >= 2)
        def _():
            wait_tile(1 - p)
        wait_tile(p)
        for_zero_chunks(lambda cp: cp.wait())


def _dispatch(t, col, row, dst, nz, zdst):
    n_tok = t.shape[0]
    r_cap, _ = _moe_capacity(n_tok)
    tm = TM_PROJ
    grid_spec = pltpu.PrefetchScalarGridSpec(
        num_scalar_prefetch=3,
        grid=(n_tok // tm,),
        in_specs=[
            pl.BlockSpec((tm, D_MODEL), lambda i, *_: (i, 0)),
            pl.BlockSpec((tm, LANES), lambda i, *_: (i, 0)),
            pl.BlockSpec((SUBLANES, tm), lambda i, *_: (0, i)),
        ],
        out_specs=pl.BlockSpec(memory_space=pl.ANY),
        scratch_shapes=[
            pltpu.VMEM((2, L_CAP, D_XS), jnp.bfloat16),
            pltpu.VMEM((SEG_ALIGN, D_XS), jnp.bfloat16),
            pltpu.SemaphoreType.DMA((2,)),
            pltpu.SemaphoreType.DMA((1,)),
        ],
    )
    return pl.pallas_call(
        _dispatch_kernel,
        grid_spec=grid_spec,
        out_shape=jax.ShapeDtypeStruct((r_cap + DUMP_ROWS, D_XS), jnp.bfloat16),
        compiler_params=pltpu.CompilerParams(dimension_semantics=("arbitrary",),
                                             vmem_limit_bytes=VMEM_LIMIT),
        name="moe_dispatch",
    )(dst, nz, zdst, t, col, row)


def _moe_kernel(layer, blk_e_ref, nused_ref, nvalid_ref, next_e_ref, xs_ref, wg_hbm, wu_hbm, wd_hbm,
                ys_ref, wg_st, wu_st, wd_st, wg_bf, wu_bf, wd_bf, sem):
    b = pl.program_id(0)
    nv = nvalid_ref[b]
    e = blk_e_ref[b]
    new_expert = jnp.logical_or(b == 0, e != blk_e_ref[jnp.maximum(b - 1, 0)])

    def fetch(expert):
        return (pltpu.make_async_copy(wg_hbm.at[layer, expert], wg_st, sem.at[0]),
                pltpu.make_async_copy(wu_hbm.at[layer, expert], wu_st, sem.at[1]),
                pltpu.make_async_copy(wd_hbm.at[layer, expert], wd_st, sem.at[2]))

    @pl.when(b == 0)
    def _():
        for cp in fetch(e):
            cp.start()

    @pl.when(new_expert)
    def _():
        for cp in fetch(e):
            cp.wait()
        wg_bf[...] = _bf16(wg_st[...])
        wu_bf[...] = _bf16(wu_st[...])
        wd_bf[...] = _bf16(wd_st[...])

        @pl.when(next_e_ref[b] != e)
        def _():
            for cp in fetch(next_e_ref[b]):
                cp.start()

    def compute(rows):
        x = xs_ref[0:rows, 0:D_MODEL]
        y = None
        fw = D_FF_EXPERT // MOE_FF_SPLIT
        for c in range(MOE_FF_SPLIT):
            cols = slice(c * fw, (c + 1) * fw)
            a = jnp.dot(x, wg_bf[:, cols], preferred_element_type=jnp.float32)
            u = jnp.dot(x, wu_bf[:, cols], preferred_element_type=jnp.float32)
            h = a * jax.nn.sigmoid(a) * u
            yc = jnp.dot(_bf16(h), wd_bf[cols, :], preferred_element_type=jnp.float32)
            y = yc if y is None else y + yc
        sd = xs_ref[0:rows, D_MODEL:D_XS].astype(jnp.float32)
        e_blk = blk_e_ref[b].astype(jnp.float32)
        w0 = sd[:, COL_W0H:COL_W0H + 1] + sd[:, COL_W0H + 1:COL_W0H + 2] + sd[:, COL_W0H + 2:COL_W0H + 3]
        w1 = sd[:, COL_W1H:COL_W1H + 1] + sd[:, COL_W1H + 1:COL_W1H + 2] + sd[:, COL_W1H + 2:COL_W1H + 3]
        w = jnp.where(sd[:, COL_E0:COL_E0 + 1] == e_blk, w0, w1)
        ys_ref[0:rows, :] = _bf16(y * w)

    for k in range(1, MOE_BM // MOE_TAIL + 1):
        rows = k * MOE_TAIL

        @pl.when(jnp.logical_and(nv > rows - MOE_TAIL, nv <= rows))
        def _(rows=rows):
            compute(rows)
            if rows < MOE_BM:
                ys_ref[rows:, :] = jnp.zeros((MOE_BM - rows, D_MODEL), ys_ref.dtype)


def _moe(xs, blk_e, nused, nvalid, next_e, wg, wu, wd, layer):
    blk = lambda b, be, nu, nv, ne: (jnp.maximum(jnp.minimum(b, nu[0] - 1), 0), 0)
    r_cap = xs.shape[0] - DUMP_ROWS
    grid_spec = pltpu.PrefetchScalarGridSpec(
        num_scalar_prefetch=4,
        grid=(r_cap // MOE_BM,),
        in_specs=[
            pl.BlockSpec((MOE_BM, D_XS), blk),
            pl.BlockSpec(memory_space=pl.ANY),
            pl.BlockSpec(memory_space=pl.ANY),
            pl.BlockSpec(memory_space=pl.ANY),
        ],
        out_specs=pl.BlockSpec((MOE_BM, D_MODEL), blk),
        scratch_shapes=[
            pltpu.VMEM((D_MODEL, D_FF_EXPERT), jnp.float32),
            pltpu.VMEM((D_MODEL, D_FF_EXPERT), jnp.float32),
            pltpu.VMEM((D_FF_EXPERT, D_MODEL), jnp.float32),
            pltpu.VMEM((D_MODEL, D_FF_EXPERT), jnp.bfloat16),
            pltpu.VMEM((D_MODEL, D_FF_EXPERT), jnp.bfloat16),
            pltpu.VMEM((D_FF_EXPERT, D_MODEL), jnp.bfloat16),
            pltpu.SemaphoreType.DMA((3,)),
        ],
    )
    return pl.pallas_call(
        functools.partial(_moe_kernel, layer),
        grid_spec=grid_spec,
        out_shape=jax.ShapeDtypeStruct((r_cap, D_MODEL), jnp.bfloat16),
        compiler_params=pltpu.CompilerParams(dimension_semantics=("arbitrary",),
                                             vmem_limit_bytes=VMEM_LIMIT),
        name="moe_experts",
    )(blk_e, nused, nvalid, next_e, xs, wg, wu, wd)


def _combine_kernel(src_ref, x1_ref, col_ref, ys_hbm, out_ref, ybuf, sem):
    i = pl.program_id(0)
    n = pl.num_programs(0)
    p = i % COMBINE_BUFS
    tm = x1_ref.shape[0]

    def start_tile(tile, par):
        for j in range(N_CHUNK):
            s = pl.ds(pl.multiple_of(src_ref[tile, j], SEG_ALIGN), SEG_ALIGN)
            pltpu.make_async_copy(ys_hbm.at[s], ybuf.at[par, j * SEG_ALIGN:(j + 1) * SEG_ALIGN],
                                  sem.at[par]).start()

    def wait_buf(par):
        pltpu.make_async_copy(ys_hbm.at[pl.ds(0, L_CAP)], ybuf.at[par], sem.at[par]).wait()

    @pl.when(i == 0)
    def _():
        start_tile(0, 0)
        start_tile(jnp.minimum(1, n - 1), 1)

    wait_buf(p)
    start_tile(jnp.minimum(i + 2, n - 1), (i + 2) % COMBINE_BUFS)

    rows = COMBINE_ROWS
    for c in range(tm // rows):
        col = col_ref[c * rows:(c + 1) * rows, :]
        pos0 = col[:, ROW_POS0:ROW_POS0 + 1].astype(jnp.int32)
        pos1 = col[:, ROW_POS1:ROW_POS1 + 1].astype(jnp.int32)
        l = lax.broadcasted_iota(jnp.int32, (rows, L_CAP), 1)
        perm = jnp.logical_or(l == pos0, l == pos1).astype(jnp.bfloat16)
        y = jnp.dot(perm, ybuf[p], preferred_element_type=jnp.float32)
        out_ref[c * rows:(c + 1) * rows, :] = x1_ref[c * rows:(c + 1) * rows, :] + y

    @pl.when(i == n - 1)
    def _():
        wait_buf((i + 1) % COMBINE_BUFS)
        wait_buf((i + 2) % COMBINE_BUFS)


def _combine(x1, col, ys, src):
    n_tok = x1.shape[0]
    tm = TM_PROJ
    grid_spec = pltpu.PrefetchScalarGridSpec(
        num_scalar_prefetch=1,
        grid=(n_tok // tm,),
        in_specs=[
            pl.BlockSpec((tm, D_MODEL), lambda i, *_: (i, 0)),
            pl.BlockSpec((tm, LANES), lambda i, *_: (i, 0)),
            pl.BlockSpec(memory_space=pl.ANY),
        ],
        out_specs=pl.BlockSpec((tm, D_MODEL), lambda i, *_: (i, 0)),
        scratch_shapes=[
            pltpu.VMEM((COMBINE_BUFS, L_CAP, D_MODEL), jnp.bfloat16),
            pltpu.SemaphoreType.DMA((COMBINE_BUFS,)),
        ],
    )
    return pl.pallas_call(
        _combine_kernel,
        grid_spec=grid_spec,
        out_shape=jax.ShapeDtypeStruct((n_tok, D_MODEL), jnp.float32),
        compiler_params=pltpu.CompilerParams(dimension_semantics=("arbitrary",),
                                             vmem_limit_bytes=VMEM_LIMIT),
        name="moe_combine",
    )(src, x1, col, ys)


def _routing_tables(counts, n_blk_cap):
    r_cap = n_blk_cap * MOE_BM
    i32 = jnp.int32
    ca = ((counts + (SEG_ALIGN - 1)) // SEG_ALIGN) * SEG_ALIGN
    lend = jnp.cumsum(ca, axis=1)
    lstart = lend - ca
    tot = jnp.sum(ca, axis=0)
    region = ((tot + (MOE_BM - 1)) // MOE_BM) * MOE_BM
    rend = jnp.cumsum(region)
    base = rend - region
    gstart = base[None, :] + jnp.cumsum(ca, axis=0) - ca
    j16 = jnp.arange(N_CHUNK, dtype=i32) * SEG_ALIGN
    in_seg = jnp.logical_and(lstart[:, None, :] <= j16[None, :, None], j16[None, :, None] < lend[:, None, :])
    seg_row = (jnp.sum(jnp.where(in_seg, (gstart - lstart)[:, None, :], 0), axis=2) + j16[None, :]).astype(i32)
    used = j16[None, :] < lend[:, -1:]
    tile_par = (jnp.arange(counts.shape[0], dtype=i32) % 2)[:, None]
    park = r_cap + (tile_par * N_CHUNK + jnp.arange(N_CHUNK, dtype=i32)[None, :]) * SEG_ALIGN
    dst = jnp.where(used, seg_row, park).astype(i32)
    src = jnp.where(used, seg_row, 0).astype(i32)
    nused = (rend[-1] // MOE_BM).astype(i32)
    brow = jnp.minimum(jnp.arange(n_blk_cap, dtype=i32), nused - 1) * MOE_BM
    blk_e = jnp.minimum(jnp.sum(rend[None, :] <= brow[:, None], axis=1), N_EXPERTS - 1).astype(i32)
    zstart = (base + tot).astype(i32)
    nzc = ((((tot + (MOE_TAIL - 1)) // MOE_TAIL) * MOE_TAIL - tot) // SEG_ALIGN).astype(i32)
    blk_i = jnp.arange(n_blk_cap, dtype=i32)
    seg_end = jnp.sum(jnp.where(blk_e[:, None] == jnp.arange(N_EXPERTS)[None, :], (base + tot)[None, :], 0), axis=1)
    nvalid = jnp.where(blk_i < nused, jnp.clip(seg_end - blk_i * MOE_BM, 0, MOE_BM), 0).astype(i32)
    eid = jnp.arange(N_EXPERTS, dtype=i32)
    later = jnp.logical_and(eid[None, :] > eid[:, None], (region > 0)[None, :])
    nxt = jnp.min(jnp.where(later, eid[None, :], N_EXPERTS), axis=1)
    nxt = jnp.where(nxt < N_EXPERTS, nxt, eid)
    next_e = jnp.sum(jnp.where(blk_e[:, None] == eid[None, :], nxt[None, :], 0), axis=1).astype(i32)
    return dst, src, nzc, zstart, blk_e, nused[None], nvalid, next_e


def _t5_bucket_np(dist):
    max_exact = N_BUCKETS // 2
    d = np.maximum(dist, 1).astype(np.float32)
    large = max_exact + (np.log(d / max_exact) / math.log(MAX_DISTANCE / max_exact)
                         * (N_BUCKETS - max_exact)).astype(np.int32)
    large = np.minimum(large, N_BUCKETS - 1)
    return np.where(dist < max_exact, dist, large)


def _attention_bias(rel_bias):
    qi = np.arange(ATT_BLOCK)[:, None]
    kj = np.arange(2 * ATT_BLOCK)[None, :]
    dist = qi + ATT_BLOCK - kj
    in_window = (dist >= 0) & (dist < WINDOW)
    bucket = _t5_bucket_np(np.clip(dist, 0, WINDOW - 1))
    onehot = (bucket[None] == np.arange(N_BUCKETS)[:, None, None]).astype(np.float32)
    bias = jnp.einsum('nh,nqk->hqk', rel_bias.astype(jnp.float32), jnp.asarray(onehot),
                      precision=lax.Precision.HIGHEST)
    bias = jnp.where(jnp.asarray(in_window)[None], bias * LOG2E, NEG_BIG)
    no_prev = jnp.asarray(kj < ATT_BLOCK)[None]
    return jnp.stack([bias, jnp.where(no_prev, NEG_BIG, bias)])


def _block_diag_mean(width, block):
    idx = np.arange(width) // block
    return jnp.asarray((idx[:, None] == idx[None, :]).astype(np.float32) / block, dtype=jnp.bfloat16)


def kernel(x, rel_bias, norm_mix_g, w_in, q_norm_g, k_norm_g, attn_sink, conv_w, conv_b, gate_b,
           mlstm_norm_g, w_out, norm_ffn_g, w_router_group, b_router_group, w_router_expert,
           b_router_expert, w_gate, w_up, w_down):
    batch, seq_len, _ = x.shape
    n_tok = batch * seq_len
    assert seq_len % TM_OUT == 0 and seq_len % ML_KCHUNK == 0
    f32 = jnp.float32
    bias = _attention_bias(rel_bias)
    bdq = _block_diag_mean(ATT_WIDTH, ATT_HEAD_DIM)
    bdk = _block_diag_mean(LANES, ATT_HEAD_DIM)
    x2 = x.reshape(n_tok, D_MODEL)
    for l in range(DEPTH):
        qg = (jnp.tile(q_norm_g[l].astype(f32), ATT_HEADS) * (ATT_HEAD_DIM ** -0.5 * LOG2E))[None, :]
        kg = jnp.tile(k_norm_g[l].astype(f32), ATT_KV_HEADS)[None, :]
        gbt = gate_b[l].astype(f32)[:, None]
        qn, kn, vd, hm = _front(
            x2.reshape(batch, seq_len, D_MODEL), norm_mix_g[l][None, :], w_in, l, qg, kg, bdq, bdk,
            conv_w[l], conv_b[l][None, :], gbt, mlstm_norm_g[l][None, :])
        flat = lambda a: a.reshape(n_tok, a.shape[-1])
        att = _attention(flat(qn), flat(kn), flat(vd), bias, attn_sink[l].astype(f32) * LOG2E,
                         batch, seq_len)
        n_rt = 4 * SUBLANES
        wrt = jnp.pad(jnp.concatenate([w_router_expert[l], w_router_group[l]], axis=1).astype(f32).T,
                      ((0, n_rt - N_EXPERTS - N_GROUPS), (0, 0)))
        brt = jnp.pad(jnp.concatenate([b_router_expert[l], b_router_group[l]]).astype(f32),
                      (0, n_rt - N_EXPERTS - N_GROUPS))[:, None]
        wrt_hi = _bf16(wrt)
        wrt = jnp.concatenate([wrt_hi, _bf16(wrt - wrt_hi.astype(f32))], axis=0)
        x1, t, row, col, cnt = _out_proj_router(x2, att, hm.reshape(n_tok, ML_WIDTH), w_out, l,
                                                norm_ffn_g[l][None, :], wrt, brt)
        _, n_blk_cap = _moe_capacity(n_tok)
        dst, src, nz, zdst, blk_e, nused, nvalid, next_e = _routing_tables(cnt[:, :, 0], n_blk_cap)
        xs = _dispatch(t, col, row, dst, nz, zdst)
        ys = _moe(xs, blk_e, nused, nvalid, next_e, w_gate, w_up, w_down, l)
        x2 = _combine(x1, col, ys, src)
    return x2.reshape(batch, seq_len, D_MODEL)
```

```python
import functools
import math

import jax
import jax.numpy as jnp
import numpy as np
from jax import lax
from jax.experimental import pallas as pl
from jax.experimental.pallas import tpu as pltpu

D_MODEL = 1024
DEPTH = 2
ATT_HEADS = 8
ATT_KV_HEADS = 2
ATT_HEAD_DIM = 64
ATT_WIDTH = ATT_HEADS * ATT_HEAD_DIM
WINDOW = 128
ATT_BLOCK = 128
N_BUCKETS = 32
MAX_DISTANCE = 128
ML_HEADS = 4
ML_DQK = 64
ML_DV = 128
ML_WIDTH = ML_HEADS * ML_DV
ML_CHUNK = 64
CONV_K = 4
N_GROUPS = 4
EXPERTS_PER_GROUP = 4
N_EXPERTS = N_GROUPS * EXPERTS_PER_GROUP
D_FF_EXPERT = 512
EPS = 1e-6

LANES = 128
SUBLANES = 8
NEG_BIG = -1e30
LOG2E = math.log2(math.e)
VMEM_LIMIT = 48 * 1024 * 1024
FRONT_VMEM_LIMIT = 56 * 1024 * 1024

O_Q = 0
O_K = O_Q + ATT_WIDTH
O_V = O_K + ATT_KV_HEADS * ATT_HEAD_DIM
O_QM = O_V + ATT_KV_HEADS * ATT_HEAD_DIM
O_KM = O_QM + ML_HEADS * ML_DQK
O_VM = O_KM + ML_HEADS * ML_DQK
O_OM = O_VM + ML_WIDTH
O_G = O_OM + ML_WIDTH
N_IN = O_G + 2 * ML_HEADS

ATT_STEP_BLOCKS = 8
TM_OUT = 1024
TM_PROJ = 512
ML_KCHUNK = 256

SEG_ALIGN = 16
MOE_BM = 1024
MOE_TAIL = 256
MOE_FF_SPLIT = 2
L_CAP = 2 * TM_PROJ + N_EXPERTS * SEG_ALIGN
N_CHUNK = L_CAP // SEG_ALIGN
D_XS = D_MODEL + LANES
DUMP_ROWS = 2 * N_CHUNK * SEG_ALIGN
DISPATCH_ROWS = 256
COMBINE_ROWS = 256
COMBINE_BUFS = 3
ROW_POS0, ROW_POS1, ROW_W0, ROW_W1, ROW_E0, ROW_E1 = 0, 1, 2, 3, 4, 5
COL_W0H, COL_W1H, COL_E0, COL_E1 = 8, 11, 14, 15


def _moe_capacity(n_tok):
    n_tiles = n_tok // TM_PROJ
    rows = 2 * n_tok + n_tiles * N_EXPERTS * (SEG_ALIGN - 1) + N_EXPERTS * (MOE_BM - SEG_ALIGN)
    n_blk = -(-rows // MOE_BM)
    return n_blk * MOE_BM, n_blk


def _bf16(a):
    return a.astype(jnp.bfloat16)


def _split3(a):
    hi = _bf16(a)
    r1 = a - hi.astype(jnp.float32)
    mid = _bf16(r1)
    return hi, mid, _bf16(r1 - mid.astype(jnp.float32))


def _log_sigmoid(z):
    return jnp.minimum(z, 0.0) - jnp.log(1.0 + jnp.exp(-jnp.abs(z)))


def _front_kernel(x_ref, g_ref, w_ref, qg_ref, kg_ref, bdq_ref, bdk_ref, cw_ref, cb_ref,
                  gbt_ref, ng_ref,
                  qn_ref, kn_ref, vd_ref, hm_ref,
                  w_bf, wgt_ref, conv_scr, qkm_s, vm_s, om_s, gt_s, c_scr, m_scr):
    s = pl.program_id(0)
    batch = x_ref.shape[0]
    L = ML_KCHUNK
    wr = s % 2
    rd = 1 - wr

    n_chunks = pl.num_programs(0) - 1

    @pl.when(s == 0)
    def _():
        w_bf[...] = _bf16(w_ref[...])
        wgt_ref[...] = _bf16(w_ref[O_G:N_IN, :])
        conv_scr[:, 0:SUBLANES, :] = jnp.zeros((batch, SUBLANES, conv_scr.shape[2]), jnp.float32)
        c_scr[...] = jnp.zeros(c_scr.shape, jnp.float32)
        m_scr[...] = jnp.zeros(m_scr.shape, jnp.float32)

    @pl.when(s > 0)
    def _():
        conv_scr[:, 0:SUBLANES, :] = conv_scr[:, L:L + SUBLANES, :]

    def scan_pieces():
        return _mlstm_pieces(qkm_s.at[rd], vm_s.at[rd], om_s.at[rd], gt_s.at[rd], ng_ref, hm_ref,
                             c_scr, m_scr, batch)

    def proj_pieces():
        return _project_pieces(x_ref, g_ref, w_bf, wgt_ref, qg_ref, kg_ref, bdq_ref, bdk_ref, cw_ref,
                               cb_ref, gbt_ref, qn_ref, kn_ref, vd_ref, qkm_s.at[wr], vm_s.at[wr],
                               om_s.at[wr], gt_s.at[wr], conv_scr)

    @pl.when(s == 0)
    def _():
        norm_part, proj_parts = proj_pieces()
        hbs = [norm_part(b) for b in range(batch)]
        for part in proj_parts:
            for b in range(batch):
                part(b, hbs[b])

    @pl.when(jnp.logical_and(s > 0, s < n_chunks))
    def _():
        scan_prep, scan_gates, scan_head, scan_finish = scan_pieces()
        norm_part, proj_parts = proj_pieces()
        preps = [scan_prep(b) for b in range(batch)]
        hbs = [norm_part(b) for b in range(batch)]
        gates = [scan_gates(b, 0, preps[b]) for b in range(batch)]
        for stage in range(len(proj_parts)):
            if stage < ML_HEADS:
                nxt = []
                for b in range(batch):
                    scan_head(b, stage, gates[b])
                    if stage + 1 < ML_HEADS:
                        nxt.append(scan_gates(b, stage + 1, preps[b]))
                gates = nxt
            for b in range(batch):
                proj_parts[stage](b, hbs[b])
        scan_finish()

    @pl.when(s == n_chunks)
    def _():
        scan_prep, scan_gates, scan_head, scan_finish = scan_pieces()
        preps = [scan_prep(b) for b in range(batch)]
        gates = [scan_gates(b, 0, preps[b]) for b in range(batch)]
        for h in range(ML_HEADS):
            nxt = []
            for b in range(batch):
                scan_head(b, h, gates[b])
                if h + 1 < ML_HEADS:
                    nxt.append(scan_gates(b, h + 1, preps[b]))
            gates = nxt
        scan_finish()


def _project_pieces(x_ref, g_ref, w_bf, wgt_ref, qg_ref, kg_ref, bdq_ref, bdk_ref, cw_ref, cb_ref,
                    gbt_ref, qn_ref, kn_ref, vd_ref, qkm_ref, vm_ref, om_ref, gatet_ref, conv_scr):
    L = ML_KCHUNK
    conv_w = 2 * ML_HEADS * ML_DQK

    def proj(hb, c0, width):
        return lax.dot_general(hb, w_bf[c0:c0 + width, :], (((1,), (1,)), ((), ())),
                               preferred_element_type=jnp.float32)

    def norm_part(b):
        x = x_ref[b]
        hn = x * lax.rsqrt(jnp.mean(x * x, axis=-1, keepdims=True) + EPS) * g_ref[...]
        return _bf16(hn)

    def q_part(b, hb):
        q = proj(hb, O_Q, ATT_WIDTH)
        q_ms = jnp.dot(_bf16(q * q), bdq_ref[...], preferred_element_type=jnp.float32)
        qn_ref[b] = _bf16(q * lax.rsqrt(q_ms + EPS) * qg_ref[...])

    def kv_part(b, hb):
        kv = proj(hb, O_K, 2 * LANES)
        k = kv[:, 0:LANES]
        v = kv[:, LANES:2 * LANES]
        k_ms = jnp.dot(_bf16(k * k), bdk_ref[...], preferred_element_type=jnp.float32)
        kn = k * lax.rsqrt(k_ms + EPS) * kg_ref[...]
        low = lax.broadcasted_iota(jnp.int32, kn.shape, 1) < ATT_HEAD_DIM

        def dup_heads(a):
            swapped = pltpu.roll(a, ATT_HEAD_DIM, axis=1)
            return jnp.concatenate([jnp.where(low, a, swapped), jnp.where(low, swapped, a)], axis=1)

        kn_ref[b] = _bf16(dup_heads(kn))
        vd_ref[b] = _bf16(dup_heads(v))

    def conv_part(b, hb):
        qk = proj(hb, O_QM, conv_w)
        conv_scr[b, SUBLANES:SUBLANES + L, :] = qk
        y = qk * cw_ref[CONV_K - 1:CONV_K, :] + cb_ref[...]
        for j in range(CONV_K - 1):
            off = SUBLANES - (CONV_K - 1) + j
            y = y + conv_scr[b, off:off + L, :] * cw_ref[j:j + 1, :]
        y = y * jax.nn.sigmoid(y)
        lane = lax.broadcasted_iota(jnp.int32, y.shape, 1)
        y = jnp.where(lane >= ML_HEADS * ML_DQK, y * (ML_DQK ** -0.5), y)
        qkm_ref[b] = _bf16(y)

    def vm_part(b, hb):
        vm_ref[b] = _bf16(proj(hb, O_VM, ML_WIDTH))

    def om_part(b, hb):
        om_ref[b] = _bf16(jax.nn.sigmoid(proj(hb, O_OM, ML_WIDTH)))

        gt = lax.dot_general(wgt_ref[...], hb, (((1,), (1,)), ((), ())),
                             preferred_element_type=jnp.float32) + gbt_ref[...]
        grow = lax.broadcasted_iota(jnp.int32, gt.shape, 0)
        gatet_ref[b] = jnp.where(grow >= ML_HEADS, _log_sigmoid(gt), gt) * LOG2E

    return norm_part, (q_part, kv_part, conv_part, vm_part, om_part)


def _front(x3, g, w_in, layer, qg, kg, bdq, bdk, cw, cb, gbt, ng):
    batch, seq_len, _ = x3.shape
    L = ML_KCHUNK
    nc = seq_len // L
    fix = lambda s: (0, 0)
    cur = lambda s: (0, jnp.minimum(s, nc - 1), 0)
    prev = lambda s: (0, jnp.maximum(s - 1, 0), 0)
    kv_w = 2 * ATT_KV_HEADS * ATT_HEAD_DIM
    conv_w = 2 * ML_HEADS * ML_DQK
    return pl.pallas_call(
        _front_kernel,
        grid=(nc + 1,),
        in_specs=[
            pl.BlockSpec((batch, L, D_MODEL), cur),
            pl.BlockSpec((1, D_MODEL), fix),
            pl.BlockSpec((None, N_IN, D_MODEL), lambda s: (layer, 0, 0), pipeline_mode=pl.Buffered(1)),
            pl.BlockSpec((1, ATT_WIDTH), fix),
            pl.BlockSpec((1, LANES), fix),
            pl.BlockSpec((ATT_WIDTH, ATT_WIDTH), fix),
            pl.BlockSpec((LANES, LANES), fix),
            pl.BlockSpec((CONV_K, conv_w), fix),
            pl.BlockSpec((1, conv_w), fix),
            pl.BlockSpec((SUBLANES, 1), fix),
            pl.BlockSpec((1, ML_WIDTH), fix),
        ],
        out_specs=[
            pl.BlockSpec((batch, L, ATT_WIDTH), cur),
            pl.BlockSpec((batch, L, kv_w), cur),
            pl.BlockSpec((batch, L, kv_w), cur),
            pl.BlockSpec((batch, L, ML_WIDTH), prev),
        ],
        out_shape=(
            jax.ShapeDtypeStruct((batch, seq_len, ATT_WIDTH), jnp.bfloat16),
            jax.ShapeDtypeStruct((batch, seq_len, kv_w), jnp.bfloat16),
            jax.ShapeDtypeStruct((batch, seq_len, kv_w), jnp.bfloat16),
            jax.ShapeDtypeStruct((batch, seq_len, ML_WIDTH), jnp.bfloat16),
        ),
        scratch_shapes=[
            pltpu.VMEM((N_IN, D_MODEL), jnp.bfloat16),
            pltpu.VMEM((SUBLANES, D_MODEL), jnp.bfloat16),
            pltpu.VMEM((batch, L + 2 * SUBLANES, conv_w), jnp.float32),
            pltpu.VMEM((2, batch, L, conv_w), jnp.bfloat16),
            pltpu.VMEM((2, batch, L, ML_WIDTH), jnp.bfloat16),
            pltpu.VMEM((2, batch, L, ML_WIDTH), jnp.bfloat16),
            pltpu.VMEM((2, batch, SUBLANES, L), jnp.float32),
            pltpu.VMEM((batch * ML_HEADS // 2, 2 * ML_DQK, 2 * ML_DV), jnp.float32),
            pltpu.VMEM((batch * ML_HEADS, 1, LANES), jnp.float32),
        ],
        compiler_params=pltpu.CompilerParams(dimension_semantics=("arbitrary",),
                                             vmem_limit_bytes=FRONT_VMEM_LIMIT),
        name="proj_mlstm",
    )(x3, g, w_in, qg, kg, bdq, bdk, cw, cb, gbt, ng)


def _attn_kernel(sink_ref, q_ref, kp_ref, kc_ref, vp_ref, vc_ref, bias_ref, o_ref):
    i = pl.program_id(1)
    blk = ATT_BLOCK
    lane = lax.broadcasted_iota(jnp.int32, (blk, LANES), 1)
    low = lane < ATT_HEAD_DIM
    group = ATT_HEADS // ATT_KV_HEADS
    ones = jnp.ones((2 * blk, LANES), jnp.bfloat16)
    for j in range(ATT_STEP_BLOCKS):
        rows = slice(j * blk, (j + 1) * blk)
        first = (i == 0).astype(jnp.int32) if j == 0 else 0
        for kv in range(ATT_KV_HEADS):
            lanes = slice(kv * LANES, (kv + 1) * LANES)
            if j == 0:
                kcat = jnp.concatenate([kp_ref[:, lanes], kc_ref[0:blk, lanes]], axis=0)
                vrows = jnp.concatenate([vp_ref[:, lanes], vc_ref[0:blk, lanes]], axis=0)
            else:
                kcat = kc_ref[(j - 1) * blk:(j + 1) * blk, lanes]
                vrows = vc_ref[(j - 1) * blk:(j + 1) * blk, lanes]
            vcat = jnp.concatenate([vrows, ones], axis=1)
            heads = range(kv * group, (kv + 1) * group)
            qs = []
            for h in heads:
                qp = q_ref[rows, (h // 2) * LANES:(h // 2 + 1) * LANES]
                qs.append(jnp.where(low if h % 2 == 0 else jnp.logical_not(low), qp, jnp.zeros_like(qp)))
            s_all = lax.dot_general(jnp.concatenate(qs, axis=0), kcat, (((1,), (1,)), ((), ())),
                                    preferred_element_type=jnp.float32)
            ps, ms = [], []
            for n, h in enumerate(heads):
                logits = s_all[n * blk:(n + 1) * blk, :] + bias_ref[first, h]
                row_max = jnp.broadcast_to(jnp.max(logits, axis=-1, keepdims=True), (blk, LANES))
                m = jnp.maximum(row_max, sink_ref[h])
                ps.append(_bf16(jnp.exp2(logits - jnp.concatenate([m, m], axis=1))))
                ms.append(m)
            o_all = jnp.dot(jnp.concatenate(ps, axis=0), vcat, preferred_element_type=jnp.float32)
            outs = []
            for n, h in enumerate(heads):
                o = o_all[n * blk:(n + 1) * blk, :]
                den = o[:, LANES:2 * LANES] + jnp.exp2(sink_ref[h] - ms[n])
                outs.append(o[:, 0:LANES] / den)
            for n in range(0, group, 2):
                pair = (kv * group + n) // 2
                o_ref[rows, pair * LANES:(pair + 1) * LANES] = _bf16(
                    jnp.where(low, outs[n], outs[n + 1]))


def _attention(qn, kn, vd, bias, sink, batch, seq_len):
    step = ATT_STEP_BLOCKS * ATT_BLOCK
    ns = seq_len // step
    cur = lambda b, i, s: (b * ns + i, 0)
    prev = lambda b, i, s: ((b * ns + i) * ATT_STEP_BLOCKS - jnp.minimum(i, 1), 0)
    grid_spec = pltpu.PrefetchScalarGridSpec(
        num_scalar_prefetch=1,
        grid=(batch, ns),
        in_specs=[
            pl.BlockSpec((step, ATT_WIDTH), cur),
            pl.BlockSpec((ATT_BLOCK, 2 * LANES), prev),
            pl.BlockSpec((step, 2 * LANES), cur),
            pl.BlockSpec((ATT_BLOCK, 2 * LANES), prev),
            pl.BlockSpec((step, 2 * LANES), cur),
            pl.BlockSpec((2, ATT_HEADS, ATT_BLOCK, 2 * ATT_BLOCK), lambda b, i, s: (0, 0, 0, 0)),
        ],
        out_specs=pl.BlockSpec((step, ATT_WIDTH), cur),
    )
    return pl.pallas_call(
        _attn_kernel,
        grid_spec=grid_spec,
        out_shape=jax.ShapeDtypeStruct((batch * seq_len, ATT_WIDTH), jnp.bfloat16),
        compiler_params=pltpu.CompilerParams(dimension_semantics=("arbitrary", "arbitrary"),
                                             vmem_limit_bytes=VMEM_LIMIT),
        name="swa_attention",
    )(sink, qn, kn, kn, vd, vd, bias)


def _mlstm_pieces(qk_ref, v_ref, o_ref, gt_ref, ng_ref, out_ref, c_scr, m_scr, batch):
    L = ML_KCHUNK
    pairs = ML_HEADS // 2
    r_i = lax.broadcasted_iota(jnp.int32, (L, L), 0)
    c_i = lax.broadcasted_iota(jnp.int32, (L, L), 1)
    causal = c_i <= r_i
    tril_bf = causal.astype(jnp.bfloat16)
    triu_bf = (r_i <= c_i).astype(jnp.bfloat16)
    mean_dv = jnp.full((ML_DV, ML_DV), 1.0 / ML_DV, jnp.bfloat16)
    lane = lax.broadcasted_iota(jnp.int32, (L, LANES), 1)
    low = lane < ML_DQK
    ones_dv = jnp.ones((L, ML_DV), jnp.bfloat16)
    row2 = lax.broadcasted_iota(jnp.int32, (2 * ML_DQK, 2 * ML_DV), 0)

    def twice(a):
        return jnp.concatenate([a, a], axis=1)

    m_state = [m_scr[k] for k in range(batch * ML_HEADS)]
    c_state = [c_scr[k] for k in range(batch * pairs)]

    def batch_prep(b):
        gt = gt_ref[b]
        g = jnp.concatenate([gt, jnp.zeros((LANES - SUBLANES, L), jnp.float32)], axis=0).T
        bcols = sum(jnp.dot(tril_bf, part, preferred_element_type=jnp.float32) for part in _split3(g))
        brows = sum(jnp.dot(part, triu_bf, preferred_element_type=jnp.float32) for part in _split3(gt))
        return gt, g, bcols, brows

    carry = {}

    def gate_piece(b, h, prep):
        gt, g, bcols, brows = prep
        m_prev = m_state[b * ML_HEADS + h]
        bc = jnp.broadcast_to(bcols[:, ML_HEADS + h:ML_HEADS + h + 1], (L, LANES))
        lic = jnp.broadcast_to(g[:, h:h + 1], (L, LANES))
        br = brows[ML_HEADS + h:ML_HEADS + h + 1, :]
        lir = gt[h:h + 1, :]
        log_d = jnp.where(causal, twice(bc) - (br - lir), NEG_BIG)
        m_inter = bc + m_prev
        row_max = jnp.broadcast_to(jnp.max(log_d, axis=-1, keepdims=True), (L, LANES))
        m_row = jnp.maximum(m_inter, row_max)
        d = jnp.exp2(log_d - twice(m_row))
        inter = jnp.exp2(m_inter - m_row)
        b_last = bc[L - 1:L, :]
        log_w = b_last - bc + lic
        m_next = jnp.maximum(b_last + m_prev, jnp.max(log_w, axis=0, keepdims=True))
        w = jnp.exp2(log_w - m_next)
        decay = jnp.exp2(b_last + m_prev - m_next)
        m_state[b * ML_HEADS + h] = m_next
        return d, inter, jnp.exp2(-m_row), w, decay

    def head_piece(b, h, gates):
        d, inter, floor, w, decay = gates
        pair, sub = divmod(h, 2)
        sidx = b * pairs + pair
        qp = qk_ref[b, :, pair * LANES:(pair + 1) * LANES]
        kp = qk_ref[b, :, (pairs + pair) * LANES:(pairs + pair + 1) * LANES]
        c_pair = c_state[sidx]
        c_bf = _bf16(c_pair)
        sel = low if sub == 0 else jnp.logical_not(low)
        qm = jnp.where(sel, qp, jnp.zeros_like(qp))
        s = lax.dot_general(qm, kp, (((1,), (1,)), ((), ())),
                            preferred_element_type=jnp.float32) * d
        v_ext = jnp.concatenate([v_ref[b, :, h * ML_DV:(h + 1) * ML_DV], ones_dv],
                                axis=-1)
        num = twice(inter) * jnp.dot(qm, c_bf, preferred_element_type=jnp.float32) \
            + jnp.dot(_bf16(s), v_ext, preferred_element_type=jnp.float32)
        den = num[:, ML_DV:2 * ML_DV]
        hval = num[:, 0:ML_DV] / jnp.maximum(jnp.abs(den), floor)
        h_ms = jnp.dot(_bf16(hval * hval), mean_dv, preferred_element_type=jnp.float32)
        hn = hval * lax.rsqrt(h_ms + EPS)
        hn = hn * ng_ref[:, h * ML_DV:(h + 1) * ML_DV]
        out_ref[b, :, h * ML_DV:(h + 1) * ML_DV] = _bf16(
            hn * o_ref[b, :, h * ML_DV:(h + 1) * ML_DV].astype(jnp.float32))
        kw = _bf16(jnp.where(sel, kp.astype(jnp.float32) * w, 0.0))
        upd = lax.dot_general(kw, v_ext, (((0,), (0,)), ((), ())),
                              preferred_element_type=jnp.float32)
        if sub == 0:
            carry[sidx] = (decay, upd)
        else:
            decay0, upd0 = carry.pop(sidx)
            decay_rows = jnp.where(row2 < ML_DQK, twice(decay0), twice(decay))
            c_state[sidx] = decay_rows * c_pair + (upd0 + upd)

    def finish():
        for k in range(batch * ML_HEADS):
            m_scr[k] = m_state[k]
        for k in range(batch * pairs):
            c_scr[k] = c_state[k]

    return batch_prep, gate_piece, head_piece, finish


def _out_proj_router_kernel(x_ref, att_ref, hm_ref, wo_ref, g_ref, wrt_ref, brt_ref,
                            x1_ref, t_ref, row_ref, col_ref, cnt_ref, wo_bf):
    tm = x_ref.shape[0]
    tr = TM_PROJ
    hi = lax.Precision.HIGHEST

    @pl.when(pl.program_id(0) == 0)
    def _():
        wo_bf[...] = _bf16(wo_ref[...])

    logits_all = []
    x1_all = []
    for sb in range(tm // TM_PROJ):
        rs = slice(sb * TM_PROJ, (sb + 1) * TM_PROJ)
        mixed = jnp.concatenate([att_ref[rs, :], hm_ref[rs, :]], axis=1)
        x1_all.append(x_ref[rs, :] + jnp.dot(mixed, wo_bf[...], preferred_element_type=jnp.float32))
    for sb in range(tm // TM_PROJ):
        rs = slice(sb * TM_PROJ, (sb + 1) * TM_PROJ)
        x1 = x1_all[sb]
        x1_ref[rs, :] = x1
        tn = x1 * lax.rsqrt(jnp.mean(x1 * x1, axis=-1, keepdims=True) + EPS) * g_ref[...]
        tn_hi = _bf16(tn)
        t_ref[rs, :] = tn_hi
        tn_lo = _bf16(tn - tn_hi.astype(jnp.float32))
        nt = (((1,), (1,)), ((), ()))
        n_rt = wrt_ref.shape[0] // 2
        p_hi = lax.dot_general(wrt_ref[...], tn_hi, nt, preferred_element_type=jnp.float32)
        p_lo = lax.dot_general(wrt_ref[0:n_rt, :], tn_lo, nt, preferred_element_type=jnp.float32)
        logits_all.append(p_hi[0:n_rt, :] + p_hi[n_rt:, :] + p_lo + brt_ref[...])

    for sb in range(tm // TM_PROJ):
        rs = slice(sb * TM_PROJ, (sb + 1) * TM_PROJ)
        logits = logits_all[sb]
        el_all = logits[0:N_EXPERTS, :]
        gl = logits[N_EXPERTS:N_EXPERTS + SUBLANES, :]
        grow = lax.broadcasted_iota(jnp.int32, gl.shape, 0).astype(jnp.float32)
        gl = jnp.where(grow < N_GROUPS, gl, NEG_BIG)
        gmax = jnp.max(gl, axis=0, keepdims=True)
        grp = jnp.min(jnp.where(gl == gmax, grow, float(N_GROUPS)), axis=0, keepdims=True)
        p_grp = 1.0 / jnp.sum(jnp.exp(gl - gmax), axis=0, keepdims=True)
        erow = lax.broadcasted_iota(jnp.int32, el_all.shape, 0).astype(jnp.float32)
        egrp = jnp.floor(erow * (1.0 / EXPERTS_PER_GROUP))
        el = jnp.where(egrp == grp, el_all, NEG_BIG)
        e1 = jnp.max(el, axis=0, keepdims=True)
        i1 = jnp.min(jnp.where(el == e1, erow, float(N_EXPERTS)), axis=0, keepdims=True)
        el2 = jnp.where(erow == i1, NEG_BIG, el)
        e2 = jnp.max(el2, axis=0, keepdims=True)
        i2 = jnp.min(jnp.where(el2 == e2, erow, float(N_EXPERTS)), axis=0, keepdims=True)
        z2 = jnp.exp(e2 - e1)
        w1 = p_grp / (1.0 + z2)
        w2 = p_grp * z2 / (1.0 + z2)
        sel1 = erow == i1
        sel2 = erow == i2
        onehot = jnp.logical_or(sel1, sel2)
        t_r = lax.broadcasted_iota(jnp.int32, (tr, tr), 0)
        t_c = lax.broadcasted_iota(jnp.int32, (tr, tr), 1)
        before = (t_r < t_c).astype(jnp.bfloat16)
        rank = jnp.dot(onehot.astype(jnp.bfloat16), before, preferred_element_type=jnp.float32)
        cnt = jnp.sum(onehot.astype(jnp.float32), axis=1, keepdims=True)
        cnt_al = jnp.floor((cnt + (SEG_ALIGN - 1)) * (1.0 / SEG_ALIGN)) * SEG_ALIGN
        e_r = lax.broadcasted_iota(jnp.int32, (N_EXPERTS, N_EXPERTS), 0)
        e_c = lax.broadcasted_iota(jnp.int32, (N_EXPERTS, N_EXPERTS), 1)
        lstart = jnp.dot((e_c < e_r).astype(jnp.float32), jnp.broadcast_to(cnt_al, (N_EXPERTS, LANES)),
                         precision=hi, preferred_element_type=jnp.float32)[:, 0:1]
        slot = lstart + rank
        pos1 = jnp.sum(jnp.where(sel1, slot, 0.0), axis=0, keepdims=True)
        pos2 = jnp.sum(jnp.where(sel2, slot, 0.0), axis=0, keepdims=True)
        r8 = lax.broadcasted_iota(jnp.int32, (SUBLANES, tr), 0)
        info = jnp.where(r8 == ROW_POS0, pos1, jnp.where(r8 == ROW_POS1, pos2, jnp.where(
            r8 == ROW_W0, w1, jnp.where(r8 == ROW_W1, w2, jnp.where(
                r8 == ROW_E0, i1, jnp.where(r8 == ROW_E1, i2, 0.0))))))
        row_ref[:, rs] = info

        def split3(w):
            h = _bf16(w).astype(jnp.float32)
            m = _bf16(w - h).astype(jnp.float32)
            return h, m, _bf16(w - h - m).astype(jnp.float32)

        w1h, w1m, w1l = split3(w1)
        w2h, w2m, w2l = split3(w2)
        parts = jnp.where(r8 == 0, w1h, jnp.where(r8 == 1, w1m, jnp.where(r8 == 2, w1l, jnp.where(
            r8 == 3, w2h, jnp.where(r8 == 4, w2m, jnp.where(r8 == 5, w2l, jnp.where(r8 == 6, i1, i2)))))))
        col_ref[rs, :] = jnp.concatenate(
            [info, parts, jnp.zeros((LANES - 2 * SUBLANES, tr), jnp.float32)], axis=0).T
        cnt_ref[sb] = jnp.broadcast_to(cnt, (N_EXPERTS, LANES)).astype(jnp.int32)


def _out_proj_router(x2, att, hm, wo, layer, g, wrt, brt):
    t = x2.shape[0]
    tm = TM_OUT
    row = lambda i: (i, 0)
    fix = lambda i: (0, 0)
    return pl.pallas_call(
        _out_proj_router_kernel,
        grid=(t // tm,),
        in_specs=[
            pl.BlockSpec((tm, D_MODEL), row),
            pl.BlockSpec((tm, ATT_WIDTH), row),
            pl.BlockSpec((tm, ML_WIDTH), row),
            pl.BlockSpec((None, D_MODEL, D_MODEL), lambda i: (layer, 0, 0), pipeline_mode=pl.Buffered(1)),
            pl.BlockSpec((1, D_MODEL), fix),
            pl.BlockSpec((8 * SUBLANES, D_MODEL), fix),
            pl.BlockSpec((4 * SUBLANES, 1), fix),
        ],
        out_specs=[
            pl.BlockSpec((tm, D_MODEL), row),
            pl.BlockSpec((tm, D_MODEL), row),
            pl.BlockSpec((SUBLANES, tm), lambda i: (0, i)),
            pl.BlockSpec((tm, LANES), row),
            pl.BlockSpec((tm // TM_PROJ, N_EXPERTS, LANES), lambda i: (i, 0, 0)),
        ],
        out_shape=(
            jax.ShapeDtypeStruct((t, D_MODEL), jnp.float32),
            jax.ShapeDtypeStruct((t, D_MODEL), jnp.bfloat16),
            jax.ShapeDtypeStruct((SUBLANES, t), jnp.float32),
            jax.ShapeDtypeStruct((t, LANES), jnp.float32),
            jax.ShapeDtypeStruct((t // TM_PROJ, N_EXPERTS, LANES), jnp.int32),
        ),
        scratch_shapes=[pltpu.VMEM((D_MODEL, D_MODEL), jnp.bfloat16)],
        compiler_params=pltpu.CompilerParams(dimension_semantics=("arbitrary",),
                                             vmem_limit_bytes=VMEM_LIMIT),
        name="out_proj_router",
    )(x2, att, hm, wo, g, wrt, brt)


def _dispatch_kernel(dst_ref, nz_ref, zdst_ref, t_ref, col_ref, row_ref, xs_hbm,
                     xbuf, zx, sem_x, sem_z):
    i = pl.program_id(0)
    n = pl.num_programs(0)
    p = i % 2
    tm = t_ref.shape[0]

    def copy(tile, par, j):
        d = pl.ds(pl.multiple_of(dst_ref[tile, j], SEG_ALIGN), SEG_ALIGN)
        return pltpu.make_async_copy(xbuf.at[par, j * SEG_ALIGN:(j + 1) * SEG_ALIGN], xs_hbm.at[d],
                                     sem_x.at[par])

    def zero_copy(e, j):
        d = pl.ds(pl.multiple_of(zdst_ref[e] + j * SEG_ALIGN, SEG_ALIGN), SEG_ALIGN)
        return pltpu.make_async_copy(zx, xs_hbm.at[d], sem_z.at[0])

    def for_zero_chunks(fn):
        for e in range(N_EXPERTS):
            def body(j, c, e=e):
                fn(zero_copy(e, j))
                return c
            lax.fori_loop(0, nz_ref[e], body, 0)

    def wait_tile(par):
        pltpu.make_async_copy(xbuf.at[par], xs_hbm.at[pl.ds(0, L_CAP)], sem_x.at[par]).wait()

    @pl.when(i == 0)
    def _():
        zx[...] = jnp.zeros(zx.shape, zx.dtype)
        for_zero_chunks(lambda cp: cp.start())

    @pl.when(i >= 2)
    def _():
        wait_tile(p)

    pos0 = row_ref[ROW_POS0:ROW_POS0 + 1, :].astype(jnp.int32)
    pos1 = row_ref[ROW_POS1:ROW_POS1 + 1, :].astype(jnp.int32)
    t = t_ref[...]
    side = _bf16(col_ref[...])
    rows = DISPATCH_ROWS
    for c in range(L_CAP // rows):
        r = lax.broadcasted_iota(jnp.int32, (rows, tm), 0) + c * rows
        perm = jnp.logical_or(r == pos0, r == pos1).astype(jnp.bfloat16)
        xbuf[p, c * rows:(c + 1) * rows, 0:D_MODEL] = _bf16(
            jnp.dot(perm, t, preferred_element_type=jnp.float32))
        xbuf[p, c * rows:(c + 1) * rows, D_MODEL:D_XS] = _bf16(
            jnp.dot(perm, side, preferred_element_type=jnp.float32))
        for j in range(c * rows // SEG_ALIGN, (c + 1) * rows // SEG_ALIGN):
            copy(i, p, j).start()

    @pl.when(i == n - 1)
    def _():
        @pl.when(n >= 2)
        def _():
            wait_tile(1 - p)
        wait_tile(p)
        for_zero_chunks(lambda cp: cp.wait())


def _dispatch(t, col, row, dst, nz, zdst):
    n_tok = t.shape[0]
    r_cap, _ = _moe_capacity(n_tok)
    tm = TM_PROJ
    grid_spec = pltpu.PrefetchScalarGridSpec(
        num_scalar_prefetch=3,
        grid=(n_tok // tm,),
        in_specs=[
            pl.BlockSpec((tm, D_MODEL), lambda i, *_: (i, 0)),
            pl.BlockSpec((tm, LANES), lambda i, *_: (i, 0)),
            pl.BlockSpec((SUBLANES, tm), lambda i, *_: (0, i)),
        ],
        out_specs=pl.BlockSpec(memory_space=pl.ANY),
        scratch_shapes=[
            pltpu.VMEM((2, L_CAP, D_XS), jnp.bfloat16),
            pltpu.VMEM((SEG_ALIGN, D_XS), jnp.bfloat16),
            pltpu.SemaphoreType.DMA((2,)),
            pltpu.SemaphoreType.DMA((1,)),
        ],
    )
    return pl.pallas_call(
        _dispatch_kernel,
        grid_spec=grid_spec,
        out_shape=jax.ShapeDtypeStruct((r_cap + DUMP_ROWS, D_XS), jnp.bfloat16),
        compiler_params=pltpu.CompilerParams(dimension_semantics=("arbitrary",),
                                             vmem_limit_bytes=VMEM_LIMIT),
        name="moe_dispatch",
    )(dst, nz, zdst, t, col, row)


def _moe_kernel(layer, blk_e_ref, nused_ref, nvalid_ref, next_e_ref, xs_ref, wg_hbm, wu_hbm, wd_hbm,
                ys_ref, wg_st, wu_st, wd_st, wg_bf, wu_bf, wd_bf, sem):
    b = pl.program_id(0)
    nv = nvalid_ref[b]
    e = blk_e_ref[b]
    new_expert = jnp.logical_or(b == 0, e != blk_e_ref[jnp.maximum(b - 1, 0)])

    def fetch(expert):
        return (pltpu.make_async_copy(wg_hbm.at[layer, expert], wg_st, sem.at[0]),
                pltpu.make_async_copy(wu_hbm.at[layer, expert], wu_st, sem.at[1]),
                pltpu.make_async_copy(wd_hbm.at[layer, expert], wd_st, sem.at[2]))

    @pl.when(b == 0)
    def _():
        for cp in fetch(e):
            cp.start()

    @pl.when(new_expert)
    def _():
        for cp in fetch(e):
            cp.wait()
        wg_bf[...] = _bf16(wg_st[...])
        wu_bf[...] = _bf16(wu_st[...])
        wd_bf[...] = _bf16(wd_st[...])

        @pl.when(next_e_ref[b] != e)
        def _():
            for cp in fetch(next_e_ref[b]):
                cp.start()

    def compute(rows):
        x = xs_ref[0:rows, 0:D_MODEL]
        y = None
        fw = D_FF_EXPERT // MOE_FF_SPLIT
        for c in range(MOE_FF_SPLIT):
            cols = slice(c * fw, (c + 1) * fw)
            a = jnp.dot(x, wg_bf[:, cols], preferred_element_type=jnp.float32)
            u = jnp.dot(x, wu_bf[:, cols], preferred_element_type=jnp.float32)
            h = a * jax.nn.sigmoid(a) * u
            yc = jnp.dot(_bf16(h), wd_bf[cols, :], preferred_element_type=jnp.float32)
            y = yc if y is None else y + yc
        sd = xs_ref[0:rows, D_MODEL:D_XS].astype(jnp.float32)
        e_blk = blk_e_ref[b].astype(jnp.float32)
        w0 = sd[:, COL_W0H:COL_W0H + 1] + sd[:, COL_W0H + 1:COL_W0H + 2] + sd[:, COL_W0H + 2:COL_W0H + 3]
        w1 = sd[:, COL_W1H:COL_W1H + 1] + sd[:, COL_W1H + 1:COL_W1H + 2] + sd[:, COL_W1H + 2:COL_W1H + 3]
        w = jnp.where(sd[:, COL_E0:COL_E0 + 1] == e_blk, w0, w1)
        ys_ref[0:rows, :] = _bf16(y * w)

    for k in range(1, MOE_BM // MOE_TAIL + 1):
        rows = k * MOE_TAIL

        @pl.when(jnp.logical_and(nv > rows - MOE_TAIL, nv <= rows))
        def _(rows=rows):
            compute(rows)
            if rows < MOE_BM:
                ys_ref[rows:, :] = jnp.zeros((MOE_BM - rows, D_MODEL), ys_ref.dtype)


def _moe(xs, blk_e, nused, nvalid, next_e, wg, wu, wd, layer):
    blk = lambda b, be, nu, nv, ne: (jnp.maximum(jnp.minimum(b, nu[0] - 1), 0), 0)
    r_cap = xs.shape[0] - DUMP_ROWS
    grid_spec = pltpu.PrefetchScalarGridSpec(
        num_scalar_prefetch=4,
        grid=(r_cap // MOE_BM,),
        in_specs=[
            pl.BlockSpec((MOE_BM, D_XS), blk),
            pl.BlockSpec(memory_space=pl.ANY),
            pl.BlockSpec(memory_space=pl.ANY),
            pl.BlockSpec(memory_space=pl.ANY),
        ],
        out_specs=pl.BlockSpec((MOE_BM, D_MODEL), blk),
        scratch_shapes=[
            pltpu.VMEM((D_MODEL, D_FF_EXPERT), jnp.float32),
            pltpu.VMEM((D_MODEL, D_FF_EXPERT), jnp.float32),
            pltpu.VMEM((D_FF_EXPERT, D_MODEL), jnp.float32),
            pltpu.VMEM((D_MODEL, D_FF_EXPERT), jnp.bfloat16),
            pltpu.VMEM((D_MODEL, D_FF_EXPERT), jnp.bfloat16),
            pltpu.VMEM((D_FF_EXPERT, D_MODEL), jnp.bfloat16),
            pltpu.SemaphoreType.DMA((3,)),
        ],
    )
    return pl.pallas_call(
        functools.partial(_moe_kernel, layer),
        grid_spec=grid_spec,
        out_shape=jax.ShapeDtypeStruct((r_cap, D_MODEL), jnp.bfloat16),
        compiler_params=pltpu.CompilerParams(dimension_semantics=("arbitrary",),
                                             vmem_limit_bytes=VMEM_LIMIT),
        name="moe_experts",
    )(blk_e, nused, nvalid, next_e, xs, wg, wu, wd)


def _combine_kernel(src_ref, x1_ref, col_ref, ys_hbm, out_ref, ybuf, sem):
    i = pl.program_id(0)
    n = pl.num_programs(0)
    p = i % COMBINE_BUFS
    tm = x1_ref.shape[0]

    def start_tile(tile, par):
        for j in range(N_CHUNK):
            s = pl.ds(pl.multiple_of(src_ref[tile, j], SEG_ALIGN), SEG_ALIGN)
            pltpu.make_async_copy(ys_hbm.at[s], ybuf.at[par, j * SEG_ALIGN:(j + 1) * SEG_ALIGN],
                                  sem.at[par]).start()

    def wait_buf(par):
        pltpu.make_async_copy(ys_hbm.at[pl.ds(0, L_CAP)], ybuf.at[par], sem.at[par]).wait()

    @pl.when(i == 0)
    def _():
        start_tile(0, 0)
        start_tile(jnp.minimum(1, n - 1), 1)

    wait_buf(p)
    start_tile(jnp.minimum(i + 2, n - 1), (i + 2) % COMBINE_BUFS)

    rows = COMBINE_ROWS
    for c in range(tm // rows):
        col = col_ref[c * rows:(c + 1) * rows, :]
        pos0 = col[:, ROW_POS0:ROW_POS0 + 1].astype(jnp.int32)
        pos1 = col[:, ROW_POS1:ROW_POS1 + 1].astype(jnp.int32)
        l = lax.broadcasted_iota(jnp.int32, (rows, L_CAP), 1)
        perm = jnp.logical_or(l == pos0, l == pos1).astype(jnp.bfloat16)
        y = jnp.dot(perm, ybuf[p], preferred_element_type=jnp.float32)
        out_ref[c * rows:(c + 1) * rows, :] = x1_ref[c * rows:(c + 1) * rows, :] + y

    @pl.when(i == n - 1)
    def _():
        wait_buf((i + 1) % COMBINE_BUFS)
        wait_buf((i + 2) % COMBINE_BUFS)


def _combine(x1, col, ys, src):
    n_tok = x1.shape[0]
    tm = TM_PROJ
    grid_spec = pltpu.PrefetchScalarGridSpec(
        num_scalar_prefetch=1,
        grid=(n_tok // tm,),
        in_specs=[
            pl.BlockSpec((tm, D_MODEL), lambda i, *_: (i, 0)),
            pl.BlockSpec((tm, LANES), lambda i, *_: (i, 0)),
            pl.BlockSpec(memory_space=pl.ANY),
        ],
        out_specs=pl.BlockSpec((tm, D_MODEL), lambda i, *_: (i, 0)),
        scratch_shapes=[
            pltpu.VMEM((COMBINE_BUFS, L_CAP, D_MODEL), jnp.bfloat16),
            pltpu.SemaphoreType.DMA((COMBINE_BUFS,)),
        ],
    )
    return pl.pallas_call(
        _combine_kernel,
        grid_spec=grid_spec,
        out_shape=jax.ShapeDtypeStruct((n_tok, D_MODEL), jnp.float32),
        compiler_params=pltpu.CompilerParams(dimension_semantics=("arbitrary",),
                                             vmem_limit_bytes=VMEM_LIMIT),
        name="moe_combine",
    )(src, x1, col, ys)


def _routing_tables(counts, n_blk_cap):
    r_cap = n_blk_cap * MOE_BM
    i32 = jnp.int32
    ca = ((counts + (SEG_ALIGN - 1)) // SEG_ALIGN) * SEG_ALIGN
    lend = jnp.cumsum(ca, axis=1)
    lstart = lend - ca
    tot = jnp.sum(ca, axis=0)
    region = ((tot + (MOE_BM - 1)) // MOE_BM) * MOE_BM
    rend = jnp.cumsum(region)
    base = rend - region
    gstart = base[None, :] + jnp.cumsum(ca, axis=0) - ca
    j16 = jnp.arange(N_CHUNK, dtype=i32) * SEG_ALIGN
    in_seg = jnp.logical_and(lstart[:, None, :] <= j16[None, :, None], j16[None, :, None] < lend[:, None, :])
    seg_row = (jnp.sum(jnp.where(in_seg, (gstart - lstart)[:, None, :], 0), axis=2) + j16[None, :]).astype(i32)
    used = j16[None, :] < lend[:, -1:]
    tile_par = (jnp.arange(counts.shape[0], dtype=i32) % 2)[:, None]
    park = r_cap + (tile_par * N_CHUNK + jnp.arange(N_CHUNK, dtype=i32)[None, :]) * SEG_ALIGN
    dst = jnp.where(used, seg_row, park).astype(i32)
    src = jnp.where(used, seg_row, 0).astype(i32)
    nused = (rend[-1] // MOE_BM).astype(i32)
    brow = jnp.minimum(jnp.arange(n_blk_cap, dtype=i32), nused - 1) * MOE_BM
    blk_e = jnp.minimum(jnp.sum(rend[None, :] <= brow[:, None], axis=1), N_EXPERTS - 1).astype(i32)
    zstart = (base + tot).astype(i32)
    nzc = ((((tot + (MOE_TAIL - 1)) // MOE_TAIL) * MOE_TAIL - tot) // SEG_ALIGN).astype(i32)
    blk_i = jnp.arange(n_blk_cap, dtype=i32)
    seg_end = jnp.sum(jnp.where(blk_e[:, None] == jnp.arange(N_EXPERTS)[None, :], (base + tot)[None, :], 0), axis=1)
    nvalid = jnp.where(blk_i < nused, jnp.clip(seg_end - blk_i * MOE_BM, 0, MOE_BM), 0).astype(i32)
    eid = jnp.arange(N_EXPERTS, dtype=i32)
    later = jnp.logical_and(eid[None, :] > eid[:, None], (region > 0)[None, :])
    nxt = jnp.min(jnp.where(later, eid[None, :], N_EXPERTS), axis=1)
    nxt = jnp.where(nxt < N_EXPERTS, nxt, eid)
    next_e = jnp.sum(jnp.where(blk_e[:, None] == eid[None, :], nxt[None, :], 0), axis=1).astype(i32)
    return dst, src, nzc, zstart, blk_e, nused[None], nvalid, next_e


def _t5_bucket_np(dist):
    max_exact = N_BUCKETS // 2
    d = np.maximum(dist, 1).astype(np.float32)
    large = max_exact + (np.log(d / max_exact) / math.log(MAX_DISTANCE / max_exact)
                         * (N_BUCKETS - max_exact)).astype(np.int32)
    large = np.minimum(large, N_BUCKETS - 1)
    return np.where(dist < max_exact, dist, large)


def _attention_bias(rel_bias):
    qi = np.arange(ATT_BLOCK)[:, None]
    kj = np.arange(2 * ATT_BLOCK)[None, :]
    dist = qi + ATT_BLOCK - kj
    in_window = (dist >= 0) & (dist < WINDOW)
    bucket = _t5_bucket_np(np.clip(dist, 0, WINDOW - 1))
    onehot = (bucket[None] == np.arange(N_BUCKETS)[:, None, None]).astype(np.float32)
    bias = jnp.einsum('nh,nqk->hqk', rel_bias.astype(jnp.float32), jnp.asarray(onehot),
                      precision=lax.Precision.HIGHEST)
    bias = jnp.where(jnp.asarray(in_window)[None], bias * LOG2E, NEG_BIG)
    no_prev = jnp.asarray(kj < ATT_BLOCK)[None]
    return jnp.stack([bias, jnp.where(no_prev, NEG_BIG, bias)])


def _block_diag_mean(width, block):
    idx = np.arange(width) // block
    return jnp.asarray((idx[:, None] == idx[None, :]).astype(np.float32) / block, dtype=jnp.bfloat16)


def kernel(x, rel_bias, norm_mix_g, w_in, q_norm_g, k_norm_g, attn_sink, conv_w, conv_b, gate_b,
           mlstm_norm_g, w_out, norm_ffn_g, w_router_group, b_router_group, w_router_expert,
           b_router_expert, w_gate, w_up, w_down):
    batch, seq_len, _ = x.shape
    n_tok = batch * seq_len
    assert seq_len % TM_OUT == 0 and seq_len % ML_KCHUNK == 0
    f32 = jnp.float32
    bias = _attention_bias(rel_bias)
    bdq = _block_diag_mean(ATT_WIDTH, ATT_HEAD_DIM)
    bdk = _block_diag_mean(LANES, ATT_HEAD_DIM)
    x2 = x.reshape(n_tok, D_MODEL)
    w_in_t = jnp.swapaxes(w_in, 1, 2)
    for l in range(DEPTH):
        qg = (jnp.tile(q_norm_g[l].astype(f32), ATT_HEADS) * (ATT_HEAD_DIM ** -0.5 * LOG2E))[None, :]
        kg = jnp.tile(k_norm_g[l].astype(f32), ATT_KV_HEADS)[None, :]
        gbt = gate_b[l].astype(f32)[:, None]
        qn, kn, vd, hm = _front(
            x2.reshape(batch, seq_len, D_MODEL), norm_mix_g[l][None, :], w_in_t, l, qg, kg, bdq, bdk,
            conv_w[l], conv_b[l][None, :], gbt, mlstm_norm_g[l][None, :])
        flat = lambda a: a.reshape(n_tok, a.shape[-1])
        att = _attention(flat(qn), flat(kn), flat(vd), bias, attn_sink[l].astype(f32) * LOG2E,
                         batch, seq_len)
        n_rt = 4 * SUBLANES
        wrt = jnp.pad(jnp.concatenate([w_router_expert[l], w_router_group[l]], axis=1).astype(f32).T,
                      ((0, n_rt - N_EXPERTS - N_GROUPS), (0, 0)))
        brt = jnp.pad(jnp.concatenate([b_router_expert[l], b_router_group[l]]).astype(f32),
                      (0, n_rt - N_EXPERTS - N_GROUPS))[:, None]
        wrt_hi = _bf16(wrt)
        wrt = jnp.concatenate([wrt_hi, _bf16(wrt - wrt_hi.astype(f32))], axis=0)
        x1, t, row, col, cnt = _out_proj_router(x2, att, hm.reshape(n_tok, ML_WIDTH), w_out, l,
                                                norm_ffn_g[l][None, :], wrt, brt)
        _, n_blk_cap = _moe_capacity(n_tok)
        dst, src, nz, zdst, blk_e, nused, nvalid, next_e = _routing_tables(cnt[:, :, 0], n_blk_cap)
        xs = _dispatch(t, col, row, dst, nz, zdst)
        ys = _moe(xs, blk_e, nused, nvalid, next_e, w_gate, w_up, w_down, l)
        x2 = _combine(x1, col, ys, src)
    return x2.reshape(batch, seq_len, D_MODEL)
```
